```python
import math
import jax, jax.numpy as jnp
from jax import lax
import numpy as np

D_MODEL = 1024
BATCH = 8
SEQ = 2048
DEPTH = 4

D_MIX = D_MODEL
N_MIXERS = 4
W_GRP = D_MIX // N_MIXERS
SGU_HEADS = 4
SGU_HEAD_DIM = W_GRP // SGU_HEADS
CHUNK = 128
POOL_WINDOWS = (2, 4, 8, 16)
POOL_GROUPS = len(POOL_WINDOWS)
POOL_GROUP_DIM = W_GRP // POOL_GROUPS
CONV_WIDTH = 3
S5_GROUP_CH = 16
S5_GROUPS = W_GRP // S5_GROUP_CH
S5_STATE = 64
P_IN = 2 * W_GRP + W_GRP + 3 * W_GRP + W_GRP
D_FF = ((8 * D_MODEL // 3 + 255) // 256) * 256
N_ADA = 9
EPS = 1e-6

kernel_name = "hybrid_parallel_mixer_trunk"


def rmsnorm(x, g):
    xf = x.astype(jnp.float32)
    y = xf * lax.rsqrt(jnp.mean(xf * xf, axis=-1, keepdims=True) + EPS)
    return (y * g.astype(jnp.float32)).astype(x.dtype)


def group_rmsnorm(y, g):
    b, s, _ = y.shape
    yf = y.astype(jnp.float32).reshape(b, s, N_MIXERS, W_GRP)
    yf = yf * lax.rsqrt(jnp.mean(yf * yf, axis=-1, keepdims=True) + EPS)
    return (yf.reshape(b, s, D_MIX) * g.astype(jnp.float32)).astype(y.dtype)


def modulate(h, shift, scale):
    return h * (1.0 + scale) + shift


def swiglu(h, w_in, w_out):
    a, b = jnp.split(h @ w_in, 2, axis=-1)
    return (jax.nn.silu(a) * b) @ w_out


def sgu_mixer(z, w_s, b_s):
    bsz, s, _ = z.shape
    z = jax.nn.gelu(z)
    u, v = jnp.split(z, 2, axis=-1)
    v = v.reshape(bsz, s // CHUNK, CHUNK, SGU_HEADS, SGU_HEAD_DIM)
    vf = v.astype(jnp.float32)
    mu = jnp.mean(vf, axis=-1, keepdims=True)
    var = jnp.mean(jnp.square(vf - mu), axis=-1, keepdims=True)
    v = ((vf - mu) * lax.rsqrt(var + EPS)).astype(z.dtype)
    mask = jnp.tril(jnp.ones((CHUNK, CHUNK), dtype=w_s.dtype))
    mixed = jnp.einsum('hts,bnshd->bnthd', w_s * mask, v)
    mixed = mixed + b_s.T[None, None, :, :, None]
    return u * mixed.reshape(bsz, s, W_GRP)


def pool_mixer(z, w_p, scale):
    bsz, s, _ = z.shape
    zf = z.astype(jnp.float32).reshape(bsz, s, POOL_GROUPS, POOL_GROUP_DIM)
    cs = jnp.concatenate([jnp.zeros_like(zf[:, :1]), jnp.cumsum(zf, axis=1)], axis=1)
    t = jnp.arange(s)
    win = jnp.array(POOL_WINDOWS, dtype=jnp.int32)
    lo = jnp.maximum(t[:, None] + 1 - win[None, :], 0)
    cnt = (t[:, None] + 1 - lo).astype(jnp.float32)
    lower = cs[:, lo, jnp.arange(POOL_GROUPS)[None, :]]
    mean = (cs[:, 1:] - lower) / cnt[None, :, :, None]
    p = (mean - zf).astype(z.dtype)
    out = jnp.einsum('bsgc,gcd->bsgd', p, w_p).reshape(bsz, s, W_GRP)
    return out * scale


def conv_mixer(z, conv_w):
    bg, cg, xh = jnp.split(z, 3, axis=-1)
    y = cg * xh
    y = lax.conv_general_dilated(
        y, conv_w[:, None, :], window_strides=(1,), padding=[(CONV_WIDTH - 1, 0)],
        dimension_numbers=('NWC', 'WIO', 'NWC'), feature_group_count=W_GRP)
    return bg * y


def s5_mixer(u, lam_re, lam_im, b_re, b_im, c_re, c_im, d, log_dt, glu_w, glu_b):
    bsz, s, _ = u.shape
    f32 = jnp.float32
    dt = jnp.exp(log_dt.astype(f32))[:, None]
    lre, lim = lam_re.astype(f32), lam_im.astype(f32)
    mag = jnp.exp(lre * dt)
    ang = lim * dt
    a_re, a_im = mag * jnp.cos(ang), mag * jnp.sin(ang)
    nr, ni = a_re - 1.0, a_im
    den = lre * lre + lim * lim
    k_re = (nr * lre + ni * lim) / den
    k_im = (ni * lre - nr * lim) / den
    br, bi = b_re.astype(f32), b_im.astype(f32)
    bb_re = k_re[..., None] * br - k_im[..., None] * bi
    bb_im = k_re[..., None] * bi + k_im[..., None] * br
    uf = u.astype(f32)
    ug = uf.reshape(bsz, s, S5_GROUPS, S5_GROUP_CH)
    bu_re = jnp.einsum('bsgc,gpc->bsgp', ug, bb_re)
    bu_im = jnp.einsum('bsgc,gpc->bsgp', ug, bb_im)
    ar = jnp.broadcast_to(a_re, bu_re.shape)
    ai = jnp.broadcast_to(a_im, bu_re.shape)

    def combine(e1, e2):
        a1r, a1i, b1r, b1i = e1
        a2r, a2i, b2r, b2i = e2
        return (a2r * a1r - a2i * a1i,
                a2r * a1i + a2i * a1r,
                a2r * b1r - a2i * b1i + b2r,
                a2r * b1i + a2i * b1r + b2i)

    _, _, xr, xi = lax.associative_scan(combine, (ar, ai, bu_re, bu_im), axis=1)
    y = (jnp.einsum('gcp,bsgp->bsgc', c_re.astype(f32), xr)
         - jnp.einsum('gcp,bsgp->bsgc', c_im.astype(f32), xi))
    y = y.reshape(bsz, s, W_GRP) + d.astype(f32) * uf
    y = jax.nn.gelu(y).astype(u.dtype)
    return y * jax.nn.sigmoid(y @ glu_w + glu_b)


def _fwd_setup_inputs(seed: int = 0) -> dict:
    key = jax.random.key(seed)
    ks = jax.random.split(key, 32)
    f32 = jnp.float32
    L, D = DEPTH, D_MODEL

    def nrm(k, shape, scale):
        return jax.random.normal(k, shape, f32) * scale

    def gain(k, shape):
        return 1.0 + 0.05 * jax.random.normal(k, shape, f32)

    lam_im0 = jnp.broadcast_to(math.pi * jnp.arange(S5_STATE, dtype=f32), (L, S5_GROUPS, S5_STATE))
    return {
        "x": nrm(ks[0], (BATCH, SEQ, D), 1.0),
        "c": nrm(ks[1], (BATCH, D), 1.0),
        "ada_w": nrm(ks[2], (L, D, N_ADA * D), 0.5 * D ** -0.5),
        "ada_b": nrm(ks[3], (L, N_ADA * D), 0.01),
        "norm1_g": gain(ks[4], (L, D)),
        "ffn1_w_in": nrm(ks[5], (L, D, 2 * D_FF), D ** -0.5),
        "ffn1_w_out": nrm(ks[6], (L, D_FF, D), D_FF ** -0.5),
        "norm2_g": gain(ks[7], (L, D)),
        "w_mix_in": nrm(ks[8], (L, D, P_IN), D ** -0.5),
        "sgu_w": nrm(ks[9], (L, SGU_HEADS, CHUNK, CHUNK), CHUNK ** -0.5),
        "sgu_b": 1.0 + nrm(ks[10], (L, SGU_HEADS, CHUNK), 0.1),
        "pool_w": nrm(ks[11], (L, POOL_GROUPS, POOL_GROUP_DIM, POOL_GROUP_DIM), POOL_GROUP_DIM ** -0.5),
        "pool_scale": 1.0 + nrm(ks[12], (L, W_GRP), 0.1),
        "conv_w": nrm(ks[13], (L, CONV_WIDTH, W_GRP), CONV_WIDTH ** -0.5),
        "s5_lambda_re": -0.5 + nrm(ks[14], (L, S5_GROUPS, S5_STATE), 0.01),
        "s5_lambda_im": lam_im0 + nrm(ks[15], (L, S5_GROUPS, S5_STATE), 0.01),
        "s5_b_re": nrm(ks[16], (L, S5_GROUPS, S5_STATE, S5_GROUP_CH), (2 * S5_GROUP_CH) ** -0.5),
        "s5_b_im": nrm(ks[17], (L, S5_GROUPS, S5_STATE, S5_GROUP_CH), (2 * S5_GROUP_CH) ** -0.5),
        "s5_c_re": nrm(ks[18], (L, S5_GROUPS, S5_GROUP_CH, S5_STATE), (2 * S5_STATE) ** -0.5),
        "s5_c_im": nrm(ks[19], (L, S5_GROUPS, S5_GROUP_CH, S5_STATE), (2 * S5_STATE) ** -0.5),
        "s5_d": nrm(ks[20], (L, W_GRP), 1.0),
        "s5_log_dt": jax.random.uniform(ks[21], (L, S5_GROUPS), f32, math.log(1e-3), math.log(1e-1)),
        "s5_glu_w": nrm(ks[22], (L, W_GRP, W_GRP), W_GRP ** -0.5),
        "s5_glu_b": nrm(ks[23], (L, W_GRP), 0.01),
        "mix_norm_g": gain(ks[24], (L, D_MIX)),
        "w_mix_out": nrm(ks[25], (L, D_MIX, D), D_MIX ** -0.5),
        "norm3_g": gain(ks[26], (L, D)),
        "ffn2_w_in": nrm(ks[27], (L, D, 2 * D_FF), D ** -0.5),
        "ffn2_w_out": nrm(ks[28], (L, D_FF, D), D_FF ** -0.5),
        "final_norm_g": gain(ks[29], (D,)),
    }


def _fwd_reference(x, c, ada_w, ada_b, norm1_g, ffn1_w_in, ffn1_w_out, norm2_g, w_mix_in,
              sgu_w, sgu_b, pool_w, pool_scale, conv_w, s5_lambda_re, s5_lambda_im,
              s5_b_re, s5_b_im, s5_c_re, s5_c_im, s5_d, s5_log_dt, s5_glu_w, s5_glu_b,
              mix_norm_g, w_mix_out, norm3_g, ffn2_w_in, ffn2_w_out, final_norm_g):
    c_act = jax.nn.silu(c)
    for l in range(DEPTH):
        cond = (c_act @ ada_w[l] + ada_b[l])[:, None, :]
        sh1, sc1, g1, sh2, sc2, g2, sh3, sc3, g3 = jnp.split(cond, N_ADA, axis=-1)

        h = modulate(rmsnorm(x, norm1_g[l]), sh1, sc1)
        x = x + 0.5 * g1 * swiglu(h, ffn1_w_in[l], ffn1_w_out[l])

        h = modulate(rmsnorm(x, norm2_g[l]), sh2, sc2)
        z = h @ w_mix_in[l]
        za, zb, zc, zd = jnp.split(z, [2 * W_GRP, 3 * W_GRP, 6 * W_GRP], axis=-1)
        ya = sgu_mixer(za, sgu_w[l], sgu_b[l])
        yb = pool_mixer(zb, pool_w[l], pool_scale[l])
        yc = conv_mixer(zc, conv_w[l])
        yd = s5_mixer(zd, s5_lambda_re[l], s5_lambda_im[l], s5_b_re[l], s5_b_im[l],
                      s5_c_re[l], s5_c_im[l], s5_d[l], s5_log_dt[l], s5_glu_w[l], s5_glu_b[l])
        y = group_rmsnorm(jnp.concatenate([ya, yb, yc, yd], axis=-1), mix_norm_g[l])
        x = x + g2 * (y @ w_mix_out[l])

        h = modulate(rmsnorm(x, norm3_g[l]), sh3, sc3)
        x = x + 0.5 * g3 * swiglu(h, ffn2_w_in[l], ffn2_w_out[l])
    return rmsnorm(x, final_norm_g)


import jax as _jax
import jax.numpy as _jnp

TWIN_FORMAT = 'train_step'
FWD_PARAMS = ['x', 'c', 'ada_w', 'ada_b', 'norm1_g', 'ffn1_w_in', 'ffn1_w_out', 'norm2_g', 'w_mix_in', 'sgu_w', 'sgu_b', 'pool_w', 'pool_scale', 'conv_w', 's5_lambda_re', 's5_lambda_im', 's5_b_re', 's5_b_im', 's5_c_re', 's5_c_im', 's5_d', 's5_log_dt', 's5_glu_w', 's5_glu_b', 'mix_norm_g', 'w_mix_out', 'norm3_g', 'ffn2_w_in', 'ffn2_w_out', 'final_norm_g']
TWIN_WEIGHTS = ['ada_w', 'ada_b', 'norm1_g', 'ffn1_w_in', 'ffn1_w_out', 'norm2_g', 'w_mix_in', 'sgu_w', 'sgu_b', 'pool_w', 'pool_scale', 'conv_w', 's5_lambda_re', 's5_lambda_im', 's5_b_re', 's5_b_im', 's5_c_re', 's5_c_im', 's5_d', 's5_log_dt', 's5_glu_w', 's5_glu_b', 'mix_norm_g', 'w_mix_out', 'norm3_g', 'ffn2_w_in', 'ffn2_w_out', 'final_norm_g']
TWIN_DIFF_INPUT = 'x'
TWIN_INPUTS = ['x', 'c', 'ada_w', 'ada_b', 'norm1_g', 'ffn1_w_in', 'ffn1_w_out', 'norm2_g', 'w_mix_in', 'sgu_w', 'sgu_b', 'pool_w', 'pool_scale', 'conv_w', 's5_lambda_re', 's5_lambda_im', 's5_b_re', 's5_b_im', 's5_c_re', 's5_c_im', 's5_d', 's5_log_dt', 's5_glu_w', 's5_glu_b', 'mix_norm_g', 'w_mix_out', 'norm3_g', 'ffn2_w_in', 'ffn2_w_out', 'final_norm_g', 'loss_target', 'm_ada_w', 'm_ada_b', 'm_norm1_g', 'm_ffn1_w_in', 'm_ffn1_w_out', 'm_norm2_g', 'm_w_mix_in', 'm_sgu_w', 'm_sgu_b', 'm_pool_w', 'm_pool_scale', 'm_conv_w', 'm_s5_lambda_re', 'm_s5_lambda_im', 'm_s5_b_re', 'm_s5_b_im', 'm_s5_c_re', 'm_s5_c_im', 'm_s5_d', 'm_s5_log_dt', 'm_s5_glu_w', 'm_s5_glu_b', 'm_mix_norm_g', 'm_w_mix_out', 'm_norm3_g', 'm_ffn2_w_in', 'm_ffn2_w_out', 'm_final_norm_g', 'v_ada_w', 'v_ada_b', 'v_norm1_g', 'v_ffn1_w_in', 'v_ffn1_w_out', 'v_norm2_g', 'v_w_mix_in', 'v_sgu_w', 'v_sgu_b', 'v_pool_w', 'v_pool_scale', 'v_conv_w', 'v_s5_lambda_re', 'v_s5_lambda_im', 'v_s5_b_re', 'v_s5_b_im', 'v_s5_c_re', 'v_s5_c_im', 'v_s5_d', 'v_s5_log_dt', 'v_s5_glu_w', 'v_s5_glu_b', 'v_mix_norm_g', 'v_w_mix_out', 'v_norm3_g', 'v_ffn2_w_in', 'v_ffn2_w_out', 'v_final_norm_g']
TWIN_OUTPUTS = ['loss', 'grad_x', 'grad_ada_w', 'grad_ada_b', 'grad_norm1_g', 'grad_ffn1_w_in', 'grad_ffn1_w_out', 'grad_norm2_g', 'grad_w_mix_in', 'grad_sgu_w', 'grad_sgu_b', 'grad_pool_w', 'grad_pool_scale', 'grad_conv_w', 'grad_s5_lambda_re', 'grad_s5_lambda_im', 'grad_s5_b_re', 'grad_s5_b_im', 'grad_s5_c_re', 'grad_s5_c_im', 'grad_s5_d', 'grad_s5_log_dt', 'grad_s5_glu_w', 'grad_s5_glu_b', 'grad_mix_norm_g', 'grad_w_mix_out', 'grad_norm3_g', 'grad_ffn2_w_in', 'grad_ffn2_w_out', 'grad_final_norm_g', 'delta_ada_w', 'delta_ada_b', 'delta_norm1_g', 'delta_ffn1_w_in', 'delta_ffn1_w_out', 'delta_norm2_g', 'delta_w_mix_in', 'delta_sgu_w', 'delta_sgu_b', 'delta_pool_w', 'delta_pool_scale', 'delta_conv_w', 'delta_s5_lambda_re', 'delta_s5_lambda_im', 'delta_s5_b_re', 'delta_s5_b_im', 'delta_s5_c_re', 'delta_s5_c_im', 'delta_s5_d', 'delta_s5_log_dt', 'delta_s5_glu_w', 'delta_s5_glu_b', 'delta_mix_norm_g', 'delta_w_mix_out', 'delta_norm3_g', 'delta_ffn2_w_in', 'delta_ffn2_w_out', 'delta_final_norm_g', 'new_m_ada_w', 'new_m_ada_b', 'new_m_norm1_g', 'new_m_ffn1_w_in', 'new_m_ffn1_w_out', 'new_m_norm2_g', 'new_m_w_mix_in', 'new_m_sgu_w', 'new_m_sgu_b', 'new_m_pool_w', 'new_m_pool_scale', 'new_m_conv_w', 'new_m_s5_lambda_re', 'new_m_s5_lambda_im', 'new_m_s5_b_re', 'new_m_s5_b_im', 'new_m_s5_c_re', 'new_m_s5_c_im', 'new_m_s5_d', 'new_m_s5_log_dt', 'new_m_s5_glu_w', 'new_m_s5_glu_b', 'new_m_mix_norm_g', 'new_m_w_mix_out', 'new_m_norm3_g', 'new_m_ffn2_w_in', 'new_m_ffn2_w_out', 'new_m_final_norm_g', 'new_v_ada_w', 'new_v_ada_b', 'new_v_norm1_g', 'new_v_ffn1_w_in', 'new_v_ffn1_w_out', 'new_v_norm2_g', 'new_v_w_mix_in', 'new_v_sgu_w', 'new_v_sgu_b', 'new_v_pool_w', 'new_v_pool_scale', 'new_v_conv_w', 'new_v_s5_lambda_re', 'new_v_s5_lambda_im', 'new_v_s5_b_re', 'new_v_s5_b_im', 'new_v_s5_c_re', 'new_v_s5_c_im', 'new_v_s5_d', 'new_v_s5_log_dt', 'new_v_s5_glu_w', 'new_v_s5_glu_b', 'new_v_mix_norm_g', 'new_v_w_mix_out', 'new_v_norm3_g', 'new_v_ffn2_w_in', 'new_v_ffn2_w_out', 'new_v_final_norm_g']
TWIN_LEAF_KINDS = {'loss': 'loss', 'grad_x': 'grad_x', 'grad_ada_w': 'grad_w', 'grad_ada_b': 'grad_w', 'grad_norm1_g': 'grad_w', 'grad_ffn1_w_in': 'grad_w', 'grad_ffn1_w_out': 'grad_w', 'grad_norm2_g': 'grad_w', 'grad_w_mix_in': 'grad_w', 'grad_sgu_w': 'grad_w', 'grad_sgu_b': 'grad_w', 'grad_pool_w': 'grad_w', 'grad_pool_scale': 'grad_w', 'grad_conv_w': 'grad_w', 'grad_s5_lambda_re': 'grad_w', 'grad_s5_lambda_im': 'grad_w', 'grad_s5_b_re': 'grad_w', 'grad_s5_b_im': 'grad_w', 'grad_s5_c_re': 'grad_w', 'grad_s5_c_im': 'grad_w', 'grad_s5_d': 'grad_w', 'grad_s5_log_dt': 'grad_w', 'grad_s5_glu_w': 'grad_w', 'grad_s5_glu_b': 'grad_w', 'grad_mix_norm_g': 'grad_w', 'grad_w_mix_out': 'grad_w', 'grad_norm3_g': 'grad_w', 'grad_ffn2_w_in': 'grad_w', 'grad_ffn2_w_out': 'grad_w', 'grad_final_norm_g': 'grad_w', 'delta_ada_w': 'delta_w', 'delta_ada_b': 'delta_w', 'delta_norm1_g': 'delta_w', 'delta_ffn1_w_in': 'delta_w', 'delta_ffn1_w_out': 'delta_w', 'delta_norm2_g': 'delta_w', 'delta_w_mix_in': 'delta_w', 'delta_sgu_w': 'delta_w', 'delta_sgu_b': 'delta_w', 'delta_pool_w': 'delta_w', 'delta_pool_scale': 'delta_w', 'delta_conv_w': 'delta_w', 'delta_s5_lambda_re': 'delta_w', 'delta_s5_lambda_im': 'delta_w', 'delta_s5_b_re': 'delta_w', 'delta_s5_b_im': 'delta_w', 'delta_s5_c_re': 'delta_w', 'delta_s5_c_im': 'delta_w', 'delta_s5_d': 'delta_w', 'delta_s5_log_dt': 'delta_w', 'delta_s5_glu_w': 'delta_w', 'delta_s5_glu_b': 'delta_w', 'delta_mix_norm_g': 'delta_w', 'delta_w_mix_out': 'delta_w', 'delta_norm3_g': 'delta_w', 'delta_ffn2_w_in': 'delta_w', 'delta_ffn2_w_out': 'delta_w', 'delta_final_norm_g': 'delta_w', 'new_m_ada_w': 'new_m', 'new_m_ada_b': 'new_m', 'new_m_norm1_g': 'new_m', 'new_m_ffn1_w_in': 'new_m', 'new_m_ffn1_w_out': 'new_m', 'new_m_norm2_g': 'new_m', 'new_m_w_mix_in': 'new_m', 'new_m_sgu_w': 'new_m', 'new_m_sgu_b': 'new_m', 'new_m_pool_w': 'new_m', 'new_m_pool_scale': 'new_m', 'new_m_conv_w': 'new_m', 'new_m_s5_lambda_re': 'new_m', 'new_m_s5_lambda_im': 'new_m', 'new_m_s5_b_re': 'new_m', 'new_m_s5_b_im': 'new_m', 'new_m_s5_c_re': 'new_m', 'new_m_s5_c_im': 'new_m', 'new_m_s5_d': 'new_m', 'new_m_s5_log_dt': 'new_m', 'new_m_s5_glu_w': 'new_m', 'new_m_s5_glu_b': 'new_m', 'new_m_mix_norm_g': 'new_m', 'new_m_w_mix_out': 'new_m', 'new_m_norm3_g': 'new_m', 'new_m_ffn2_w_in': 'new_m', 'new_m_ffn2_w_out': 'new_m', 'new_m_final_norm_g': 'new_m', 'new_v_ada_w': 'new_v', 'new_v_ada_b': 'new_v', 'new_v_norm1_g': 'new_v', 'new_v_ffn1_w_in': 'new_v', 'new_v_ffn1_w_out': 'new_v', 'new_v_norm2_g': 'new_v', 'new_v_w_mix_in': 'new_v', 'new_v_sgu_w': 'new_v', 'new_v_sgu_b': 'new_v', 'new_v_pool_w': 'new_v', 'new_v_pool_scale': 'new_v', 'new_v_conv_w': 'new_v', 'new_v_s5_lambda_re': 'new_v', 'new_v_s5_lambda_im': 'new_v', 'new_v_s5_b_re': 'new_v', 'new_v_s5_b_im': 'new_v', 'new_v_s5_c_re': 'new_v', 'new_v_s5_c_im': 'new_v', 'new_v_s5_d': 'new_v', 'new_v_s5_log_dt': 'new_v', 'new_v_s5_glu_w': 'new_v', 'new_v_s5_glu_b': 'new_v', 'new_v_mix_norm_g': 'new_v', 'new_v_w_mix_out': 'new_v', 'new_v_norm3_g': 'new_v', 'new_v_ffn2_w_in': 'new_v', 'new_v_ffn2_w_out': 'new_v', 'new_v_final_norm_g': 'new_v'}


def _forward(args):
    return _fwd_reference(*[args[k] for k in FWD_PARAMS])


def _output_shape():
    out = _jax.eval_shape(lambda: _forward(_fwd_setup_inputs(0)))
    return out.shape, out.dtype

N_MICROBATCH = 1
ADAM_LR = 0.001
ADAM_B1 = 0.9
ADAM_B2 = 0.999
ADAM_EPS = 1e-08
ADAM_WD = 0.01
ADAM_STEP = 10
PER_EXAMPLE_BATCH_AXIS = {'x': 0, 'c': 0, 'loss_target': 0}
SHARED_INPUTS = []
_WEIGHT_DTYPES = {'ada_w': _jnp.float32, 'ada_b': _jnp.float32, 'norm1_g': _jnp.float32, 'ffn1_w_in': _jnp.float32, 'ffn1_w_out': _jnp.float32, 'norm2_g': _jnp.float32, 'w_mix_in': _jnp.float32, 'sgu_w': _jnp.float32, 'sgu_b': _jnp.float32, 'pool_w': _jnp.float32, 'pool_scale': _jnp.float32, 'conv_w': _jnp.float32, 's5_lambda_re': _jnp.float32, 's5_lambda_im': _jnp.float32, 's5_b_re': _jnp.float32, 's5_b_im': _jnp.float32, 's5_c_re': _jnp.float32, 's5_c_im': _jnp.float32, 's5_d': _jnp.float32, 's5_log_dt': _jnp.float32, 's5_glu_w': _jnp.float32, 's5_glu_b': _jnp.float32, 'mix_norm_g': _jnp.float32, 'w_mix_out': _jnp.float32, 'norm3_g': _jnp.float32, 'ffn2_w_in': _jnp.float32, 'ffn2_w_out': _jnp.float32, 'final_norm_g': _jnp.float32}
MOMENT_SCALE = {'ada_w': 3.904997e-02, 'ada_b': 6.626340e-02, 'norm1_g': 1.943905e-02, 'ffn1_w_in': 8.569315e-03, 'ffn1_w_out': 1.400892e-02, 'norm2_g': 4.629854e-02, 'w_mix_in': 3.605182e-02, 'sgu_w': 1.428381e-02, 'sgu_b': 2.013675e-02, 'pool_w': 3.676261e-02, 'pool_scale': 3.975065e-02, 'conv_w': 3.634821e-02, 's5_lambda_re': 3.912204e-03, 's5_lambda_im': 3.085373e-03, 's5_b_re': 1.849323e-03, 's5_b_im': 1.845502e-03, 's5_c_re': 3.269580e-03, 's5_c_im': 3.429513e-03, 's5_d': 4.543900e-02, 's5_log_dt': 1.372311e+00, 's5_glu_w': 1.058985e-02, 's5_glu_b': 1.557249e-02, 'mix_norm_g': 3.815738e-02, 'w_mix_out': 3.788630e-02, 'norm3_g': 1.887309e-02, 'ffn2_w_in': 8.063618e-03, 'ffn2_w_out': 1.317997e-02, 'final_norm_g': 1.609784e+01}


def _to_microbatches(a, axis):
    t = _jnp.moveaxis(a, axis, 0)
    t = t.reshape((N_MICROBATCH, t.shape[0] // N_MICROBATCH) + t.shape[1:])
    return _jnp.moveaxis(t, 1, axis + 1)


def setup_inputs(seed: int = 0) -> dict:
    inp = _fwd_setup_inputs(seed)
    key = _jax.random.fold_in(_jax.random.key(seed), 7919)
    shape, _ = _output_shape()
    out = dict(inp)
    out["loss_target"] = _jax.random.normal(_jax.random.fold_in(key, 0), shape, _jnp.float32)
    for i, name in enumerate(TWIN_WEIGHTS):
        w = inp[name].astype(_jnp.float32)
        if MOMENT_SCALE is None:
            s = _jnp.sqrt(_jnp.mean(_jnp.square(w)) + 1e-30)
        else:
            s = MOMENT_SCALE[name]
        km, kv = _jax.random.split(_jax.random.fold_in(key, i + 1))
        out[name] = w
        out["m_" + name] = s * _jax.random.normal(km, w.shape, _jnp.float32)
        out["v_" + name] = (s * s) * _jax.random.uniform(kv, w.shape, _jnp.float32, 0.5, 1.5)
    if N_MICROBATCH > 1:
        for name, axis in PER_EXAMPLE_BATCH_AXIS.items():
            out[name] = _to_microbatches(out[name], axis)
    return {'x': out['x'], 'c': out['c'], 'ada_w': out['ada_w'], 'ada_b': out['ada_b'], 'norm1_g': out['norm1_g'], 'ffn1_w_in': out['ffn1_w_in'], 'ffn1_w_out': out['ffn1_w_out'], 'norm2_g': out['norm2_g'], 'w_mix_in': out['w_mix_in'], 'sgu_w': out['sgu_w'], 'sgu_b': out['sgu_b'], 'pool_w': out['pool_w'], 'pool_scale': out['pool_scale'], 'conv_w': out['conv_w'], 's5_lambda_re': out['s5_lambda_re'], 's5_lambda_im': out['s5_lambda_im'], 's5_b_re': out['s5_b_re'], 's5_b_im': out['s5_b_im'], 's5_c_re': out['s5_c_re'], 's5_c_im': out['s5_c_im'], 's5_d': out['s5_d'], 's5_log_dt': out['s5_log_dt'], 's5_glu_w': out['s5_glu_w'], 's5_glu_b': out['s5_glu_b'], 'mix_norm_g': out['mix_norm_g'], 'w_mix_out': out['w_mix_out'], 'norm3_g': out['norm3_g'], 'ffn2_w_in': out['ffn2_w_in'], 'ffn2_w_out': out['ffn2_w_out'], 'final_norm_g': out['final_norm_g'], 'loss_target': out['loss_target'], 'm_ada_w': out['m_ada_w'], 'm_ada_b': out['m_ada_b'], 'm_norm1_g': out['m_norm1_g'], 'm_ffn1_w_in': out['m_ffn1_w_in'], 'm_ffn1_w_out': out['m_ffn1_w_out'], 'm_norm2_g': out['m_norm2_g'], 'm_w_mix_in': out['m_w_mix_in'], 'm_sgu_w': out['m_sgu_w'], 'm_sgu_b': out['m_sgu_b'], 'm_pool_w': out['m_pool_w'], 'm_pool_scale': out['m_pool_scale'], 'm_conv_w': out['m_conv_w'], 'm_s5_lambda_re': out['m_s5_lambda_re'], 'm_s5_lambda_im': out['m_s5_lambda_im'], 'm_s5_b_re': out['m_s5_b_re'], 'm_s5_b_im': out['m_s5_b_im'], 'm_s5_c_re': out['m_s5_c_re'], 'm_s5_c_im': out['m_s5_c_im'], 'm_s5_d': out['m_s5_d'], 'm_s5_log_dt': out['m_s5_log_dt'], 'm_s5_glu_w': out['m_s5_glu_w'], 'm_s5_glu_b': out['m_s5_glu_b'], 'm_mix_norm_g': out['m_mix_norm_g'], 'm_w_mix_out': out['m_w_mix_out'], 'm_norm3_g': out['m_norm3_g'], 'm_ffn2_w_in': out['m_ffn2_w_in'], 'm_ffn2_w_out': out['m_ffn2_w_out'], 'm_final_norm_g': out['m_final_norm_g'], 'v_ada_w': out['v_ada_w'], 'v_ada_b': out['v_ada_b'], 'v_norm1_g': out['v_norm1_g'], 'v_ffn1_w_in': out['v_ffn1_w_in'], 'v_ffn1_w_out': out['v_ffn1_w_out'], 'v_norm2_g': out['v_norm2_g'], 'v_w_mix_in': out['v_w_mix_in'], 'v_sgu_w': out['v_sgu_w'], 'v_sgu_b': out['v_sgu_b'], 'v_pool_w': out['v_pool_w'], 'v_pool_scale': out['v_pool_scale'], 'v_conv_w': out['v_conv_w'], 'v_s5_lambda_re': out['v_s5_lambda_re'], 'v_s5_lambda_im': out['v_s5_lambda_im'], 'v_s5_b_re': out['v_s5_b_re'], 'v_s5_b_im': out['v_s5_b_im'], 'v_s5_c_re': out['v_s5_c_re'], 'v_s5_c_im': out['v_s5_c_im'], 'v_s5_d': out['v_s5_d'], 'v_s5_log_dt': out['v_s5_log_dt'], 'v_s5_glu_w': out['v_s5_glu_w'], 'v_s5_glu_b': out['v_s5_glu_b'], 'v_mix_norm_g': out['v_mix_norm_g'], 'v_w_mix_out': out['v_w_mix_out'], 'v_norm3_g': out['v_norm3_g'], 'v_ffn2_w_in': out['v_ffn2_w_in'], 'v_ffn2_w_out': out['v_ffn2_w_out'], 'v_final_norm_g': out['v_final_norm_g']}


def _loss(weights, diff, rest, loss_target):
    with _jax.named_scope("forward"):
        args = {**rest, TWIN_DIFF_INPUT: diff, **{k: w.astype(_WEIGHT_DTYPES[k]) for k, w in weights.items()}}
        y = _forward(args)
    with _jax.named_scope("loss_head"):
        err = _jnp.square(y.astype(_jnp.float32) - loss_target)
        return 0.5 * _jnp.sum(_jnp.mean(err, axis=-1)) if err.ndim else 0.5 * err


def _adamw(w, g, m, v):
    m = ADAM_B1 * m + (1.0 - ADAM_B1) * g
    v = ADAM_B2 * v + (1.0 - ADAM_B2) * _jnp.square(g)
    m_hat = m / (1.0 - ADAM_B1 ** ADAM_STEP)
    v_hat = v / (1.0 - ADAM_B2 ** ADAM_STEP)
    delta = -ADAM_LR * (m_hat / (_jnp.sqrt(v_hat) + ADAM_EPS) + ADAM_WD * w)
    return delta, m, v


def reference(x, c, ada_w, ada_b, norm1_g, ffn1_w_in, ffn1_w_out, norm2_g, w_mix_in, sgu_w, sgu_b, pool_w, pool_scale, conv_w, s5_lambda_re, s5_lambda_im, s5_b_re, s5_b_im, s5_c_re, s5_c_im, s5_d, s5_log_dt, s5_glu_w, s5_glu_b, mix_norm_g, w_mix_out, norm3_g, ffn2_w_in, ffn2_w_out, final_norm_g, loss_target, m_ada_w, m_ada_b, m_norm1_g, m_ffn1_w_in, m_ffn1_w_out, m_norm2_g, m_w_mix_in, m_sgu_w, m_sgu_b, m_pool_w, m_pool_scale, m_conv_w, m_s5_lambda_re, m_s5_lambda_im, m_s5_b_re, m_s5_b_im, m_s5_c_re, m_s5_c_im, m_s5_d, m_s5_log_dt, m_s5_glu_w, m_s5_glu_b, m_mix_norm_g, m_w_mix_out, m_norm3_g, m_ffn2_w_in, m_ffn2_w_out, m_final_norm_g, v_ada_w, v_ada_b, v_norm1_g, v_ffn1_w_in, v_ffn1_w_out, v_norm2_g, v_w_mix_in, v_sgu_w, v_sgu_b, v_pool_w, v_pool_scale, v_conv_w, v_s5_lambda_re, v_s5_lambda_im, v_s5_b_re, v_s5_b_im, v_s5_c_re, v_s5_c_im, v_s5_d, v_s5_log_dt, v_s5_glu_w, v_s5_glu_b, v_mix_norm_g, v_w_mix_out, v_norm3_g, v_ffn2_w_in, v_ffn2_w_out, v_final_norm_g):
    given = dict(x=x, c=c, ada_w=ada_w, ada_b=ada_b, norm1_g=norm1_g, ffn1_w_in=ffn1_w_in, ffn1_w_out=ffn1_w_out, norm2_g=norm2_g, w_mix_in=w_mix_in, sgu_w=sgu_w, sgu_b=sgu_b, pool_w=pool_w, pool_scale=pool_scale, conv_w=conv_w, s5_lambda_re=s5_lambda_re, s5_lambda_im=s5_lambda_im, s5_b_re=s5_b_re, s5_b_im=s5_b_im, s5_c_re=s5_c_re, s5_c_im=s5_c_im, s5_d=s5_d, s5_log_dt=s5_log_dt, s5_glu_w=s5_glu_w, s5_glu_b=s5_glu_b, mix_norm_g=mix_norm_g, w_mix_out=w_mix_out, norm3_g=norm3_g, ffn2_w_in=ffn2_w_in, ffn2_w_out=ffn2_w_out, final_norm_g=final_norm_g, loss_target=loss_target, m_ada_w=m_ada_w, m_ada_b=m_ada_b, m_norm1_g=m_norm1_g, m_ffn1_w_in=m_ffn1_w_in, m_ffn1_w_out=m_ffn1_w_out, m_norm2_g=m_norm2_g, m_w_mix_in=m_w_mix_in, m_sgu_w=m_sgu_w, m_sgu_b=m_sgu_b, m_pool_w=m_pool_w, m_pool_scale=m_pool_scale, m_conv_w=m_conv_w, m_s5_lambda_re=m_s5_lambda_re, m_s5_lambda_im=m_s5_lambda_im, m_s5_b_re=m_s5_b_re, m_s5_b_im=m_s5_b_im, m_s5_c_re=m_s5_c_re, m_s5_c_im=m_s5_c_im, m_s5_d=m_s5_d, m_s5_log_dt=m_s5_log_dt, m_s5_glu_w=m_s5_glu_w, m_s5_glu_b=m_s5_glu_b, m_mix_norm_g=m_mix_norm_g, m_w_mix_out=m_w_mix_out, m_norm3_g=m_norm3_g, m_ffn2_w_in=m_ffn2_w_in, m_ffn2_w_out=m_ffn2_w_out, m_final_norm_g=m_final_norm_g, v_ada_w=v_ada_w, v_ada_b=v_ada_b, v_norm1_g=v_norm1_g, v_ffn1_w_in=v_ffn1_w_in, v_ffn1_w_out=v_ffn1_w_out, v_norm2_g=v_norm2_g, v_w_mix_in=v_w_mix_in, v_sgu_w=v_sgu_w, v_sgu_b=v_sgu_b, v_pool_w=v_pool_w, v_pool_scale=v_pool_scale, v_conv_w=v_conv_w, v_s5_lambda_re=v_s5_lambda_re, v_s5_lambda_im=v_s5_lambda_im, v_s5_b_re=v_s5_b_re, v_s5_b_im=v_s5_b_im, v_s5_c_re=v_s5_c_re, v_s5_c_im=v_s5_c_im, v_s5_d=v_s5_d, v_s5_log_dt=v_s5_log_dt, v_s5_glu_w=v_s5_glu_w, v_s5_glu_b=v_s5_glu_b, v_mix_norm_g=v_mix_norm_g, v_w_mix_out=v_w_mix_out, v_norm3_g=v_norm3_g, v_ffn2_w_in=v_ffn2_w_in, v_ffn2_w_out=v_ffn2_w_out, v_final_norm_g=v_final_norm_g)
    weights = {n: given[n] for n in TWIN_WEIGHTS}
    shared = {n: given[n] for n in SHARED_INPUTS}
    per_example = {n: given[n] for n in ['x', 'c']}
    grad_fn = _jax.value_and_grad(_loss, argnums=(0, 1))

    def one_microbatch(ex, loss_target):
        ex = dict(ex)
        diff = ex.pop(TWIN_DIFF_INPUT)
        return grad_fn(weights, diff, {**shared, **ex}, loss_target)

    if N_MICROBATCH == 1:
        loss, (grad_w, grad_x) = one_microbatch(per_example, given["loss_target"])
    else:
        def body(carry, xs):
            loss_sum, grad_sum = carry
            l_k, (gw_k, gx_k) = one_microbatch(xs[0], xs[1])
            with _jax.named_scope("update"):
                return (loss_sum + l_k, _jax.tree.map(_jnp.add, grad_sum, gw_k)), gx_k

        init = (_jnp.zeros((), _jnp.float32), _jax.tree.map(_jnp.zeros_like, weights))
        (loss, grad_w), grad_x = _jax.lax.scan(body, init, (per_example, given["loss_target"]))
    with _jax.named_scope("update"):
        delta_w, new_m, new_v = {}, {}, {}
        for n in TWIN_WEIGHTS:
            delta_w[n], new_m[n], new_v[n] = _adamw(weights[n], grad_w[n], given["m_" + n], given["v_" + n])
    return (loss, grad_x, *[grad_w[n] for n in TWIN_WEIGHTS], *[delta_w[n] for n in TWIN_WEIGHTS],
            *[new_m[n] for n in TWIN_WEIGHTS], *[new_v[n] for n in TWIN_WEIGHTS])
```

```python
import functools
import math

import jax
import jax.numpy as jnp
from jax import lax
from jax.experimental import pallas as pl
from jax.experimental.pallas import tpu as pltpu

F32, BF16 = jnp.float32, jnp.bfloat16
EPS = 1e-6
DEPTH = 4
N_DEV = 8
N_CHIP = 4
W_GRP = 256
CHUNK = 128
N_SEG = 8
LANES = 128
FFN_TF = 256
VMEM_LIMIT = 56 * 1024 * 1024
ADAM_LR, ADAM_B1, ADAM_B2, ADAM_EPS, ADAM_WD, ADAM_STEP = 0.001, 0.9, 0.999, 1e-08, 0.01, 10
MESH_ID = pl.DeviceIdType.MESH
HI = lax.Precision.HIGHEST
ANY = pl.BlockSpec(memory_space=pl.ANY)


def _cp(**kw):
    return pltpu.CompilerParams(vmem_limit_bytes=VMEM_LIMIT, **kw)


def _dot(a, b):
    return jnp.dot(a.astype(BF16), b.astype(BF16), preferred_element_type=F32)


def _dot_nt(a, b):
    return lax.dot_general(a.astype(BF16), b.astype(BF16), (((1,), (1,)), ((), ())), preferred_element_type=F32)


def _dot_tn(a, b):
    return lax.dot_general(a.astype(BF16), b.astype(BF16), (((0,), (0,)), ((), ())), preferred_element_type=F32)


def _dot_hi(a, b):
    return jnp.dot(a, b, preferred_element_type=F32, precision=HI)


def _gelu(x):
    k = 0.7978845608028654
    t = jnp.tanh(k * (x + 0.044715 * x * x * x))
    return 0.5 * x * (1.0 + t), t


def _gelu_grad(x, t):
    k = 0.7978845608028654
    return 0.5 * (1.0 + t) + 0.5 * x * (1.0 - t * t) * k * (1.0 + 3.0 * 0.044715 * x * x)


def _iota(shape, axis):
    return lax.broadcasted_iota(jnp.int32, shape, axis)


def _full(shape):
    nd = len(shape)
    return pl.BlockSpec(shape, lambda *_: (0,) * nd)


def _norm_mod(xv, g, shift, scale):
    r = lax.rsqrt(jnp.mean(xv * xv, axis=-1, keepdims=True) + EPS)
    return (xv * r * g) * (1.0 + scale) + shift


def _norm_mod_bwd(xv, g, scale, dh):
    r = lax.rsqrt(jnp.mean(xv * xv, axis=-1, keepdims=True) + EPS)
    xh = xv * r
    n = xh * g
    dsh = jnp.sum(dh, axis=0, keepdims=True)
    dsc = jnp.sum(dh * n, axis=0, keepdims=True)
    dn = dh * (1.0 + scale)
    dg = jnp.sum(dn * xh, axis=0, keepdims=True)
    dxh = dn * g
    dx = r * (dxh - xh * jnp.mean(dxh * xh, axis=-1, keepdims=True))
    return dx, dsh, dsc, dg


def _tm(s):
    return min(s, 1024)


def _ffn_fwd(x, mod, g, wi, wo):
    s, d = x.shape
    f = wo.shape[0]
    tf, tm = FFN_TF, _tm(s)
    nf, nt = f // tf, s // tm

    def body(x_ref, mod_ref, g_ref, wa_ref, wb_ref, wo_ref, xn_ref, h_ref, a_ref, b_ref, o_ref):
        j = pl.program_id(1)

        @pl.when(j == 0)
        def _():
            hh = _norm_mod(x_ref[...], g_ref[...], mod_ref[0:1, :], mod_ref[1:2, :])
            h_ref[...] = hh.astype(BF16)
            o_ref[...] = jnp.zeros_like(o_ref)

        h = h_ref[...]
        a = jnp.dot(h, wa_ref[...], preferred_element_type=F32)
        b = jnp.dot(h, wb_ref[...], preferred_element_type=F32)
        a_ref[...] = a.astype(BF16)
        b_ref[...] = b.astype(BF16)
        u = (a * jax.nn.sigmoid(a)) * b
        o_ref[...] += jnp.dot(u.astype(BF16), wo_ref[...], preferred_element_type=F32)

        @pl.when(j == nf - 1)
        def _():
            xn_ref[...] = x_ref[...] + 0.5 * mod_ref[2:3, :] * o_ref[...]

    row = pl.BlockSpec((tm, d), lambda i, j: (i, 0))
    chunk = pl.BlockSpec((tm, tf), lambda i, j: (i, j))
    return pl.pallas_call(
        body, name="ffn_fwd", grid=(nt, nf),
        in_specs=[row, _full((3, d)), _full((1, d)),
                  pl.BlockSpec((None, d, tf), lambda i, j: (0, 0, j)),
                  pl.BlockSpec((None, d, tf), lambda i, j: (1, 0, j)),
                  pl.BlockSpec((tf, d), lambda i, j: (j, 0))],
        out_specs=[row, row, chunk, chunk, row],
        out_shape=[jax.ShapeDtypeStruct((s, d), F32), jax.ShapeDtypeStruct((s, d), BF16),
                   jax.ShapeDtypeStruct((s, f), BF16), jax.ShapeDtypeStruct((s, f), BF16),
                   jax.ShapeDtypeStruct((s, d), F32)],
        compiler_params=_cp(dimension_semantics=("parallel", "arbitrary")),
    )(x, mod, g, wi, wi, wo)


def _gate_bwd(dxo, o, gate, half):
    s, d = dxo.shape
    tm = _tm(s)
    nt = s // tm

    def body(dx_ref, o_ref, gate_ref, do_ref, dg_ref):
        i = pl.program_id(0)

        @pl.when(i == 0)
        def _():
            dg_ref[...] = jnp.zeros_like(dg_ref)

        dx = dx_ref[...]
        do_ref[...] = (half * gate_ref[...] * dx).astype(BF16)
        dg_ref[...] += half * jnp.sum(o_ref[...] * dx, axis=0, keepdims=True)

    row = pl.BlockSpec((tm, d), lambda i: (i, 0))
    return pl.pallas_call(
        body, name="gate_bwd", grid=(nt,),
        in_specs=[row, row, _full((1, d))], out_specs=[row, _full((1, d))],
        out_shape=[jax.ShapeDtypeStruct((s, d), BF16), jax.ShapeDtypeStruct((1, d), F32)],
        compiler_params=_cp(dimension_semantics=("arbitrary",)),
    )(dxo, o, gate)


def _ffn_bwd_main(do, h, a, b, wo):
    s, d = do.shape
    f = wo.shape[0]
    tf, tm = FFN_TF, _tm(s)
    nf, nt = f // tf, s // tm

    def body(do_ref, h_ref, a_ref, b_ref, wo_ref, dza_ref, dzb_ref, dwi_ref, dwo_ref, acc_a, acc_b, acc_o):
        i = pl.program_id(1)

        @pl.when(i == 0)
        def _():
            acc_a[...] = jnp.zeros_like(acc_a)
            acc_b[...] = jnp.zeros_like(acc_b)
            acc_o[...] = jnp.zeros_like(acc_o)

        dov = do_ref[...]
        hv = h_ref[...]
        du = lax.dot_general(dov, wo_ref[...], (((1,), (1,)), ((), ())), preferred_element_type=F32)
        av = a_ref[...].astype(F32)
        bv = b_ref[...].astype(F32)
        sa = jax.nn.sigmoid(av)
        si = av * sa
        u = (si * bv).astype(BF16)
        da = (du * bv * (sa * (1.0 + av * (1.0 - sa)))).astype(BF16)
        db = (du * si).astype(BF16)
        dza_ref[...] = da
        dzb_ref[...] = db
        acc_o[...] += _dot_tn(u, dov)
        acc_a[...] += _dot_tn(hv, da)
        acc_b[...] += _dot_tn(hv, db)

        @pl.when(i == nt - 1)
        def _():
            dwi_ref[0] = acc_a[...].astype(BF16)
            dwi_ref[1] = acc_b[...].astype(BF16)
            dwo_ref[...] = acc_o[...].astype(BF16)

    row = pl.BlockSpec((tm, d), lambda j, i: (i, 0))
    chunk = pl.BlockSpec((tm, tf), lambda j, i: (i, j))
    return pl.pallas_call(
        body, name="ffn_bwd_main", grid=(nf, nt),
        in_specs=[row, row, chunk, chunk, pl.BlockSpec((tf, d), lambda j, i: (j, 0))],
        out_specs=[chunk, chunk, pl.BlockSpec((2, d, tf), lambda j, i: (0, 0, j)),
                   pl.BlockSpec((tf, d), lambda j, i: (j, 0))],
        out_shape=[jax.ShapeDtypeStruct((s, f), BF16), jax.ShapeDtypeStruct((s, f), BF16),
                   jax.ShapeDtypeStruct((2, d, f), BF16), jax.ShapeDtypeStruct((f, d), BF16)],
        scratch_shapes=[pltpu.VMEM((d, tf), F32), pltpu.VMEM((d, tf), F32), pltpu.VMEM((tf, d), F32)],
        compiler_params=_cp(dimension_semantics=("parallel", "arbitrary")),
    )(do, h, a, b, wo)


def _ffn_bwd_in(dza, dzb, wi, x, dxo, mod, g):
    s, d = x.shape
    f = dza.shape[1]
    tf, tm = FFN_TF, _tm(s)
    nf, nt = f // tf, s // tm

    def body(dza_ref, dzb_ref, wa_ref, wb_ref, x_ref, dxo_ref, mod_ref, g_ref, dx_ref, rows_ref, acc):
        i, j = pl.program_id(0), pl.program_id(1)

        @pl.when(jnp.logical_and(i == 0, j == 0))
        def _():
            rows_ref[...] = jnp.zeros_like(rows_ref)

        @pl.when(j == 0)
        def _():
            acc[...] = jnp.zeros_like(acc)

        acc[...] += (lax.dot_general(dza_ref[...], wa_ref[...], (((1,), (1,)), ((), ())), preferred_element_type=F32)
                     + lax.dot_general(dzb_ref[...], wb_ref[...], (((1,), (1,)), ((), ())), preferred_element_type=F32))

        @pl.when(j == nf - 1)
        def _():
            dx, dsh, dsc, dg = _norm_mod_bwd(x_ref[...], g_ref[...], mod_ref[1:2, :], acc[...])
            dx_ref[...] = dx + dxo_ref[...]
            rows_ref[0:1, :] += dsh
            rows_ref[1:2, :] += dsc
            rows_ref[2:3, :] += dg

    row = pl.BlockSpec((tm, d), lambda i, j: (i, 0))
    chunk = pl.BlockSpec((tm, tf), lambda i, j: (i, j))
    return pl.pallas_call(
        body, name="ffn_bwd_in", grid=(nt, nf),
        in_specs=[chunk, chunk,
                  pl.BlockSpec((None, d, tf), lambda i, j: (0, 0, j)),
                  pl.BlockSpec((None, d, tf), lambda i, j: (1, 0, j)),
                  row, row, _full((3, d)), _full((1, d))],
        out_specs=[row, _full((8, d))],
        out_shape=[jax.ShapeDtypeStruct((s, d), F32), jax.ShapeDtypeStruct((8, d), F32)],
        scratch_shapes=[pltpu.VMEM((tm, d), F32)],
        compiler_params=_cp(dimension_semantics=("arbitrary", "arbitrary")),
    )(dza, dzb, wi, wi, x, dxo, mod, g)


def _mix_in_fwd(x, mod, g, wmit):
    s, d = x.shape
    p = wmit.shape[0]
    tm = _tm(s)

    def body(x_ref, mod_ref, g_ref, w_ref, z_ref, h_ref):
        hh = _norm_mod(x_ref[...], g_ref[...], mod_ref[0:1, :], mod_ref[1:2, :]).astype(BF16)
        h_ref[...] = hh
        z_ref[...] = lax.dot_general(hh, w_ref[...], (((1,), (1,)), ((), ())), preferred_element_type=F32)

    row = pl.BlockSpec((tm, d), lambda i: (i, 0))
    return pl.pallas_call(
        body, name="mix_in_fwd", grid=(s // tm,),
        in_specs=[row, _full((3, d)), _full((1, d)), _full((p, d))],
        out_specs=[pl.BlockSpec((tm, p), lambda i: (i, 0)), row],
        out_shape=[jax.ShapeDtypeStruct((s, p), F32), jax.ShapeDtypeStruct((s, d), BF16)],
        compiler_params=_cp(dimension_semantics=("parallel",)),
    )(x, mod, g, wmit)


def _mix_in_bwd(dzs, h, wmit, x, dxo, mod, g):
    s, d = x.shape
    p = wmit.shape[0]
    tm = min(s, 512)
    nt = s // tm

    def body(za_ref, zb_ref, zc_ref, zd_ref, h_ref, w_ref, x_ref, dxo_ref, mod_ref, g_ref,
             dx_ref, rows_ref, dw_ref, acc):
        i = pl.program_id(0)

        @pl.when(i == 0)
        def _():
            rows_ref[...] = jnp.zeros_like(rows_ref)
            acc[...] = jnp.zeros_like(acc)

        dz = jnp.concatenate([za_ref[...], zb_ref[...], zc_ref[...], zd_ref[...]], axis=1).astype(BF16)
        acc[...] += _dot_tn(dz, h_ref[...])
        dh = jnp.dot(dz, w_ref[...], preferred_element_type=F32)
        dx, dsh, dsc, dg = _norm_mod_bwd(x_ref[...], g_ref[...], mod_ref[1:2, :], dh)
        dx_ref[...] = dx + dxo_ref[...]
        rows_ref[0:1, :] += dsh
        rows_ref[1:2, :] += dsc
        rows_ref[2:3, :] += dg

        @pl.when(i == nt - 1)
        def _():
            dw_ref[...] = acc[...].astype(BF16)

    row = pl.BlockSpec((tm, d), lambda i: (i, 0))
    zspecs = [pl.BlockSpec((tm, z.shape[1]), lambda i: (i, 0)) for z in dzs]
    return pl.pallas_call(
        body, name="mix_in_bwd", grid=(nt,),
        in_specs=zspecs + [row, _full((p, d)), row, row, _full((3, d)), _full((1, d))],
        out_specs=[row, _full((8, d)), _full((p, d))],
        out_shape=[jax.ShapeDtypeStruct((s, d), F32), jax.ShapeDtypeStruct((8, d), F32),
                   jax.ShapeDtypeStruct((p, d), BF16)],
        scratch_shapes=[pltpu.VMEM((p, d), F32)],
        compiler_params=_cp(dimension_semantics=("arbitrary",)),
    )(*dzs, h, wmit, x, dxo, mod, g)


def _group_norm(ys, mng):
    outs, hats, rs = [], [], []
    for k, y in enumerate(ys):
        r = lax.rsqrt(jnp.mean(y * y, axis=-1, keepdims=True) + EPS)
        yh = y * r
        hats.append(yh)
        rs.append(r)
        outs.append(yh * mng[:, k * W_GRP:(k + 1) * W_GRP])
    return jnp.concatenate(outs, axis=1), hats, rs


def _mix_out_fwd(ys, mng, wmo, x, gate):
    s, d = x.shape
    tm = _tm(s)

    def body(ya, yb, yc, yd, mng_ref, w_ref, x_ref, gate_ref, xn_ref, m_ref):
        yn, _, _ = _group_norm([ya[...], yb[...], yc[...], yd[...]], mng_ref[...])
        m = jnp.dot(yn.astype(BF16), w_ref[...], preferred_element_type=F32)
        m_ref[...] = m
        xn_ref[...] = x_ref[...] + gate_ref[...] * m

    row = pl.BlockSpec((tm, d), lambda i: (i, 0))
    grp = pl.BlockSpec((tm, W_GRP), lambda i: (i, 0))
    return pl.pallas_call(
        body, name="mix_out_fwd", grid=(s // tm,),
        in_specs=[grp, grp, grp, grp, _full((1, d)), _full((d, d)), row, _full((1, d))],
        out_specs=[row, row],
        out_shape=[jax.ShapeDtypeStruct((s, d), F32), jax.ShapeDtypeStruct((s, d), F32)],
        compiler_params=_cp(dimension_semantics=("parallel",)),
    )(*ys, mng, wmo, x, gate)


def _mix_out_bwd(dxo, m, gate, ys, mng, wmo):
    s, d = dxo.shape
    tm = min(s, 512)
    nt = s // tm

    def body(dxo_ref, m_ref, gate_ref, ya, yb, yc, yd, mng_ref, w_ref,
             dya, dyb, dyc, dyd, dgate_ref, dmng_ref, dw_ref, acc):
        i = pl.program_id(0)

        @pl.when(i == 0)
        def _():
            dgate_ref[...] = jnp.zeros_like(dgate_ref)
            dmng_ref[...] = jnp.zeros_like(dmng_ref)
            acc[...] = jnp.zeros_like(acc)

        dxv = dxo_ref[...]
        dgate_ref[...] += jnp.sum(m_ref[...] * dxv, axis=0, keepdims=True)
        dm = (gate_ref[...] * dxv).astype(BF16)
        mng = mng_ref[...]
        yn, hats, rs = _group_norm([ya[...], yb[...], yc[...], yd[...]], mng)
        acc[...] += _dot_tn(yn, dm)
        dyn = lax.dot_general(dm, w_ref[...], (((1,), (1,)), ((), ())), preferred_element_type=F32)
        dmng_parts = []
        for k, (yh, r, out) in enumerate(zip(hats, rs, (dya, dyb, dyc, dyd))):
            dk = dyn[:, k * W_GRP:(k + 1) * W_GRP]
            dmng_parts.append(jnp.sum(dk * yh, axis=0, keepdims=True))
            dyh = dk * mng[:, k * W_GRP:(k + 1) * W_GRP]
            out[...] = r * (dyh - yh * jnp.mean(dyh * yh, axis=-1, keepdims=True))
        dmng_ref[...] += jnp.concatenate(dmng_parts, axis=1)

        @pl.when(i == nt - 1)
        def _():
            dw_ref[...] = acc[...].astype(BF16)

    row = pl.BlockSpec((tm, d), lambda i: (i, 0))
    grp = pl.BlockSpec((tm, W_GRP), lambda i: (i, 0))
    return pl.pallas_call(
        body, name="mix_out_bwd", grid=(nt,),
        in_specs=[row, row, _full((1, d)), grp, grp, grp, grp, _full((1, d)), _full((d, d))],
        out_specs=[grp, grp, grp, grp, _full((1, d)), _full((1, d)), _full((d, d))],
        out_shape=[jax.ShapeDtypeStruct((s, W_GRP), F32)] * 4
        + [jax.ShapeDtypeStruct((1, d), F32), jax.ShapeDtypeStruct((1, d), F32), jax.ShapeDtypeStruct((d, d), BF16)],
        scratch_shapes=[pltpu.VMEM((d, d), F32)],
        compiler_params=_cp(dimension_semantics=("arbitrary",)),
    )(dxo, m, gate, *ys, mng, wmo)


def _sgu_consts():
    r = _iota((W_GRP, W_GRP), 0) >> 6
    c = _iota((W_GRP, W_GRP), 1) >> 6
    avg = jnp.where(r == c, 1.0 / 64.0, 0.0).astype(F32)
    tril = _iota((CHUNK, CHUNK), 0) >= _iota((CHUNK, CHUNK), 1)
    head = _iota((CHUNK, W_GRP), 1) >> 6
    return avg, tril, head


def _sgu_pre(za, avg):
    zg, t = _gelu(za)
    u, v = zg[:, :W_GRP], zg[:, W_GRP:]
    mu = _dot_hi(v, avg)
    vc = v - mu
    r = lax.rsqrt(_dot_hi(vc * vc, avg) + EPS)
    return t, u, vc * r, r


def _sgu_fwd(z, sgu_w, bias_full):
    s = z.shape[0]
    tm = min(s, 512)

    def body(za_ref, w_ref, bias_ref, ya_ref):
        avg, tril, head = _sgu_consts()
        _, u, vn, _ = _sgu_pre(za_ref[...], avg)
        wm = [jnp.where(tril, w_ref[h], 0.0).astype(BF16) for h in range(4)]
        vb = vn.astype(BF16)
        for n in range(tm // CHUNK):
            rows = slice(n * CHUNK, (n + 1) * CHUNK)
            mixed = bias_ref[...]
            for h in range(4):
                mixed = mixed + jnp.where(head == h, jnp.dot(wm[h], vb[rows], preferred_element_type=F32), 0.0)
            ya_ref[rows, :] = u[rows] * mixed

    return pl.pallas_call(
        body, name="sgu_fwd", grid=(s // tm,),
        in_specs=[pl.BlockSpec((tm, 2 * W_GRP), lambda i: (i, 0)), _full((4, CHUNK, CHUNK)), _full((CHUNK, W_GRP))],
        out_specs=pl.BlockSpec((tm, W_GRP), lambda i: (i, 0)),
        out_shape=jax.ShapeDtypeStruct((s, W_GRP), F32),
        compiler_params=_cp(dimension_semantics=("parallel",)),
    )(z, sgu_w, bias_full)


def _sgu_bwd(z, dya, sgu_w, bias_full):
    s = z.shape[0]
    tm = min(s, 512)
    nt = s // tm

    def body(za_ref, dya_ref, w_ref, bias_ref, dza_ref, dw_ref, db_ref, du_s, dvn_s):
        i = pl.program_id(0)

        @pl.when(i == 0)
        def _():
            dw_ref[...] = jnp.zeros_like(dw_ref)
            db_ref[...] = jnp.zeros_like(db_ref)

        avg, tril, head = _sgu_consts()
        za = za_ref[...]
        t, u, vn, r = _sgu_pre(za, avg)
        wm = [jnp.where(tril, w_ref[h], 0.0).astype(BF16) for h in range(4)]
        vb = vn.astype(BF16)
        dya = dya_ref[...]
        dw = [jnp.zeros((CHUNK, CHUNK), F32) for _ in range(4)]
        db = jnp.zeros((CHUNK, W_GRP), F32)
        for n in range(tm // CHUNK):
            rows = slice(n * CHUNK, (n + 1) * CHUNK)
            mixed = bias_ref[...]
            for h in range(4):
                mixed = mixed + jnp.where(head == h, jnp.dot(wm[h], vb[rows], preferred_element_type=F32), 0.0)
            dmix = dya[rows] * u[rows]
            du_s[rows, :] = dya[rows] * mixed
            db = db + dmix
            dmb = dmix.astype(BF16)
            dvn = jnp.zeros((CHUNK, W_GRP), F32)
            for h in range(4):
                dmh = jnp.where(head == h, dmix, 0.0)
                dw[h] = dw[h] + _dot_nt(dmh, vb[rows])
                dvn = dvn + jnp.where(head == h, _dot_tn(wm[h], dmb), 0.0)
            dvn_s[rows, :] = dvn
        for h in range(4):
            dw_ref[h] += jnp.where(tril, dw[h], 0.0)
        sel = ((_iota((W_GRP, CHUNK), 0) >> 6) == _iota((W_GRP, CHUNK), 1)).astype(F32)
        db_ref[...] += _dot_hi(db, sel)
        dvn = dvn_s[...]
        dv = r * (dvn - _dot_hi(dvn, avg) - vn * _dot_hi(dvn * vn, avg))
        dzg = jnp.concatenate([du_s[...], dv], axis=1)
        dza_ref[...] = dzg * _gelu_grad(za, t)

    return pl.pallas_call(
        body, name="sgu_bwd", grid=(nt,),
        in_specs=[pl.BlockSpec((tm, 2 * W_GRP), lambda i: (i, 0)), pl.BlockSpec((tm, W_GRP), lambda i: (i, 0)),
                  _full((4, CHUNK, CHUNK)), _full((CHUNK, W_GRP))],
        out_specs=[pl.BlockSpec((tm, 2 * W_GRP), lambda i: (i, 0)), _full((4, CHUNK, CHUNK)), _full((CHUNK, CHUNK))],
        out_shape=[jax.ShapeDtypeStruct((s, 2 * W_GRP), F32), jax.ShapeDtypeStruct((4, CHUNK, CHUNK), F32),
                   jax.ShapeDtypeStruct((CHUNK, CHUNK), F32)],
        scratch_shapes=[pltpu.VMEM((tm, W_GRP), F32), pltpu.VMEM((tm, W_GRP), F32)],
        compiler_params=_cp(dimension_semantics=("arbitrary",)),
    )(z, dya, sgu_w, bias_full)


def _shift_down(x, k):
    return jnp.where(_iota(x.shape, 0) < k, 0.0, pltpu.roll(x, k, 0))


def _shift_up(x, k):
    n = x.shape[0]
    return jnp.where(_iota(x.shape, 0) >= n - k, 0.0, pltpu.roll(x, n - k, 0))


def _by_pool_group(shape, v2, v4, v8, v16):
    col = _iota(shape, 1)
    return jnp.where(col < 64, v2, jnp.where(col < 128, v4, jnp.where(col < 192, v8, v16)))


def _pool_core(zb, pw2):
    s2 = zb + _shift_down(zb, 1)
    s4 = s2 + _shift_down(s2, 2)
    s8 = s4 + _shift_down(s4, 4)
    s16 = s8 + _shift_down(s8, 8)
    win = _by_pool_group(zb.shape, s2, s4, s8, s16)
    wlen = _by_pool_group(zb.shape, 2.0, 4.0, 8.0, 16.0)
    cnt = jnp.minimum((_iota(zb.shape, 0) + 1).astype(F32), wlen)
    p = win / cnt - zb
    wt = jnp.tile(pw2, (1, 4))
    wbd = jnp.where((_iota(wt.shape, 0) >> 6) == (_iota(wt.shape, 1) >> 6), wt, 0.0).astype(BF16)
    return p, cnt, wbd


def _conv_core(zc, cw):
    bg, cg, xh = zc[:, :W_GRP], zc[:, W_GRP:2 * W_GRP], zc[:, 2 * W_GRP:]
    y = cg * xh
    y1, y2 = _shift_down(y, 1), _shift_down(y, 2)
    out = cw[2:3, :] * y + cw[1:2, :] * y1 + cw[0:1, :] * y2
    return bg, cg, xh, y, y1, y2, out


def _poolconv_fwd(z, pw2, pscale, cw):
    s = z.shape[0]

    def body(zb_ref, zc_ref, pw_ref, ps_ref, cw_ref, yb_ref, yc_ref):
        p, _, wbd = _pool_core(zb_ref[...], pw_ref[...])
        yb_ref[...] = jnp.dot(p.astype(BF16), wbd, preferred_element_type=F32) * ps_ref[...]
        bg, _, _, _, _, _, out = _conv_core(zc_ref[...], cw_ref[...])
        yc_ref[...] = bg * out

    return pl.pallas_call(
        body, name="poolconv_fwd", grid=(1,),
        in_specs=[pl.BlockSpec((s, W_GRP), lambda i: (0, 2)), pl.BlockSpec((s, 3 * W_GRP), lambda i: (0, 1)),
                  _full((W_GRP, 64)), _full((1, W_GRP)), _full((3, W_GRP))],
        out_specs=[_full((s, W_GRP)), _full((s, W_GRP))],
        out_shape=[jax.ShapeDtypeStruct((s, W_GRP), F32)] * 2,
        compiler_params=_cp(dimension_semantics=("arbitrary",)),
    )(z, z, pw2, pscale, cw)


def _poolconv_bwd(z, dyb, dyc, pw2, pscale, cw):
    s = z.shape[0]

    def body(zb_ref, zc_ref, dyb_ref, dyc_ref, pw_ref, ps_ref, cw_ref, dzb_ref, dzc_ref, dw_ref, dps_ref, dcw_ref):
        zb = zb_ref[...]
        p, cnt, wbd = _pool_core(zb, pw_ref[...])
        pb = p.astype(BF16)
        out = jnp.dot(pb, wbd, preferred_element_type=F32)
        dyb = dyb_ref[...]
        dps_ref[...] = jnp.sum(dyb * out, axis=0, keepdims=True)
        dout = (dyb * ps_ref[...]).astype(BF16)
        dw = _dot_tn(pb, dout)
        dw_ref[...] = jnp.where((_iota(dw.shape, 0) >> 6) == (_iota(dw.shape, 1) >> 6), dw, 0.0)
        dp = lax.dot_general(dout, wbd, (((1,), (1,)), ((), ())), preferred_element_type=F32)
        dwin = dp / cnt
        t2 = dwin + _shift_up(dwin, 1)
        t4 = t2 + _shift_up(t2, 2)
        t8 = t4 + _shift_up(t4, 4)
        t16 = t8 + _shift_up(t8, 8)
        dzb_ref[...] = _by_pool_group(zb.shape, t2, t4, t8, t16) - dp

        cw = cw_ref[...]
        bg, cg, xh, y, y1, y2, out = _conv_core(zc_ref[...], cw)
        dyc = dyc_ref[...]
        dout = dyc * bg
        dcw_ref[...] = jnp.zeros_like(dcw_ref)
        dcw_ref[0:1, :] = jnp.sum(dout * y2, axis=0, keepdims=True)
        dcw_ref[1:2, :] = jnp.sum(dout * y1, axis=0, keepdims=True)
        dcw_ref[2:3, :] = jnp.sum(dout * y, axis=0, keepdims=True)
        dy = cw[2:3, :] * dout + cw[1:2, :] * _shift_up(dout, 1) + cw[0:1, :] * _shift_up(dout, 2)
        dzc_ref[...] = jnp.concatenate([dyc * out, dy * xh, dy * cg], axis=1)

    return pl.pallas_call(
        body, name="poolconv_bwd", grid=(1,),
        in_specs=[pl.BlockSpec((s, W_GRP), lambda i: (0, 2)), pl.BlockSpec((s, 3 * W_GRP), lambda i: (0, 1)),
                  _full((s, W_GRP)), _full((s, W_GRP)), _full((W_GRP, 64)), _full((1, W_GRP)), _full((3, W_GRP))],
        out_specs=[_full((s, W_GRP)), _full((s, 3 * W_GRP)), _full((W_GRP, W_GRP)), _full((1, W_GRP)), _full((8, W_GRP))],
        out_shape=[jax.ShapeDtypeStruct((s, W_GRP), F32), jax.ShapeDtypeStruct((s, 3 * W_GRP), F32),
                   jax.ShapeDtypeStruct((W_GRP, W_GRP), F32), jax.ShapeDtypeStruct((1, W_GRP), F32),
                   jax.ShapeDtypeStruct((8, W_GRP), F32)],
        compiler_params=_cp(dimension_semantics=("arbitrary",)),
    )(z, z, dyb, dyc, pw2, pscale, cw)


N_STATE = 1024
HALF_STATE = N_STATE // 2
HALF_CH = W_GRP // 2
N_SLAB = HALF_STATE // LANES


def _s5_disc(lre, lim, ldt):
    dt = jnp.exp(ldt)
    mag = jnp.exp(lre * dt)
    ang = lim * dt
    ar, ai = mag * jnp.cos(ang), mag * jnp.sin(ang)
    nr, ni = ar - 1.0, ai
    den = lre * lre + lim * lim
    kr = (nr * lre + ni * lim) / den
    ki = (ni * lre - nr * lim) / den
    return ar, ai, kr, ki


def _s5_mats(colp, br, bi, cr, ci):
    _, _, kr, ki = _s5_disc(colp[:, 0:1], colp[:, 1:2], colp[:, 2:3])
    bbr = kr * br - ki * bi
    bbi = kr * bi + ki * br
    bmask = (_iota((HALF_STATE, HALF_CH), 0) >> 6) == (_iota((HALF_STATE, HALF_CH), 1) >> 4)
    cmask = (_iota((HALF_CH, HALF_STATE), 0) >> 4) == (_iota((HALF_CH, HALF_STATE), 1) >> 6)
    btr = jnp.where(bmask, jnp.tile(bbr, (1, 8)), 0.0).astype(BF16)
    bti = jnp.where(bmask, jnp.tile(bbi, (1, 8)), 0.0).astype(BF16)
    ctr = jnp.where(cmask, jnp.tile(cr, (1, 8)), 0.0).astype(BF16)
    cti = jnp.where(cmask, jnp.tile(ci, (1, 8)), 0.0).astype(BF16)
    return kr, ki, btr, bti, ctr, cti, bmask, cmask


def _slab(q):
    return slice(q * LANES, (q + 1) * LANES)


def _cmul(ar, ai, br, bi):
    return ar * br - ai * bi, ar * bi + ai * br


def _sub_shift(x, k, up):
    row = _iota(x.shape, 0)
    if up:
        return jnp.where(row >= N_SEG - k, 0.0, pltpu.roll(x, N_SEG - k, 0))
    return jnp.where(row < k, 0.0, pltpu.roll(x, k, 0))


def _scan(xr, xi, ar_row, ai_row, seg, reverse):
    nlog = int(math.log2(seg))
    assert (1 << nlog) == seg
    for q0 in range(0, N_SLAB, 4):
        qs = list(range(q0, q0 + 4))
        aq = [(jnp.broadcast_to(ar_row[:, _slab(q)], (N_SEG, LANES)),
               jnp.broadcast_to(ai_row[:, _slab(q)], (N_SEG, LANES))) for q in qs]
        zero = jnp.zeros((N_SEG, LANES), F32)

        def local(jj, carry, qs=qs, aq=aq):
            j = seg - 1 - jj if reverse else jj
            out = []
            for n, q in enumerate(qs):
                rows = pl.ds(j, N_SEG, stride=seg)
                pr, pi = _cmul(aq[n][0], aq[n][1], carry[2 * n], carry[2 * n + 1])
                nr = pr + xr[q, rows, :]
                ni = pi + xi[q, rows, :]
                xr[q, rows, :] = nr
                xi[q, rows, :] = ni
                out += [nr, ni]
            return tuple(out)

        fin = lax.fori_loop(0, seg, local, (zero,) * 8)
        cins = []
        for n in range(4):
            er, ei = fin[2 * n], fin[2 * n + 1]
            pr, pi = aq[n]
            for _ in range(nlog):
                pr, pi = _cmul(pr, pi, pr, pi)
            yr, yi = er, ei
            for k in (1, 2, 4):
                sr, si = _cmul(pr, pi, _sub_shift(yr, k, reverse), _sub_shift(yi, k, reverse))
                yr, yi = yr + sr, yi + si
                pr, pi = _cmul(pr, pi, pr, pi)
            cins.append((_sub_shift(yr, 1, reverse), _sub_shift(yi, 1, reverse)))

        def fix(jj, carry, qs=qs, aq=aq, cins=cins):
            j = seg - 1 - jj if reverse else jj
            out = []
            for n, q in enumerate(qs):
                rows = pl.ds(j, N_SEG, stride=seg)
                pwr, pwi = carry[2 * n], carry[2 * n + 1]
                cr, ci = _cmul(pwr, pwi, cins[n][0], cins[n][1])
                xr[q, rows, :] += cr
                xi[q, rows, :] += ci
                nr, ni = _cmul(pwr, pwi, aq[n][0], aq[n][1])
                out += [nr, ni]
            return tuple(out)

        lax.fori_loop(0, seg, fix, tuple(v for pair in aq for v in pair))


def _s5_forward_states(u, btr, bti, ar_row, ai_row, xr, xi, seg):
    ub = u.astype(BF16)
    for q in range(N_SLAB):
        xr[q] = _dot_nt(ub, btr[_slab(q), :])
        xi[q] = _dot_nt(ub, bti[_slab(q), :])
    _scan(xr, xi, ar_row, ai_row, seg, False)


def _s5_readout(u, xr, xi, ctr, cti, d):
    y = d * u
    for q in range(N_SLAB):
        y = y + _dot_nt(xr[q], ctr[:, _slab(q)]) - _dot_nt(xi[q], cti[:, _slab(q)])
    return y


def _s5_param_specs():
    return [pl.BlockSpec((3, HALF_STATE), lambda i: (0, i)), pl.BlockSpec((HALF_STATE, 3), lambda i: (i, 0)),
            pl.BlockSpec((HALF_STATE, 16), lambda i: (i, 0)), pl.BlockSpec((HALF_STATE, 16), lambda i: (i, 0)),
            pl.BlockSpec((HALF_CH, 64), lambda i: (i, 0)), pl.BlockSpec((HALF_CH, 64), lambda i: (i, 0)),
            pl.BlockSpec((1, HALF_CH), lambda i: (0, i))]


def _s5_core_fwd(z, sp):
    s = z.shape[0]
    seg = s // N_SEG

    def body(u_ref, rowp, colp, br, bi, cr, ci, d_ref, y_ref, xr, xi):
        ar, ai, _, _ = _s5_disc(rowp[0:1, :], rowp[1:2, :], rowp[2:3, :])
        _, _, btr, bti, ctr, cti, _, _ = _s5_mats(colp[...], br[...], bi[...], cr[...], ci[...])
        u = u_ref[...]
        _s5_forward_states(u, btr, bti, ar, ai, xr, xi, seg)
        y_ref[...] = _s5_readout(u, xr, xi, ctr, cti, d_ref[...])

    return pl.pallas_call(
        body, name="s5_core_fwd", grid=(2,),
        in_specs=[pl.BlockSpec((s, HALF_CH), lambda i: (0, 12 + i))] + _s5_param_specs(),
        out_specs=pl.BlockSpec((s, HALF_CH), lambda i: (0, i)),
        out_shape=jax.ShapeDtypeStruct((s, W_GRP), F32),
        scratch_shapes=[pltpu.VMEM((N_SLAB, s, LANES), F32)] * 2,
        compiler_params=_cp(dimension_semantics=("parallel",)),
    )(z, *sp)


def _s5_glu_fwd(y, gw, gb):
    s = y.shape[0]
    tm = _tm(s)

    def body(y_ref, gw_ref, gb_ref, o_ref):
        yg, _ = _gelu(y_ref[...])
        o_ref[...] = yg * jax.nn.sigmoid(_dot(yg, gw_ref[...]) + gb_ref[...])

    blk = pl.BlockSpec((tm, W_GRP), lambda i: (i, 0))
    return pl.pallas_call(
        body, name="s5_glu_fwd", grid=(s // tm,),
        in_specs=[blk, _full((W_GRP, W_GRP)), _full((1, W_GRP))], out_specs=blk,
        out_shape=jax.ShapeDtypeStruct((s, W_GRP), F32),
        compiler_params=_cp(dimension_semantics=("parallel",)),
    )(y, gw, gb)


def _s5_glu_bwd(y, dyd, gw, gb):
    s = y.shape[0]
    tm = _tm(s)

    def body(y_ref, dyd_ref, gw_ref, gb_ref, dy_ref, dgw_ref, dgb_ref):
        i = pl.program_id(0)

        @pl.when(i == 0)
        def _():
            dgw_ref[...] = jnp.zeros_like(dgw_ref)
            dgb_ref[...] = jnp.zeros_like(dgb_ref)

        y, gw, dyd = y_ref[...], gw_ref[...], dyd_ref[...]
        yg, t = _gelu(y)
        gate = jax.nn.sigmoid(_dot(yg, gw) + gb_ref[...])
        dlin = dyd * yg * gate * (1.0 - gate)
        dgw_ref[...] += _dot_tn(yg, dlin)
        dgb_ref[...] += jnp.sum(dlin, axis=0, keepdims=True)
        dy_ref[...] = (dyd * gate + _dot_nt(dlin, gw)) * _gelu_grad(y, t)

    blk = pl.BlockSpec((tm, W_GRP), lambda i: (i, 0))
    return pl.pallas_call(
        body, name="s5_glu_bwd", grid=(s // tm,),
        in_specs=[blk, blk, _full((W_GRP, W_GRP)), _full((1, W_GRP))],
        out_specs=[blk, _full((W_GRP, W_GRP)), _full((1, W_GRP))],
        out_shape=[jax.ShapeDtypeStruct((s, W_GRP), F32), jax.ShapeDtypeStruct((W_GRP, W_GRP), F32),
                   jax.ShapeDtypeStruct((1, W_GRP), F32)],
        compiler_params=_cp(dimension_semantics=("arbitrary",)),
    )(y, dyd, gw, gb)


def _s5_core_bwd(z, dy, sp):
    s = z.shape[0]
    seg = s // N_SEG

    def body(u_ref, dy_ref, rowp, colp, br_ref, bi_ref, cr_ref, ci_ref, d_ref,
             du_ref, dbr_ref, dbi_ref, dcr_ref, dci_ref, dd_ref, da_ref, dk_ref,
             xr, xi, gr, gi):
        ar, ai, _, _ = _s5_disc(rowp[0:1, :], rowp[1:2, :], rowp[2:3, :])
        br, bi = br_ref[...], bi_ref[...]
        kr, ki, btr, bti, ctr, cti, bmask, cmask = _s5_mats(colp[...], br, bi, cr_ref[...], ci_ref[...])
        u = u_ref[...]
        d = d_ref[...]
        _s5_forward_states(u, btr, bti, ar, ai, xr, xi, seg)

        dy = dy_ref[...]
        dd_ref[...] = jnp.sum(dy * u, axis=0, keepdims=True)
        du = d * dy
        dyb = dy.astype(BF16)
        dctr, dcti = [], []
        for q in range(N_SLAB):
            gr[q] = jnp.dot(dyb, ctr[:, _slab(q)], preferred_element_type=F32)
            gi[q] = -jnp.dot(dyb, cti[:, _slab(q)], preferred_element_type=F32)
            dctr.append(_dot_tn(dyb, xr[q]))
            dcti.append(-_dot_tn(dyb, xi[q]))
        selp = ((_iota((HALF_STATE, 64), 0) & 63) == _iota((HALF_STATE, 64), 1)).astype(F32)
        dcr_ref[...] = _dot_hi(jnp.where(cmask, jnp.concatenate(dctr, axis=1), 0.0), selp)
        dci_ref[...] = _dot_hi(jnp.where(cmask, jnp.concatenate(dcti, axis=1), 0.0), selp)

        _scan(gr, gi, ar, -ai, seg, True)

        dar, dai = [], []
        for q in range(N_SLAB):
            def acc_step(j, carry, q=q):
                rows, prev = pl.ds(j, N_SEG, stride=seg), pl.ds(j - 1, N_SEG, stride=seg)
                g_r, g_i, p_r, p_i = gr[q, rows, :], gi[q, rows, :], xr[q, prev, :], xi[q, prev, :]
                return carry[0] + g_r * p_r + g_i * p_i, carry[1] - g_r * p_i + g_i * p_r
            first, last = pl.ds(0, N_SEG, stride=seg), pl.ds(seg - 1, N_SEG, stride=seg)
            p_r, p_i = _sub_shift(xr[q, last, :], 1, False), _sub_shift(xi[q, last, :], 1, False)
            g_r, g_i = gr[q, first, :], gi[q, first, :]
            s_r, s_i = lax.fori_loop(1, seg, acc_step, (g_r * p_r + g_i * p_i, -g_r * p_i + g_i * p_r))
            dar.append(jnp.sum(s_r, axis=0, keepdims=True))
            dai.append(jnp.sum(s_i, axis=0, keepdims=True))
        da_ref[...] = jnp.zeros_like(da_ref)
        da_ref[0:1, :] = jnp.concatenate(dar, axis=1)
        da_ref[1:2, :] = jnp.concatenate(dai, axis=1)

        ub = u.astype(BF16)
        dbtr, dbti = [], []
        for q in range(N_SLAB):
            g_r, g_i = gr[q].astype(BF16), gi[q].astype(BF16)
            du = du + jnp.dot(g_r, btr[_slab(q), :], preferred_element_type=F32) \
                + jnp.dot(g_i, bti[_slab(q), :], preferred_element_type=F32)
            dbtr.append(_dot_tn(g_r, ub))
            dbti.append(_dot_tn(g_i, ub))
        du_ref[...] = du
        selc = ((_iota((HALF_CH, 16), 0) & 15) == _iota((HALF_CH, 16), 1)).astype(F32)
        dbbr = _dot_hi(jnp.where(bmask, jnp.concatenate(dbtr, axis=0), 0.0), selc)
        dbbi = _dot_hi(jnp.where(bmask, jnp.concatenate(dbti, axis=0), 0.0), selc)
        dbr_ref[...] = kr * dbbr + ki * dbbi
        dbi_ref[...] = kr * dbbi - ki * dbbr
        dk_ref[:, 0:1] = jnp.sum(dbbr * br + dbbi * bi, axis=1, keepdims=True)
        dk_ref[:, 1:2] = jnp.sum(dbbi * br - dbbr * bi, axis=1, keepdims=True)

    half = pl.BlockSpec((s, HALF_CH), lambda i: (0, i))
    return pl.pallas_call(
        body, name="s5_core_bwd", grid=(2,),
        in_specs=[pl.BlockSpec((s, HALF_CH), lambda i: (0, 12 + i)), half] + _s5_param_specs(),
        out_specs=[half, pl.BlockSpec((HALF_STATE, 16), lambda i: (i, 0)), pl.BlockSpec((HALF_STATE, 16), lambda i: (i, 0)),
                   pl.BlockSpec((HALF_CH, 64), lambda i: (i, 0)), pl.BlockSpec((HALF_CH, 64), lambda i: (i, 0)),
                   pl.BlockSpec((1, HALF_CH), lambda i: (0, i)), pl.BlockSpec((8, HALF_STATE), lambda i: (0, i)),
                   pl.BlockSpec((HALF_STATE, 2), lambda i: (i, 0))],
        out_shape=[jax.ShapeDtypeStruct((s, W_GRP), F32), jax.ShapeDtypeStruct((N_STATE, 16), F32),
                   jax.ShapeDtypeStruct((N_STATE, 16), F32), jax.ShapeDtypeStruct((W_GRP, 64), F32),
                   jax.ShapeDtypeStruct((W_GRP, 64), F32), jax.ShapeDtypeStruct((1, W_GRP), F32),
                   jax.ShapeDtypeStruct((8, N_STATE), F32), jax.ShapeDtypeStruct((N_STATE, 2), F32)],
        scratch_shapes=[pltpu.VMEM((N_SLAB, s, LANES), F32)] * 4,
        compiler_params=_cp(dimension_semantics=("parallel",)),
    )(z, dy, *sp)


def _s5_param_bwd(lre, lim, ldt, da_r, da_i, dk_r, dk_i):
    n = lre.shape[0]

    def body(lre_ref, lim_ref, ldt_ref, dar_ref, dai_ref, dkr_ref, dki_ref, o_re, o_im, o_dt):
        lre, lim, ldt = lre_ref[...], lim_ref[...], ldt_ref[...]
        dt = jnp.exp(ldt)
        ar, ai, kr, ki = _s5_disc(lre, lim, ldt)
        mag = jnp.exp(lre * dt)
        den = lre * lre + lim * lim
        dkr, dki = dkr_ref[...], dki_ref[...]
        nr, ni = ar - 1.0, ai
        d_ar = dar_ref[...] + (dkr * lre - dki * lim) / den
        d_ai = dai_ref[...] + (dkr * lim + dki * lre) / den
        kk = (kr * dkr + ki * dki) * 2.0 / den
        d_lre = (dkr * nr + dki * ni) / den - kk * lre
        d_lim = (dkr * ni - dki * nr) / den - kk * lim
        d_mag = (d_ar * ar + d_ai * ai) / mag
        d_ang = d_ai * ar - d_ar * ai
        o_re[...] = d_lre + d_mag * mag * dt
        o_im[...] = d_lim + d_ang * dt
        o_dt[...] = jnp.sum((d_mag * mag * lre + d_ang * lim) * dt, axis=1, keepdims=True)

    return pl.pallas_call(
        body, name="s5_param_bwd",
        out_shape=[jax.ShapeDtypeStruct((n, 64), F32), jax.ShapeDtypeStruct((n, 64), F32),
                   jax.ShapeDtypeStruct((n, 1), F32)],
    )(lre, lim, ldt, da_r, da_i, dk_r, dk_i)


def _loss_head(x, fg, target):
    s, d = x.shape
    tm = _tm(s)

    def body(x_ref, fg_ref, t_ref, loss_ref, dx_ref, dfg_ref):
        i = pl.program_id(0)

        @pl.when(i == 0)
        def _():
            loss_ref[...] = jnp.zeros_like(loss_ref)
            dfg_ref[...] = jnp.zeros_like(dfg_ref)

        xv, g = x_ref[...], fg_ref[...]
        r = lax.rsqrt(jnp.mean(xv * xv, axis=-1, keepdims=True) + EPS)
        xh = xv * r
        err = xh * g - t_ref[...]
        loss_ref[...] += 0.5 * jnp.sum(jnp.mean(err * err, axis=-1, keepdims=True), axis=0, keepdims=True)
        dy = err * (1.0 / d)
        dfg_ref[...] += jnp.sum(dy * xh, axis=0, keepdims=True)
        dxh = dy * g
        dx_ref[...] = r * (dxh - xh * jnp.mean(dxh * xh, axis=-1, keepdims=True))

    row = pl.BlockSpec((tm, d), lambda i: (i, 0))
    return pl.pallas_call(
        body, name="loss_head", grid=(s // tm,),
        in_specs=[row, _full((1, d)), row], out_specs=[_full((1, 1)), row, _full((1, d))],
        out_shape=[jax.ShapeDtypeStruct((1, 1), F32), jax.ShapeDtypeStruct((s, d), F32),
                   jax.ShapeDtypeStruct((1, d), F32)],
        compiler_params=_cp(dimension_semantics=("arbitrary",)),
    )(x, fg, target)


ADA_TN = 384


def _cond_fwd(cact, ada_w, ada_b_loc):
    nl, d, n = ada_w.shape

    def body(c_ref, w_ref, b_ref, o_ref):
        o_ref[...] = _dot(c_ref[...], w_ref[...]) + b_ref[...]

    return pl.pallas_call(
        body, name="cond_fwd", grid=(nl, n // ADA_TN),
        in_specs=[_full((N_DEV, d)), pl.BlockSpec((None, d, ADA_TN), lambda l, j: (l, 0, j)),
                  pl.BlockSpec((None, 1, ADA_TN), lambda l, j: (l, 0, j))],
        out_specs=pl.BlockSpec((None, N_DEV, ADA_TN), lambda l, j: (l, 0, j)),
        out_shape=jax.ShapeDtypeStruct((nl, N_DEV, n), F32),
        compiler_params=_cp(dimension_semantics=("parallel", "parallel")),
    )(cact, ada_w, ada_b_loc)


ELEMENTWISE_BLOCK_BYTES = 1 << 20


def _row_tile(r, c, itemsize=4):
    best = None
    for t in range(8, r + 1, 8):
        if r % t == 0 and t * c * itemsize <= ELEMENTWISE_BLOCK_BYTES:
            best = t
    return best if best is not None else r


def _adamw_math(w, g, m, v):
    m = ADAM_B1 * m + (1.0 - ADAM_B1) * g
    v = ADAM_B2 * v + (1.0 - ADAM_B2) * (g * g)
    m_hat = m / (1.0 - ADAM_B1 ** ADAM_STEP)
    v_hat = v / (1.0 - ADAM_B2 ** ADAM_STEP)
    delta = -ADAM_LR * (m_hat / (jnp.sqrt(v_hat) + ADAM_EPS) + ADAM_WD * w)
    return delta, m, v


def _ada_w_update(cact, dcond_loc, w, m, v):
    nl, d, n = w.shape

    def body(c_ref, dc_ref, w_ref, m_ref, v_ref, g_out, d_out, m_out, v_out):
        g = _dot_tn(c_ref[...], dc_ref[...])
        g_out[...] = g
        d_out[...], m_out[...], v_out[...] = _adamw_math(w_ref[...], g, m_ref[...], v_ref[...])

    blk = pl.BlockSpec((None, d, ADA_TN), lambda l, j: (l, 0, j))
    return pl.pallas_call(
        body, name="ada_w_update", grid=(nl, n // ADA_TN),
        in_specs=[_full((N_DEV, d)), pl.BlockSpec((None, N_DEV, ADA_TN), lambda l, j: (l, 0, j)), blk, blk, blk],
        out_specs=[blk] * 4, out_shape=[jax.ShapeDtypeStruct((nl, d, n), F32)] * 4,
        compiler_params=_cp(dimension_semantics=("parallel", "parallel")),
    )(cact, dcond_loc, w, m, v)


def _adamw(w, g, m, v, name):
    b, r, c = w.shape
    tr = _row_tile(r, c)

    def body(w_ref, g_ref, m_ref, v_ref, d_out, m_out, v_out):
        d_out[...], m_out[...], v_out[...] = _adamw_math(w_ref[...], g_ref[...], m_ref[...], v_ref[...])

    blk = pl.BlockSpec((None, tr, c), lambda i, j: (i, j, 0))
    return pl.pallas_call(
        body, name=name, grid=(b, r // tr), in_specs=[blk] * 4, out_specs=[blk] * 3,
        out_shape=[jax.ShapeDtypeStruct((b, r, c), F32)] * 3,
        compiler_params=_cp(dimension_semantics=("parallel", "parallel")),
    )(w, g, m, v)


def _place():
    x, y, c = lax.axis_index("x"), lax.axis_index("y"), lax.axis_index("c")
    chips = [(1 - x, y), (x, 1 - y), (1 - x, 1 - y)]
    return x, y, c, chips


def _remote(src, dst, send_sem, recv_sem, to):
    return pltpu.make_async_remote_copy(src_ref=src, dst_ref=dst, send_sem=send_sem, recv_sem=recv_sem,
                                        device_id=to, device_id_type=MESH_ID)


def _sems(n):
    return [pltpu.SemaphoreType.DMA((n,)), pltpu.SemaphoreType.DMA((n,))]


def _all_gather8(v, name):
    r, cdim = v.shape

    def body(x_ref, out_ref, send_sems, recv_sems, local_sem):
        x, y, c, chips = _place()
        sibling = (x, y, 1 - c)

        def slot(px, py, pc):
            return out_ref.at[4 * px + 2 * py + pc]

        mine = pltpu.make_async_copy(x_ref, slot(x, y, c), local_sem)
        mine.start()
        first = [_remote(x_ref, slot(x, y, c), send_sems.at[0], recv_sems.at[0], sibling)]
        first += [_remote(x_ref, slot(x, y, c), send_sems.at[1 + j], recv_sems.at[1 + j], (*chip, c))
                  for j, chip in enumerate(chips)]
        for cp in first:
            cp.start()
        passed = []
        for j, chip in enumerate(chips):
            blk = slot(*chip, c)
            _remote(blk, blk, send_sems.at[1 + j], recv_sems.at[1 + j], (x, y, c)).wait_recv()
            fw = _remote(blk, blk, send_sems.at[4 + j], recv_sems.at[4 + j], sibling)
            fw.start()
            passed.append(fw)
        blk = slot(x, y, 1 - c)
        _remote(blk, blk, send_sems.at[0], recv_sems.at[0], (x, y, c)).wait_recv()
        for j, chip in enumerate(chips):
            blk = slot(*chip, 1 - c)
            _remote(blk, blk, send_sems.at[4 + j], recv_sems.at[4 + j], (x, y, c)).wait_recv()
        for cp in first + passed:
            cp.wait_send()
        mine.wait()

    return pl.pallas_call(
        body, name=name, out_shape=jax.ShapeDtypeStruct((N_DEV, r, cdim), v.dtype),
        in_specs=[ANY], out_specs=ANY,
        scratch_shapes=_sems(7) + [pltpu.SemaphoreType.DMA(())],
    )(v)


def _weight_gather(srcs):
    nt = len(srcs)
    kinds = [k for _, k in srcs]
    shapes = []
    for a, kind in srcs:
        nl, _, r, cc = a.shape
        shapes.append((nl, 2, 2, r, 2 * cc) if kind == "col" else (nl, N_CHIP, 2, r, cc))

    def body(*refs):
        src, dst = refs[:nt], refs[nt:2 * nt]
        send_sems, recv_sems, local_sems = refs[2 * nt:]
        x, y, c, chips = _place()
        sibling = (x, y, 1 - c)
        kme = 2 * x + y

        def block(t, k, h):
            if kinds[t] == "col":
                ncol = src[t].shape[3]
                return dst[t].at[:, k // 2, h, :, pl.ds(pl.multiple_of((k % 2) * ncol, LANES), ncol)]
            return dst[t].at[:, k, h]

        local = []
        for t in range(nt):
            for h in range(2):
                cp = pltpu.make_async_copy(src[t].at[:, h], block(t, kme, h), local_sems.at[2 * t + h])
                cp.start()
                local.append(cp)
        sends = []
        for t in range(nt):
            for j, chip in enumerate(chips):
                cp = _remote(src[t].at[:, c], block(t, kme, c), send_sems.at[6 * t + j], recv_sems.at[6 * t + j],
                             (*chip, c))
                cp.start()
                sends.append(cp)
        for t in range(nt):
            for j, chip in enumerate(chips):
                blk = block(t, 2 * chip[0] + chip[1], c)
                _remote(blk, blk, send_sems.at[6 * t + j], recv_sems.at[6 * t + j], (x, y, c)).wait_recv()
                fw = _remote(blk, blk, send_sems.at[6 * t + 3 + j], recv_sems.at[6 * t + 3 + j], sibling)
                fw.start()
                sends.append(fw)
        for t in range(nt):
            for j, chip in enumerate(chips):
                blk = block(t, 2 * chip[0] + chip[1], 1 - c)
                _remote(blk, blk, send_sems.at[6 * t + 3 + j], recv_sems.at[6 * t + 3 + j], (x, y, c)).wait_recv()
        for cp in sends:
            cp.wait_send()
        for cp in local:
            cp.wait()

    return pl.pallas_call(
        body, name="weight_gather",
        out_shape=[jax.ShapeDtypeStruct(s, BF16) for s in shapes],
        in_specs=[ANY] * nt, out_specs=[ANY] * nt,
        scratch_shapes=_sems(6 * nt) + [pltpu.SemaphoreType.DMA((2 * nt,))],
    )(*[a for a, _ in srcs])


def _sibling_exchange(views):
    nt = len(views)

    def body(*refs):
        src, land = refs[:nt], refs[nt:2 * nt]
        send_sems, recv_sems = refs[2 * nt:]
        x, y, c, _ = _place()
        cps = [_remote(src[t].at[:, 1 - c], land[t], send_sems.at[t], recv_sems.at[t], (x, y, 1 - c))
               for t in range(nt)]
        for cp in cps:
            cp.start()
        for cp in cps:
            cp.wait()

    return pl.pallas_call(
        body, name="grad_sibling_exchange",
        out_shape=[jax.ShapeDtypeStruct((v.shape[0],) + v.shape[2:], v.dtype) for v in views],
        in_specs=[ANY] * nt, out_specs=[ANY] * nt, scratch_shapes=_sems(nt),
    )(*views)


def _chip_scatter(parts, kinds):
    nt = len(parts)

    def body(*refs):
        src, land = refs[:nt], refs[nt:2 * nt]
        send_sems, recv_sems = refs[2 * nt:]
        x, y, c, chips = _place()
        cps = []
        for t in range(nt):
            for j, chip in enumerate(chips):
                k = 2 * chip[0] + chip[1]
                if kinds[t] == "col":
                    ncol = land[t].shape[2]
                    win = src[t].at[k // 2, :, pl.ds(pl.multiple_of((k % 2) * ncol, LANES), ncol)]
                else:
                    win = src[t].at[k]
                cps.append(_remote(win, land[t].at[j], send_sems.at[3 * t + j], recv_sems.at[3 * t + j], (*chip, c)))
        for cp in cps:
            cp.start()
        for cp in cps:
            cp.wait()

    shapes = []
    for p, kind in zip(parts, kinds):
        shapes.append((3, p.shape[1], p.shape[2] // 2) if kind == "col" else (3,) + p.shape[1:])
    return pl.pallas_call(
        body, name="grad_chip_scatter",
        out_shape=[jax.ShapeDtypeStruct(s, BF16) for s in shapes],
        in_specs=[ANY] * nt, out_specs=[ANY] * nt, scratch_shapes=_sems(3 * nt),
    )(*parts)


def _sibling_share(halves):
    nt = len(halves)

    def body(*refs):
        src, dst = refs[:nt], refs[nt:2 * nt]
        send_sems, recv_sems, local_sems = refs[2 * nt:]
        x, y, c, _ = _place()
        local, cps = [], []
        for t in range(nt):
            lc = pltpu.make_async_copy(src[t], dst[t].at[c], local_sems.at[t])
            lc.start()
            local.append(lc)
            cp = _remote(src[t], dst[t].at[c], send_sems.at[t], recv_sems.at[t], (x, y, 1 - c))
            cp.start()
            cps.append(cp)
        for t in range(nt):
            other = dst[t].at[1 - c]
            _remote(other, other, send_sems.at[t], recv_sems.at[t], (x, y, c)).wait_recv()
        for cp in cps:
            cp.wait_send()
        for lc in local:
            lc.wait()

    return pl.pallas_call(
        body, name="grad_sibling_share",
        out_shape=[jax.ShapeDtypeStruct((2,) + h.shape, h.dtype) for h in halves],
        in_specs=[ANY] * nt, out_specs=[ANY] * nt,
        scratch_shapes=_sems(nt) + [pltpu.SemaphoreType.DMA((nt,))],
    )(*halves)


def _pair_sum(view, land, cidx):
    b, _, r, cc = view.shape

    def body(c_ref, own_ref, land_ref, o_ref):
        o_ref[...] = (own_ref[...].astype(F32) + land_ref[...].astype(F32)).astype(BF16)

    return pl.pallas_call(
        body, name="grad_pair_sum", out_shape=jax.ShapeDtypeStruct((b, r, cc), BF16),
        grid_spec=pltpu.PrefetchScalarGridSpec(
            num_scalar_prefetch=1, grid=(b,),
            in_specs=[pl.BlockSpec((None, None, r, cc), lambda i, c: (i, c[0], 0, 0)),
                      pl.BlockSpec((None, r, cc), lambda i, c: (i, 0, 0))],
            out_specs=pl.BlockSpec((None, r, cc), lambda i, c: (i, 0, 0))),
        compiler_params=_cp(dimension_semantics=("parallel",)),
    )(cidx, view, land)


def _chip_sum(part, land, kind, kidx):
    _, r, cc = land.shape

    def body(k_ref, own_ref, land_ref, o_ref):
        acc = own_ref[...].astype(F32)
        for j in range(3):
            acc = acc + land_ref[j].astype(F32)
        o_ref[...] = acc

    if kind == "col":
        own = pl.BlockSpec((None, r, cc), lambda i, k: (k[0] // 2, 0, k[0] % 2))
    else:
        own = pl.BlockSpec((None, r, cc), lambda i, k: (k[0], 0, 0))
    return pl.pallas_call(
        body, name="grad_chip_sum", out_shape=jax.ShapeDtypeStruct((r, cc), F32),
        grid_spec=pltpu.PrefetchScalarGridSpec(
            num_scalar_prefetch=1, grid=(1,),
            in_specs=[own, pl.BlockSpec((3, r, cc), lambda i, k: (0, 0, 0))],
            out_specs=pl.BlockSpec((r, cc), lambda i, k: (0, 0))),
        compiler_params=_cp(dimension_semantics=("arbitrary",)),
    )(kidx, part, land)


def _sum8(g):
    _, r, cc = g.shape
    tr = r
    for cand in (512, 256, 128, 64, 32, 16, 8):
        if r % cand == 0:
            tr = cand
            break

    def body(g_ref, o_ref):
        acc = g_ref[0]
        for d in range(1, N_DEV):
            acc = acc + g_ref[d]
        o_ref[...] = acc

    return pl.pallas_call(
        body, name="small_grad_sum", grid=(r // tr,),
        in_specs=[pl.BlockSpec((N_DEV, tr, cc), lambda i: (0, i, 0))],
        out_specs=pl.BlockSpec((tr, cc), lambda i: (i, 0)),
        out_shape=jax.ShapeDtypeStruct((r, cc), F32),
        compiler_params=_cp(dimension_semantics=("parallel",)),
    )(g)


def _silu_rows(c):
    def body(c_ref, o_ref):
        v = c_ref[...]
        o_ref[...] = v * jax.nn.sigmoid(v)

    return pl.pallas_call(body, name="cond_silu", out_shape=jax.ShapeDtypeStruct(c.shape, F32))(c)


def _pack(arrays):
    flat = jnp.concatenate([a.reshape(-1) for a in arrays])
    n = flat.shape[0]
    pad = (-n) % (8 * LANES)
    return jnp.pad(flat, (0, pad)).reshape(-1, LANES)


def _unpack(packed, shapes):
    flat = packed.reshape(-1)
    out, off = [], 0
    for s in shapes:
        n = math.prod(s)
        out.append(flat[off:off + n].reshape(s))
        off += n
    return out


def _reduce_big_grads(grads, kinds, cidx, kidx):
    views = []
    for g, kind in zip(grads, kinds):
        if kind == "col":
            views.append(g.reshape(2, 2, g.shape[1] // 2, g.shape[2]))
        else:
            views.append(g.reshape(N_CHIP, 2, g.shape[0] // (2 * N_CHIP), g.shape[1]))
    lands = _sibling_exchange(views)
    parts = [_pair_sum(v, ld, cidx) for v, ld in zip(views, lands)]
    lands = _chip_scatter(parts, kinds)
    halves = [_chip_sum(p, ld, kind, kidx) for p, ld, kind in zip(parts, lands, kinds)]
    fulls = _sibling_share(halves)
    return [f.reshape(2 * f.shape[1], f.shape[2]) for f in fulls]


SMALL_NAMES = ["ada_b", "norm1_g", "norm2_g", "sgu_w", "sgu_b", "pool_w", "pool_scale", "conv_w", "s5_lambda_re",
               "s5_lambda_im", "s5_b_re", "s5_b_im", "s5_c_re", "s5_c_im", "s5_d", "s5_log_dt", "s5_glu_w", "s5_glu_b",
               "mix_norm_g", "norm3_g", "final_norm_g"]
BIG_NAMES = ["ffn1_w_in", "ffn1_w_out", "w_mix_in", "w_mix_out", "ffn2_w_in", "ffn2_w_out"]
BIG_KINDS = ["col", "row", "row", "row", "col", "row"]
WEIGHT_ORDER = ["ada_w", "ada_b", "norm1_g", "ffn1_w_in", "ffn1_w_out", "norm2_g", "w_mix_in", "sgu_w", "sgu_b", "pool_w",
                "pool_scale", "conv_w", "s5_lambda_re", "s5_lambda_im", "s5_b_re", "s5_b_im", "s5_c_re", "s5_c_im", "s5_d",
                "s5_log_dt", "s5_glu_w", "s5_glu_b", "mix_norm_g", "w_mix_out", "norm3_g", "ffn2_w_in", "ffn2_w_out",
                "final_norm_g"]


def _local_step(x, target, cond, big, p):
    nl, d = DEPTH, x.shape[1]
    row = lambda a: a.reshape(1, -1)
    saved = []
    for l in range(nl):
        wi1, wo1, wmit, wmo, wi2, wo2 = big[l]
        mod1, mod2, mod3 = cond[l, 0:3], cond[l, 3:6], cond[l, 6:9]
        lre, lim = p["s5_lambda_re"][l].reshape(-1), p["s5_lambda_im"][l].reshape(-1)
        ldt = jnp.repeat(p["s5_log_dt"][l], 64)
        rowp = jnp.stack([lre, lim, ldt])
        sp = (rowp, rowp.T, p["s5_b_re"][l].reshape(N_STATE, 16), p["s5_b_im"][l].reshape(N_STATE, 16),
              p["s5_c_re"][l].reshape(W_GRP, 64), p["s5_c_im"][l].reshape(W_GRP, 64), row(p["s5_d"][l]))
        glu = (p["s5_glu_w"][l], row(p["s5_glu_b"][l]))
        bias_full = jnp.repeat(p["sgu_b"][l].T, 64, axis=1)
        pw2 = p["pool_w"][l].reshape(W_GRP, 64)
        x1, h1, a1, b1, o1 = _ffn_fwd(x, mod1, row(p["norm1_g"][l]), wi1, wo1)
        z, h2 = _mix_in_fwd(x1, mod2, row(p["norm2_g"][l]), wmit)
        ya = _sgu_fwd(z, p["sgu_w"][l], bias_full)
        yb, yc = _poolconv_fwd(z, pw2, row(p["pool_scale"][l]), p["conv_w"][l])
        ypre = _s5_core_fwd(z, sp)
        yd = _s5_glu_fwd(ypre, *glu)
        ys = (ya, yb, yc, yd)
        x2, m = _mix_out_fwd(ys, row(p["mix_norm_g"][l]), wmo, x1, mod2[2:3])
        x3, h3, a3, b3, o3 = _ffn_fwd(x2, mod3, row(p["norm3_g"][l]), wi2, wo2)
        saved.append((x, x1, x2, h1, a1, b1, o1, z, h2, ys, m, h3, a3, b3, o3, sp, bias_full, pw2, ypre, glu))
        x = x3

    loss, dx, dfg = _loss_head(x, row(p["final_norm_g"]), target)

    big_grads = [None] * nl
    sg = {n: [None] * nl for n in SMALL_NAMES if n not in ("ada_b", "final_norm_g")}
    dcond = [None] * nl
    s5_da, s5_dk = [None] * nl, [None] * nl
    for l in reversed(range(nl)):
        wi1, wo1, wmit, wmo, wi2, wo2 = big[l]
        mod1, mod2, mod3 = cond[l, 0:3], cond[l, 3:6], cond[l, 6:9]
        x0, x1, x2, h1, a1, b1, o1, z, h2, ys, m, h3, a3, b3, o3, sp, bias_full, pw2, ypre, glu = saved[l]
        do, dgate3 = _gate_bwd(dx, o3, mod3[2:3], 0.5)
        dza, dzb, dwi2, dwo2 = _ffn_bwd_main(do, h3, a3, b3, wo2)
        dx, rows3 = _ffn_bwd_in(dza, dzb, wi2, x2, dx, mod3, row(p["norm3_g"][l]))
        outs = _mix_out_bwd(dx, m, mod2[2:3], ys, row(p["mix_norm_g"][l]), wmo)
        dys, dgate2, dmng, dwmo = outs[0:4], outs[4], outs[5], outs[6]
        dza_, dsw, dsb = _sgu_bwd(z, dys[0], p["sgu_w"][l], bias_full)
        dzb_, dzc_, dwbd, dps, dcw = _poolconv_bwd(z, dys[1], dys[2], pw2, row(p["pool_scale"][l]), p["conv_w"][l])
        dypre, dgw, dgb = _s5_glu_bwd(ypre, dys[3], *glu)
        dzd_, dbr, dbi, dcr, dci, dd, da, dk = _s5_core_bwd(z, dypre, sp)
        dx, rows2, dwmit = _mix_in_bwd((dza_, dzb_, dzc_, dzd_), h2, wmit, x1, dx, mod2, row(p["norm2_g"][l]))
        do, dgate1 = _gate_bwd(dx, o1, mod1[2:3], 0.5)
        dza, dzb, dwi1, dwo1 = _ffn_bwd_main(do, h1, a1, b1, wo1)
        dx, rows1 = _ffn_bwd_in(dza, dzb, wi1, x0, dx, mod1, row(p["norm1_g"][l]))

        big_grads[l] = [dwi1, dwo1, dwmit, dwmo, dwi2, dwo2]
        dcond[l] = jnp.concatenate([rows1[0:2], dgate1, rows2[0:2], dgate2, rows3[0:2], dgate3], axis=0)
        sg["norm1_g"][l], sg["norm2_g"][l], sg["norm3_g"][l] = rows1[2], rows2[2], rows3[2]
        sg["mix_norm_g"][l] = dmng[0]
        sg["sgu_w"][l] = dsw
        sg["sgu_b"][l] = dsb[:, 0:4].T
        g4 = dwbd.reshape(4, 64, 4, 64)
        sg["pool_w"][l] = jnp.stack([g4[k, :, k, :] for k in range(4)])
        sg["pool_scale"][l] = dps[0]
        sg["conv_w"][l] = dcw[0:3]
        sg["s5_b_re"][l], sg["s5_b_im"][l] = dbr.reshape(16, 64, 16), dbi.reshape(16, 64, 16)
        sg["s5_c_re"][l], sg["s5_c_im"][l] = dcr.reshape(16, 16, 64), dci.reshape(16, 16, 64)
        sg["s5_d"][l] = dd[0]
        sg["s5_glu_w"][l], sg["s5_glu_b"][l] = dgw, dgb[0]
        s5_da[l], s5_dk[l] = da, dk

    n16 = nl * 16
    dlre, dlim, dldt = _s5_param_bwd(
        p["s5_lambda_re"].reshape(n16, 64), p["s5_lambda_im"].reshape(n16, 64),
        jnp.repeat(p["s5_log_dt"].reshape(n16, 1), 64, axis=1),
        jnp.stack([a[0] for a in s5_da]).reshape(n16, 64), jnp.stack([a[1] for a in s5_da]).reshape(n16, 64),
        jnp.stack([k[:, 0] for k in s5_dk]).reshape(n16, 64), jnp.stack([k[:, 1] for k in s5_dk]).reshape(n16, 64))
    small = {n: jnp.stack(v) for n, v in sg.items() if v[0] is not None}
    small["s5_lambda_re"] = dlre.reshape(nl, 16, 64)
    small["s5_lambda_im"] = dlim.reshape(nl, 16, 64)
    small["s5_log_dt"] = dldt.reshape(nl, 16)
    small["final_norm_g"] = dfg[0]
    return loss, dx, big_grads, small, jnp.stack(dcond)


def kernel(x, c, ada_w, ada_b, norm1_g, ffn1_w_in, ffn1_w_out, norm2_g, w_mix_in, sgu_w, sgu_b, pool_w, pool_scale, conv_w, s5_lambda_re, s5_lambda_im, s5_b_re, s5_b_im, s5_c_re, s5_c_im, s5_d, s5_log_dt, s5_glu_w, s5_glu_b, mix_norm_g, w_mix_out, norm3_g, ffn2_w_in, ffn2_w_out, final_norm_g, loss_target, m_ada_w, m_ada_b, m_norm1_g, m_ffn1_w_in, m_ffn1_w_out, m_norm2_g, m_w_mix_in, m_sgu_w, m_sgu_b, m_pool_w, m_pool_scale, m_conv_w, m_s5_lambda_re, m_s5_lambda_im, m_s5_b_re, m_s5_b_im, m_s5_c_re, m_s5_c_im, m_s5_d, m_s5_log_dt, m_s5_glu_w, m_s5_glu_b, m_mix_norm_g, m_w_mix_out, m_norm3_g, m_ffn2_w_in, m_ffn2_w_out, m_final_norm_g, v_ada_w, v_ada_b, v_norm1_g, v_ffn1_w_in, v_ffn1_w_out, v_norm2_g, v_w_mix_in, v_sgu_w, v_sgu_b, v_pool_w, v_pool_scale, v_conv_w, v_s5_lambda_re, v_s5_lambda_im, v_s5_b_re, v_s5_b_im, v_s5_c_re, v_s5_c_im, v_s5_d, v_s5_log_dt, v_s5_glu_w, v_s5_glu_b, v_mix_norm_g, v_w_mix_out, v_norm3_g, v_ffn2_w_in, v_ffn2_w_out, v_final_norm_g):
    args = dict(locals())
    w = {n: args[n] for n in WEIGHT_ORDER}
    mom = {n: args["m_" + n] for n in WEIGHT_ORDER}
    vel = {n: args["v_" + n] for n in WEIGHT_ORDER}
    nl, d = DEPTH, x.shape[-1]
    s = x.shape[1]
    px, py, pc = lax.axis_index("x"), lax.axis_index("y"), lax.axis_index("c")
    kme = 2 * px + py
    me = 2 * kme + pc
    cidx = jnp.reshape(pc, (1,)).astype(jnp.int32)
    kidx = jnp.reshape(kme, (1,)).astype(jnp.int32)

    cact = _silu_rows(c)
    n_conv, n_glu = conv_w.size, s5_glu_w.size
    pre = _pack([cact, conv_w, s5_glu_w])
    pre_all = _all_gather8(pre, "gather_prelude").reshape(N_DEV, -1)
    cact_all = pre_all[:, :d]
    conv_full = jnp.concatenate(
        [pre_all[2 * k, d:d + n_conv].reshape(conv_w.shape) for k in range(N_CHIP)], axis=2)
    glu_full = jnp.concatenate(
        [pre_all[2 * k, d + n_conv:d + n_conv + n_glu].reshape(s5_glu_w.shape) for k in range(N_CHIP)], axis=1)

    n_ada = ada_w.shape[2]
    ada_b_loc = lax.dynamic_slice_in_dim(ada_b, kme * n_ada, n_ada, axis=1).reshape(nl, 1, n_ada)
    cond_part = _cond_fwd(cact_all, ada_w, ada_b_loc)
    cond_all = _all_gather8(cond_part.reshape(nl * N_DEV, n_ada), "gather_cond").reshape(N_DEV, nl, N_DEV, n_ada)
    cond_me = jnp.concatenate(
        [lax.dynamic_index_in_dim(cond_all[2 * k], me, axis=1, keepdims=False) for k in range(N_CHIP)], axis=1)
    cond = cond_me.reshape(nl, 9, d)

    def halves(a):
        return a.reshape(a.shape[0], 2, a.shape[1] // 2, a.shape[2])

    srcs = [(halves(ffn1_w_in.astype(BF16)), "col"), (halves(ffn1_w_out.astype(BF16)), "row"),
            (halves(jnp.swapaxes(w_mix_in, 1, 2).astype(BF16)), "row"), (halves(w_mix_out.astype(BF16)), "row"),
            (halves(ffn2_w_in.astype(BF16)), "col"), (halves(ffn2_w_out.astype(BF16)), "row")]
    gathered = _weight_gather(srcs)
    full = []
    for g, (_, kind) in zip(gathered, srcs):
        if kind == "col":
            full.append(g.reshape(nl, 2, g.shape[2] * g.shape[3], g.shape[4]))
        else:
            full.append(g.reshape(nl, g.shape[1] * g.shape[2] * g.shape[3], g.shape[4]))
    big = [[f[l] for f in full] for l in range(nl)]

    p = {n: w[n] for n in SMALL_NAMES}
    p["conv_w"], p["s5_glu_w"] = conv_full, glu_full
    loss, dx, big_grads, small, dcond = _local_step(x[0], loss_target[0], cond, big, p)

    small_order = [n for n in SMALL_NAMES if n != "ada_b"]
    packed = _pack([dcond] + [small[n] for n in small_order])
    gathered_small = _all_gather8(packed, "gather_small_grads")
    total = _sum8(gathered_small)
    shapes = [dcond.shape] + [small[n].shape for n in small_order]
    tot = dict(zip(["ada_b"] + small_order, _unpack(total, shapes)))
    grads = {n: tot[n] for n in SMALL_NAMES}
    grads["ada_b"] = tot["ada_b"].reshape(nl, 9 * d)
    grads["conv_w"] = lax.dynamic_slice_in_dim(tot["conv_w"], kme * conv_w.shape[2], conv_w.shape[2], axis=2)
    grads["s5_glu_w"] = lax.dynamic_slice_in_dim(tot["s5_glu_w"], kme * s5_glu_w.shape[1], s5_glu_w.shape[1], axis=1)

    dcond_all = gathered_small.reshape(N_DEV, -1)[:, :dcond.size].reshape(N_DEV, nl, 9 * d)
    dcond_loc = jnp.swapaxes(lax.dynamic_slice_in_dim(dcond_all, kme * n_ada, n_ada, axis=2), 0, 1)
    g_ada, d_ada, m_ada, v_ada = _ada_w_update(cact_all, dcond_loc, ada_w, m_ada_w, v_ada_w)

    reduced = [_reduce_big_grads(big_grads[l], BIG_KINDS, cidx, kidx) for l in range(nl)]
    for t, n in enumerate(BIG_NAMES):
        g = jnp.stack([reduced[l][t] for l in range(nl)])
        grads[n] = jnp.swapaxes(g, 1, 2) if n == "w_mix_in" else g

    delta, new_m, new_v = {}, {}, {}
    grads["ada_w"], delta["ada_w"], new_m["ada_w"], new_v["ada_w"] = g_ada, d_ada, m_ada, v_ada
    for n in BIG_NAMES:
        delta[n], new_m[n], new_v[n] = _adamw(w[n], grads[n], mom[n], vel[n], "adamw_" + n)
    sw, sg_, sm, sv = (_pack([t[n] for n in SMALL_NAMES])[None] for t in (w, grads, mom, vel))
    outs = _adamw(sw, sg_, sm, sv, "adamw_small")
    sshapes = [w[n].shape for n in SMALL_NAMES]
    for res, o in zip((delta, new_m, new_v), outs):
        res.update(dict(zip(SMALL_NAMES, _unpack(o[0], sshapes))))

    loss_total = lax.psum(loss[0, 0], ("x", "y", "c"))
    return (loss_total, dx[None], *[grads[n] for n in WEIGHT_ORDER], *[delta[n] for n in WEIGHT_ORDER],
            *[new_m[n] for n in WEIGHT_ORDER], *[new_v[n] for n in WEIGHT_ORDER])
```

```python
import functools
import math

import jax
import jax.numpy as jnp
from jax import lax
from jax.experimental import pallas as pl
from jax.experimental.pallas import tpu as pltpu

F32, BF16 = jnp.float32, jnp.bfloat16
EPS = 1e-6
DEPTH = 4
N_DEV = 8
N_CHIP = 4
W_GRP = 256
CHUNK = 128
N_SEG = 8
LANES = 128
FFN_TF = 256
VMEM_LIMIT = 56 * 1024 * 1024
ADAM_LR, ADAM_B1, ADAM_B2, ADAM_EPS, ADAM_WD, ADAM_STEP = 0.001, 0.9, 0.999, 1e-08, 0.01, 10
MESH_ID = pl.DeviceIdType.MESH
HI = lax.Precision.HIGHEST
ANY = pl.BlockSpec(memory_space=pl.ANY)


def _cp(**kw):
    return pltpu.CompilerParams(vmem_limit_bytes=VMEM_LIMIT, **kw)


def _dot(a, b):
    return jnp.dot(a.astype(BF16), b.astype(BF16), preferred_element_type=F32)


def _dot_nt(a, b):
    return lax.dot_general(a.astype(BF16), b.astype(BF16), (((1,), (1,)), ((), ())), preferred_element_type=F32)


def _dot_tn(a, b):
    return lax.dot_general(a.astype(BF16), b.astype(BF16), (((0,), (0,)), ((), ())), preferred_element_type=F32)


def _dot_hi(a, b):
    return jnp.dot(a, b, preferred_element_type=F32, precision=HI)


def _gelu(x):
    k = 0.7978845608028654
    t = jnp.tanh(k * (x + 0.044715 * x * x * x))
    return 0.5 * x * (1.0 + t), t


def _gelu_grad(x, t):
    k = 0.7978845608028654
    return 0.5 * (1.0 + t) + 0.5 * x * (1.0 - t * t) * k * (1.0 + 3.0 * 0.044715 * x * x)


def _iota(shape, axis):
    return lax.broadcasted_iota(jnp.int32, shape, axis)


def _full(shape):
    nd = len(shape)
    return pl.BlockSpec(shape, lambda *_: (0,) * nd)


def _norm_mod(xv, g, shift, scale):
    r = lax.rsqrt(jnp.mean(xv * xv, axis=-1, keepdims=True) + EPS)
    return (xv * r * g) * (1.0 + scale) + shift


def _norm_mod_bwd(xv, g, scale, dh):
    r = lax.rsqrt(jnp.mean(xv * xv, axis=-1, keepdims=True) + EPS)
    xh = xv * r
    n = xh * g
    dsh = jnp.sum(dh, axis=0, keepdims=True)
    dsc = jnp.sum(dh * n, axis=0, keepdims=True)
    dn = dh * (1.0 + scale)
    dg = jnp.sum(dn * xh, axis=0, keepdims=True)
    dxh = dn * g
    dx = r * (dxh - xh * jnp.mean(dxh * xh, axis=-1, keepdims=True))
    return dx, dsh, dsc, dg


def _tm(s):
    return min(s, 1024)


def _ffn_fwd(x, mod, g, wi, wo):
    s, d = x.shape
    f = wo.shape[0]
    tf, tm = FFN_TF, _tm(s)
    nf, nt = f // tf, s // tm

    def body(x_ref, mod_ref, g_ref, wa_ref, wb_ref, wo_ref, xn_ref, h_ref, a_ref, b_ref, o_ref):
        j = pl.program_id(1)

        @pl.when(j == 0)
        def _():
            hh = _norm_mod(x_ref[...], g_ref[...], mod_ref[0:1, :], mod_ref[1:2, :])
            h_ref[...] = hh.astype(BF16)
            o_ref[...] = jnp.zeros_like(o_ref)

        h = h_ref[...]
        a = jnp.dot(h, wa_ref[...], preferred_element_type=F32)
        b = jnp.dot(h, wb_ref[...], preferred_element_type=F32)
        a_ref[...] = a.astype(BF16)
        b_ref[...] = b.astype(BF16)
        u = (a * jax.nn.sigmoid(a)) * b
        o_ref[...] += jnp.dot(u.astype(BF16), wo_ref[...], preferred_element_type=F32)

        @pl.when(j == nf - 1)
        def _():
            xn_ref[...] = x_ref[...] + 0.5 * mod_ref[2:3, :] * o_ref[...]

    row = pl.BlockSpec((tm, d), lambda i, j: (i, 0))
    chunk = pl.BlockSpec((tm, tf), lambda i, j: (i, j))
    return pl.pallas_call(
        body, name="ffn_fwd", grid=(nt, nf),
        in_specs=[row, _full((3, d)), _full((1, d)),
                  pl.BlockSpec((None, d, tf), lambda i, j: (0, 0, j)),
                  pl.BlockSpec((None, d, tf), lambda i, j: (1, 0, j)),
                  pl.BlockSpec((tf, d), lambda i, j: (j, 0))],
        out_specs=[row, row, chunk, chunk, row],
        out_shape=[jax.ShapeDtypeStruct((s, d), F32), jax.ShapeDtypeStruct((s, d), BF16),
                   jax.ShapeDtypeStruct((s, f), BF16), jax.ShapeDtypeStruct((s, f), BF16),
                   jax.ShapeDtypeStruct((s, d), F32)],
        compiler_params=_cp(dimension_semantics=("parallel", "arbitrary")),
    )(x, mod, g, wi, wi, wo)


def _gate_bwd(dxo, o, gate, half):
    s, d = dxo.shape
    tm = _tm(s)
    nt = s // tm

    def body(dx_ref, o_ref, gate_ref, do_ref, dg_ref):
        i = pl.program_id(0)

        @pl.when(i == 0)
        def _():
            dg_ref[...] = jnp.zeros_like(dg_ref)

        dx = dx_ref[...]
        do_ref[...] = (half * gate_ref[...] * dx).astype(BF16)
        dg_ref[...] += half * jnp.sum(o_ref[...] * dx, axis=0, keepdims=True)

    row = pl.BlockSpec((tm, d), lambda i: (i, 0))
    return pl.pallas_call(
        body, name="gate_bwd", grid=(nt,),
        in_specs=[row, row, _full((1, d))], out_specs=[row, _full((1, d))],
        out_shape=[jax.ShapeDtypeStruct((s, d), BF16), jax.ShapeDtypeStruct((1, d), F32)],
        compiler_params=_cp(dimension_semantics=("arbitrary",)),
    )(dxo, o, gate)


def _ffn_bwd_main(do, h, a, b, wo):
    s, d = do.shape
    f = wo.shape[0]
    tf, tm = FFN_TF, _tm(s)
    nf, nt = f // tf, s // tm

    def body(do_ref, h_ref, a_ref, b_ref, wo_ref, dza_ref, dzb_ref, dwi_ref, dwo_ref, acc_a, acc_b, acc_o):
        i = pl.program_id(1)

        @pl.when(i == 0)
        def _():
            acc_a[...] = jnp.zeros_like(acc_a)
            acc_b[...] = jnp.zeros_like(acc_b)
            acc_o[...] = jnp.zeros_like(acc_o)

        dov = do_ref[...]
        hv = h_ref[...]
        du = lax.dot_general(dov, wo_ref[...], (((1,), (1,)), ((), ())), preferred_element_type=F32)
        av = a_ref[...].astype(F32)
        bv = b_ref[...].astype(F32)
        sa = jax.nn.sigmoid(av)
        si = av * sa
        u = (si * bv).astype(BF16)
        da = (du * bv * (sa * (1.0 + av * (1.0 - sa)))).astype(BF16)
        db = (du * si).astype(BF16)
        dza_ref[...] = da
        dzb_ref[...] = db
        acc_o[...] += _dot_tn(u, dov)
        acc_a[...] += _dot_tn(hv, da)
        acc_b[...] += _dot_tn(hv, db)

        @pl.when(i == nt - 1)
        def _():
            dwi_ref[0] = acc_a[...].astype(BF16)
            dwi_ref[1] = acc_b[...].astype(BF16)
            dwo_ref[...] = acc_o[...].astype(BF16)

    row = pl.BlockSpec((tm, d), lambda j, i: (i, 0))
    chunk = pl.BlockSpec((tm, tf), lambda j, i: (i, j))
    return pl.pallas_call(
        body, name="ffn_bwd_main", grid=(nf, nt),
        in_specs=[row, row, chunk, chunk, pl.BlockSpec((tf, d), lambda j, i: (j, 0))],
        out_specs=[chunk, chunk, pl.BlockSpec((2, d, tf), lambda j, i: (0, 0, j)),
                   pl.BlockSpec((tf, d), lambda j, i: (j, 0))],
        out_shape=[jax.ShapeDtypeStruct((s, f), BF16), jax.ShapeDtypeStruct((s, f), BF16),
                   jax.ShapeDtypeStruct((2, d, f), BF16), jax.ShapeDtypeStruct((f, d), BF16)],
        scratch_shapes=[pltpu.VMEM((d, tf), F32), pltpu.VMEM((d, tf), F32), pltpu.VMEM((tf, d), F32)],
        compiler_params=_cp(dimension_semantics=("parallel", "arbitrary")),
    )(do, h, a, b, wo)


def _ffn_bwd_in(dza, dzb, wi, x, dxo, mod, g):
    s, d = x.shape
    f = dza.shape[1]
    tf, tm = FFN_TF, _tm(s)
    nf, nt = f // tf, s // tm

    def body(dza_ref, dzb_ref, wa_ref, wb_ref, x_ref, dxo_ref, mod_ref, g_ref, dx_ref, rows_ref, acc):
        i, j = pl.program_id(0), pl.program_id(1)

        @pl.when(jnp.logical_and(i == 0, j == 0))
        def _():
            rows_ref[...] = jnp.zeros_like(rows_ref)

        @pl.when(j == 0)
        def _():
            acc[...] = jnp.zeros_like(acc)

        acc[...] += (lax.dot_general(dza_ref[...], wa_ref[...], (((1,), (1,)), ((), ())), preferred_element_type=F32)
                     + lax.dot_general(dzb_ref[...], wb_ref[...], (((1,), (1,)), ((), ())), preferred_element_type=F32))

        @pl.when(j == nf - 1)
        def _():
            dx, dsh, dsc, dg = _norm_mod_bwd(x_ref[...], g_ref[...], mod_ref[1:2, :], acc[...])
            dx_ref[...] = dx + dxo_ref[...]
            rows_ref[0:1, :] += dsh
            rows_ref[1:2, :] += dsc
            rows_ref[2:3, :] += dg

    row = pl.BlockSpec((tm, d), lambda i, j: (i, 0))
    chunk = pl.BlockSpec((tm, tf), lambda i, j: (i, j))
    return pl.pallas_call(
        body, name="ffn_bwd_in", grid=(nt, nf),
        in_specs=[chunk, chunk,
                  pl.BlockSpec((None, d, tf), lambda i, j: (0, 0, j)),
                  pl.BlockSpec((None, d, tf), lambda i, j: (1, 0, j)),
                  row, row, _full((3, d)), _full((1, d))],
        out_specs=[row, _full((8, d))],
        out_shape=[jax.ShapeDtypeStruct((s, d), F32), jax.ShapeDtypeStruct((8, d), F32)],
        scratch_shapes=[pltpu.VMEM((tm, d), F32)],
        compiler_params=_cp(dimension_semantics=("arbitrary", "arbitrary")),
    )(dza, dzb, wi, wi, x, dxo, mod, g)


def _mix_in_fwd(x, mod, g, wmit):
    s, d = x.shape
    p = wmit.shape[0]
    tm = _tm(s)

    def body(x_ref, mod_ref, g_ref, w_ref, z_ref, h_ref):
        hh = _norm_mod(x_ref[...], g_ref[...], mod_ref[0:1, :], mod_ref[1:2, :]).astype(BF16)
        h_ref[...] = hh
        z_ref[...] = lax.dot_general(hh, w_ref[...], (((1,), (1,)), ((), ())), preferred_element_type=F32)

    row = pl.BlockSpec((tm, d), lambda i: (i, 0))
    return pl.pallas_call(
        body, name="mix_in_fwd", grid=(s // tm,),
        in_specs=[row, _full((3, d)), _full((1, d)), _full((p, d))],
        out_specs=[pl.BlockSpec((tm, p), lambda i: (i, 0)), row],
        out_shape=[jax.ShapeDtypeStruct((s, p), F32), jax.ShapeDtypeStruct((s, d), BF16)],
        compiler_params=_cp(dimension_semantics=("parallel",)),
    )(x, mod, g, wmit)


def _mix_in_bwd(dzs, h, wmit, x, dxo, mod, g):
    s, d = x.shape
    p = wmit.shape[0]
    tm = min(s, 512)
    nt = s // tm

    def body(za_ref, zb_ref, zc_ref, zd_ref, h_ref, w_ref, x_ref, dxo_ref, mod_ref, g_ref,
             dx_ref, rows_ref, dw_ref, acc):
        i = pl.program_id(0)

        @pl.when(i == 0)
        def _():
            rows_ref[...] = jnp.zeros_like(rows_ref)
            acc[...] = jnp.zeros_like(acc)

        dz = jnp.concatenate([za_ref[...], zb_ref[...], zc_ref[...], zd_ref[...]], axis=1).astype(BF16)
        acc[...] += _dot_tn(dz, h_ref[...])
        dh = jnp.dot(dz, w_ref[...], preferred_element_type=F32)
        dx, dsh, dsc, dg = _norm_mod_bwd(x_ref[...], g_ref[...], mod_ref[1:2, :], dh)
        dx_ref[...] = dx + dxo_ref[...]
        rows_ref[0:1, :] += dsh
        rows_ref[1:2, :] += dsc
        rows_ref[2:3, :] += dg

        @pl.when(i == nt - 1)
        def _():
            dw_ref[...] = acc[...].astype(BF16)

    row = pl.BlockSpec((tm, d), lambda i: (i, 0))
    zspecs = [pl.BlockSpec((tm, z.shape[1]), lambda i: (i, 0)) for z in dzs]
    return pl.pallas_call(
        body, name="mix_in_bwd", grid=(nt,),
        in_specs=zspecs + [row, _full((p, d)), row, row, _full((3, d)), _full((1, d))],
        out_specs=[row, _full((8, d)), _full((p, d))],
        out_shape=[jax.ShapeDtypeStruct((s, d), F32), jax.ShapeDtypeStruct((8, d), F32),
                   jax.ShapeDtypeStruct((p, d), BF16)],
        scratch_shapes=[pltpu.VMEM((p, d), F32)],
        compiler_params=_cp(dimension_semantics=("arbitrary",)),
    )(*dzs, h, wmit, x, dxo, mod, g)


def _group_norm(ys, mng):
    outs, hats, rs = [], [], []
    for k, y in enumerate(ys):
        r = lax.rsqrt(jnp.mean(y * y, axis=-1, keepdims=True) + EPS)
        yh = y * r
        hats.append(yh)
        rs.append(r)
        outs.append(yh * mng[:, k * W_GRP:(k + 1) * W_GRP])
    return jnp.concatenate(outs, axis=1), hats, rs


def _mix_out_fwd(ys, mng, wmo, x, gate):
    s, d = x.shape
    tm = _tm(s)

    def body(ya, yb, yc, yd, mng_ref, w_ref, x_ref, gate_ref, xn_ref, m_ref):
        yn, _, _ = _group_norm([ya[...], yb[...], yc[...], yd[...]], mng_ref[...])
        m = jnp.dot(yn.astype(BF16), w_ref[...], preferred_element_type=F32)
        m_ref[...] = m
        xn_ref[...] = x_ref[...] + gate_ref[...] * m

    row = pl.BlockSpec((tm, d), lambda i: (i, 0))
    grp = pl.BlockSpec((tm, W_GRP), lambda i: (i, 0))
    return pl.pallas_call(
        body, name="mix_out_fwd", grid=(s // tm,),
        in_specs=[grp, grp, grp, grp, _full((1, d)), _full((d, d)), row, _full((1, d))],
        out_specs=[row, row],
        out_shape=[jax.ShapeDtypeStruct((s, d), F32), jax.ShapeDtypeStruct((s, d), F32)],
        compiler_params=_cp(dimension_semantics=("parallel",)),
    )(*ys, mng, wmo, x, gate)


def _mix_out_bwd(dxo, m, gate, ys, mng, wmo):
    s, d = dxo.shape
    tm = min(s, 512)
    nt = s // tm

    def body(dxo_ref, m_ref, gate_ref, ya, yb, yc, yd, mng_ref, w_ref,
             dya, dyb, dyc, dyd, dgate_ref, dmng_ref, dw_ref, acc):
        i = pl.program_id(0)

        @pl.when(i == 0)
        def _():
            dgate_ref[...] = jnp.zeros_like(dgate_ref)
            dmng_ref[...] = jnp.zeros_like(dmng_ref)
            acc[...] = jnp.zeros_like(acc)

        dxv = dxo_ref[...]
        dgate_ref[...] += jnp.sum(m_ref[...] * dxv, axis=0, keepdims=True)
        dm = (gate_ref[...] * dxv).astype(BF16)
        mng = mng_ref[...]
        yn, hats, rs = _group_norm([ya[...], yb[...], yc[...], yd[...]], mng)
        acc[...] += _dot_tn(yn, dm)
        dyn = lax.dot_general(dm, w_ref[...], (((1,), (1,)), ((), ())), preferred_element_type=F32)
        dmng_parts = []
        for k, (yh, r, out) in enumerate(zip(hats, rs, (dya, dyb, dyc, dyd))):
            dk = dyn[:, k * W_GRP:(k + 1) * W_GRP]
            dmng_parts.append(jnp.sum(dk * yh, axis=0, keepdims=True))
            dyh = dk * mng[:, k * W_GRP:(k + 1) * W_GRP]
            out[...] = r * (dyh - yh * jnp.mean(dyh * yh, axis=-1, keepdims=True))
        dmng_ref[...] += jnp.concatenate(dmng_parts, axis=1)

        @pl.when(i == nt - 1)
        def _():
            dw_ref[...] = acc[...].astype(BF16)

    row = pl.BlockSpec((tm, d), lambda i: (i, 0))
    grp = pl.BlockSpec((tm, W_GRP), lambda i: (i, 0))
    return pl.pallas_call(
        body, name="mix_out_bwd", grid=(nt,),
        in_specs=[row, row, _full((1, d)), grp, grp, grp, grp, _full((1, d)), _full((d, d))],
        out_specs=[grp, grp, grp, grp, _full((1, d)), _full((1, d)), _full((d, d))],
        out_shape=[jax.ShapeDtypeStruct((s, W_GRP), F32)] * 4
        + [jax.ShapeDtypeStruct((1, d), F32), jax.ShapeDtypeStruct((1, d), F32), jax.ShapeDtypeStruct((d, d), BF16)],
        scratch_shapes=[pltpu.VMEM((d, d), F32)],
        compiler_params=_cp(dimension_semantics=("arbitrary",)),
    )(dxo, m, gate, *ys, mng, wmo)


def _sgu_consts():
    r = _iota((W_GRP, W_GRP), 0) >> 6
    c = _iota((W_GRP, W_GRP), 1) >> 6
    avg = jnp.where(r == c, 1.0 / 64.0, 0.0).astype(F32)
    tril = _iota((CHUNK, CHUNK), 0) >= _iota((CHUNK, CHUNK), 1)
    head = _iota((CHUNK, W_GRP), 1) >> 6
    return avg, tril, head


def _sgu_pre(za, avg):
    zg, t = _gelu(za)
    u, v = zg[:, :W_GRP], zg[:, W_GRP:]
    mu = _dot_hi(v, avg)
    vc = v - mu
    r = lax.rsqrt(_dot_hi(vc * vc, avg) + EPS)
    return t, u, vc * r, r


def _sgu_fwd(z, sgu_w, bias_full):
    s = z.shape[0]
    tm = min(s, 512)

    def body(za_ref, w_ref, bias_ref, ya_ref):
        avg, tril, head = _sgu_consts()
        _, u, vn, _ = _sgu_pre(za_ref[...], avg)
        wm = [jnp.where(tril, w_ref[h], 0.0).astype(BF16) for h in range(4)]
        vb = vn.astype(BF16)
        for n in range(tm // CHUNK):
            rows = slice(n * CHUNK, (n + 1) * CHUNK)
            mixed = bias_ref[...]
            for h in range(4):
                mixed = mixed + jnp.where(head == h, jnp.dot(wm[h], vb[rows], preferred_element_type=F32), 0.0)
            ya_ref[rows, :] = u[rows] * mixed

    return pl.pallas_call(
        body, name="sgu_fwd", grid=(s // tm,),
        in_specs=[pl.BlockSpec((tm, 2 * W_GRP), lambda i: (i, 0)), _full((4, CHUNK, CHUNK)), _full((CHUNK, W_GRP))],
        out_specs=pl.BlockSpec((tm, W_GRP), lambda i: (i, 0)),
        out_shape=jax.ShapeDtypeStruct((s, W_GRP), F32),
        compiler_params=_cp(dimension_semantics=("parallel",)),
    )(z, sgu_w, bias_full)


def _sgu_bwd(z, dya, sgu_w, bias_full):
    s = z.shape[0]
    tm = min(s, 512)
    nt = s // tm

    def body(za_ref, dya_ref, w_ref, bias_ref, dza_ref, dw_ref, db_ref, du_s, dvn_s):
        i = pl.program_id(0)

        @pl.when(i == 0)
        def _():
            dw_ref[...] = jnp.zeros_like(dw_ref)
            db_ref[...] = jnp.zeros_like(db_ref)

        avg, tril, head = _sgu_consts()
        za = za_ref[...]
        t, u, vn, r = _sgu_pre(za, avg)
        wm = [jnp.where(tril, w_ref[h], 0.0).astype(BF16) for h in range(4)]
        vb = vn.astype(BF16)
        dya = dya_ref[...]
        dw = [jnp.zeros((CHUNK, CHUNK), F32) for _ in range(4)]
        db = jnp.zeros((CHUNK, W_GRP), F32)
        for n in range(tm // CHUNK):
            rows = slice(n * CHUNK, (n + 1) * CHUNK)
            mixed = bias_ref[...]
            for h in range(4):
                mixed = mixed + jnp.where(head == h, jnp.dot(wm[h], vb[rows], preferred_element_type=F32), 0.0)
            dmix = dya[rows] * u[rows]
            du_s[rows, :] = dya[rows] * mixed
            db = db + dmix
            dmb = dmix.astype(BF16)
            dvn = jnp.zeros((CHUNK, W_GRP), F32)
            for h in range(4):
                dmh = jnp.where(head == h, dmix, 0.0)
                dw[h] = dw[h] + _dot_nt(dmh, vb[rows])
                dvn = dvn + jnp.where(head == h, _dot_tn(wm[h], dmb), 0.0)
            dvn_s[rows, :] = dvn
        for h in range(4):
            dw_ref[h] += jnp.where(tril, dw[h], 0.0)
        sel = ((_iota((W_GRP, CHUNK), 0) >> 6) == _iota((W_GRP, CHUNK), 1)).astype(F32)
        db_ref[...] += _dot_hi(db, sel)
        dvn = dvn_s[...]
        dv = r * (dvn - _dot_hi(dvn, avg) - vn * _dot_hi(dvn * vn, avg))
        dzg = jnp.concatenate([du_s[...], dv], axis=1)
        dza_ref[...] = dzg * _gelu_grad(za, t)

    return pl.pallas_call(
        body, name="sgu_bwd", grid=(nt,),
        in_specs=[pl.BlockSpec((tm, 2 * W_GRP), lambda i: (i, 0)), pl.BlockSpec((tm, W_GRP), lambda i: (i, 0)),
                  _full((4, CHUNK, CHUNK)), _full((CHUNK, W_GRP))],
        out_specs=[pl.BlockSpec((tm, 2 * W_GRP), lambda i: (i, 0)), _full((4, CHUNK, CHUNK)), _full((CHUNK, CHUNK))],
        out_shape=[jax.ShapeDtypeStruct((s, 2 * W_GRP), F32), jax.ShapeDtypeStruct((4, CHUNK, CHUNK), F32),
                   jax.ShapeDtypeStruct((CHUNK, CHUNK), F32)],
        scratch_shapes=[pltpu.VMEM((tm, W_GRP), F32), pltpu.VMEM((tm, W_GRP), F32)],
        compiler_params=_cp(dimension_semantics=("arbitrary",)),
    )(z, dya, sgu_w, bias_full)


def _shift_down(x, k):
    return jnp.where(_iota(x.shape, 0) < k, 0.0, pltpu.roll(x, k, 0))


def _shift_up(x, k):
    n = x.shape[0]
    return jnp.where(_iota(x.shape, 0) >= n - k, 0.0, pltpu.roll(x, n - k, 0))


def _by_pool_group(shape, v2, v4, v8, v16):
    col = _iota(shape, 1)
    return jnp.where(col < 64, v2, jnp.where(col < 128, v4, jnp.where(col < 192, v8, v16)))


def _pool_core(zb, pw2):
    s2 = zb + _shift_down(zb, 1)
    s4 = s2 + _shift_down(s2, 2)
    s8 = s4 + _shift_down(s4, 4)
    s16 = s8 + _shift_down(s8, 8)
    win = _by_pool_group(zb.shape, s2, s4, s8, s16)
    wlen = _by_pool_group(zb.shape, 2.0, 4.0, 8.0, 16.0)
    cnt = jnp.minimum((_iota(zb.shape, 0) + 1).astype(F32), wlen)
    p = win / cnt - zb
    wt = jnp.tile(pw2, (1, 4))
    wbd = jnp.where((_iota(wt.shape, 0) >> 6) == (_iota(wt.shape, 1) >> 6), wt, 0.0).astype(BF16)
    return p, cnt, wbd


def _conv_core(zc, cw):
    bg, cg, xh = zc[:, :W_GRP], zc[:, W_GRP:2 * W_GRP], zc[:, 2 * W_GRP:]
    y = cg * xh
    y1, y2 = _shift_down(y, 1), _shift_down(y, 2)
    out = cw[2:3, :] * y + cw[1:2, :] * y1 + cw[0:1, :] * y2
    return bg, cg, xh, y, y1, y2, out


def _poolconv_fwd(z, pw2, pscale, cw):
    s = z.shape[0]

    def body(zb_ref, zc_ref, pw_ref, ps_ref, cw_ref, yb_ref, yc_ref):
        p, _, wbd = _pool_core(zb_ref[...], pw_ref[...])
        yb_ref[...] = jnp.dot(p.astype(BF16), wbd, preferred_element_type=F32) * ps_ref[...]
        bg, _, _, _, _, _, out = _conv_core(zc_ref[...], cw_ref[...])
        yc_ref[...] = bg * out

    return pl.pallas_call(
        body, name="poolconv_fwd", grid=(1,),
        in_specs=[pl.BlockSpec((s, W_GRP), lambda i: (0, 2)), pl.BlockSpec((s, 3 * W_GRP), lambda i: (0, 1)),
                  _full((W_GRP, 64)), _full((1, W_GRP)), _full((3, W_GRP))],
        out_specs=[_full((s, W_GRP)), _full((s, W_GRP))],
        out_shape=[jax.ShapeDtypeStruct((s, W_GRP), F32)] * 2,
        compiler_params=_cp(dimension_semantics=("arbitrary",)),
    )(z, z, pw2, pscale, cw)


def _poolconv_bwd(z, dyb, dyc, pw2, pscale, cw):
    s = z.shape[0]

    def body(zb_ref, zc_ref, dyb_ref, dyc_ref, pw_ref, ps_ref, cw_ref, dzb_ref, dzc_ref, dw_ref, dps_ref, dcw_ref):
        zb = zb_ref[...]
        p, cnt, wbd = _pool_core(zb, pw_ref[...])
        pb = p.astype(BF16)
        out = jnp.dot(pb, wbd, preferred_element_type=F32)
        dyb = dyb_ref[...]
        dps_ref[...] = jnp.sum(dyb * out, axis=0, keepdims=True)
        dout = (dyb * ps_ref[...]).astype(BF16)
        dw = _dot_tn(pb, dout)
        dw_ref[...] = jnp.where((_iota(dw.shape, 0) >> 6) == (_iota(dw.shape, 1) >> 6), dw, 0.0)
        dp = lax.dot_general(dout, wbd, (((1,), (1,)), ((), ())), preferred_element_type=F32)
        dwin = dp / cnt
        t2 = dwin + _shift_up(dwin, 1)
        t4 = t2 + _shift_up(t2, 2)
        t8 = t4 + _shift_up(t4, 4)
        t16 = t8 + _shift_up(t8, 8)
        dzb_ref[...] = _by_pool_group(zb.shape, t2, t4, t8, t16) - dp

        cw = cw_ref[...]
        bg, cg, xh, y, y1, y2, out = _conv_core(zc_ref[...], cw)
        dyc = dyc_ref[...]
        dout = dyc * bg
        dcw_ref[...] = jnp.zeros_like(dcw_ref)
        dcw_ref[0:1, :] = jnp.sum(dout * y2, axis=0, keepdims=True)
        dcw_ref[1:2, :] = jnp.sum(dout * y1, axis=0, keepdims=True)
        dcw_ref[2:3, :] = jnp.sum(dout * y, axis=0, keepdims=True)
        dy = cw[2:3, :] * dout + cw[1:2, :] * _shift_up(dout, 1) + cw[0:1, :] * _shift_up(dout, 2)
        dzc_ref[...] = jnp.concatenate([dyc * out, dy * xh, dy * cg], axis=1)

    return pl.pallas_call(
        body, name="poolconv_bwd", grid=(1,),
        in_specs=[pl.BlockSpec((s, W_GRP), lambda i: (0, 2)), pl.BlockSpec((s, 3 * W_GRP), lambda i: (0, 1)),
                  _full((s, W_GRP)), _full((s, W_GRP)), _full((W_GRP, 64)), _full((1, W_GRP)), _full((3, W_GRP))],
        out_specs=[_full((s, W_GRP)), _full((s, 3 * W_GRP)), _full((W_GRP, W_GRP)), _full((1, W_GRP)), _full((8, W_GRP))],
        out_shape=[jax.ShapeDtypeStruct((s, W_GRP), F32), jax.ShapeDtypeStruct((s, 3 * W_GRP), F32),
                   jax.ShapeDtypeStruct((W_GRP, W_GRP), F32), jax.ShapeDtypeStruct((1, W_GRP), F32),
                   jax.ShapeDtypeStruct((8, W_GRP), F32)],
        compiler_params=_cp(dimension_semantics=("arbitrary",)),
    )(z, z, dyb, dyc, pw2, pscale, cw)


N_STATE = 1024
HALF_STATE = N_STATE // 2
HALF_CH = W_GRP // 2
N_SLAB = HALF_STATE // LANES


def _s5_disc(lre, lim, ldt):
    dt = jnp.exp(ldt)
    mag = jnp.exp(lre * dt)
    ang = lim * dt
    ar, ai = mag * jnp.cos(ang), mag * jnp.sin(ang)
    nr, ni = ar - 1.0, ai
    den = lre * lre + lim * lim
    kr = (nr * lre + ni * lim) / den
    ki = (ni * lre - nr * lim) / den
    return ar, ai, kr, ki


def _s5_mats(colp, br, bi, cr, ci):
    _, _, kr, ki = _s5_disc(colp[:, 0:1], colp[:, 1:2], colp[:, 2:3])
    bbr = kr * br - ki * bi
    bbi = kr * bi + ki * br
    bmask = (_iota((HALF_STATE, HALF_CH), 0) >> 6) == (_iota((HALF_STATE, HALF_CH), 1) >> 4)
    cmask = (_iota((HALF_CH, HALF_STATE), 0) >> 4) == (_iota((HALF_CH, HALF_STATE), 1) >> 6)
    btr = jnp.where(bmask, jnp.tile(bbr, (1, 8)), 0.0).astype(BF16)
    bti = jnp.where(bmask, jnp.tile(bbi, (1, 8)), 0.0).astype(BF16)
    ctr = jnp.where(cmask, jnp.tile(cr, (1, 8)), 0.0).astype(BF16)
    cti = jnp.where(cmask, jnp.tile(ci, (1, 8)), 0.0).astype(BF16)
    return kr, ki, btr, bti, ctr, cti, bmask, cmask


def _slab(q):
    return slice(q * LANES, (q + 1) * LANES)


def _cmul(ar, ai, br, bi):
    return ar * br - ai * bi, ar * bi + ai * br


def _sub_shift(x, k, up):
    row = _iota(x.shape, 0)
    if up:
        return jnp.where(row >= N_SEG - k, 0.0, pltpu.roll(x, N_SEG - k, 0))
    return jnp.where(row < k, 0.0, pltpu.roll(x, k, 0))


def _scan(xr, xi, ar_row, ai_row, seg, reverse):
    nlog = int(math.log2(seg))
    assert (1 << nlog) == seg
    for q0 in range(0, N_SLAB, 4):
        qs = list(range(q0, q0 + 4))
        aq = [(jnp.broadcast_to(ar_row[:, _slab(q)], (N_SEG, LANES)),
               jnp.broadcast_to(ai_row[:, _slab(q)], (N_SEG, LANES))) for q in qs]
        zero = jnp.zeros((N_SEG, LANES), F32)

        def local(jj, carry, qs=qs, aq=aq):
            j = seg - 1 - jj if reverse else jj
            out = []
            for n, q in enumerate(qs):
                rows = pl.ds(j, N_SEG, stride=seg)
                pr, pi = _cmul(aq[n][0], aq[n][1], carry[2 * n], carry[2 * n + 1])
                nr = pr + xr[q, rows, :]
                ni = pi + xi[q, rows, :]
                xr[q, rows, :] = nr
                xi[q, rows, :] = ni
                out += [nr, ni]
            return tuple(out)

        fin = lax.fori_loop(0, seg, local, (zero,) * 8)
        cins = []
        for n in range(4):
            er, ei = fin[2 * n], fin[2 * n + 1]
            pr, pi = aq[n]
            for _ in range(nlog):
                pr, pi = _cmul(pr, pi, pr, pi)
            yr, yi = er, ei
            for k in (1, 2, 4):
                sr, si = _cmul(pr, pi, _sub_shift(yr, k, reverse), _sub_shift(yi, k, reverse))
                yr, yi = yr + sr, yi + si
                pr, pi = _cmul(pr, pi, pr, pi)
            cins.append((_sub_shift(yr, 1, reverse), _sub_shift(yi, 1, reverse)))

        def fix(jj, carry, qs=qs, aq=aq, cins=cins):
            j = seg - 1 - jj if reverse else jj
            out = []
            for n, q in enumerate(qs):
                rows = pl.ds(j, N_SEG, stride=seg)
                pwr, pwi = carry[2 * n], carry[2 * n + 1]
                cr, ci = _cmul(pwr, pwi, cins[n][0], cins[n][1])
                xr[q, rows, :] += cr
                xi[q, rows, :] += ci
                nr, ni = _cmul(pwr, pwi, aq[n][0], aq[n][1])
                out += [nr, ni]
            return tuple(out)

        lax.fori_loop(0, seg, fix, tuple(v for pair in aq for v in pair))


def _s5_forward_states(u, btr, bti, ar_row, ai_row, xr, xi, seg):
    ub = u.astype(BF16)
    for q in range(N_SLAB):
        xr[q] = _dot_nt(ub, btr[_slab(q), :])
        xi[q] = _dot_nt(ub, bti[_slab(q), :])
    _scan(xr, xi, ar_row, ai_row, seg, False)


def _s5_readout(u, xr, xi, ctr, cti, d):
    y = d * u
    for q in range(N_SLAB):
        y = y + _dot_nt(xr[q], ctr[:, _slab(q)]) - _dot_nt(xi[q], cti[:, _slab(q)])
    return y


def _s5_param_specs():
    return [pl.BlockSpec((3, HALF_STATE), lambda i: (0, i)), pl.BlockSpec((HALF_STATE, 3), lambda i: (i, 0)),
            pl.BlockSpec((HALF_STATE, 16), lambda i: (i, 0)), pl.BlockSpec((HALF_STATE, 16), lambda i: (i, 0)),
            pl.BlockSpec((HALF_CH, 64), lambda i: (i, 0)), pl.BlockSpec((HALF_CH, 64), lambda i: (i, 0)),
            pl.BlockSpec((1, HALF_CH), lambda i: (0, i))]


def _s5_core_fwd(z, sp):
    s = z.shape[0]
    seg = s // N_SEG

    def body(u_ref, rowp, colp, br, bi, cr, ci, d_ref, y_ref, xr, xi):
        ar, ai, _, _ = _s5_disc(rowp[0:1, :], rowp[1:2, :], rowp[2:3, :])
        _, _, btr, bti, ctr, cti, _, _ = _s5_mats(colp[...], br[...], bi[...], cr[...], ci[...])
        u = u_ref[...]
        _s5_forward_states(u, btr, bti, ar, ai, xr, xi, seg)
        y_ref[...] = _s5_readout(u, xr, xi, ctr, cti, d_ref[...])

    return pl.pallas_call(
        body, name="s5_core_fwd", grid=(2,),
        in_specs=[pl.BlockSpec((s, HALF_CH), lambda i: (0, 12 + i))] + _s5_param_specs(),
        out_specs=pl.BlockSpec((s, HALF_CH), lambda i: (0, i)),
        out_shape=jax.ShapeDtypeStruct((s, W_GRP), F32),
        scratch_shapes=[pltpu.VMEM((N_SLAB, s, LANES), F32)] * 2,
        compiler_params=_cp(dimension_semantics=("parallel",)),
    )(z, *sp)


def _s5_glu_fwd(y, gw, gb):
    s = y.shape[0]
    tm = _tm(s)

    def body(y_ref, gw_ref, gb_ref, o_ref):
        yg, _ = _gelu(y_ref[...])
        o_ref[...] = yg * jax.nn.sigmoid(_dot(yg, gw_ref[...]) + gb_ref[...])

    blk = pl.BlockSpec((tm, W_GRP), lambda i: (i, 0))
    return pl.pallas_call(
        body, name="s5_glu_fwd", grid=(s // tm,),
        in_specs=[blk, _full((W_GRP, W_GRP)), _full((1, W_GRP))], out_specs=blk,
        out_shape=jax.ShapeDtypeStruct((s, W_GRP), F32),
        compiler_params=_cp(dimension_semantics=("parallel",)),
    )(y, gw, gb)


def _s5_glu_bwd(y, dyd, gw, gb):
    s = y.shape[0]
    tm = _tm(s)

    def body(y_ref, dyd_ref, gw_ref, gb_ref, dy_ref, dgw_ref, dgb_ref):
        i = pl.program_id(0)

        @pl.when(i == 0)
        def _():
            dgw_ref[...] = jnp.zeros_like(dgw_ref)
            dgb_ref[...] = jnp.zeros_like(dgb_ref)

        y, gw, dyd = y_ref[...], gw_ref[...], dyd_ref[...]
        yg, t = _gelu(y)
        gate = jax.nn.sigmoid(_dot(yg, gw) + gb_ref[...])
        dlin = dyd * yg * gate * (1.0 - gate)
        dgw_ref[...] += _dot_tn(yg, dlin)
        dgb_ref[...] += jnp.sum(dlin, axis=0, keepdims=True)
        dy_ref[...] = (dyd * gate + _dot_nt(dlin, gw)) * _gelu_grad(y, t)

    blk = pl.BlockSpec((tm, W_GRP), lambda i: (i, 0))
    return pl.pallas_call(
        body, name="s5_glu_bwd", grid=(s // tm,),
        in_specs=[blk, blk, _full((W_GRP, W_GRP)), _full((1, W_GRP))],
        out_specs=[blk, _full((W_GRP, W_GRP)), _full((1, W_GRP))],
        out_shape=[jax.ShapeDtypeStruct((s, W_GRP), F32), jax.ShapeDtypeStruct((W_GRP, W_GRP), F32),
                   jax.ShapeDtypeStruct((1, W_GRP), F32)],
        compiler_params=_cp(dimension_semantics=("arbitrary",)),
    )(y, dyd, gw, gb)


def _s5_core_bwd(z, dy, sp):
    s = z.shape[0]
    seg = s // N_SEG

    def body(u_ref, dy_ref, rowp, colp, br_ref, bi_ref, cr_ref, ci_ref, d_ref,
             du_ref, dbr_ref, dbi_ref, dcr_ref, dci_ref, dd_ref, da_ref, dk_ref,
             xr, xi, gr, gi):
        ar, ai, _, _ = _s5_disc(rowp[0:1, :], rowp[1:2, :], rowp[2:3, :])
        br, bi = br_ref[...], bi_ref[...]
        kr, ki, btr, bti, ctr, cti, bmask, cmask = _s5_mats(colp[...], br, bi, cr_ref[...], ci_ref[...])
        u = u_ref[...]
        d = d_ref[...]
        _s5_forward_states(u, btr, bti, ar, ai, xr, xi, seg)

        dy = dy_ref[...]
        dd_ref[...] = jnp.sum(dy * u, axis=0, keepdims=True)
        du = d * dy
        dyb = dy.astype(BF16)
        dctr, dcti = [], []
        for q in range(N_SLAB):
            gr[q] = jnp.dot(dyb, ctr[:, _slab(q)], preferred_element_type=F32)
            gi[q] = -jnp.dot(dyb, cti[:, _slab(q)], preferred_element_type=F32)
            dctr.append(_dot_tn(dyb, xr[q]))
            dcti.append(-_dot_tn(dyb, xi[q]))
        selp = ((_iota((HALF_STATE, 64), 0) & 63) == _iota((HALF_STATE, 64), 1)).astype(F32)
        dcr_ref[...] = _dot_hi(jnp.where(cmask, jnp.concatenate(dctr, axis=1), 0.0), selp)
        dci_ref[...] = _dot_hi(jnp.where(cmask, jnp.concatenate(dcti, axis=1), 0.0), selp)

        _scan(gr, gi, ar, -ai, seg, True)

        dar, dai = [], []
        for q in range(N_SLAB):
            def acc_step(j, carry, q=q):
                rows, prev = pl.ds(j, N_SEG, stride=seg), pl.ds(j - 1, N_SEG, stride=seg)
                g_r, g_i, p_r, p_i = gr[q, rows, :], gi[q, rows, :], xr[q, prev, :], xi[q, prev, :]
                return carry[0] + g_r * p_r + g_i * p_i, carry[1] - g_r * p_i + g_i * p_r
            first, last = pl.ds(0, N_SEG, stride=seg), pl.ds(seg - 1, N_SEG, stride=seg)
            p_r, p_i = _sub_shift(xr[q, last, :], 1, False), _sub_shift(xi[q, last, :], 1, False)
            g_r, g_i = gr[q, first, :], gi[q, first, :]
            s_r, s_i = lax.fori_loop(1, seg, acc_step, (g_r * p_r + g_i * p_i, -g_r * p_i + g_i * p_r))
            dar.append(jnp.sum(s_r, axis=0, keepdims=True))
            dai.append(jnp.sum(s_i, axis=0, keepdims=True))
        da_ref[...] = jnp.zeros_like(da_ref)
        da_ref[0:1, :] = jnp.concatenate(dar, axis=1)
        da_ref[1:2, :] = jnp.concatenate(dai, axis=1)

        ub = u.astype(BF16)
        dbtr, dbti = [], []
        for q in range(N_SLAB):
            g_r, g_i = gr[q].astype(BF16), gi[q].astype(BF16)
            du = du + jnp.dot(g_r, btr[_slab(q), :], preferred_element_type=F32) \
                + jnp.dot(g_i, bti[_slab(q), :], preferred_element_type=F32)
            dbtr.append(_dot_tn(g_r, ub))
            dbti.append(_dot_tn(g_i, ub))
        du_ref[...] = du
        selc = ((_iota((HALF_CH, 16), 0) & 15) == _iota((HALF_CH, 16), 1)).astype(F32)
        dbbr = _dot_hi(jnp.where(bmask, jnp.concatenate(dbtr, axis=0), 0.0), selc)
        dbbi = _dot_hi(jnp.where(bmask, jnp.concatenate(dbti, axis=0), 0.0), selc)
        dbr_ref[...] = kr * dbbr + ki * dbbi
        dbi_ref[...] = kr * dbbi - ki * dbbr
        dk_ref[:, 0:1] = jnp.sum(dbbr * br + dbbi * bi, axis=1, keepdims=True)
        dk_ref[:, 1:2] = jnp.sum(dbbi * br - dbbr * bi, axis=1, keepdims=True)

    half = pl.BlockSpec((s, HALF_CH), lambda i: (0, i))
    return pl.pallas_call(
        body, name="s5_core_bwd", grid=(2,),
        in_specs=[pl.BlockSpec((s, HALF_CH), lambda i: (0, 12 + i)), half] + _s5_param_specs(),
        out_specs=[half, pl.BlockSpec((HALF_STATE, 16), lambda i: (i, 0)), pl.BlockSpec((HALF_STATE, 16), lambda i: (i, 0)),
                   pl.BlockSpec((HALF_CH, 64), lambda i: (i, 0)), pl.BlockSpec((HALF_CH, 64), lambda i: (i, 0)),
                   pl.BlockSpec((1, HALF_CH), lambda i: (0, i)), pl.BlockSpec((8, HALF_STATE), lambda i: (0, i)),
                   pl.BlockSpec((HALF_STATE, 2), lambda i: (i, 0))],
        out_shape=[jax.ShapeDtypeStruct((s, W_GRP), F32), jax.ShapeDtypeStruct((N_STATE, 16), F32),
                   jax.ShapeDtypeStruct((N_STATE, 16), F32), jax.ShapeDtypeStruct((W_GRP, 64), F32),
                   jax.ShapeDtypeStruct((W_GRP, 64), F32), jax.ShapeDtypeStruct((1, W_GRP), F32),
                   jax.ShapeDtypeStruct((8, N_STATE), F32), jax.ShapeDtypeStruct((N_STATE, 2), F32)],
        scratch_shapes=[pltpu.VMEM((N_SLAB, s, LANES), F32)] * 4,
        compiler_params=_cp(dimension_semantics=("parallel",)),
    )(z, dy, *sp)


def _s5_param_bwd(lre, lim, ldt, da_r, da_i, dk_r, dk_i):
    n = lre.shape[0]

    def body(lre_ref, lim_ref, ldt_ref, dar_ref, dai_ref, dkr_ref, dki_ref, o_re, o_im, o_dt):
        lre, lim, ldt = lre_ref[...], lim_ref[...], ldt_ref[...]
        dt = jnp.exp(ldt)
        ar, ai, kr, ki = _s5_disc(lre, lim, ldt)
        mag = jnp.exp(lre * dt)
        den = lre * lre + lim * lim
        dkr, dki = dkr_ref[...], dki_ref[...]
        nr, ni = ar - 1.0, ai
        d_ar = dar_ref[...] + (dkr * lre - dki * lim) / den
        d_ai = dai_ref[...] + (dkr * lim + dki * lre) / den
        kk = (kr * dkr + ki * dki) * 2.0 / den
        d_lre = (dkr * nr + dki * ni) / den - kk * lre
        d_lim = (dkr * ni - dki * nr) / den - kk * lim
        d_mag = (d_ar * ar + d_ai * ai) / mag
        d_ang = d_ai * ar - d_ar * ai
        o_re[...] = d_lre + d_mag * mag * dt
        o_im[...] = d_lim + d_ang * dt
        o_dt[...] = jnp.sum((d_mag * mag * lre + d_ang * lim) * dt, axis=1, keepdims=True)

    return pl.pallas_call(
        body, name="s5_param_bwd",
        out_shape=[jax.ShapeDtypeStruct((n, 64), F32), jax.ShapeDtypeStruct((n, 64), F32),
                   jax.ShapeDtypeStruct((n, 1), F32)],
    )(lre, lim, ldt, da_r, da_i, dk_r, dk_i)


def _loss_head(x, fg, target):
    s, d = x.shape
    tm = _tm(s)

    def body(x_ref, fg_ref, t_ref, loss_ref, dx_ref, dfg_ref):
        i = pl.program_id(0)

        @pl.when(i == 0)
        def _():
            loss_ref[...] = jnp.zeros_like(loss_ref)
            dfg_ref[...] = jnp.zeros_like(dfg_ref)

        xv, g = x_ref[...], fg_ref[...]
        r = lax.rsqrt(jnp.mean(xv * xv, axis=-1, keepdims=True) + EPS)
        xh = xv * r
        err = xh * g - t_ref[...]
        loss_ref[...] += 0.5 * jnp.sum(jnp.mean(err * err, axis=-1, keepdims=True), axis=0, keepdims=True)
        dy = err * (1.0 / d)
        dfg_ref[...] += jnp.sum(dy * xh, axis=0, keepdims=True)
        dxh = dy * g
        dx_ref[...] = r * (dxh - xh * jnp.mean(dxh * xh, axis=-1, keepdims=True))

    row = pl.BlockSpec((tm, d), lambda i: (i, 0))
    return pl.pallas_call(
        body, name="loss_head", grid=(s // tm,),
        in_specs=[row, _full((1, d)), row], out_specs=[_full((1, 1)), row, _full((1, d))],
        out_shape=[jax.ShapeDtypeStruct((1, 1), F32), jax.ShapeDtypeStruct((s, d), F32),
                   jax.ShapeDtypeStruct((1, d), F32)],
        compiler_params=_cp(dimension_semantics=("arbitrary",)),
    )(x, fg, target)


ADA_TN = 384


def _cond_fwd(cact, ada_w, ada_b_loc):
    nl, d, n = ada_w.shape

    def body(c_ref, w_ref, b_ref, o_ref):
        o_ref[...] = _dot(c_ref[...], w_ref[...]) + b_ref[...]

    return pl.pallas_call(
        body, name="cond_fwd", grid=(nl, n // ADA_TN),
        in_specs=[_full((N_DEV, d)), pl.BlockSpec((None, d, ADA_TN), lambda l, j: (l, 0, j)),
                  pl.BlockSpec((None, 1, ADA_TN), lambda l, j: (l, 0, j))],
        out_specs=pl.BlockSpec((None, N_DEV, ADA_TN), lambda l, j: (l, 0, j)),
        out_shape=jax.ShapeDtypeStruct((nl, N_DEV, n), F32),
        compiler_params=_cp(dimension_semantics=("parallel", "parallel")),
    )(cact, ada_w, ada_b_loc)


ELEMENTWISE_BLOCK_BYTES = 1 << 20


def _row_tile(r, c, itemsize=4):
    best = None
    for t in range(8, r + 1, 8):
        if r % t == 0 and t * c * itemsize <= ELEMENTWISE_BLOCK_BYTES:
            best = t
    return best if best is not None else r


def _adamw_math(w, g, m, v):
    m = ADAM_B1 * m + (1.0 - ADAM_B1) * g
    v = ADAM_B2 * v + (1.0 - ADAM_B2) * (g * g)
    m_hat = m / (1.0 - ADAM_B1 ** ADAM_STEP)
    v_hat = v / (1.0 - ADAM_B2 ** ADAM_STEP)
    delta = -ADAM_LR * (m_hat / (jnp.sqrt(v_hat) + ADAM_EPS) + ADAM_WD * w)
    return delta, m, v


def _ada_w_update(cact, dcond_loc, w, m, v):
    nl, d, n = w.shape

    def body(c_ref, dc_ref, w_ref, m_ref, v_ref, g_out, d_out, m_out, v_out):
        g = _dot_tn(c_ref[...], dc_ref[...])
        g_out[...] = g
        d_out[...], m_out[...], v_out[...] = _adamw_math(w_ref[...], g, m_ref[...], v_ref[...])

    blk = pl.BlockSpec((None, d, ADA_TN), lambda l, j: (l, 0, j))
    return pl.pallas_call(
        body, name="ada_w_update", grid=(nl, n // ADA_TN),
        in_specs=[_full((N_DEV, d)), pl.BlockSpec((None, N_DEV, ADA_TN), lambda l, j: (l, 0, j)), blk, blk, blk],
        out_specs=[blk] * 4, out_shape=[jax.ShapeDtypeStruct((nl, d, n), F32)] * 4,
        compiler_params=_cp(dimension_semantics=("parallel", "parallel")),
    )(cact, dcond_loc, w, m, v)


def _adamw(w, g, m, v, name):
    b, r, c = w.shape
    tr = _row_tile(r, c)

    def body(w_ref, g_ref, m_ref, v_ref, d_out, m_out, v_out):
        d_out[...], m_out[...], v_out[...] = _adamw_math(w_ref[...], g_ref[...], m_ref[...], v_ref[...])

    blk = pl.BlockSpec((None, tr, c), lambda i, j: (i, j, 0))
    return pl.pallas_call(
        body, name=name, grid=(b, r // tr), in_specs=[blk] * 4, out_specs=[blk] * 3,
        out_shape=[jax.ShapeDtypeStruct((b, r, c), F32)] * 3,
        compiler_params=_cp(dimension_semantics=("parallel", "parallel")),
    )(w, g, m, v)


def _place():
    x, y, c = lax.axis_index("x"), lax.axis_index("y"), lax.axis_index("c")
    chips = [(1 - x, y), (x, 1 - y), (1 - x, 1 - y)]
    return x, y, c, chips


def _remote(src, dst, send_sem, recv_sem, to):
    return pltpu.make_async_remote_copy(src_ref=src, dst_ref=dst, send_sem=send_sem, recv_sem=recv_sem,
                                        device_id=to, device_id_type=MESH_ID)


def _sems(n):
    return [pltpu.SemaphoreType.DMA((n,)), pltpu.SemaphoreType.DMA((n,))]


def _all_gather8(v, name):
    r, cdim = v.shape

    def body(x_ref, out_ref, stage, send_sems, recv_sems):
        x, y, c, chips = _place()
        sibling = (x, y, 1 - c)

        def slot(px, py, pc):
            return out_ref.at[4 * px + 2 * py + pc]

        first = [_remote(x_ref, slot(x, y, c), send_sems.at[0], recv_sems.at[0], sibling)]
        first += [_remote(x_ref, slot(x, y, c), send_sems.at[1 + j], recv_sems.at[1 + j], (*chip, c))
                  for j, chip in enumerate(chips)]
        for cp in first:
            cp.start()
        pltpu.sync_copy(x_ref, stage)
        pltpu.sync_copy(stage, slot(x, y, c))
        passed = []
        for j, chip in enumerate(chips):
            blk = slot(*chip, c)
            _remote(blk, blk, send_sems.at[1 + j], recv_sems.at[1 + j], (x, y, c)).wait_recv()
            fw = _remote(blk, blk, send_sems.at[4 + j], recv_sems.at[4 + j], sibling)
            fw.start()
            passed.append(fw)
        blk = slot(x, y, 1 - c)
        _remote(blk, blk, send_sems.at[0], recv_sems.at[0], (x, y, c)).wait_recv()
        for j, chip in enumerate(chips):
            blk = slot(*chip, 1 - c)
            _remote(blk, blk, send_sems.at[4 + j], recv_sems.at[4 + j], (x, y, c)).wait_recv()
        for cp in first + passed:
            cp.wait_send()

    return pl.pallas_call(
        body, name=name, out_shape=jax.ShapeDtypeStruct((N_DEV, r, cdim), v.dtype),
        in_specs=[ANY], out_specs=ANY,
        scratch_shapes=[pltpu.VMEM((r, cdim), v.dtype)] + _sems(7),
        compiler_params=_cp(),
    )(v)


def _place_weights(ws, layer, kidx):
    steps = 4
    shapes, in_specs, out_specs = [], [], []
    for w, kind in zip(ws, BIG_KINDS):
        _, a, b = w.shape
        in_specs.append(pl.BlockSpec((None, a // steps, b), lambda i, k: (layer, i, 0)))
        if kind == "col":
            shapes.append((2, a, 2 * b))
            out_specs.append(pl.BlockSpec((None, a // steps, b), lambda i, k: (k[0] // 2, i, k[0] % 2)))
        else:
            shapes.append((N_CHIP, a, b))
            out_specs.append(pl.BlockSpec((None, a // steps, b), lambda i, k: (k[0], i, 0)))

    def body(k_ref, *refs):
        for t in range(len(ws)):
            refs[len(ws) + t][...] = refs[t][...].astype(BF16)

    return pl.pallas_call(
        body, name="place_weights", out_shape=[jax.ShapeDtypeStruct(s, BF16) for s in shapes],
        grid_spec=pltpu.PrefetchScalarGridSpec(num_scalar_prefetch=1, grid=(steps,), in_specs=in_specs,
                                               out_specs=out_specs),
        compiler_params=_cp(dimension_semantics=("parallel",)),
    )(kidx, *ws)


def _weight_gather(placed, kinds):
    nt = len(placed)

    def body(*refs):
        dst = refs[nt:2 * nt]
        send_sems, recv_sems = refs[2 * nt:]
        x, y, c, chips = _place()
        sibling = (x, y, 1 - c)
        kme = 2 * x + y

        def block(t, k, h):
            if kinds[t] == "col":
                ncol = dst[t].shape[3] // 2
                return dst[t].at[k // 2, h, :, pl.ds(pl.multiple_of((k % 2) * ncol, LANES), ncol)]
            return dst[t].at[k, h]

        sends = []
        for t in range(nt):
            for j, chip in enumerate(chips):
                own = block(t, kme, c)
                cp = _remote(own, own, send_sems.at[6 * t + j], recv_sems.at[6 * t + j], (*chip, c))
                cp.start()
                sends.append(cp)
        for t in range(nt):
            for j, chip in enumerate(chips):
                blk = block(t, 2 * chip[0] + chip[1], c)
                _remote(blk, blk, send_sems.at[6 * t + j], recv_sems.at[6 * t + j], (x, y, c)).wait_recv()
                fw = _remote(blk, blk, send_sems.at[6 * t + 3 + j], recv_sems.at[6 * t + 3 + j], sibling)
                fw.start()
                sends.append(fw)
        for t in range(nt):
            for j, chip in enumerate(chips):
                blk = block(t, 2 * chip[0] + chip[1], 1 - c)
                _remote(blk, blk, send_sems.at[6 * t + 3 + j], recv_sems.at[6 * t + 3 + j], (x, y, c)).wait_recv()
        for cp in sends:
            cp.wait_send()

    return pl.pallas_call(
        body, name="weight_gather",
        out_shape=[jax.ShapeDtypeStruct(a.shape, a.dtype) for a in placed],
        in_specs=[ANY] * nt, out_specs=[ANY] * nt, input_output_aliases={t: t for t in range(nt)},
        scratch_shapes=_sems(6 * nt),
    )(*placed)


def _sibling_exchange(views):
    nt = len(views)

    def body(*refs):
        src, land = refs[:nt], refs[nt:2 * nt]
        send_sems, recv_sems = refs[2 * nt:]
        x, y, c, _ = _place()
        cps = [_remote(src[t].at[:, 1 - c], land[t], send_sems.at[t], recv_sems.at[t], (x, y, 1 - c))
               for t in range(nt)]
        for cp in cps:
            cp.start()
        for cp in cps:
            cp.wait()

    return pl.pallas_call(
        body, name="grad_sibling_exchange",
        out_shape=[jax.ShapeDtypeStruct((v.shape[0],) + v.shape[2:], v.dtype) for v in views],
        in_specs=[ANY] * nt, out_specs=[ANY] * nt, scratch_shapes=_sems(nt),
    )(*views)


def _chip_scatter(parts, kinds):
    nt = len(parts)

    def body(*refs):
        src, land = refs[:nt], refs[nt:2 * nt]
        send_sems, recv_sems = refs[2 * nt:]
        x, y, c, chips = _place()
        cps = []
        for t in range(nt):
            for j, chip in enumerate(chips):
                k = 2 * chip[0] + chip[1]
                if kinds[t] == "col":
                    ncol = land[t].shape[2]
                    win = src[t].at[k // 2, :, pl.ds(pl.multiple_of((k % 2) * ncol, LANES), ncol)]
                else:
                    win = src[t].at[k]
                cps.append(_remote(win, land[t].at[j], send_sems.at[3 * t + j], recv_sems.at[3 * t + j], (*chip, c)))
        for cp in cps:
            cp.start()
        for cp in cps:
            cp.wait()

    shapes = []
    for p, kind in zip(parts, kinds):
        shapes.append((3, p.shape[1], p.shape[2] // 2) if kind == "col" else (3,) + p.shape[1:])
    return pl.pallas_call(
        body, name="grad_chip_scatter",
        out_shape=[jax.ShapeDtypeStruct(s, BF16) for s in shapes],
        in_specs=[ANY] * nt, out_specs=[ANY] * nt, scratch_shapes=_sems(3 * nt),
    )(*parts)


def _sibling_share(fulls):
    nt = len(fulls)

    def body(*refs):
        dst = refs[nt:2 * nt]
        send_sems, recv_sems = refs[2 * nt:]
        x, y, c, _ = _place()
        cps = []
        for t in range(nt):
            mine = dst[t].at[c]
            cp = _remote(mine, mine, send_sems.at[t], recv_sems.at[t], (x, y, 1 - c))
            cp.start()
            cps.append(cp)
        for t in range(nt):
            other = dst[t].at[1 - c]
            _remote(other, other, send_sems.at[t], recv_sems.at[t], (x, y, c)).wait_recv()
        for cp in cps:
            cp.wait_send()

    return pl.pallas_call(
        body, name="grad_sibling_share",
        out_shape=[jax.ShapeDtypeStruct(f.shape, f.dtype) for f in fulls],
        in_specs=[ANY] * nt, out_specs=[ANY] * nt, input_output_aliases={t: t for t in range(nt)},
        scratch_shapes=_sems(nt),
    )(*fulls)


SUM_STEPS = 4


def _pair_sum(views, lands, ck):
    nt = len(views)
    in_specs, out_specs, shapes = [], [], []
    for v in views:
        b, _, r, cc = v.shape
        per = SUM_STEPS // b
        tr = r // per
        in_specs.append(pl.BlockSpec((None, None, tr, cc), lambda i, s, per=per: (i // per, s[0], i % per, 0)))
        out_specs.append(pl.BlockSpec((None, tr, cc), lambda i, s, per=per: (i // per, i % per, 0)))
        shapes.append((b, r, cc))
    in_specs = in_specs + out_specs

    def body(s_ref, *refs):
        for t in range(nt):
            refs[2 * nt + t][...] = (refs[t][...].astype(F32) + refs[nt + t][...].astype(F32)).astype(BF16)

    return pl.pallas_call(
        body, name="grad_pair_sum", out_shape=[jax.ShapeDtypeStruct(s, BF16) for s in shapes],
        grid_spec=pltpu.PrefetchScalarGridSpec(num_scalar_prefetch=1, grid=(SUM_STEPS,), in_specs=in_specs,
                                               out_specs=out_specs),
        compiler_params=_cp(dimension_semantics=("parallel",)),
    )(ck, *views, *lands)


def _chip_sum(parts, lands, kinds, ck):
    nt = len(parts)
    steps = 2
    in_own, in_land, out_specs, shapes = [], [], [], []
    for ld, kind in zip(lands, kinds):
        _, r, cc = ld.shape
        tr = r // steps
        if kind == "col":
            in_own.append(pl.BlockSpec((None, tr, cc), lambda i, s: (s[1] // 2, i, s[1] % 2)))
        else:
            in_own.append(pl.BlockSpec((None, tr, cc), lambda i, s: (s[1], i, 0)))
        in_land.append(pl.BlockSpec((3, tr, cc), lambda i, s: (0, i, 0)))
        out_specs.append(pl.BlockSpec((None, tr, cc), lambda i, s: (s[0], i, 0)))
        shapes.append((2, r, cc))

    def body(s_ref, *refs):
        for t in range(nt):
            acc = refs[t][...].astype(F32)
            for j in range(3):
                acc = acc + refs[nt + t][j].astype(F32)
            refs[2 * nt + t][...] = acc

    return pl.pallas_call(
        body, name="grad_chip_sum", out_shape=[jax.ShapeDtypeStruct(s, F32) for s in shapes],
        grid_spec=pltpu.PrefetchScalarGridSpec(num_scalar_prefetch=1, grid=(steps,), in_specs=in_own + in_land,
                                               out_specs=out_specs),
        compiler_params=_cp(dimension_semantics=("parallel",)),
    )(ck, *parts, *lands)


def _sum8(g):
    _, r, cc = g.shape
    tr = _row_tile(r, N_DEV * cc)

    def body(g_ref, o_ref):
        acc = g_ref[0]
        for d in range(1, N_DEV):
            acc = acc + g_ref[d]
        o_ref[...] = acc

    return pl.pallas_call(
        body, name="small_grad_sum", grid=(r // tr,),
        in_specs=[pl.BlockSpec((N_DEV, tr, cc), lambda i: (0, i, 0))],
        out_specs=pl.BlockSpec((tr, cc), lambda i: (i, 0)),
        out_shape=jax.ShapeDtypeStruct((r, cc), F32),
        compiler_params=_cp(dimension_semantics=("parallel",)),
    )(g)


def _silu_rows(c):
    def body(c_ref, o_ref):
        v = c_ref[...]
        o_ref[...] = v * jax.nn.sigmoid(v)

    return pl.pallas_call(body, name="cond_silu", out_shape=jax.ShapeDtypeStruct(c.shape, F32))(c)


def _pack(arrays):
    flat = jnp.concatenate([a.reshape(-1) for a in arrays])
    n = flat.shape[0]
    pad = (-n) % (256 * LANES)
    return jnp.pad(flat, (0, pad)).reshape(-1, LANES)


def _unpack(packed, shapes):
    flat = packed.reshape(-1)
    out, off = [], 0
    for s in shapes:
        n = math.prod(s)
        out.append(flat[off:off + n].reshape(s))
        off += n
    return out


def _reduce_big_grads(grads, kinds, ck):
    views = []
    for g, kind in zip(grads, kinds):
        if kind == "col":
            views.append(g.reshape(2, 2, g.shape[1] // 2, g.shape[2]))
        else:
            views.append(g.reshape(N_CHIP, 2, g.shape[0] // (2 * N_CHIP), g.shape[1]))
    lands = _sibling_exchange(views)
    parts = _pair_sum(views, lands, ck)
    lands = _chip_scatter(parts, kinds)
    fulls = _sibling_share(_chip_sum(parts, lands, kinds, ck))
    return [f.reshape(2 * f.shape[1], f.shape[2]) for f in fulls]


SMALL_NAMES = ["ada_b", "norm1_g", "norm2_g", "sgu_w", "sgu_b", "pool_w", "pool_scale", "conv_w", "s5_lambda_re",
               "s5_lambda_im", "s5_b_re", "s5_b_im", "s5_c_re", "s5_c_im", "s5_d", "s5_log_dt", "s5_glu_w", "s5_glu_b",
               "mix_norm_g", "norm3_g", "final_norm_g"]
BIG_NAMES = ["ffn1_w_in", "ffn1_w_out", "w_mix_in", "w_mix_out", "ffn2_w_in", "ffn2_w_out"]
BIG_KINDS = ["col", "row", "row", "row", "col", "row"]
WEIGHT_ORDER = ["ada_w", "ada_b", "norm1_g", "ffn1_w_in", "ffn1_w_out", "norm2_g", "w_mix_in", "sgu_w", "sgu_b", "pool_w",
                "pool_scale", "conv_w", "s5_lambda_re", "s5_lambda_im", "s5_b_re", "s5_b_im", "s5_c_re", "s5_c_im", "s5_d",
                "s5_log_dt", "s5_glu_w", "s5_glu_b", "mix_norm_g", "w_mix_out", "norm3_g", "ffn2_w_in", "ffn2_w_out",
                "final_norm_g"]


def _local_step(x, target, cond, big, p):
    nl, d = DEPTH, x.shape[1]
    row = lambda a: a.reshape(1, -1)
    saved = []
    for l in range(nl):
        wi1, wo1, wmit, wmo, wi2, wo2 = big[l]
        mod1, mod2, mod3 = cond[l, 0:3], cond[l, 3:6], cond[l, 6:9]
        lre, lim = p["s5_lambda_re"][l].reshape(-1), p["s5_lambda_im"][l].reshape(-1)
        ldt = jnp.repeat(p["s5_log_dt"][l], 64)
        rowp = jnp.stack([lre, lim, ldt])
        sp = (rowp, rowp.T, p["s5_b_re"][l].reshape(N_STATE, 16), p["s5_b_im"][l].reshape(N_STATE, 16),
              p["s5_c_re"][l].reshape(W_GRP, 64), p["s5_c_im"][l].reshape(W_GRP, 64), row(p["s5_d"][l]))
        glu = (p["s5_glu_w"][l], row(p["s5_glu_b"][l]))
        bias_full = jnp.repeat(p["sgu_b"][l].T, 64, axis=1)
        pw2 = p["pool_w"][l].reshape(W_GRP, 64)
        x1, h1, a1, b1, o1 = _ffn_fwd(x, mod1, row(p["norm1_g"][l]), wi1, wo1)
        z, h2 = _mix_in_fwd(x1, mod2, row(p["norm2_g"][l]), wmit)
        ya = _sgu_fwd(z, p["sgu_w"][l], bias_full)
        yb, yc = _poolconv_fwd(z, pw2, row(p["pool_scale"][l]), p["conv_w"][l])
        ypre = _s5_core_fwd(z, sp)
        yd = _s5_glu_fwd(ypre, *glu)
        ys = (ya, yb, yc, yd)
        x2, m = _mix_out_fwd(ys, row(p["mix_norm_g"][l]), wmo, x1, mod2[2:3])
        x3, h3, a3, b3, o3 = _ffn_fwd(x2, mod3, row(p["norm3_g"][l]), wi2, wo2)
        saved.append((x, x1, x2, h1, a1, b1, o1, z, h2, ys, m, h3, a3, b3, o3, sp, bias_full, pw2, ypre, glu))
        x = x3

    loss, dx, dfg = _loss_head(x, row(p["final_norm_g"]), target)

    big_grads = [None] * nl
    sg = {n: [None] * nl for n in SMALL_NAMES if n not in ("ada_b", "final_norm_g")}
    dcond = [None] * nl
    s5_da, s5_dk = [None] * nl, [None] * nl
    for l in reversed(range(nl)):
        wi1, wo1, wmit, wmo, wi2, wo2 = big[l]
        mod1, mod2, mod3 = cond[l, 0:3], cond[l, 3:6], cond[l, 6:9]
        x0, x1, x2, h1, a1, b1, o1, z, h2, ys, m, h3, a3, b3, o3, sp, bias_full, pw2, ypre, glu = saved[l]
        do, dgate3 = _gate_bwd(dx, o3, mod3[2:3], 0.5)
        dza, dzb, dwi2, dwo2 = _ffn_bwd_main(do, h3, a3, b3, wo2)
        dx, rows3 = _ffn_bwd_in(dza, dzb, wi2, x2, dx, mod3, row(p["norm3_g"][l]))
        outs = _mix_out_bwd(dx, m, mod2[2:3], ys, row(p["mix_norm_g"][l]), wmo)
        dys, dgate2, dmng, dwmo = outs[0:4], outs[4], outs[5], outs[6]
        dza_, dsw, dsb = _sgu_bwd(z, dys[0], p["sgu_w"][l], bias_full)
        dzb_, dzc_, dwbd, dps, dcw = _poolconv_bwd(z, dys[1], dys[2], pw2, row(p["pool_scale"][l]), p["conv_w"][l])
        dypre, dgw, dgb = _s5_glu_bwd(ypre, dys[3], *glu)
        dzd_, dbr, dbi, dcr, dci, dd, da, dk = _s5_core_bwd(z, dypre, sp)
        dx, rows2, dwmit = _mix_in_bwd((dza_, dzb_, dzc_, dzd_), h2, wmit, x1, dx, mod2, row(p["norm2_g"][l]))
        do, dgate1 = _gate_bwd(dx, o1, mod1[2:3], 0.5)
        dza, dzb, dwi1, dwo1 = _ffn_bwd_main(do, h1, a1, b1, wo1)
        dx, rows1 = _ffn_bwd_in(dza, dzb, wi1, x0, dx, mod1, row(p["norm1_g"][l]))

        big_grads[l] = [dwi1, dwo1, dwmit, dwmo, dwi2, dwo2]
        dcond[l] = jnp.concatenate([rows1[0:2], dgate1, rows2[0:2], dgate2, rows3[0:2], dgate3], axis=0)
        sg["norm1_g"][l], sg["norm2_g"][l], sg["norm3_g"][l] = rows1[2], rows2[2], rows3[2]
        sg["mix_norm_g"][l] = dmng[0]
        sg["sgu_w"][l] = dsw
        sg["sgu_b"][l] = dsb[:, 0:4].T
        g4 = dwbd.reshape(4, 64, 4, 64)
        sg["pool_w"][l] = jnp.stack([g4[k, :, k, :] for k in range(4)])
        sg["pool_scale"][l] = dps[0]
        sg["conv_w"][l] = dcw[0:3]
        sg["s5_b_re"][l], sg["s5_b_im"][l] = dbr.reshape(16, 64, 16), dbi.reshape(16, 64, 16)
        sg["s5_c_re"][l], sg["s5_c_im"][l] = dcr.reshape(16, 16, 64), dci.reshape(16, 16, 64)
        sg["s5_d"][l] = dd[0]
        sg["s5_glu_w"][l], sg["s5_glu_b"][l] = dgw, dgb[0]
        s5_da[l], s5_dk[l] = da, dk

    n16 = nl * 16
    dlre, dlim, dldt = _s5_param_bwd(
        p["s5_lambda_re"].reshape(n16, 64), p["s5_lambda_im"].reshape(n16, 64),
        jnp.repeat(p["s5_log_dt"].reshape(n16, 1), 64, axis=1),
        jnp.stack([a[0] for a in s5_da]).reshape(n16, 64), jnp.stack([a[1] for a in s5_da]).reshape(n16, 64),
        jnp.stack([k[:, 0] for k in s5_dk]).reshape(n16, 64), jnp.stack([k[:, 1] for k in s5_dk]).reshape(n16, 64))
    small = {n: jnp.stack(v) for n, v in sg.items() if v[0] is not None}
    small["s5_lambda_re"] = dlre.reshape(nl, 16, 64)
    small["s5_lambda_im"] = dlim.reshape(nl, 16, 64)
    small["s5_log_dt"] = dldt.reshape(nl, 16)
    small["final_norm_g"] = dfg[0]
    return loss, dx, big_grads, small, jnp.stack(dcond)


def kernel(x, c, ada_w, ada_b, norm1_g, ffn1_w_in, ffn1_w_out, norm2_g, w_mix_in, sgu_w, sgu_b, pool_w, pool_scale, conv_w, s5_lambda_re, s5_lambda_im, s5_b_re, s5_b_im, s5_c_re, s5_c_im, s5_d, s5_log_dt, s5_glu_w, s5_glu_b, mix_norm_g, w_mix_out, norm3_g, ffn2_w_in, ffn2_w_out, final_norm_g, loss_target, m_ada_w, m_ada_b, m_norm1_g, m_ffn1_w_in, m_ffn1_w_out, m_norm2_g, m_w_mix_in, m_sgu_w, m_sgu_b, m_pool_w, m_pool_scale, m_conv_w, m_s5_lambda_re, m_s5_lambda_im, m_s5_b_re, m_s5_b_im, m_s5_c_re, m_s5_c_im, m_s5_d, m_s5_log_dt, m_s5_glu_w, m_s5_glu_b, m_mix_norm_g, m_w_mix_out, m_norm3_g, m_ffn2_w_in, m_ffn2_w_out, m_final_norm_g, v_ada_w, v_ada_b, v_norm1_g, v_ffn1_w_in, v_ffn1_w_out, v_norm2_g, v_w_mix_in, v_sgu_w, v_sgu_b, v_pool_w, v_pool_scale, v_conv_w, v_s5_lambda_re, v_s5_lambda_im, v_s5_b_re, v_s5_b_im, v_s5_c_re, v_s5_c_im, v_s5_d, v_s5_log_dt, v_s5_glu_w, v_s5_glu_b, v_mix_norm_g, v_w_mix_out, v_norm3_g, v_ffn2_w_in, v_ffn2_w_out, v_final_norm_g):
    args = dict(locals())
    w = {n: args[n] for n in WEIGHT_ORDER}
    mom = {n: args["m_" + n] for n in WEIGHT_ORDER}
    vel = {n: args["v_" + n] for n in WEIGHT_ORDER}
    nl, d = DEPTH, x.shape[-1]
    s = x.shape[1]
    px, py, pc = lax.axis_index("x"), lax.axis_index("y"), lax.axis_index("c")
    kme = 2 * px + py
    me = 2 * kme + pc
    kidx = jnp.reshape(kme, (1,)).astype(jnp.int32)

    cact = _silu_rows(c)
    n_conv, n_glu = conv_w.size, s5_glu_w.size
    pre = _pack([cact, conv_w, s5_glu_w])
    pre_all = _all_gather8(pre, "gather_prelude").reshape(N_DEV, -1)
    cact_all = pre_all[:, :d]
    conv_full = jnp.concatenate(
        [pre_all[2 * k, d:d + n_conv].reshape(conv_w.shape) for k in range(N_CHIP)], axis=2)
    glu_full = jnp.concatenate(
        [pre_all[2 * k, d + n_conv:d + n_conv + n_glu].reshape(s5_glu_w.shape) for k in range(N_CHIP)], axis=1)

    n_ada = ada_w.shape[2]
    ada_b_loc = lax.dynamic_slice_in_dim(ada_b, kme * n_ada, n_ada, axis=1).reshape(nl, 1, n_ada)
    cond_part = _cond_fwd(cact_all, ada_w, ada_b_loc)
    cond_all = _all_gather8(cond_part.reshape(nl * N_DEV, n_ada), "gather_cond").reshape(N_DEV, nl, N_DEV, n_ada)
    cond_me = jnp.concatenate(
        [lax.dynamic_index_in_dim(cond_all[2 * k], me, axis=1, keepdims=False) for k in range(N_CHIP)], axis=1)
    cond = cond_me.reshape(nl, 9, d)

    shards = [ffn1_w_in, ffn1_w_out, jnp.swapaxes(w_mix_in, 1, 2), w_mix_out, ffn2_w_in, ffn2_w_out]
    big = []
    for l in range(nl):
        placed = _place_weights(shards, l, kidx)
        views = [a.reshape(a.shape[0], 2, a.shape[1] // 2, a.shape[2]) for a in placed]
        gathered = _weight_gather(views, BIG_KINDS)
        big.append([g.reshape(a.shape) if kind == "col" else g.reshape(a.shape[0] * a.shape[1], a.shape[2])
                    for g, a, kind in zip(gathered, placed, BIG_KINDS)])

    p = {n: w[n] for n in SMALL_NAMES}
    p["conv_w"], p["s5_glu_w"] = conv_full, glu_full
    loss, dx, big_grads, small, dcond = _local_step(x[0], loss_target[0], cond, big, p)

    small_order = [n for n in SMALL_NAMES if n != "ada_b"]
    packed = _pack([dcond] + [small[n] for n in small_order])
    gathered_small = _all_gather8(packed, "gather_small_grads")
    total = _sum8(gathered_small)
    shapes = [dcond.shape] + [small[n].shape for n in small_order]
    tot = dict(zip(["ada_b"] + small_order, _unpack(total, shapes)))
    grads = {n: tot[n] for n in SMALL_NAMES}
    grads["ada_b"] = tot["ada_b"].reshape(nl, 9 * d)
    grads["conv_w"] = lax.dynamic_slice_in_dim(tot["conv_w"], kme * conv_w.shape[2], conv_w.shape[2], axis=2)
    grads["s5_glu_w"] = lax.dynamic_slice_in_dim(tot["s5_glu_w"], kme * s5_glu_w.shape[1], s5_glu_w.shape[1], axis=1)

    dcond_all = gathered_small.reshape(N_DEV, -1)[:, :dcond.size].reshape(N_DEV, nl, 9 * d)
    dcond_loc = jnp.swapaxes(lax.dynamic_slice_in_dim(dcond_all, kme * n_ada, n_ada, axis=2), 0, 1)
    g_ada, d_ada, m_ada, v_ada = _ada_w_update(cact_all, dcond_loc, ada_w, m_ada_w, v_ada_w)

    ck = jnp.stack([pc, kme]).astype(jnp.int32)
    reduced = [_reduce_big_grads(big_grads[l], BIG_KINDS, ck) for l in range(nl)]
    for t, n in enumerate(BIG_NAMES):
        g = jnp.stack([reduced[l][t] for l in range(nl)])
        grads[n] = jnp.swapaxes(g, 1, 2) if n == "w_mix_in" else g

    delta, new_m, new_v = {}, {}, {}
    grads["ada_w"], delta["ada_w"], new_m["ada_w"], new_v["ada_w"] = g_ada, d_ada, m_ada, v_ada
    for n in BIG_NAMES:
        delta[n], new_m[n], new_v[n] = _adamw(w[n], grads[n], mom[n], vel[n], "adamw_" + n)
    sw, sg_, sm, sv = (_pack([t[n] for n in SMALL_NAMES])[None] for t in (w, grads, mom, vel))
    outs = _adamw(sw, sg_, sm, sv, "adamw_small")
    sshapes = [w[n].shape for n in SMALL_NAMES]
    for res, o in zip((delta, new_m, new_v), outs):
        res.update(dict(zip(SMALL_NAMES, _unpack(o[0], sshapes))))

    loss_total = lax.psum(loss[0, 0], ("x", "y", "c"))
    return (loss_total, dx[None], *[grads[n] for n in WEIGHT_ORDER], *[delta[n] for n in WEIGHT_ORDER],
            *[new_m[n] for n in WEIGHT_ORDER], *[new_v[n] for n in WEIGHT_ORDER])
```

```python
import functools
import math

import jax
import jax.numpy as jnp
from jax import lax
from jax.experimental import pallas as pl
from jax.experimental.pallas import tpu as pltpu

F32, BF16 = jnp.float32, jnp.bfloat16
EPS = 1e-6
DEPTH = 4
N_DEV = 8
N_CHIP = 4
W_GRP = 256
CHUNK = 128
N_SEG = 8
LANES = 128
FFN_TF = 256
VMEM_LIMIT = 56 * 1024 * 1024
ADAM_LR, ADAM_B1, ADAM_B2, ADAM_EPS, ADAM_WD, ADAM_STEP = 0.001, 0.9, 0.999, 1e-08, 0.01, 10
MESH_ID = pl.DeviceIdType.MESH
HI = lax.Precision.HIGHEST
ANY = pl.BlockSpec(memory_space=pl.ANY)


def _cp(**kw):
    return pltpu.CompilerParams(vmem_limit_bytes=VMEM_LIMIT, **kw)


def _dot(a, b):
    return jnp.dot(a.astype(BF16), b.astype(BF16), preferred_element_type=F32)


def _dot_nt(a, b):
    return lax.dot_general(a.astype(BF16), b.astype(BF16), (((1,), (1,)), ((), ())), preferred_element_type=F32)


def _dot_tn(a, b):
    return lax.dot_general(a.astype(BF16), b.astype(BF16), (((0,), (0,)), ((), ())), preferred_element_type=F32)


def _dot_hi(a, b):
    return jnp.dot(a, b, preferred_element_type=F32, precision=HI)


def _gelu(x):
    k = 0.7978845608028654
    t = jnp.tanh(k * (x + 0.044715 * x * x * x))
    return 0.5 * x * (1.0 + t), t


def _gelu_grad(x, t):
    k = 0.7978845608028654
    return 0.5 * (1.0 + t) + 0.5 * x * (1.0 - t * t) * k * (1.0 + 3.0 * 0.044715 * x * x)


def _iota(shape, axis):
    return lax.broadcasted_iota(jnp.int32, shape, axis)


def _full(shape):
    nd = len(shape)
    return pl.BlockSpec(shape, lambda *_: (0,) * nd)


def _norm_mod(xv, g, shift, scale):
    r = lax.rsqrt(jnp.mean(xv * xv, axis=-1, keepdims=True) + EPS)
    return (xv * r * g) * (1.0 + scale) + shift


def _norm_mod_bwd(xv, g, scale, dh):
    r = lax.rsqrt(jnp.mean(xv * xv, axis=-1, keepdims=True) + EPS)
    xh = xv * r
    n = xh * g
    dsh = jnp.sum(dh, axis=0, keepdims=True)
    dsc = jnp.sum(dh * n, axis=0, keepdims=True)
    dn = dh * (1.0 + scale)
    dg = jnp.sum(dn * xh, axis=0, keepdims=True)
    dxh = dn * g
    dx = r * (dxh - xh * jnp.mean(dxh * xh, axis=-1, keepdims=True))
    return dx, dsh, dsc, dg


def _tm(s):
    return min(s, 1024)


def _ffn_fwd(x, mod, g, wi, wo):
    s, d = x.shape
    f = wo.shape[0]
    tf, tm = FFN_TF, _tm(s)
    nf, nt = f // tf, s // tm

    def body(x_ref, mod_ref, g_ref, wa_ref, wb_ref, wo_ref, xn_ref, h_ref, a_ref, b_ref, o_ref):
        j = pl.program_id(1)

        @pl.when(j == 0)
        def _():
            hh = _norm_mod(x_ref[...], g_ref[...], mod_ref[0:1, :], mod_ref[1:2, :])
            h_ref[...] = hh.astype(BF16)
            o_ref[...] = jnp.zeros_like(o_ref)

        h = h_ref[...]
        a = jnp.dot(h, wa_ref[...], preferred_element_type=F32)
        b = jnp.dot(h, wb_ref[...], preferred_element_type=F32)
        a_ref[...] = a.astype(BF16)
        b_ref[...] = b.astype(BF16)
        u = (a * jax.nn.sigmoid(a)) * b
        o_ref[...] += jnp.dot(u.astype(BF16), wo_ref[...], preferred_element_type=F32)

        @pl.when(j == nf - 1)
        def _():
            xn_ref[...] = x_ref[...] + 0.5 * mod_ref[2:3, :] * o_ref[...]

    row = pl.BlockSpec((tm, d), lambda i, j: (i, 0))
    chunk = pl.BlockSpec((tm, tf), lambda i, j: (i, j))
    return pl.pallas_call(
        body, name="ffn_fwd", grid=(nt, nf),
        in_specs=[row, _full((3, d)), _full((1, d)),
                  pl.BlockSpec((None, d, tf), lambda i, j: (0, 0, j)),
                  pl.BlockSpec((None, d, tf), lambda i, j: (1, 0, j)),
                  pl.BlockSpec((tf, d), lambda i, j: (j, 0))],
        out_specs=[row, row, chunk, chunk, row],
        out_shape=[jax.ShapeDtypeStruct((s, d), F32), jax.ShapeDtypeStruct((s, d), BF16),
                   jax.ShapeDtypeStruct((s, f), BF16), jax.ShapeDtypeStruct((s, f), BF16),
                   jax.ShapeDtypeStruct((s, d), F32)],
        compiler_params=_cp(dimension_semantics=("parallel", "arbitrary")),
    )(x, mod, g, wi, wi, wo)


def _gate_bwd(dxo, o, gate, half):
    s, d = dxo.shape
    tm = _tm(s)
    nt = s // tm

    def body(dx_ref, o_ref, gate_ref, do_ref, dg_ref):
        i = pl.program_id(0)

        @pl.when(i == 0)
        def _():
            dg_ref[...] = jnp.zeros_like(dg_ref)

        dx = dx_ref[...]
        do_ref[...] = (half * gate_ref[...] * dx).astype(BF16)
        dg_ref[...] += half * jnp.sum(o_ref[...] * dx, axis=0, keepdims=True)

    row = pl.BlockSpec((tm, d), lambda i: (i, 0))
    return pl.pallas_call(
        body, name="gate_bwd", grid=(nt,),
        in_specs=[row, row, _full((1, d))], out_specs=[row, _full((1, d))],
        out_shape=[jax.ShapeDtypeStruct((s, d), BF16), jax.ShapeDtypeStruct((1, d), F32)],
        compiler_params=_cp(dimension_semantics=("arbitrary",)),
    )(dxo, o, gate)


def _ffn_bwd_main(do, h, a, b, wo):
    s, d = do.shape
    f = wo.shape[0]
    tf, tm = FFN_TF, _tm(s)
    nf, nt = f // tf, s // tm

    def body(do_ref, h_ref, a_ref, b_ref, wo_ref, dza_ref, dzb_ref, dwi_ref, dwo_ref, acc_a, acc_b, acc_o):
        i = pl.program_id(1)

        @pl.when(i == 0)
        def _():
            acc_a[...] = jnp.zeros_like(acc_a)
            acc_b[...] = jnp.zeros_like(acc_b)
            acc_o[...] = jnp.zeros_like(acc_o)

        dov = do_ref[...]
        hv = h_ref[...]
        du = lax.dot_general(dov, wo_ref[...], (((1,), (1,)), ((), ())), preferred_element_type=F32)
        av = a_ref[...].astype(F32)
        bv = b_ref[...].astype(F32)
        sa = jax.nn.sigmoid(av)
        si = av * sa
        u = (si * bv).astype(BF16)
        da = (du * bv * (sa * (1.0 + av * (1.0 - sa)))).astype(BF16)
        db = (du * si).astype(BF16)
        dza_ref[...] = da
        dzb_ref[...] = db
        acc_o[...] += _dot_tn(u, dov)
        acc_a[...] += _dot_tn(hv, da)
        acc_b[...] += _dot_tn(hv, db)

        @pl.when(i == nt - 1)
        def _():
            dwi_ref[0] = acc_a[...].astype(BF16)
            dwi_ref[1] = acc_b[...].astype(BF16)
            dwo_ref[...] = acc_o[...].astype(BF16)

    row = pl.BlockSpec((tm, d), lambda j, i: (i, 0))
    chunk = pl.BlockSpec((tm, tf), lambda j, i: (i, j))
    return pl.pallas_call(
        body, name="ffn_bwd_main", grid=(nf, nt),
        in_specs=[row, row, chunk, chunk, pl.BlockSpec((tf, d), lambda j, i: (j, 0))],
        out_specs=[chunk, chunk, pl.BlockSpec((2, d, tf), lambda j, i: (0, 0, j)),
                   pl.BlockSpec((tf, d), lambda j, i: (j, 0))],
        out_shape=[jax.ShapeDtypeStruct((s, f), BF16), jax.ShapeDtypeStruct((s, f), BF16),
                   jax.ShapeDtypeStruct((2, d, f), BF16), jax.ShapeDtypeStruct((f, d), BF16)],
        scratch_shapes=[pltpu.VMEM((d, tf), F32), pltpu.VMEM((d, tf), F32), pltpu.VMEM((tf, d), F32)],
        compiler_params=_cp(dimension_semantics=("parallel", "arbitrary")),
    )(do, h, a, b, wo)


def _ffn_bwd_in(dza, dzb, wi, x, dxo, mod, g):
    s, d = x.shape
    f = dza.shape[1]
    tf, tm = FFN_TF, _tm(s)
    nf, nt = f // tf, s // tm

    def body(dza_ref, dzb_ref, wa_ref, wb_ref, x_ref, dxo_ref, mod_ref, g_ref, dx_ref, rows_ref, acc):
        i, j = pl.program_id(0), pl.program_id(1)

        @pl.when(jnp.logical_and(i == 0, j == 0))
        def _():
            rows_ref[...] = jnp.zeros_like(rows_ref)

        @pl.when(j == 0)
        def _():
            acc[...] = jnp.zeros_like(acc)

        acc[...] += (lax.dot_general(dza_ref[...], wa_ref[...], (((1,), (1,)), ((), ())), preferred_element_type=F32)
                     + lax.dot_general(dzb_ref[...], wb_ref[...], (((1,), (1,)), ((), ())), preferred_element_type=F32))

        @pl.when(j == nf - 1)
        def _():
            dx, dsh, dsc, dg = _norm_mod_bwd(x_ref[...], g_ref[...], mod_ref[1:2, :], acc[...])
            dx_ref[...] = dx + dxo_ref[...]
            rows_ref[0:1, :] += dsh
            rows_ref[1:2, :] += dsc
            rows_ref[2:3, :] += dg

    row = pl.BlockSpec((tm, d), lambda i, j: (i, 0))
    chunk = pl.BlockSpec((tm, tf), lambda i, j: (i, j))
    return pl.pallas_call(
        body, name="ffn_bwd_in", grid=(nt, nf),
        in_specs=[chunk, chunk,
                  pl.BlockSpec((None, d, tf), lambda i, j: (0, 0, j)),
                  pl.BlockSpec((None, d, tf), lambda i, j: (1, 0, j)),
                  row, row, _full((3, d)), _full((1, d))],
        out_specs=[row, _full((8, d))],
        out_shape=[jax.ShapeDtypeStruct((s, d), F32), jax.ShapeDtypeStruct((8, d), F32)],
        scratch_shapes=[pltpu.VMEM((tm, d), F32)],
        compiler_params=_cp(dimension_semantics=("arbitrary", "arbitrary")),
    )(dza, dzb, wi, wi, x, dxo, mod, g)


def _mix_in_fwd(x, mod, g, wmit):
    s, d = x.shape
    p = wmit.shape[0]
    tm = _tm(s)

    def body(x_ref, mod_ref, g_ref, w_ref, z_ref, h_ref):
        hh = _norm_mod(x_ref[...], g_ref[...], mod_ref[0:1, :], mod_ref[1:2, :]).astype(BF16)
        h_ref[...] = hh
        z_ref[...] = lax.dot_general(hh, w_ref[...], (((1,), (1,)), ((), ())), preferred_element_type=F32)

    row = pl.BlockSpec((tm, d), lambda i: (i, 0))
    return pl.pallas_call(
        body, name="mix_in_fwd", grid=(s // tm,),
        in_specs=[row, _full((3, d)), _full((1, d)), _full((p, d))],
        out_specs=[pl.BlockSpec((tm, p), lambda i: (i, 0)), row],
        out_shape=[jax.ShapeDtypeStruct((s, p), F32), jax.ShapeDtypeStruct((s, d), BF16)],
        compiler_params=_cp(dimension_semantics=("parallel",)),
    )(x, mod, g, wmit)


def _mix_in_bwd(dzs, h, wmit, x, dxo, mod, g):
    s, d = x.shape
    p = wmit.shape[0]
    tm = min(s, 512)
    nt = s // tm

    def body(za_ref, zb_ref, zc_ref, zd_ref, h_ref, w_ref, x_ref, dxo_ref, mod_ref, g_ref,
             dx_ref, rows_ref, dw_ref, acc):
        i = pl.program_id(0)

        @pl.when(i == 0)
        def _():
            rows_ref[...] = jnp.zeros_like(rows_ref)
            acc[...] = jnp.zeros_like(acc)

        dz = jnp.concatenate([za_ref[...], zb_ref[...], zc_ref[...], zd_ref[...]], axis=1).astype(BF16)
        acc[...] += _dot_tn(dz, h_ref[...])
        dh = jnp.dot(dz, w_ref[...], preferred_element_type=F32)
        dx, dsh, dsc, dg = _norm_mod_bwd(x_ref[...], g_ref[...], mod_ref[1:2, :], dh)
        dx_ref[...] = dx + dxo_ref[...]
        rows_ref[0:1, :] += dsh
        rows_ref[1:2, :] += dsc
        rows_ref[2:3, :] += dg

        @pl.when(i == nt - 1)
        def _():
            dw_ref[...] = acc[...].astype(BF16)

    row = pl.BlockSpec((tm, d), lambda i: (i, 0))
    zspecs = [pl.BlockSpec((tm, z.shape[1]), lambda i: (i, 0)) for z in dzs]
    return pl.pallas_call(
        body, name="mix_in_bwd", grid=(nt,),
        in_specs=zspecs + [row, _full((p, d)), row, row, _full((3, d)), _full((1, d))],
        out_specs=[row, _full((8, d)), _full((p, d))],
        out_shape=[jax.ShapeDtypeStruct((s, d), F32), jax.ShapeDtypeStruct((8, d), F32),
                   jax.ShapeDtypeStruct((p, d), BF16)],
        scratch_shapes=[pltpu.VMEM((p, d), F32)],
        compiler_params=_cp(dimension_semantics=("arbitrary",)),
    )(*dzs, h, wmit, x, dxo, mod, g)


def _group_norm(ys, mng):
    outs, hats, rs = [], [], []
    for k, y in enumerate(ys):
        r = lax.rsqrt(jnp.mean(y * y, axis=-1, keepdims=True) + EPS)
        yh = y * r
        hats.append(yh)
        rs.append(r)
        outs.append(yh * mng[:, k * W_GRP:(k + 1) * W_GRP])
    return jnp.concatenate(outs, axis=1), hats, rs


def _mix_out_fwd(ys, mng, wmo, x, gate):
    s, d = x.shape
    tm = _tm(s)

    def body(ya, yb, yc, yd, mng_ref, w_ref, x_ref, gate_ref, xn_ref, m_ref):
        yn, _, _ = _group_norm([ya[...], yb[...], yc[...], yd[...]], mng_ref[...])
        m = jnp.dot(yn.astype(BF16), w_ref[...], preferred_element_type=F32)
        m_ref[...] = m
        xn_ref[...] = x_ref[...] + gate_ref[...] * m

    row = pl.BlockSpec((tm, d), lambda i: (i, 0))
    grp = pl.BlockSpec((tm, W_GRP), lambda i: (i, 0))
    return pl.pallas_call(
        body, name="mix_out_fwd", grid=(s // tm,),
        in_specs=[grp, grp, grp, grp, _full((1, d)), _full((d, d)), row, _full((1, d))],
        out_specs=[row, row],
        out_shape=[jax.ShapeDtypeStruct((s, d), F32), jax.ShapeDtypeStruct((s, d), F32)],
        compiler_params=_cp(dimension_semantics=("parallel",)),
    )(*ys, mng, wmo, x, gate)


def _mix_out_bwd(dxo, m, gate, ys, mng, wmo):
    s, d = dxo.shape
    tm = min(s, 512)
    nt = s // tm

    def body(dxo_ref, m_ref, gate_ref, ya, yb, yc, yd, mng_ref, w_ref,
             dya, dyb, dyc, dyd, dgate_ref, dmng_ref, dw_ref, acc):
        i = pl.program_id(0)

        @pl.when(i == 0)
        def _():
            dgate_ref[...] = jnp.zeros_like(dgate_ref)
            dmng_ref[...] = jnp.zeros_like(dmng_ref)
            acc[...] = jnp.zeros_like(acc)

        dxv = dxo_ref[...]
        dgate_ref[...] += jnp.sum(m_ref[...] * dxv, axis=0, keepdims=True)
        dm = (gate_ref[...] * dxv).astype(BF16)
        mng = mng_ref[...]
        yn, hats, rs = _group_norm([ya[...], yb[...], yc[...], yd[...]], mng)
        acc[...] += _dot_tn(yn, dm)
        dyn = lax.dot_general(dm, w_ref[...], (((1,), (1,)), ((), ())), preferred_element_type=F32)
        dmng_parts = []
        for k, (yh, r, out) in enumerate(zip(hats, rs, (dya, dyb, dyc, dyd))):
            dk = dyn[:, k * W_GRP:(k + 1) * W_GRP]
            dmng_parts.append(jnp.sum(dk * yh, axis=0, keepdims=True))
            dyh = dk * mng[:, k * W_GRP:(k + 1) * W_GRP]
            out[...] = r * (dyh - yh * jnp.mean(dyh * yh, axis=-1, keepdims=True))
        dmng_ref[...] += jnp.concatenate(dmng_parts, axis=1)

        @pl.when(i == nt - 1)
        def _():
            dw_ref[...] = acc[...].astype(BF16)

    row = pl.BlockSpec((tm, d), lambda i: (i, 0))
    grp = pl.BlockSpec((tm, W_GRP), lambda i: (i, 0))
    return pl.pallas_call(
        body, name="mix_out_bwd", grid=(nt,),
        in_specs=[row, row, _full((1, d)), grp, grp, grp, grp, _full((1, d)), _full((d, d))],
        out_specs=[grp, grp, grp, grp, _full((1, d)), _full((1, d)), _full((d, d))],
        out_shape=[jax.ShapeDtypeStruct((s, W_GRP), F32)] * 4
        + [jax.ShapeDtypeStruct((1, d), F32), jax.ShapeDtypeStruct((1, d), F32), jax.ShapeDtypeStruct((d, d), BF16)],
        scratch_shapes=[pltpu.VMEM((d, d), F32)],
        compiler_params=_cp(dimension_semantics=("arbitrary",)),
    )(dxo, m, gate, *ys, mng, wmo)


def _sgu_consts():
    r = _iota((W_GRP, W_GRP), 0) >> 6
    c = _iota((W_GRP, W_GRP), 1) >> 6
    avg = jnp.where(r == c, 1.0 / 64.0, 0.0).astype(F32)
    tril = _iota((CHUNK, CHUNK), 0) >= _iota((CHUNK, CHUNK), 1)
    head = _iota((CHUNK, W_GRP), 1) >> 6
    return avg, tril, head


def _sgu_pre(za, avg):
    zg, t = _gelu(za)
    u, v = zg[:, :W_GRP], zg[:, W_GRP:]
    mu = _dot_hi(v, avg)
    vc = v - mu
    r = lax.rsqrt(_dot_hi(vc * vc, avg) + EPS)
    return t, u, vc * r, r


def _sgu_fwd(z, sgu_w, bias_full):
    s = z.shape[0]
    tm = min(s, 512)

    def body(za_ref, w_ref, bias_ref, ya_ref):
        avg, tril, head = _sgu_consts()
        _, u, vn, _ = _sgu_pre(za_ref[...], avg)
        wm = [jnp.where(tril, w_ref[h], 0.0).astype(BF16) for h in range(4)]
        vb = vn.astype(BF16)
        for n in range(tm // CHUNK):
            rows = slice(n * CHUNK, (n + 1) * CHUNK)
            mixed = bias_ref[...]
            for h in range(4):
                mixed = mixed + jnp.where(head == h, jnp.dot(wm[h], vb[rows], preferred_element_type=F32), 0.0)
            ya_ref[rows, :] = u[rows] * mixed

    return pl.pallas_call(
        body, name="sgu_fwd", grid=(s // tm,),
        in_specs=[pl.BlockSpec((tm, 2 * W_GRP), lambda i: (i, 0)), _full((4, CHUNK, CHUNK)), _full((CHUNK, W_GRP))],
        out_specs=pl.BlockSpec((tm, W_GRP), lambda i: (i, 0)),
        out_shape=jax.ShapeDtypeStruct((s, W_GRP), F32),
        compiler_params=_cp(dimension_semantics=("parallel",)),
    )(z, sgu_w, bias_full)


def _sgu_bwd(z, dya, sgu_w, bias_full):
    s = z.shape[0]
    tm = min(s, 512)
    nt = s // tm

    def body(za_ref, dya_ref, w_ref, bias_ref, dza_ref, dw_ref, db_ref, du_s, dvn_s):
        i = pl.program_id(0)

        @pl.when(i == 0)
        def _():
            dw_ref[...] = jnp.zeros_like(dw_ref)
            db_ref[...] = jnp.zeros_like(db_ref)

        avg, tril, head = _sgu_consts()
        za = za_ref[...]
        t, u, vn, r = _sgu_pre(za, avg)
        wm = [jnp.where(tril, w_ref[h], 0.0).astype(BF16) for h in range(4)]
        vb = vn.astype(BF16)
        dya = dya_ref[...]
        dw = [jnp.zeros((CHUNK, CHUNK), F32) for _ in range(4)]
        db = jnp.zeros((CHUNK, W_GRP), F32)
        for n in range(tm // CHUNK):
            rows = slice(n * CHUNK, (n + 1) * CHUNK)
            mixed = bias_ref[...]
            for h in range(4):
                mixed = mixed + jnp.where(head == h, jnp.dot(wm[h], vb[rows], preferred_element_type=F32), 0.0)
            dmix = dya[rows] * u[rows]
            du_s[rows, :] = dya[rows] * mixed
            db = db + dmix
            dmb = dmix.astype(BF16)
            dvn = jnp.zeros((CHUNK, W_GRP), F32)
            for h in range(4):
                dmh = jnp.where(head == h, dmix, 0.0)
                dw[h] = dw[h] + _dot_nt(dmh, vb[rows])
                dvn = dvn + jnp.where(head == h, _dot_tn(wm[h], dmb), 0.0)
            dvn_s[rows, :] = dvn
        for h in range(4):
            dw_ref[h] += jnp.where(tril, dw[h], 0.0)
        sel = ((_iota((W_GRP, CHUNK), 0) >> 6) == _iota((W_GRP, CHUNK), 1)).astype(F32)
        db_ref[...] += _dot_hi(db, sel)
        dvn = dvn_s[...]
        dv = r * (dvn - _dot_hi(dvn, avg) - vn * _dot_hi(dvn * vn, avg))
        dzg = jnp.concatenate([du_s[...], dv], axis=1)
        dza_ref[...] = dzg * _gelu_grad(za, t)

    return pl.pallas_call(
        body, name="sgu_bwd", grid=(nt,),
        in_specs=[pl.BlockSpec((tm, 2 * W_GRP), lambda i: (i, 0)), pl.BlockSpec((tm, W_GRP), lambda i: (i, 0)),
                  _full((4, CHUNK, CHUNK)), _full((CHUNK, W_GRP))],
        out_specs=[pl.BlockSpec((tm, 2 * W_GRP), lambda i: (i, 0)), _full((4, CHUNK, CHUNK)), _full((CHUNK, CHUNK))],
        out_shape=[jax.ShapeDtypeStruct((s, 2 * W_GRP), F32), jax.ShapeDtypeStruct((4, CHUNK, CHUNK), F32),
                   jax.ShapeDtypeStruct((CHUNK, CHUNK), F32)],
        scratch_shapes=[pltpu.VMEM((tm, W_GRP), F32), pltpu.VMEM((tm, W_GRP), F32)],
        compiler_params=_cp(dimension_semantics=("arbitrary",)),
    )(z, dya, sgu_w, bias_full)


def _shift_down(x, k):
    return jnp.where(_iota(x.shape, 0) < k, 0.0, pltpu.roll(x, k, 0))


def _shift_up(x, k):
    n = x.shape[0]
    return jnp.where(_iota(x.shape, 0) >= n - k, 0.0, pltpu.roll(x, n - k, 0))


def _by_pool_group(shape, v2, v4, v8, v16):
    col = _iota(shape, 1)
    return jnp.where(col < 64, v2, jnp.where(col < 128, v4, jnp.where(col < 192, v8, v16)))


def _pool_core(zb, pw2):
    s2 = zb + _shift_down(zb, 1)
    s4 = s2 + _shift_down(s2, 2)
    s8 = s4 + _shift_down(s4, 4)
    s16 = s8 + _shift_down(s8, 8)
    win = _by_pool_group(zb.shape, s2, s4, s8, s16)
    wlen = _by_pool_group(zb.shape, 2.0, 4.0, 8.0, 16.0)
    cnt = jnp.minimum((_iota(zb.shape, 0) + 1).astype(F32), wlen)
    p = win / cnt - zb
    wt = jnp.tile(pw2, (1, 4))
    wbd = jnp.where((_iota(wt.shape, 0) >> 6) == (_iota(wt.shape, 1) >> 6), wt, 0.0).astype(BF16)
    return p, cnt, wbd


def _conv_core(zc, cw):
    bg, cg, xh = zc[:, :W_GRP], zc[:, W_GRP:2 * W_GRP], zc[:, 2 * W_GRP:]
    y = cg * xh
    y1, y2 = _shift_down(y, 1), _shift_down(y, 2)
    out = cw[2:3, :] * y + cw[1:2, :] * y1 + cw[0:1, :] * y2
    return bg, cg, xh, y, y1, y2, out


def _poolconv_fwd(z, pw2, pscale, cw):
    s = z.shape[0]

    def body(zb_ref, zc_ref, pw_ref, ps_ref, cw_ref, yb_ref, yc_ref):
        p, _, wbd = _pool_core(zb_ref[...], pw_ref[...])
        yb_ref[...] = jnp.dot(p.astype(BF16), wbd, preferred_element_type=F32) * ps_ref[...]
        bg, _, _, _, _, _, out = _conv_core(zc_ref[...], cw_ref[...])
        yc_ref[...] = bg * out

    return pl.pallas_call(
        body, name="poolconv_fwd", grid=(1,),
        in_specs=[pl.BlockSpec((s, W_GRP), lambda i: (0, 2)), pl.BlockSpec((s, 3 * W_GRP), lambda i: (0, 1)),
                  _full((W_GRP, 64)), _full((1, W_GRP)), _full((3, W_GRP))],
        out_specs=[_full((s, W_GRP)), _full((s, W_GRP))],
        out_shape=[jax.ShapeDtypeStruct((s, W_GRP), F32)] * 2,
        compiler_params=_cp(dimension_semantics=("arbitrary",)),
    )(z, z, pw2, pscale, cw)


def _poolconv_bwd(z, dyb, dyc, pw2, pscale, cw):
    s = z.shape[0]

    def body(zb_ref, zc_ref, dyb_ref, dyc_ref, pw_ref, ps_ref, cw_ref, dzb_ref, dzc_ref, dw_ref, dps_ref, dcw_ref):
        zb = zb_ref[...]
        p, cnt, wbd = _pool_core(zb, pw_ref[...])
        pb = p.astype(BF16)
        out = jnp.dot(pb, wbd, preferred_element_type=F32)
        dyb = dyb_ref[...]
        dps_ref[...] = jnp.sum(dyb * out, axis=0, keepdims=True)
        dout = (dyb * ps_ref[...]).astype(BF16)
        dw = _dot_tn(pb, dout)
        dw_ref[...] = jnp.where((_iota(dw.shape, 0) >> 6) == (_iota(dw.shape, 1) >> 6), dw, 0.0)
        dp = lax.dot_general(dout, wbd, (((1,), (1,)), ((), ())), preferred_element_type=F32)
        dwin = dp / cnt
        t2 = dwin + _shift_up(dwin, 1)
        t4 = t2 + _shift_up(t2, 2)
        t8 = t4 + _shift_up(t4, 4)
        t16 = t8 + _shift_up(t8, 8)
        dzb_ref[...] = _by_pool_group(zb.shape, t2, t4, t8, t16) - dp

        cw = cw_ref[...]
        bg, cg, xh, y, y1, y2, out = _conv_core(zc_ref[...], cw)
        dyc = dyc_ref[...]
        dout = dyc * bg
        dcw_ref[...] = jnp.zeros_like(dcw_ref)
        dcw_ref[0:1, :] = jnp.sum(dout * y2, axis=0, keepdims=True)
        dcw_ref[1:2, :] = jnp.sum(dout * y1, axis=0, keepdims=True)
        dcw_ref[2:3, :] = jnp.sum(dout * y, axis=0, keepdims=True)
        dy = cw[2:3, :] * dout + cw[1:2, :] * _shift_up(dout, 1) + cw[0:1, :] * _shift_up(dout, 2)
        dzc_ref[...] = jnp.concatenate([dyc * out, dy * xh, dy * cg], axis=1)

    return pl.pallas_call(
        body, name="poolconv_bwd", grid=(1,),
        in_specs=[pl.BlockSpec((s, W_GRP), lambda i: (0, 2)), pl.BlockSpec((s, 3 * W_GRP), lambda i: (0, 1)),
                  _full((s, W_GRP)), _full((s, W_GRP)), _full((W_GRP, 64)), _full((1, W_GRP)), _full((3, W_GRP))],
        out_specs=[_full((s, W_GRP)), _full((s, 3 * W_GRP)), _full((W_GRP, W_GRP)), _full((1, W_GRP)), _full((8, W_GRP))],
        out_shape=[jax.ShapeDtypeStruct((s, W_GRP), F32), jax.ShapeDtypeStruct((s, 3 * W_GRP), F32),
                   jax.ShapeDtypeStruct((W_GRP, W_GRP), F32), jax.ShapeDtypeStruct((1, W_GRP), F32),
                   jax.ShapeDtypeStruct((8, W_GRP), F32)],
        compiler_params=_cp(dimension_semantics=("arbitrary",)),
    )(z, z, dyb, dyc, pw2, pscale, cw)


N_STATE = 1024
HALF_STATE = N_STATE // 2
HALF_CH = W_GRP // 2
N_SLAB = HALF_STATE // LANES


def _s5_disc(lre, lim, ldt):
    dt = jnp.exp(ldt)
    mag = jnp.exp(lre * dt)
    ang = lim * dt
    ar, ai = mag * jnp.cos(ang), mag * jnp.sin(ang)
    nr, ni = ar - 1.0, ai
    den = lre * lre + lim * lim
    kr = (nr * lre + ni * lim) / den
    ki = (ni * lre - nr * lim) / den
    return ar, ai, kr, ki


def _s5_mats(colp, br, bi, cr, ci):
    _, _, kr, ki = _s5_disc(colp[:, 0:1], colp[:, 1:2], colp[:, 2:3])
    bbr = kr * br - ki * bi
    bbi = kr * bi + ki * br
    bmask = (_iota((HALF_STATE, HALF_CH), 0) >> 6) == (_iota((HALF_STATE, HALF_CH), 1) >> 4)
    cmask = (_iota((HALF_CH, HALF_STATE), 0) >> 4) == (_iota((HALF_CH, HALF_STATE), 1) >> 6)
    btr = jnp.where(bmask, jnp.tile(bbr, (1, 8)), 0.0).astype(BF16)
    bti = jnp.where(bmask, jnp.tile(bbi, (1, 8)), 0.0).astype(BF16)
    ctr = jnp.where(cmask, jnp.tile(cr, (1, 8)), 0.0).astype(BF16)
    cti = jnp.where(cmask, jnp.tile(ci, (1, 8)), 0.0).astype(BF16)
    return kr, ki, btr, bti, ctr, cti, bmask, cmask


def _slab(q):
    return slice(q * LANES, (q + 1) * LANES)


def _cmul(ar, ai, br, bi):
    return ar * br - ai * bi, ar * bi + ai * br


def _sub_shift(x, k, up):
    row = _iota(x.shape, 0)
    if up:
        return jnp.where(row >= N_SEG - k, 0.0, pltpu.roll(x, N_SEG - k, 0))
    return jnp.where(row < k, 0.0, pltpu.roll(x, k, 0))


def _seg_rows(j):
    return pl.ds(pl.multiple_of(j * N_SEG, N_SEG), N_SEG)


def _interleave(src, dst, seg):
    def step(j, carry):
        dst[_seg_rows(j), :] = src[pl.ds(j, N_SEG, stride=seg), :]
        return carry
    lax.fori_loop(0, seg, step, 0)


def _deinterleave(src, dst, seg):
    def step(j, carry):
        dst[pl.ds(j, N_SEG, stride=seg), :] = src[_seg_rows(j), :]
        return carry
    lax.fori_loop(0, seg, step, 0)


def _scan(xr, xi, ar_row, ai_row, seg, reverse):
    nlog = int(math.log2(seg))
    assert (1 << nlog) == seg
    for q0 in range(0, N_SLAB, 4):
        qs = list(range(q0, q0 + 4))
        aq = [(jnp.broadcast_to(ar_row[:, _slab(q)], (N_SEG, LANES)),
               jnp.broadcast_to(ai_row[:, _slab(q)], (N_SEG, LANES))) for q in qs]
        zero = jnp.zeros((N_SEG, LANES), F32)

        def local(jj, carry, qs=qs, aq=aq):
            j = seg - 1 - jj if reverse else jj
            out = []
            for n, q in enumerate(qs):
                rows = _seg_rows(j)
                pr, pi = _cmul(aq[n][0], aq[n][1], carry[2 * n], carry[2 * n + 1])
                nr = pr + xr[q, rows, :]
                ni = pi + xi[q, rows, :]
                xr[q, rows, :] = nr
                xi[q, rows, :] = ni
                out += [nr, ni]
            return tuple(out)

        fin = lax.fori_loop(0, seg, local, (zero,) * 8)
        cins = []
        for n in range(4):
            er, ei = fin[2 * n], fin[2 * n + 1]
            pr, pi = aq[n]
            for _ in range(nlog):
                pr, pi = _cmul(pr, pi, pr, pi)
            yr, yi = er, ei
            for k in (1, 2, 4):
                sr, si = _cmul(pr, pi, _sub_shift(yr, k, reverse), _sub_shift(yi, k, reverse))
                yr, yi = yr + sr, yi + si
                pr, pi = _cmul(pr, pi, pr, pi)
            cins.append((_sub_shift(yr, 1, reverse), _sub_shift(yi, 1, reverse)))

        def fix(jj, carry, qs=qs, aq=aq, cins=cins):
            j = seg - 1 - jj if reverse else jj
            out = []
            for n, q in enumerate(qs):
                rows = _seg_rows(j)
                pwr, pwi = carry[2 * n], carry[2 * n + 1]
                cr, ci = _cmul(pwr, pwi, cins[n][0], cins[n][1])
                xr[q, rows, :] += cr
                xi[q, rows, :] += ci
                nr, ni = _cmul(pwr, pwi, aq[n][0], aq[n][1])
                out += [nr, ni]
            return tuple(out)

        lax.fori_loop(0, seg, fix, tuple(v for pair in aq for v in pair))


def _s5_forward_states(u, btr, bti, ar_row, ai_row, xr, xi, seg):
    ub = u.astype(BF16)
    for q in range(N_SLAB):
        xr[q] = _dot_nt(ub, btr[_slab(q), :])
        xi[q] = _dot_nt(ub, bti[_slab(q), :])
    _scan(xr, xi, ar_row, ai_row, seg, False)


def _s5_readout(u, xr, xi, ctr, cti, d):
    y = d * u
    for q in range(N_SLAB):
        y = y + _dot_nt(xr[q], ctr[:, _slab(q)]) - _dot_nt(xi[q], cti[:, _slab(q)])
    return y


def _s5_param_specs():
    return [pl.BlockSpec((3, HALF_STATE), lambda i: (0, i)), pl.BlockSpec((HALF_STATE, 3), lambda i: (i, 0)),
            pl.BlockSpec((HALF_STATE, 16), lambda i: (i, 0)), pl.BlockSpec((HALF_STATE, 16), lambda i: (i, 0)),
            pl.BlockSpec((HALF_CH, 64), lambda i: (i, 0)), pl.BlockSpec((HALF_CH, 64), lambda i: (i, 0)),
            pl.BlockSpec((1, HALF_CH), lambda i: (0, i))]


def _s5_core_fwd(z, sp):
    s = z.shape[0]
    seg = s // N_SEG

    def body(u_ref, rowp, colp, br, bi, cr, ci, d_ref, y_ref, xr, xi, us, ys):
        ar, ai, _, _ = _s5_disc(rowp[0:1, :], rowp[1:2, :], rowp[2:3, :])
        _, _, btr, bti, ctr, cti, _, _ = _s5_mats(colp[...], br[...], bi[...], cr[...], ci[...])
        _interleave(u_ref, us, seg)
        u = us[...]
        _s5_forward_states(u, btr, bti, ar, ai, xr, xi, seg)
        ys[...] = _s5_readout(u, xr, xi, ctr, cti, d_ref[...])
        _deinterleave(ys, y_ref, seg)

    return pl.pallas_call(
        body, name="s5_core_fwd", grid=(2,),
        in_specs=[pl.BlockSpec((s, HALF_CH), lambda i: (0, 12 + i))] + _s5_param_specs(),
        out_specs=pl.BlockSpec((s, HALF_CH), lambda i: (0, i)),
        out_shape=jax.ShapeDtypeStruct((s, W_GRP), F32),
        scratch_shapes=[pltpu.VMEM((N_SLAB, s, LANES), F32)] * 2 + [pltpu.VMEM((s, HALF_CH), F32)] * 2,
        compiler_params=_cp(dimension_semantics=("parallel",)),
    )(z, *sp)


def _s5_glu_fwd(y, gw, gb):
    s = y.shape[0]
    tm = _tm(s)

    def body(y_ref, gw_ref, gb_ref, o_ref):
        yg, _ = _gelu(y_ref[...])
        o_ref[...] = yg * jax.nn.sigmoid(_dot(yg, gw_ref[...]) + gb_ref[...])

    blk = pl.BlockSpec((tm, W_GRP), lambda i: (i, 0))
    return pl.pallas_call(
        body, name="s5_glu_fwd", grid=(s // tm,),
        in_specs=[blk, _full((W_GRP, W_GRP)), _full((1, W_GRP))], out_specs=blk,
        out_shape=jax.ShapeDtypeStruct((s, W_GRP), F32),
        compiler_params=_cp(dimension_semantics=("parallel",)),
    )(y, gw, gb)


def _s5_glu_bwd(y, dyd, gw, gb):
    s = y.shape[0]
    tm = _tm(s)

    def body(y_ref, dyd_ref, gw_ref, gb_ref, dy_ref, dgw_ref, dgb_ref):
        i = pl.program_id(0)

        @pl.when(i == 0)
        def _():
            dgw_ref[...] = jnp.zeros_like(dgw_ref)
            dgb_ref[...] = jnp.zeros_like(dgb_ref)

        y, gw, dyd = y_ref[...], gw_ref[...], dyd_ref[...]
        yg, t = _gelu(y)
        gate = jax.nn.sigmoid(_dot(yg, gw) + gb_ref[...])
        dlin = dyd * yg * gate * (1.0 - gate)
        dgw_ref[...] += _dot_tn(yg, dlin)
        dgb_ref[...] += jnp.sum(dlin, axis=0, keepdims=True)
        dy_ref[...] = (dyd * gate + _dot_nt(dlin, gw)) * _gelu_grad(y, t)

    blk = pl.BlockSpec((tm, W_GRP), lambda i: (i, 0))
    return pl.pallas_call(
        body, name="s5_glu_bwd", grid=(s // tm,),
        in_specs=[blk, blk, _full((W_GRP, W_GRP)), _full((1, W_GRP))],
        out_specs=[blk, _full((W_GRP, W_GRP)), _full((1, W_GRP))],
        out_shape=[jax.ShapeDtypeStruct((s, W_GRP), F32), jax.ShapeDtypeStruct((W_GRP, W_GRP), F32),
                   jax.ShapeDtypeStruct((1, W_GRP), F32)],
        compiler_params=_cp(dimension_semantics=("arbitrary",)),
    )(y, dyd, gw, gb)


def _s5_core_bwd(z, dy, sp):
    s = z.shape[0]
    seg = s // N_SEG

    def body(u_ref, dy_ref, rowp, colp, br_ref, bi_ref, cr_ref, ci_ref, d_ref,
             du_ref, dbr_ref, dbi_ref, dcr_ref, dci_ref, dd_ref, da_ref, dk_ref,
             xr, xi, gr, gi, us, dys):
        ar, ai, _, _ = _s5_disc(rowp[0:1, :], rowp[1:2, :], rowp[2:3, :])
        br, bi = br_ref[...], bi_ref[...]
        kr, ki, btr, bti, ctr, cti, bmask, cmask = _s5_mats(colp[...], br, bi, cr_ref[...], ci_ref[...])
        _interleave(u_ref, us, seg)
        _interleave(dy_ref, dys, seg)
        u = us[...]
        d = d_ref[...]
        _s5_forward_states(u, btr, bti, ar, ai, xr, xi, seg)

        dy = dys[...]
        dd_ref[...] = jnp.sum(dy * u, axis=0, keepdims=True)
        du = d * dy
        dyb = dy.astype(BF16)
        dctr, dcti = [], []
        for q in range(N_SLAB):
            gr[q] = jnp.dot(dyb, ctr[:, _slab(q)], preferred_element_type=F32)
            gi[q] = -jnp.dot(dyb, cti[:, _slab(q)], preferred_element_type=F32)
            dctr.append(_dot_tn(dyb, xr[q]))
            dcti.append(-_dot_tn(dyb, xi[q]))
        selp = ((_iota((HALF_STATE, 64), 0) & 63) == _iota((HALF_STATE, 64), 1)).astype(F32)
        dcr_ref[...] = _dot_hi(jnp.where(cmask, jnp.concatenate(dctr, axis=1), 0.0), selp)
        dci_ref[...] = _dot_hi(jnp.where(cmask, jnp.concatenate(dcti, axis=1), 0.0), selp)

        _scan(gr, gi, ar, -ai, seg, True)

        dar, dai = [], []
        for q in range(N_SLAB):
            def acc_step(j, carry, q=q):
                rows, prev = _seg_rows(j), _seg_rows(j - 1)
                g_r, g_i, p_r, p_i = gr[q, rows, :], gi[q, rows, :], xr[q, prev, :], xi[q, prev, :]
                return carry[0] + g_r * p_r + g_i * p_i, carry[1] - g_r * p_i + g_i * p_r
            first, last = _seg_rows(0), _seg_rows(seg - 1)
            p_r, p_i = _sub_shift(xr[q, last, :], 1, False), _sub_shift(xi[q, last, :], 1, False)
            g_r, g_i = gr[q, first, :], gi[q, first, :]
            s_r, s_i = lax.fori_loop(1, seg, acc_step, (g_r * p_r + g_i * p_i, -g_r * p_i + g_i * p_r))
            dar.append(jnp.sum(s_r, axis=0, keepdims=True))
            dai.append(jnp.sum(s_i, axis=0, keepdims=True))
        da_ref[...] = jnp.zeros_like(da_ref)
        da_ref[0:1, :] = jnp.concatenate(dar, axis=1)
        da_ref[1:2, :] = jnp.concatenate(dai, axis=1)

        ub = u.astype(BF16)
        dbtr, dbti = [], []
        for q in range(N_SLAB):
            g_r, g_i = gr[q].astype(BF16), gi[q].astype(BF16)
            du = du + jnp.dot(g_r, btr[_slab(q), :], preferred_element_type=F32) \
                + jnp.dot(g_i, bti[_slab(q), :], preferred_element_type=F32)
            dbtr.append(_dot_tn(g_r, ub))
            dbti.append(_dot_tn(g_i, ub))
        us[...] = du
        _deinterleave(us, du_ref, seg)
        selc =((_iota((HALF_CH, 16), 0) & 15) == _iota((HALF_CH, 16), 1)).astype(F32)
        dbbr = _dot_hi(jnp.where(bmask, jnp.concatenate(dbtr, axis=0), 0.0), selc)
        dbbi = _dot_hi(jnp.where(bmask, jnp.concatenate(dbti, axis=0), 0.0), selc)
        dbr_ref[...] = kr * dbbr + ki * dbbi
        dbi_ref[...] = kr * dbbi - ki * dbbr
        dk_ref[:, 0:1] = jnp.sum(dbbr * br + dbbi * bi, axis=1, keepdims=True)
        dk_ref[:, 1:2] = jnp.sum(dbbi * br - dbbr * bi, axis=1, keepdims=True)

    half = pl.BlockSpec((s, HALF_CH), lambda i: (0, i))
    return pl.pallas_call(
        body, name="s5_core_bwd", grid=(2,),
        in_specs=[pl.BlockSpec((s, HALF_CH), lambda i: (0, 12 + i)), half] + _s5_param_specs(),
        out_specs=[half, pl.BlockSpec((HALF_STATE, 16), lambda i: (i, 0)), pl.BlockSpec((HALF_STATE, 16), lambda i: (i, 0)),
                   pl.BlockSpec((HALF_CH, 64), lambda i: (i, 0)), pl.BlockSpec((HALF_CH, 64), lambda i: (i, 0)),
                   pl.BlockSpec((1, HALF_CH), lambda i: (0, i)), pl.BlockSpec((8, HALF_STATE), lambda i: (0, i)),
                   pl.BlockSpec((HALF_STATE, 2), lambda i: (i, 0))],
        out_shape=[jax.ShapeDtypeStruct((s, W_GRP), F32), jax.ShapeDtypeStruct((N_STATE, 16), F32),
                   jax.ShapeDtypeStruct((N_STATE, 16), F32), jax.ShapeDtypeStruct((W_GRP, 64), F32),
                   jax.ShapeDtypeStruct((W_GRP, 64), F32), jax.ShapeDtypeStruct((1, W_GRP), F32),
                   jax.ShapeDtypeStruct((8, N_STATE), F32), jax.ShapeDtypeStruct((N_STATE, 2), F32)],
        scratch_shapes=[pltpu.VMEM((N_SLAB, s, LANES), F32)] * 4 + [pltpu.VMEM((s, HALF_CH), F32)] * 2,
        compiler_params=_cp(dimension_semantics=("parallel",)),
    )(z, dy, *sp)


def _s5_param_bwd(lre, lim, ldt, da_r, da_i, dk_r, dk_i):
    n = lre.shape[0]

    def body(lre_ref, lim_ref, ldt_ref, dar_ref, dai_ref, dkr_ref, dki_ref, o_re, o_im, o_dt):
        lre, lim, ldt = lre_ref[...], lim_ref[...], ldt_ref[...]
        dt = jnp.exp(ldt)
        ar, ai, kr, ki = _s5_disc(lre, lim, ldt)
        mag = jnp.exp(lre * dt)
        den = lre * lre + lim * lim
        dkr, dki = dkr_ref[...], dki_ref[...]
        nr, ni = ar - 1.0, ai
        d_ar = dar_ref[...] + (dkr * lre - dki * lim) / den
        d_ai = dai_ref[...] + (dkr * lim + dki * lre) / den
        kk = (kr * dkr + ki * dki) * 2.0 / den
        d_lre = (dkr * nr + dki * ni) / den - kk * lre
        d_lim = (dkr * ni - dki * nr) / den - kk * lim
        d_mag = (d_ar * ar + d_ai * ai) / mag
        d_ang = d_ai * ar - d_ar * ai
        o_re[...] = d_lre + d_mag * mag * dt
        o_im[...] = d_lim + d_ang * dt
        o_dt[...] = jnp.sum((d_mag * mag * lre + d_ang * lim) * dt, axis=1, keepdims=True)

    return pl.pallas_call(
        body, name="s5_param_bwd",
        out_shape=[jax.ShapeDtypeStruct((n, 64), F32), jax.ShapeDtypeStruct((n, 64), F32),
                   jax.ShapeDtypeStruct((n, 1), F32)],
    )(lre, lim, ldt, da_r, da_i, dk_r, dk_i)


def _loss_head(x, fg, target):
    s, d = x.shape
    tm = _tm(s)

    def body(x_ref, fg_ref, t_ref, loss_ref, dx_ref, dfg_ref):
        i = pl.program_id(0)

        @pl.when(i == 0)
        def _():
            loss_ref[...] = jnp.zeros_like(loss_ref)
            dfg_ref[...] = jnp.zeros_like(dfg_ref)

        xv, g = x_ref[...], fg_ref[...]
        r = lax.rsqrt(jnp.mean(xv * xv, axis=-1, keepdims=True) + EPS)
        xh = xv * r
        err = xh * g - t_ref[...]
        loss_ref[...] += 0.5 * jnp.sum(jnp.mean(err * err, axis=-1, keepdims=True), axis=0, keepdims=True)
        dy = err * (1.0 / d)
        dfg_ref[...] += jnp.sum(dy * xh, axis=0, keepdims=True)
        dxh = dy * g
        dx_ref[...] = r * (dxh - xh * jnp.mean(dxh * xh, axis=-1, keepdims=True))

    row = pl.BlockSpec((tm, d), lambda i: (i, 0))
    return pl.pallas_call(
        body, name="loss_head", grid=(s // tm,),
        in_specs=[row, _full((1, d)), row], out_specs=[_full((1, 1)), row, _full((1, d))],
        out_shape=[jax.ShapeDtypeStruct((1, 1), F32), jax.ShapeDtypeStruct((s, d), F32),
                   jax.ShapeDtypeStruct((1, d), F32)],
        compiler_params=_cp(dimension_semantics=("arbitrary",)),
    )(x, fg, target)


ADA_TN = 384


def _cond_fwd(cact, ada_w, ada_b_loc):
    nl, d, n = ada_w.shape

    def body(c_ref, w_ref, b_ref, o_ref):
        o_ref[...] = _dot(c_ref[...], w_ref[...]) + b_ref[...]

    return pl.pallas_call(
        body, name="cond_fwd", grid=(nl, n // ADA_TN),
        in_specs=[_full((N_DEV, d)), pl.BlockSpec((None, d, ADA_TN), lambda l, j: (l, 0, j)),
                  pl.BlockSpec((None, 1, ADA_TN), lambda l, j: (l, 0, j))],
        out_specs=pl.BlockSpec((None, N_DEV, ADA_TN), lambda l, j: (l, 0, j)),
        out_shape=jax.ShapeDtypeStruct((nl, N_DEV, n), F32),
        compiler_params=_cp(dimension_semantics=("parallel", "parallel")),
    )(cact, ada_w, ada_b_loc)


ELEMENTWISE_BLOCK_BYTES = 1 << 20


def _row_tile(r, c, itemsize=4):
    best = None
    for t in range(8, r + 1, 8):
        if r % t == 0 and t * c * itemsize <= ELEMENTWISE_BLOCK_BYTES:
            best = t
    return best if best is not None else r


def _adamw_math(w, g, m, v):
    m = ADAM_B1 * m + (1.0 - ADAM_B1) * g
    v = ADAM_B2 * v + (1.0 - ADAM_B2) * (g * g)
    m_hat = m / (1.0 - ADAM_B1 ** ADAM_STEP)
    v_hat = v / (1.0 - ADAM_B2 ** ADAM_STEP)
    delta = -ADAM_LR * (m_hat / (jnp.sqrt(v_hat) + ADAM_EPS) + ADAM_WD * w)
    return delta, m, v


def _ada_w_update(cact, dcond_loc, w, m, v):
    nl, d, n = w.shape

    def body(c_ref, dc_ref, w_ref, m_ref, v_ref, g_out, d_out, m_out, v_out):
        g = _dot_tn(c_ref[...], dc_ref[...])
        g_out[...] = g
        d_out[...], m_out[...], v_out[...] = _adamw_math(w_ref[...], g, m_ref[...], v_ref[...])

    blk = pl.BlockSpec((None, d, ADA_TN), lambda l, j: (l, 0, j))
    return pl.pallas_call(
        body, name="ada_w_update", grid=(nl, n // ADA_TN),
        in_specs=[_full((N_DEV, d)), pl.BlockSpec((None, N_DEV, ADA_TN), lambda l, j: (l, 0, j)), blk, blk, blk],
        out_specs=[blk] * 4, out_shape=[jax.ShapeDtypeStruct((nl, d, n), F32)] * 4,
        compiler_params=_cp(dimension_semantics=("parallel", "parallel")),
    )(cact, dcond_loc, w, m, v)


def _adamw(w, g, m, v, name):
    b, r, c = w.shape
    tr = _row_tile(r, c)

    def body(w_ref, g_ref, m_ref, v_ref, d_out, m_out, v_out):
        d_out[...], m_out[...], v_out[...] = _adamw_math(w_ref[...], g_ref[...], m_ref[...], v_ref[...])

    blk = pl.BlockSpec((None, tr, c), lambda i, j: (i, j, 0))
    return pl.pallas_call(
        body, name=name, grid=(b, r // tr), in_specs=[blk] * 4, out_specs=[blk] * 3,
        out_shape=[jax.ShapeDtypeStruct((b, r, c), F32)] * 3,
        compiler_params=_cp(dimension_semantics=("parallel", "parallel")),
    )(w, g, m, v)


def _place():
    x, y, c = lax.axis_index("x"), lax.axis_index("y"), lax.axis_index("c")
    chips = [(1 - x, y), (x, 1 - y), (1 - x, 1 - y)]
    return x, y, c, chips


def _remote(src, dst, send_sem, recv_sem, to):
    return pltpu.make_async_remote_copy(src_ref=src, dst_ref=dst, send_sem=send_sem, recv_sem=recv_sem,
                                        device_id=to, device_id_type=MESH_ID)


def _sems(n):
    return [pltpu.SemaphoreType.DMA((n,)), pltpu.SemaphoreType.DMA((n,))]


def _all_gather8(v, name):
    r, cdim = v.shape

    def body(x_ref, out_ref, stage, send_sems, recv_sems):
        x, y, c, chips = _place()
        sibling = (x, y, 1 - c)

        def slot(px, py, pc):
            return out_ref.at[4 * px + 2 * py + pc]

        first = [_remote(x_ref, slot(x, y, c), send_sems.at[0], recv_sems.at[0], sibling)]
        first += [_remote(x_ref, slot(x, y, c), send_sems.at[1 + j], recv_sems.at[1 + j], (*chip, c))
                  for j, chip in enumerate(chips)]
        for cp in first:
            cp.start()
        pltpu.sync_copy(x_ref, stage)
        pltpu.sync_copy(stage, slot(x, y, c))
        passed = []
        for j, chip in enumerate(chips):
            blk = slot(*chip, c)
            _remote(blk, blk, send_sems.at[1 + j], recv_sems.at[1 + j], (x, y, c)).wait_recv()
            fw = _remote(blk, blk, send_sems.at[4 + j], recv_sems.at[4 + j], sibling)
            fw.start()
            passed.append(fw)
        blk = slot(x, y, 1 - c)
        _remote(blk, blk, send_sems.at[0], recv_sems.at[0], (x, y, c)).wait_recv()
        for j, chip in enumerate(chips):
            blk = slot(*chip, 1 - c)
            _remote(blk, blk, send_sems.at[4 + j], recv_sems.at[4 + j], (x, y, c)).wait_recv()
        for cp in first + passed:
            cp.wait_send()

    return pl.pallas_call(
        body, name=name, out_shape=jax.ShapeDtypeStruct((N_DEV, r, cdim), v.dtype),
        in_specs=[ANY], out_specs=ANY,
        scratch_shapes=[pltpu.VMEM((r, cdim), v.dtype)] + _sems(7),
        compiler_params=_cp(),
    )(v)


def _place_weights(ws, layer, kidx):
    steps = 4
    shapes, in_specs, out_specs = [], [], []
    for w, kind in zip(ws, BIG_KINDS):
        _, a, b = w.shape
        in_specs.append(pl.BlockSpec((None, a // steps, b), lambda i, k: (layer, i, 0)))
        if kind == "col":
            shapes.append((2, a, 2 * b))
            out_specs.append(pl.BlockSpec((None, a // steps, b), lambda i, k: (k[0] // 2, i, k[0] % 2)))
        else:
            shapes.append((N_CHIP, a, b))
            out_specs.append(pl.BlockSpec((None, a // steps, b), lambda i, k: (k[0], i, 0)))

    def body(k_ref, *refs):
        for t in range(len(ws)):
            refs[len(ws) + t][...] = refs[t][...].astype(BF16)

    return pl.pallas_call(
        body, name="place_weights", out_shape=[jax.ShapeDtypeStruct(s, BF16) for s in shapes],
        grid_spec=pltpu.PrefetchScalarGridSpec(num_scalar_prefetch=1, grid=(steps,), in_specs=in_specs,
                                               out_specs=out_specs),
        compiler_params=_cp(dimension_semantics=("parallel",)),
    )(kidx, *ws)


HBM = pl.BlockSpec(memory_space=pltpu.HBM)
SEM = pl.BlockSpec(memory_space=pltpu.SEMAPHORE)
EFFECT = pltpu.SideEffectType.DATAFLOW_SIDE_EFFECTING


def _weight_block(ref, kind, k, h):
    if kind == "col":
        ncol = ref.shape[3] // 2
        return ref.at[k // 2, h, :, pl.ds(pl.multiple_of((k % 2) * ncol, LANES), ncol)]
    return ref.at[k, h]


def _in_hbm(a):
    return pltpu.with_memory_space_constraint(a, pltpu.HBM)


def _weight_send_start(placed, kinds, name):
    nt = len(placed)

    def body(*refs):
        send_sems, recv_sems = refs[nt], refs[nt + 1]
        dst = refs[nt + 2:2 * nt + 2]
        token = refs[2 * nt + 2]
        x, y, c, chips = _place()
        kme = 2 * x + y
        for t in range(nt):
            for j, chip in enumerate(chips):
                own = _weight_block(dst[t], kinds[t], kme, c)
                _remote(own, own, send_sems.at[3 * t + j], recv_sems.at[3 * t + j], (*chip, c)).start()
        token[...] = jnp.zeros_like(token)

    return pl.pallas_call(
        body, name=name,
        out_shape=(pltpu.SemaphoreType.DMA((3 * nt,)), pltpu.SemaphoreType.DMA((3 * nt,)),
                   *[pltpu.HBM(a.shape, a.dtype) for a in placed], jax.ShapeDtypeStruct((8, LANES), F32)),
        in_specs=[HBM] * nt, out_specs=(SEM, SEM, *[HBM] * nt, pl.BlockSpec(memory_space=pltpu.VMEM)),
        input_output_aliases={t: 2 + t for t in range(nt)},
        compiler_params=pltpu.CompilerParams(has_side_effects=EFFECT),
    )(*[_in_hbm(a) for a in placed])


def _weight_send_wait(send_sems, recv_sems, arrays, kinds, after, name):
    nt = len(arrays)

    def body(*refs):
        arr = refs[:nt]
        send_sems, recv_sems = refs[nt], refs[nt + 1]
        x, y, c, chips = _place()
        kme = 2 * x + y
        for t in range(nt):
            for j, chip in enumerate(chips):
                own = _weight_block(arr[t], kinds[t], kme, c)
                got = _weight_block(arr[t], kinds[t], 2 * chip[0] + chip[1], c)
                cp = _remote(own, got, send_sems.at[3 * t + j], recv_sems.at[3 * t + j], (*chip, c))
                cp.wait_send()
                cp.wait_recv()

    return pl.pallas_call(
        body, name=name, out_shape=[pltpu.HBM(a.shape, a.dtype) for a in arrays],
        in_specs=[HBM] * nt + [SEM, SEM, ANY], out_specs=[HBM] * nt,
        input_output_aliases={t: t for t in range(nt)},
        compiler_params=pltpu.CompilerParams(has_side_effects=EFFECT),
    )(*arrays, send_sems, recv_sems, after)


def _weight_forward(arrays, kinds):
    nt = len(arrays)

    def body(*refs):
        dst = refs[nt:2 * nt]
        send_sems, recv_sems = refs[2 * nt:]
        x, y, c, chips = _place()
        sends = []
        for t in range(nt):
            for j, chip in enumerate(chips):
                blk = _weight_block(dst[t], kinds[t], 2 * chip[0] + chip[1], c)
                fw = _remote(blk, blk, send_sems.at[3 * t + j], recv_sems.at[3 * t + j], (x, y, 1 - c))
                fw.start()
                sends.append(fw)
        for t in range(nt):
            for j, chip in enumerate(chips):
                blk = _weight_block(dst[t], kinds[t], 2 * chip[0] + chip[1], 1 - c)
                _remote(blk, blk, send_sems.at[3 * t + j], recv_sems.at[3 * t + j], (x, y, c)).wait_recv()
        for cp in sends:
            cp.wait_send()

    return pl.pallas_call(
        body, name="weight_forward",
        out_shape=[jax.ShapeDtypeStruct(a.shape, a.dtype) for a in arrays],
        in_specs=[ANY] * nt, out_specs=[ANY] * nt, input_output_aliases={t: t for t in range(nt)},
        scratch_shapes=_sems(3 * nt),
    )(*arrays)


def _sibling_exchange(views):
    nt = len(views)

    def body(*refs):
        src, land = refs[:nt], refs[nt:2 * nt]
        send_sems, recv_sems = refs[2 * nt:]
        x, y, c, _ = _place()
        cps = [_remote(src[t].at[:, 1 - c], land[t], send_sems.at[t], recv_sems.at[t], (x, y, 1 - c))
               for t in range(nt)]
        for cp in cps:
            cp.start()
        for cp in cps:
            cp.wait()

    return pl.pallas_call(
        body, name="grad_sibling_exchange",
        out_shape=[jax.ShapeDtypeStruct((v.shape[0],) + v.shape[2:], v.dtype) for v in views],
        in_specs=[ANY] * nt, out_specs=[ANY] * nt, scratch_shapes=_sems(nt),
    )(*views)


def _scatter_copies(src, land, kinds, send_sems, recv_sems):
    x, y, c, chips = _place()
    cps = []
    for t in range(len(src)):
        for j, chip in enumerate(chips):
            k = 2 * chip[0] + chip[1]
            if kinds[t] == "col":
                ncol = land[t].shape[2]
                win = src[t].at[k // 2, :, pl.ds(pl.multiple_of((k % 2) * ncol, LANES), ncol)]
            else:
                win = src[t].at[k]
            cps.append(_remote(win, land[t].at[j], send_sems.at[3 * t + j], recv_sems.at[3 * t + j], (*chip, c)))
    return cps


def _chip_scatter_start(parts, kinds, name):
    nt = len(parts)
    shapes = []
    for p, kind in zip(parts, kinds):
        shapes.append((3, p.shape[1], p.shape[2] // 2) if kind == "col" else (3,) + p.shape[1:])

    def body(*refs):
        send_sems, recv_sems = refs[2 * nt], refs[2 * nt + 1]
        src, land = refs[2 * nt + 2:3 * nt + 2], refs[3 * nt + 2:4 * nt + 2]
        token = refs[4 * nt + 2]
        for cp in _scatter_copies(src, land, kinds, send_sems, recv_sems):
            cp.start()
        token[...] = jnp.zeros_like(token)

    lands = [lax.empty(s, BF16) for s in shapes]
    return pl.pallas_call(
        body, name=name,
        out_shape=(pltpu.SemaphoreType.DMA((3 * nt,)), pltpu.SemaphoreType.DMA((3 * nt,)),
                   *[pltpu.HBM(a.shape, a.dtype) for a in parts], *[pltpu.HBM(s, BF16) for s in shapes],
                   jax.ShapeDtypeStruct((8, LANES), F32)),
        in_specs=[HBM] * (2 * nt), out_specs=(SEM, SEM, *[HBM] * (2 * nt), pl.BlockSpec(memory_space=pltpu.VMEM)),
        input_output_aliases={t: 2 + t for t in range(2 * nt)},
        compiler_params=pltpu.CompilerParams(has_side_effects=EFFECT),
    )(*[_in_hbm(a) for a in parts], *[_in_hbm(a) for a in lands])


def _chip_scatter_wait(send_sems, recv_sems, parts, lands, kinds, after, name):
    nt = len(parts)

    def body(*refs):
        src, land = refs[:nt], refs[nt:2 * nt]
        send_sems, recv_sems = refs[2 * nt], refs[2 * nt + 1]
        for cp in _scatter_copies(src, land, kinds, send_sems, recv_sems):
            cp.wait_send()
            cp.wait_recv()

    outs = pl.pallas_call(
        body, name=name, out_shape=[pltpu.HBM(a.shape, a.dtype) for a in list(parts) + list(lands)],
        in_specs=[HBM] * (2 * nt) + [SEM, SEM, ANY], out_specs=[HBM] * (2 * nt),
        input_output_aliases={t: t for t in range(2 * nt)},
        compiler_params=pltpu.CompilerParams(has_side_effects=EFFECT),
    )(*parts, *lands, send_sems, recv_sems, after)
    return outs[:nt], outs[nt:]


def _sibling_share(fulls):
    nt = len(fulls)

    def body(*refs):
        dst = refs[nt:2 * nt]
        send_sems, recv_sems = refs[2 * nt:]
        x, y, c, _ = _place()
        cps = []
        for t in range(nt):
            mine = dst[t].at[c]
            cp = _remote(mine, mine, send_sems.at[t], recv_sems.at[t], (x, y, 1 - c))
            cp.start()
            cps.append(cp)
        for t in range(nt):
            other = dst[t].at[1 - c]
            _remote(other, other, send_sems.at[t], recv_sems.at[t], (x, y, c)).wait_recv()
        for cp in cps:
            cp.wait_send()

    return pl.pallas_call(
        body, name="grad_sibling_share",
        out_shape=[jax.ShapeDtypeStruct(f.shape, f.dtype) for f in fulls],
        in_specs=[ANY] * nt, out_specs=[ANY] * nt, input_output_aliases={t: t for t in range(nt)},
        scratch_shapes=_sems(nt),
    )(*fulls)


SUM_STEPS = 4


def _pair_sum(views, lands, ck):
    nt = len(views)
    in_specs, out_specs, shapes = [], [], []
    for v in views:
        b, _, r, cc = v.shape
        per = SUM_STEPS // b
        tr = r // per
        in_specs.append(pl.BlockSpec((None, None, tr, cc), lambda i, s, per=per: (i // per, s[0], i % per, 0)))
        out_specs.append(pl.BlockSpec((None, tr, cc), lambda i, s, per=per: (i // per, i % per, 0)))
        shapes.append((b, r, cc))
    in_specs = in_specs + out_specs

    def body(s_ref, *refs):
        for t in range(nt):
            refs[2 * nt + t][...] = (refs[t][...].astype(F32) + refs[nt + t][...].astype(F32)).astype(BF16)

    return pl.pallas_call(
        body, name="grad_pair_sum", out_shape=[jax.ShapeDtypeStruct(s, BF16) for s in shapes],
        grid_spec=pltpu.PrefetchScalarGridSpec(num_scalar_prefetch=1, grid=(SUM_STEPS,), in_specs=in_specs,
                                               out_specs=out_specs),
        compiler_params=_cp(dimension_semantics=("parallel",)),
    )(ck, *views, *lands)


def _chip_sum(parts, lands, kinds, ck):
    nt = len(parts)
    steps = 2
    in_own, in_land, out_specs, shapes = [], [], [], []
    for ld, kind in zip(lands, kinds):
        _, r, cc = ld.shape
        tr = r // steps
        if kind == "col":
            in_own.append(pl.BlockSpec((None, tr, cc), lambda i, s: (s[1] // 2, i, s[1] % 2)))
        else:
            in_own.append(pl.BlockSpec((None, tr, cc), lambda i, s: (s[1], i, 0)))
        in_land.append(pl.BlockSpec((3, tr, cc), lambda i, s: (0, i, 0)))
        out_specs.append(pl.BlockSpec((None, tr, cc), lambda i, s: (s[0], i, 0)))
        shapes.append((2, r, cc))

    def body(s_ref, *refs):
        for t in range(nt):
            acc = refs[t][...].astype(F32)
            for j in range(3):
                acc = acc + refs[nt + t][j].astype(F32)
            refs[2 * nt + t][...] = acc

    return pl.pallas_call(
        body, name="grad_chip_sum", out_shape=[jax.ShapeDtypeStruct(s, F32) for s in shapes],
        grid_spec=pltpu.PrefetchScalarGridSpec(num_scalar_prefetch=1, grid=(steps,), in_specs=in_own + in_land,
                                               out_specs=out_specs),
        compiler_params=_cp(dimension_semantics=("parallel",)),
    )(ck, *parts, *lands)


def _sum8(g):
    _, r, cc = g.shape
    tr = _row_tile(r, N_DEV * cc)

    def body(g_ref, o_ref):
        acc = g_ref[0]
        for d in range(1, N_DEV):
            acc = acc + g_ref[d]
        o_ref[...] = acc

    return pl.pallas_call(
        body, name="small_grad_sum", grid=(r // tr,),
        in_specs=[pl.BlockSpec((N_DEV, tr, cc), lambda i: (0, i, 0))],
        out_specs=pl.BlockSpec((tr, cc), lambda i: (i, 0)),
        out_shape=jax.ShapeDtypeStruct((r, cc), F32),
        compiler_params=_cp(dimension_semantics=("parallel",)),
    )(g)


def _silu_rows(c):
    def body(c_ref, o_ref):
        v = c_ref[...]
        o_ref[...] = v * jax.nn.sigmoid(v)

    return pl.pallas_call(body, name="cond_silu", out_shape=jax.ShapeDtypeStruct(c.shape, F32))(c)


def _pack(arrays):
    flat = jnp.concatenate([a.reshape(-1) for a in arrays])
    n = flat.shape[0]
    pad = (-n) % (256 * LANES)
    return jnp.pad(flat, (0, pad)).reshape(-1, LANES)


def _unpack(packed, shapes):
    flat = packed.reshape(-1)
    out, off = [], 0
    for s in shapes:
        n = math.prod(s)
        out.append(flat[off:off + n].reshape(s))
        off += n
    return out


def _reduce_big_grads(grads, kinds, ck, layer):
    views = []
    for g, kind in zip(grads, kinds):
        if kind == "col":
            views.append(g.reshape(2, 2, g.shape[1] // 2, g.shape[2]))
        else:
            views.append(g.reshape(N_CHIP, 2, g.shape[0] // (2 * N_CHIP), g.shape[1]))
    lands = _sibling_exchange(views)
    parts = _pair_sum(views, lands, ck)
    return _chip_scatter_start(parts, kinds, "grad_scatter_start_%d" % layer)


def _finish_big_grads(started, kinds, ck, after, layer):
    nt = len(kinds)
    send_sems, recv_sems = started[0], started[1]
    parts, lands = started[2:2 + nt], started[2 + nt:2 + 2 * nt]
    parts, lands = _chip_scatter_wait(send_sems, recv_sems, parts, lands, kinds, after, "grad_scatter_wait_%d" % layer)
    fulls = _sibling_share(_chip_sum(parts, lands, kinds, ck))
    return [f.reshape(2 * f.shape[1], f.shape[2]) for f in fulls]


SMALL_NAMES = ["ada_b", "norm1_g", "norm2_g", "sgu_w", "sgu_b", "pool_w", "pool_scale", "conv_w", "s5_lambda_re",
               "s5_lambda_im", "s5_b_re", "s5_b_im", "s5_c_re", "s5_c_im", "s5_d", "s5_log_dt", "s5_glu_w", "s5_glu_b",
               "mix_norm_g", "norm3_g", "final_norm_g"]
BIG_NAMES = ["ffn1_w_in", "ffn1_w_out", "w_mix_in", "w_mix_out", "ffn2_w_in", "ffn2_w_out"]
BIG_KINDS = ["col", "row", "row", "row", "col", "row"]
WEIGHT_ORDER = ["ada_w", "ada_b", "norm1_g", "ffn1_w_in", "ffn1_w_out", "norm2_g", "w_mix_in", "sgu_w", "sgu_b", "pool_w",
                "pool_scale", "conv_w", "s5_lambda_re", "s5_lambda_im", "s5_b_re", "s5_b_im", "s5_c_re", "s5_c_im", "s5_d",
                "s5_log_dt", "s5_glu_w", "s5_glu_b", "mix_norm_g", "w_mix_out", "norm3_g", "ffn2_w_in", "ffn2_w_out",
                "final_norm_g"]


def _local_step(x, target, cond, fetch_weights, p, emit_grads):
    nl, d = DEPTH, x.shape[1]
    row = lambda a: a.reshape(1, -1)
    saved = []
    for l in range(nl):
        (wi1, wo1, wmit, wmo, wi2, wo2), tok = fetch_weights(l, x)
        cl = cond[l] + tok
        mod1, mod2, mod3 = cl[0:3], cl[3:6], cl[6:9]
        lre, lim = p["s5_lambda_re"][l].reshape(-1), p["s5_lambda_im"][l].reshape(-1)
        ldt = jnp.repeat(p["s5_log_dt"][l], 64)
        rowp = jnp.stack([lre, lim, ldt])
        sp = (rowp, rowp.T, p["s5_b_re"][l].reshape(N_STATE, 16), p["s5_b_im"][l].reshape(N_STATE, 16),
              p["s5_c_re"][l].reshape(W_GRP, 64), p["s5_c_im"][l].reshape(W_GRP, 64), row(p["s5_d"][l]))
        glu = (p["s5_glu_w"][l], row(p["s5_glu_b"][l]))
        bias_full = jnp.repeat(p["sgu_b"][l].T, 64, axis=1)
        pw2 = p["pool_w"][l].reshape(W_GRP, 64)
        x1, h1, a1, b1, o1 = _ffn_fwd(x, mod1, row(p["norm1_g"][l]), wi1, wo1)
        z, h2 = _mix_in_fwd(x1, mod2, row(p["norm2_g"][l]), wmit)
        ya = _sgu_fwd(z, p["sgu_w"][l], bias_full)
        yb, yc = _poolconv_fwd(z, pw2, row(p["pool_scale"][l]), p["conv_w"][l])
        ypre = _s5_core_fwd(z, sp)
        yd = _s5_glu_fwd(ypre, *glu)
        ys = (ya, yb, yc, yd)
        x2, m = _mix_out_fwd(ys, row(p["mix_norm_g"][l]), wmo, x1, mod2[2:3])
        x3, h3, a3, b3, o3 = _ffn_fwd(x2, mod3, row(p["norm3_g"][l]), wi2, wo2)
        saved.append((x, x1, x2, h1, a1, b1, o1, z, h2, ys, m, h3, a3, b3, o3, sp, bias_full, pw2, ypre, glu,
                      (wi1, wo1, wmit, wmo, wi2, wo2), cl))
        x = x3

    loss, dx, dfg = _loss_head(x, row(p["final_norm_g"]), target)

    sg = {n: [None] * nl for n in SMALL_NAMES if n not in ("ada_b", "final_norm_g")}
    dcond = [None] * nl
    s5_da, s5_dk = [None] * nl, [None] * nl
    tok = 0.0
    for l in reversed(range(nl)):
        (x0, x1, x2, h1, a1, b1, o1, z, h2, ys, m, h3, a3, b3, o3, sp, bias_full, pw2, ypre, glu,
         (wi1, wo1, wmit, wmo, wi2, wo2), cl) = saved[l]
        cl = cl + tok
        mod1, mod2, mod3 = cl[0:3], cl[3:6], cl[6:9]
        do, dgate3 = _gate_bwd(dx, o3, mod3[2:3], 0.5)
        dza, dzb, dwi2, dwo2 = _ffn_bwd_main(do, h3, a3, b3, wo2)
        dx, rows3 = _ffn_bwd_in(dza, dzb, wi2, x2, dx, mod3, row(p["norm3_g"][l]))
        outs = _mix_out_bwd(dx, m, mod2[2:3], ys, row(p["mix_norm_g"][l]), wmo)
        dys, dgate2, dmng, dwmo = outs[0:4], outs[4], outs[5], outs[6]
        dza_, dsw, dsb = _sgu_bwd(z, dys[0], p["sgu_w"][l], bias_full)
        dzb_, dzc_, dwbd, dps, dcw = _poolconv_bwd(z, dys[1], dys[2], pw2, row(p["pool_scale"][l]), p["conv_w"][l])
        dypre, dgw, dgb = _s5_glu_bwd(ypre, dys[3], *glu)
        dzd_, dbr, dbi, dcr, dci, dd, da, dk = _s5_core_bwd(z, dypre, sp)
        dx, rows2, dwmit = _mix_in_bwd((dza_, dzb_, dzc_, dzd_), h2, wmit, x1, dx, mod2, row(p["norm2_g"][l]))
        do, dgate1 = _gate_bwd(dx, o1, mod1[2:3], 0.5)
        dza, dzb, dwi1, dwo1 = _ffn_bwd_main(do, h1, a1, b1, wo1)
        dx, rows1 = _ffn_bwd_in(dza, dzb, wi1, x0, dx, mod1, row(p["norm1_g"][l]))

        tok = emit_grads(l, [dwi1, dwo1, dwmit, dwmo, dwi2, dwo2], dx)
        dcond[l] = jnp.concatenate([rows1[0:2], dgate1, rows2[0:2], dgate2, rows3[0:2], dgate3], axis=0)
        sg["norm1_g"][l], sg["norm2_g"][l], sg["norm3_g"][l] = rows1[2], rows2[2], rows3[2]
        sg["mix_norm_g"][l] = dmng[0]
        sg["sgu_w"][l] = dsw
        sg["sgu_b"][l] = dsb[:, 0:4].T
        g4 = dwbd.reshape(4, 64, 4, 64)
        sg["pool_w"][l] = jnp.stack([g4[k, :, k, :] for k in range(4)])
        sg["pool_scale"][l] = dps[0]
        sg["conv_w"][l] = dcw[0:3]
        sg["s5_b_re"][l], sg["s5_b_im"][l] = dbr.reshape(16, 64, 16), dbi.reshape(16, 64, 16)
        sg["s5_c_re"][l], sg["s5_c_im"][l] = dcr.reshape(16, 16, 64), dci.reshape(16, 16, 64)
        sg["s5_d"][l] = dd[0]
        sg["s5_glu_w"][l], sg["s5_glu_b"][l] = dgw, dgb[0]
        s5_da[l], s5_dk[l] = da, dk

    n16 = nl * 16
    dlre, dlim, dldt = _s5_param_bwd(
        p["s5_lambda_re"].reshape(n16, 64), p["s5_lambda_im"].reshape(n16, 64),
        jnp.repeat(p["s5_log_dt"].reshape(n16, 1), 64, axis=1),
        jnp.stack([a[0] for a in s5_da]).reshape(n16, 64), jnp.stack([a[1] for a in s5_da]).reshape(n16, 64),
        jnp.stack([k[:, 0] for k in s5_dk]).reshape(n16, 64), jnp.stack([k[:, 1] for k in s5_dk]).reshape(n16, 64))
    small = {n: jnp.stack(v) for n, v in sg.items() if v[0] is not None}
    small["s5_lambda_re"] = dlre.reshape(nl, 16, 64)
    small["s5_lambda_im"] = dlim.reshape(nl, 16, 64)
    small["s5_log_dt"] = dldt.reshape(nl, 16)
    small["final_norm_g"] = dfg[0]
    return loss, dx, small, jnp.stack(dcond)


def kernel(x, c, ada_w, ada_b, norm1_g, ffn1_w_in, ffn1_w_out, norm2_g, w_mix_in, sgu_w, sgu_b, pool_w, pool_scale, conv_w, s5_lambda_re, s5_lambda_im, s5_b_re, s5_b_im, s5_c_re, s5_c_im, s5_d, s5_log_dt, s5_glu_w, s5_glu_b, mix_norm_g, w_mix_out, norm3_g, ffn2_w_in, ffn2_w_out, final_norm_g, loss_target, m_ada_w, m_ada_b, m_norm1_g, m_ffn1_w_in, m_ffn1_w_out, m_norm2_g, m_w_mix_in, m_sgu_w, m_sgu_b, m_pool_w, m_pool_scale, m_conv_w, m_s5_lambda_re, m_s5_lambda_im, m_s5_b_re, m_s5_b_im, m_s5_c_re, m_s5_c_im, m_s5_d, m_s5_log_dt, m_s5_glu_w, m_s5_glu_b, m_mix_norm_g, m_w_mix_out, m_norm3_g, m_ffn2_w_in, m_ffn2_w_out, m_final_norm_g, v_ada_w, v_ada_b, v_norm1_g, v_ffn1_w_in, v_ffn1_w_out, v_norm2_g, v_w_mix_in, v_sgu_w, v_sgu_b, v_pool_w, v_pool_scale, v_conv_w, v_s5_lambda_re, v_s5_lambda_im, v_s5_b_re, v_s5_b_im, v_s5_c_re, v_s5_c_im, v_s5_d, v_s5_log_dt, v_s5_glu_w, v_s5_glu_b, v_mix_norm_g, v_w_mix_out, v_norm3_g, v_ffn2_w_in, v_ffn2_w_out, v_final_norm_g):
    args = dict(locals())
    w = {n: args[n] for n in WEIGHT_ORDER}
    mom = {n: args["m_" + n] for n in WEIGHT_ORDER}
    vel = {n: args["v_" + n] for n in WEIGHT_ORDER}
    nl, d = DEPTH, x.shape[-1]
    s = x.shape[1]
    px, py, pc = lax.axis_index("x"), lax.axis_index("y"), lax.axis_index("c")
    kme = 2 * px + py
    me = 2 * kme + pc
    kidx = jnp.reshape(kme, (1,)).astype(jnp.int32)

    shards = [ffn1_w_in, ffn1_w_out, jnp.swapaxes(w_mix_in, 1, 2), w_mix_out, ffn2_w_in, ffn2_w_out]
    started_weights = {}

    def start_weights(l):
        placed = _place_weights(shards, l, kidx)
        views = [a.reshape(a.shape[0], 2, a.shape[1] // 2, a.shape[2]) for a in placed]
        *handles, token = _weight_send_start(views, BIG_KINDS, "weight_send_start_%d" % l)
        started_weights[l] = handles
        return token[0, 0]

    tok0 = start_weights(0)

    cact = _silu_rows(c + tok0)
    n_conv, n_glu = conv_w.size, s5_glu_w.size
    pre = _pack([cact, conv_w, s5_glu_w])
    pre_all = _all_gather8(pre, "gather_prelude").reshape(N_DEV, -1)
    cact_all = pre_all[:, :d]
    conv_full = jnp.concatenate(
        [pre_all[2 * k, d:d + n_conv].reshape(conv_w.shape) for k in range(N_CHIP)], axis=2)
    glu_full = jnp.concatenate(
        [pre_all[2 * k, d + n_conv:d + n_conv + n_glu].reshape(s5_glu_w.shape) for k in range(N_CHIP)], axis=1)

    n_ada = ada_w.shape[2]
    ada_b_loc = lax.dynamic_slice_in_dim(ada_b, kme * n_ada, n_ada, axis=1).reshape(nl, 1, n_ada)
    cond_part = _cond_fwd(cact_all, ada_w, ada_b_loc)
    cond_all = _all_gather8(cond_part.reshape(nl * N_DEV, n_ada), "gather_cond").reshape(N_DEV, nl, N_DEV, n_ada)
    cond_me = jnp.concatenate(
        [lax.dynamic_index_in_dim(cond_all[2 * k], me, axis=1, keepdims=False) for k in range(N_CHIP)], axis=1)
    cond = cond_me.reshape(nl, 9, d)

    def fetch_weights(l, after):
        send_sems, recv_sems, *views = started_weights.pop(l)
        views = _weight_send_wait(send_sems, recv_sems, views, BIG_KINDS, after, "weight_send_wait_%d" % l)
        views = _weight_forward(views, BIG_KINDS)
        tok = start_weights(l + 1) if l + 1 < nl else 0.0
        full = [v.reshape(2, 2 * v.shape[2], v.shape[3]) if kind == "col" else v.reshape(-1, v.shape[3])
                for v, kind in zip(views, BIG_KINDS)]
        return full, tok

    ck = jnp.stack([pc, kme]).astype(jnp.int32)
    reduced = [None] * nl
    pending = []

    def emit_grads(l, grads_l, after):
        started = _reduce_big_grads(grads_l, BIG_KINDS, ck, l)
        if pending:
            prev, prev_started = pending.pop()
            reduced[prev] = _finish_big_grads(prev_started, BIG_KINDS, ck, after, prev)
        pending.append((l, started))
        return started[-1][0, 0]

    p = {n: w[n] for n in SMALL_NAMES}
    p["conv_w"], p["s5_glu_w"] = conv_full, glu_full
    loss, dx, small, dcond = _local_step(x[0], loss_target[0], cond, fetch_weights, p, emit_grads)

    small_order = [n for n in SMALL_NAMES if n != "ada_b"]
    packed = _pack([dcond] + [small[n] for n in small_order])
    gathered_small = _all_gather8(packed, "gather_small_grads")
    total = _sum8(gathered_small)
    shapes = [dcond.shape] + [small[n].shape for n in small_order]
    tot = dict(zip(["ada_b"] + small_order, _unpack(total, shapes)))
    grads = {n: tot[n] for n in SMALL_NAMES}
    grads["ada_b"] = tot["ada_b"].reshape(nl, 9 * d)
    grads["conv_w"] = lax.dynamic_slice_in_dim(tot["conv_w"], kme * conv_w.shape[2], conv_w.shape[2], axis=2)
    grads["s5_glu_w"] = lax.dynamic_slice_in_dim(tot["s5_glu_w"], kme * s5_glu_w.shape[1], s5_glu_w.shape[1], axis=1)

    dcond_all = gathered_small.reshape(N_DEV, -1)[:, :dcond.size].reshape(N_DEV, nl, 9 * d)
    dcond_loc = jnp.swapaxes(lax.dynamic_slice_in_dim(dcond_all, kme * n_ada, n_ada, axis=2), 0, 1)
    g_ada, d_ada, m_ada, v_ada = _ada_w_update(cact_all, dcond_loc, ada_w, m_ada_w, v_ada_w)

    last, last_started = pending.pop()
    reduced[last] = _finish_big_grads(last_started, BIG_KINDS, ck, g_ada, last)
    for t, n in enumerate(BIG_NAMES):
        g = jnp.stack([reduced[l][t] for l in range(nl)])
        grads[n] = jnp.swapaxes(g, 1, 2) if n == "w_mix_in" else g

    delta, new_m, new_v = {}, {}, {}
    grads["ada_w"], delta["ada_w"], new_m["ada_w"], new_v["ada_w"] = g_ada, d_ada, m_ada, v_ada
    for n in BIG_NAMES:
        delta[n], new_m[n], new_v[n] = _adamw(w[n], grads[n], mom[n], vel[n], "adamw_" + n)
    sw, sg_, sm, sv = (_pack([t[n] for n in SMALL_NAMES])[None] for t in (w, grads, mom, vel))
    outs = _adamw(sw, sg_, sm, sv, "adamw_small")
    sshapes = [w[n].shape for n in SMALL_NAMES]
    for res, o in zip((delta, new_m, new_v), outs):
        res.update(dict(zip(SMALL_NAMES, _unpack(o[0], sshapes))))

    loss_total = lax.psum(loss[0, 0], ("x", "y", "c"))
    return (loss_total, dx[None], *[grads[n] for n in WEIGHT_ORDER], *[delta[n] for n in WEIGHT_ORDER],
            *[new_m[n] for n in WEIGHT_ORDER], *[new_v[n] for n in WEIGHT_ORDER])
```

```python
import functools
import math

import jax
import jax.numpy as jnp
from jax import lax
from jax.experimental import pallas as pl
from jax.experimental.pallas import tpu as pltpu

F32, BF16 = jnp.float32, jnp.bfloat16
EPS = 1e-6
DEPTH = 4
N_DEV = 8
N_CHIP = 4
W_GRP = 256
CHUNK = 128
N_SEG = 8
LANES = 128
FFN_TF = 256
FFN_TF_WIDE = 1408
FFN_TM_WIDE = 512
VMEM_LIMIT = 56 * 1024 * 1024
ADAM_LR, ADAM_B1, ADAM_B2, ADAM_EPS, ADAM_WD, ADAM_STEP = 0.001, 0.9, 0.999, 1e-08, 0.01, 10
MESH_ID = pl.DeviceIdType.MESH
HI = lax.Precision.HIGHEST
ANY = pl.BlockSpec(memory_space=pl.ANY)


def _cp(**kw):
    return pltpu.CompilerParams(vmem_limit_bytes=VMEM_LIMIT, **kw)


def _dot(a, b):
    return jnp.dot(a.astype(BF16), b.astype(BF16), preferred_element_type=F32)


def _dot_nt(a, b):
    return lax.dot_general(a.astype(BF16), b.astype(BF16), (((1,), (1,)), ((), ())), preferred_element_type=F32)


def _dot_tn(a, b):
    return lax.dot_general(a.astype(BF16), b.astype(BF16), (((0,), (0,)), ((), ())), preferred_element_type=F32)


def _dot_hi(a, b):
    return jnp.dot(a, b, preferred_element_type=F32, precision=HI)


def _gelu(x):
    k = 0.7978845608028654
    t = jnp.tanh(k * (x + 0.044715 * x * x * x))
    return 0.5 * x * (1.0 + t), t


def _gelu_grad(x, t):
    k = 0.7978845608028654
    return 0.5 * (1.0 + t) + 0.5 * x * (1.0 - t * t) * k * (1.0 + 3.0 * 0.044715 * x * x)


def _iota(shape, axis):
    return lax.broadcasted_iota(jnp.int32, shape, axis)


def _full(shape):
    nd = len(shape)
    return pl.BlockSpec(shape, lambda *_: (0,) * nd)


def _norm_mod(xv, g, shift, scale):
    r = lax.rsqrt(jnp.mean(xv * xv, axis=-1, keepdims=True) + EPS)
    return (xv * r * g) * (1.0 + scale) + shift


def _norm_mod_bwd(xv, g, scale, dh):
    r = lax.rsqrt(jnp.mean(xv * xv, axis=-1, keepdims=True) + EPS)
    xh = xv * r
    n = xh * g
    dsh = jnp.sum(dh, axis=0, keepdims=True)
    dsc = jnp.sum(dh * n, axis=0, keepdims=True)
    dn = dh * (1.0 + scale)
    dg = jnp.sum(dn * xh, axis=0, keepdims=True)
    dxh = dn * g
    dx = r * (dxh - xh * jnp.mean(dxh * xh, axis=-1, keepdims=True))
    return dx, dsh, dsc, dg


def _tm(s):
    return min(s, 1024)


def _ffn_fwd(x, mod, g, wi, wo):
    s, d = x.shape
    f = wo.shape[0]
    tf, tm = FFN_TF_WIDE, min(s, FFN_TM_WIDE)
    nf, nt = f // tf, s // tm

    def body(x_ref, mod_ref, g_ref, wa_ref, wb_ref, wo_ref, xn_ref, h_ref, a_ref, b_ref, o_ref):
        j = pl.program_id(1)

        @pl.when(j == 0)
        def _():
            hh = _norm_mod(x_ref[...], g_ref[...], mod_ref[0:1, :], mod_ref[1:2, :])
            h_ref[...] = hh.astype(BF16)
            o_ref[...] = jnp.zeros_like(o_ref)

        h = h_ref[...]
        a = jnp.dot(h, wa_ref[...], preferred_element_type=F32)
        b = jnp.dot(h, wb_ref[...], preferred_element_type=F32)
        a_ref[...] = a.astype(BF16)
        b_ref[...] = b.astype(BF16)
        u = (a * jax.nn.sigmoid(a)) * b
        o_ref[...] += jnp.dot(u.astype(BF16), wo_ref[...], preferred_element_type=F32)

        @pl.when(j == nf - 1)
        def _():
            xn_ref[...] = x_ref[...] + 0.5 * mod_ref[2:3, :] * o_ref[...]

    row = pl.BlockSpec((tm, d), lambda i, j: (i, 0))
    chunk = pl.BlockSpec((tm, tf), lambda i, j: (i, j))
    return pl.pallas_call(
        body, name="ffn_fwd", grid=(nt, nf),
        in_specs=[row, _full((3, d)), _full((1, d)),
                  pl.BlockSpec((None, d, tf), lambda i, j: (0, 0, j)),
                  pl.BlockSpec((None, d, tf), lambda i, j: (1, 0, j)),
                  pl.BlockSpec((tf, d), lambda i, j: (j, 0))],
        out_specs=[row, row, chunk, chunk, row],
        out_shape=[jax.ShapeDtypeStruct((s, d), F32), jax.ShapeDtypeStruct((s, d), BF16),
                   jax.ShapeDtypeStruct((s, f), BF16), jax.ShapeDtypeStruct((s, f), BF16),
                   jax.ShapeDtypeStruct((s, d), F32)],
        compiler_params=_cp(dimension_semantics=("parallel", "arbitrary")),
    )(x, mod, g, wi, wi, wo)


def _gate_bwd(dxo, o, gate, half):
    s, d = dxo.shape
    tm = _tm(s)
    nt = s // tm

    def body(dx_ref, o_ref, gate_ref, do_ref, dg_ref):
        i = pl.program_id(0)

        @pl.when(i == 0)
        def _():
            dg_ref[...] = jnp.zeros_like(dg_ref)

        dx = dx_ref[...]
        do_ref[...] = (half * gate_ref[...] * dx).astype(BF16)
        dg_ref[...] += half * jnp.sum(o_ref[...] * dx, axis=0, keepdims=True)

    row = pl.BlockSpec((tm, d), lambda i: (i, 0))
    return pl.pallas_call(
        body, name="gate_bwd", grid=(nt,),
        in_specs=[row, row, _full((1, d))], out_specs=[row, _full((1, d))],
        out_shape=[jax.ShapeDtypeStruct((s, d), BF16), jax.ShapeDtypeStruct((1, d), F32)],
        compiler_params=_cp(dimension_semantics=("arbitrary",)),
    )(dxo, o, gate)


def _ffn_bwd_main(do, h, a, b, wo):
    s, d = do.shape
    f = wo.shape[0]
    tf = FFN_TF
    nf = f // tf

    def body(do_ref, h_ref, a_ref, b_ref, wo_ref, dza_ref, dzb_ref, dwi_ref, dwo_ref):
        dov = do_ref[...]
        hv = h_ref[...]
        du = lax.dot_general(dov, wo_ref[...], (((1,), (1,)), ((), ())), preferred_element_type=F32)
        av = a_ref[...].astype(F32)
        bv = b_ref[...].astype(F32)
        sa = jax.nn.sigmoid(av)
        si = av * sa
        u = (si * bv).astype(BF16)
        da = (du * bv * (sa * (1.0 + av * (1.0 - sa)))).astype(BF16)
        db = (du * si).astype(BF16)
        dza_ref[...] = da
        dzb_ref[...] = db
        dwo_ref[...] = _dot_tn(u, dov).astype(BF16)
        dwi_ref[0] = _dot_tn(hv, da).astype(BF16)
        dwi_ref[1] = _dot_tn(hv, db).astype(BF16)

    chunk = pl.BlockSpec((s, tf), lambda j: (0, j))
    return pl.pallas_call(
        body, name="ffn_bwd_main", grid=(nf,),
        in_specs=[_full((s, d)), _full((s, d)), chunk, chunk, pl.BlockSpec((tf, d), lambda j: (j, 0))],
        out_specs=[chunk, chunk, pl.BlockSpec((2, d, tf), lambda j: (0, 0, j)),
                   pl.BlockSpec((tf, d), lambda j: (j, 0))],
        out_shape=[jax.ShapeDtypeStruct((s, f), BF16), jax.ShapeDtypeStruct((s, f), BF16),
                   jax.ShapeDtypeStruct((2, d, f), BF16), jax.ShapeDtypeStruct((f, d), BF16)],
        compiler_params=_cp(dimension_semantics=("parallel",)),
    )(do, h, a, b, wo)


def _ffn_bwd_in(dza, dzb, wi, x, dxo, mod, g):
    s, d = x.shape
    f = dza.shape[1]
    tf, tm = FFN_TF_WIDE, min(s, FFN_TM_WIDE)
    nf, nt = f // tf, s // tm

    def body(dza_ref, dzb_ref, wa_ref, wb_ref, x_ref, dxo_ref, mod_ref, g_ref, dx_ref, rows_ref, acc):
        i, j = pl.program_id(0), pl.program_id(1)

        @pl.when(jnp.logical_and(i == 0, j == 0))
        def _():
            rows_ref[...] = jnp.zeros_like(rows_ref)

        @pl.when(j == 0)
        def _():
            acc[...] = jnp.zeros_like(acc)

        acc[...] += (lax.dot_general(dza_ref[...], wa_ref[...], (((1,), (1,)), ((), ())), preferred_element_type=F32)
                     + lax.dot_general(dzb_ref[...], wb_ref[...], (((1,), (1,)), ((), ())), preferred_element_type=F32))

        @pl.when(j == nf - 1)
        def _():
            dx, dsh, dsc, dg = _norm_mod_bwd(x_ref[...], g_ref[...], mod_ref[1:2, :], acc[...])
            dx_ref[...] = dx + dxo_ref[...]
            rows_ref[0:1, :] += dsh
            rows_ref[1:2, :] += dsc
            rows_ref[2:3, :] += dg

    row = pl.BlockSpec((tm, d), lambda i, j: (i, 0))
    chunk = pl.BlockSpec((tm, tf), lambda i, j: (i, j))
    return pl.pallas_call(
        body, name="ffn_bwd_in", grid=(nt, nf),
        in_specs=[chunk, chunk,
                  pl.BlockSpec((None, d, tf), lambda i, j: (0, 0, j)),
                  pl.BlockSpec((None, d, tf), lambda i, j: (1, 0, j)),
                  row, row, _full((3, d)), _full((1, d))],
        out_specs=[row, _full((8, d))],
        out_shape=[jax.ShapeDtypeStruct((s, d), F32), jax.ShapeDtypeStruct((8, d), F32)],
        scratch_shapes=[pltpu.VMEM((tm, d), F32)],
        compiler_params=_cp(dimension_semantics=("arbitrary", "arbitrary")),
    )(dza, dzb, wi, wi, x, dxo, mod, g)


def _mix_in_fwd(x, mod, g, wmit):
    s, d = x.shape
    p = wmit.shape[0]
    tm = _tm(s)

    def body(x_ref, mod_ref, g_ref, w_ref, z_ref, h_ref):
        hh = _norm_mod(x_ref[...], g_ref[...], mod_ref[0:1, :], mod_ref[1:2, :]).astype(BF16)
        h_ref[...] = hh
        z_ref[...] = lax.dot_general(hh, w_ref[...], (((1,), (1,)), ((), ())), preferred_element_type=F32)

    row = pl.BlockSpec((tm, d), lambda i: (i, 0))
    return pl.pallas_call(
        body, name="mix_in_fwd", grid=(s // tm,),
        in_specs=[row, _full((3, d)), _full((1, d)), _full((p, d))],
        out_specs=[pl.BlockSpec((tm, p), lambda i: (i, 0)), row],
        out_shape=[jax.ShapeDtypeStruct((s, p), F32), jax.ShapeDtypeStruct((s, d), BF16)],
        compiler_params=_cp(dimension_semantics=("parallel",)),
    )(x, mod, g, wmit)


def _mix_in_bwd(dzs, h, wmit, x, dxo, mod, g):
    s, d = x.shape
    p = wmit.shape[0]
    tm = min(s, 512)
    nt = s // tm

    def body(za_ref, zb_ref, zc_ref, zd_ref, h_ref, w_ref, x_ref, dxo_ref, mod_ref, g_ref,
             dx_ref, rows_ref, dw_ref, acc):
        i = pl.program_id(0)

        @pl.when(i == 0)
        def _():
            rows_ref[...] = jnp.zeros_like(rows_ref)
            acc[...] = jnp.zeros_like(acc)

        dz = jnp.concatenate([za_ref[...], zb_ref[...], zc_ref[...], zd_ref[...]], axis=1).astype(BF16)
        acc[...] += _dot_tn(dz, h_ref[...])
        dh = jnp.dot(dz, w_ref[...], preferred_element_type=F32)
        dx, dsh, dsc, dg = _norm_mod_bwd(x_ref[...], g_ref[...], mod_ref[1:2, :], dh)
        dx_ref[...] = dx + dxo_ref[...]
        rows_ref[0:1, :] += dsh
        rows_ref[1:2, :] += dsc
        rows_ref[2:3, :] += dg

        @pl.when(i == nt - 1)
        def _():
            dw_ref[...] = acc[...].astype(BF16)

    row = pl.BlockSpec((tm, d), lambda i: (i, 0))
    zspecs = [pl.BlockSpec((tm, z.shape[1]), lambda i: (i, 0)) for z in dzs]
    return pl.pallas_call(
        body, name="mix_in_bwd", grid=(nt,),
        in_specs=zspecs + [row, _full((p, d)), row, row, _full((3, d)), _full((1, d))],
        out_specs=[row, _full((8, d)), _full((p, d))],
        out_shape=[jax.ShapeDtypeStruct((s, d), F32), jax.ShapeDtypeStruct((8, d), F32),
                   jax.ShapeDtypeStruct((p, d), BF16)],
        scratch_shapes=[pltpu.VMEM((p, d), F32)],
        compiler_params=_cp(dimension_semantics=("arbitrary",)),
    )(*dzs, h, wmit, x, dxo, mod, g)


def _group_norm(ys, mng):
    outs, hats, rs = [], [], []
    for k, y in enumerate(ys):
        r = lax.rsqrt(jnp.mean(y * y, axis=-1, keepdims=True) + EPS)
        yh = y * r
        hats.append(yh)
        rs.append(r)
        outs.append(yh * mng[:, k * W_GRP:(k + 1) * W_GRP])
    return jnp.concatenate(outs, axis=1), hats, rs


def _mix_out_fwd(ys, mng, wmo, x, gate):
    s, d = x.shape
    tm = _tm(s)

    def body(ya, yb, yc, yd, mng_ref, w_ref, x_ref, gate_ref, xn_ref, m_ref):
        yn, _, _ = _group_norm([ya[...], yb[...], yc[...], yd[...]], mng_ref[...])
        m = jnp.dot(yn.astype(BF16), w_ref[...], preferred_element_type=F32)
        m_ref[...] = m
        xn_ref[...] = x_ref[...] + gate_ref[...] * m

    row = pl.BlockSpec((tm, d), lambda i: (i, 0))
    grp = pl.BlockSpec((tm, W_GRP), lambda i: (i, 0))
    return pl.pallas_call(
        body, name="mix_out_fwd", grid=(s // tm,),
        in_specs=[grp, grp, grp, grp, _full((1, d)), _full((d, d)), row, _full((1, d))],
        out_specs=[row, row],
        out_shape=[jax.ShapeDtypeStruct((s, d), F32), jax.ShapeDtypeStruct((s, d), F32)],
        compiler_params=_cp(dimension_semantics=("parallel",)),
    )(*ys, mng, wmo, x, gate)


def _mix_out_bwd(dxo, m, gate, ys, mng, wmo):
    s, d = dxo.shape
    tm = min(s, 512)
    nt = s // tm

    def body(dxo_ref, m_ref, gate_ref, ya, yb, yc, yd, mng_ref, w_ref,
             dya, dyb, dyc, dyd, dgate_ref, dmng_ref, dw_ref, acc):
        i = pl.program_id(0)

        @pl.when(i == 0)
        def _():
            dgate_ref[...] = jnp.zeros_like(dgate_ref)
            dmng_ref[...] = jnp.zeros_like(dmng_ref)
            acc[...] = jnp.zeros_like(acc)

        dxv = dxo_ref[...]
        dgate_ref[...] += jnp.sum(m_ref[...] * dxv, axis=0, keepdims=True)
        dm = (gate_ref[...] * dxv).astype(BF16)
        mng = mng_ref[...]
        yn, hats, rs = _group_norm([ya[...], yb[...], yc[...], yd[...]], mng)
        acc[...] += _dot_tn(yn, dm)
        dyn = lax.dot_general(dm, w_ref[...], (((1,), (1,)), ((), ())), preferred_element_type=F32)
        dmng_parts = []
        for k, (yh, r, out) in enumerate(zip(hats, rs, (dya, dyb, dyc, dyd))):
            dk = dyn[:, k * W_GRP:(k + 1) * W_GRP]
            dmng_parts.append(jnp.sum(dk * yh, axis=0, keepdims=True))
            dyh = dk * mng[:, k * W_GRP:(k + 1) * W_GRP]
            out[...] = r * (dyh - yh * jnp.mean(dyh * yh, axis=-1, keepdims=True))
        dmng_ref[...] += jnp.concatenate(dmng_parts, axis=1)

        @pl.when(i == nt - 1)
        def _():
            dw_ref[...] = acc[...].astype(BF16)

    row = pl.BlockSpec((tm, d), lambda i: (i, 0))
    grp = pl.BlockSpec((tm, W_GRP), lambda i: (i, 0))
    return pl.pallas_call(
        body, name="mix_out_bwd", grid=(nt,),
        in_specs=[row, row, _full((1, d)), grp, grp, grp, grp, _full((1, d)), _full((d, d))],
        out_specs=[grp, grp, grp, grp, _full((1, d)), _full((1, d)), _full((d, d))],
        out_shape=[jax.ShapeDtypeStruct((s, W_GRP), F32)] * 4
        + [jax.ShapeDtypeStruct((1, d), F32), jax.ShapeDtypeStruct((1, d), F32), jax.ShapeDtypeStruct((d, d), BF16)],
        scratch_shapes=[pltpu.VMEM((d, d), F32)],
        compiler_params=_cp(dimension_semantics=("arbitrary",)),
    )(dxo, m, gate, *ys, mng, wmo)


def _sgu_consts():
    r = _iota((W_GRP, W_GRP), 0) >> 6
    c = _iota((W_GRP, W_GRP), 1) >> 6
    avg = jnp.where(r == c, 1.0 / 64.0, 0.0).astype(F32)
    tril = _iota((CHUNK, CHUNK), 0) >= _iota((CHUNK, CHUNK), 1)
    head = _iota((CHUNK, W_GRP), 1) >> 6
    return avg, tril, head


def _sgu_pre(za, avg):
    zg, t = _gelu(za)
    u, v = zg[:, :W_GRP], zg[:, W_GRP:]
    mu = _dot_hi(v, avg)
    vc = v - mu
    r = lax.rsqrt(_dot_hi(vc * vc, avg) + EPS)
    return t, u, vc * r, r


def _sgu_fwd(z, sgu_w, bias_full):
    s = z.shape[0]
    tm = min(s, 512)

    def body(za_ref, w_ref, bias_ref, ya_ref):
        avg, tril, head = _sgu_consts()
        _, u, vn, _ = _sgu_pre(za_ref[...], avg)
        wm = [jnp.where(tril, w_ref[h], 0.0).astype(BF16) for h in range(4)]
        vb = vn.astype(BF16)
        for n in range(tm // CHUNK):
            rows = slice(n * CHUNK, (n + 1) * CHUNK)
            mixed = bias_ref[...]
            for h in range(4):
                mixed = mixed + jnp.where(head == h, jnp.dot(wm[h], vb[rows], preferred_element_type=F32), 0.0)
            ya_ref[rows, :] = u[rows] * mixed

    return pl.pallas_call(
        body, name="sgu_fwd", grid=(s // tm,),
        in_specs=[pl.BlockSpec((tm, 2 * W_GRP), lambda i: (i, 0)), _full((4, CHUNK, CHUNK)), _full((CHUNK, W_GRP))],
        out_specs=pl.BlockSpec((tm, W_GRP), lambda i: (i, 0)),
        out_shape=jax.ShapeDtypeStruct((s, W_GRP), F32),
        compiler_params=_cp(dimension_semantics=("parallel",)),
    )(z, sgu_w, bias_full)


def _sgu_bwd(z, dya, sgu_w, bias_full):
    s = z.shape[0]
    tm = min(s, 512)
    nt = s // tm

    def body(za_ref, dya_ref, w_ref, bias_ref, dza_ref, dw_ref, db_ref, du_s, dvn_s):
        i = pl.program_id(0)

        @pl.when(i == 0)
        def _():
            dw_ref[...] = jnp.zeros_like(dw_ref)
            db_ref[...] = jnp.zeros_like(db_ref)

        avg, tril, head = _sgu_consts()
        za = za_ref[...]
        t, u, vn, r = _sgu_pre(za, avg)
        wm = [jnp.where(tril, w_ref[h], 0.0).astype(BF16) for h in range(4)]
        vb = vn.astype(BF16)
        dya = dya_ref[...]
        dw = [jnp.zeros((CHUNK, CHUNK), F32) for _ in range(4)]
        db = jnp.zeros((CHUNK, W_GRP), F32)
        for n in range(tm // CHUNK):
            rows = slice(n * CHUNK, (n + 1) * CHUNK)
            mixed = bias_ref[...]
            for h in range(4):
                mixed = mixed + jnp.where(head == h, jnp.dot(wm[h], vb[rows], preferred_element_type=F32), 0.0)
            dmix = dya[rows] * u[rows]
            du_s[rows, :] = dya[rows] * mixed
            db = db + dmix
            dmb = dmix.astype(BF16)
            dvn = jnp.zeros((CHUNK, W_GRP), F32)
            for h in range(4):
                dmh = jnp.where(head == h, dmix, 0.0)
                dw[h] = dw[h] + _dot_nt(dmh, vb[rows])
                dvn = dvn + jnp.where(head == h, _dot_tn(wm[h], dmb), 0.0)
            dvn_s[rows, :] = dvn
        for h in range(4):
            dw_ref[h] += jnp.where(tril, dw[h], 0.0)
        sel = ((_iota((W_GRP, CHUNK), 0) >> 6) == _iota((W_GRP, CHUNK), 1)).astype(F32)
        db_ref[...] += _dot_hi(db, sel)
        dvn = dvn_s[...]
        dv = r * (dvn - _dot_hi(dvn, avg) - vn * _dot_hi(dvn * vn, avg))
        dzg = jnp.concatenate([du_s[...], dv], axis=1)
        dza_ref[...] = dzg * _gelu_grad(za, t)

    return pl.pallas_call(
        body, name="sgu_bwd", grid=(nt,),
        in_specs=[pl.BlockSpec((tm, 2 * W_GRP), lambda i: (i, 0)), pl.BlockSpec((tm, W_GRP), lambda i: (i, 0)),
                  _full((4, CHUNK, CHUNK)), _full((CHUNK, W_GRP))],
        out_specs=[pl.BlockSpec((tm, 2 * W_GRP), lambda i: (i, 0)), _full((4, CHUNK, CHUNK)), _full((CHUNK, CHUNK))],
        out_shape=[jax.ShapeDtypeStruct((s, 2 * W_GRP), F32), jax.ShapeDtypeStruct((4, CHUNK, CHUNK), F32),
                   jax.ShapeDtypeStruct((CHUNK, CHUNK), F32)],
        scratch_shapes=[pltpu.VMEM((tm, W_GRP), F32), pltpu.VMEM((tm, W_GRP), F32)],
        compiler_params=_cp(dimension_semantics=("arbitrary",)),
    )(z, dya, sgu_w, bias_full)


def _shift_down(x, k):
    return jnp.where(_iota(x.shape, 0) < k, 0.0, pltpu.roll(x, k, 0))


def _shift_up(x, k):
    n = x.shape[0]
    return jnp.where(_iota(x.shape, 0) >= n - k, 0.0, pltpu.roll(x, n - k, 0))


def _by_pool_group(shape, v2, v4, v8, v16):
    col = _iota(shape, 1)
    return jnp.where(col < 64, v2, jnp.where(col < 128, v4, jnp.where(col < 192, v8, v16)))


def _pool_core(zb, pw2):
    s2 = zb + _shift_down(zb, 1)
    s4 = s2 + _shift_down(s2, 2)
    s8 = s4 + _shift_down(s4, 4)
    s16 = s8 + _shift_down(s8, 8)
    win = _by_pool_group(zb.shape, s2, s4, s8, s16)
    wlen = _by_pool_group(zb.shape, 2.0, 4.0, 8.0, 16.0)
    cnt = jnp.minimum((_iota(zb.shape, 0) + 1).astype(F32), wlen)
    p = win / cnt - zb
    wt = jnp.tile(pw2, (1, 4))
    wbd = jnp.where((_iota(wt.shape, 0) >> 6) == (_iota(wt.shape, 1) >> 6), wt, 0.0).astype(BF16)
    return p, cnt, wbd


def _conv_core(zc, cw):
    bg, cg, xh = zc[:, :W_GRP], zc[:, W_GRP:2 * W_GRP], zc[:, 2 * W_GRP:]
    y = cg * xh
    y1, y2 = _shift_down(y, 1), _shift_down(y, 2)
    out = cw[2:3, :] * y + cw[1:2, :] * y1 + cw[0:1, :] * y2
    return bg, cg, xh, y, y1, y2, out


def _poolconv_fwd(z, pw2, pscale, cw):
    s = z.shape[0]

    def body(zb_ref, zc_ref, pw_ref, ps_ref, cw_ref, yb_ref, yc_ref):
        p, _, wbd = _pool_core(zb_ref[...], pw_ref[...])
        yb_ref[...] = jnp.dot(p.astype(BF16), wbd, preferred_element_type=F32) * ps_ref[...]
        bg, _, _, _, _, _, out = _conv_core(zc_ref[...], cw_ref[...])
        yc_ref[...] = bg * out

    return pl.pallas_call(
        body, name="poolconv_fwd", grid=(1,),
        in_specs=[pl.BlockSpec((s, W_GRP), lambda i: (0, 2)), pl.BlockSpec((s, 3 * W_GRP), lambda i: (0, 1)),
                  _full((W_GRP, 64)), _full((1, W_GRP)), _full((3, W_GRP))],
        out_specs=[_full((s, W_GRP)), _full((s, W_GRP))],
        out_shape=[jax.ShapeDtypeStruct((s, W_GRP), F32)] * 2,
        compiler_params=_cp(dimension_semantics=("arbitrary",)),
    )(z, z, pw2, pscale, cw)


def _poolconv_bwd(z, dyb, dyc, pw2, pscale, cw):
    s = z.shape[0]

    def body(zb_ref, zc_ref, dyb_ref, dyc_ref, pw_ref, ps_ref, cw_ref, dzb_ref, dzc_ref, dw_ref, dps_ref, dcw_ref):
        zb = zb_ref[...]
        p, cnt, wbd = _pool_core(zb, pw_ref[...])
        pb = p.astype(BF16)
        out = jnp.dot(pb, wbd, preferred_element_type=F32)
        dyb = dyb_ref[...]
        dps_ref[...] = jnp.sum(dyb * out, axis=0, keepdims=True)
        dout = (dyb * ps_ref[...]).astype(BF16)
        dw = _dot_tn(pb, dout)
        dw_ref[...] = jnp.where((_iota(dw.shape, 0) >> 6) == (_iota(dw.shape, 1) >> 6), dw, 0.0)
        dp = lax.dot_general(dout, wbd, (((1,), (1,)), ((), ())), preferred_element_type=F32)
        dwin = dp / cnt
        t2 = dwin + _shift_up(dwin, 1)
        t4 = t2 + _shift_up(t2, 2)
        t8 = t4 + _shift_up(t4, 4)
        t16 = t8 + _shift_up(t8, 8)
        dzb_ref[...] = _by_pool_group(zb.shape, t2, t4, t8, t16) - dp

        cw = cw_ref[...]
        bg, cg, xh, y, y1, y2, out = _conv_core(zc_ref[...], cw)
        dyc = dyc_ref[...]
        dout = dyc * bg
        dcw_ref[...] = jnp.zeros_like(dcw_ref)
        dcw_ref[0:1, :] = jnp.sum(dout * y2, axis=0, keepdims=True)
        dcw_ref[1:2, :] = jnp.sum(dout * y1, axis=0, keepdims=True)
        dcw_ref[2:3, :] = jnp.sum(dout * y, axis=0, keepdims=True)
        dy = cw[2:3, :] * dout + cw[1:2, :] * _shift_up(dout, 1) + cw[0:1, :] * _shift_up(dout, 2)
        dzc_ref[...] = jnp.concatenate([dyc * out, dy * xh, dy * cg], axis=1)

    return pl.pallas_call(
        body, name="poolconv_bwd", grid=(1,),
        in_specs=[pl.BlockSpec((s, W_GRP), lambda i: (0, 2)), pl.BlockSpec((s, 3 * W_GRP), lambda i: (0, 1)),
                  _full((s, W_GRP)), _full((s, W_GRP)), _full((W_GRP, 64)), _full((1, W_GRP)), _full((3, W_GRP))],
        out_specs=[_full((s, W_GRP)), _full((s, 3 * W_GRP)), _full((W_GRP, W_GRP)), _full((1, W_GRP)), _full((8, W_GRP))],
        out_shape=[jax.ShapeDtypeStruct((s, W_GRP), F32), jax.ShapeDtypeStruct((s, 3 * W_GRP), F32),
                   jax.ShapeDtypeStruct((W_GRP, W_GRP), F32), jax.ShapeDtypeStruct((1, W_GRP), F32),
                   jax.ShapeDtypeStruct((8, W_GRP), F32)],
        compiler_params=_cp(dimension_semantics=("arbitrary",)),
    )(z, z, dyb, dyc, pw2, pscale, cw)


N_STATE = 1024
HALF_STATE = N_STATE // 2
HALF_CH = W_GRP // 2
N_SLAB = HALF_STATE // LANES


def _s5_disc(lre, lim, ldt):
    dt = jnp.exp(ldt)
    mag = jnp.exp(lre * dt)
    ang = lim * dt
    ar, ai = mag * jnp.cos(ang), mag * jnp.sin(ang)
    nr, ni = ar - 1.0, ai
    den = lre * lre + lim * lim
    kr = (nr * lre + ni * lim) / den
    ki = (ni * lre - nr * lim) / den
    return ar, ai, kr, ki


def _s5_mats(colp, br, bi, cr, ci):
    _, _, kr, ki = _s5_disc(colp[:, 0:1], colp[:, 1:2], colp[:, 2:3])
    bbr = kr * br - ki * bi
    bbi = kr * bi + ki * br
    bmask = (_iota((HALF_STATE, HALF_CH), 0) >> 6) == (_iota((HALF_STATE, HALF_CH), 1) >> 4)
    cmask = (_iota((HALF_CH, HALF_STATE), 0) >> 4) == (_iota((HALF_CH, HALF_STATE), 1) >> 6)
    btr = jnp.where(bmask, jnp.tile(bbr, (1, 8)), 0.0).astype(BF16)
    bti = jnp.where(bmask, jnp.tile(bbi, (1, 8)), 0.0).astype(BF16)
    ctr = jnp.where(cmask, jnp.tile(cr, (1, 8)), 0.0).astype(BF16)
    cti = jnp.where(cmask, jnp.tile(ci, (1, 8)), 0.0).astype(BF16)
    return kr, ki, btr, bti, ctr, cti, bmask, cmask


def _slab(q):
    return slice(q * LANES, (q + 1) * LANES)


def _cmul(ar, ai, br, bi):
    return ar * br - ai * bi, ar * bi + ai * br


def _sub_shift(x, k, up):
    row = _iota(x.shape, 0)
    if up:
        return jnp.where(row >= N_SEG - k, 0.0, pltpu.roll(x, N_SEG - k, 0))
    return jnp.where(row < k, 0.0, pltpu.roll(x, k, 0))


def _seg_rows(j):
    return pl.ds(pl.multiple_of(j * N_SEG, N_SEG), N_SEG)


def _interleave(src, dst, seg):
    def step(j, carry):
        dst[_seg_rows(j), :] = src[pl.ds(j, N_SEG, stride=seg), :]
        return carry
    lax.fori_loop(0, seg, step, 0)


def _deinterleave(src, dst, seg):
    def step(j, carry):
        dst[pl.ds(j, N_SEG, stride=seg), :] = src[_seg_rows(j), :]
        return carry
    lax.fori_loop(0, seg, step, 0)


def _scan(xr, xi, ar_row, ai_row, seg, reverse):
    nlog = int(math.log2(seg))
    assert (1 << nlog) == seg
    for q0 in range(0, N_SLAB, 4):
        qs = list(range(q0, q0 + 4))
        aq = [(jnp.broadcast_to(ar_row[:, _slab(q)], (N_SEG, LANES)),
               jnp.broadcast_to(ai_row[:, _slab(q)], (N_SEG, LANES))) for q in qs]
        zero = jnp.zeros((N_SEG, LANES), F32)

        def local(jj, carry, qs=qs, aq=aq):
            j = seg - 1 - jj if reverse else jj
            out = []
            for n, q in enumerate(qs):
                rows = _seg_rows(j)
                pr, pi = _cmul(aq[n][0], aq[n][1], carry[2 * n], carry[2 * n + 1])
                nr = pr + xr[q, rows, :]
                ni = pi + xi[q, rows, :]
                xr[q, rows, :] = nr
                xi[q, rows, :] = ni
                out += [nr, ni]
            return tuple(out)

        fin = lax.fori_loop(0, seg, local, (zero,) * 8)
        cins = []
        for n in range(4):
            er, ei = fin[2 * n], fin[2 * n + 1]
            pr, pi = aq[n]
            for _ in range(nlog):
                pr, pi = _cmul(pr, pi, pr, pi)
            yr, yi = er, ei
            for k in (1, 2, 4):
                sr, si = _cmul(pr, pi, _sub_shift(yr, k, reverse), _sub_shift(yi, k, reverse))
                yr, yi = yr + sr, yi + si
                pr, pi = _cmul(pr, pi, pr, pi)
            cins.append((_sub_shift(yr, 1, reverse), _sub_shift(yi, 1, reverse)))

        def fix(jj, carry, qs=qs, aq=aq, cins=cins):
            j = seg - 1 - jj if reverse else jj
            out = []
            for n, q in enumerate(qs):
                rows = _seg_rows(j)
                pwr, pwi = carry[2 * n], carry[2 * n + 1]
                cr, ci = _cmul(pwr, pwi, cins[n][0], cins[n][1])
                xr[q, rows, :] += cr
                xi[q, rows, :] += ci
                nr, ni = _cmul(pwr, pwi, aq[n][0], aq[n][1])
                out += [nr, ni]
            return tuple(out)

        lax.fori_loop(0, seg, fix, tuple(v for pair in aq for v in pair))


def _s5_forward_states(u, btr, bti, ar_row, ai_row, xr, xi, seg):
    ub = u.astype(BF16)
    for q in range(N_SLAB):
        xr[q] = _dot_nt(ub, btr[_slab(q), :])
        xi[q] = _dot_nt(ub, bti[_slab(q), :])
    _scan(xr, xi, ar_row, ai_row, seg, False)


def _s5_readout(u, xr, xi, ctr, cti, d):
    y = d * u
    for q in range(N_SLAB):
        y = y + _dot_nt(xr[q], ctr[:, _slab(q)]) - _dot_nt(xi[q], cti[:, _slab(q)])
    return y


def _s5_param_specs():
    return [pl.BlockSpec((3, HALF_STATE), lambda i: (0, i)), pl.BlockSpec((HALF_STATE, 3), lambda i: (i, 0)),
            pl.BlockSpec((HALF_STATE, 16), lambda i: (i, 0)), pl.BlockSpec((HALF_STATE, 16), lambda i: (i, 0)),
            pl.BlockSpec((HALF_CH, 64), lambda i: (i, 0)), pl.BlockSpec((HALF_CH, 64), lambda i: (i, 0)),
            pl.BlockSpec((1, HALF_CH), lambda i: (0, i))]


def _s5_core_fwd(z, sp):
    s = z.shape[0]
    seg = s // N_SEG

    def body(u_ref, rowp, colp, br, bi, cr, ci, d_ref, y_ref, xr, xi, us, ys):
        ar, ai, _, _ = _s5_disc(rowp[0:1, :], rowp[1:2, :], rowp[2:3, :])
        _, _, btr, bti, ctr, cti, _, _ = _s5_mats(colp[...], br[...], bi[...], cr[...], ci[...])
        _interleave(u_ref, us, seg)
        u = us[...]
        _s5_forward_states(u, btr, bti, ar, ai, xr, xi, seg)
        ys[...] = _s5_readout(u, xr, xi, ctr, cti, d_ref[...])
        _deinterleave(ys, y_ref, seg)

    return pl.pallas_call(
        body, name="s5_core_fwd", grid=(2,),
        in_specs=[pl.BlockSpec((s, HALF_CH), lambda i: (0, 12 + i))] + _s5_param_specs(),
        out_specs=pl.BlockSpec((s, HALF_CH), lambda i: (0, i)),
        out_shape=jax.ShapeDtypeStruct((s, W_GRP), F32),
        scratch_shapes=[pltpu.VMEM((N_SLAB, s, LANES), F32)] * 2 + [pltpu.VMEM((s, HALF_CH), F32)] * 2,
        compiler_params=_cp(dimension_semantics=("parallel",)),
    )(z, *sp)


def _s5_glu_fwd(y, gw, gb):
    s = y.shape[0]
    tm = _tm(s)

    def body(y_ref, gw_ref, gb_ref, o_ref):
        yg, _ = _gelu(y_ref[...])
        o_ref[...] = yg * jax.nn.sigmoid(_dot(yg, gw_ref[...]) + gb_ref[...])

    blk = pl.BlockSpec((tm, W_GRP), lambda i: (i, 0))
    return pl.pallas_call(
        body, name="s5_glu_fwd", grid=(s // tm,),
        in_specs=[blk, _full((W_GRP, W_GRP)), _full((1, W_GRP))], out_specs=blk,
        out_shape=jax.ShapeDtypeStruct((s, W_GRP), F32),
        compiler_params=_cp(dimension_semantics=("parallel",)),
    )(y, gw, gb)


def _s5_glu_bwd(y, dyd, gw, gb):
    s = y.shape[0]
    tm = _tm(s)

    def body(y_ref, dyd_ref, gw_ref, gb_ref, dy_ref, dgw_ref, dgb_ref):
        i = pl.program_id(0)

        @pl.when(i == 0)
        def _():
            dgw_ref[...] = jnp.zeros_like(dgw_ref)
            dgb_ref[...] = jnp.zeros_like(dgb_ref)

        y, gw, dyd = y_ref[...], gw_ref[...], dyd_ref[...]
        yg, t = _gelu(y)
        gate = jax.nn.sigmoid(_dot(yg, gw) + gb_ref[...])
        dlin = dyd * yg * gate * (1.0 - gate)
        dgw_ref[...] += _dot_tn(yg, dlin)
        dgb_ref[...] += jnp.sum(dlin, axis=0, keepdims=True)
        dy_ref[...] = (dyd * gate + _dot_nt(dlin, gw)) * _gelu_grad(y, t)

    blk = pl.BlockSpec((tm, W_GRP), lambda i: (i, 0))
    return pl.pallas_call(
        body, name="s5_glu_bwd", grid=(s // tm,),
        in_specs=[blk, blk, _full((W_GRP, W_GRP)), _full((1, W_GRP))],
        out_specs=[blk, _full((W_GRP, W_GRP)), _full((1, W_GRP))],
        out_shape=[jax.ShapeDtypeStruct((s, W_GRP), F32), jax.ShapeDtypeStruct((W_GRP, W_GRP), F32),
                   jax.ShapeDtypeStruct((1, W_GRP), F32)],
        compiler_params=_cp(dimension_semantics=("arbitrary",)),
    )(y, dyd, gw, gb)


def _s5_core_bwd(z, dy, sp):
    s = z.shape[0]
    seg = s // N_SEG

    def body(u_ref, dy_ref, rowp, colp, br_ref, bi_ref, cr_ref, ci_ref, d_ref,
             du_ref, dbr_ref, dbi_ref, dcr_ref, dci_ref, dd_ref, da_ref, dk_ref,
             xr, xi, gr, gi, us, dys):
        ar, ai, _, _ = _s5_disc(rowp[0:1, :], rowp[1:2, :], rowp[2:3, :])
        br, bi = br_ref[...], bi_ref[...]
        kr, ki, btr, bti, ctr, cti, bmask, cmask = _s5_mats(colp[...], br, bi, cr_ref[...], ci_ref[...])
        _interleave(u_ref, us, seg)
        _interleave(dy_ref, dys, seg)
        u = us[...]
        d = d_ref[...]
        _s5_forward_states(u, btr, bti, ar, ai, xr, xi, seg)

        dy = dys[...]
        dd_ref[...] = jnp.sum(dy * u, axis=0, keepdims=True)
        du = d * dy
        dyb = dy.astype(BF16)
        dctr, dcti = [], []
        for q in range(N_SLAB):
            gr[q] = jnp.dot(dyb, ctr[:, _slab(q)], preferred_element_type=F32)
            gi[q] = -jnp.dot(dyb, cti[:, _slab(q)], preferred_element_type=F32)
            dctr.append(_dot_tn(dyb, xr[q]))
            dcti.append(-_dot_tn(dyb, xi[q]))
        selp = ((_iota((HALF_STATE, 64), 0) & 63) == _iota((HALF_STATE, 64), 1)).astype(F32)
        dcr_ref[...] = _dot_hi(jnp.where(cmask, jnp.concatenate(dctr, axis=1), 0.0), selp)
        dci_ref[...] = _dot_hi(jnp.where(cmask, jnp.concatenate(dcti, axis=1), 0.0), selp)

        _scan(gr, gi, ar, -ai, seg, True)

        dar, dai = [], []
        for q in range(N_SLAB):
            def acc_step(j, carry, q=q):
                rows, prev = _seg_rows(j), _seg_rows(j - 1)
                g_r, g_i, p_r, p_i = gr[q, rows, :], gi[q, rows, :], xr[q, prev, :], xi[q, prev, :]
                return carry[0] + g_r * p_r + g_i * p_i, carry[1] - g_r * p_i + g_i * p_r
            first, last = _seg_rows(0), _seg_rows(seg - 1)
            p_r, p_i = _sub_shift(xr[q, last, :], 1, False), _sub_shift(xi[q, last, :], 1, False)
            g_r, g_i = gr[q, first, :], gi[q, first, :]
            s_r, s_i = lax.fori_loop(1, seg, acc_step, (g_r * p_r + g_i * p_i, -g_r * p_i + g_i * p_r))
            dar.append(jnp.sum(s_r, axis=0, keepdims=True))
            dai.append(jnp.sum(s_i, axis=0, keepdims=True))
        da_ref[...] = jnp.zeros_like(da_ref)
        da_ref[0:1, :] = jnp.concatenate(dar, axis=1)
        da_ref[1:2, :] = jnp.concatenate(dai, axis=1)

        ub = u.astype(BF16)
        dbtr, dbti = [], []
        for q in range(N_SLAB):
            g_r, g_i = gr[q].astype(BF16), gi[q].astype(BF16)
            du = du + jnp.dot(g_r, btr[_slab(q), :], preferred_element_type=F32) \
                + jnp.dot(g_i, bti[_slab(q), :], preferred_element_type=F32)
            dbtr.append(_dot_tn(g_r, ub))
            dbti.append(_dot_tn(g_i, ub))
        us[...] = du
        _deinterleave(us, du_ref, seg)
        selc =((_iota((HALF_CH, 16), 0) & 15) == _iota((HALF_CH, 16), 1)).astype(F32)
        dbbr = _dot_hi(jnp.where(bmask, jnp.concatenate(dbtr, axis=0), 0.0), selc)
        dbbi = _dot_hi(jnp.where(bmask, jnp.concatenate(dbti, axis=0), 0.0), selc)
        dbr_ref[...] = kr * dbbr + ki * dbbi
        dbi_ref[...] = kr * dbbi - ki * dbbr
        dk_ref[:, 0:1] = jnp.sum(dbbr * br + dbbi * bi, axis=1, keepdims=True)
        dk_ref[:, 1:2] = jnp.sum(dbbi * br - dbbr * bi, axis=1, keepdims=True)

    half = pl.BlockSpec((s, HALF_CH), lambda i: (0, i))
    return pl.pallas_call(
        body, name="s5_core_bwd", grid=(2,),
        in_specs=[pl.BlockSpec((s, HALF_CH), lambda i: (0, 12 + i)), half] + _s5_param_specs(),
        out_specs=[half, pl.BlockSpec((HALF_STATE, 16), lambda i: (i, 0)), pl.BlockSpec((HALF_STATE, 16), lambda i: (i, 0)),
                   pl.BlockSpec((HALF_CH, 64), lambda i: (i, 0)), pl.BlockSpec((HALF_CH, 64), lambda i: (i, 0)),
                   pl.BlockSpec((1, HALF_CH), lambda i: (0, i)), pl.BlockSpec((8, HALF_STATE), lambda i: (0, i)),
                   pl.BlockSpec((HALF_STATE, 2), lambda i: (i, 0))],
        out_shape=[jax.ShapeDtypeStruct((s, W_GRP), F32), jax.ShapeDtypeStruct((N_STATE, 16), F32),
                   jax.ShapeDtypeStruct((N_STATE, 16), F32), jax.ShapeDtypeStruct((W_GRP, 64), F32),
                   jax.ShapeDtypeStruct((W_GRP, 64), F32), jax.ShapeDtypeStruct((1, W_GRP), F32),
                   jax.ShapeDtypeStruct((8, N_STATE), F32), jax.ShapeDtypeStruct((N_STATE, 2), F32)],
        scratch_shapes=[pltpu.VMEM((N_SLAB, s, LANES), F32)] * 4 + [pltpu.VMEM((s, HALF_CH), F32)] * 2,
        compiler_params=_cp(dimension_semantics=("parallel",)),
    )(z, dy, *sp)


def _s5_param_bwd(lre, lim, ldt, da_r, da_i, dk_r, dk_i):
    n = lre.shape[0]

    def body(lre_ref, lim_ref, ldt_ref, dar_ref, dai_ref, dkr_ref, dki_ref, o_re, o_im, o_dt):
        lre, lim, ldt = lre_ref[...], lim_ref[...], ldt_ref[...]
        dt = jnp.exp(ldt)
        ar, ai, kr, ki = _s5_disc(lre, lim, ldt)
        mag = jnp.exp(lre * dt)
        den = lre * lre + lim * lim
        dkr, dki = dkr_ref[...], dki_ref[...]
        nr, ni = ar - 1.0, ai
        d_ar = dar_ref[...] + (dkr * lre - dki * lim) / den
        d_ai = dai_ref[...] + (dkr * lim + dki * lre) / den
        kk = (kr * dkr + ki * dki) * 2.0 / den
        d_lre = (dkr * nr + dki * ni) / den - kk * lre
        d_lim = (dkr * ni - dki * nr) / den - kk * lim
        d_mag = (d_ar * ar + d_ai * ai) / mag
        d_ang = d_ai * ar - d_ar * ai
        o_re[...] = d_lre + d_mag * mag * dt
        o_im[...] = d_lim + d_ang * dt
        o_dt[...] = jnp.sum((d_mag * mag * lre + d_ang * lim) * dt, axis=1, keepdims=True)

    return pl.pallas_call(
        body, name="s5_param_bwd",
        out_shape=[jax.ShapeDtypeStruct((n, 64), F32), jax.ShapeDtypeStruct((n, 64), F32),
                   jax.ShapeDtypeStruct((n, 1), F32)],
    )(lre, lim, ldt, da_r, da_i, dk_r, dk_i)


def _loss_head(x, fg, target):
    s, d = x.shape
    tm = _tm(s)

    def body(x_ref, fg_ref, t_ref, loss_ref, dx_ref, dfg_ref):
        i = pl.program_id(0)

        @pl.when(i == 0)
        def _():
            loss_ref[...] = jnp.zeros_like(loss_ref)
            dfg_ref[...] = jnp.zeros_like(dfg_ref)

        xv, g = x_ref[...], fg_ref[...]
        r = lax.rsqrt(jnp.mean(xv * xv, axis=-1, keepdims=True) + EPS)
        xh = xv * r
        err = xh * g - t_ref[...]
        loss_ref[...] += 0.5 * jnp.sum(jnp.mean(err * err, axis=-1, keepdims=True), axis=0, keepdims=True)
        dy = err * (1.0 / d)
        dfg_ref[...] += jnp.sum(dy * xh, axis=0, keepdims=True)
        dxh = dy * g
        dx_ref[...] = r * (dxh - xh * jnp.mean(dxh * xh, axis=-1, keepdims=True))

    row = pl.BlockSpec((tm, d), lambda i: (i, 0))
    return pl.pallas_call(
        body, name="loss_head", grid=(s // tm,),
        in_specs=[row, _full((1, d)), row], out_specs=[_full((1, 1)), row, _full((1, d))],
        out_shape=[jax.ShapeDtypeStruct((1, 1), F32), jax.ShapeDtypeStruct((s, d), F32),
                   jax.ShapeDtypeStruct((1, d), F32)],
        compiler_params=_cp(dimension_semantics=("arbitrary",)),
    )(x, fg, target)


ADA_TN = 384


def _cond_fwd(cact, ada_w, ada_b_loc):
    nl, d, n = ada_w.shape

    def body(c_ref, w_ref, b_ref, o_ref):
        o_ref[...] = _dot(c_ref[...], w_ref[...]) + b_ref[...]

    return pl.pallas_call(
        body, name="cond_fwd", grid=(nl, n // ADA_TN),
        in_specs=[_full((N_DEV, d)), pl.BlockSpec((None, d, ADA_TN), lambda l, j: (l, 0, j)),
                  pl.BlockSpec((None, 1, ADA_TN), lambda l, j: (l, 0, j))],
        out_specs=pl.BlockSpec((None, N_DEV, ADA_TN), lambda l, j: (l, 0, j)),
        out_shape=jax.ShapeDtypeStruct((nl, N_DEV, n), F32),
        compiler_params=_cp(dimension_semantics=("parallel", "parallel")),
    )(cact, ada_w, ada_b_loc)


ELEMENTWISE_BLOCK_BYTES = 1 << 20


def _row_tile(r, c, itemsize=4):
    best = None
    for t in range(8, r + 1, 8):
        if r % t == 0 and t * c * itemsize <= ELEMENTWISE_BLOCK_BYTES:
            best = t
    return best if best is not None else r


def _adamw_math(w, g, m, v):
    m = ADAM_B1 * m + (1.0 - ADAM_B1) * g
    v = ADAM_B2 * v + (1.0 - ADAM_B2) * (g * g)
    m_hat = m / (1.0 - ADAM_B1 ** ADAM_STEP)
    v_hat = v / (1.0 - ADAM_B2 ** ADAM_STEP)
    delta = -ADAM_LR * (m_hat / (jnp.sqrt(v_hat) + ADAM_EPS) + ADAM_WD * w)
    return delta, m, v


def _ada_w_update(cact, dcond_loc, w, m, v):
    nl, d, n = w.shape

    def body(c_ref, dc_ref, w_ref, m_ref, v_ref, g_out, d_out, m_out, v_out):
        g = _dot_tn(c_ref[...], dc_ref[...])
        g_out[...] = g
        d_out[...], m_out[...], v_out[...] = _adamw_math(w_ref[...], g, m_ref[...], v_ref[...])

    blk = pl.BlockSpec((None, d, ADA_TN), lambda l, j: (l, 0, j))
    return pl.pallas_call(
        body, name="ada_w_update", grid=(nl, n // ADA_TN),
        in_specs=[_full((N_DEV, d)), pl.BlockSpec((None, N_DEV, ADA_TN), lambda l, j: (l, 0, j)), blk, blk, blk],
        out_specs=[blk] * 4, out_shape=[jax.ShapeDtypeStruct((nl, d, n), F32)] * 4,
        compiler_params=_cp(dimension_semantics=("parallel", "parallel")),
    )(cact, dcond_loc, w, m, v)


def _adamw(w, g, m, v, name):
    b, r, c = w.shape
    tr = _row_tile(r, c)

    def body(w_ref, g_ref, m_ref, v_ref, d_out, m_out, v_out):
        d_out[...], m_out[...], v_out[...] = _adamw_math(w_ref[...], g_ref[...], m_ref[...], v_ref[...])

    blk = pl.BlockSpec((None, tr, c), lambda i, j: (i, j, 0))
    return pl.pallas_call(
        body, name=name, grid=(b, r // tr), in_specs=[blk] * 4, out_specs=[blk] * 3,
        out_shape=[jax.ShapeDtypeStruct((b, r, c), F32)] * 3,
        compiler_params=_cp(dimension_semantics=("parallel", "parallel")),
    )(w, g, m, v)


def _place():
    x, y, c = lax.axis_index("x"), lax.axis_index("y"), lax.axis_index("c")
    chips = [(1 - x, y), (x, 1 - y), (1 - x, 1 - y)]
    return x, y, c, chips


def _remote(src, dst, send_sem, recv_sem, to):
    return pltpu.make_async_remote_copy(src_ref=src, dst_ref=dst, send_sem=send_sem, recv_sem=recv_sem,
                                        device_id=to, device_id_type=MESH_ID)


def _sems(n):
    return [pltpu.SemaphoreType.DMA((n,)), pltpu.SemaphoreType.DMA((n,))]


def _all_gather8(v, name):
    r, cdim = v.shape

    def body(x_ref, out_ref, stage, send_sems, recv_sems):
        x, y, c, chips = _place()
        sibling = (x, y, 1 - c)

        def slot(px, py, pc):
            return out_ref.at[4 * px + 2 * py + pc]

        first = [_remote(x_ref, slot(x, y, c), send_sems.at[0], recv_sems.at[0], sibling)]
        first += [_remote(x_ref, slot(x, y, c), send_sems.at[1 + j], recv_sems.at[1 + j], (*chip, c))
                  for j, chip in enumerate(chips)]
        for cp in first:
            cp.start()
        pltpu.sync_copy(x_ref, stage)
        pltpu.sync_copy(stage, slot(x, y, c))
        passed = []
        for j, chip in enumerate(chips):
            blk = slot(*chip, c)
            _remote(blk, blk, send_sems.at[1 + j], recv_sems.at[1 + j], (x, y, c)).wait_recv()
            fw = _remote(blk, blk, send_sems.at[4 + j], recv_sems.at[4 + j], sibling)
            fw.start()
            passed.append(fw)
        blk = slot(x, y, 1 - c)
        _remote(blk, blk, send_sems.at[0], recv_sems.at[0], (x, y, c)).wait_recv()
        for j, chip in enumerate(chips):
            blk = slot(*chip, 1 - c)
            _remote(blk, blk, send_sems.at[4 + j], recv_sems.at[4 + j], (x, y, c)).wait_recv()
        for cp in first + passed:
            cp.wait_send()

    return pl.pallas_call(
        body, name=name, out_shape=jax.ShapeDtypeStruct((N_DEV, r, cdim), v.dtype),
        in_specs=[ANY], out_specs=ANY,
        scratch_shapes=[pltpu.VMEM((r, cdim), v.dtype)] + _sems(7),
        compiler_params=_cp(),
    )(v)


def _place_weights(ws, layer, kidx):
    steps = 4
    shapes, in_specs, out_specs = [], [], []
    for w, kind in zip(ws, BIG_KINDS):
        _, a, b = w.shape
        in_specs.append(pl.BlockSpec((None, a // steps, b), lambda i, k: (layer, i, 0)))
        if kind == "col":
            shapes.append((2, a, 2 * b))
            out_specs.append(pl.BlockSpec((None, a // steps, b), lambda i, k: (k[0] // 2, i, k[0] % 2)))
        else:
            shapes.append((N_CHIP, a, b))
            out_specs.append(pl.BlockSpec((None, a // steps, b), lambda i, k: (k[0], i, 0)))

    def body(k_ref, *refs):
        for t in range(len(ws)):
            refs[len(ws) + t][...] = refs[t][...].astype(BF16)

    return pl.pallas_call(
        body, name="place_weights", out_shape=[jax.ShapeDtypeStruct(s, BF16) for s in shapes],
        grid_spec=pltpu.PrefetchScalarGridSpec(num_scalar_prefetch=1, grid=(steps,), in_specs=in_specs,
                                               out_specs=out_specs),
        compiler_params=_cp(dimension_semantics=("parallel",)),
    )(kidx, *ws)


HBM = pl.BlockSpec(memory_space=pltpu.HBM)
SEM = pl.BlockSpec(memory_space=pltpu.SEMAPHORE)
EFFECT = pltpu.SideEffectType.DATAFLOW_SIDE_EFFECTING


def _weight_block(ref, kind, k, h):
    if kind == "col":
        ncol = ref.shape[3] // 2
        return ref.at[k // 2, h, :, pl.ds(pl.multiple_of((k % 2) * ncol, LANES), ncol)]
    return ref.at[k, h]


def _in_hbm(a):
    return pltpu.with_memory_space_constraint(a, pltpu.HBM)


def _weight_send_start(placed, kinds, name):
    nt = len(placed)

    def body(*refs):
        send_sems, recv_sems = refs[nt], refs[nt + 1]
        dst = refs[nt + 2:2 * nt + 2]
        token = refs[2 * nt + 2]
        x, y, c, chips = _place()
        kme = 2 * x + y
        for t in range(nt):
            for j, chip in enumerate(chips):
                own = _weight_block(dst[t], kinds[t], kme, c)
                _remote(own, own, send_sems.at[3 * t + j], recv_sems.at[3 * t + j], (*chip, c)).start()
        token[...] = jnp.zeros_like(token)

    return pl.pallas_call(
        body, name=name,
        out_shape=(pltpu.SemaphoreType.DMA((3 * nt,)), pltpu.SemaphoreType.DMA((3 * nt,)),
                   *[pltpu.HBM(a.shape, a.dtype) for a in placed], jax.ShapeDtypeStruct((8, LANES), F32)),
        in_specs=[HBM] * nt, out_specs=(SEM, SEM, *[HBM] * nt, pl.BlockSpec(memory_space=pltpu.VMEM)),
        input_output_aliases={t: 2 + t for t in range(nt)},
        compiler_params=pltpu.CompilerParams(has_side_effects=EFFECT),
    )(*[_in_hbm(a) for a in placed])


def _weight_send_wait(send_sems, recv_sems, arrays, kinds, after, name):
    nt = len(arrays)

    def body(*refs):
        arr = refs[:nt]
        send_sems, recv_sems = refs[nt], refs[nt + 1]
        x, y, c, chips = _place()
        kme = 2 * x + y
        for t in range(nt):
            for j, chip in enumerate(chips):
                own = _weight_block(arr[t], kinds[t], kme, c)
                got = _weight_block(arr[t], kinds[t], 2 * chip[0] + chip[1], c)
                cp = _remote(own, got, send_sems.at[3 * t + j], recv_sems.at[3 * t + j], (*chip, c))
                cp.wait_send()
                cp.wait_recv()

    return pl.pallas_call(
        body, name=name, out_shape=[pltpu.HBM(a.shape, a.dtype) for a in arrays],
        in_specs=[HBM] * nt + [SEM, SEM, ANY], out_specs=[HBM] * nt,
        input_output_aliases={t: t for t in range(nt)},
        compiler_params=pltpu.CompilerParams(has_side_effects=EFFECT),
    )(*arrays, send_sems, recv_sems, after)


def _weight_forward(arrays, kinds):
    nt = len(arrays)

    def body(*refs):
        dst = refs[nt:2 * nt]
        send_sems, recv_sems = refs[2 * nt:]
        x, y, c, chips = _place()
        sends = []
        for t in range(nt):
            for j, chip in enumerate(chips):
                blk = _weight_block(dst[t], kinds[t], 2 * chip[0] + chip[1], c)
                fw = _remote(blk, blk, send_sems.at[3 * t + j], recv_sems.at[3 * t + j], (x, y, 1 - c))
                fw.start()
                sends.append(fw)
        for t in range(nt):
            for j, chip in enumerate(chips):
                blk = _weight_block(dst[t], kinds[t], 2 * chip[0] + chip[1], 1 - c)
                _remote(blk, blk, send_sems.at[3 * t + j], recv_sems.at[3 * t + j], (x, y, c)).wait_recv()
        for cp in sends:
            cp.wait_send()

    return pl.pallas_call(
        body, name="weight_forward",
        out_shape=[jax.ShapeDtypeStruct(a.shape, a.dtype) for a in arrays],
        in_specs=[ANY] * nt, out_specs=[ANY] * nt, input_output_aliases={t: t for t in range(nt)},
        scratch_shapes=_sems(3 * nt),
    )(*arrays)


def _sibling_exchange(views):
    nt = len(views)

    def body(*refs):
        src, land = refs[:nt], refs[nt:2 * nt]
        send_sems, recv_sems = refs[2 * nt:]
        x, y, c, _ = _place()
        cps = [_remote(src[t].at[:, 1 - c], land[t], send_sems.at[t], recv_sems.at[t], (x, y, 1 - c))
               for t in range(nt)]
        for cp in cps:
            cp.start()
        for cp in cps:
            cp.wait()

    return pl.pallas_call(
        body, name="grad_sibling_exchange",
        out_shape=[jax.ShapeDtypeStruct((v.shape[0],) + v.shape[2:], v.dtype) for v in views],
        in_specs=[ANY] * nt, out_specs=[ANY] * nt, scratch_shapes=_sems(nt),
    )(*views)


def _scatter_copies(src, land, kinds, send_sems, recv_sems):
    x, y, c, chips = _place()
    cps = []
    for t in range(len(src)):
        for j, chip in enumerate(chips):
            k = 2 * chip[0] + chip[1]
            if kinds[t] == "col":
                ncol = land[t].shape[2]
                win = src[t].at[k // 2, :, pl.ds(pl.multiple_of((k % 2) * ncol, LANES), ncol)]
            else:
                win = src[t].at[k]
            cps.append(_remote(win, land[t].at[j], send_sems.at[3 * t + j], recv_sems.at[3 * t + j], (*chip, c)))
    return cps


def _chip_scatter_start(parts, kinds, name):
    nt = len(parts)
    shapes = []
    for p, kind in zip(parts, kinds):
        shapes.append((3, p.shape[1], p.shape[2] // 2) if kind == "col" else (3,) + p.shape[1:])

    def body(*refs):
        send_sems, recv_sems = refs[2 * nt], refs[2 * nt + 1]
        src, land = refs[2 * nt + 2:3 * nt + 2], refs[3 * nt + 2:4 * nt + 2]
        token = refs[4 * nt + 2]
        for cp in _scatter_copies(src, land, kinds, send_sems, recv_sems):
            cp.start()
        token[...] = jnp.zeros_like(token)

    lands = [lax.empty(s, BF16) for s in shapes]
    return pl.pallas_call(
        body, name=name,
        out_shape=(pltpu.SemaphoreType.DMA((3 * nt,)), pltpu.SemaphoreType.DMA((3 * nt,)),
                   *[pltpu.HBM(a.shape, a.dtype) for a in parts], *[pltpu.HBM(s, BF16) for s in shapes],
                   jax.ShapeDtypeStruct((8, LANES), F32)),
        in_specs=[HBM] * (2 * nt), out_specs=(SEM, SEM, *[HBM] * (2 * nt), pl.BlockSpec(memory_space=pltpu.VMEM)),
        input_output_aliases={t: 2 + t for t in range(2 * nt)},
        compiler_params=pltpu.CompilerParams(has_side_effects=EFFECT),
    )(*[_in_hbm(a) for a in parts], *[_in_hbm(a) for a in lands])


def _chip_scatter_wait(send_sems, recv_sems, parts, lands, kinds, after, name):
    nt = len(parts)

    def body(*refs):
        src, land = refs[:nt], refs[nt:2 * nt]
        send_sems, recv_sems = refs[2 * nt], refs[2 * nt + 1]
        for cp in _scatter_copies(src, land, kinds, send_sems, recv_sems):
            cp.wait_send()
            cp.wait_recv()

    outs = pl.pallas_call(
        body, name=name, out_shape=[pltpu.HBM(a.shape, a.dtype) for a in list(parts) + list(lands)],
        in_specs=[HBM] * (2 * nt) + [SEM, SEM, ANY], out_specs=[HBM] * (2 * nt),
        input_output_aliases={t: t for t in range(2 * nt)},
        compiler_params=pltpu.CompilerParams(has_side_effects=EFFECT),
    )(*parts, *lands, send_sems, recv_sems, after)
    return outs[:nt], outs[nt:]


def _sibling_share(fulls):
    nt = len(fulls)

    def body(*refs):
        dst = refs[nt:2 * nt]
        send_sems, recv_sems = refs[2 * nt:]
        x, y, c, _ = _place()
        cps = []
        for t in range(nt):
            mine = dst[t].at[c]
            cp = _remote(mine, mine, send_sems.at[t], recv_sems.at[t], (x, y, 1 - c))
            cp.start()
            cps.append(cp)
        for t in range(nt):
            other = dst[t].at[1 - c]
            _remote(other, other, send_sems.at[t], recv_sems.at[t], (x, y, c)).wait_recv()
        for cp in cps:
            cp.wait_send()

    return pl.pallas_call(
        body, name="grad_sibling_share",
        out_shape=[jax.ShapeDtypeStruct(f.shape, f.dtype) for f in fulls],
        in_specs=[ANY] * nt, out_specs=[ANY] * nt, input_output_aliases={t: t for t in range(nt)},
        scratch_shapes=_sems(nt),
    )(*fulls)


SUM_STEPS = 4


def _pair_sum(views, lands, ck):
    nt = len(views)
    in_specs, out_specs, shapes = [], [], []
    for v in views:
        b, _, r, cc = v.shape
        per = SUM_STEPS // b
        tr = r // per
        in_specs.append(pl.BlockSpec((None, None, tr, cc), lambda i, s, per=per: (i // per, s[0], i % per, 0)))
        out_specs.append(pl.BlockSpec((None, tr, cc), lambda i, s, per=per: (i // per, i % per, 0)))
        shapes.append((b, r, cc))
    in_specs = in_specs + out_specs

    def body(s_ref, *refs):
        for t in range(nt):
            refs[2 * nt + t][...] = (refs[t][...].astype(F32) + refs[nt + t][...].astype(F32)).astype(BF16)

    return pl.pallas_call(
        body, name="grad_pair_sum", out_shape=[jax.ShapeDtypeStruct(s, BF16) for s in shapes],
        grid_spec=pltpu.PrefetchScalarGridSpec(num_scalar_prefetch=1, grid=(SUM_STEPS,), in_specs=in_specs,
                                               out_specs=out_specs),
        compiler_params=_cp(dimension_semantics=("parallel",)),
    )(ck, *views, *lands)


def _chip_sum(parts, lands, kinds, ck):
    nt = len(parts)
    steps = 2
    in_own, in_land, out_specs, shapes = [], [], [], []
    for ld, kind in zip(lands, kinds):
        _, r, cc = ld.shape
        tr = r // steps
        if kind == "col":
            in_own.append(pl.BlockSpec((None, tr, cc), lambda i, s: (s[1] // 2, i, s[1] % 2)))
        else:
            in_own.append(pl.BlockSpec((None, tr, cc), lambda i, s: (s[1], i, 0)))
        in_land.append(pl.BlockSpec((3, tr, cc), lambda i, s: (0, i, 0)))
        out_specs.append(pl.BlockSpec((None, tr, cc), lambda i, s: (s[0], i, 0)))
        shapes.append((2, r, cc))

    def body(s_ref, *refs):
        for t in range(nt):
            acc = refs[t][...].astype(F32)
            for j in range(3):
                acc = acc + refs[nt + t][j].astype(F32)
            refs[2 * nt + t][...] = acc

    return pl.pallas_call(
        body, name="grad_chip_sum", out_shape=[jax.ShapeDtypeStruct(s, F32) for s in shapes],
        grid_spec=pltpu.PrefetchScalarGridSpec(num_scalar_prefetch=1, grid=(steps,), in_specs=in_own + in_land,
                                               out_specs=out_specs),
        compiler_params=_cp(dimension_semantics=("parallel",)),
    )(ck, *parts, *lands)


def _sum8(g):
    _, r, cc = g.shape
    tr = _row_tile(r, N_DEV * cc)

    def body(g_ref, o_ref):
        acc = g_ref[0].astype(F32)
        for d in range(1, N_DEV):
            acc = acc + g_ref[d].astype(F32)
        o_ref[...] = acc

    return pl.pallas_call(
        body, name="small_grad_sum", grid=(r // tr,),
        in_specs=[pl.BlockSpec((N_DEV, tr, cc), lambda i: (0, i, 0))],
        out_specs=pl.BlockSpec((tr, cc), lambda i: (i, 0)),
        out_shape=jax.ShapeDtypeStruct((r, cc), F32),
        compiler_params=_cp(dimension_semantics=("parallel",)),
    )(g)


def _silu_rows(c):
    def body(c_ref, o_ref):
        v = c_ref[...]
        o_ref[...] = v * jax.nn.sigmoid(v)

    return pl.pallas_call(body, name="cond_silu", out_shape=jax.ShapeDtypeStruct(c.shape, F32))(c)


def _pack(arrays):
    rows = []
    for a in arrays:
        flat = a.reshape(-1)
        rows.append(jnp.pad(flat, (0, (-flat.shape[0]) % (8 * LANES))).reshape(-1, LANES))
    n = sum(r.shape[0] for r in rows)
    if n % 256:
        rows.append(jnp.zeros((256 - n % 256, LANES), rows[0].dtype))
    return jnp.concatenate(rows, axis=0)


def _unpack(packed, shapes):
    out, off = [], 0
    for s in shapes:
        n = math.prod(s)
        nr = 8 * -(-n // (8 * LANES))
        out.append(packed[off:off + nr].reshape(-1)[:n].reshape(s))
        off += nr
    return out


def _as_rows(a):
    return a.reshape(1, -1) if a.ndim == 1 else a.reshape(-1, a.shape[-1])


def _adamw_many(ws, gs, ms, vs, name, steps=1):
    nt = len(ws)

    def body(*refs):
        for t in range(nt):
            w_ref, g_ref, m_ref, v_ref = (refs[k * nt + t] for k in range(4))
            d, m, v = _adamw_math(w_ref[...], g_ref[...], m_ref[...], v_ref[...])
            refs[4 * nt + t][...] = d
            refs[5 * nt + t][...] = m
            refs[6 * nt + t][...] = v

    shapes = [jax.ShapeDtypeStruct(a.shape, F32) for a in ws]
    if steps == 1:
        outs = pl.pallas_call(body, name=name, out_shape=shapes * 3, compiler_params=_cp())(*ws, *gs, *ms, *vs)
    else:
        specs = [pl.BlockSpec((a.shape[0] // steps, a.shape[1]), lambda i: (i, 0)) for a in ws]
        outs = pl.pallas_call(
            body, name=name, grid=(steps,), in_specs=specs * 4, out_specs=specs * 3, out_shape=shapes * 3,
            compiler_params=_cp(dimension_semantics=("parallel",)),
        )(*ws, *gs, *ms, *vs)
    return outs[:nt], outs[nt:2 * nt], outs[2 * nt:]


def _reduce_big_grads(grads, kinds, ck, layer):
    views = []
    for g, kind in zip(grads, kinds):
        if kind == "col":
            views.append(g.reshape(2, 2, g.shape[1] // 2, g.shape[2]))
        else:
            views.append(g.reshape(N_CHIP, 2, g.shape[0] // (2 * N_CHIP), g.shape[1]))
    lands = _sibling_exchange(views)
    parts = _pair_sum(views, lands, ck)
    return _chip_scatter_start(parts, kinds, "grad_scatter_start_%d" % layer)


def _finish_big_grads(started, kinds, ck, after, layer):
    nt = len(kinds)
    send_sems, recv_sems = started[0], started[1]
    parts, lands = started[2:2 + nt], started[2 + nt:2 + 2 * nt]
    parts, lands = _chip_scatter_wait(send_sems, recv_sems, parts, lands, kinds, after, "grad_scatter_wait_%d" % layer)
    fulls = _sibling_share(_chip_sum(parts, lands, kinds, ck))
    return [f.reshape(2 * f.shape[1], f.shape[2]) for f in fulls]


SMALL_NAMES = ["ada_b", "norm1_g", "norm2_g", "sgu_w", "sgu_b", "pool_w", "pool_scale", "conv_w", "s5_lambda_re",
               "s5_lambda_im", "s5_b_re", "s5_b_im", "s5_c_re", "s5_c_im", "s5_d", "s5_log_dt", "s5_glu_w", "s5_glu_b",
               "mix_norm_g", "norm3_g", "final_norm_g"]
BIG_NAMES = ["ffn1_w_in", "ffn1_w_out", "w_mix_in", "w_mix_out", "ffn2_w_in", "ffn2_w_out"]
BIG_KINDS = ["col", "row", "row", "row", "col", "row"]
WEIGHT_ORDER = ["ada_w", "ada_b", "norm1_g", "ffn1_w_in", "ffn1_w_out", "norm2_g", "w_mix_in", "sgu_w", "sgu_b", "pool_w",
                "pool_scale", "conv_w", "s5_lambda_re", "s5_lambda_im", "s5_b_re", "s5_b_im", "s5_c_re", "s5_c_im", "s5_d",
                "s5_log_dt", "s5_glu_w", "s5_glu_b", "mix_norm_g", "w_mix_out", "norm3_g", "ffn2_w_in", "ffn2_w_out",
                "final_norm_g"]


def _local_step(x, target, cond, fetch_weights, p, emit_grads):
    nl, d = DEPTH, x.shape[1]
    row = lambda a: a.reshape(1, -1)
    saved = []
    for l in range(nl):
        (wi1, wo1, wmit, wmo, wi2, wo2), tok = fetch_weights(l, x)
        cl = cond[l] + tok
        mod1, mod2, mod3 = cl[0:3], cl[3:6], cl[6:9]
        lre, lim = p["s5_lambda_re"][l].reshape(-1), p["s5_lambda_im"][l].reshape(-1)
        ldt = jnp.repeat(p["s5_log_dt"][l], 64)
        rowp = jnp.stack([lre, lim, ldt])
        sp = (rowp, rowp.T, p["s5_b_re"][l].reshape(N_STATE, 16), p["s5_b_im"][l].reshape(N_STATE, 16),
              p["s5_c_re"][l].reshape(W_GRP, 64), p["s5_c_im"][l].reshape(W_GRP, 64), row(p["s5_d"][l]))
        glu = (p["s5_glu_w"][l], row(p["s5_glu_b"][l]))
        bias_full = jnp.repeat(p["sgu_b"][l].T, 64, axis=1)
        pw2 = p["pool_w"][l].reshape(W_GRP, 64)
        x1, h1, a1, b1, o1 = _ffn_fwd(x, mod1, row(p["norm1_g"][l]), wi1, wo1)
        z, h2 = _mix_in_fwd(x1, mod2, row(p["norm2_g"][l]), wmit)
        ya = _sgu_fwd(z, p["sgu_w"][l], bias_full)
        yb, yc = _poolconv_fwd(z, pw2, row(p["pool_scale"][l]), p["conv_w"][l])
        ypre = _s5_core_fwd(z, sp)
        yd = _s5_glu_fwd(ypre, *glu)
        ys = (ya, yb, yc, yd)
        x2, m = _mix_out_fwd(ys, row(p["mix_norm_g"][l]), wmo, x1, mod2[2:3])
        x3, h3, a3, b3, o3 = _ffn_fwd(x2, mod3, row(p["norm3_g"][l]), wi2, wo2)
        saved.append((x, x1, x2, h1, a1, b1, o1, z, h2, ys, m, h3, a3, b3, o3, sp, bias_full, pw2, ypre, glu,
                      (wi1, wo1, wmit, wmo, wi2, wo2), cl))
        x = x3

    loss, dx, dfg = _loss_head(x, row(p["final_norm_g"]), target)

    sg = {n: [None] * nl for n in SMALL_NAMES if n not in ("ada_b", "final_norm_g")}
    dcond = [None] * nl
    s5_da, s5_dk = [None] * nl, [None] * nl
    tok = 0.0
    for l in reversed(range(nl)):
        (x0, x1, x2, h1, a1, b1, o1, z, h2, ys, m, h3, a3, b3, o3, sp, bias_full, pw2, ypre, glu,
         (wi1, wo1, wmit, wmo, wi2, wo2), cl) = saved[l]
        cl = cl + tok
        mod1, mod2, mod3 = cl[0:3], cl[3:6], cl[6:9]
        do, dgate3 = _gate_bwd(dx, o3, mod3[2:3], 0.5)
        dza, dzb, dwi2, dwo2 = _ffn_bwd_main(do, h3, a3, b3, wo2)
        dx, rows3 = _ffn_bwd_in(dza, dzb, wi2, x2, dx, mod3, row(p["norm3_g"][l]))
        outs = _mix_out_bwd(dx, m, mod2[2:3], ys, row(p["mix_norm_g"][l]), wmo)
        dys, dgate2, dmng, dwmo = outs[0:4], outs[4], outs[5], outs[6]
        dza_, dsw, dsb = _sgu_bwd(z, dys[0], p["sgu_w"][l], bias_full)
        dzb_, dzc_, dwbd, dps, dcw = _poolconv_bwd(z, dys[1], dys[2], pw2, row(p["pool_scale"][l]), p["conv_w"][l])
        dypre, dgw, dgb = _s5_glu_bwd(ypre, dys[3], *glu)
        dzd_, dbr, dbi, dcr, dci, dd, da, dk = _s5_core_bwd(z, dypre, sp)
        dx, rows2, dwmit = _mix_in_bwd((dza_, dzb_, dzc_, dzd_), h2, wmit, x1, dx, mod2, row(p["norm2_g"][l]))
        do, dgate1 = _gate_bwd(dx, o1, mod1[2:3], 0.5)
        dza, dzb, dwi1, dwo1 = _ffn_bwd_main(do, h1, a1, b1, wo1)
        dx, rows1 = _ffn_bwd_in(dza, dzb, wi1, x0, dx, mod1, row(p["norm1_g"][l]))

        tok = emit_grads(l, [dwi1, dwo1, dwmit, dwmo, dwi2, dwo2], dx)
        dcond[l] = jnp.concatenate([rows1[0:2], dgate1, rows2[0:2], dgate2, rows3[0:2], dgate3], axis=0)
        sg["norm1_g"][l], sg["norm2_g"][l], sg["norm3_g"][l] = rows1[2], rows2[2], rows3[2]
        sg["mix_norm_g"][l] = dmng[0]
        sg["sgu_w"][l] = dsw
        sg["sgu_b"][l] = dsb[:, 0:4].T
        g4 = dwbd.reshape(4, 64, 4, 64)
        sg["pool_w"][l] = jnp.stack([g4[k, :, k, :] for k in range(4)])
        sg["pool_scale"][l] = dps[0]
        sg["conv_w"][l] = dcw[0:3]
        sg["s5_b_re"][l], sg["s5_b_im"][l] = dbr.reshape(16, 64, 16), dbi.reshape(16, 64, 16)
        sg["s5_c_re"][l], sg["s5_c_im"][l] = dcr.reshape(16, 16, 64), dci.reshape(16, 16, 64)
        sg["s5_d"][l] = dd[0]
        sg["s5_glu_w"][l], sg["s5_glu_b"][l] = dgw, dgb[0]
        s5_da[l], s5_dk[l] = da, dk

    n16 = nl * 16
    dlre, dlim, dldt = _s5_param_bwd(
        p["s5_lambda_re"].reshape(n16, 64), p["s5_lambda_im"].reshape(n16, 64),
        jnp.repeat(p["s5_log_dt"].reshape(n16, 1), 64, axis=1),
        jnp.stack([a[0] for a in s5_da]).reshape(n16, 64), jnp.stack([a[1] for a in s5_da]).reshape(n16, 64),
        jnp.stack([k[:, 0] for k in s5_dk]).reshape(n16, 64), jnp.stack([k[:, 1] for k in s5_dk]).reshape(n16, 64))
    small = {n: jnp.stack(v) for n, v in sg.items() if v[0] is not None}
    small["s5_lambda_re"] = dlre.reshape(nl, 16, 64)
    small["s5_lambda_im"] = dlim.reshape(nl, 16, 64)
    small["s5_log_dt"] = dldt.reshape(nl, 16)
    small["final_norm_g"] = dfg[0]
    return loss, dx, small, jnp.stack(dcond)


def kernel(x, c, ada_w, ada_b, norm1_g, ffn1_w_in, ffn1_w_out, norm2_g, w_mix_in, sgu_w, sgu_b, pool_w, pool_scale, conv_w, s5_lambda_re, s5_lambda_im, s5_b_re, s5_b_im, s5_c_re, s5_c_im, s5_d, s5_log_dt, s5_glu_w, s5_glu_b, mix_norm_g, w_mix_out, norm3_g, ffn2_w_in, ffn2_w_out, final_norm_g, loss_target, m_ada_w, m_ada_b, m_norm1_g, m_ffn1_w_in, m_ffn1_w_out, m_norm2_g, m_w_mix_in, m_sgu_w, m_sgu_b, m_pool_w, m_pool_scale, m_conv_w, m_s5_lambda_re, m_s5_lambda_im, m_s5_b_re, m_s5_b_im, m_s5_c_re, m_s5_c_im, m_s5_d, m_s5_log_dt, m_s5_glu_w, m_s5_glu_b, m_mix_norm_g, m_w_mix_out, m_norm3_g, m_ffn2_w_in, m_ffn2_w_out, m_final_norm_g, v_ada_w, v_ada_b, v_norm1_g, v_ffn1_w_in, v_ffn1_w_out, v_norm2_g, v_w_mix_in, v_sgu_w, v_sgu_b, v_pool_w, v_pool_scale, v_conv_w, v_s5_lambda_re, v_s5_lambda_im, v_s5_b_re, v_s5_b_im, v_s5_c_re, v_s5_c_im, v_s5_d, v_s5_log_dt, v_s5_glu_w, v_s5_glu_b, v_mix_norm_g, v_w_mix_out, v_norm3_g, v_ffn2_w_in, v_ffn2_w_out, v_final_norm_g):
    args = dict(locals())
    w = {n: args[n] for n in WEIGHT_ORDER}
    mom = {n: args["m_" + n] for n in WEIGHT_ORDER}
    vel = {n: args["v_" + n] for n in WEIGHT_ORDER}
    nl, d = DEPTH, x.shape[-1]
    s = x.shape[1]
    px, py, pc = lax.axis_index("x"), lax.axis_index("y"), lax.axis_index("c")
    kme = 2 * px + py
    me = 2 * kme + pc
    kidx = jnp.reshape(kme, (1,)).astype(jnp.int32)

    shards = [ffn1_w_in, ffn1_w_out, jnp.swapaxes(w_mix_in, 1, 2), w_mix_out, ffn2_w_in, ffn2_w_out]
    started_weights = {}

    def start_weights(l):
        placed = _place_weights(shards, l, kidx)
        views = [a.reshape(a.shape[0], 2, a.shape[1] // 2, a.shape[2]) for a in placed]
        *handles, token = _weight_send_start(views, BIG_KINDS, "weight_send_start_%d" % l)
        started_weights[l] = handles
        return token[0, 0]

    tok0 = start_weights(0)

    cact = _silu_rows(c + tok0)
    pre = _pack([cact, conv_w, s5_glu_w])
    pre_all = _all_gather8(pre, "gather_prelude")
    parts = [_unpack(pre_all[dev], [cact.shape, conv_w.shape, s5_glu_w.shape]) for dev in range(N_DEV)]
    cact_all = pre_all[:, :d // LANES, :].reshape(N_DEV, d)
    conv_full = jnp.concatenate([parts[2 * k][1] for k in range(N_CHIP)], axis=2)
    glu_full = jnp.concatenate([parts[2 * k][2] for k in range(N_CHIP)], axis=1)

    n_ada = ada_w.shape[2]
    ada_b_loc = lax.dynamic_slice_in_dim(ada_b, kme * n_ada, n_ada, axis=1).reshape(nl, 1, n_ada)
    cond_part = _cond_fwd(cact_all, ada_w, ada_b_loc)
    cond_all = _all_gather8(cond_part.reshape(nl * N_DEV, n_ada), "gather_cond").reshape(N_DEV, nl, N_DEV, n_ada)
    cond_me = jnp.concatenate(
        [lax.dynamic_index_in_dim(cond_all[2 * k], me, axis=1, keepdims=False) for k in range(N_CHIP)], axis=1)
    cond = cond_me.reshape(nl, 9, d)

    def fetch_weights(l, after):
        send_sems, recv_sems, *views = started_weights.pop(l)
        views = _weight_send_wait(send_sems, recv_sems, views, BIG_KINDS, after, "weight_send_wait_%d" % l)
        views = _weight_forward(views, BIG_KINDS)
        tok = start_weights(l + 1) if l + 1 < nl else 0.0
        full = [v.reshape(2, 2 * v.shape[2], v.shape[3]) if kind == "col" else v.reshape(-1, v.shape[3])
                for v, kind in zip(views, BIG_KINDS)]
        return full, tok

    ck = jnp.stack([pc, kme]).astype(jnp.int32)
    reduced = [None] * nl
    pending = []

    def emit_grads(l, grads_l, after):
        started = _reduce_big_grads(grads_l, BIG_KINDS, ck, l)
        if pending:
            prev, prev_started = pending.pop()
            reduced[prev] = _finish_big_grads(prev_started, BIG_KINDS, ck, after, prev)
        pending.append((l, started))
        return started[-1][0, 0]

    p = {n: w[n] for n in SMALL_NAMES}
    p["conv_w"], p["s5_glu_w"] = conv_full, glu_full
    loss, dx, small, dcond = _local_step(x[0], loss_target[0], cond, fetch_weights, p, emit_grads)

    small_order = [n for n in SMALL_NAMES if n != "ada_b"]
    packed = _pack([dcond] + [small[n] for n in small_order])
    gathered_small = _all_gather8(packed.astype(BF16), "gather_small_grads")
    total = _sum8(gathered_small)
    shapes = [dcond.shape] + [small[n].shape for n in small_order]
    tot = dict(zip(["ada_b"] + small_order, _unpack(total, shapes)))
    grads = {n: tot[n] for n in SMALL_NAMES}
    grads["ada_b"] = tot["ada_b"].reshape(nl, 9 * d)
    grads["conv_w"] = lax.dynamic_slice_in_dim(tot["conv_w"], kme * conv_w.shape[2], conv_w.shape[2], axis=2)
    grads["s5_glu_w"] = lax.dynamic_slice_in_dim(tot["s5_glu_w"], kme * s5_glu_w.shape[1], s5_glu_w.shape[1], axis=1)

    dcond_all = gathered_small.reshape(N_DEV, -1)[:, :dcond.size].reshape(N_DEV, nl, 9 * d)
    dcond_loc = jnp.swapaxes(lax.dynamic_slice_in_dim(dcond_all, kme * n_ada, n_ada, axis=2), 0, 1)
    g_ada, d_ada, m_ada, v_ada = _ada_w_update(cact_all, dcond_loc, ada_w, m_ada_w, v_ada_w)

    last, last_started = pending.pop()
    reduced[last] = _finish_big_grads(last_started, BIG_KINDS, ck, g_ada, last)
    for t, n in enumerate(BIG_NAMES):
        g = jnp.stack([reduced[l][t] for l in range(nl)])
        grads[n] = jnp.swapaxes(g, 1, 2) if n == "w_mix_in" else g

    delta, new_m, new_v = {}, {}, {}
    grads["ada_w"], delta["ada_w"], new_m["ada_w"], new_v["ada_w"] = g_ada, d_ada, m_ada, v_ada
    for n in BIG_NAMES:
        delta[n], new_m[n], new_v[n] = _adamw(w[n], grads[n], mom[n], vel[n], "adamw_" + n)
    wide = ("s5_b_re", "s5_b_im")
    for names, call, steps in (([n for n in SMALL_NAMES if n not in wide], "adamw_small", 1),
                               (list(wide), "adamw_s5_b", DEPTH)):
        outs = _adamw_many(*[[_as_rows(t[n]) for n in names] for t in (w, grads, mom, vel)], call, steps)
        for res, o in zip((delta, new_m, new_v), outs):
            res.update({n: a.reshape(w[n].shape) for n, a in zip(names, o)})

    loss_total = lax.psum(loss[0, 0], ("x", "y", "c"))
    return (loss_total, dx[None], *[grads[n] for n in WEIGHT_ORDER], *[delta[n] for n in WEIGHT_ORDER],
            *[new_m[n] for n in WEIGHT_ORDER], *[new_v[n] for n in WEIGHT_ORDER])
```

```python
import functools
import math

import jax
import jax.numpy as jnp
from jax import lax
from jax.experimental import pallas as pl
from jax.experimental.pallas import tpu as pltpu

F32, BF16 = jnp.float32, jnp.bfloat16
EPS = 1e-6
DEPTH = 4
N_DEV = 8
N_CHIP = 4
W_GRP = 256
CHUNK = 128
N_SEG = 8
LANES = 128
FFN_TF = 256
FFN_TF_WIDE = 1408
FFN_TM_WIDE = 512
VMEM_LIMIT = 56 * 1024 * 1024
ADAM_LR, ADAM_B1, ADAM_B2, ADAM_EPS, ADAM_WD, ADAM_STEP = 0.001, 0.9, 0.999, 1e-08, 0.01, 10
MESH_ID = pl.DeviceIdType.MESH
HI = lax.Precision.HIGHEST
ANY = pl.BlockSpec(memory_space=pl.ANY)


def _cp(**kw):
    return pltpu.CompilerParams(vmem_limit_bytes=VMEM_LIMIT, **kw)


def _dot(a, b):
    return jnp.dot(a.astype(BF16), b.astype(BF16), preferred_element_type=F32)


def _dot_nt(a, b):
    return lax.dot_general(a.astype(BF16), b.astype(BF16), (((1,), (1,)), ((), ())), preferred_element_type=F32)


def _dot_tn(a, b):
    return lax.dot_general(a.astype(BF16), b.astype(BF16), (((0,), (0,)), ((), ())), preferred_element_type=F32)


def _dot_hi(a, b):
    return jnp.dot(a, b, preferred_element_type=F32, precision=HI)


def _gelu(x):
    k = 0.7978845608028654
    t = jnp.tanh(k * (x + 0.044715 * x * x * x))
    return 0.5 * x * (1.0 + t), t


def _gelu_grad(x, t):
    k = 0.7978845608028654
    return 0.5 * (1.0 + t) + 0.5 * x * (1.0 - t * t) * k * (1.0 + 3.0 * 0.044715 * x * x)


def _iota(shape, axis):
    return lax.broadcasted_iota(jnp.int32, shape, axis)


def _full(shape):
    nd = len(shape)
    return pl.BlockSpec(shape, lambda *_: (0,) * nd)


def _norm_mod(xv, g, shift, scale):
    r = lax.rsqrt(jnp.mean(xv * xv, axis=-1, keepdims=True) + EPS)
    return (xv * r * g) * (1.0 + scale) + shift


def _norm_mod_bwd(xv, g, scale, dh):
    r = lax.rsqrt(jnp.mean(xv * xv, axis=-1, keepdims=True) + EPS)
    xh = xv * r
    n = xh * g
    dsh = jnp.sum(dh, axis=0, keepdims=True)
    dsc = jnp.sum(dh * n, axis=0, keepdims=True)
    dn = dh * (1.0 + scale)
    dg = jnp.sum(dn * xh, axis=0, keepdims=True)
    dxh = dn * g
    dx = r * (dxh - xh * jnp.mean(dxh * xh, axis=-1, keepdims=True))
    return dx, dsh, dsc, dg


def _tm(s):
    return min(s, 1024)


def _ffn_fwd(x, mod, g, wi, wo):
    s, d = x.shape
    f = wo.shape[0]
    tf, tm = FFN_TF_WIDE, min(s, FFN_TM_WIDE)
    nf, nt = f // tf, s // tm

    def body(x_ref, mod_ref, g_ref, wa_ref, wb_ref, wo_ref, xn_ref, h_ref, a_ref, b_ref, o_ref):
        j = pl.program_id(1)

        @pl.when(j == 0)
        def _():
            hh = _norm_mod(x_ref[...], g_ref[...], mod_ref[0:1, :], mod_ref[1:2, :])
            h_ref[...] = hh.astype(BF16)
            o_ref[...] = jnp.zeros_like(o_ref)

        h = h_ref[...]
        a = jnp.dot(h, wa_ref[...], preferred_element_type=F32)
        b = jnp.dot(h, wb_ref[...], preferred_element_type=F32)
        a_ref[...] = a.astype(BF16)
        b_ref[...] = b.astype(BF16)
        u = (a * jax.nn.sigmoid(a)) * b
        o_ref[...] += jnp.dot(u.astype(BF16), wo_ref[...], preferred_element_type=F32)

        @pl.when(j == nf - 1)
        def _():
            xn_ref[...] = x_ref[...] + 0.5 * mod_ref[2:3, :] * o_ref[...]

    row = pl.BlockSpec((tm, d), lambda i, j: (i, 0))
    chunk = pl.BlockSpec((tm, tf), lambda i, j: (i, j))
    return pl.pallas_call(
        body, name="ffn_fwd", grid=(nt, nf),
        in_specs=[row, _full((3, d)), _full((1, d)),
                  pl.BlockSpec((None, d, tf), lambda i, j: (0, 0, j)),
                  pl.BlockSpec((None, d, tf), lambda i, j: (1, 0, j)),
                  pl.BlockSpec((tf, d), lambda i, j: (j, 0))],
        out_specs=[row, row, chunk, chunk, row],
        out_shape=[jax.ShapeDtypeStruct((s, d), F32), jax.ShapeDtypeStruct((s, d), BF16),
                   jax.ShapeDtypeStruct((s, f), BF16), jax.ShapeDtypeStruct((s, f), BF16),
                   jax.ShapeDtypeStruct((s, d), F32)],
        compiler_params=_cp(dimension_semantics=("parallel", "arbitrary")),
    )(x, mod, g, wi, wi, wo)


def _gate_bwd(dxo, o, gate, half):
    s, d = dxo.shape
    tm = _tm(s)
    nt = s // tm

    def body(dx_ref, o_ref, gate_ref, do_ref, dg_ref):
        i = pl.program_id(0)

        @pl.when(i == 0)
        def _():
            dg_ref[...] = jnp.zeros_like(dg_ref)

        dx = dx_ref[...]
        do_ref[...] = (half * gate_ref[...] * dx).astype(BF16)
        dg_ref[...] += half * jnp.sum(o_ref[...] * dx, axis=0, keepdims=True)

    row = pl.BlockSpec((tm, d), lambda i: (i, 0))
    return pl.pallas_call(
        body, name="gate_bwd", grid=(nt,),
        in_specs=[row, row, _full((1, d))], out_specs=[row, _full((1, d))],
        out_shape=[jax.ShapeDtypeStruct((s, d), BF16), jax.ShapeDtypeStruct((1, d), F32)],
        compiler_params=_cp(dimension_semantics=("arbitrary",)),
    )(dxo, o, gate)


def _ffn_bwd_main(do, h, a, b, wo):
    s, d = do.shape
    f = wo.shape[0]
    tf = FFN_TF
    nf = f // tf

    def body(do_ref, h_ref, a_ref, b_ref, wo_ref, dza_ref, dzb_ref, dwi_ref, dwo_ref):
        dov = do_ref[...]
        hv = h_ref[...]
        du = lax.dot_general(dov, wo_ref[...], (((1,), (1,)), ((), ())), preferred_element_type=F32)
        av = a_ref[...].astype(F32)
        bv = b_ref[...].astype(F32)
        sa = jax.nn.sigmoid(av)
        si = av * sa
        u = (si * bv).astype(BF16)
        da = (du * bv * (sa * (1.0 + av * (1.0 - sa)))).astype(BF16)
        db = (du * si).astype(BF16)
        dza_ref[...] = da
        dzb_ref[...] = db
        dwo_ref[...] = _dot_tn(u, dov).astype(BF16)
        dwi_ref[0] = _dot_tn(hv, da).astype(BF16)
        dwi_ref[1] = _dot_tn(hv, db).astype(BF16)

    chunk = pl.BlockSpec((s, tf), lambda j: (0, j))
    return pl.pallas_call(
        body, name="ffn_bwd_main", grid=(nf,),
        in_specs=[_full((s, d)), _full((s, d)), chunk, chunk, pl.BlockSpec((tf, d), lambda j: (j, 0))],
        out_specs=[chunk, chunk, pl.BlockSpec((2, d, tf), lambda j: (0, 0, j)),
                   pl.BlockSpec((tf, d), lambda j: (j, 0))],
        out_shape=[jax.ShapeDtypeStruct((s, f), BF16), jax.ShapeDtypeStruct((s, f), BF16),
                   jax.ShapeDtypeStruct((2, d, f), BF16), jax.ShapeDtypeStruct((f, d), BF16)],
        compiler_params=_cp(dimension_semantics=("parallel",)),
    )(do, h, a, b, wo)


def _ffn_bwd_in(dza, dzb, wi, x, dxo, mod, g):
    s, d = x.shape
    f = dza.shape[1]
    tf, tm = FFN_TF_WIDE, min(s, FFN_TM_WIDE)
    nf, nt = f // tf, s // tm

    def body(dza_ref, dzb_ref, wa_ref, wb_ref, x_ref, dxo_ref, mod_ref, g_ref, dx_ref, rows_ref, acc):
        i, j = pl.program_id(0), pl.program_id(1)

        @pl.when(jnp.logical_and(i == 0, j == 0))
        def _():
            rows_ref[...] = jnp.zeros_like(rows_ref)

        @pl.when(j == 0)
        def _():
            acc[...] = jnp.zeros_like(acc)

        acc[...] += (lax.dot_general(dza_ref[...], wa_ref[...], (((1,), (1,)), ((), ())), preferred_element_type=F32)
                     + lax.dot_general(dzb_ref[...], wb_ref[...], (((1,), (1,)), ((), ())), preferred_element_type=F32))

        @pl.when(j == nf - 1)
        def _():
            dx, dsh, dsc, dg = _norm_mod_bwd(x_ref[...], g_ref[...], mod_ref[1:2, :], acc[...])
            dx_ref[...] = dx + dxo_ref[...]
            rows_ref[0:1, :] += dsh
            rows_ref[1:2, :] += dsc
            rows_ref[2:3, :] += dg

    row = pl.BlockSpec((tm, d), lambda i, j: (i, 0))
    chunk = pl.BlockSpec((tm, tf), lambda i, j: (i, j))
    return pl.pallas_call(
        body, name="ffn_bwd_in", grid=(nt, nf),
        in_specs=[chunk, chunk,
                  pl.BlockSpec((None, d, tf), lambda i, j: (0, 0, j)),
                  pl.BlockSpec((None, d, tf), lambda i, j: (1, 0, j)),
                  row, row, _full((3, d)), _full((1, d))],
        out_specs=[row, _full((8, d))],
        out_shape=[jax.ShapeDtypeStruct((s, d), F32), jax.ShapeDtypeStruct((8, d), F32)],
        scratch_shapes=[pltpu.VMEM((tm, d), F32)],
        compiler_params=_cp(dimension_semantics=("arbitrary", "arbitrary")),
    )(dza, dzb, wi, wi, x, dxo, mod, g)


def _mix_in_fwd(x, mod, g, wmit):
    s, d = x.shape
    p = wmit.shape[0]
    tm = _tm(s)

    def body(x_ref, mod_ref, g_ref, w_ref, z_ref, h_ref):
        hh = _norm_mod(x_ref[...], g_ref[...], mod_ref[0:1, :], mod_ref[1:2, :]).astype(BF16)
        h_ref[...] = hh
        z_ref[...] = lax.dot_general(hh, w_ref[...], (((1,), (1,)), ((), ())), preferred_element_type=F32)

    row = pl.BlockSpec((tm, d), lambda i: (i, 0))
    return pl.pallas_call(
        body, name="mix_in_fwd", grid=(s // tm,),
        in_specs=[row, _full((3, d)), _full((1, d)), _full((p, d))],
        out_specs=[pl.BlockSpec((tm, p), lambda i: (i, 0)), row],
        out_shape=[jax.ShapeDtypeStruct((s, p), F32), jax.ShapeDtypeStruct((s, d), BF16)],
        compiler_params=_cp(dimension_semantics=("parallel",)),
    )(x, mod, g, wmit)


def _mix_in_bwd(dzs, h, wmit, x, dxo, mod, g):
    s, d = x.shape
    p = wmit.shape[0]
    tm = min(s, 512)
    nt = s // tm

    def body(za_ref, zb_ref, zc_ref, zd_ref, h_ref, w_ref, x_ref, dxo_ref, mod_ref, g_ref,
             dx_ref, rows_ref, dw_ref, acc):
        i = pl.program_id(0)

        @pl.when(i == 0)
        def _():
            rows_ref[...] = jnp.zeros_like(rows_ref)
            acc[...] = jnp.zeros_like(acc)

        dz = jnp.concatenate([za_ref[...], zb_ref[...], zc_ref[...], zd_ref[...]], axis=1).astype(BF16)
        acc[...] += _dot_tn(dz, h_ref[...])
        dh = jnp.dot(dz, w_ref[...], preferred_element_type=F32)
        dx, dsh, dsc, dg = _norm_mod_bwd(x_ref[...], g_ref[...], mod_ref[1:2, :], dh)
        dx_ref[...] = dx + dxo_ref[...]
        rows_ref[0:1, :] += dsh
        rows_ref[1:2, :] += dsc
        rows_ref[2:3, :] += dg

        @pl.when(i == nt - 1)
        def _():
            dw_ref[...] = acc[...].astype(BF16)

    row = pl.BlockSpec((tm, d), lambda i: (i, 0))
    zspecs = [pl.BlockSpec((tm, z.shape[1]), lambda i: (i, 0)) for z in dzs]
    return pl.pallas_call(
        body, name="mix_in_bwd", grid=(nt,),
        in_specs=zspecs + [row, _full((p, d)), row, row, _full((3, d)), _full((1, d))],
        out_specs=[row, _full((8, d)), _full((p, d))],
        out_shape=[jax.ShapeDtypeStruct((s, d), F32), jax.ShapeDtypeStruct((8, d), F32),
                   jax.ShapeDtypeStruct((p, d), BF16)],
        scratch_shapes=[pltpu.VMEM((p, d), F32)],
        compiler_params=_cp(dimension_semantics=("arbitrary",)),
    )(*dzs, h, wmit, x, dxo, mod, g)


def _group_norm(ys, mng):
    outs, hats, rs = [], [], []
    for k, y in enumerate(ys):
        r = lax.rsqrt(jnp.mean(y * y, axis=-1, keepdims=True) + EPS)
        yh = y * r
        hats.append(yh)
        rs.append(r)
        outs.append(yh * mng[:, k * W_GRP:(k + 1) * W_GRP])
    return jnp.concatenate(outs, axis=1), hats, rs


def _mix_out_fwd(ys, mng, wmo, x, gate):
    s, d = x.shape
    tm = _tm(s)

    def body(ya, yb, yc, yd, mng_ref, w_ref, x_ref, gate_ref, xn_ref, m_ref):
        yn, _, _ = _group_norm([ya[...], yb[...], yc[...], yd[...]], mng_ref[...])
        m = jnp.dot(yn.astype(BF16), w_ref[...], preferred_element_type=F32)
        m_ref[...] = m
        xn_ref[...] = x_ref[...] + gate_ref[...] * m

    row = pl.BlockSpec((tm, d), lambda i: (i, 0))
    grp = pl.BlockSpec((tm, W_GRP), lambda i: (i, 0))
    return pl.pallas_call(
        body, name="mix_out_fwd", grid=(s // tm,),
        in_specs=[grp, grp, grp, grp, _full((1, d)), _full((d, d)), row, _full((1, d))],
        out_specs=[row, row],
        out_shape=[jax.ShapeDtypeStruct((s, d), F32), jax.ShapeDtypeStruct((s, d), F32)],
        compiler_params=_cp(dimension_semantics=("parallel",)),
    )(*ys, mng, wmo, x, gate)


def _mix_out_bwd(dxo, m, gate, ys, mng, wmo):
    s, d = dxo.shape
    tm = min(s, 512)
    nt = s // tm

    def body(dxo_ref, m_ref, gate_ref, ya, yb, yc, yd, mng_ref, w_ref,
             dya, dyb, dyc, dyd, dgate_ref, dmng_ref, dw_ref, acc):
        i = pl.program_id(0)

        @pl.when(i == 0)
        def _():
            dgate_ref[...] = jnp.zeros_like(dgate_ref)
            dmng_ref[...] = jnp.zeros_like(dmng_ref)
            acc[...] = jnp.zeros_like(acc)

        dxv = dxo_ref[...]
        dgate_ref[...] += jnp.sum(m_ref[...] * dxv, axis=0, keepdims=True)
        dm = (gate_ref[...] * dxv).astype(BF16)
        mng = mng_ref[...]
        yn, hats, rs = _group_norm([ya[...], yb[...], yc[...], yd[...]], mng)
        acc[...] += _dot_tn(yn, dm)
        dyn = lax.dot_general(dm, w_ref[...], (((1,), (1,)), ((), ())), preferred_element_type=F32)
        dmng_parts = []
        for k, (yh, r, out) in enumerate(zip(hats, rs, (dya, dyb, dyc, dyd))):
            dk = dyn[:, k * W_GRP:(k + 1) * W_GRP]
            dmng_parts.append(jnp.sum(dk * yh, axis=0, keepdims=True))
            dyh = dk * mng[:, k * W_GRP:(k + 1) * W_GRP]
            out[...] = r * (dyh - yh * jnp.mean(dyh * yh, axis=-1, keepdims=True))
        dmng_ref[...] += jnp.concatenate(dmng_parts, axis=1)

        @pl.when(i == nt - 1)
        def _():
            dw_ref[...] = acc[...].astype(BF16)

    row = pl.BlockSpec((tm, d), lambda i: (i, 0))
    grp = pl.BlockSpec((tm, W_GRP), lambda i: (i, 0))
    return pl.pallas_call(
        body, name="mix_out_bwd", grid=(nt,),
        in_specs=[row, row, _full((1, d)), grp, grp, grp, grp, _full((1, d)), _full((d, d))],
        out_specs=[grp, grp, grp, grp, _full((1, d)), _full((1, d)), _full((d, d))],
        out_shape=[jax.ShapeDtypeStruct((s, W_GRP), F32)] * 4
        + [jax.ShapeDtypeStruct((1, d), F32), jax.ShapeDtypeStruct((1, d), F32), jax.ShapeDtypeStruct((d, d), BF16)],
        scratch_shapes=[pltpu.VMEM((d, d), F32)],
        compiler_params=_cp(dimension_semantics=("arbitrary",)),
    )(dxo, m, gate, *ys, mng, wmo)


def _sgu_consts():
    r = _iota((W_GRP, W_GRP), 0) >> 6
    c = _iota((W_GRP, W_GRP), 1) >> 6
    avg = jnp.where(r == c, 1.0 / 64.0, 0.0).astype(F32)
    tril = _iota((CHUNK, CHUNK), 0) >= _iota((CHUNK, CHUNK), 1)
    head = _iota((CHUNK, W_GRP), 1) >> 6
    return avg, tril, head


def _sgu_pre(za, avg):
    zg, t = _gelu(za)
    u, v = zg[:, :W_GRP], zg[:, W_GRP:]
    mu = _dot_hi(v, avg)
    vc = v - mu
    r = lax.rsqrt(_dot_hi(vc * vc, avg) + EPS)
    return t, u, vc * r, r


def _sgu_fwd(z, sgu_w, bias_full):
    s = z.shape[0]
    tm = min(s, 512)

    def body(za_ref, w_ref, bias_ref, ya_ref):
        avg, tril, head = _sgu_consts()
        _, u, vn, _ = _sgu_pre(za_ref[...], avg)
        wm = [jnp.where(tril, w_ref[h], 0.0).astype(BF16) for h in range(4)]
        vb = vn.astype(BF16)
        for n in range(tm // CHUNK):
            rows = slice(n * CHUNK, (n + 1) * CHUNK)
            mixed = bias_ref[...]
            for h in range(4):
                mixed = mixed + jnp.where(head == h, jnp.dot(wm[h], vb[rows], preferred_element_type=F32), 0.0)
            ya_ref[rows, :] = u[rows] * mixed

    return pl.pallas_call(
        body, name="sgu_fwd", grid=(s // tm,),
        in_specs=[pl.BlockSpec((tm, 2 * W_GRP), lambda i: (i, 0)), _full((4, CHUNK, CHUNK)), _full((CHUNK, W_GRP))],
        out_specs=pl.BlockSpec((tm, W_GRP), lambda i: (i, 0)),
        out_shape=jax.ShapeDtypeStruct((s, W_GRP), F32),
        compiler_params=_cp(dimension_semantics=("parallel",)),
    )(z, sgu_w, bias_full)


def _sgu_bwd(z, dya, sgu_w, bias_full):
    s = z.shape[0]
    tm = min(s, 512)
    nt = s // tm

    def body(za_ref, dya_ref, w_ref, bias_ref, dza_ref, dw_ref, db_ref, du_s, dvn_s):
        i = pl.program_id(0)

        @pl.when(i == 0)
        def _():
            dw_ref[...] = jnp.zeros_like(dw_ref)
            db_ref[...] = jnp.zeros_like(db_ref)

        avg, tril, head = _sgu_consts()
        za = za_ref[...]
        t, u, vn, r = _sgu_pre(za, avg)
        wm = [jnp.where(tril, w_ref[h], 0.0).astype(BF16) for h in range(4)]
        vb = vn.astype(BF16)
        dya = dya_ref[...]
        dw = [jnp.zeros((CHUNK, CHUNK), F32) for _ in range(4)]
        db = jnp.zeros((CHUNK, W_GRP), F32)
        for n in range(tm // CHUNK):
            rows = slice(n * CHUNK, (n + 1) * CHUNK)
            mixed = bias_ref[...]
            for h in range(4):
                mixed = mixed + jnp.where(head == h, jnp.dot(wm[h], vb[rows], preferred_element_type=F32), 0.0)
            dmix = dya[rows] * u[rows]
            du_s[rows, :] = dya[rows] * mixed
            db = db + dmix
            dmb = dmix.astype(BF16)
            dvn = jnp.zeros((CHUNK, W_GRP), F32)
            for h in range(4):
                dmh = jnp.where(head == h, dmix, 0.0)
                dw[h] = dw[h] + _dot_nt(dmh, vb[rows])
                dvn = dvn + jnp.where(head == h, _dot_tn(wm[h], dmb), 0.0)
            dvn_s[rows, :] = dvn
        for h in range(4):
            dw_ref[h] += jnp.where(tril, dw[h], 0.0)
        sel = ((_iota((W_GRP, CHUNK), 0) >> 6) == _iota((W_GRP, CHUNK), 1)).astype(F32)
        db_ref[...] += _dot_hi(db, sel)
        dvn = dvn_s[...]
        dv = r * (dvn - _dot_hi(dvn, avg) - vn * _dot_hi(dvn * vn, avg))
        dzg = jnp.concatenate([du_s[...], dv], axis=1)
        dza_ref[...] = dzg * _gelu_grad(za, t)

    return pl.pallas_call(
        body, name="sgu_bwd", grid=(nt,),
        in_specs=[pl.BlockSpec((tm, 2 * W_GRP), lambda i: (i, 0)), pl.BlockSpec((tm, W_GRP), lambda i: (i, 0)),
                  _full((4, CHUNK, CHUNK)), _full((CHUNK, W_GRP))],
        out_specs=[pl.BlockSpec((tm, 2 * W_GRP), lambda i: (i, 0)), _full((4, CHUNK, CHUNK)), _full((CHUNK, CHUNK))],
        out_shape=[jax.ShapeDtypeStruct((s, 2 * W_GRP), F32), jax.ShapeDtypeStruct((4, CHUNK, CHUNK), F32),
                   jax.ShapeDtypeStruct((CHUNK, CHUNK), F32)],
        scratch_shapes=[pltpu.VMEM((tm, W_GRP), F32), pltpu.VMEM((tm, W_GRP), F32)],
        compiler_params=_cp(dimension_semantics=("arbitrary",)),
    )(z, dya, sgu_w, bias_full)


def _shift_down(x, k):
    return jnp.where(_iota(x.shape, 0) < k, 0.0, pltpu.roll(x, k, 0))


def _shift_up(x, k):
    n = x.shape[0]
    return jnp.where(_iota(x.shape, 0) >= n - k, 0.0, pltpu.roll(x, n - k, 0))


def _by_pool_group(shape, v2, v4, v8, v16):
    col = _iota(shape, 1)
    return jnp.where(col < 64, v2, jnp.where(col < 128, v4, jnp.where(col < 192, v8, v16)))


def _pool_core(zb, pw2):
    s2 = zb + _shift_down(zb, 1)
    s4 = s2 + _shift_down(s2, 2)
    s8 = s4 + _shift_down(s4, 4)
    s16 = s8 + _shift_down(s8, 8)
    win = _by_pool_group(zb.shape, s2, s4, s8, s16)
    wlen = _by_pool_group(zb.shape, 2.0, 4.0, 8.0, 16.0)
    cnt = jnp.minimum((_iota(zb.shape, 0) + 1).astype(F32), wlen)
    p = win / cnt - zb
    wt = jnp.tile(pw2, (1, 4))
    wbd = jnp.where((_iota(wt.shape, 0) >> 6) == (_iota(wt.shape, 1) >> 6), wt, 0.0).astype(BF16)
    return p, cnt, wbd


def _conv_core(zc, cw):
    bg, cg, xh = zc[:, :W_GRP], zc[:, W_GRP:2 * W_GRP], zc[:, 2 * W_GRP:]
    y = cg * xh
    y1, y2 = _shift_down(y, 1), _shift_down(y, 2)
    out = cw[2:3, :] * y + cw[1:2, :] * y1 + cw[0:1, :] * y2
    return bg, cg, xh, y, y1, y2, out


def _poolconv_fwd(z, pw2, pscale, cw):
    s = z.shape[0]

    def body(zb_ref, zc_ref, pw_ref, ps_ref, cw_ref, yb_ref, yc_ref):
        p, _, wbd = _pool_core(zb_ref[...], pw_ref[...])
        yb_ref[...] = jnp.dot(p.astype(BF16), wbd, preferred_element_type=F32) * ps_ref[...]
        bg, _, _, _, _, _, out = _conv_core(zc_ref[...], cw_ref[...])
        yc_ref[...] = bg * out

    return pl.pallas_call(
        body, name="poolconv_fwd", grid=(1,),
        in_specs=[pl.BlockSpec((s, W_GRP), lambda i: (0, 2)), pl.BlockSpec((s, 3 * W_GRP), lambda i: (0, 1)),
                  _full((W_GRP, 64)), _full((1, W_GRP)), _full((3, W_GRP))],
        out_specs=[_full((s, W_GRP)), _full((s, W_GRP))],
        out_shape=[jax.ShapeDtypeStruct((s, W_GRP), F32)] * 2,
        compiler_params=_cp(dimension_semantics=("arbitrary",)),
    )(z, z, pw2, pscale, cw)


def _poolconv_bwd(z, dyb, dyc, pw2, pscale, cw):
    s = z.shape[0]

    def body(zb_ref, zc_ref, dyb_ref, dyc_ref, pw_ref, ps_ref, cw_ref, dzb_ref, dzc_ref, dw_ref, dps_ref, dcw_ref):
        zb = zb_ref[...]
        p, cnt, wbd = _pool_core(zb, pw_ref[...])
        pb = p.astype(BF16)
        out = jnp.dot(pb, wbd, preferred_element_type=F32)
        dyb = dyb_ref[...]
        dps_ref[...] = jnp.sum(dyb * out, axis=0, keepdims=True)
        dout = (dyb * ps_ref[...]).astype(BF16)
        dw = _dot_tn(pb, dout)
        dw_ref[...] = jnp.where((_iota(dw.shape, 0) >> 6) == (_iota(dw.shape, 1) >> 6), dw, 0.0)
        dp = lax.dot_general(dout, wbd, (((1,), (1,)), ((), ())), preferred_element_type=F32)
        dwin = dp / cnt
        t2 = dwin + _shift_up(dwin, 1)
        t4 = t2 + _shift_up(t2, 2)
        t8 = t4 + _shift_up(t4, 4)
        t16 = t8 + _shift_up(t8, 8)
        dzb_ref[...] = _by_pool_group(zb.shape, t2, t4, t8, t16) - dp

        cw = cw_ref[...]
        bg, cg, xh, y, y1, y2, out = _conv_core(zc_ref[...], cw)
        dyc = dyc_ref[...]
        dout = dyc * bg
        dcw_ref[...] = jnp.zeros_like(dcw_ref)
        dcw_ref[0:1, :] = jnp.sum(dout * y2, axis=0, keepdims=True)
        dcw_ref[1:2, :] = jnp.sum(dout * y1, axis=0, keepdims=True)
        dcw_ref[2:3, :] = jnp.sum(dout * y, axis=0, keepdims=True)
        dy = cw[2:3, :] * dout + cw[1:2, :] * _shift_up(dout, 1) + cw[0:1, :] * _shift_up(dout, 2)
        dzc_ref[...] = jnp.concatenate([dyc * out, dy * xh, dy * cg], axis=1)

    return pl.pallas_call(
        body, name="poolconv_bwd", grid=(1,),
        in_specs=[pl.BlockSpec((s, W_GRP), lambda i: (0, 2)), pl.BlockSpec((s, 3 * W_GRP), lambda i: (0, 1)),
                  _full((s, W_GRP)), _full((s, W_GRP)), _full((W_GRP, 64)), _full((1, W_GRP)), _full((3, W_GRP))],
        out_specs=[_full((s, W_GRP)), _full((s, 3 * W_GRP)), _full((W_GRP, W_GRP)), _full((1, W_GRP)), _full((8, W_GRP))],
        out_shape=[jax.ShapeDtypeStruct((s, W_GRP), F32), jax.ShapeDtypeStruct((s, 3 * W_GRP), F32),
                   jax.ShapeDtypeStruct((W_GRP, W_GRP), F32), jax.ShapeDtypeStruct((1, W_GRP), F32),
                   jax.ShapeDtypeStruct((8, W_GRP), F32)],
        compiler_params=_cp(dimension_semantics=("arbitrary",)),
    )(z, z, dyb, dyc, pw2, pscale, cw)


N_STATE = 1024
HALF_STATE = N_STATE // 2
HALF_CH = W_GRP // 2
N_SLAB = HALF_STATE // LANES


def _s5_disc(lre, lim, ldt):
    dt = jnp.exp(ldt)
    mag = jnp.exp(lre * dt)
    ang = lim * dt
    ar, ai = mag * jnp.cos(ang), mag * jnp.sin(ang)
    nr, ni = ar - 1.0, ai
    den = lre * lre + lim * lim
    kr = (nr * lre + ni * lim) / den
    ki = (ni * lre - nr * lim) / den
    return ar, ai, kr, ki


def _s5_mats(colp, br, bi, cr, ci):
    _, _, kr, ki = _s5_disc(colp[:, 0:1], colp[:, 1:2], colp[:, 2:3])
    bbr = kr * br - ki * bi
    bbi = kr * bi + ki * br
    bmask = (_iota((HALF_STATE, HALF_CH), 0) >> 6) == (_iota((HALF_STATE, HALF_CH), 1) >> 4)
    cmask = (_iota((HALF_CH, HALF_STATE), 0) >> 4) == (_iota((HALF_CH, HALF_STATE), 1) >> 6)
    btr = jnp.where(bmask, jnp.tile(bbr, (1, 8)), 0.0).astype(BF16)
    bti = jnp.where(bmask, jnp.tile(bbi, (1, 8)), 0.0).astype(BF16)
    ctr = jnp.where(cmask, jnp.tile(cr, (1, 8)), 0.0).astype(BF16)
    cti = jnp.where(cmask, jnp.tile(ci, (1, 8)), 0.0).astype(BF16)
    return kr, ki, btr, bti, ctr, cti, bmask, cmask


def _slab(q):
    return slice(q * LANES, (q + 1) * LANES)


def _cmul(ar, ai, br, bi):
    return ar * br - ai * bi, ar * bi + ai * br


def _sub_shift(x, k, up):
    row = _iota(x.shape, 0)
    if up:
        return jnp.where(row >= N_SEG - k, 0.0, pltpu.roll(x, N_SEG - k, 0))
    return jnp.where(row < k, 0.0, pltpu.roll(x, k, 0))


def _seg_rows(j):
    return pl.ds(pl.multiple_of(j * N_SEG, N_SEG), N_SEG)


def _interleave(src, dst, seg):
    def step(j, carry):
        dst[_seg_rows(j), :] = src[pl.ds(j, N_SEG, stride=seg), :]
        return carry
    lax.fori_loop(0, seg, step, 0)


def _deinterleave(src, dst, seg):
    def step(j, carry):
        dst[pl.ds(j, N_SEG, stride=seg), :] = src[_seg_rows(j), :]
        return carry
    lax.fori_loop(0, seg, step, 0)


def _scan(xr, xi, ar_row, ai_row, seg, reverse):
    nlog = int(math.log2(seg))
    assert (1 << nlog) == seg
    for q0 in range(0, N_SLAB, 4):
        qs = list(range(q0, q0 + 4))
        aq = [(jnp.broadcast_to(ar_row[:, _slab(q)], (N_SEG, LANES)),
               jnp.broadcast_to(ai_row[:, _slab(q)], (N_SEG, LANES))) for q in qs]
        zero = jnp.zeros((N_SEG, LANES), F32)

        def local(jj, carry, qs=qs, aq=aq):
            j = seg - 1 - jj if reverse else jj
            out = []
            for n, q in enumerate(qs):
                rows = _seg_rows(j)
                pr, pi = _cmul(aq[n][0], aq[n][1], carry[2 * n], carry[2 * n + 1])
                nr = pr + xr[q, rows, :]
                ni = pi + xi[q, rows, :]
                xr[q, rows, :] = nr
                xi[q, rows, :] = ni
                out += [nr, ni]
            return tuple(out)

        fin = lax.fori_loop(0, seg, local, (zero,) * 8)
        cins = []
        for n in range(4):
            er, ei = fin[2 * n], fin[2 * n + 1]
            pr, pi = aq[n]
            for _ in range(nlog):
                pr, pi = _cmul(pr, pi, pr, pi)
            yr, yi = er, ei
            for k in (1, 2, 4):
                sr, si = _cmul(pr, pi, _sub_shift(yr, k, reverse), _sub_shift(yi, k, reverse))
                yr, yi = yr + sr, yi + si
                pr, pi = _cmul(pr, pi, pr, pi)
            cins.append((_sub_shift(yr, 1, reverse), _sub_shift(yi, 1, reverse)))

        def fix(jj, carry, qs=qs, aq=aq, cins=cins):
            j = seg - 1 - jj if reverse else jj
            out = []
            for n, q in enumerate(qs):
                rows = _seg_rows(j)
                pwr, pwi = carry[2 * n], carry[2 * n + 1]
                cr, ci = _cmul(pwr, pwi, cins[n][0], cins[n][1])
                xr[q, rows, :] += cr
                xi[q, rows, :] += ci
                nr, ni = _cmul(pwr, pwi, aq[n][0], aq[n][1])
                out += [nr, ni]
            return tuple(out)

        lax.fori_loop(0, seg, fix, tuple(v for pair in aq for v in pair))


def _s5_forward_states(u, btr, bti, ar_row, ai_row, xr, xi, seg):
    ub = u.astype(BF16)
    for q in range(N_SLAB):
        xr[q] = _dot_nt(ub, btr[_slab(q), :])
        xi[q] = _dot_nt(ub, bti[_slab(q), :])
    _scan(xr, xi, ar_row, ai_row, seg, False)


def _s5_readout(u, xr, xi, ctr, cti, d):
    y = d * u
    for q in range(N_SLAB):
        y = y + _dot_nt(xr[q], ctr[:, _slab(q)]) - _dot_nt(xi[q], cti[:, _slab(q)])
    return y


def _s5_param_specs():
    return [pl.BlockSpec((3, HALF_STATE), lambda i: (0, i)), pl.BlockSpec((HALF_STATE, 3), lambda i: (i, 0)),
            pl.BlockSpec((HALF_STATE, 16), lambda i: (i, 0)), pl.BlockSpec((HALF_STATE, 16), lambda i: (i, 0)),
            pl.BlockSpec((HALF_CH, 64), lambda i: (i, 0)), pl.BlockSpec((HALF_CH, 64), lambda i: (i, 0)),
            pl.BlockSpec((1, HALF_CH), lambda i: (0, i))]


def _s5_core_fwd(z, sp):
    s = z.shape[0]
    seg = s // N_SEG

    def body(u_ref, rowp, colp, br, bi, cr, ci, d_ref, y_ref, xr, xi, us, ys):
        ar, ai, _, _ = _s5_disc(rowp[0:1, :], rowp[1:2, :], rowp[2:3, :])
        _, _, btr, bti, ctr, cti, _, _ = _s5_mats(colp[...], br[...], bi[...], cr[...], ci[...])
        _interleave(u_ref, us, seg)
        u = us[...]
        _s5_forward_states(u, btr, bti, ar, ai, xr, xi, seg)
        ys[...] = _s5_readout(u, xr, xi, ctr, cti, d_ref[...])
        _deinterleave(ys, y_ref, seg)

    return pl.pallas_call(
        body, name="s5_core_fwd", grid=(2,),
        in_specs=[pl.BlockSpec((s, HALF_CH), lambda i: (0, 12 + i))] + _s5_param_specs(),
        out_specs=pl.BlockSpec((s, HALF_CH), lambda i: (0, i)),
        out_shape=jax.ShapeDtypeStruct((s, W_GRP), F32),
        scratch_shapes=[pltpu.VMEM((N_SLAB, s, LANES), F32)] * 2 + [pltpu.VMEM((s, HALF_CH), F32)] * 2,
        compiler_params=_cp(dimension_semantics=("parallel",)),
    )(z, *sp)


def _s5_glu_fwd(y, gw, gb):
    s = y.shape[0]
    tm = _tm(s)

    def body(y_ref, gw_ref, gb_ref, o_ref):
        yg, _ = _gelu(y_ref[...])
        o_ref[...] = yg * jax.nn.sigmoid(_dot(yg, gw_ref[...]) + gb_ref[...])

    blk = pl.BlockSpec((tm, W_GRP), lambda i: (i, 0))
    return pl.pallas_call(
        body, name="s5_glu_fwd", grid=(s // tm,),
        in_specs=[blk, _full((W_GRP, W_GRP)), _full((1, W_GRP))], out_specs=blk,
        out_shape=jax.ShapeDtypeStruct((s, W_GRP), F32),
        compiler_params=_cp(dimension_semantics=("parallel",)),
    )(y, gw, gb)


def _s5_glu_bwd(y, dyd, gw, gb):
    s = y.shape[0]
    tm = _tm(s)

    def body(y_ref, dyd_ref, gw_ref, gb_ref, dy_ref, dgw_ref, dgb_ref):
        i = pl.program_id(0)

        @pl.when(i == 0)
        def _():
            dgw_ref[...] = jnp.zeros_like(dgw_ref)
            dgb_ref[...] = jnp.zeros_like(dgb_ref)

        y, gw, dyd = y_ref[...], gw_ref[...], dyd_ref[...]
        yg, t = _gelu(y)
        gate = jax.nn.sigmoid(_dot(yg, gw) + gb_ref[...])
        dlin = dyd * yg * gate * (1.0 - gate)
        dgw_ref[...] += _dot_tn(yg, dlin)
        dgb_ref[...] += jnp.sum(dlin, axis=0, keepdims=True)
        dy_ref[...] = (dyd * gate + _dot_nt(dlin, gw)) * _gelu_grad(y, t)

    blk = pl.BlockSpec((tm, W_GRP), lambda i: (i, 0))
    return pl.pallas_call(
        body, name="s5_glu_bwd", grid=(s // tm,),
        in_specs=[blk, blk, _full((W_GRP, W_GRP)), _full((1, W_GRP))],
        out_specs=[blk, _full((W_GRP, W_GRP)), _full((1, W_GRP))],
        out_shape=[jax.ShapeDtypeStruct((s, W_GRP), F32), jax.ShapeDtypeStruct((W_GRP, W_GRP), F32),
                   jax.ShapeDtypeStruct((1, W_GRP), F32)],
        compiler_params=_cp(dimension_semantics=("arbitrary",)),
    )(y, dyd, gw, gb)


def _s5_core_bwd(z, dy, sp):
    s = z.shape[0]
    seg = s // N_SEG

    def body(u_ref, dy_ref, rowp, colp, br_ref, bi_ref, cr_ref, ci_ref, d_ref,
             du_ref, dbr_ref, dbi_ref, dcr_ref, dci_ref, dd_ref, da_ref, dk_ref,
             xr, xi, gr, gi, us, dys):
        ar, ai, _, _ = _s5_disc(rowp[0:1, :], rowp[1:2, :], rowp[2:3, :])
        br, bi = br_ref[...], bi_ref[...]
        kr, ki, btr, bti, ctr, cti, bmask, cmask = _s5_mats(colp[...], br, bi, cr_ref[...], ci_ref[...])
        _interleave(u_ref, us, seg)
        _interleave(dy_ref, dys, seg)
        u = us[...]
        d = d_ref[...]
        _s5_forward_states(u, btr, bti, ar, ai, xr, xi, seg)

        dy = dys[...]
        dd_ref[...] = jnp.sum(dy * u, axis=0, keepdims=True)
        du = d * dy
        dyb = dy.astype(BF16)
        dctr, dcti = [], []
        for q in range(N_SLAB):
            gr[q] = jnp.dot(dyb, ctr[:, _slab(q)], preferred_element_type=F32)
            gi[q] = -jnp.dot(dyb, cti[:, _slab(q)], preferred_element_type=F32)
            dctr.append(_dot_tn(dyb, xr[q]))
            dcti.append(-_dot_tn(dyb, xi[q]))
        selp = ((_iota((HALF_STATE, 64), 0) & 63) == _iota((HALF_STATE, 64), 1)).astype(F32)
        dcr_ref[...] = _dot_hi(jnp.where(cmask, jnp.concatenate(dctr, axis=1), 0.0), selp)
        dci_ref[...] = _dot_hi(jnp.where(cmask, jnp.concatenate(dcti, axis=1), 0.0), selp)

        _scan(gr, gi, ar, -ai, seg, True)

        dar, dai = [], []
        for q in range(N_SLAB):
            def acc_step(j, carry, q=q):
                rows, prev = _seg_rows(j), _seg_rows(j - 1)
                g_r, g_i, p_r, p_i = gr[q, rows, :], gi[q, rows, :], xr[q, prev, :], xi[q, prev, :]
                return carry[0] + g_r * p_r + g_i * p_i, carry[1] - g_r * p_i + g_i * p_r
            first, last = _seg_rows(0), _seg_rows(seg - 1)
            p_r, p_i = _sub_shift(xr[q, last, :], 1, False), _sub_shift(xi[q, last, :], 1, False)
            g_r, g_i = gr[q, first, :], gi[q, first, :]
            s_r, s_i = lax.fori_loop(1, seg, acc_step, (g_r * p_r + g_i * p_i, -g_r * p_i + g_i * p_r))
            dar.append(jnp.sum(s_r, axis=0, keepdims=True))
            dai.append(jnp.sum(s_i, axis=0, keepdims=True))
        da_ref[...] = jnp.zeros_like(da_ref)
        da_ref[0:1, :] = jnp.concatenate(dar, axis=1)
        da_ref[1:2, :] = jnp.concatenate(dai, axis=1)

        ub = u.astype(BF16)
        dbtr, dbti = [], []
        for q in range(N_SLAB):
            g_r, g_i = gr[q].astype(BF16), gi[q].astype(BF16)
            du = du + jnp.dot(g_r, btr[_slab(q), :], preferred_element_type=F32) \
                + jnp.dot(g_i, bti[_slab(q), :], preferred_element_type=F32)
            dbtr.append(_dot_tn(g_r, ub))
            dbti.append(_dot_tn(g_i, ub))
        us[...] = du
        _deinterleave(us, du_ref, seg)
        selc =((_iota((HALF_CH, 16), 0) & 15) == _iota((HALF_CH, 16), 1)).astype(F32)
        dbbr = _dot_hi(jnp.where(bmask, jnp.concatenate(dbtr, axis=0), 0.0), selc)
        dbbi = _dot_hi(jnp.where(bmask, jnp.concatenate(dbti, axis=0), 0.0), selc)
        dbr_ref[...] = kr * dbbr + ki * dbbi
        dbi_ref[...] = kr * dbbi - ki * dbbr
        dk_ref[:, 0:1] = jnp.sum(dbbr * br + dbbi * bi, axis=1, keepdims=True)
        dk_ref[:, 1:2] = jnp.sum(dbbi * br - dbbr * bi, axis=1, keepdims=True)

    half = pl.BlockSpec((s, HALF_CH), lambda i: (0, i))
    return pl.pallas_call(
        body, name="s5_core_bwd", grid=(2,),
        in_specs=[pl.BlockSpec((s, HALF_CH), lambda i: (0, 12 + i)), half] + _s5_param_specs(),
        out_specs=[half, pl.BlockSpec((HALF_STATE, 16), lambda i: (i, 0)), pl.BlockSpec((HALF_STATE, 16), lambda i: (i, 0)),
                   pl.BlockSpec((HALF_CH, 64), lambda i: (i, 0)), pl.BlockSpec((HALF_CH, 64), lambda i: (i, 0)),
                   pl.BlockSpec((1, HALF_CH), lambda i: (0, i)), pl.BlockSpec((8, HALF_STATE), lambda i: (0, i)),
                   pl.BlockSpec((HALF_STATE, 2), lambda i: (i, 0))],
        out_shape=[jax.ShapeDtypeStruct((s, W_GRP), F32), jax.ShapeDtypeStruct((N_STATE, 16), F32),
                   jax.ShapeDtypeStruct((N_STATE, 16), F32), jax.ShapeDtypeStruct((W_GRP, 64), F32),
                   jax.ShapeDtypeStruct((W_GRP, 64), F32), jax.ShapeDtypeStruct((1, W_GRP), F32),
                   jax.ShapeDtypeStruct((8, N_STATE), F32), jax.ShapeDtypeStruct((N_STATE, 2), F32)],
        scratch_shapes=[pltpu.VMEM((N_SLAB, s, LANES), F32)] * 4 + [pltpu.VMEM((s, HALF_CH), F32)] * 2,
        compiler_params=_cp(dimension_semantics=("parallel",)),
    )(z, dy, *sp)


def _s5_param_bwd(lre, lim, ldt, da_r, da_i, dk_r, dk_i):
    n = lre.shape[0]

    def body(lre_ref, lim_ref, ldt_ref, dar_ref, dai_ref, dkr_ref, dki_ref, o_re, o_im, o_dt):
        lre, lim, ldt = lre_ref[...], lim_ref[...], ldt_ref[...]
        dt = jnp.exp(ldt)
        ar, ai, kr, ki = _s5_disc(lre, lim, ldt)
        mag = jnp.exp(lre * dt)
        den = lre * lre + lim * lim
        dkr, dki = dkr_ref[...], dki_ref[...]
        nr, ni = ar - 1.0, ai
        d_ar = dar_ref[...] + (dkr * lre - dki * lim) / den
        d_ai = dai_ref[...] + (dkr * lim + dki * lre) / den
        kk = (kr * dkr + ki * dki) * 2.0 / den
        d_lre = (dkr * nr + dki * ni) / den - kk * lre
        d_lim = (dkr * ni - dki * nr) / den - kk * lim
        d_mag = (d_ar * ar + d_ai * ai) / mag
        d_ang = d_ai * ar - d_ar * ai
        o_re[...] = d_lre + d_mag * mag * dt
        o_im[...] = d_lim + d_ang * dt
        o_dt[...] = jnp.sum((d_mag * mag * lre + d_ang * lim) * dt, axis=1, keepdims=True)

    return pl.pallas_call(
        body, name="s5_param_bwd",
        out_shape=[jax.ShapeDtypeStruct((n, 64), F32), jax.ShapeDtypeStruct((n, 64), F32),
                   jax.ShapeDtypeStruct((n, 1), F32)],
    )(lre, lim, ldt, da_r, da_i, dk_r, dk_i)


def _loss_head(x, fg, target):
    s, d = x.shape
    tm = _tm(s)

    def body(x_ref, fg_ref, t_ref, loss_ref, dx_ref, dfg_ref):
        i = pl.program_id(0)

        @pl.when(i == 0)
        def _():
            loss_ref[...] = jnp.zeros_like(loss_ref)
            dfg_ref[...] = jnp.zeros_like(dfg_ref)

        xv, g = x_ref[...], fg_ref[...]
        r = lax.rsqrt(jnp.mean(xv * xv, axis=-1, keepdims=True) + EPS)
        xh = xv * r
        err = xh * g - t_ref[...]
        loss_ref[...] += 0.5 * jnp.sum(jnp.mean(err * err, axis=-1, keepdims=True), axis=0, keepdims=True)
        dy = err * (1.0 / d)
        dfg_ref[...] += jnp.sum(dy * xh, axis=0, keepdims=True)
        dxh = dy * g
        dx_ref[...] = r * (dxh - xh * jnp.mean(dxh * xh, axis=-1, keepdims=True))

    row = pl.BlockSpec((tm, d), lambda i: (i, 0))
    return pl.pallas_call(
        body, name="loss_head", grid=(s // tm,),
        in_specs=[row, _full((1, d)), row], out_specs=[_full((1, 1)), row, _full((1, d))],
        out_shape=[jax.ShapeDtypeStruct((1, 1), F32), jax.ShapeDtypeStruct((s, d), F32),
                   jax.ShapeDtypeStruct((1, d), F32)],
        compiler_params=_cp(dimension_semantics=("arbitrary",)),
    )(x, fg, target)


ADA_TN = 384


def _cond_fwd(cact, ada_w, ada_b_loc):
    nl, d, n = ada_w.shape

    def body(c_ref, w_ref, b_ref, o_ref):
        o_ref[...] = _dot(c_ref[...], w_ref[...]) + b_ref[...]

    return pl.pallas_call(
        body, name="cond_fwd", grid=(nl, n // ADA_TN),
        in_specs=[_full((N_DEV, d)), pl.BlockSpec((None, d, ADA_TN), lambda l, j: (l, 0, j)),
                  pl.BlockSpec((None, 1, ADA_TN), lambda l, j: (l, 0, j))],
        out_specs=pl.BlockSpec((None, N_DEV, ADA_TN), lambda l, j: (l, 0, j)),
        out_shape=jax.ShapeDtypeStruct((nl, N_DEV, n), F32),
        compiler_params=_cp(dimension_semantics=("parallel", "parallel")),
    )(cact, ada_w, ada_b_loc)


ELEMENTWISE_BLOCK_BYTES = 1 << 20


def _row_tile(r, c, itemsize=4):
    best = None
    for t in range(8, r + 1, 8):
        if r % t == 0 and t * c * itemsize <= ELEMENTWISE_BLOCK_BYTES:
            best = t
    return best if best is not None else r


def _adamw_math(w, g, m, v):
    m = ADAM_B1 * m + (1.0 - ADAM_B1) * g
    v = ADAM_B2 * v + (1.0 - ADAM_B2) * (g * g)
    m_hat = m / (1.0 - ADAM_B1 ** ADAM_STEP)
    v_hat = v / (1.0 - ADAM_B2 ** ADAM_STEP)
    delta = -ADAM_LR * (m_hat / (jnp.sqrt(v_hat) + ADAM_EPS) + ADAM_WD * w)
    return delta, m, v


def _ada_w_update(cact, dcond_loc, w, m, v):
    nl, d, n = w.shape

    def body(c_ref, dc_ref, w_ref, m_ref, v_ref, g_out, d_out, m_out, v_out):
        g = _dot_tn(c_ref[...], dc_ref[...])
        g_out[...] = g
        d_out[...], m_out[...], v_out[...] = _adamw_math(w_ref[...], g, m_ref[...], v_ref[...])

    blk = pl.BlockSpec((None, d, ADA_TN), lambda l, j: (l, 0, j))
    return pl.pallas_call(
        body, name="ada_w_update", grid=(nl, n // ADA_TN),
        in_specs=[_full((N_DEV, d)), pl.BlockSpec((None, N_DEV, ADA_TN), lambda l, j: (l, 0, j)), blk, blk, blk],
        out_specs=[blk] * 4, out_shape=[jax.ShapeDtypeStruct((nl, d, n), F32)] * 4,
        compiler_params=_cp(dimension_semantics=("parallel", "parallel")),
    )(cact, dcond_loc, w, m, v)


def _adamw(w, g, m, v, name):
    b, r, c = w.shape
    tr = _row_tile(r, c)

    def body(w_ref, g_ref, m_ref, v_ref, d_out, m_out, v_out):
        d_out[...], m_out[...], v_out[...] = _adamw_math(w_ref[...], g_ref[...], m_ref[...], v_ref[...])

    blk = pl.BlockSpec((None, tr, c), lambda i, j: (i, j, 0))
    return pl.pallas_call(
        body, name=name, grid=(b, r // tr), in_specs=[blk] * 4, out_specs=[blk] * 3,
        out_shape=[jax.ShapeDtypeStruct((b, r, c), F32)] * 3,
        compiler_params=_cp(dimension_semantics=("parallel", "parallel")),
    )(w, g, m, v)


def _place():
    x, y, c = lax.axis_index("x"), lax.axis_index("y"), lax.axis_index("c")
    chips = [(1 - x, y), (x, 1 - y), (1 - x, 1 - y)]
    return x, y, c, chips


def _remote(src, dst, send_sem, recv_sem, to):
    return pltpu.make_async_remote_copy(src_ref=src, dst_ref=dst, send_sem=send_sem, recv_sem=recv_sem,
                                        device_id=to, device_id_type=MESH_ID)


def _sems(n):
    return [pltpu.SemaphoreType.DMA((n,)), pltpu.SemaphoreType.DMA((n,))]


def _all_gather8(v, name):
    r, cdim = v.shape

    def body(x_ref, out_ref, stage, send_sems, recv_sems):
        x, y, c, chips = _place()
        sibling = (x, y, 1 - c)

        def slot(px, py, pc):
            return out_ref.at[4 * px + 2 * py + pc]

        first = [_remote(x_ref, slot(x, y, c), send_sems.at[0], recv_sems.at[0], sibling)]
        first += [_remote(x_ref, slot(x, y, c), send_sems.at[1 + j], recv_sems.at[1 + j], (*chip, c))
                  for j, chip in enumerate(chips)]
        for cp in first:
            cp.start()
        pltpu.sync_copy(x_ref, stage)
        pltpu.sync_copy(stage, slot(x, y, c))
        passed = []
        for j, chip in enumerate(chips):
            blk = slot(*chip, c)
            _remote(blk, blk, send_sems.at[1 + j], recv_sems.at[1 + j], (x, y, c)).wait_recv()
            fw = _remote(blk, blk, send_sems.at[4 + j], recv_sems.at[4 + j], sibling)
            fw.start()
            passed.append(fw)
        blk = slot(x, y, 1 - c)
        _remote(blk, blk, send_sems.at[0], recv_sems.at[0], (x, y, c)).wait_recv()
        for j, chip in enumerate(chips):
            blk = slot(*chip, 1 - c)
            _remote(blk, blk, send_sems.at[4 + j], recv_sems.at[4 + j], (x, y, c)).wait_recv()
        for cp in first + passed:
            cp.wait_send()

    return pl.pallas_call(
        body, name=name, out_shape=jax.ShapeDtypeStruct((N_DEV, r, cdim), v.dtype),
        in_specs=[ANY], out_specs=ANY,
        scratch_shapes=[pltpu.VMEM((r, cdim), v.dtype)] + _sems(7),
        compiler_params=_cp(),
    )(v)


def _place_weights(ws, layer, kidx):
    steps = 4
    shapes, in_specs, out_specs = [], [], []
    for w, kind in zip(ws, BIG_KINDS):
        _, a, b = w.shape
        in_specs.append(pl.BlockSpec((None, a // steps, b), lambda i, k: (layer, i, 0)))
        if kind == "col":
            shapes.append((2, a, 2 * b))
            out_specs.append(pl.BlockSpec((None, a // steps, b), lambda i, k: (k[0] // 2, i, k[0] % 2)))
        else:
            shapes.append((N_CHIP, a, b))
            out_specs.append(pl.BlockSpec((None, a // steps, b), lambda i, k: (k[0], i, 0)))

    def body(k_ref, *refs):
        for t in range(len(ws)):
            refs[len(ws) + t][...] = refs[t][...].astype(BF16)

    return pl.pallas_call(
        body, name="place_weights", out_shape=[jax.ShapeDtypeStruct(s, BF16) for s in shapes],
        grid_spec=pltpu.PrefetchScalarGridSpec(num_scalar_prefetch=1, grid=(steps,), in_specs=in_specs,
                                               out_specs=out_specs),
        compiler_params=_cp(dimension_semantics=("parallel",)),
    )(kidx, *ws)


HBM = pl.BlockSpec(memory_space=pltpu.HBM)
SEM = pl.BlockSpec(memory_space=pltpu.SEMAPHORE)
EFFECT = pltpu.SideEffectType.DATAFLOW_SIDE_EFFECTING


def _weight_block(ref, kind, k, h):
    if kind == "col":
        ncol = ref.shape[3] // 2
        return ref.at[k // 2, h, :, pl.ds(pl.multiple_of((k % 2) * ncol, LANES), ncol)]
    return ref.at[k, h]


def _in_hbm(a):
    return pltpu.with_memory_space_constraint(a, pltpu.HBM)


def _weight_send_start(placed, kinds, name):
    nt = len(placed)

    def body(*refs):
        send_sems, recv_sems = refs[nt], refs[nt + 1]
        dst = refs[nt + 2:2 * nt + 2]
        token = refs[2 * nt + 2]
        x, y, c, chips = _place()
        kme = 2 * x + y
        for t in range(nt):
            for j, chip in enumerate(chips):
                own = _weight_block(dst[t], kinds[t], kme, c)
                _remote(own, own, send_sems.at[3 * t + j], recv_sems.at[3 * t + j], (*chip, c)).start()
        token[...] = jnp.zeros_like(token)

    return pl.pallas_call(
        body, name=name,
        out_shape=(pltpu.SemaphoreType.DMA((3 * nt,)), pltpu.SemaphoreType.DMA((3 * nt,)),
                   *[pltpu.HBM(a.shape, a.dtype) for a in placed], jax.ShapeDtypeStruct((8, LANES), F32)),
        in_specs=[HBM] * nt, out_specs=(SEM, SEM, *[HBM] * nt, pl.BlockSpec(memory_space=pltpu.VMEM)),
        input_output_aliases={t: 2 + t for t in range(nt)},
        compiler_params=pltpu.CompilerParams(has_side_effects=EFFECT),
    )(*[_in_hbm(a) for a in placed])


def _weight_send_wait(send_sems, recv_sems, arrays, kinds, after, name):
    nt = len(arrays)

    def body(*refs):
        arr = refs[:nt]
        send_sems, recv_sems = refs[nt], refs[nt + 1]
        x, y, c, chips = _place()
        kme = 2 * x + y
        for t in range(nt):
            for j, chip in enumerate(chips):
                own = _weight_block(arr[t], kinds[t], kme, c)
                got = _weight_block(arr[t], kinds[t], 2 * chip[0] + chip[1], c)
                cp = _remote(own, got, send_sems.at[3 * t + j], recv_sems.at[3 * t + j], (*chip, c))
                cp.wait_send()
                cp.wait_recv()

    return pl.pallas_call(
        body, name=name, out_shape=[pltpu.HBM(a.shape, a.dtype) for a in arrays],
        in_specs=[HBM] * nt + [SEM, SEM, ANY], out_specs=[HBM] * nt,
        input_output_aliases={t: t for t in range(nt)},
        compiler_params=pltpu.CompilerParams(has_side_effects=EFFECT),
    )(*arrays, send_sems, recv_sems, after)


def _forward_copies(kinds):
    def make(refs, send_sems, recv_sems):
        x, y, c, chips = _place()
        cps = []
        for t in range(len(kinds)):
            for j, chip in enumerate(chips):
                blk = _weight_block(refs[t], kinds[t], 2 * chip[0] + chip[1], c)
                cps.append(_remote(blk, blk, send_sems.at[3 * t + j], recv_sems.at[3 * t + j], (x, y, 1 - c)))
        return cps
    return make


def _split_start(name, arrays, n_copies, make_copies):
    na = len(arrays)

    def body(*refs):
        send_sems, recv_sems = refs[na], refs[na + 1]
        for cp in make_copies(refs[na + 2:2 * na + 2], send_sems, recv_sems):
            cp.start()
        token = refs[2 * na + 2]
        token[...] = jnp.zeros_like(token)

    return pl.pallas_call(
        body, name=name,
        out_shape=(pltpu.SemaphoreType.DMA((n_copies,)), pltpu.SemaphoreType.DMA((n_copies,)),
                   *[pltpu.HBM(a.shape, a.dtype) for a in arrays], jax.ShapeDtypeStruct((8, LANES), F32)),
        in_specs=[HBM] * na, out_specs=(SEM, SEM, *[HBM] * na, pl.BlockSpec(memory_space=pltpu.VMEM)),
        input_output_aliases={t: 2 + t for t in range(na)},
        compiler_params=pltpu.CompilerParams(has_side_effects=EFFECT),
    )(*[_in_hbm(a) for a in arrays])


def _split_wait(name, started, make_copies, after):
    send_sems, recv_sems, *arrays, _ = started
    na = len(arrays)

    def body(*refs):
        send_sems, recv_sems = refs[na], refs[na + 1]
        for cp in make_copies(refs[:na], send_sems, recv_sems):
            cp.wait_send()
            cp.wait_recv()

    return pl.pallas_call(
        body, name=name, out_shape=[pltpu.HBM(a.shape, a.dtype) for a in arrays],
        in_specs=[HBM] * na + [SEM, SEM, ANY], out_specs=[HBM] * na,
        input_output_aliases={t: t for t in range(na)},
        compiler_params=pltpu.CompilerParams(has_side_effects=EFFECT),
    )(*arrays, send_sems, recv_sems, after)


def _exchange_copies(nt):
    def make(refs, send_sems, recv_sems):
        x, y, c, _ = _place()
        return [_remote(refs[t].at[:, 1 - c], refs[nt + t], send_sems.at[t], recv_sems.at[t], (x, y, 1 - c))
                for t in range(nt)]
    return make


def _sibling_exchange_start(views, name):
    lands = [lax.empty((v.shape[0],) + v.shape[2:], v.dtype) for v in views]
    return _split_start(name, list(views) + lands, len(views), _exchange_copies(len(views)))


def _sibling_exchange_wait(started, after, name):
    nt = (len(started) - 3) // 2
    outs = _split_wait(name, started, _exchange_copies(nt), after)
    return outs[:nt], outs[nt:]


def _scatter_copies(src, land, kinds, send_sems, recv_sems):
    x, y, c, chips = _place()
    cps = []
    for t in range(len(src)):
        for j, chip in enumerate(chips):
            k = 2 * chip[0] + chip[1]
            if kinds[t] == "col":
                ncol = land[t].shape[2]
                win = src[t].at[k // 2, :, pl.ds(pl.multiple_of((k % 2) * ncol, LANES), ncol)]
            else:
                win = src[t].at[k]
            cps.append(_remote(win, land[t].at[j], send_sems.at[3 * t + j], recv_sems.at[3 * t + j], (*chip, c)))
    return cps


def _chip_scatter_start(parts, kinds, name):
    nt = len(parts)
    shapes = []
    for p, kind in zip(parts, kinds):
        shapes.append((3, p.shape[1], p.shape[2] // 2) if kind == "col" else (3,) + p.shape[1:])

    def body(*refs):
        send_sems, recv_sems = refs[2 * nt], refs[2 * nt + 1]
        src, land = refs[2 * nt + 2:3 * nt + 2], refs[3 * nt + 2:4 * nt + 2]
        token = refs[4 * nt + 2]
        for cp in _scatter_copies(src, land, kinds, send_sems, recv_sems):
            cp.start()
        token[...] = jnp.zeros_like(token)

    lands = [lax.empty(s, BF16) for s in shapes]
    return pl.pallas_call(
        body, name=name,
        out_shape=(pltpu.SemaphoreType.DMA((3 * nt,)), pltpu.SemaphoreType.DMA((3 * nt,)),
                   *[pltpu.HBM(a.shape, a.dtype) for a in parts], *[pltpu.HBM(s, BF16) for s in shapes],
                   jax.ShapeDtypeStruct((8, LANES), F32)),
        in_specs=[HBM] * (2 * nt), out_specs=(SEM, SEM, *[HBM] * (2 * nt), pl.BlockSpec(memory_space=pltpu.VMEM)),
        input_output_aliases={t: 2 + t for t in range(2 * nt)},
        compiler_params=pltpu.CompilerParams(has_side_effects=EFFECT),
    )(*[_in_hbm(a) for a in parts], *[_in_hbm(a) for a in lands])


def _chip_scatter_wait(send_sems, recv_sems, parts, lands, kinds, after, name):
    nt = len(parts)

    def body(*refs):
        src, land = refs[:nt], refs[nt:2 * nt]
        send_sems, recv_sems = refs[2 * nt], refs[2 * nt + 1]
        for cp in _scatter_copies(src, land, kinds, send_sems, recv_sems):
            cp.wait_send()
            cp.wait_recv()

    outs = pl.pallas_call(
        body, name=name, out_shape=[pltpu.HBM(a.shape, a.dtype) for a in list(parts) + list(lands)],
        in_specs=[HBM] * (2 * nt) + [SEM, SEM, ANY], out_specs=[HBM] * (2 * nt),
        input_output_aliases={t: t for t in range(2 * nt)},
        compiler_params=pltpu.CompilerParams(has_side_effects=EFFECT),
    )(*parts, *lands, send_sems, recv_sems, after)
    return outs[:nt], outs[nt:]


def _share_copies(nt):
    def make(refs, send_sems, recv_sems):
        x, y, c, _ = _place()
        return [_remote(refs[t].at[c], refs[t].at[c], send_sems.at[t], recv_sems.at[t], (x, y, 1 - c))
                for t in range(nt)]
    return make


def _sibling_share_start(fulls, name):
    return _split_start(name, list(fulls), len(fulls), _share_copies(len(fulls)))


def _sibling_share_wait(started, after, name):
    return _split_wait(name, started, _share_copies(len(started) - 3), after)


SUM_STEPS = 4


def _pair_sum(views, lands, ck):
    nt = len(views)
    in_specs, out_specs, shapes = [], [], []
    for v in views:
        b, _, r, cc = v.shape
        per = SUM_STEPS // b
        tr = r // per
        in_specs.append(pl.BlockSpec((None, None, tr, cc), lambda i, s, per=per: (i // per, s[0], i % per, 0)))
        out_specs.append(pl.BlockSpec((None, tr, cc), lambda i, s, per=per: (i // per, i % per, 0)))
        shapes.append((b, r, cc))
    in_specs = in_specs + out_specs

    def body(s_ref, *refs):
        for t in range(nt):
            refs[2 * nt + t][...] = (refs[t][...].astype(F32) + refs[nt + t][...].astype(F32)).astype(BF16)

    return pl.pallas_call(
        body, name="grad_pair_sum", out_shape=[jax.ShapeDtypeStruct(s, BF16) for s in shapes],
        grid_spec=pltpu.PrefetchScalarGridSpec(num_scalar_prefetch=1, grid=(SUM_STEPS,), in_specs=in_specs,
                                               out_specs=out_specs),
        compiler_params=_cp(dimension_semantics=("parallel",)),
    )(ck, *views, *lands)


def _chip_sum(parts, lands, kinds, ck):
    nt = len(parts)
    steps = 2
    in_own, in_land, out_specs, shapes = [], [], [], []
    for ld, kind in zip(lands, kinds):
        _, r, cc = ld.shape
        tr = r // steps
        if kind == "col":
            in_own.append(pl.BlockSpec((None, tr, cc), lambda i, s: (s[1] // 2, i, s[1] % 2)))
        else:
            in_own.append(pl.BlockSpec((None, tr, cc), lambda i, s: (s[1], i, 0)))
        in_land.append(pl.BlockSpec((3, tr, cc), lambda i, s: (0, i, 0)))
        out_specs.append(pl.BlockSpec((None, tr, cc), lambda i, s: (s[0], i, 0)))
        shapes.append((2, r, cc))

    def body(s_ref, *refs):
        for t in range(nt):
            acc = refs[t][...].astype(F32)
            for j in range(3):
                acc = acc + refs[nt + t][j].astype(F32)
            refs[2 * nt + t][...] = acc

    return pl.pallas_call(
        body, name="grad_chip_sum", out_shape=[jax.ShapeDtypeStruct(s, F32) for s in shapes],
        grid_spec=pltpu.PrefetchScalarGridSpec(num_scalar_prefetch=1, grid=(steps,), in_specs=in_own + in_land,
                                               out_specs=out_specs),
        compiler_params=_cp(dimension_semantics=("parallel",)),
    )(ck, *parts, *lands)


def _sum8(g):
    _, r, cc = g.shape
    tr = _row_tile(r, N_DEV * cc)

    def body(g_ref, o_ref):
        acc = g_ref[0].astype(F32)
        for d in range(1, N_DEV):
            acc = acc + g_ref[d].astype(F32)
        o_ref[...] = acc

    return pl.pallas_call(
        body, name="small_grad_sum", grid=(r // tr,),
        in_specs=[pl.BlockSpec((N_DEV, tr, cc), lambda i: (0, i, 0))],
        out_specs=pl.BlockSpec((tr, cc), lambda i: (i, 0)),
        out_shape=jax.ShapeDtypeStruct((r, cc), F32),
        compiler_params=_cp(dimension_semantics=("parallel",)),
    )(g)


def _silu_rows(c):
    def body(c_ref, o_ref):
        v = c_ref[...]
        o_ref[...] = v * jax.nn.sigmoid(v)

    return pl.pallas_call(body, name="cond_silu", out_shape=jax.ShapeDtypeStruct(c.shape, F32))(c)


def _pack(arrays):
    rows = []
    for a in arrays:
        flat = a.reshape(-1)
        rows.append(jnp.pad(flat, (0, (-flat.shape[0]) % (8 * LANES))).reshape(-1, LANES))
    n = sum(r.shape[0] for r in rows)
    if n % 256:
        rows.append(jnp.zeros((256 - n % 256, LANES), rows[0].dtype))
    return jnp.concatenate(rows, axis=0)


def _unpack(packed, shapes):
    out, off = [], 0
    for s in shapes:
        n = math.prod(s)
        nr = 8 * -(-n // (8 * LANES))
        out.append(packed[off:off + nr].reshape(-1)[:n].reshape(s))
        off += nr
    return out


def _as_rows(a):
    return a.reshape(1, -1) if a.ndim == 1 else a.reshape(-1, a.shape[-1])


def _adamw_many(ws, gs, ms, vs, name, steps=1):
    nt = len(ws)

    def body(*refs):
        for t in range(nt):
            w_ref, g_ref, m_ref, v_ref = (refs[k * nt + t] for k in range(4))
            d, m, v = _adamw_math(w_ref[...], g_ref[...], m_ref[...], v_ref[...])
            refs[4 * nt + t][...] = d
            refs[5 * nt + t][...] = m
            refs[6 * nt + t][...] = v

    shapes = [jax.ShapeDtypeStruct(a.shape, F32) for a in ws]
    if steps == 1:
        outs = pl.pallas_call(body, name=name, out_shape=shapes * 3, compiler_params=_cp())(*ws, *gs, *ms, *vs)
    else:
        specs = [pl.BlockSpec((a.shape[0] // steps, a.shape[1]), lambda i: (i, 0)) for a in ws]
        outs = pl.pallas_call(
            body, name=name, grid=(steps,), in_specs=specs * 4, out_specs=specs * 3, out_shape=shapes * 3,
            compiler_params=_cp(dimension_semantics=("parallel",)),
        )(*ws, *gs, *ms, *vs)
    return outs[:nt], outs[nt:2 * nt], outs[2 * nt:]


def _exchange_big_grads(grads, kinds, layer):
    views = []
    for g, kind in zip(grads, kinds):
        if kind == "col":
            views.append(g.reshape(2, 2, g.shape[1] // 2, g.shape[2]))
        else:
            views.append(g.reshape(N_CHIP, 2, g.shape[0] // (2 * N_CHIP), g.shape[1]))
    return _sibling_exchange_start(views, "grad_exchange_start_%d" % layer)


def _scatter_big_grads(exchanged, kinds, ck, after, layer):
    views, lands = _sibling_exchange_wait(exchanged, after, "grad_exchange_wait_%d" % layer)
    parts = _pair_sum(views, lands, ck)
    return _chip_scatter_start(parts, kinds, "grad_scatter_start_%d" % layer)


def _finish_big_grads(started, kinds, ck, after, layer):
    nt = len(kinds)
    send_sems, recv_sems = started[0], started[1]
    parts, lands = started[2:2 + nt], started[2 + nt:2 + 2 * nt]
    parts, lands = _chip_scatter_wait(send_sems, recv_sems, parts, lands, kinds, after, "grad_scatter_wait_%d" % layer)
    return _sibling_share_start(_chip_sum(parts, lands, kinds, ck), "grad_share_start_%d" % layer)


def _adamw_layer(ws, gs, ms, vs, stacks, layer, name, steps):
    nt = len(ws)
    stacks = [s if s is not None else tuple(lax.empty(w.shape, F32) for _ in range(4)) for s, w in zip(stacks, ws)]

    def body(*refs):
        for t in range(nt):
            w_ref, g_ref, m_ref, v_ref = (refs[k * nt + t] for k in range(4))
            outs = refs[8 * nt + 4 * t:8 * nt + 4 * t + 4]
            g = g_ref[...]
            outs[0][...] = g
            outs[1][...], outs[2][...], outs[3][...] = _adamw_math(w_ref[...], g, m_ref[...], v_ref[...])

    in_specs, g_specs, out_specs = [], [], []
    for w in ws:
        _, r, c = w.shape
        in_specs.append(pl.BlockSpec((None, r // steps, c), lambda i: (layer, i, 0)))
        g_specs.append(pl.BlockSpec((r // steps, c), lambda i: (i, 0)))
        out_specs += [pl.BlockSpec((None, r // steps, c), lambda i: (layer, i, 0))] * 4
    in_specs = in_specs + g_specs + in_specs * 2 + [ANY] * (4 * nt)
    flat = [a for s in stacks for a in s]
    outs = pl.pallas_call(
        body, name=name, grid=(steps,), in_specs=in_specs, out_specs=out_specs,
        out_shape=[jax.ShapeDtypeStruct(a.shape, F32) for a in flat],
        input_output_aliases={4 * nt + k: k for k in range(4 * nt)},
        compiler_params=_cp(dimension_semantics=("parallel",)),
    )(*ws, *gs, *ms, *vs, *flat)
    return [tuple(outs[4 * t:4 * t + 4]) for t in range(nt)]


SMALL_NAMES = ["ada_b", "norm1_g", "norm2_g", "sgu_w", "sgu_b", "pool_w", "pool_scale", "conv_w", "s5_lambda_re",
               "s5_lambda_im", "s5_b_re", "s5_b_im", "s5_c_re", "s5_c_im", "s5_d", "s5_log_dt", "s5_glu_w", "s5_glu_b",
               "mix_norm_g", "norm3_g", "final_norm_g"]
BIG_NAMES = ["ffn1_w_in", "ffn1_w_out", "w_mix_in", "w_mix_out", "ffn2_w_in", "ffn2_w_out"]
BIG_KINDS = ["col", "row", "row", "row", "col", "row"]
WEIGHT_ORDER = ["ada_w", "ada_b", "norm1_g", "ffn1_w_in", "ffn1_w_out", "norm2_g", "w_mix_in", "sgu_w", "sgu_b", "pool_w",
                "pool_scale", "conv_w", "s5_lambda_re", "s5_lambda_im", "s5_b_re", "s5_b_im", "s5_c_re", "s5_c_im", "s5_d",
                "s5_log_dt", "s5_glu_w", "s5_glu_b", "mix_norm_g", "w_mix_out", "norm3_g", "ffn2_w_in", "ffn2_w_out",
                "final_norm_g"]


def _local_step(x, target, cond, fetch_weights, prefetch_weights, p, emit_grads):
    nl, d = DEPTH, x.shape[1]
    row = lambda a: a.reshape(1, -1)
    saved = []
    for l in range(nl):
        (wi1, wo1, wmit, wmo, wi2, wo2), tok = fetch_weights(l, x)
        cl = cond[l] + tok
        mod1, mod2, mod3 = cl[0:3], cl[3:6], cl[6:9]
        lre, lim = p["s5_lambda_re"][l].reshape(-1), p["s5_lambda_im"][l].reshape(-1)
        ldt = jnp.repeat(p["s5_log_dt"][l], 64)
        rowp = jnp.stack([lre, lim, ldt])
        sp = (rowp, rowp.T, p["s5_b_re"][l].reshape(N_STATE, 16), p["s5_b_im"][l].reshape(N_STATE, 16),
              p["s5_c_re"][l].reshape(W_GRP, 64), p["s5_c_im"][l].reshape(W_GRP, 64), row(p["s5_d"][l]))
        glu = (p["s5_glu_w"][l], row(p["s5_glu_b"][l]))
        bias_full = jnp.repeat(p["sgu_b"][l].T, 64, axis=1)
        pw2 = p["pool_w"][l].reshape(W_GRP, 64)
        x1, h1, a1, b1, o1 = _ffn_fwd(x, mod1, row(p["norm1_g"][l]), wi1, wo1)
        z, h2 = _mix_in_fwd(x1, mod2, row(p["norm2_g"][l]), wmit)
        ya = _sgu_fwd(z, p["sgu_w"][l], bias_full)
        yb, yc = _poolconv_fwd(z, pw2, row(p["pool_scale"][l]), p["conv_w"][l])
        ypre = _s5_core_fwd(z, sp)
        yd = _s5_glu_fwd(ypre, *glu)
        ys = (ya, yb, yc, yd)
        x2, m = _mix_out_fwd(ys, row(p["mix_norm_g"][l]), wmo, x1, mod2[2:3])
        mod3 = mod3 + prefetch_weights(l + 1, x2)
        x3, h3, a3, b3, o3 = _ffn_fwd(x2, mod3, row(p["norm3_g"][l]), wi2, wo2)
        saved.append((x, x1, x2, h1, a1, b1, o1, z, h2, ys, m, h3, a3, b3, o3, sp, bias_full, pw2, ypre, glu,
                      (wi1, wo1, wmit, wmo, wi2, wo2), cl))
        x = x3

    loss, dx, dfg = _loss_head(x, row(p["final_norm_g"]), target)

    sg = {n: [None] * nl for n in SMALL_NAMES if n not in ("ada_b", "final_norm_g")}
    dcond = [None] * nl
    s5_da, s5_dk = [None] * nl, [None] * nl
    tok = 0.0
    for l in reversed(range(nl)):
        (x0, x1, x2, h1, a1, b1, o1, z, h2, ys, m, h3, a3, b3, o3, sp, bias_full, pw2, ypre, glu,
         (wi1, wo1, wmit, wmo, wi2, wo2), cl) = saved[l]
        cl = cl + tok
        mod1, mod2, mod3 = cl[0:3], cl[3:6], cl[6:9]
        do, dgate3 = _gate_bwd(dx, o3, mod3[2:3], 0.5)
        dza, dzb, dwi2, dwo2 = _ffn_bwd_main(do, h3, a3, b3, wo2)
        dx, rows3 = _ffn_bwd_in(dza, dzb, wi2, x2, dx, mod3, row(p["norm3_g"][l]))
        outs = _mix_out_bwd(dx, m, mod2[2:3], ys, row(p["mix_norm_g"][l]), wmo)
        dys, dgate2, dmng, dwmo = outs[0:4], outs[4], outs[5], outs[6]
        dza_, dsw, dsb = _sgu_bwd(z, dys[0], p["sgu_w"][l], bias_full)
        dzb_, dzc_, dwbd, dps, dcw = _poolconv_bwd(z, dys[1], dys[2], pw2, row(p["pool_scale"][l]), p["conv_w"][l])
        dypre, dgw, dgb = _s5_glu_bwd(ypre, dys[3], *glu)
        dzd_, dbr, dbi, dcr, dci, dd, da, dk = _s5_core_bwd(z, dypre, sp)
        dx, rows2, dwmit = _mix_in_bwd((dza_, dzb_, dzc_, dzd_), h2, wmit, x1, dx, mod2, row(p["norm2_g"][l]))
        do, dgate1 = _gate_bwd(dx, o1, mod1[2:3], 0.5)
        dza, dzb, dwi1, dwo1 = _ffn_bwd_main(do, h1, a1, b1, wo1)
        tok, layer_done = emit_grads(l, [dwi1, dwo1, dwmit, dwmo, dwi2, dwo2])
        dx, rows1 = _ffn_bwd_in(dza, dzb, wi1, x0, dx, mod1 + tok, row(p["norm1_g"][l]))
        tok = layer_done(dx)
        dcond[l] = jnp.concatenate([rows1[0:2], dgate1, rows2[0:2], dgate2, rows3[0:2], dgate3], axis=0)
        sg["norm1_g"][l], sg["norm2_g"][l], sg["norm3_g"][l] = rows1[2], rows2[2], rows3[2]
        sg["mix_norm_g"][l] = dmng[0]
        sg["sgu_w"][l] = dsw
        sg["sgu_b"][l] = dsb[:, 0:4].T
        g4 = dwbd.reshape(4, 64, 4, 64)
        sg["pool_w"][l] = jnp.stack([g4[k, :, k, :] for k in range(4)])
        sg["pool_scale"][l] = dps[0]
        sg["conv_w"][l] = dcw[0:3]
        sg["s5_b_re"][l], sg["s5_b_im"][l] = dbr.reshape(16, 64, 16), dbi.reshape(16, 64, 16)
        sg["s5_c_re"][l], sg["s5_c_im"][l] = dcr.reshape(16, 16, 64), dci.reshape(16, 16, 64)
        sg["s5_d"][l] = dd[0]
        sg["s5_glu_w"][l], sg["s5_glu_b"][l] = dgw, dgb[0]
        s5_da[l], s5_dk[l] = da, dk

    n16 = nl * 16
    dlre, dlim, dldt = _s5_param_bwd(
        p["s5_lambda_re"].reshape(n16, 64), p["s5_lambda_im"].reshape(n16, 64),
        jnp.repeat(p["s5_log_dt"].reshape(n16, 1), 64, axis=1),
        jnp.stack([a[0] for a in s5_da]).reshape(n16, 64), jnp.stack([a[1] for a in s5_da]).reshape(n16, 64),
        jnp.stack([k[:, 0] for k in s5_dk]).reshape(n16, 64), jnp.stack([k[:, 1] for k in s5_dk]).reshape(n16, 64))
    small = {n: jnp.stack(v) for n, v in sg.items() if v[0] is not None}
    small["s5_lambda_re"] = dlre.reshape(nl, 16, 64)
    small["s5_lambda_im"] = dlim.reshape(nl, 16, 64)
    small["s5_log_dt"] = dldt.reshape(nl, 16)
    small["final_norm_g"] = dfg[0]
    return loss, dx, small, jnp.stack(dcond)


def kernel(x, c, ada_w, ada_b, norm1_g, ffn1_w_in, ffn1_w_out, norm2_g, w_mix_in, sgu_w, sgu_b, pool_w, pool_scale, conv_w, s5_lambda_re, s5_lambda_im, s5_b_re, s5_b_im, s5_c_re, s5_c_im, s5_d, s5_log_dt, s5_glu_w, s5_glu_b, mix_norm_g, w_mix_out, norm3_g, ffn2_w_in, ffn2_w_out, final_norm_g, loss_target, m_ada_w, m_ada_b, m_norm1_g, m_ffn1_w_in, m_ffn1_w_out, m_norm2_g, m_w_mix_in, m_sgu_w, m_sgu_b, m_pool_w, m_pool_scale, m_conv_w, m_s5_lambda_re, m_s5_lambda_im, m_s5_b_re, m_s5_b_im, m_s5_c_re, m_s5_c_im, m_s5_d, m_s5_log_dt, m_s5_glu_w, m_s5_glu_b, m_mix_norm_g, m_w_mix_out, m_norm3_g, m_ffn2_w_in, m_ffn2_w_out, m_final_norm_g, v_ada_w, v_ada_b, v_norm1_g, v_ffn1_w_in, v_ffn1_w_out, v_norm2_g, v_w_mix_in, v_sgu_w, v_sgu_b, v_pool_w, v_pool_scale, v_conv_w, v_s5_lambda_re, v_s5_lambda_im, v_s5_b_re, v_s5_b_im, v_s5_c_re, v_s5_c_im, v_s5_d, v_s5_log_dt, v_s5_glu_w, v_s5_glu_b, v_mix_norm_g, v_w_mix_out, v_norm3_g, v_ffn2_w_in, v_ffn2_w_out, v_final_norm_g):
    args = dict(locals())
    w = {n: args[n] for n in WEIGHT_ORDER}
    mom = {n: args["m_" + n] for n in WEIGHT_ORDER}
    vel = {n: args["v_" + n] for n in WEIGHT_ORDER}
    nl, d = DEPTH, x.shape[-1]
    s = x.shape[1]
    px, py, pc = lax.axis_index("x"), lax.axis_index("y"), lax.axis_index("c")
    kme = 2 * px + py
    me = 2 * kme + pc
    kidx = jnp.reshape(kme, (1,)).astype(jnp.int32)

    shards = [ffn1_w_in, ffn1_w_out, jnp.swapaxes(w_mix_in, 1, 2), w_mix_out, ffn2_w_in, ffn2_w_out]
    started_weights = {}

    def start_weights(l):
        placed = _place_weights(shards, l, kidx)
        views = [a.reshape(a.shape[0], 2, a.shape[1] // 2, a.shape[2]) for a in placed]
        *handles, token = _weight_send_start(views, BIG_KINDS, "weight_send_start_%d" % l)
        started_weights[l] = handles
        return token[0, 0]

    tok0 = start_weights(0)

    cact = _silu_rows(c + tok0)
    pre = _pack([cact, conv_w, s5_glu_w])
    pre_all = _all_gather8(pre, "gather_prelude")
    parts = [_unpack(pre_all[dev], [cact.shape, conv_w.shape, s5_glu_w.shape]) for dev in range(N_DEV)]
    cact_all = pre_all[:, :d // LANES, :].reshape(N_DEV, d)
    conv_full = jnp.concatenate([parts[2 * k][1] for k in range(N_CHIP)], axis=2)
    glu_full = jnp.concatenate([parts[2 * k][2] for k in range(N_CHIP)], axis=1)

    n_ada = ada_w.shape[2]
    ada_b_loc = lax.dynamic_slice_in_dim(ada_b, kme * n_ada, n_ada, axis=1).reshape(nl, 1, n_ada)
    cond_part = _cond_fwd(cact_all, ada_w, ada_b_loc)
    cond_all = _all_gather8(cond_part.reshape(nl * N_DEV, n_ada), "gather_cond").reshape(N_DEV, nl, N_DEV, n_ada)
    cond_me = jnp.concatenate(
        [lax.dynamic_index_in_dim(cond_all[2 * k], me, axis=1, keepdims=False) for k in range(N_CHIP)], axis=1)
    cond = cond_me.reshape(nl, 9, d)

    forwarding = {}

    def prefetch_weights(l, after):
        if l >= nl:
            return 0.0
        send_sems, recv_sems, *views = started_weights.pop(l)
        views = _weight_send_wait(send_sems, recv_sems, views, BIG_KINDS, after, "weight_send_wait_%d" % l)
        forwarding[l] = _split_start("weight_forward_start_%d" % l, views, 3 * len(views), _forward_copies(BIG_KINDS))
        return forwarding[l][-1][0, 0]

    def fetch_weights(l, after):
        if l not in forwarding:
            prefetch_weights(l, after)
        views = _split_wait("weight_forward_wait_%d" % l, forwarding.pop(l), _forward_copies(BIG_KINDS), after)
        tok = start_weights(l + 1) if l + 1 < nl else 0.0
        full = [v.reshape(2, 2 * v.shape[2], v.shape[3]) if kind == "col" else v.reshape(-1, v.shape[3])
                for v, kind in zip(views, BIG_KINDS)]
        return full, tok

    ck = jnp.stack([pc, kme]).astype(jnp.int32)
    scattering, sharing = [], []
    stacks = {n: None for n in BIG_NAMES}
    groups = ((["ffn1_w_in", "ffn2_w_in"], 16, "adamw_w_in"),
              (["ffn1_w_out", "w_mix_in", "w_mix_out", "ffn2_w_out"], 8, "adamw_w_out"))

    def apply_adamw(l, fulls):
        g = {n: f.reshape(2 * f.shape[1], f.shape[2]) for n, f in zip(BIG_NAMES, fulls)}
        g["w_mix_in"] = g["w_mix_in"].T
        for names, steps, call in groups:
            outs = _adamw_layer([w[n] for n in names], [g[n] for n in names], [mom[n] for n in names],
                                [vel[n] for n in names], [stacks[n] for n in names], l, call, steps)
            stacks.update(zip(names, outs))

    def retire(after):
        if sharing:
            l2, shared = sharing.pop(0)
            apply_adamw(l2, _sibling_share_wait(shared, after, "grad_share_wait_%d" % l2))
        if scattering:
            l1, scattered = scattering.pop(0)
            sharing.append((l1, _finish_big_grads(scattered, BIG_KINDS, ck, after, l1)))

    def emit_grads(l, grads_l):
        exchanged = _exchange_big_grads(grads_l, BIG_KINDS, l)

        def layer_done(after):
            started = _scatter_big_grads(exchanged, BIG_KINDS, ck, after, l)
            retire(after)
            scattering.append((l, started))
            return started[-1][0, 0]

        return exchanged[-1][0, 0], layer_done

    p = {n: w[n] for n in SMALL_NAMES}
    p["conv_w"], p["s5_glu_w"] = conv_full, glu_full
    loss, dx, small, dcond = _local_step(x[0], loss_target[0], cond, fetch_weights, prefetch_weights, p, emit_grads)

    small_order = [n for n in SMALL_NAMES if n != "ada_b"]
    packed = _pack([dcond] + [small[n] for n in small_order])
    gathered_small = _all_gather8(packed.astype(BF16), "gather_small_grads")
    total = _sum8(gathered_small)
    shapes = [dcond.shape] + [small[n].shape for n in small_order]
    tot = dict(zip(["ada_b"] + small_order, _unpack(total, shapes)))
    grads = {n: tot[n] for n in SMALL_NAMES}
    grads["ada_b"] = tot["ada_b"].reshape(nl, 9 * d)
    grads["conv_w"] = lax.dynamic_slice_in_dim(tot["conv_w"], kme * conv_w.shape[2], conv_w.shape[2], axis=2)
    grads["s5_glu_w"] = lax.dynamic_slice_in_dim(tot["s5_glu_w"], kme * s5_glu_w.shape[1], s5_glu_w.shape[1], axis=1)

    dcond_all = gathered_small.reshape(N_DEV, -1)[:, :dcond.size].reshape(N_DEV, nl, 9 * d)
    dcond_loc = jnp.swapaxes(lax.dynamic_slice_in_dim(dcond_all, kme * n_ada, n_ada, axis=2), 0, 1)
    g_ada, d_ada, m_ada, v_ada = _ada_w_update(cact_all, dcond_loc, ada_w, m_ada_w, v_ada_w)

    while scattering or sharing:
        retire(g_ada)
    delta, new_m, new_v = {}, {}, {}
    for n in BIG_NAMES:
        grads[n], delta[n], new_m[n], new_v[n] = stacks[n]

    grads["ada_w"], delta["ada_w"], new_m["ada_w"], new_v["ada_w"] = g_ada, d_ada, m_ada, v_ada
    wide = ("s5_b_re", "s5_b_im")
    for names, call, steps in (([n for n in SMALL_NAMES if n not in wide], "adamw_small", 1),
                               (list(wide), "adamw_s5_b", DEPTH)):
        outs = _adamw_many(*[[_as_rows(t[n]) for n in names] for t in (w, grads, mom, vel)], call, steps)
        for res, o in zip((delta, new_m, new_v), outs):
            res.update({n: a.reshape(w[n].shape) for n, a in zip(names, o)})

    loss_total = lax.psum(loss[0, 0], ("x", "y", "c"))
    return (loss_total, dx[None], *[grads[n] for n in WEIGHT_ORDER], *[delta[n] for n in WEIGHT_ORDER],
            *[new_m[n] for n in WEIGHT_ORDER], *[new_v[n] for n in WEIGHT_ORDER])
```

```python
import functools
import math

import jax
import jax.numpy as jnp
from jax import lax
from jax.experimental import pallas as pl
from jax.experimental.pallas import tpu as pltpu

F32, BF16 = jnp.float32, jnp.bfloat16
EPS = 1e-6
DEPTH = 4
N_DEV = 8
N_CHIP = 4
W_GRP = 256
CHUNK = 128
N_SEG = 8
LANES = 128
FFN_TF = 256
FFN_TF_WIDE = 1408
FFN_TM_WIDE = 512
VMEM_LIMIT = 56 * 1024 * 1024
ADAM_LR, ADAM_B1, ADAM_B2, ADAM_EPS, ADAM_WD, ADAM_STEP = 0.001, 0.9, 0.999, 1e-08, 0.01, 10
MESH_ID = pl.DeviceIdType.MESH
HI = lax.Precision.HIGHEST
ANY = pl.BlockSpec(memory_space=pl.ANY)


def _cp(**kw):
    return pltpu.CompilerParams(vmem_limit_bytes=VMEM_LIMIT, **kw)


def _dot(a, b):
    return jnp.dot(a.astype(BF16), b.astype(BF16), preferred_element_type=F32)


def _dot_nt(a, b):
    return lax.dot_general(a.astype(BF16), b.astype(BF16), (((1,), (1,)), ((), ())), preferred_element_type=F32)


def _dot_tn(a, b):
    return lax.dot_general(a.astype(BF16), b.astype(BF16), (((0,), (0,)), ((), ())), preferred_element_type=F32)


def _dot_hi(a, b):
    return jnp.dot(a, b, preferred_element_type=F32, precision=HI)


def _gelu(x):
    k = 0.7978845608028654
    t = jnp.tanh(k * (x + 0.044715 * x * x * x))
    return 0.5 * x * (1.0 + t), t


def _gelu_grad(x, t):
    k = 0.7978845608028654
    return 0.5 * (1.0 + t) + 0.5 * x * (1.0 - t * t) * k * (1.0 + 3.0 * 0.044715 * x * x)


def _iota(shape, axis):
    return lax.broadcasted_iota(jnp.int32, shape, axis)


def _full(shape):
    nd = len(shape)
    return pl.BlockSpec(shape, lambda *_: (0,) * nd)


def _norm_mod(xv, g, shift, scale):
    r = lax.rsqrt(jnp.mean(xv * xv, axis=-1, keepdims=True) + EPS)
    return (xv * r * g) * (1.0 + scale) + shift


def _norm_mod_bwd(xv, g, scale, dh):
    r = lax.rsqrt(jnp.mean(xv * xv, axis=-1, keepdims=True) + EPS)
    xh = xv * r
    n = xh * g
    dsh = jnp.sum(dh, axis=0, keepdims=True)
    dsc = jnp.sum(dh * n, axis=0, keepdims=True)
    dn = dh * (1.0 + scale)
    dg = jnp.sum(dn * xh, axis=0, keepdims=True)
    dxh = dn * g
    dx = r * (dxh - xh * jnp.mean(dxh * xh, axis=-1, keepdims=True))
    return dx, dsh, dsc, dg


def _tm(s):
    return min(s, 1024)


def _ffn_fwd(x, mod, g, wi, wo):
    s, d = x.shape
    f = wo.shape[0]
    tf, tm = FFN_TF_WIDE, min(s, FFN_TM_WIDE)
    nf, nt = f // tf, s // tm

    def body(x_ref, mod_ref, g_ref, wa_ref, wb_ref, wo_ref, xn_ref, h_ref, a_ref, b_ref, o_ref, acc):
        j = pl.program_id(1)

        @pl.when(j == 0)
        def _():
            hh = _norm_mod(x_ref[...], g_ref[...], mod_ref[0:1, :], mod_ref[1:2, :])
            h_ref[...] = hh.astype(BF16)
            acc[...] = jnp.zeros_like(acc)

        h = h_ref[...]
        a = jnp.dot(h, wa_ref[...], preferred_element_type=F32)
        b = jnp.dot(h, wb_ref[...], preferred_element_type=F32)
        a_ref[...] = a.astype(BF16)
        b_ref[...] = b.astype(BF16)
        u = (a * jax.nn.sigmoid(a)) * b
        acc[...] += jnp.dot(u.astype(BF16), wo_ref[...], preferred_element_type=F32)

        @pl.when(j == nf - 1)
        def _():
            o = acc[...]
            o_ref[...] = o.astype(BF16)
            xn_ref[...] = x_ref[...] + 0.5 * mod_ref[2:3, :] * o

    row = pl.BlockSpec((tm, d), lambda i, j: (i, 0))
    chunk = pl.BlockSpec((tm, tf), lambda i, j: (i, j))
    return pl.pallas_call(
        body, name="ffn_fwd", grid=(nt, nf),
        in_specs=[row, _full((3, d)), _full((1, d)),
                  pl.BlockSpec((None, d, tf), lambda i, j: (0, 0, j)),
                  pl.BlockSpec((None, d, tf), lambda i, j: (1, 0, j)),
                  pl.BlockSpec((tf, d), lambda i, j: (j, 0))],
        out_specs=[row, row, chunk, chunk, row],
        out_shape=[jax.ShapeDtypeStruct((s, d), F32), jax.ShapeDtypeStruct((s, d), BF16),
                   jax.ShapeDtypeStruct((s, f), BF16), jax.ShapeDtypeStruct((s, f), BF16),
                   jax.ShapeDtypeStruct((s, d), BF16)],
        scratch_shapes=[pltpu.VMEM((tm, d), F32)],
        compiler_params=_cp(dimension_semantics=("parallel", "arbitrary")),
    )(x, mod, g, wi, wi, wo)


def _ffn_bwd_main(dxo, o, gate, h, a, b, wo):
    s, d = dxo.shape
    f = wo.shape[0]
    tf = FFN_TF
    nf = f // tf

    def body(dxo_ref, o_ref, gate_ref, h_ref, a_ref, b_ref, wo_ref, dza_ref, dzb_ref, dwi_ref, dwo_ref, dg_ref, do_s):
        @pl.when(pl.program_id(0) == 0)
        def _():
            dxv = dxo_ref[...]
            do_s[...] = (0.5 * gate_ref[...] * dxv).astype(BF16)
            dg_ref[...] = 0.5 * jnp.sum(o_ref[...].astype(F32) * dxv, axis=0, keepdims=True)

        dov = do_s[...]
        hv = h_ref[...]
        du = lax.dot_general(dov, wo_ref[...], (((1,), (1,)), ((), ())), preferred_element_type=F32)
        av = a_ref[...].astype(F32)
        bv = b_ref[...].astype(F32)
        sa = jax.nn.sigmoid(av)
        si = av * sa
        u = (si * bv).astype(BF16)
        da = (du * bv * (sa * (1.0 + av * (1.0 - sa)))).astype(BF16)
        db = (du * si).astype(BF16)
        dza_ref[...] = da
        dzb_ref[...] = db
        dwo_ref[...] = _dot_tn(u, dov).astype(BF16)
        dwi_ref[0] = _dot_tn(hv, da).astype(BF16)
        dwi_ref[1] = _dot_tn(hv, db).astype(BF16)

    chunk = pl.BlockSpec((s, tf), lambda j: (0, j))
    once = lambda: pl.BlockSpec((s, d), lambda j: (0, 0), pipeline_mode=pl.Buffered(1))
    return pl.pallas_call(
        body, name="ffn_bwd_main", grid=(nf,),
        in_specs=[once(), once(), _full((1, d)), once(), chunk, chunk, pl.BlockSpec((tf, d), lambda j: (j, 0))],
        out_specs=[chunk, chunk, pl.BlockSpec((2, d, tf), lambda j: (0, 0, j)),
                   pl.BlockSpec((tf, d), lambda j: (j, 0)), _full((1, d))],
        out_shape=[jax.ShapeDtypeStruct((s, f), BF16), jax.ShapeDtypeStruct((s, f), BF16),
                   jax.ShapeDtypeStruct((2, d, f), BF16), jax.ShapeDtypeStruct((f, d), BF16),
                   jax.ShapeDtypeStruct((1, d), F32)],
        scratch_shapes=[pltpu.VMEM((s, d), BF16)],
        compiler_params=_cp(dimension_semantics=("arbitrary",)),
    )(dxo, o, gate, h, a, b, wo)


def _ffn_bwd_in(dza, dzb, wi, x, dxo, mod, g):
    s, d = x.shape
    f = dza.shape[1]
    tf, tm = FFN_TF_WIDE, min(s, FFN_TM_WIDE)
    nf, nt = f // tf, s // tm

    def body(dza_ref, dzb_ref, wa_ref, wb_ref, x_ref, dxo_ref, mod_ref, g_ref, dx_ref, rows_ref, acc):
        i, j = pl.program_id(0), pl.program_id(1)

        @pl.when(jnp.logical_and(i == 0, j == 0))
        def _():
            rows_ref[...] = jnp.zeros_like(rows_ref)

        @pl.when(j == 0)
        def _():
            acc[...] = jnp.zeros_like(acc)

        acc[...] += (lax.dot_general(dza_ref[...], wa_ref[...], (((1,), (1,)), ((), ())), preferred_element_type=F32)
                     + lax.dot_general(dzb_ref[...], wb_ref[...], (((1,), (1,)), ((), ())), preferred_element_type=F32))

        @pl.when(j == nf - 1)
        def _():
            dx, dsh, dsc, dg = _norm_mod_bwd(x_ref[...], g_ref[...], mod_ref[1:2, :], acc[...])
            dx_ref[...] = dx + dxo_ref[...]
            rows_ref[0:1, :] += dsh
            rows_ref[1:2, :] += dsc
            rows_ref[2:3, :] += dg

    row = pl.BlockSpec((tm, d), lambda i, j: (i, 0))
    chunk = pl.BlockSpec((tm, tf), lambda i, j: (i, j))
    return pl.pallas_call(
        body, name="ffn_bwd_in", grid=(nt, nf),
        in_specs=[chunk, chunk,
                  pl.BlockSpec((None, d, tf), lambda i, j: (0, 0, j)),
                  pl.BlockSpec((None, d, tf), lambda i, j: (1, 0, j)),
                  row, row, _full((3, d)), _full((1, d))],
        out_specs=[row, _full((8, d))],
        out_shape=[jax.ShapeDtypeStruct((s, d), F32), jax.ShapeDtypeStruct((8, d), F32)],
        scratch_shapes=[pltpu.VMEM((tm, d), F32)],
        compiler_params=_cp(dimension_semantics=("arbitrary", "arbitrary")),
    )(dza, dzb, wi, wi, x, dxo, mod, g)


def _mix_in_fwd(x, mod, g, wmit):
    s, d = x.shape
    p = wmit.shape[0]
    tm = _tm(s)

    def body(x_ref, mod_ref, g_ref, w_ref, z_ref, h_ref):
        hh = _norm_mod(x_ref[...], g_ref[...], mod_ref[0:1, :], mod_ref[1:2, :]).astype(BF16)
        h_ref[...] = hh
        z_ref[...] = lax.dot_general(hh, w_ref[...], (((1,), (1,)), ((), ())), preferred_element_type=F32)

    row = pl.BlockSpec((tm, d), lambda i: (i, 0))
    return pl.pallas_call(
        body, name="mix_in_fwd", grid=(s // tm,),
        in_specs=[row, _full((3, d)), _full((1, d)), _full((p, d))],
        out_specs=[pl.BlockSpec((tm, p), lambda i: (i, 0)), row],
        out_shape=[jax.ShapeDtypeStruct((s, p), F32), jax.ShapeDtypeStruct((s, d), BF16)],
        compiler_params=_cp(dimension_semantics=("parallel",)),
    )(x, mod, g, wmit)


def _mix_in_bwd(dzs, h, wmit, x, dxo, mod, g):
    s, d = x.shape
    p = wmit.shape[0]
    tm = min(s, 512)
    nt = s // tm

    def body(za_ref, zb_ref, zc_ref, zd_ref, h_ref, w_ref, x_ref, dxo_ref, mod_ref, g_ref,
             dx_ref, rows_ref, dw_ref, acc):
        i = pl.program_id(0)

        @pl.when(i == 0)
        def _():
            rows_ref[...] = jnp.zeros_like(rows_ref)
            acc[...] = jnp.zeros_like(acc)

        dz = jnp.concatenate([za_ref[...], zb_ref[...], zc_ref[...], zd_ref[...]], axis=1).astype(BF16)
        acc[...] += _dot_tn(dz, h_ref[...])
        dh = jnp.dot(dz, w_ref[...], preferred_element_type=F32)
        dx, dsh, dsc, dg = _norm_mod_bwd(x_ref[...], g_ref[...], mod_ref[1:2, :], dh)
        dx_ref[...] = dx + dxo_ref[...]
        rows_ref[0:1, :] += dsh
        rows_ref[1:2, :] += dsc
        rows_ref[2:3, :] += dg

        @pl.when(i == nt - 1)
        def _():
            dw_ref[...] = acc[...].astype(BF16)

    row = pl.BlockSpec((tm, d), lambda i: (i, 0))
    zspecs = [pl.BlockSpec((tm, z.shape[1]), lambda i: (i, 0)) for z in dzs]
    return pl.pallas_call(
        body, name="mix_in_bwd", grid=(nt,),
        in_specs=zspecs + [row, _full((p, d)), row, row, _full((3, d)), _full((1, d))],
        out_specs=[row, _full((8, d)), _full((p, d))],
        out_shape=[jax.ShapeDtypeStruct((s, d), F32), jax.ShapeDtypeStruct((8, d), F32),
                   jax.ShapeDtypeStruct((p, d), BF16)],
        scratch_shapes=[pltpu.VMEM((p, d), F32)],
        compiler_params=_cp(dimension_semantics=("arbitrary",)),
    )(*dzs, h, wmit, x, dxo, mod, g)


def _group_norm(ys, mng):
    outs, hats, rs = [], [], []
    for k, y in enumerate(ys):
        r = lax.rsqrt(jnp.mean(y * y, axis=-1, keepdims=True) + EPS)
        yh = y * r
        hats.append(yh)
        rs.append(r)
        outs.append(yh * mng[:, k * W_GRP:(k + 1) * W_GRP])
    return jnp.concatenate(outs, axis=1), hats, rs


def _mix_out_fwd(ys, mng, wmo, x, gate):
    s, d = x.shape
    tm = _tm(s)

    def body(ya, yb, yc, yd, mng_ref, w_ref, x_ref, gate_ref, xn_ref, m_ref):
        yn, _, _ = _group_norm([ya[...], yb[...], yc[...], yd[...]], mng_ref[...])
        m = jnp.dot(yn.astype(BF16), w_ref[...], preferred_element_type=F32)
        m_ref[...] = m
        xn_ref[...] = x_ref[...] + gate_ref[...] * m

    row = pl.BlockSpec((tm, d), lambda i: (i, 0))
    grp = pl.BlockSpec((tm, W_GRP), lambda i: (i, 0))
    return pl.pallas_call(
        body, name="mix_out_fwd", grid=(s // tm,),
        in_specs=[grp, grp, grp, grp, _full((1, d)), _full((d, d)), row, _full((1, d))],
        out_specs=[row, row],
        out_shape=[jax.ShapeDtypeStruct((s, d), F32), jax.ShapeDtypeStruct((s, d), F32)],
        compiler_params=_cp(dimension_semantics=("parallel",)),
    )(*ys, mng, wmo, x, gate)


def _mix_out_bwd(dxo, m, gate, ys, mng, wmo):
    s, d = dxo.shape
    tm = min(s, 512)
    nt = s // tm

    def body(dxo_ref, m_ref, gate_ref, ya, yb, yc, yd, mng_ref, w_ref,
             dya, dyb, dyc, dyd, dgate_ref, dmng_ref, dw_ref, acc):
        i = pl.program_id(0)

        @pl.when(i == 0)
        def _():
            dgate_ref[...] = jnp.zeros_like(dgate_ref)
            dmng_ref[...] = jnp.zeros_like(dmng_ref)
            acc[...] = jnp.zeros_like(acc)

        dxv = dxo_ref[...]
        dgate_ref[...] += jnp.sum(m_ref[...] * dxv, axis=0, keepdims=True)
        dm = (gate_ref[...] * dxv).astype(BF16)
        mng = mng_ref[...]
        yn, hats, rs = _group_norm([ya[...], yb[...], yc[...], yd[...]], mng)
        acc[...] += _dot_tn(yn, dm)
        dyn = lax.dot_general(dm, w_ref[...], (((1,), (1,)), ((), ())), preferred_element_type=F32)
        dmng_parts = []
        for k, (yh, r, out) in enumerate(zip(hats, rs, (dya, dyb, dyc, dyd))):
            dk = dyn[:, k * W_GRP:(k + 1) * W_GRP]
            dmng_parts.append(jnp.sum(dk * yh, axis=0, keepdims=True))
            dyh = dk * mng[:, k * W_GRP:(k + 1) * W_GRP]
            out[...] = r * (dyh - yh * jnp.mean(dyh * yh, axis=-1, keepdims=True))
        dmng_ref[...] += jnp.concatenate(dmng_parts, axis=1)

        @pl.when(i == nt - 1)
        def _():
            dw_ref[...] = acc[...].astype(BF16)

    row = pl.BlockSpec((tm, d), lambda i: (i, 0))
    grp = pl.BlockSpec((tm, W_GRP), lambda i: (i, 0))
    return pl.pallas_call(
        body, name="mix_out_bwd", grid=(nt,),
        in_specs=[row, row, _full((1, d)), grp, grp, grp, grp, _full((1, d)), _full((d, d))],
        out_specs=[grp, grp, grp, grp, _full((1, d)), _full((1, d)), _full((d, d))],
        out_shape=[jax.ShapeDtypeStruct((s, W_GRP), F32)] * 4
        + [jax.ShapeDtypeStruct((1, d), F32), jax.ShapeDtypeStruct((1, d), F32), jax.ShapeDtypeStruct((d, d), BF16)],
        scratch_shapes=[pltpu.VMEM((d, d), F32)],
        compiler_params=_cp(dimension_semantics=("arbitrary",)),
    )(dxo, m, gate, *ys, mng, wmo)


def _sgu_consts():
    r = _iota((W_GRP, W_GRP), 0) >> 6
    c = _iota((W_GRP, W_GRP), 1) >> 6
    avg = jnp.where(r == c, 1.0 / 64.0, 0.0).astype(F32)
    tril = _iota((CHUNK, CHUNK), 0) >= _iota((CHUNK, CHUNK), 1)
    head = _iota((CHUNK, W_GRP), 1) >> 6
    return avg, tril, head


def _sgu_pre(za, avg):
    zg, t = _gelu(za)
    u, v = zg[:, :W_GRP], zg[:, W_GRP:]
    mu = _dot_hi(v, avg)
    vc = v - mu
    r = lax.rsqrt(_dot_hi(vc * vc, avg) + EPS)
    return t, u, vc * r, r


def _sgu_fwd(z, sgu_w, bias_full):
    s = z.shape[0]
    tm = min(s, 512)

    def body(za_ref, w_ref, bias_ref, ya_ref):
        avg, tril, head = _sgu_consts()
        _, u, vn, _ = _sgu_pre(za_ref[...], avg)
        wm = [jnp.where(tril, w_ref[h], 0.0).astype(BF16) for h in range(4)]
        vb = vn.astype(BF16)
        for n in range(tm // CHUNK):
            rows = slice(n * CHUNK, (n + 1) * CHUNK)
            mixed = bias_ref[...]
            for h in range(4):
                mixed = mixed + jnp.where(head == h, jnp.dot(wm[h], vb[rows], preferred_element_type=F32), 0.0)
            ya_ref[rows, :] = u[rows] * mixed

    return pl.pallas_call(
        body, name="sgu_fwd", grid=(s // tm,),
        in_specs=[pl.BlockSpec((tm, 2 * W_GRP), lambda i: (i, 0)), _full((4, CHUNK, CHUNK)), _full((CHUNK, W_GRP))],
        out_specs=pl.BlockSpec((tm, W_GRP), lambda i: (i, 0)),
        out_shape=jax.ShapeDtypeStruct((s, W_GRP), F32),
        compiler_params=_cp(dimension_semantics=("parallel",)),
    )(z, sgu_w, bias_full)


def _sgu_bwd(z, dya, sgu_w, bias_full):
    s = z.shape[0]
    tm = min(s, 512)
    nt = s // tm

    def body(za_ref, dya_ref, w_ref, bias_ref, dza_ref, dw_ref, db_ref, du_s, dvn_s):
        i = pl.program_id(0)

        @pl.when(i == 0)
        def _():
            dw_ref[...] = jnp.zeros_like(dw_ref)
            db_ref[...] = jnp.zeros_like(db_ref)

        avg, tril, head = _sgu_consts()
        za = za_ref[...]
        t, u, vn, r = _sgu_pre(za, avg)
        wm = [jnp.where(tril, w_ref[h], 0.0).astype(BF16) for h in range(4)]
        vb = vn.astype(BF16)
        dya = dya_ref[...]
        dw = [jnp.zeros((CHUNK, CHUNK), F32) for _ in range(4)]
        db = jnp.zeros((CHUNK, W_GRP), F32)
        for n in range(tm // CHUNK):
            rows = slice(n * CHUNK, (n + 1) * CHUNK)
            mixed = bias_ref[...]
            for h in range(4):
                mixed = mixed + jnp.where(head == h, jnp.dot(wm[h], vb[rows], preferred_element_type=F32), 0.0)
            dmix = dya[rows] * u[rows]
            du_s[rows, :] = dya[rows] * mixed
            db = db + dmix
            dmb = dmix.astype(BF16)
            dvn = jnp.zeros((CHUNK, W_GRP), F32)
            for h in range(4):
                dmh = jnp.where(head == h, dmix, 0.0)
                dw[h] = dw[h] + _dot_nt(dmh, vb[rows])
                dvn = dvn + jnp.where(head == h, _dot_tn(wm[h], dmb), 0.0)
            dvn_s[rows, :] = dvn
        for h in range(4):
            dw_ref[h] += jnp.where(tril, dw[h], 0.0)
        sel = ((_iota((W_GRP, CHUNK), 0) >> 6) == _iota((W_GRP, CHUNK), 1)).astype(F32)
        db_ref[...] += _dot_hi(db, sel)
        dvn = dvn_s[...]
        dv = r * (dvn - _dot_hi(dvn, avg) - vn * _dot_hi(dvn * vn, avg))
        dzg = jnp.concatenate([du_s[...], dv], axis=1)
        dza_ref[...] = dzg * _gelu_grad(za, t)

    return pl.pallas_call(
        body, name="sgu_bwd", grid=(nt,),
        in_specs=[pl.BlockSpec((tm, 2 * W_GRP), lambda i: (i, 0)), pl.BlockSpec((tm, W_GRP), lambda i: (i, 0)),
                  _full((4, CHUNK, CHUNK)), _full((CHUNK, W_GRP))],
        out_specs=[pl.BlockSpec((tm, 2 * W_GRP), lambda i: (i, 0)), _full((4, CHUNK, CHUNK)), _full((CHUNK, CHUNK))],
        out_shape=[jax.ShapeDtypeStruct((s, 2 * W_GRP), F32), jax.ShapeDtypeStruct((4, CHUNK, CHUNK), F32),
                   jax.ShapeDtypeStruct((CHUNK, CHUNK), F32)],
        scratch_shapes=[pltpu.VMEM((tm, W_GRP), F32), pltpu.VMEM((tm, W_GRP), F32)],
        compiler_params=_cp(dimension_semantics=("arbitrary",)),
    )(z, dya, sgu_w, bias_full)


def _shift_down(x, k):
    return jnp.where(_iota(x.shape, 0) < k, 0.0, pltpu.roll(x, k, 0))


def _shift_up(x, k):
    n = x.shape[0]
    return jnp.where(_iota(x.shape, 0) >= n - k, 0.0, pltpu.roll(x, n - k, 0))


def _by_pool_group(shape, v2, v4, v8, v16):
    col = _iota(shape, 1)
    return jnp.where(col < 64, v2, jnp.where(col < 128, v4, jnp.where(col < 192, v8, v16)))


def _pool_core(zb, pw2):
    s2 = zb + _shift_down(zb, 1)
    s4 = s2 + _shift_down(s2, 2)
    s8 = s4 + _shift_down(s4, 4)
    s16 = s8 + _shift_down(s8, 8)
    win = _by_pool_group(zb.shape, s2, s4, s8, s16)
    wlen = _by_pool_group(zb.shape, 2.0, 4.0, 8.0, 16.0)
    cnt = jnp.minimum((_iota(zb.shape, 0) + 1).astype(F32), wlen)
    p = win / cnt - zb
    wt = jnp.tile(pw2, (1, 4))
    wbd = jnp.where((_iota(wt.shape, 0) >> 6) == (_iota(wt.shape, 1) >> 6), wt, 0.0).astype(BF16)
    return p, cnt, wbd


def _conv_core(zc, cw):
    bg, cg, xh = zc[:, :W_GRP], zc[:, W_GRP:2 * W_GRP], zc[:, 2 * W_GRP:]
    y = cg * xh
    y1, y2 = _shift_down(y, 1), _shift_down(y, 2)
    out = cw[2:3, :] * y + cw[1:2, :] * y1 + cw[0:1, :] * y2
    return bg, cg, xh, y, y1, y2, out


def _poolconv_fwd(z, pw2, pscale, cw):
    s = z.shape[0]

    def body(zb_ref, zc_ref, pw_ref, ps_ref, cw_ref, yb_ref, yc_ref):
        p, _, wbd = _pool_core(zb_ref[...], pw_ref[...])
        yb_ref[...] = jnp.dot(p.astype(BF16), wbd, preferred_element_type=F32) * ps_ref[...]
        bg, _, _, _, _, _, out = _conv_core(zc_ref[...], cw_ref[...])
        yc_ref[...] = bg * out

    return pl.pallas_call(
        body, name="poolconv_fwd", grid=(1,),
        in_specs=[pl.BlockSpec((s, W_GRP), lambda i: (0, 2)), pl.BlockSpec((s, 3 * W_GRP), lambda i: (0, 1)),
                  _full((W_GRP, 64)), _full((1, W_GRP)), _full((3, W_GRP))],
        out_specs=[_full((s, W_GRP)), _full((s, W_GRP))],
        out_shape=[jax.ShapeDtypeStruct((s, W_GRP), F32)] * 2,
        compiler_params=_cp(dimension_semantics=("arbitrary",)),
    )(z, z, pw2, pscale, cw)


def _poolconv_bwd(z, dyb, dyc, pw2, pscale, cw):
    s = z.shape[0]

    def body(zb_ref, zc_ref, dyb_ref, dyc_ref, pw_ref, ps_ref, cw_ref, dzb_ref, dzc_ref, dw_ref, dps_ref, dcw_ref):
        zb = zb_ref[...]
        p, cnt, wbd = _pool_core(zb, pw_ref[...])
        pb = p.astype(BF16)
        out = jnp.dot(pb, wbd, preferred_element_type=F32)
        dyb = dyb_ref[...]
        dps_ref[...] = jnp.sum(dyb * out, axis=0, keepdims=True)
        dout = (dyb * ps_ref[...]).astype(BF16)
        dw = _dot_tn(pb, dout)
        dw_ref[...] = jnp.where((_iota(dw.shape, 0) >> 6) == (_iota(dw.shape, 1) >> 6), dw, 0.0)
        dp = lax.dot_general(dout, wbd, (((1,), (1,)), ((), ())), preferred_element_type=F32)
        dwin = dp / cnt
        t2 = dwin + _shift_up(dwin, 1)
        t4 = t2 + _shift_up(t2, 2)
        t8 = t4 + _shift_up(t4, 4)
        t16 = t8 + _shift_up(t8, 8)
        dzb_ref[...] = _by_pool_group(zb.shape, t2, t4, t8, t16) - dp

        cw = cw_ref[...]
        bg, cg, xh, y, y1, y2, out = _conv_core(zc_ref[...], cw)
        dyc = dyc_ref[...]
        dout = dyc * bg
        dcw_ref[...] = jnp.zeros_like(dcw_ref)
        dcw_ref[0:1, :] = jnp.sum(dout * y2, axis=0, keepdims=True)
        dcw_ref[1:2, :] = jnp.sum(dout * y1, axis=0, keepdims=True)
        dcw_ref[2:3, :] = jnp.sum(dout * y, axis=0, keepdims=True)
        dy = cw[2:3, :] * dout + cw[1:2, :] * _shift_up(dout, 1) + cw[0:1, :] * _shift_up(dout, 2)
        dzc_ref[...] = jnp.concatenate([dyc * out, dy * xh, dy * cg], axis=1)

    return pl.pallas_call(
        body, name="poolconv_bwd", grid=(1,),
        in_specs=[pl.BlockSpec((s, W_GRP), lambda i: (0, 2)), pl.BlockSpec((s, 3 * W_GRP), lambda i: (0, 1)),
                  _full((s, W_GRP)), _full((s, W_GRP)), _full((W_GRP, 64)), _full((1, W_GRP)), _full((3, W_GRP))],
        out_specs=[_full((s, W_GRP)), _full((s, 3 * W_GRP)), _full((W_GRP, W_GRP)), _full((1, W_GRP)), _full((8, W_GRP))],
        out_shape=[jax.ShapeDtypeStruct((s, W_GRP), F32), jax.ShapeDtypeStruct((s, 3 * W_GRP), F32),
                   jax.ShapeDtypeStruct((W_GRP, W_GRP), F32), jax.ShapeDtypeStruct((1, W_GRP), F32),
                   jax.ShapeDtypeStruct((8, W_GRP), F32)],
        compiler_params=_cp(dimension_semantics=("arbitrary",)),
    )(z, z, dyb, dyc, pw2, pscale, cw)


N_STATE = 1024
HALF_STATE = N_STATE // 2
HALF_CH = W_GRP // 2
N_SLAB = HALF_STATE // LANES


def _s5_disc(lre, lim, ldt):
    dt = jnp.exp(ldt)
    mag = jnp.exp(lre * dt)
    ang = lim * dt
    ar, ai = mag * jnp.cos(ang), mag * jnp.sin(ang)
    nr, ni = ar - 1.0, ai
    den = lre * lre + lim * lim
    kr = (nr * lre + ni * lim) / den
    ki = (ni * lre - nr * lim) / den
    return ar, ai, kr, ki


def _s5_mats(colp, br, bi, cr, ci):
    _, _, kr, ki = _s5_disc(colp[:, 0:1], colp[:, 1:2], colp[:, 2:3])
    bbr = kr * br - ki * bi
    bbi = kr * bi + ki * br
    bmask = (_iota((HALF_STATE, HALF_CH), 0) >> 6) == (_iota((HALF_STATE, HALF_CH), 1) >> 4)
    cmask = (_iota((HALF_CH, HALF_STATE), 0) >> 4) == (_iota((HALF_CH, HALF_STATE), 1) >> 6)
    btr = jnp.where(bmask, jnp.tile(bbr, (1, 8)), 0.0).astype(BF16)
    bti = jnp.where(bmask, jnp.tile(bbi, (1, 8)), 0.0).astype(BF16)
    ctr = jnp.where(cmask, jnp.tile(cr, (1, 8)), 0.0).astype(BF16)
    cti = jnp.where(cmask, jnp.tile(ci, (1, 8)), 0.0).astype(BF16)
    return kr, ki, btr, bti, ctr, cti, bmask, cmask


def _slab(q):
    return slice(q * LANES, (q + 1) * LANES)


def _cmul(ar, ai, br, bi):
    return ar * br - ai * bi, ar * bi + ai * br


def _sub_shift(x, k, up):
    row = _iota(x.shape, 0)
    if up:
        return jnp.where(row >= N_SEG - k, 0.0, pltpu.roll(x, N_SEG - k, 0))
    return jnp.where(row < k, 0.0, pltpu.roll(x, k, 0))


def _seg_rows(j):
    return pl.ds(pl.multiple_of(j * N_SEG, N_SEG), N_SEG)


def _interleave(src, dst, seg):
    def step(j, carry):
        dst[_seg_rows(j), :] = src[pl.ds(j, N_SEG, stride=seg), :]
        return carry
    lax.fori_loop(0, seg, step, 0)


def _deinterleave(src, dst, seg):
    def step(j, carry):
        dst[pl.ds(j, N_SEG, stride=seg), :] = src[_seg_rows(j), :]
        return carry
    lax.fori_loop(0, seg, step, 0)


def _scan(xr, xi, ar_row, ai_row, seg, reverse):
    nlog = int(math.log2(seg))
    assert (1 << nlog) == seg
    for q0 in range(0, N_SLAB, 4):
        qs = list(range(q0, q0 + 4))
        aq = [(jnp.broadcast_to(ar_row[:, _slab(q)], (N_SEG, LANES)),
               jnp.broadcast_to(ai_row[:, _slab(q)], (N_SEG, LANES))) for q in qs]
        zero = jnp.zeros((N_SEG, LANES), F32)

        def local(jj, carry, qs=qs, aq=aq):
            j = seg - 1 - jj if reverse else jj
            out = []
            for n, q in enumerate(qs):
                rows = _seg_rows(j)
                pr, pi = _cmul(aq[n][0], aq[n][1], carry[2 * n], carry[2 * n + 1])
                nr = pr + xr[q, rows, :]
                ni = pi + xi[q, rows, :]
                xr[q, rows, :] = nr
                xi[q, rows, :] = ni
                out += [nr, ni]
            return tuple(out)

        fin = lax.fori_loop(0, seg, local, (zero,) * 8)
        cins = []
        for n in range(4):
            er, ei = fin[2 * n], fin[2 * n + 1]
            pr, pi = aq[n]
            for _ in range(nlog):
                pr, pi = _cmul(pr, pi, pr, pi)
            yr, yi = er, ei
            for k in (1, 2, 4):
                sr, si = _cmul(pr, pi, _sub_shift(yr, k, reverse), _sub_shift(yi, k, reverse))
                yr, yi = yr + sr, yi + si
                pr, pi = _cmul(pr, pi, pr, pi)
            cins.append((_sub_shift(yr, 1, reverse), _sub_shift(yi, 1, reverse)))

        def fix(jj, carry, qs=qs, aq=aq, cins=cins):
            j = seg - 1 - jj if reverse else jj
            out = []
            for n, q in enumerate(qs):
                rows = _seg_rows(j)
                pwr, pwi = carry[2 * n], carry[2 * n + 1]
                cr, ci = _cmul(pwr, pwi, cins[n][0], cins[n][1])
                xr[q, rows, :] += cr
                xi[q, rows, :] += ci
                nr, ni = _cmul(pwr, pwi, aq[n][0], aq[n][1])
                out += [nr, ni]
            return tuple(out)

        lax.fori_loop(0, seg, fix, tuple(v for pair in aq for v in pair))


def _s5_forward_states(u, btr, bti, ar_row, ai_row, xr, xi, seg):
    ub = u.astype(BF16)
    for q in range(N_SLAB):
        xr[q] = _dot_nt(ub, btr[_slab(q), :])
        xi[q] = _dot_nt(ub, bti[_slab(q), :])
    _scan(xr, xi, ar_row, ai_row, seg, False)


def _s5_readout(u, xr, xi, ctr, cti, d):
    y = d * u
    for q in range(N_SLAB):
        y = y + _dot_nt(xr[q], ctr[:, _slab(q)]) - _dot_nt(xi[q], cti[:, _slab(q)])
    return y


def _s5_param_specs():
    return [pl.BlockSpec((3, HALF_STATE), lambda i: (0, i)), pl.BlockSpec((HALF_STATE, 3), lambda i: (i, 0)),
            pl.BlockSpec((HALF_STATE, 16), lambda i: (i, 0)), pl.BlockSpec((HALF_STATE, 16), lambda i: (i, 0)),
            pl.BlockSpec((HALF_CH, 64), lambda i: (i, 0)), pl.BlockSpec((HALF_CH, 64), lambda i: (i, 0)),
            pl.BlockSpec((1, HALF_CH), lambda i: (0, i))]


def _s5_core_fwd(z, sp):
    s = z.shape[0]
    seg = s // N_SEG

    def body(u_ref, rowp, colp, br, bi, cr, ci, d_ref, y_ref, xr, xi, us, ys):
        ar, ai, _, _ = _s5_disc(rowp[0:1, :], rowp[1:2, :], rowp[2:3, :])
        _, _, btr, bti, ctr, cti, _, _ = _s5_mats(colp[...], br[...], bi[...], cr[...], ci[...])
        _interleave(u_ref, us, seg)
        u = us[...]
        _s5_forward_states(u, btr, bti, ar, ai, xr, xi, seg)
        ys[...] = _s5_readout(u, xr, xi, ctr, cti, d_ref[...])
        _deinterleave(ys, y_ref, seg)

    return pl.pallas_call(
        body, name="s5_core_fwd", grid=(2,),
        in_specs=[pl.BlockSpec((s, HALF_CH), lambda i: (0, 12 + i))] + _s5_param_specs(),
        out_specs=pl.BlockSpec((s, HALF_CH), lambda i: (0, i)),
        out_shape=jax.ShapeDtypeStruct((s, W_GRP), F32),
        scratch_shapes=[pltpu.VMEM((N_SLAB, s, LANES), F32)] * 2 + [pltpu.VMEM((s, HALF_CH), F32)] * 2,
        compiler_params=_cp(dimension_semantics=("parallel",)),
    )(z, *sp)


def _s5_glu_fwd(y, gw, gb):
    s = y.shape[0]
    tm = _tm(s)

    def body(y_ref, gw_ref, gb_ref, o_ref):
        yg, _ = _gelu(y_ref[...])
        o_ref[...] = yg * jax.nn.sigmoid(_dot(yg, gw_ref[...]) + gb_ref[...])

    blk = pl.BlockSpec((tm, W_GRP), lambda i: (i, 0))
    return pl.pallas_call(
        body, name="s5_glu_fwd", grid=(s // tm,),
        in_specs=[blk, _full((W_GRP, W_GRP)), _full((1, W_GRP))], out_specs=blk,
        out_shape=jax.ShapeDtypeStruct((s, W_GRP), F32),
        compiler_params=_cp(dimension_semantics=("parallel",)),
    )(y, gw, gb)


def _s5_glu_bwd(y, dyd, gw, gb):
    s = y.shape[0]
    tm = _tm(s)

    def body(y_ref, dyd_ref, gw_ref, gb_ref, dy_ref, dgw_ref, dgb_ref):
        i = pl.program_id(0)

        @pl.when(i == 0)
        def _():
            dgw_ref[...] = jnp.zeros_like(dgw_ref)
            dgb_ref[...] = jnp.zeros_like(dgb_ref)

        y, gw, dyd = y_ref[...], gw_ref[...], dyd_ref[...]
        yg, t = _gelu(y)
        gate = jax.nn.sigmoid(_dot(yg, gw) + gb_ref[...])
        dlin = dyd * yg * gate * (1.0 - gate)
        dgw_ref[...] += _dot_tn(yg, dlin)
        dgb_ref[...] += jnp.sum(dlin, axis=0, keepdims=True)
        dy_ref[...] = (dyd * gate + _dot_nt(dlin, gw)) * _gelu_grad(y, t)

    blk = pl.BlockSpec((tm, W_GRP), lambda i: (i, 0))
    return pl.pallas_call(
        body, name="s5_glu_bwd", grid=(s // tm,),
        in_specs=[blk, blk, _full((W_GRP, W_GRP)), _full((1, W_GRP))],
        out_specs=[blk, _full((W_GRP, W_GRP)), _full((1, W_GRP))],
        out_shape=[jax.ShapeDtypeStruct((s, W_GRP), F32), jax.ShapeDtypeStruct((W_GRP, W_GRP), F32),
                   jax.ShapeDtypeStruct((1, W_GRP), F32)],
        compiler_params=_cp(dimension_semantics=("arbitrary",)),
    )(y, dyd, gw, gb)


def _s5_core_bwd(z, dy, sp):
    s = z.shape[0]
    seg = s // N_SEG

    def body(u_ref, dy_ref, rowp, colp, br_ref, bi_ref, cr_ref, ci_ref, d_ref,
             du_ref, dbr_ref, dbi_ref, dcr_ref, dci_ref, dd_ref, da_ref, dk_ref,
             xr, xi, gr, gi, us, dys):
        ar, ai, _, _ = _s5_disc(rowp[0:1, :], rowp[1:2, :], rowp[2:3, :])
        br, bi = br_ref[...], bi_ref[...]
        kr, ki, btr, bti, ctr, cti, bmask, cmask = _s5_mats(colp[...], br, bi, cr_ref[...], ci_ref[...])
        _interleave(u_ref, us, seg)
        _interleave(dy_ref, dys, seg)
        u = us[...]
        d = d_ref[...]
        _s5_forward_states(u, btr, bti, ar, ai, xr, xi, seg)

        dy = dys[...]
        dd_ref[...] = jnp.sum(dy * u, axis=0, keepdims=True)
        du = d * dy
        dyb = dy.astype(BF16)
        dctr, dcti = [], []
        for q in range(N_SLAB):
            gr[q] = jnp.dot(dyb, ctr[:, _slab(q)], preferred_element_type=F32)
            gi[q] = -jnp.dot(dyb, cti[:, _slab(q)], preferred_element_type=F32)
            dctr.append(_dot_tn(dyb, xr[q]))
            dcti.append(-_dot_tn(dyb, xi[q]))
        selp = ((_iota((HALF_STATE, 64), 0) & 63) == _iota((HALF_STATE, 64), 1)).astype(F32)
        dcr_ref[...] = _dot_hi(jnp.where(cmask, jnp.concatenate(dctr, axis=1), 0.0), selp)
        dci_ref[...] = _dot_hi(jnp.where(cmask, jnp.concatenate(dcti, axis=1), 0.0), selp)

        _scan(gr, gi, ar, -ai, seg, True)

        dar, dai = [], []
        for q in range(N_SLAB):
            def acc_step(j, carry, q=q):
                rows, prev = _seg_rows(j), _seg_rows(j - 1)
                g_r, g_i, p_r, p_i = gr[q, rows, :], gi[q, rows, :], xr[q, prev, :], xi[q, prev, :]
                return carry[0] + g_r * p_r + g_i * p_i, carry[1] - g_r * p_i + g_i * p_r
            first, last = _seg_rows(0), _seg_rows(seg - 1)
            p_r, p_i = _sub_shift(xr[q, last, :], 1, False), _sub_shift(xi[q, last, :], 1, False)
            g_r, g_i = gr[q, first, :], gi[q, first, :]
            s_r, s_i = lax.fori_loop(1, seg, acc_step, (g_r * p_r + g_i * p_i, -g_r * p_i + g_i * p_r))
            dar.append(jnp.sum(s_r, axis=0, keepdims=True))
            dai.append(jnp.sum(s_i, axis=0, keepdims=True))
        da_ref[...] = jnp.zeros_like(da_ref)
        da_ref[0:1, :] = jnp.concatenate(dar, axis=1)
        da_ref[1:2, :] = jnp.concatenate(dai, axis=1)

        ub = u.astype(BF16)
        dbtr, dbti = [], []
        for q in range(N_SLAB):
            g_r, g_i = gr[q].astype(BF16), gi[q].astype(BF16)
            du = du + jnp.dot(g_r, btr[_slab(q), :], preferred_element_type=F32) \
                + jnp.dot(g_i, bti[_slab(q), :], preferred_element_type=F32)
            dbtr.append(_dot_tn(g_r, ub))
            dbti.append(_dot_tn(g_i, ub))
        us[...] = du
        _deinterleave(us, du_ref, seg)
        selc =((_iota((HALF_CH, 16), 0) & 15) == _iota((HALF_CH, 16), 1)).astype(F32)
        dbbr = _dot_hi(jnp.where(bmask, jnp.concatenate(dbtr, axis=0), 0.0), selc)
        dbbi = _dot_hi(jnp.where(bmask, jnp.concatenate(dbti, axis=0), 0.0), selc)
        dbr_ref[...] = kr * dbbr + ki * dbbi
        dbi_ref[...] = kr * dbbi - ki * dbbr
        dk_ref[:, 0:1] = jnp.sum(dbbr * br + dbbi * bi, axis=1, keepdims=True)
        dk_ref[:, 1:2] = jnp.sum(dbbi * br - dbbr * bi, axis=1, keepdims=True)

    half = pl.BlockSpec((s, HALF_CH), lambda i: (0, i))
    return pl.pallas_call(
        body, name="s5_core_bwd", grid=(2,),
        in_specs=[pl.BlockSpec((s, HALF_CH), lambda i: (0, 12 + i)), half] + _s5_param_specs(),
        out_specs=[half, pl.BlockSpec((HALF_STATE, 16), lambda i: (i, 0)), pl.BlockSpec((HALF_STATE, 16), lambda i: (i, 0)),
                   pl.BlockSpec((HALF_CH, 64), lambda i: (i, 0)), pl.BlockSpec((HALF_CH, 64), lambda i: (i, 0)),
                   pl.BlockSpec((1, HALF_CH), lambda i: (0, i)), pl.BlockSpec((8, HALF_STATE), lambda i: (0, i)),
                   pl.BlockSpec((HALF_STATE, 2), lambda i: (i, 0))],
        out_shape=[jax.ShapeDtypeStruct((s, W_GRP), F32), jax.ShapeDtypeStruct((N_STATE, 16), F32),
                   jax.ShapeDtypeStruct((N_STATE, 16), F32), jax.ShapeDtypeStruct((W_GRP, 64), F32),
                   jax.ShapeDtypeStruct((W_GRP, 64), F32), jax.ShapeDtypeStruct((1, W_GRP), F32),
                   jax.ShapeDtypeStruct((8, N_STATE), F32), jax.ShapeDtypeStruct((N_STATE, 2), F32)],
        scratch_shapes=[pltpu.VMEM((N_SLAB, s, LANES), F32)] * 4 + [pltpu.VMEM((s, HALF_CH), F32)] * 2,
        compiler_params=_cp(dimension_semantics=("parallel",)),
    )(z, dy, *sp)


def _s5_param_bwd(lre, lim, ldt, da_r, da_i, dk_r, dk_i):
    n = lre.shape[0]

    def body(lre_ref, lim_ref, ldt_ref, dar_ref, dai_ref, dkr_ref, dki_ref, o_re, o_im, o_dt):
        lre, lim, ldt = lre_ref[...], lim_ref[...], ldt_ref[...]
        dt = jnp.exp(ldt)
        ar, ai, kr, ki = _s5_disc(lre, lim, ldt)
        mag = jnp.exp(lre * dt)
        den = lre * lre + lim * lim
        dkr, dki = dkr_ref[...], dki_ref[...]
        nr, ni = ar - 1.0, ai
        d_ar = dar_ref[...] + (dkr * lre - dki * lim) / den
        d_ai = dai_ref[...] + (dkr * lim + dki * lre) / den
        kk = (kr * dkr + ki * dki) * 2.0 / den
        d_lre = (dkr * nr + dki * ni) / den - kk * lre
        d_lim = (dkr * ni - dki * nr) / den - kk * lim
        d_mag = (d_ar * ar + d_ai * ai) / mag
        d_ang = d_ai * ar - d_ar * ai
        o_re[...] = d_lre + d_mag * mag * dt
        o_im[...] = d_lim + d_ang * dt
        o_dt[...] = jnp.sum((d_mag * mag * lre + d_ang * lim) * dt, axis=1, keepdims=True)

    return pl.pallas_call(
        body, name="s5_param_bwd",
        out_shape=[jax.ShapeDtypeStruct((n, 64), F32), jax.ShapeDtypeStruct((n, 64), F32),
                   jax.ShapeDtypeStruct((n, 1), F32)],
    )(lre, lim, ldt, da_r, da_i, dk_r, dk_i)


def _loss_head(x, fg, target):
    s, d = x.shape
    tm = _tm(s)

    def body(x_ref, fg_ref, t_ref, loss_ref, dx_ref, dfg_ref):
        i = pl.program_id(0)

        @pl.when(i == 0)
        def _():
            loss_ref[...] = jnp.zeros_like(loss_ref)
            dfg_ref[...] = jnp.zeros_like(dfg_ref)

        xv, g = x_ref[...], fg_ref[...]
        r = lax.rsqrt(jnp.mean(xv * xv, axis=-1, keepdims=True) + EPS)
        xh = xv * r
        err = xh * g - t_ref[...]
        loss_ref[...] += 0.5 * jnp.sum(jnp.mean(err * err, axis=-1, keepdims=True), axis=0, keepdims=True)
        dy = err * (1.0 / d)
        dfg_ref[...] += jnp.sum(dy * xh, axis=0, keepdims=True)
        dxh = dy * g
        dx_ref[...] = r * (dxh - xh * jnp.mean(dxh * xh, axis=-1, keepdims=True))

    row = pl.BlockSpec((tm, d), lambda i: (i, 0))
    return pl.pallas_call(
        body, name="loss_head", grid=(s // tm,),
        in_specs=[row, _full((1, d)), row], out_specs=[_full((1, 1)), row, _full((1, d))],
        out_shape=[jax.ShapeDtypeStruct((1, 1), F32), jax.ShapeDtypeStruct((s, d), F32),
                   jax.ShapeDtypeStruct((1, d), F32)],
        compiler_params=_cp(dimension_semantics=("arbitrary",)),
    )(x, fg, target)


ADA_TN = 384


def _cond_fwd(cact, ada_w, ada_b_loc):
    nl, d, n = ada_w.shape

    def body(c_ref, w_ref, b_ref, o_ref):
        o_ref[...] = _dot(c_ref[...], w_ref[...]) + b_ref[...]

    return pl.pallas_call(
        body, name="cond_fwd", grid=(nl, n // ADA_TN),
        in_specs=[_full((N_DEV, d)), pl.BlockSpec((None, d, ADA_TN), lambda l, j: (l, 0, j)),
                  pl.BlockSpec((None, 1, ADA_TN), lambda l, j: (l, 0, j))],
        out_specs=pl.BlockSpec((None, N_DEV, ADA_TN), lambda l, j: (l, 0, j)),
        out_shape=jax.ShapeDtypeStruct((nl, N_DEV, n), F32),
        compiler_params=_cp(dimension_semantics=("parallel", "parallel")),
    )(cact, ada_w, ada_b_loc)


ELEMENTWISE_BLOCK_BYTES = 1 << 20


def _row_tile(r, c, itemsize=4):
    best = None
    for t in range(8, r + 1, 8):
        if r % t == 0 and t * c * itemsize <= ELEMENTWISE_BLOCK_BYTES:
            best = t
    return best if best is not None else r


def _adamw_math(w, g, m, v):
    m = ADAM_B1 * m + (1.0 - ADAM_B1) * g
    v = ADAM_B2 * v + (1.0 - ADAM_B2) * (g * g)
    m_hat = m / (1.0 - ADAM_B1 ** ADAM_STEP)
    v_hat = v / (1.0 - ADAM_B2 ** ADAM_STEP)
    delta = -ADAM_LR * (m_hat / (jnp.sqrt(v_hat) + ADAM_EPS) + ADAM_WD * w)
    return delta, m, v


def _ada_w_update(cact, dcond_loc, w, m, v):
    nl, d, n = w.shape

    def body(c_ref, dc_ref, w_ref, m_ref, v_ref, g_out, d_out, m_out, v_out):
        g = _dot_tn(c_ref[...], dc_ref[...])
        g_out[...] = g
        d_out[...], m_out[...], v_out[...] = _adamw_math(w_ref[...], g, m_ref[...], v_ref[...])

    blk = pl.BlockSpec((None, d, ADA_TN), lambda l, j: (l, 0, j))
    return pl.pallas_call(
        body, name="ada_w_update", grid=(nl, n // ADA_TN),
        in_specs=[_full((N_DEV, d)), pl.BlockSpec((None, N_DEV, ADA_TN), lambda l, j: (l, 0, j)), blk, blk, blk],
        out_specs=[blk] * 4, out_shape=[jax.ShapeDtypeStruct((nl, d, n), F32)] * 4,
        compiler_params=_cp(dimension_semantics=("parallel", "parallel")),
    )(cact, dcond_loc, w, m, v)


def _place():
    x, y, c = lax.axis_index("x"), lax.axis_index("y"), lax.axis_index("c")
    chips = [(1 - x, y), (x, 1 - y), (1 - x, 1 - y)]
    return x, y, c, chips


def _remote(src, dst, send_sem, recv_sem, to):
    return pltpu.make_async_remote_copy(src_ref=src, dst_ref=dst, send_sem=send_sem, recv_sem=recv_sem,
                                        device_id=to, device_id_type=MESH_ID)


def _sems(n):
    return [pltpu.SemaphoreType.DMA((n,)), pltpu.SemaphoreType.DMA((n,))]


def _all_gather8(v, name):
    r, cdim = v.shape

    def body(x_ref, out_ref, stage, send_sems, recv_sems):
        x, y, c, chips = _place()
        sibling = (x, y, 1 - c)

        def slot(px, py, pc):
            return out_ref.at[4 * px + 2 * py + pc]

        first = [_remote(x_ref, slot(x, y, c), send_sems.at[0], recv_sems.at[0], sibling)]
        first += [_remote(x_ref, slot(x, y, c), send_sems.at[1 + j], recv_sems.at[1 + j], (*chip, c))
                  for j, chip in enumerate(chips)]
        for cp in first:
            cp.start()
        pltpu.sync_copy(x_ref, stage)
        pltpu.sync_copy(stage, slot(x, y, c))
        passed = []
        for j, chip in enumerate(chips):
            blk = slot(*chip, c)
            _remote(blk, blk, send_sems.at[1 + j], recv_sems.at[1 + j], (x, y, c)).wait_recv()
            fw = _remote(blk, blk, send_sems.at[4 + j], recv_sems.at[4 + j], sibling)
            fw.start()
            passed.append(fw)
        blk = slot(x, y, 1 - c)
        _remote(blk, blk, send_sems.at[0], recv_sems.at[0], (x, y, c)).wait_recv()
        for j, chip in enumerate(chips):
            blk = slot(*chip, 1 - c)
            _remote(blk, blk, send_sems.at[4 + j], recv_sems.at[4 + j], (x, y, c)).wait_recv()
        for cp in first + passed:
            cp.wait_send()

    return pl.pallas_call(
        body, name=name, out_shape=jax.ShapeDtypeStruct((N_DEV, r, cdim), v.dtype),
        in_specs=[ANY], out_specs=ANY,
        scratch_shapes=[pltpu.VMEM((r, cdim), v.dtype)] + _sems(7),
        compiler_params=_cp(),
    )(v)


def _place_weights(ws, layer, kidx, after):
    steps = 4
    shapes, in_specs, out_specs = [], [], []
    for w, kind in zip(ws, BIG_KINDS):
        _, a, b = w.shape
        in_specs.append(pl.BlockSpec((None, a // steps, b), lambda i, k: (layer, i, 0)))
        if kind == "col":
            shapes.append((2, a, 2 * b))
            out_specs.append(pl.BlockSpec((None, a // steps, b), lambda i, k: (k[0] // 2, i, k[0] % 2)))
        else:
            shapes.append((N_CHIP, a, b))
            out_specs.append(pl.BlockSpec((None, a // steps, b), lambda i, k: (k[0], i, 0)))

    def body(k_ref, *refs):
        outs = refs[len(ws) + 1:]
        for t in range(len(ws)):
            outs[t][...] = refs[t][...].astype(BF16)

    return pl.pallas_call(
        body, name="place_weights", out_shape=[jax.ShapeDtypeStruct(s, BF16) for s in shapes],
        grid_spec=pltpu.PrefetchScalarGridSpec(num_scalar_prefetch=1, grid=(steps,), in_specs=in_specs + [ANY],
                                               out_specs=out_specs),
        compiler_params=_cp(dimension_semantics=("parallel",)),
    )(kidx, *ws, after)


HBM = pl.BlockSpec(memory_space=pltpu.HBM)
SEM = pl.BlockSpec(memory_space=pltpu.SEMAPHORE)
EFFECT = pltpu.SideEffectType.DATAFLOW_SIDE_EFFECTING


def _weight_block(ref, kind, k, h):
    if kind == "col":
        ncol = ref.shape[3] // 2
        return ref.at[k // 2, h, :, pl.ds(pl.multiple_of((k % 2) * ncol, LANES), ncol)]
    return ref.at[k, h]


def _in_hbm(a):
    return pltpu.with_memory_space_constraint(a, pltpu.HBM)


def _weight_send_start(placed, kinds, name):
    nt = len(placed)

    def body(*refs):
        send_sems, recv_sems = refs[nt], refs[nt + 1]
        dst = refs[nt + 2:2 * nt + 2]
        token = refs[2 * nt + 2]
        x, y, c, chips = _place()
        kme = 2 * x + y
        for t in range(nt):
            for j, chip in enumerate(chips):
                own = _weight_block(dst[t], kinds[t], kme, c)
                _remote(own, own, send_sems.at[3 * t + j], recv_sems.at[3 * t + j], (*chip, c)).start()
        token[...] = jnp.zeros_like(token)

    return pl.pallas_call(
        body, name=name,
        out_shape=(pltpu.SemaphoreType.DMA((3 * nt,)), pltpu.SemaphoreType.DMA((3 * nt,)),
                   *[pltpu.HBM(a.shape, a.dtype) for a in placed], jax.ShapeDtypeStruct((8, LANES), F32)),
        in_specs=[HBM] * nt, out_specs=(SEM, SEM, *[HBM] * nt, pl.BlockSpec(memory_space=pltpu.VMEM)),
        input_output_aliases={t: 2 + t for t in range(nt)},
        compiler_params=pltpu.CompilerParams(has_side_effects=EFFECT),
    )(*[_in_hbm(a) for a in placed])


def _weight_send_wait(send_sems, recv_sems, arrays, kinds, after, name):
    nt = len(arrays)

    def body(*refs):
        arr = refs[:nt]
        send_sems, recv_sems = refs[nt], refs[nt + 1]
        x, y, c, chips = _place()
        kme = 2 * x + y
        for t in range(nt):
            for j, chip in enumerate(chips):
                own = _weight_block(arr[t], kinds[t], kme, c)
                got = _weight_block(arr[t], kinds[t], 2 * chip[0] + chip[1], c)
                cp = _remote(own, got, send_sems.at[3 * t + j], recv_sems.at[3 * t + j], (*chip, c))
                cp.wait_send()
                cp.wait_recv()

    return pl.pallas_call(
        body, name=name, out_shape=[pltpu.HBM(a.shape, a.dtype) for a in arrays],
        in_specs=[HBM] * nt + [SEM, SEM, ANY], out_specs=[HBM] * nt,
        input_output_aliases={t: t for t in range(nt)},
        compiler_params=pltpu.CompilerParams(has_side_effects=EFFECT),
    )(*arrays, send_sems, recv_sems, after)


def _forward_copies(kinds):
    def make(refs, send_sems, recv_sems):
        x, y, c, chips = _place()
        cps = []
        for t in range(len(kinds)):
            for j, chip in enumerate(chips):
                blk = _weight_block(refs[t], kinds[t], 2 * chip[0] + chip[1], c)
                cps.append(_remote(blk, blk, send_sems.at[3 * t + j], recv_sems.at[3 * t + j], (x, y, 1 - c)))
        return cps
    return make


def _split_start(name, arrays, n_copies, make_copies):
    na = len(arrays)

    def body(*refs):
        send_sems, recv_sems = refs[na], refs[na + 1]
        for cp in make_copies(refs[na + 2:2 * na + 2], send_sems, recv_sems):
            cp.start()
        token = refs[2 * na + 2]
        token[...] = jnp.zeros_like(token)

    return pl.pallas_call(
        body, name=name,
        out_shape=(pltpu.SemaphoreType.DMA((n_copies,)), pltpu.SemaphoreType.DMA((n_copies,)),
                   *[pltpu.HBM(a.shape, a.dtype) for a in arrays], jax.ShapeDtypeStruct((8, LANES), F32)),
        in_specs=[HBM] * na, out_specs=(SEM, SEM, *[HBM] * na, pl.BlockSpec(memory_space=pltpu.VMEM)),
        input_output_aliases={t: 2 + t for t in range(na)},
        compiler_params=pltpu.CompilerParams(has_side_effects=EFFECT),
    )(*[_in_hbm(a) for a in arrays])


def _split_wait(name, started, make_copies, after):
    send_sems, recv_sems, *arrays, _ = started
    na = len(arrays)

    def body(*refs):
        send_sems, recv_sems = refs[na], refs[na + 1]
        for cp in make_copies(refs[:na], send_sems, recv_sems):
            cp.wait_send()
            cp.wait_recv()

    return pl.pallas_call(
        body, name=name, out_shape=[pltpu.HBM(a.shape, a.dtype) for a in arrays],
        in_specs=[HBM] * na + [SEM, SEM, ANY], out_specs=[HBM] * na,
        input_output_aliases={t: t for t in range(na)},
        compiler_params=pltpu.CompilerParams(has_side_effects=EFFECT),
    )(*arrays, send_sems, recv_sems, after)


def _exchange_copies(nt):
    def make(refs, send_sems, recv_sems):
        x, y, c, _ = _place()
        return [_remote(refs[t].at[:, 1 - c], refs[nt + t], send_sems.at[t], recv_sems.at[t], (x, y, 1 - c))
                for t in range(nt)]
    return make


def _sibling_exchange_start(views, name):
    lands = [lax.empty((v.shape[0],) + v.shape[2:], v.dtype) for v in views]
    return _split_start(name, list(views) + lands, len(views), _exchange_copies(len(views)))


def _sibling_exchange_wait(started, after, name):
    nt = (len(started) - 3) // 2
    outs = _split_wait(name, started, _exchange_copies(nt), after)
    return outs[:nt], outs[nt:]


def _scatter_copies(src, land, kinds, send_sems, recv_sems):
    x, y, c, chips = _place()
    cps = []
    for t in range(len(src)):
        for j, chip in enumerate(chips):
            k = 2 * chip[0] + chip[1]
            if kinds[t] == "col":
                ncol = land[t].shape[2]
                win = src[t].at[k // 2, :, pl.ds(pl.multiple_of((k % 2) * ncol, LANES), ncol)]
            else:
                win = src[t].at[k]
            cps.append(_remote(win, land[t].at[j], send_sems.at[3 * t + j], recv_sems.at[3 * t + j], (*chip, c)))
    return cps


def _chip_scatter_start(parts, kinds, name):
    nt = len(parts)
    shapes = []
    for p, kind in zip(parts, kinds):
        shapes.append((3, p.shape[1], p.shape[2] // 2) if kind == "col" else (3,) + p.shape[1:])

    def body(*refs):
        send_sems, recv_sems = refs[2 * nt], refs[2 * nt + 1]
        src, land = refs[2 * nt + 2:3 * nt + 2], refs[3 * nt + 2:4 * nt + 2]
        token = refs[4 * nt + 2]
        for cp in _scatter_copies(src, land, kinds, send_sems, recv_sems):
            cp.start()
        token[...] = jnp.zeros_like(token)

    lands = [lax.empty(s, BF16) for s in shapes]
    return pl.pallas_call(
        body, name=name,
        out_shape=(pltpu.SemaphoreType.DMA((3 * nt,)), pltpu.SemaphoreType.DMA((3 * nt,)),
                   *[pltpu.HBM(a.shape, a.dtype) for a in parts], *[pltpu.HBM(s, BF16) for s in shapes],
                   jax.ShapeDtypeStruct((8, LANES), F32)),
        in_specs=[HBM] * (2 * nt), out_specs=(SEM, SEM, *[HBM] * (2 * nt), pl.BlockSpec(memory_space=pltpu.VMEM)),
        input_output_aliases={t: 2 + t for t in range(2 * nt)},
        compiler_params=pltpu.CompilerParams(has_side_effects=EFFECT),
    )(*[_in_hbm(a) for a in parts], *[_in_hbm(a) for a in lands])


def _chip_scatter_wait(send_sems, recv_sems, parts, lands, kinds, after, name):
    nt = len(parts)

    def body(*refs):
        src, land = refs[:nt], refs[nt:2 * nt]
        send_sems, recv_sems = refs[2 * nt], refs[2 * nt + 1]
        for cp in _scatter_copies(src, land, kinds, send_sems, recv_sems):
            cp.wait_send()
            cp.wait_recv()

    outs = pl.pallas_call(
        body, name=name, out_shape=[pltpu.HBM(a.shape, a.dtype) for a in list(parts) + list(lands)],
        in_specs=[HBM] * (2 * nt) + [SEM, SEM, ANY], out_specs=[HBM] * (2 * nt),
        input_output_aliases={t: t for t in range(2 * nt)},
        compiler_params=pltpu.CompilerParams(has_side_effects=EFFECT),
    )(*parts, *lands, send_sems, recv_sems, after)
    return outs[:nt], outs[nt:]


def _share_copies(nt):
    def make(refs, send_sems, recv_sems):
        x, y, c, _ = _place()
        return [_remote(refs[t].at[c], refs[t].at[c], send_sems.at[t], recv_sems.at[t], (x, y, 1 - c))
                for t in range(nt)]
    return make


def _sibling_share_start(fulls, name):
    return _split_start(name, list(fulls), len(fulls), _share_copies(len(fulls)))


def _sibling_share_wait(started, after, name):
    return _split_wait(name, started, _share_copies(len(started) - 3), after)


SUM_STEPS = 4


def _pair_sum(views, lands, ck):
    nt = len(views)
    in_specs, out_specs, shapes = [], [], []
    for v in views:
        b, _, r, cc = v.shape
        per = SUM_STEPS // b
        tr = r // per
        in_specs.append(pl.BlockSpec((None, None, tr, cc), lambda i, s, per=per: (i // per, s[0], i % per, 0)))
        out_specs.append(pl.BlockSpec((None, tr, cc), lambda i, s, per=per: (i // per, i % per, 0)))
        shapes.append((b, r, cc))
    in_specs = in_specs + out_specs

    def body(s_ref, *refs):
        for t in range(nt):
            refs[2 * nt + t][...] = (refs[t][...].astype(F32) + refs[nt + t][...].astype(F32)).astype(BF16)

    return pl.pallas_call(
        body, name="grad_pair_sum", out_shape=[jax.ShapeDtypeStruct(s, BF16) for s in shapes],
        grid_spec=pltpu.PrefetchScalarGridSpec(num_scalar_prefetch=1, grid=(SUM_STEPS,), in_specs=in_specs,
                                               out_specs=out_specs),
        compiler_params=_cp(dimension_semantics=("parallel",)),
    )(ck, *views, *lands)


def _chip_sum(parts, lands, kinds, ck):
    nt = len(parts)
    steps = 2
    in_own, in_land, out_specs, shapes = [], [], [], []
    for ld, kind in zip(lands, kinds):
        _, r, cc = ld.shape
        tr = r // steps
        if kind == "col":
            in_own.append(pl.BlockSpec((None, tr, cc), lambda i, s: (s[1] // 2, i, s[1] % 2)))
        else:
            in_own.append(pl.BlockSpec((None, tr, cc), lambda i, s: (s[1], i, 0)))
        in_land.append(pl.BlockSpec((3, tr, cc), lambda i, s: (0, i, 0)))
        out_specs.append(pl.BlockSpec((None, tr, cc), lambda i, s: (s[0], i, 0)))
        shapes.append((2, r, cc))

    def body(s_ref, *refs):
        for t in range(nt):
            acc = refs[t][...].astype(F32)
            for j in range(3):
                acc = acc + refs[nt + t][j].astype(F32)
            refs[2 * nt + t][...] = acc

    return pl.pallas_call(
        body, name="grad_chip_sum", out_shape=[jax.ShapeDtypeStruct(s, F32) for s in shapes],
        grid_spec=pltpu.PrefetchScalarGridSpec(num_scalar_prefetch=1, grid=(steps,), in_specs=in_own + in_land,
                                               out_specs=out_specs),
        compiler_params=_cp(dimension_semantics=("parallel",)),
    )(ck, *parts, *lands)


def _sum8(g):
    _, r, cc = g.shape
    tr = _row_tile(r, N_DEV * cc)

    def body(g_ref, o_ref):
        acc = g_ref[0].astype(F32)
        for d in range(1, N_DEV):
            acc = acc + g_ref[d].astype(F32)
        o_ref[...] = acc

    return pl.pallas_call(
        body, name="small_grad_sum", grid=(r // tr,),
        in_specs=[pl.BlockSpec((N_DEV, tr, cc), lambda i: (0, i, 0))],
        out_specs=pl.BlockSpec((tr, cc), lambda i: (i, 0)),
        out_shape=jax.ShapeDtypeStruct((r, cc), F32),
        compiler_params=_cp(dimension_semantics=("parallel",)),
    )(g)


def _silu_rows(c):
    def body(c_ref, o_ref):
        v = c_ref[...]
        o_ref[...] = v * jax.nn.sigmoid(v)

    return pl.pallas_call(body, name="cond_silu", out_shape=jax.ShapeDtypeStruct(c.shape, F32))(c)


def _pack(arrays):
    rows = []
    for a in arrays:
        flat = a.reshape(-1)
        rows.append(jnp.pad(flat, (0, (-flat.shape[0]) % (8 * LANES))).reshape(-1, LANES))
    n = sum(r.shape[0] for r in rows)
    if n % 256:
        rows.append(jnp.zeros((256 - n % 256, LANES), rows[0].dtype))
    return jnp.concatenate(rows, axis=0)


def _unpack(packed, shapes):
    out, off = [], 0
    for s in shapes:
        n = math.prod(s)
        nr = 8 * -(-n // (8 * LANES))
        out.append(packed[off:off + nr].reshape(-1)[:n].reshape(s))
        off += nr
    return out


def _as_rows(a):
    return a.reshape(1, -1) if a.ndim == 1 else a.reshape(-1, a.shape[-1])


def _adamw_many(ws, gs, ms, vs, name, steps=1):
    nt = len(ws)

    def body(*refs):
        for t in range(nt):
            w_ref, g_ref, m_ref, v_ref = (refs[k * nt + t] for k in range(4))
            d, m, v = _adamw_math(w_ref[...], g_ref[...], m_ref[...], v_ref[...])
            refs[4 * nt + t][...] = d
            refs[5 * nt + t][...] = m
            refs[6 * nt + t][...] = v

    shapes = [jax.ShapeDtypeStruct(a.shape, F32) for a in ws]
    if steps == 1:
        outs = pl.pallas_call(body, name=name, out_shape=shapes * 3, compiler_params=_cp())(*ws, *gs, *ms, *vs)
    else:
        specs = [pl.BlockSpec((a.shape[0] // steps, a.shape[1]), lambda i: (i, 0)) for a in ws]
        outs = pl.pallas_call(
            body, name=name, grid=(steps,), in_specs=specs * 4, out_specs=specs * 3, out_shape=shapes * 3,
            compiler_params=_cp(dimension_semantics=("parallel",)),
        )(*ws, *gs, *ms, *vs)
    return outs[:nt], outs[nt:2 * nt], outs[2 * nt:]


def _exchange_big_grads(grads, kinds, layer):
    views = []
    for g, kind in zip(grads, kinds):
        if kind == "col":
            views.append(g.reshape(2, 2, g.shape[1] // 2, g.shape[2]))
        else:
            views.append(g.reshape(N_CHIP, 2, g.shape[0] // (2 * N_CHIP), g.shape[1]))
    return _sibling_exchange_start(views, "grad_exchange_start_%d" % layer)


def _scatter_big_grads(exchanged, kinds, ck, after, layer):
    views, lands = _sibling_exchange_wait(exchanged, after, "grad_exchange_wait_%d" % layer)
    parts = _pair_sum(views, lands, ck)
    return _chip_scatter_start(parts, kinds, "grad_scatter_start_%d" % layer)


def _finish_big_grads(started, kinds, ck, after, layer):
    nt = len(kinds)
    send_sems, recv_sems = started[0], started[1]
    parts, lands = started[2:2 + nt], started[2 + nt:2 + 2 * nt]
    parts, lands = _chip_scatter_wait(send_sems, recv_sems, parts, lands, kinds, after, "grad_scatter_wait_%d" % layer)
    return _sibling_share_start(_chip_sum(parts, lands, kinds, ck), "grad_share_start_%d" % layer)


def _adamw_layer(ws, gs, ms, vs, stacks, layer, name, steps):
    nt = len(ws)
    stacks = [s if s is not None else tuple(lax.empty(w.shape, F32) for _ in range(4)) for s, w in zip(stacks, ws)]

    def body(*refs):
        for t in range(nt):
            w_ref, g_ref, m_ref, v_ref = (refs[k * nt + t] for k in range(4))
            outs = refs[8 * nt + 4 * t:8 * nt + 4 * t + 4]
            g = g_ref[...]
            outs[0][...] = g
            outs[1][...], outs[2][...], outs[3][...] = _adamw_math(w_ref[...], g, m_ref[...], v_ref[...])

    in_specs, g_specs, out_specs = [], [], []
    for w in ws:
        _, r, c = w.shape
        in_specs.append(pl.BlockSpec((None, r // steps, c), lambda i: (layer, i, 0)))
        g_specs.append(pl.BlockSpec((r // steps, c), lambda i: (i, 0)))
        out_specs += [pl.BlockSpec((None, r // steps, c), lambda i: (layer, i, 0))] * 4
    in_specs = in_specs + g_specs + in_specs * 2 + [ANY] * (4 * nt)
    flat = [a for s in stacks for a in s]
    outs = pl.pallas_call(
        body, name=name, grid=(steps,), in_specs=in_specs, out_specs=out_specs,
        out_shape=[jax.ShapeDtypeStruct(a.shape, F32) for a in flat],
        input_output_aliases={4 * nt + k: k for k in range(4 * nt)},
        compiler_params=_cp(dimension_semantics=("parallel",)),
    )(*ws, *gs, *ms, *vs, *flat)
    return [tuple(outs[4 * t:4 * t + 4]) for t in range(nt)]


SMALL_NAMES = ["ada_b", "norm1_g", "norm2_g", "sgu_w", "sgu_b", "pool_w", "pool_scale", "conv_w", "s5_lambda_re",
               "s5_lambda_im", "s5_b_re", "s5_b_im", "s5_c_re", "s5_c_im", "s5_d", "s5_log_dt", "s5_glu_w", "s5_glu_b",
               "mix_norm_g", "norm3_g", "final_norm_g"]
BIG_NAMES = ["ffn1_w_in", "ffn1_w_out", "w_mix_in", "w_mix_out", "ffn2_w_in", "ffn2_w_out"]
BIG_KINDS = ["col", "row", "row", "row", "col", "row"]
WEIGHT_ORDER = ["ada_w", "ada_b", "norm1_g", "ffn1_w_in", "ffn1_w_out", "norm2_g", "w_mix_in", "sgu_w", "sgu_b", "pool_w",
                "pool_scale", "conv_w", "s5_lambda_re", "s5_lambda_im", "s5_b_re", "s5_b_im", "s5_c_re", "s5_c_im", "s5_d",
                "s5_log_dt", "s5_glu_w", "s5_glu_b", "mix_norm_g", "w_mix_out", "norm3_g", "ffn2_w_in", "ffn2_w_out",
                "final_norm_g"]


def _local_step(x, target, cond, fetch_weights, prefetch_weights, p, emit_grads):
    nl, d = DEPTH, x.shape[1]
    row = lambda a: a.reshape(1, -1)
    saved = []
    for l in range(nl):
        (wi1, wo1, wmit, wmo, wi2, wo2), tok = fetch_weights(l, x)
        cl = cond[l] + tok
        mod1, mod2, mod3 = cl[0:3], cl[3:6], cl[6:9]
        lre, lim = p["s5_lambda_re"][l].reshape(-1), p["s5_lambda_im"][l].reshape(-1)
        ldt = jnp.repeat(p["s5_log_dt"][l], 64)
        rowp = jnp.stack([lre, lim, ldt])
        sp = (rowp, rowp.T, p["s5_b_re"][l].reshape(N_STATE, 16), p["s5_b_im"][l].reshape(N_STATE, 16),
              p["s5_c_re"][l].reshape(W_GRP, 64), p["s5_c_im"][l].reshape(W_GRP, 64), row(p["s5_d"][l]))
        glu = (p["s5_glu_w"][l], row(p["s5_glu_b"][l]))
        bias_full = jnp.repeat(p["sgu_b"][l].T, 64, axis=1)
        pw2 = p["pool_w"][l].reshape(W_GRP, 64)
        x1, h1, a1, b1, o1 = _ffn_fwd(x, mod1, row(p["norm1_g"][l]), wi1, wo1)
        z, h2 = _mix_in_fwd(x1, mod2, row(p["norm2_g"][l]), wmit)
        ya = _sgu_fwd(z, p["sgu_w"][l], bias_full)
        yb, yc = _poolconv_fwd(z, pw2, row(p["pool_scale"][l]), p["conv_w"][l])
        ypre = _s5_core_fwd(z, sp)
        yd = _s5_glu_fwd(ypre, *glu)
        ys = (ya, yb, yc, yd)
        x2, m = _mix_out_fwd(ys, row(p["mix_norm_g"][l]), wmo, x1, mod2[2:3])
        mod3 = mod3 + prefetch_weights(l + 1, x2)
        x3, h3, a3, b3, o3 = _ffn_fwd(x2, mod3, row(p["norm3_g"][l]), wi2, wo2)
        saved.append((x, x1, x2, h1, a1, b1, o1, z, h2, ys, m, h3, a3, b3, o3, sp, bias_full, pw2, ypre, glu,
                      (wi1, wo1, wmit, wmo, wi2, wo2), cl))
        x = x3

    loss, dx, dfg = _loss_head(x, row(p["final_norm_g"]), target)

    sg = {n: [None] * nl for n in SMALL_NAMES if n not in ("ada_b", "final_norm_g")}
    dcond = [None] * nl
    s5_da, s5_dk = [None] * nl, [None] * nl
    tok = 0.0
    for l in reversed(range(nl)):
        (x0, x1, x2, h1, a1, b1, o1, z, h2, ys, m, h3, a3, b3, o3, sp, bias_full, pw2, ypre, glu,
         (wi1, wo1, wmit, wmo, wi2, wo2), cl) = saved[l]
        cl = cl + tok
        mod1, mod2, mod3 = cl[0:3], cl[3:6], cl[6:9]
        dza, dzb, dwi2, dwo2, dgate3 = _ffn_bwd_main(dx, o3, mod3[2:3], h3, a3, b3, wo2)
        dx, rows3 = _ffn_bwd_in(dza, dzb, wi2, x2, dx, mod3, row(p["norm3_g"][l]))
        outs = _mix_out_bwd(dx, m, mod2[2:3], ys, row(p["mix_norm_g"][l]), wmo)
        dys, dgate2, dmng, dwmo = outs[0:4], outs[4], outs[5], outs[6]
        dza_, dsw, dsb = _sgu_bwd(z, dys[0], p["sgu_w"][l], bias_full)
        dzb_, dzc_, dwbd, dps, dcw = _poolconv_bwd(z, dys[1], dys[2], pw2, row(p["pool_scale"][l]), p["conv_w"][l])
        dypre, dgw, dgb = _s5_glu_bwd(ypre, dys[3], *glu)
        dzd_, dbr, dbi, dcr, dci, dd, da, dk = _s5_core_bwd(z, dypre, sp)
        dx, rows2, dwmit = _mix_in_bwd((dza_, dzb_, dzc_, dzd_), h2, wmit, x1, dx, mod2, row(p["norm2_g"][l]))
        dza, dzb, dwi1, dwo1, dgate1 = _ffn_bwd_main(dx, o1, mod1[2:3], h1, a1, b1, wo1)
        tok, layer_done = emit_grads(l, [dwi1, dwo1, dwmit, dwmo, dwi2, dwo2])
        dx, rows1 = _ffn_bwd_in(dza, dzb, wi1, x0, dx, mod1 + tok, row(p["norm1_g"][l]))
        tok = layer_done(dx)
        dcond[l] = jnp.concatenate([rows1[0:2], dgate1, rows2[0:2], dgate2, rows3[0:2], dgate3], axis=0)
        sg["norm1_g"][l], sg["norm2_g"][l], sg["norm3_g"][l] = rows1[2], rows2[2], rows3[2]
        sg["mix_norm_g"][l] = dmng[0]
        sg["sgu_w"][l] = dsw
        sg["sgu_b"][l] = dsb[:, 0:4].T
        g4 = dwbd.reshape(4, 64, 4, 64)
        sg["pool_w"][l] = jnp.stack([g4[k, :, k, :] for k in range(4)])
        sg["pool_scale"][l] = dps[0]
        sg["conv_w"][l] = dcw[0:3]
        sg["s5_b_re"][l], sg["s5_b_im"][l] = dbr.reshape(16, 64, 16), dbi.reshape(16, 64, 16)
        sg["s5_c_re"][l], sg["s5_c_im"][l] = dcr.reshape(16, 16, 64), dci.reshape(16, 16, 64)
        sg["s5_d"][l] = dd[0]
        sg["s5_glu_w"][l], sg["s5_glu_b"][l] = dgw, dgb[0]
        s5_da[l], s5_dk[l] = da, dk

    n16 = nl * 16
    dlre, dlim, dldt = _s5_param_bwd(
        p["s5_lambda_re"].reshape(n16, 64), p["s5_lambda_im"].reshape(n16, 64),
        jnp.repeat(p["s5_log_dt"].reshape(n16, 1), 64, axis=1),
        jnp.stack([a[0] for a in s5_da]).reshape(n16, 64), jnp.stack([a[1] for a in s5_da]).reshape(n16, 64),
        jnp.stack([k[:, 0] for k in s5_dk]).reshape(n16, 64), jnp.stack([k[:, 1] for k in s5_dk]).reshape(n16, 64))
    small = {n: jnp.stack(v) for n, v in sg.items() if v[0] is not None}
    small["s5_lambda_re"] = dlre.reshape(nl, 16, 64)
    small["s5_lambda_im"] = dlim.reshape(nl, 16, 64)
    small["s5_log_dt"] = dldt.reshape(nl, 16)
    small["final_norm_g"] = dfg[0]
    return loss, dx, small, jnp.stack(dcond)


def kernel(x, c, ada_w, ada_b, norm1_g, ffn1_w_in, ffn1_w_out, norm2_g, w_mix_in, sgu_w, sgu_b, pool_w, pool_scale, conv_w, s5_lambda_re, s5_lambda_im, s5_b_re, s5_b_im, s5_c_re, s5_c_im, s5_d, s5_log_dt, s5_glu_w, s5_glu_b, mix_norm_g, w_mix_out, norm3_g, ffn2_w_in, ffn2_w_out, final_norm_g, loss_target, m_ada_w, m_ada_b, m_norm1_g, m_ffn1_w_in, m_ffn1_w_out, m_norm2_g, m_w_mix_in, m_sgu_w, m_sgu_b, m_pool_w, m_pool_scale, m_conv_w, m_s5_lambda_re, m_s5_lambda_im, m_s5_b_re, m_s5_b_im, m_s5_c_re, m_s5_c_im, m_s5_d, m_s5_log_dt, m_s5_glu_w, m_s5_glu_b, m_mix_norm_g, m_w_mix_out, m_norm3_g, m_ffn2_w_in, m_ffn2_w_out, m_final_norm_g, v_ada_w, v_ada_b, v_norm1_g, v_ffn1_w_in, v_ffn1_w_out, v_norm2_g, v_w_mix_in, v_sgu_w, v_sgu_b, v_pool_w, v_pool_scale, v_conv_w, v_s5_lambda_re, v_s5_lambda_im, v_s5_b_re, v_s5_b_im, v_s5_c_re, v_s5_c_im, v_s5_d, v_s5_log_dt, v_s5_glu_w, v_s5_glu_b, v_mix_norm_g, v_w_mix_out, v_norm3_g, v_ffn2_w_in, v_ffn2_w_out, v_final_norm_g):
    args = dict(locals())
    w = {n: args[n] for n in WEIGHT_ORDER}
    mom = {n: args["m_" + n] for n in WEIGHT_ORDER}
    vel = {n: args["v_" + n] for n in WEIGHT_ORDER}
    nl, d = DEPTH, x.shape[-1]
    s = x.shape[1]
    px, py, pc = lax.axis_index("x"), lax.axis_index("y"), lax.axis_index("c")
    kme = 2 * px + py
    me = 2 * kme + pc
    kidx = jnp.reshape(kme, (1,)).astype(jnp.int32)

    shards = [ffn1_w_in, ffn1_w_out, jnp.swapaxes(w_mix_in, 1, 2), w_mix_out, ffn2_w_in, ffn2_w_out]
    started_weights = {}

    def start_weights(l, after):
        placed = _place_weights(shards, l, kidx, after)
        views = [a.reshape(a.shape[0], 2, a.shape[1] // 2, a.shape[2]) for a in placed]
        *handles, token = _weight_send_start(views, BIG_KINDS, "weight_send_start_%d" % l)
        started_weights[l] = handles
        return token[0, 0]

    cact = _silu_rows(c)
    pre = _pack([cact, conv_w, s5_glu_w])
    pre_all = _all_gather8(pre, "gather_prelude")
    tok0 = start_weights(0, pre_all)
    parts = [_unpack(pre_all[dev], [cact.shape, conv_w.shape, s5_glu_w.shape]) for dev in range(N_DEV)]
    cact_all = pre_all[:, :d // LANES, :].reshape(N_DEV, d)
    conv_full = jnp.concatenate([parts[2 * k][1] for k in range(N_CHIP)], axis=2)
    glu_full = jnp.concatenate([parts[2 * k][2] for k in range(N_CHIP)], axis=1)

    n_ada = ada_w.shape[2]
    ada_b_loc = lax.dynamic_slice_in_dim(ada_b, kme * n_ada, n_ada, axis=1).reshape(nl, 1, n_ada) + tok0
    cond_part = _cond_fwd(cact_all, ada_w, ada_b_loc)
    cond_all = _all_gather8(cond_part.reshape(nl * N_DEV, n_ada), "gather_cond").reshape(N_DEV, nl, N_DEV, n_ada)
    cond_me = jnp.concatenate(
        [lax.dynamic_index_in_dim(cond_all[2 * k], me, axis=1, keepdims=False) for k in range(N_CHIP)], axis=1)
    cond = cond_me.reshape(nl, 9, d)

    forwarding = {}

    def prefetch_weights(l, after):
        if l >= nl:
            return 0.0
        send_sems, recv_sems, *views = started_weights.pop(l)
        views = _weight_send_wait(send_sems, recv_sems, views, BIG_KINDS, after, "weight_send_wait_%d" % l)
        forwarding[l] = _split_start("weight_forward_start_%d" % l, views, 3 * len(views), _forward_copies(BIG_KINDS))
        return forwarding[l][-1][0, 0]

    def fetch_weights(l, after):
        if l not in forwarding:
            prefetch_weights(l, after)
        views = _split_wait("weight_forward_wait_%d" % l, forwarding.pop(l), _forward_copies(BIG_KINDS), after)
        tok = start_weights(l + 1, after) if l + 1 < nl else 0.0
        full = [v.reshape(2, 2 * v.shape[2], v.shape[3]) if kind == "col" else v.reshape(-1, v.shape[3])
                for v, kind in zip(views, BIG_KINDS)]
        return full, tok

    ck = jnp.stack([pc, kme]).astype(jnp.int32)
    scattering, sharing = [], []
    stacks = {n: None for n in BIG_NAMES}
    groups = ((["ffn1_w_in", "ffn2_w_in"], 16, "adamw_w_in"),
              (["ffn1_w_out", "w_mix_in", "w_mix_out", "ffn2_w_out"], 8, "adamw_w_out"))

    def apply_adamw(l, fulls):
        g = {n: f.reshape(2 * f.shape[1], f.shape[2]) for n, f in zip(BIG_NAMES, fulls)}
        g["w_mix_in"] = g["w_mix_in"].T
        for names, steps, call in groups:
            outs = _adamw_layer([w[n] for n in names], [g[n] for n in names], [mom[n] for n in names],
                                [vel[n] for n in names], [stacks[n] for n in names], l, call, steps)
            stacks.update(zip(names, outs))

    def retire(after):
        if sharing:
            l2, shared = sharing.pop(0)
            apply_adamw(l2, _sibling_share_wait(shared, after, "grad_share_wait_%d" % l2))
        if scattering:
            l1, scattered = scattering.pop(0)
            sharing.append((l1, _finish_big_grads(scattered, BIG_KINDS, ck, after, l1)))

    def emit_grads(l, grads_l):
        exchanged = _exchange_big_grads(grads_l, BIG_KINDS, l)

        def layer_done(after):
            started = _scatter_big_grads(exchanged, BIG_KINDS, ck, after, l)
            retire(after)
            scattering.append((l, started))
            return started[-1][0, 0]

        return exchanged[-1][0, 0], layer_done

    p = {n: w[n] for n in SMALL_NAMES}
    p["conv_w"], p["s5_glu_w"] = conv_full, glu_full
    loss, dx, small, dcond = _local_step(x[0], loss_target[0], cond, fetch_weights, prefetch_weights, p, emit_grads)

    small_order = [n for n in SMALL_NAMES if n != "ada_b"]
    packed = _pack([dcond] + [small[n] for n in small_order])
    gathered_small = _all_gather8(packed.astype(BF16), "gather_small_grads")
    total = _sum8(gathered_small)
    shapes = [dcond.shape] + [small[n].shape for n in small_order]
    tot = dict(zip(["ada_b"] + small_order, _unpack(total, shapes)))
    grads = {n: tot[n] for n in SMALL_NAMES}
    grads["ada_b"] = tot["ada_b"].reshape(nl, 9 * d)
    grads["conv_w"] = lax.dynamic_slice_in_dim(tot["conv_w"], kme * conv_w.shape[2], conv_w.shape[2], axis=2)
    grads["s5_glu_w"] = lax.dynamic_slice_in_dim(tot["s5_glu_w"], kme * s5_glu_w.shape[1], s5_glu_w.shape[1], axis=1)

    dcond_all = gathered_small.reshape(N_DEV, -1)[:, :dcond.size].reshape(N_DEV, nl, 9 * d)
    dcond_loc = jnp.swapaxes(lax.dynamic_slice_in_dim(dcond_all, kme * n_ada, n_ada, axis=2), 0, 1)
    g_ada, d_ada, m_ada, v_ada = _ada_w_update(cact_all, dcond_loc, ada_w, m_ada_w, v_ada_w)

    while scattering or sharing:
        retire(g_ada)
    delta, new_m, new_v = {}, {}, {}
    for n in BIG_NAMES:
        grads[n], delta[n], new_m[n], new_v[n] = stacks[n]

    grads["ada_w"], delta["ada_w"], new_m["ada_w"], new_v["ada_w"] = g_ada, d_ada, m_ada, v_ada
    wide = ("s5_b_re", "s5_b_im")
    for names, call, steps in (([n for n in SMALL_NAMES if n not in wide], "adamw_small", 1),
                               (list(wide), "adamw_s5_b", DEPTH)):
        outs = _adamw_many(*[[_as_rows(t[n]) for n in names] for t in (w, grads, mom, vel)], call, steps)
        for res, o in zip((delta, new_m, new_v), outs):
            res.update({n: a.reshape(w[n].shape) for n, a in zip(names, o)})

    loss_total = lax.psum(loss[0, 0], ("x", "y", "c"))
    return (loss_total, dx[None], *[grads[n] for n in WEIGHT_ORDER], *[delta[n] for n in WEIGHT_ORDER],
            *[new_m[n] for n in WEIGHT_ORDER], *[new_v[n] for n in WEIGHT_ORDER])
```

```python
import functools
import math

import jax
import jax.numpy as jnp
from jax import lax
from jax.experimental import pallas as pl
from jax.experimental.pallas import tpu as pltpu

F32, BF16 = jnp.float32, jnp.bfloat16
EPS = 1e-6
DEPTH = 4
N_DEV = 8
N_CHIP = 4
W_GRP = 256
CHUNK = 128
N_SEG = 8
LANES = 128
FFN_TF = 256
FFN_TF_WIDE = 1408
FFN_TM_WIDE = 512
VMEM_LIMIT = 56 * 1024 * 1024
ADAM_LR, ADAM_B1, ADAM_B2, ADAM_EPS, ADAM_WD, ADAM_STEP = 0.001, 0.9, 0.999, 1e-08, 0.01, 10
MESH_ID = pl.DeviceIdType.MESH
HI = lax.Precision.HIGHEST
ANY = pl.BlockSpec(memory_space=pl.ANY)


def _cp(**kw):
    return pltpu.CompilerParams(vmem_limit_bytes=VMEM_LIMIT, **kw)


def _dot(a, b):
    return jnp.dot(a.astype(BF16), b.astype(BF16), preferred_element_type=F32)


def _dot_nt(a, b):
    return lax.dot_general(a.astype(BF16), b.astype(BF16), (((1,), (1,)), ((), ())), preferred_element_type=F32)


def _dot_tn(a, b):
    return lax.dot_general(a.astype(BF16), b.astype(BF16), (((0,), (0,)), ((), ())), preferred_element_type=F32)


def _dot_hi(a, b):
    return jnp.dot(a, b, preferred_element_type=F32, precision=HI)


def _gelu(x):
    k = 0.7978845608028654
    t = jnp.tanh(k * (x + 0.044715 * x * x * x))
    return 0.5 * x * (1.0 + t), t


def _gelu_grad(x, t):
    k = 0.7978845608028654
    return 0.5 * (1.0 + t) + 0.5 * x * (1.0 - t * t) * k * (1.0 + 3.0 * 0.044715 * x * x)


def _iota(shape, axis):
    return lax.broadcasted_iota(jnp.int32, shape, axis)


def _full(shape):
    nd = len(shape)
    return pl.BlockSpec(shape, lambda *_: (0,) * nd)


def _norm_mod(xv, g, shift, scale):
    r = lax.rsqrt(jnp.mean(xv * xv, axis=-1, keepdims=True) + EPS)
    return (xv * r * g) * (1.0 + scale) + shift


def _norm_mod_bwd(xv, g, scale, dh):
    r = lax.rsqrt(jnp.mean(xv * xv, axis=-1, keepdims=True) + EPS)
    xh = xv * r
    n = xh * g
    dsh = jnp.sum(dh, axis=0, keepdims=True)
    dsc = jnp.sum(dh * n, axis=0, keepdims=True)
    dn = dh * (1.0 + scale)
    dg = jnp.sum(dn * xh, axis=0, keepdims=True)
    dxh = dn * g
    dx = r * (dxh - xh * jnp.mean(dxh * xh, axis=-1, keepdims=True))
    return dx, dsh, dsc, dg


def _tm(s):
    return min(s, 1024)


def _ffn_fwd(x, mod, g, wi, wo):
    s, d = x.shape
    f = wo.shape[0]
    tf, tm = FFN_TF_WIDE, min(s, FFN_TM_WIDE)
    nf, nt = f // tf, s // tm

    def body(x_ref, mod_ref, g_ref, wa_ref, wb_ref, wo_ref, xn_ref, h_ref, a_ref, b_ref, o_ref, acc):
        j = pl.program_id(1)

        @pl.when(j == 0)
        def _():
            hh = _norm_mod(x_ref[...], g_ref[...], mod_ref[0:1, :], mod_ref[1:2, :])
            h_ref[...] = hh.astype(BF16)
            acc[...] = jnp.zeros_like(acc)

        h = h_ref[...]
        a = jnp.dot(h, wa_ref[...], preferred_element_type=F32)
        b = jnp.dot(h, wb_ref[...], preferred_element_type=F32)
        a_ref[...] = a.astype(BF16)
        b_ref[...] = b.astype(BF16)
        u = (a * jax.nn.sigmoid(a)) * b
        acc[...] += jnp.dot(u.astype(BF16), wo_ref[...], preferred_element_type=F32)

        @pl.when(j == nf - 1)
        def _():
            o = acc[...]
            o_ref[...] = o.astype(BF16)
            xn_ref[...] = x_ref[...] + 0.5 * mod_ref[2:3, :] * o

    row = pl.BlockSpec((tm, d), lambda i, j: (i, 0))
    chunk = pl.BlockSpec((tm, tf), lambda i, j: (i, j))
    return pl.pallas_call(
        body, name="ffn_fwd", grid=(nt, nf),
        in_specs=[row, _full((3, d)), _full((1, d)),
                  pl.BlockSpec((None, d, tf), lambda i, j: (0, 0, j)),
                  pl.BlockSpec((None, d, tf), lambda i, j: (1, 0, j)),
                  pl.BlockSpec((tf, d), lambda i, j: (j, 0))],
        out_specs=[row, row, chunk, chunk, row],
        out_shape=[jax.ShapeDtypeStruct((s, d), F32), jax.ShapeDtypeStruct((s, d), BF16),
                   jax.ShapeDtypeStruct((s, f), BF16), jax.ShapeDtypeStruct((s, f), BF16),
                   jax.ShapeDtypeStruct((s, d), BF16)],
        scratch_shapes=[pltpu.VMEM((tm, d), F32)],
        compiler_params=_cp(dimension_semantics=("parallel", "arbitrary")),
    )(x, mod, g, wi, wi, wo)


def _ffn_bwd_main(dxo, o, gate, h, a, b, wo):
    s, d = dxo.shape
    f = wo.shape[0]
    tf = FFN_TF
    nf = f // tf

    def body(dxo_ref, o_ref, gate_ref, h_ref, a_ref, b_ref, wo_ref, dza_ref, dzb_ref, dwi_ref, dwo_ref, dg_ref, do_s):
        @pl.when(pl.program_id(0) == 0)
        def _():
            dxv = dxo_ref[...]
            do_s[...] = (0.5 * gate_ref[...] * dxv).astype(BF16)
            dg_ref[...] = 0.5 * jnp.sum(o_ref[...].astype(F32) * dxv, axis=0, keepdims=True)

        dov = do_s[...]
        hv = h_ref[...]
        du = lax.dot_general(dov, wo_ref[...], (((1,), (1,)), ((), ())), preferred_element_type=F32)
        av = a_ref[...].astype(F32)
        bv = b_ref[...].astype(F32)
        sa = jax.nn.sigmoid(av)
        si = av * sa
        u = (si * bv).astype(BF16)
        da = (du * bv * (sa * (1.0 + av * (1.0 - sa)))).astype(BF16)
        db = (du * si).astype(BF16)
        dza_ref[...] = da
        dzb_ref[...] = db
        dwo_ref[...] = _dot_tn(u, dov).astype(BF16)
        dwi_ref[0] = _dot_tn(hv, da).astype(BF16)
        dwi_ref[1] = _dot_tn(hv, db).astype(BF16)

    chunk = pl.BlockSpec((s, tf), lambda j: (0, j))
    once = lambda: pl.BlockSpec((s, d), lambda j: (0, 0), pipeline_mode=pl.Buffered(1))
    return pl.pallas_call(
        body, name="ffn_bwd_main", grid=(nf,),
        in_specs=[once(), once(), _full((1, d)), once(), chunk, chunk, pl.BlockSpec((tf, d), lambda j: (j, 0))],
        out_specs=[chunk, chunk, pl.BlockSpec((2, d, tf), lambda j: (0, 0, j)),
                   pl.BlockSpec((tf, d), lambda j: (j, 0)), _full((1, d))],
        out_shape=[jax.ShapeDtypeStruct((s, f), BF16), jax.ShapeDtypeStruct((s, f), BF16),
                   jax.ShapeDtypeStruct((2, d, f), BF16), jax.ShapeDtypeStruct((f, d), BF16),
                   jax.ShapeDtypeStruct((1, d), F32)],
        scratch_shapes=[pltpu.VMEM((s, d), BF16)],
        compiler_params=_cp(dimension_semantics=("arbitrary",)),
    )(dxo, o, gate, h, a, b, wo)


def _ffn_bwd_in(dza, dzb, wi, x, dxo, mod, g):
    s, d = x.shape
    f = dza.shape[1]
    tf, tm = FFN_TF_WIDE, min(s, FFN_TM_WIDE)
    nf, nt = f // tf, s // tm

    def body(dza_ref, dzb_ref, wa_ref, wb_ref, x_ref, dxo_ref, mod_ref, g_ref, dx_ref, rows_ref, acc):
        i, j = pl.program_id(0), pl.program_id(1)

        @pl.when(jnp.logical_and(i == 0, j == 0))
        def _():
            rows_ref[...] = jnp.zeros_like(rows_ref)

        @pl.when(j == 0)
        def _():
            acc[...] = jnp.zeros_like(acc)

        acc[...] += (lax.dot_general(dza_ref[...], wa_ref[...], (((1,), (1,)), ((), ())), preferred_element_type=F32)
                     + lax.dot_general(dzb_ref[...], wb_ref[...], (((1,), (1,)), ((), ())), preferred_element_type=F32))

        @pl.when(j == nf - 1)
        def _():
            dx, dsh, dsc, dg = _norm_mod_bwd(x_ref[...], g_ref[...], mod_ref[1:2, :], acc[...])
            dx_ref[...] = dx + dxo_ref[...]
            rows_ref[0:1, :] += dsh
            rows_ref[1:2, :] += dsc
            rows_ref[2:3, :] += dg

    row = pl.BlockSpec((tm, d), lambda i, j: (i, 0))
    chunk = pl.BlockSpec((tm, tf), lambda i, j: (i, j))
    return pl.pallas_call(
        body, name="ffn_bwd_in", grid=(nt, nf),
        in_specs=[chunk, chunk,
                  pl.BlockSpec((None, d, tf), lambda i, j: (0, 0, j)),
                  pl.BlockSpec((None, d, tf), lambda i, j: (1, 0, j)),
                  row, row, _full((3, d)), _full((1, d))],
        out_specs=[row, _full((8, d))],
        out_shape=[jax.ShapeDtypeStruct((s, d), F32), jax.ShapeDtypeStruct((8, d), F32)],
        scratch_shapes=[pltpu.VMEM((tm, d), F32)],
        compiler_params=_cp(dimension_semantics=("arbitrary", "arbitrary")),
    )(dza, dzb, wi, wi, x, dxo, mod, g)


def _mix_in_fwd(x, mod, g, wmit):
    s, d = x.shape
    p = wmit.shape[0]
    tm = _tm(s)

    def body(x_ref, mod_ref, g_ref, w_ref, z_ref, h_ref):
        hh = _norm_mod(x_ref[...], g_ref[...], mod_ref[0:1, :], mod_ref[1:2, :]).astype(BF16)
        h_ref[...] = hh
        z_ref[...] = lax.dot_general(hh, w_ref[...], (((1,), (1,)), ((), ())), preferred_element_type=F32)

    row = pl.BlockSpec((tm, d), lambda i: (i, 0))
    return pl.pallas_call(
        body, name="mix_in_fwd", grid=(s // tm,),
        in_specs=[row, _full((3, d)), _full((1, d)), _full((p, d))],
        out_specs=[pl.BlockSpec((tm, p), lambda i: (i, 0)), row],
        out_shape=[jax.ShapeDtypeStruct((s, p), F32), jax.ShapeDtypeStruct((s, d), BF16)],
        compiler_params=_cp(dimension_semantics=("parallel",)),
    )(x, mod, g, wmit)


def _mix_in_bwd(dzs, h, wmit, x, dxo, mod, g):
    s, d = x.shape
    p = wmit.shape[0]
    tm = min(s, 512)
    nt = s // tm

    def body(za_ref, zb_ref, zc_ref, zd_ref, h_ref, w_ref, x_ref, dxo_ref, mod_ref, g_ref,
             dx_ref, rows_ref, dw_ref, acc):
        i = pl.program_id(0)

        @pl.when(i == 0)
        def _():
            rows_ref[...] = jnp.zeros_like(rows_ref)
            acc[...] = jnp.zeros_like(acc)

        dz = jnp.concatenate([za_ref[...], zb_ref[...], zc_ref[...], zd_ref[...]], axis=1).astype(BF16)
        acc[...] += _dot_tn(dz, h_ref[...])
        dh = jnp.dot(dz, w_ref[...], preferred_element_type=F32)
        dx, dsh, dsc, dg = _norm_mod_bwd(x_ref[...], g_ref[...], mod_ref[1:2, :], dh)
        dx_ref[...] = dx + dxo_ref[...]
        rows_ref[0:1, :] += dsh
        rows_ref[1:2, :] += dsc
        rows_ref[2:3, :] += dg

        @pl.when(i == nt - 1)
        def _():
            dw_ref[...] = acc[...].astype(BF16)

    row = pl.BlockSpec((tm, d), lambda i: (i, 0))
    zspecs = [pl.BlockSpec((tm, z.shape[1]), lambda i: (i, 0)) for z in dzs]
    return pl.pallas_call(
        body, name="mix_in_bwd", grid=(nt,),
        in_specs=zspecs + [row, _full((p, d)), row, row, _full((3, d)), _full((1, d))],
        out_specs=[row, _full((8, d)), _full((p, d))],
        out_shape=[jax.ShapeDtypeStruct((s, d), F32), jax.ShapeDtypeStruct((8, d), F32),
                   jax.ShapeDtypeStruct((p, d), BF16)],
        scratch_shapes=[pltpu.VMEM((p, d), F32)],
        compiler_params=_cp(dimension_semantics=("arbitrary",)),
    )(*dzs, h, wmit, x, dxo, mod, g)


def _group_norm(ys, mng):
    outs, hats, rs = [], [], []
    for k, y in enumerate(ys):
        r = lax.rsqrt(jnp.mean(y * y, axis=-1, keepdims=True) + EPS)
        yh = y * r
        hats.append(yh)
        rs.append(r)
        outs.append(yh * mng[:, k * W_GRP:(k + 1) * W_GRP])
    return jnp.concatenate(outs, axis=1), hats, rs


def _mix_out_fwd(ys, mng, wmo, x, gate):
    s, d = x.shape
    tm = _tm(s)

    def body(ya, yb, yc, yd, mng_ref, w_ref, x_ref, gate_ref, xn_ref, m_ref):
        yn, _, _ = _group_norm([ya[...], yb[...], yc[...], yd[...]], mng_ref[...])
        m = jnp.dot(yn.astype(BF16), w_ref[...], preferred_element_type=F32)
        m_ref[...] = m
        xn_ref[...] = x_ref[...] + gate_ref[...] * m

    row = pl.BlockSpec((tm, d), lambda i: (i, 0))
    grp = pl.BlockSpec((tm, W_GRP), lambda i: (i, 0))
    return pl.pallas_call(
        body, name="mix_out_fwd", grid=(s // tm,),
        in_specs=[grp, grp, grp, grp, _full((1, d)), _full((d, d)), row, _full((1, d))],
        out_specs=[row, row],
        out_shape=[jax.ShapeDtypeStruct((s, d), F32), jax.ShapeDtypeStruct((s, d), F32)],
        compiler_params=_cp(dimension_semantics=("parallel",)),
    )(*ys, mng, wmo, x, gate)


def _mix_out_bwd(dxo, m, gate, ys, mng, wmo):
    s, d = dxo.shape
    tm = min(s, 512)
    nt = s // tm

    def body(dxo_ref, m_ref, gate_ref, ya, yb, yc, yd, mng_ref, w_ref,
             dya, dyb, dyc, dyd, dgate_ref, dmng_ref, dw_ref, acc):
        i = pl.program_id(0)

        @pl.when(i == 0)
        def _():
            dgate_ref[...] = jnp.zeros_like(dgate_ref)
            dmng_ref[...] = jnp.zeros_like(dmng_ref)
            acc[...] = jnp.zeros_like(acc)

        dxv = dxo_ref[...]
        dgate_ref[...] += jnp.sum(m_ref[...] * dxv, axis=0, keepdims=True)
        dm = (gate_ref[...] * dxv).astype(BF16)
        mng = mng_ref[...]
        yn, hats, rs = _group_norm([ya[...], yb[...], yc[...], yd[...]], mng)
        acc[...] += _dot_tn(yn, dm)
        dyn = lax.dot_general(dm, w_ref[...], (((1,), (1,)), ((), ())), preferred_element_type=F32)
        dmng_parts = []
        for k, (yh, r, out) in enumerate(zip(hats, rs, (dya, dyb, dyc, dyd))):
            dk = dyn[:, k * W_GRP:(k + 1) * W_GRP]
            dmng_parts.append(jnp.sum(dk * yh, axis=0, keepdims=True))
            dyh = dk * mng[:, k * W_GRP:(k + 1) * W_GRP]
            out[...] = r * (dyh - yh * jnp.mean(dyh * yh, axis=-1, keepdims=True))
        dmng_ref[...] += jnp.concatenate(dmng_parts, axis=1)

        @pl.when(i == nt - 1)
        def _():
            dw_ref[...] = acc[...].astype(BF16)

    row = pl.BlockSpec((tm, d), lambda i: (i, 0))
    grp = pl.BlockSpec((tm, W_GRP), lambda i: (i, 0))
    return pl.pallas_call(
        body, name="mix_out_bwd", grid=(nt,),
        in_specs=[row, row, _full((1, d)), grp, grp, grp, grp, _full((1, d)), _full((d, d))],
        out_specs=[grp, grp, grp, grp, _full((1, d)), _full((1, d)), _full((d, d))],
        out_shape=[jax.ShapeDtypeStruct((s, W_GRP), F32)] * 4
        + [jax.ShapeDtypeStruct((1, d), F32), jax.ShapeDtypeStruct((1, d), F32), jax.ShapeDtypeStruct((d, d), BF16)],
        scratch_shapes=[pltpu.VMEM((d, d), F32)],
        compiler_params=_cp(dimension_semantics=("arbitrary",)),
    )(dxo, m, gate, *ys, mng, wmo)


def _sgu_consts():
    r = _iota((W_GRP, W_GRP), 0) >> 6
    c = _iota((W_GRP, W_GRP), 1) >> 6
    avg = jnp.where(r == c, 1.0 / 64.0, 0.0).astype(F32)
    tril = _iota((CHUNK, CHUNK), 0) >= _iota((CHUNK, CHUNK), 1)
    head = _iota((CHUNK, W_GRP), 1) >> 6
    return avg, tril, head


def _sgu_pre(za, avg):
    zg, t = _gelu(za)
    u, v = zg[:, :W_GRP], zg[:, W_GRP:]
    mu = _dot_hi(v, avg)
    vc = v - mu
    r = lax.rsqrt(_dot_hi(vc * vc, avg) + EPS)
    return t, u, vc * r, r


def _sgu_fwd(z, sgu_w, bias_full):
    s = z.shape[0]
    tm = min(s, 512)

    def body(za_ref, w_ref, bias_ref, ya_ref):
        avg, tril, head = _sgu_consts()
        _, u, vn, _ = _sgu_pre(za_ref[...], avg)
        wm = [jnp.where(tril, w_ref[h], 0.0).astype(BF16) for h in range(4)]
        vb = vn.astype(BF16)
        for n in range(tm // CHUNK):
            rows = slice(n * CHUNK, (n + 1) * CHUNK)
            mixed = bias_ref[...]
            for h in range(4):
                mixed = mixed + jnp.where(head == h, jnp.dot(wm[h], vb[rows], preferred_element_type=F32), 0.0)
            ya_ref[rows, :] = u[rows] * mixed

    return pl.pallas_call(
        body, name="sgu_fwd", grid=(s // tm,),
        in_specs=[pl.BlockSpec((tm, 2 * W_GRP), lambda i: (i, 0)), _full((4, CHUNK, CHUNK)), _full((CHUNK, W_GRP))],
        out_specs=pl.BlockSpec((tm, W_GRP), lambda i: (i, 0)),
        out_shape=jax.ShapeDtypeStruct((s, W_GRP), F32),
        compiler_params=_cp(dimension_semantics=("parallel",)),
    )(z, sgu_w, bias_full)


def _sgu_bwd(z, dya, sgu_w, bias_full):
    s = z.shape[0]
    tm = min(s, 512)
    nt = s // tm

    def body(za_ref, dya_ref, w_ref, bias_ref, dza_ref, dw_ref, db_ref, du_s, dvn_s):
        i = pl.program_id(0)

        @pl.when(i == 0)
        def _():
            dw_ref[...] = jnp.zeros_like(dw_ref)
            db_ref[...] = jnp.zeros_like(db_ref)

        avg, tril, head = _sgu_consts()
        za = za_ref[...]
        t, u, vn, r = _sgu_pre(za, avg)
        wm = [jnp.where(tril, w_ref[h], 0.0).astype(BF16) for h in range(4)]
        vb = vn.astype(BF16)
        dya = dya_ref[...]
        dw = [jnp.zeros((CHUNK, CHUNK), F32) for _ in range(4)]
        db = jnp.zeros((CHUNK, W_GRP), F32)
        for n in range(tm // CHUNK):
            rows = slice(n * CHUNK, (n + 1) * CHUNK)
            mixed = bias_ref[...]
            for h in range(4):
                mixed = mixed + jnp.where(head == h, jnp.dot(wm[h], vb[rows], preferred_element_type=F32), 0.0)
            dmix = dya[rows] * u[rows]
            du_s[rows, :] = dya[rows] * mixed
            db = db + dmix
            dmb = dmix.astype(BF16)
            dvn = jnp.zeros((CHUNK, W_GRP), F32)
            for h in range(4):
                dmh = jnp.where(head == h, dmix, 0.0)
                dw[h] = dw[h] + _dot_nt(dmh, vb[rows])
                dvn = dvn + jnp.where(head == h, _dot_tn(wm[h], dmb), 0.0)
            dvn_s[rows, :] = dvn
        for h in range(4):
            dw_ref[h] += jnp.where(tril, dw[h], 0.0)
        sel = ((_iota((W_GRP, CHUNK), 0) >> 6) == _iota((W_GRP, CHUNK), 1)).astype(F32)
        db_ref[...] += _dot_hi(db, sel)
        dvn = dvn_s[...]
        dv = r * (dvn - _dot_hi(dvn, avg) - vn * _dot_hi(dvn * vn, avg))
        dzg = jnp.concatenate([du_s[...], dv], axis=1)
        dza_ref[...] = dzg * _gelu_grad(za, t)

    return pl.pallas_call(
        body, name="sgu_bwd", grid=(nt,),
        in_specs=[pl.BlockSpec((tm, 2 * W_GRP), lambda i: (i, 0)), pl.BlockSpec((tm, W_GRP), lambda i: (i, 0)),
                  _full((4, CHUNK, CHUNK)), _full((CHUNK, W_GRP))],
        out_specs=[pl.BlockSpec((tm, 2 * W_GRP), lambda i: (i, 0)), _full((4, CHUNK, CHUNK)), _full((CHUNK, CHUNK))],
        out_shape=[jax.ShapeDtypeStruct((s, 2 * W_GRP), F32), jax.ShapeDtypeStruct((4, CHUNK, CHUNK), F32),
                   jax.ShapeDtypeStruct((CHUNK, CHUNK), F32)],
        scratch_shapes=[pltpu.VMEM((tm, W_GRP), F32), pltpu.VMEM((tm, W_GRP), F32)],
        compiler_params=_cp(dimension_semantics=("arbitrary",)),
    )(z, dya, sgu_w, bias_full)


def _shift_down(x, k):
    return jnp.where(_iota(x.shape, 0) < k, 0.0, pltpu.roll(x, k, 0))


def _shift_up(x, k):
    n = x.shape[0]
    return jnp.where(_iota(x.shape, 0) >= n - k, 0.0, pltpu.roll(x, n - k, 0))


def _by_pool_group(shape, v2, v4, v8, v16):
    col = _iota(shape, 1)
    return jnp.where(col < 64, v2, jnp.where(col < 128, v4, jnp.where(col < 192, v8, v16)))


def _pool_core(zb, pw2):
    s2 = zb + _shift_down(zb, 1)
    s4 = s2 + _shift_down(s2, 2)
    s8 = s4 + _shift_down(s4, 4)
    s16 = s8 + _shift_down(s8, 8)
    win = _by_pool_group(zb.shape, s2, s4, s8, s16)
    wlen = _by_pool_group(zb.shape, 2.0, 4.0, 8.0, 16.0)
    cnt = jnp.minimum((_iota(zb.shape, 0) + 1).astype(F32), wlen)
    p = win / cnt - zb
    wt = jnp.tile(pw2, (1, 4))
    wbd = jnp.where((_iota(wt.shape, 0) >> 6) == (_iota(wt.shape, 1) >> 6), wt, 0.0).astype(BF16)
    return p, cnt, wbd


def _conv_core(zc, cw):
    bg, cg, xh = zc[:, :W_GRP], zc[:, W_GRP:2 * W_GRP], zc[:, 2 * W_GRP:]
    y = cg * xh
    y1, y2 = _shift_down(y, 1), _shift_down(y, 2)
    out = cw[2:3, :] * y + cw[1:2, :] * y1 + cw[0:1, :] * y2
    return bg, cg, xh, y, y1, y2, out


def _poolconv_fwd(z, pw2, pscale, cw):
    s = z.shape[0]

    def body(zb_ref, zc_ref, pw_ref, ps_ref, cw_ref, yb_ref, yc_ref):
        p, _, wbd = _pool_core(zb_ref[...], pw_ref[...])
        yb_ref[...] = jnp.dot(p.astype(BF16), wbd, preferred_element_type=F32) * ps_ref[...]
        bg, _, _, _, _, _, out = _conv_core(zc_ref[...], cw_ref[...])
        yc_ref[...] = bg * out

    return pl.pallas_call(
        body, name="poolconv_fwd", grid=(1,),
        in_specs=[pl.BlockSpec((s, W_GRP), lambda i: (0, 2)), pl.BlockSpec((s, 3 * W_GRP), lambda i: (0, 1)),
                  _full((W_GRP, 64)), _full((1, W_GRP)), _full((3, W_GRP))],
        out_specs=[_full((s, W_GRP)), _full((s, W_GRP))],
        out_shape=[jax.ShapeDtypeStruct((s, W_GRP), F32)] * 2,
        compiler_params=_cp(dimension_semantics=("arbitrary",)),
    )(z, z, pw2, pscale, cw)


def _poolconv_bwd(z, dyb, dyc, pw2, pscale, cw):
    s = z.shape[0]

    def body(zb_ref, zc_ref, dyb_ref, dyc_ref, pw_ref, ps_ref, cw_ref, dzb_ref, dzc_ref, dw_ref, dps_ref, dcw_ref):
        zb = zb_ref[...]
        p, cnt, wbd = _pool_core(zb, pw_ref[...])
        pb = p.astype(BF16)
        out = jnp.dot(pb, wbd, preferred_element_type=F32)
        dyb = dyb_ref[...]
        dps_ref[...] = jnp.sum(dyb * out, axis=0, keepdims=True)
        dout = (dyb * ps_ref[...]).astype(BF16)
        dw = _dot_tn(pb, dout)
        dw_ref[...] = jnp.where((_iota(dw.shape, 0) >> 6) == (_iota(dw.shape, 1) >> 6), dw, 0.0)
        dp = lax.dot_general(dout, wbd, (((1,), (1,)), ((), ())), preferred_element_type=F32)
        dwin = dp / cnt
        t2 = dwin + _shift_up(dwin, 1)
        t4 = t2 + _shift_up(t2, 2)
        t8 = t4 + _shift_up(t4, 4)
        t16 = t8 + _shift_up(t8, 8)
        dzb_ref[...] = _by_pool_group(zb.shape, t2, t4, t8, t16) - dp

        cw = cw_ref[...]
        bg, cg, xh, y, y1, y2, out = _conv_core(zc_ref[...], cw)
        dyc = dyc_ref[...]
        dout = dyc * bg
        dcw_ref[...] = jnp.zeros_like(dcw_ref)
        dcw_ref[0:1, :] = jnp.sum(dout * y2, axis=0, keepdims=True)
        dcw_ref[1:2, :] = jnp.sum(dout * y1, axis=0, keepdims=True)
        dcw_ref[2:3, :] = jnp.sum(dout * y, axis=0, keepdims=True)
        dy = cw[2:3, :] * dout + cw[1:2, :] * _shift_up(dout, 1) + cw[0:1, :] * _shift_up(dout, 2)
        dzc_ref[...] = jnp.concatenate([dyc * out, dy * xh, dy * cg], axis=1)

    return pl.pallas_call(
        body, name="poolconv_bwd", grid=(1,),
        in_specs=[pl.BlockSpec((s, W_GRP), lambda i: (0, 2)), pl.BlockSpec((s, 3 * W_GRP), lambda i: (0, 1)),
                  _full((s, W_GRP)), _full((s, W_GRP)), _full((W_GRP, 64)), _full((1, W_GRP)), _full((3, W_GRP))],
        out_specs=[_full((s, W_GRP)), _full((s, 3 * W_GRP)), _full((W_GRP, W_GRP)), _full((1, W_GRP)), _full((8, W_GRP))],
        out_shape=[jax.ShapeDtypeStruct((s, W_GRP), F32), jax.ShapeDtypeStruct((s, 3 * W_GRP), F32),
                   jax.ShapeDtypeStruct((W_GRP, W_GRP), F32), jax.ShapeDtypeStruct((1, W_GRP), F32),
                   jax.ShapeDtypeStruct((8, W_GRP), F32)],
        compiler_params=_cp(dimension_semantics=("arbitrary",)),
    )(z, z, dyb, dyc, pw2, pscale, cw)


N_STATE = 1024
HALF_STATE = N_STATE // 2
HALF_CH = W_GRP // 2
N_SLAB = HALF_STATE // LANES


def _s5_disc(lre, lim, ldt):
    dt = jnp.exp(ldt)
    mag = jnp.exp(lre * dt)
    ang = lim * dt
    ar, ai = mag * jnp.cos(ang), mag * jnp.sin(ang)
    nr, ni = ar - 1.0, ai
    den = lre * lre + lim * lim
    kr = (nr * lre + ni * lim) / den
    ki = (ni * lre - nr * lim) / den
    return ar, ai, kr, ki


def _s5_mats(colp, br, bi, cr, ci):
    _, _, kr, ki = _s5_disc(colp[:, 0:1], colp[:, 1:2], colp[:, 2:3])
    bbr = kr * br - ki * bi
    bbi = kr * bi + ki * br
    bmask = (_iota((HALF_STATE, HALF_CH), 0) >> 6) == (_iota((HALF_STATE, HALF_CH), 1) >> 4)
    cmask = (_iota((HALF_CH, HALF_STATE), 0) >> 4) == (_iota((HALF_CH, HALF_STATE), 1) >> 6)
    btr = jnp.where(bmask, jnp.tile(bbr, (1, 8)), 0.0).astype(BF16)
    bti = jnp.where(bmask, jnp.tile(bbi, (1, 8)), 0.0).astype(BF16)
    ctr = jnp.where(cmask, jnp.tile(cr, (1, 8)), 0.0).astype(BF16)
    cti = jnp.where(cmask, jnp.tile(ci, (1, 8)), 0.0).astype(BF16)
    return kr, ki, btr, bti, ctr, cti, bmask, cmask


def _slab(q):
    return slice(q * LANES, (q + 1) * LANES)


def _cmul(ar, ai, br, bi):
    return ar * br - ai * bi, ar * bi + ai * br


def _sub_shift(x, k, up):
    row = _iota(x.shape, 0)
    if up:
        return jnp.where(row >= N_SEG - k, 0.0, pltpu.roll(x, N_SEG - k, 0))
    return jnp.where(row < k, 0.0, pltpu.roll(x, k, 0))


def _seg_rows(j):
    return pl.ds(pl.multiple_of(j * N_SEG, N_SEG), N_SEG)


def _interleave(src, dst, seg):
    def step(j, carry):
        dst[_seg_rows(j), :] = src[pl.ds(j, N_SEG, stride=seg), :]
        return carry
    lax.fori_loop(0, seg, step, 0)


def _deinterleave(src, dst, seg):
    def step(j, carry):
        dst[pl.ds(j, N_SEG, stride=seg), :] = src[_seg_rows(j), :]
        return carry
    lax.fori_loop(0, seg, step, 0)


def _scan(xr, xi, ar_row, ai_row, seg, reverse):
    nlog = int(math.log2(seg))
    assert (1 << nlog) == seg
    for q0 in range(0, N_SLAB, 4):
        qs = list(range(q0, q0 + 4))
        aq = [(jnp.broadcast_to(ar_row[:, _slab(q)], (N_SEG, LANES)),
               jnp.broadcast_to(ai_row[:, _slab(q)], (N_SEG, LANES))) for q in qs]
        zero = jnp.zeros((N_SEG, LANES), F32)

        def local(jj, carry, qs=qs, aq=aq):
            j = seg - 1 - jj if reverse else jj
            out = []
            for n, q in enumerate(qs):
                rows = _seg_rows(j)
                pr, pi = _cmul(aq[n][0], aq[n][1], carry[2 * n], carry[2 * n + 1])
                nr = pr + xr[q, rows, :]
                ni = pi + xi[q, rows, :]
                xr[q, rows, :] = nr
                xi[q, rows, :] = ni
                out += [nr, ni]
            return tuple(out)

        fin = lax.fori_loop(0, seg, local, (zero,) * 8)
        cins = []
        for n in range(4):
            er, ei = fin[2 * n], fin[2 * n + 1]
            pr, pi = aq[n]
            for _ in range(nlog):
                pr, pi = _cmul(pr, pi, pr, pi)
            yr, yi = er, ei
            for k in (1, 2, 4):
                sr, si = _cmul(pr, pi, _sub_shift(yr, k, reverse), _sub_shift(yi, k, reverse))
                yr, yi = yr + sr, yi + si
                pr, pi = _cmul(pr, pi, pr, pi)
            cins.append((_sub_shift(yr, 1, reverse), _sub_shift(yi, 1, reverse)))

        def fix(jj, carry, qs=qs, aq=aq, cins=cins):
            j = seg - 1 - jj if reverse else jj
            out = []
            for n, q in enumerate(qs):
                rows = _seg_rows(j)
                pwr, pwi = carry[2 * n], carry[2 * n + 1]
                cr, ci = _cmul(pwr, pwi, cins[n][0], cins[n][1])
                xr[q, rows, :] += cr
                xi[q, rows, :] += ci
                nr, ni = _cmul(pwr, pwi, aq[n][0], aq[n][1])
                out += [nr, ni]
            return tuple(out)

        lax.fori_loop(0, seg, fix, tuple(v for pair in aq for v in pair))


def _s5_forward_states(u, btr, bti, ar_row, ai_row, xr, xi, seg):
    ub = u.astype(BF16)
    for q in range(N_SLAB):
        xr[q] = _dot_nt(ub, btr[_slab(q), :])
        xi[q] = _dot_nt(ub, bti[_slab(q), :])
    _scan(xr, xi, ar_row, ai_row, seg, False)


def _s5_readout(u, xr, xi, ctr, cti, d):
    y = d * u
    for q in range(N_SLAB):
        y = y + _dot_nt(xr[q], ctr[:, _slab(q)]) - _dot_nt(xi[q], cti[:, _slab(q)])
    return y


def _s5_param_specs():
    return [pl.BlockSpec((3, HALF_STATE), lambda i: (0, i)), pl.BlockSpec((HALF_STATE, 3), lambda i: (i, 0)),
            pl.BlockSpec((HALF_STATE, 16), lambda i: (i, 0)), pl.BlockSpec((HALF_STATE, 16), lambda i: (i, 0)),
            pl.BlockSpec((HALF_CH, 64), lambda i: (i, 0)), pl.BlockSpec((HALF_CH, 64), lambda i: (i, 0)),
            pl.BlockSpec((1, HALF_CH), lambda i: (0, i))]


def _s5_core_fwd(z, sp):
    s = z.shape[0]
    seg = s // N_SEG

    def body(u_ref, rowp, colp, br, bi, cr, ci, d_ref, y_ref, xr, xi, us, ys):
        ar, ai, _, _ = _s5_disc(rowp[0:1, :], rowp[1:2, :], rowp[2:3, :])
        _, _, btr, bti, ctr, cti, _, _ = _s5_mats(colp[...], br[...], bi[...], cr[...], ci[...])
        _interleave(u_ref, us, seg)
        u = us[...]
        _s5_forward_states(u, btr, bti, ar, ai, xr, xi, seg)
        ys[...] = _s5_readout(u, xr, xi, ctr, cti, d_ref[...])
        _deinterleave(ys, y_ref, seg)

    return pl.pallas_call(
        body, name="s5_core_fwd", grid=(2,),
        in_specs=[pl.BlockSpec((s, HALF_CH), lambda i: (0, 12 + i))] + _s5_param_specs(),
        out_specs=pl.BlockSpec((s, HALF_CH), lambda i: (0, i)),
        out_shape=jax.ShapeDtypeStruct((s, W_GRP), F32),
        scratch_shapes=[pltpu.VMEM((N_SLAB, s, LANES), F32)] * 2 + [pltpu.VMEM((s, HALF_CH), F32)] * 2,
        compiler_params=_cp(dimension_semantics=("parallel",)),
    )(z, *sp)


def _s5_glu_fwd(y, gw, gb):
    s = y.shape[0]
    tm = _tm(s)

    def body(y_ref, gw_ref, gb_ref, o_ref):
        yg, _ = _gelu(y_ref[...])
        o_ref[...] = yg * jax.nn.sigmoid(_dot(yg, gw_ref[...]) + gb_ref[...])

    blk = pl.BlockSpec((tm, W_GRP), lambda i: (i, 0))
    return pl.pallas_call(
        body, name="s5_glu_fwd", grid=(s // tm,),
        in_specs=[blk, _full((W_GRP, W_GRP)), _full((1, W_GRP))], out_specs=blk,
        out_shape=jax.ShapeDtypeStruct((s, W_GRP), F32),
        compiler_params=_cp(dimension_semantics=("parallel",)),
    )(y, gw, gb)


def _s5_glu_bwd(y, dyd, gw, gb):
    s = y.shape[0]
    tm = _tm(s)

    def body(y_ref, dyd_ref, gw_ref, gb_ref, dy_ref, dgw_ref, dgb_ref):
        i = pl.program_id(0)

        @pl.when(i == 0)
        def _():
            dgw_ref[...] = jnp.zeros_like(dgw_ref)
            dgb_ref[...] = jnp.zeros_like(dgb_ref)

        y, gw, dyd = y_ref[...], gw_ref[...], dyd_ref[...]
        yg, t = _gelu(y)
        gate = jax.nn.sigmoid(_dot(yg, gw) + gb_ref[...])
        dlin = dyd * yg * gate * (1.0 - gate)
        dgw_ref[...] += _dot_tn(yg, dlin)
        dgb_ref[...] += jnp.sum(dlin, axis=0, keepdims=True)
        dy_ref[...] = (dyd * gate + _dot_nt(dlin, gw)) * _gelu_grad(y, t)

    blk = pl.BlockSpec((tm, W_GRP), lambda i: (i, 0))
    return pl.pallas_call(
        body, name="s5_glu_bwd", grid=(s // tm,),
        in_specs=[blk, blk, _full((W_GRP, W_GRP)), _full((1, W_GRP))],
        out_specs=[blk, _full((W_GRP, W_GRP)), _full((1, W_GRP))],
        out_shape=[jax.ShapeDtypeStruct((s, W_GRP), F32), jax.ShapeDtypeStruct((W_GRP, W_GRP), F32),
                   jax.ShapeDtypeStruct((1, W_GRP), F32)],
        compiler_params=_cp(dimension_semantics=("arbitrary",)),
    )(y, dyd, gw, gb)


def _s5_core_bwd(z, dy, sp):
    s = z.shape[0]
    seg = s // N_SEG

    def body(u_ref, dy_ref, rowp, colp, br_ref, bi_ref, cr_ref, ci_ref, d_ref,
             du_ref, dbr_ref, dbi_ref, dcr_ref, dci_ref, dd_ref, da_ref, dk_ref,
             xr, xi, gr, gi, us, dys):
        ar, ai, _, _ = _s5_disc(rowp[0:1, :], rowp[1:2, :], rowp[2:3, :])
        br, bi = br_ref[...], bi_ref[...]
        kr, ki, btr, bti, ctr, cti, bmask, cmask = _s5_mats(colp[...], br, bi, cr_ref[...], ci_ref[...])
        _interleave(u_ref, us, seg)
        _interleave(dy_ref, dys, seg)
        u = us[...]
        d = d_ref[...]
        _s5_forward_states(u, btr, bti, ar, ai, xr, xi, seg)

        dy = dys[...]
        dd_ref[...] = jnp.sum(dy * u, axis=0, keepdims=True)
        du = d * dy
        dyb = dy.astype(BF16)
        dctr, dcti = [], []
        for q in range(N_SLAB):
            gr[q] = jnp.dot(dyb, ctr[:, _slab(q)], preferred_element_type=F32)
            gi[q] = -jnp.dot(dyb, cti[:, _slab(q)], preferred_element_type=F32)
            dctr.append(_dot_tn(dyb, xr[q]))
            dcti.append(-_dot_tn(dyb, xi[q]))
        selp = ((_iota((HALF_STATE, 64), 0) & 63) == _iota((HALF_STATE, 64), 1)).astype(F32)
        dcr_ref[...] = _dot_hi(jnp.where(cmask, jnp.concatenate(dctr, axis=1), 0.0), selp)
        dci_ref[...] = _dot_hi(jnp.where(cmask, jnp.concatenate(dcti, axis=1), 0.0), selp)

        _scan(gr, gi, ar, -ai, seg, True)

        dar, dai = [], []
        for q in range(N_SLAB):
            def acc_step(j, carry, q=q):
                rows, prev = _seg_rows(j), _seg_rows(j - 1)
                g_r, g_i, p_r, p_i = gr[q, rows, :], gi[q, rows, :], xr[q, prev, :], xi[q, prev, :]
                return carry[0] + g_r * p_r + g_i * p_i, carry[1] - g_r * p_i + g_i * p_r
            first, last = _seg_rows(0), _seg_rows(seg - 1)
            p_r, p_i = _sub_shift(xr[q, last, :], 1, False), _sub_shift(xi[q, last, :], 1, False)
            g_r, g_i = gr[q, first, :], gi[q, first, :]
            s_r, s_i = lax.fori_loop(1, seg, acc_step, (g_r * p_r + g_i * p_i, -g_r * p_i + g_i * p_r))
            dar.append(jnp.sum(s_r, axis=0, keepdims=True))
            dai.append(jnp.sum(s_i, axis=0, keepdims=True))
        da_ref[...] = jnp.zeros_like(da_ref)
        da_ref[0:1, :] = jnp.concatenate(dar, axis=1)
        da_ref[1:2, :] = jnp.concatenate(dai, axis=1)

        ub = u.astype(BF16)
        dbtr, dbti = [], []
        for q in range(N_SLAB):
            g_r, g_i = gr[q].astype(BF16), gi[q].astype(BF16)
            du = du + jnp.dot(g_r, btr[_slab(q), :], preferred_element_type=F32) \
                + jnp.dot(g_i, bti[_slab(q), :], preferred_element_type=F32)
            dbtr.append(_dot_tn(g_r, ub))
            dbti.append(_dot_tn(g_i, ub))
        us[...] = du
        _deinterleave(us, du_ref, seg)
        selc =((_iota((HALF_CH, 16), 0) & 15) == _iota((HALF_CH, 16), 1)).astype(F32)
        dbbr = _dot_hi(jnp.where(bmask, jnp.concatenate(dbtr, axis=0), 0.0), selc)
        dbbi = _dot_hi(jnp.where(bmask, jnp.concatenate(dbti, axis=0), 0.0), selc)
        dbr_ref[...] = kr * dbbr + ki * dbbi
        dbi_ref[...] = kr * dbbi - ki * dbbr
        dk_ref[:, 0:1] = jnp.sum(dbbr * br + dbbi * bi, axis=1, keepdims=True)
        dk_ref[:, 1:2] = jnp.sum(dbbi * br - dbbr * bi, axis=1, keepdims=True)

    half = pl.BlockSpec((s, HALF_CH), lambda i: (0, i))
    return pl.pallas_call(
        body, name="s5_core_bwd", grid=(2,),
        in_specs=[pl.BlockSpec((s, HALF_CH), lambda i: (0, 12 + i)), half] + _s5_param_specs(),
        out_specs=[half, pl.BlockSpec((HALF_STATE, 16), lambda i: (i, 0)), pl.BlockSpec((HALF_STATE, 16), lambda i: (i, 0)),
                   pl.BlockSpec((HALF_CH, 64), lambda i: (i, 0)), pl.BlockSpec((HALF_CH, 64), lambda i: (i, 0)),
                   pl.BlockSpec((1, HALF_CH), lambda i: (0, i)), pl.BlockSpec((8, HALF_STATE), lambda i: (0, i)),
                   pl.BlockSpec((HALF_STATE, 2), lambda i: (i, 0))],
        out_shape=[jax.ShapeDtypeStruct((s, W_GRP), F32), jax.ShapeDtypeStruct((N_STATE, 16), F32),
                   jax.ShapeDtypeStruct((N_STATE, 16), F32), jax.ShapeDtypeStruct((W_GRP, 64), F32),
                   jax.ShapeDtypeStruct((W_GRP, 64), F32), jax.ShapeDtypeStruct((1, W_GRP), F32),
                   jax.ShapeDtypeStruct((8, N_STATE), F32), jax.ShapeDtypeStruct((N_STATE, 2), F32)],
        scratch_shapes=[pltpu.VMEM((N_SLAB, s, LANES), F32)] * 4 + [pltpu.VMEM((s, HALF_CH), F32)] * 2,
        compiler_params=_cp(dimension_semantics=("parallel",)),
    )(z, dy, *sp)


def _s5_param_bwd(lre, lim, ldt, da_r, da_i, dk_r, dk_i):
    n = lre.shape[0]

    def body(lre_ref, lim_ref, ldt_ref, dar_ref, dai_ref, dkr_ref, dki_ref, o_re, o_im, o_dt):
        lre, lim, ldt = lre_ref[...], lim_ref[...], ldt_ref[...]
        dt = jnp.exp(ldt)
        ar, ai, kr, ki = _s5_disc(lre, lim, ldt)
        mag = jnp.exp(lre * dt)
        den = lre * lre + lim * lim
        dkr, dki = dkr_ref[...], dki_ref[...]
        nr, ni = ar - 1.0, ai
        d_ar = dar_ref[...] + (dkr * lre - dki * lim) / den
        d_ai = dai_ref[...] + (dkr * lim + dki * lre) / den
        kk = (kr * dkr + ki * dki) * 2.0 / den
        d_lre = (dkr * nr + dki * ni) / den - kk * lre
        d_lim = (dkr * ni - dki * nr) / den - kk * lim
        d_mag = (d_ar * ar + d_ai * ai) / mag
        d_ang = d_ai * ar - d_ar * ai
        o_re[...] = d_lre + d_mag * mag * dt
        o_im[...] = d_lim + d_ang * dt
        o_dt[...] = jnp.sum((d_mag * mag * lre + d_ang * lim) * dt, axis=1, keepdims=True)

    return pl.pallas_call(
        body, name="s5_param_bwd",
        out_shape=[jax.ShapeDtypeStruct((n, 64), F32), jax.ShapeDtypeStruct((n, 64), F32),
                   jax.ShapeDtypeStruct((n, 1), F32)],
    )(lre, lim, ldt, da_r, da_i, dk_r, dk_i)


def _loss_head(x, fg, target):
    s, d = x.shape
    tm = _tm(s)

    def body(x_ref, fg_ref, t_ref, loss_ref, dx_ref, dfg_ref):
        i = pl.program_id(0)

        @pl.when(i == 0)
        def _():
            loss_ref[...] = jnp.zeros_like(loss_ref)
            dfg_ref[...] = jnp.zeros_like(dfg_ref)

        xv, g = x_ref[...], fg_ref[...]
        r = lax.rsqrt(jnp.mean(xv * xv, axis=-1, keepdims=True) + EPS)
        xh = xv * r
        err = xh * g - t_ref[...]
        loss_ref[...] += 0.5 * jnp.sum(jnp.mean(err * err, axis=-1, keepdims=True), axis=0, keepdims=True)
        dy = err * (1.0 / d)
        dfg_ref[...] += jnp.sum(dy * xh, axis=0, keepdims=True)
        dxh = dy * g
        dx_ref[...] = r * (dxh - xh * jnp.mean(dxh * xh, axis=-1, keepdims=True))

    row = pl.BlockSpec((tm, d), lambda i: (i, 0))
    return pl.pallas_call(
        body, name="loss_head", grid=(s // tm,),
        in_specs=[row, _full((1, d)), row], out_specs=[_full((1, 1)), row, _full((1, d))],
        out_shape=[jax.ShapeDtypeStruct((1, 1), F32), jax.ShapeDtypeStruct((s, d), F32),
                   jax.ShapeDtypeStruct((1, d), F32)],
        compiler_params=_cp(dimension_semantics=("arbitrary",)),
    )(x, fg, target)


ADA_TN = 384


def _cond_fwd(cact, ada_w, ada_b_loc):
    nl, d, n = ada_w.shape

    def body(c_ref, w_ref, b_ref, o_ref):
        o_ref[...] = _dot(c_ref[...], w_ref[...]) + b_ref[...]

    return pl.pallas_call(
        body, name="cond_fwd", grid=(nl, n // ADA_TN),
        in_specs=[_full((N_DEV, d)), pl.BlockSpec((None, d, ADA_TN), lambda l, j: (l, 0, j)),
                  pl.BlockSpec((None, 1, ADA_TN), lambda l, j: (l, 0, j))],
        out_specs=pl.BlockSpec((None, N_DEV, ADA_TN), lambda l, j: (l, 0, j)),
        out_shape=jax.ShapeDtypeStruct((nl, N_DEV, n), F32),
        compiler_params=_cp(dimension_semantics=("parallel", "parallel")),
    )(cact, ada_w, ada_b_loc)


ELEMENTWISE_BLOCK_BYTES = 1 << 20


def _row_tile(r, c, itemsize=4):
    best = None
    for t in range(8, r + 1, 8):
        if r % t == 0 and t * c * itemsize <= ELEMENTWISE_BLOCK_BYTES:
            best = t
    return best if best is not None else r


def _adamw_math(w, g, m, v):
    m = ADAM_B1 * m + (1.0 - ADAM_B1) * g
    v = ADAM_B2 * v + (1.0 - ADAM_B2) * (g * g)
    m_hat = m / (1.0 - ADAM_B1 ** ADAM_STEP)
    v_hat = v / (1.0 - ADAM_B2 ** ADAM_STEP)
    delta = -ADAM_LR * (m_hat / (jnp.sqrt(v_hat) + ADAM_EPS) + ADAM_WD * w)
    return delta, m, v


def _ada_w_update(cact, dcond_loc, w, m, v):
    nl, d, n = w.shape

    def body(c_ref, dc_ref, w_ref, m_ref, v_ref, g_out, d_out, m_out, v_out):
        g = _dot_tn(c_ref[...], dc_ref[...])
        g_out[...] = g
        d_out[...], m_out[...], v_out[...] = _adamw_math(w_ref[...], g, m_ref[...], v_ref[...])

    blk = pl.BlockSpec((None, d, ADA_TN), lambda l, j: (l, 0, j))
    return pl.pallas_call(
        body, name="ada_w_update", grid=(nl, n // ADA_TN),
        in_specs=[_full((N_DEV, d)), pl.BlockSpec((None, N_DEV, ADA_TN), lambda l, j: (l, 0, j)), blk, blk, blk],
        out_specs=[blk] * 4, out_shape=[jax.ShapeDtypeStruct((nl, d, n), F32)] * 4,
        compiler_params=_cp(dimension_semantics=("parallel", "parallel")),
    )(cact, dcond_loc, w, m, v)


def _place():
    x, y, c = lax.axis_index("x"), lax.axis_index("y"), lax.axis_index("c")
    chips = [(1 - x, y), (x, 1 - y), (1 - x, 1 - y)]
    return x, y, c, chips


def _remote(src, dst, send_sem, recv_sem, to):
    return pltpu.make_async_remote_copy(src_ref=src, dst_ref=dst, send_sem=send_sem, recv_sem=recv_sem,
                                        device_id=to, device_id_type=MESH_ID)


def _sems(n):
    return [pltpu.SemaphoreType.DMA((n,)), pltpu.SemaphoreType.DMA((n,))]


def _all_gather8(v, name):
    r, cdim = v.shape

    def body(x_ref, out_ref, stage, send_sems, recv_sems):
        x, y, c, chips = _place()
        sibling = (x, y, 1 - c)

        def slot(px, py, pc):
            return out_ref.at[4 * px + 2 * py + pc]

        first = [_remote(x_ref, slot(x, y, c), send_sems.at[0], recv_sems.at[0], sibling)]
        first += [_remote(x_ref, slot(x, y, c), send_sems.at[1 + j], recv_sems.at[1 + j], (*chip, c))
                  for j, chip in enumerate(chips)]
        for cp in first:
            cp.start()
        pltpu.sync_copy(x_ref, stage)
        pltpu.sync_copy(stage, slot(x, y, c))
        passed = []
        for j, chip in enumerate(chips):
            blk = slot(*chip, c)
            _remote(blk, blk, send_sems.at[1 + j], recv_sems.at[1 + j], (x, y, c)).wait_recv()
            fw = _remote(blk, blk, send_sems.at[4 + j], recv_sems.at[4 + j], sibling)
            fw.start()
            passed.append(fw)
        blk = slot(x, y, 1 - c)
        _remote(blk, blk, send_sems.at[0], recv_sems.at[0], (x, y, c)).wait_recv()
        for j, chip in enumerate(chips):
            blk = slot(*chip, 1 - c)
            _remote(blk, blk, send_sems.at[4 + j], recv_sems.at[4 + j], (x, y, c)).wait_recv()
        for cp in first + passed:
            cp.wait_send()

    return pl.pallas_call(
        body, name=name, out_shape=jax.ShapeDtypeStruct((N_DEV, r, cdim), v.dtype),
        in_specs=[ANY], out_specs=ANY,
        scratch_shapes=[pltpu.VMEM((r, cdim), v.dtype)] + _sems(7),
        compiler_params=_cp(),
    )(v)


def _gather_first_copies():
    def make(refs, send_sems, recv_sems):
        x, y, c, chips = _place()
        mine = refs[0].at[4 * x + 2 * y + c]
        to = [(x, y, 1 - c)] + [(*chip, c) for chip in chips]
        return [_remote(mine, mine, send_sems.at[k], recv_sems.at[k], dev) for k, dev in enumerate(to)]
    return make


def _gather_pass_on(buf, name):
    def body(in_ref, out_ref, send_sems, recv_sems):
        x, y, c, chips = _place()
        passed = []
        for j, chip in enumerate(chips):
            blk = out_ref.at[4 * chip[0] + 2 * chip[1] + c]
            fw = _remote(blk, blk, send_sems.at[j], recv_sems.at[j], (x, y, 1 - c))
            fw.start()
            passed.append(fw)
        for j, chip in enumerate(chips):
            blk = out_ref.at[4 * chip[0] + 2 * chip[1] + 1 - c]
            _remote(blk, blk, send_sems.at[j], recv_sems.at[j], (x, y, c)).wait_recv()
        for fw in passed:
            fw.wait_send()

    return pl.pallas_call(
        body, name=name, out_shape=jax.ShapeDtypeStruct(buf.shape, buf.dtype),
        in_specs=[ANY], out_specs=ANY, input_output_aliases={0: 0}, scratch_shapes=_sems(3),
    )(buf)


def _place_weights(ws, layer, kidx, after):
    steps = 4
    shapes, in_specs, out_specs = [], [], []
    for w, kind in zip(ws, BIG_KINDS):
        _, a, b = w.shape
        in_specs.append(pl.BlockSpec((None, a // steps, b), lambda i, k: (layer, i, 0)))
        if kind == "col":
            shapes.append((2, a, 2 * b))
            out_specs.append(pl.BlockSpec((None, a // steps, b), lambda i, k: (k[0] // 2, i, k[0] % 2)))
        else:
            shapes.append((N_CHIP, a, b))
            out_specs.append(pl.BlockSpec((None, a // steps, b), lambda i, k: (k[0], i, 0)))

    def body(k_ref, *refs):
        outs = refs[len(ws) + 1:]
        for t in range(len(ws)):
            outs[t][...] = refs[t][...].astype(BF16)

    return pl.pallas_call(
        body, name="place_weights", out_shape=[jax.ShapeDtypeStruct(s, BF16) for s in shapes],
        grid_spec=pltpu.PrefetchScalarGridSpec(num_scalar_prefetch=1, grid=(steps,), in_specs=in_specs + [ANY],
                                               out_specs=out_specs),
        compiler_params=_cp(dimension_semantics=("parallel",)),
    )(kidx, *ws, after)


HBM = pl.BlockSpec(memory_space=pltpu.HBM)
SEM = pl.BlockSpec(memory_space=pltpu.SEMAPHORE)
EFFECT = pltpu.SideEffectType.DATAFLOW_SIDE_EFFECTING


def _weight_block(ref, kind, k, h):
    if kind == "col":
        ncol = ref.shape[3] // 2
        return ref.at[k // 2, h, :, pl.ds(pl.multiple_of((k % 2) * ncol, LANES), ncol)]
    return ref.at[k, h]


def _in_hbm(a):
    return pltpu.with_memory_space_constraint(a, pltpu.HBM)


def _weight_send_start(placed, kinds, name):
    nt = len(placed)

    def body(*refs):
        send_sems, recv_sems = refs[nt], refs[nt + 1]
        dst = refs[nt + 2:2 * nt + 2]
        token = refs[2 * nt + 2]
        x, y, c, chips = _place()
        kme = 2 * x + y
        for t in range(nt):
            for j, chip in enumerate(chips):
                own = _weight_block(dst[t], kinds[t], kme, c)
                _remote(own, own, send_sems.at[3 * t + j], recv_sems.at[3 * t + j], (*chip, c)).start()
        token[...] = jnp.zeros_like(token)

    return pl.pallas_call(
        body, name=name,
        out_shape=(pltpu.SemaphoreType.DMA((3 * nt,)), pltpu.SemaphoreType.DMA((3 * nt,)),
                   *[pltpu.HBM(a.shape, a.dtype) for a in placed], jax.ShapeDtypeStruct((8, LANES), F32)),
        in_specs=[HBM] * nt, out_specs=(SEM, SEM, *[HBM] * nt, pl.BlockSpec(memory_space=pltpu.VMEM)),
        input_output_aliases={t: 2 + t for t in range(nt)},
        compiler_params=pltpu.CompilerParams(has_side_effects=EFFECT),
    )(*[_in_hbm(a) for a in placed])


def _weight_send_wait(send_sems, recv_sems, arrays, kinds, after, name):
    nt = len(arrays)

    def body(*refs):
        arr = refs[:nt]
        send_sems, recv_sems = refs[nt], refs[nt + 1]
        x, y, c, chips = _place()
        kme = 2 * x + y
        for t in range(nt):
            for j, chip in enumerate(chips):
                own = _weight_block(arr[t], kinds[t], kme, c)
                got = _weight_block(arr[t], kinds[t], 2 * chip[0] + chip[1], c)
                cp = _remote(own, got, send_sems.at[3 * t + j], recv_sems.at[3 * t + j], (*chip, c))
                cp.wait_send()
                cp.wait_recv()

    return pl.pallas_call(
        body, name=name, out_shape=[pltpu.HBM(a.shape, a.dtype) for a in arrays],
        in_specs=[HBM] * nt + [SEM, SEM, ANY], out_specs=[HBM] * nt,
        input_output_aliases={t: t for t in range(nt)},
        compiler_params=pltpu.CompilerParams(has_side_effects=EFFECT),
    )(*arrays, send_sems, recv_sems, after)


def _forward_copies(kinds):
    def make(refs, send_sems, recv_sems):
        x, y, c, chips = _place()
        cps = []
        for t in range(len(kinds)):
            for j, chip in enumerate(chips):
                blk = _weight_block(refs[t], kinds[t], 2 * chip[0] + chip[1], c)
                cps.append(_remote(blk, blk, send_sems.at[3 * t + j], recv_sems.at[3 * t + j], (x, y, 1 - c)))
        return cps
    return make


def _split_start(name, arrays, n_copies, make_copies):
    na = len(arrays)

    def body(*refs):
        send_sems, recv_sems = refs[na], refs[na + 1]
        for cp in make_copies(refs[na + 2:2 * na + 2], send_sems, recv_sems):
            cp.start()
        token = refs[2 * na + 2]
        token[...] = jnp.zeros_like(token)

    return pl.pallas_call(
        body, name=name,
        out_shape=(pltpu.SemaphoreType.DMA((n_copies,)), pltpu.SemaphoreType.DMA((n_copies,)),
                   *[pltpu.HBM(a.shape, a.dtype) for a in arrays], jax.ShapeDtypeStruct((8, LANES), F32)),
        in_specs=[HBM] * na, out_specs=(SEM, SEM, *[HBM] * na, pl.BlockSpec(memory_space=pltpu.VMEM)),
        input_output_aliases={t: 2 + t for t in range(na)},
        compiler_params=pltpu.CompilerParams(has_side_effects=EFFECT),
    )(*[_in_hbm(a) for a in arrays])


def _split_wait(name, started, make_copies, after):
    send_sems, recv_sems, *arrays, _ = started
    na = len(arrays)

    def body(*refs):
        send_sems, recv_sems = refs[na], refs[na + 1]
        for cp in make_copies(refs[:na], send_sems, recv_sems):
            cp.wait_send()
            cp.wait_recv()

    return pl.pallas_call(
        body, name=name, out_shape=[pltpu.HBM(a.shape, a.dtype) for a in arrays],
        in_specs=[HBM] * na + [SEM, SEM, ANY], out_specs=[HBM] * na,
        input_output_aliases={t: t for t in range(na)},
        compiler_params=pltpu.CompilerParams(has_side_effects=EFFECT),
    )(*arrays, send_sems, recv_sems, after)


def _exchange_copies(nt):
    def make(refs, send_sems, recv_sems):
        x, y, c, _ = _place()
        return [_remote(refs[t].at[:, 1 - c], refs[nt + t], send_sems.at[t], recv_sems.at[t], (x, y, 1 - c))
                for t in range(nt)]
    return make


def _sibling_exchange_start(views, name):
    lands = [lax.empty((v.shape[0],) + v.shape[2:], v.dtype) for v in views]
    return _split_start(name, list(views) + lands, len(views), _exchange_copies(len(views)))


def _sibling_exchange_wait(started, after, name):
    nt = (len(started) - 3) // 2
    outs = _split_wait(name, started, _exchange_copies(nt), after)
    return outs[:nt], outs[nt:]


def _scatter_copies(src, land, kinds, send_sems, recv_sems):
    x, y, c, chips = _place()
    cps = []
    for t in range(len(src)):
        for j, chip in enumerate(chips):
            k = 2 * chip[0] + chip[1]
            if kinds[t] == "col":
                ncol = land[t].shape[2]
                win = src[t].at[k // 2, :, pl.ds(pl.multiple_of((k % 2) * ncol, LANES), ncol)]
            else:
                win = src[t].at[k]
            cps.append(_remote(win, land[t].at[j], send_sems.at[3 * t + j], recv_sems.at[3 * t + j], (*chip, c)))
    return cps


def _chip_scatter_start(parts, kinds, name):
    nt = len(parts)
    shapes = []
    for p, kind in zip(parts, kinds):
        shapes.append((3, p.shape[1], p.shape[2] // 2) if kind == "col" else (3,) + p.shape[1:])

    def body(*refs):
        send_sems, recv_sems = refs[2 * nt], refs[2 * nt + 1]
        src, land = refs[2 * nt + 2:3 * nt + 2], refs[3 * nt + 2:4 * nt + 2]
        token = refs[4 * nt + 2]
        for cp in _scatter_copies(src, land, kinds, send_sems, recv_sems):
            cp.start()
        token[...] = jnp.zeros_like(token)

    lands = [lax.empty(s, BF16) for s in shapes]
    return pl.pallas_call(
        body, name=name,
        out_shape=(pltpu.SemaphoreType.DMA((3 * nt,)), pltpu.SemaphoreType.DMA((3 * nt,)),
                   *[pltpu.HBM(a.shape, a.dtype) for a in parts], *[pltpu.HBM(s, BF16) for s in shapes],
                   jax.ShapeDtypeStruct((8, LANES), F32)),
        in_specs=[HBM] * (2 * nt), out_specs=(SEM, SEM, *[HBM] * (2 * nt), pl.BlockSpec(memory_space=pltpu.VMEM)),
        input_output_aliases={t: 2 + t for t in range(2 * nt)},
        compiler_params=pltpu.CompilerParams(has_side_effects=EFFECT),
    )(*[_in_hbm(a) for a in parts], *[_in_hbm(a) for a in lands])


def _chip_scatter_wait(send_sems, recv_sems, parts, lands, kinds, after, name):
    nt = len(parts)

    def body(*refs):
        src, land = refs[:nt], refs[nt:2 * nt]
        send_sems, recv_sems = refs[2 * nt], refs[2 * nt + 1]
        for cp in _scatter_copies(src, land, kinds, send_sems, recv_sems):
            cp.wait_send()
            cp.wait_recv()

    outs = pl.pallas_call(
        body, name=name, out_shape=[pltpu.HBM(a.shape, a.dtype) for a in list(parts) + list(lands)],
        in_specs=[HBM] * (2 * nt) + [SEM, SEM, ANY], out_specs=[HBM] * (2 * nt),
        input_output_aliases={t: t for t in range(2 * nt)},
        compiler_params=pltpu.CompilerParams(has_side_effects=EFFECT),
    )(*parts, *lands, send_sems, recv_sems, after)
    return outs[:nt], outs[nt:]


def _share_copies(nt):
    def make(refs, send_sems, recv_sems):
        x, y, c, _ = _place()
        return [_remote(refs[t].at[c], refs[t].at[c], send_sems.at[t], recv_sems.at[t], (x, y, 1 - c))
                for t in range(nt)]
    return make


def _sibling_share_start(fulls, name):
    return _split_start(name, list(fulls), len(fulls), _share_copies(len(fulls)))


def _sibling_share_wait(started, after, name):
    return _split_wait(name, started, _share_copies(len(started) - 3), after)


SUM_STEPS = 4


def _pair_sum(views, lands, ck):
    nt = len(views)
    in_specs, out_specs, shapes = [], [], []
    for v in views:
        b, _, r, cc = v.shape
        per = SUM_STEPS // b
        tr = r // per
        in_specs.append(pl.BlockSpec((None, None, tr, cc), lambda i, s, per=per: (i // per, s[0], i % per, 0)))
        out_specs.append(pl.BlockSpec((None, tr, cc), lambda i, s, per=per: (i // per, i % per, 0)))
        shapes.append((b, r, cc))
    in_specs = in_specs + out_specs

    def body(s_ref, *refs):
        for t in range(nt):
            refs[2 * nt + t][...] = (refs[t][...].astype(F32) + refs[nt + t][...].astype(F32)).astype(BF16)

    return pl.pallas_call(
        body, name="grad_pair_sum", out_shape=[jax.ShapeDtypeStruct(s, BF16) for s in shapes],
        grid_spec=pltpu.PrefetchScalarGridSpec(num_scalar_prefetch=1, grid=(SUM_STEPS,), in_specs=in_specs,
                                               out_specs=out_specs),
        compiler_params=_cp(dimension_semantics=("parallel",)),
    )(ck, *views, *lands)


def _chip_sum(parts, lands, kinds, ck):
    nt = len(parts)
    steps = 2
    in_own, in_land, out_specs, shapes = [], [], [], []
    for ld, kind in zip(lands, kinds):
        _, r, cc = ld.shape
        tr = r // steps
        if kind == "col":
            in_own.append(pl.BlockSpec((None, tr, cc), lambda i, s: (s[1] // 2, i, s[1] % 2)))
        else:
            in_own.append(pl.BlockSpec((None, tr, cc), lambda i, s: (s[1], i, 0)))
        in_land.append(pl.BlockSpec((3, tr, cc), lambda i, s: (0, i, 0)))
        out_specs.append(pl.BlockSpec((None, tr, cc), lambda i, s: (s[0], i, 0)))
        shapes.append((2, r, cc))

    def body(s_ref, *refs):
        for t in range(nt):
            acc = refs[t][...].astype(F32)
            for j in range(3):
                acc = acc + refs[nt + t][j].astype(F32)
            refs[2 * nt + t][...] = acc

    return pl.pallas_call(
        body, name="grad_chip_sum", out_shape=[jax.ShapeDtypeStruct(s, F32) for s in shapes],
        grid_spec=pltpu.PrefetchScalarGridSpec(num_scalar_prefetch=1, grid=(steps,), in_specs=in_own + in_land,
                                               out_specs=out_specs),
        compiler_params=_cp(dimension_semantics=("parallel",)),
    )(ck, *parts, *lands)


def _sum8(g):
    _, r, cc = g.shape
    tr = _row_tile(r, N_DEV * cc)

    def body(g_ref, o_ref):
        acc = g_ref[0].astype(F32)
        for d in range(1, N_DEV):
            acc = acc + g_ref[d].astype(F32)
        o_ref[...] = acc

    return pl.pallas_call(
        body, name="small_grad_sum", grid=(r // tr,),
        in_specs=[pl.BlockSpec((N_DEV, tr, cc), lambda i: (0, i, 0))],
        out_specs=pl.BlockSpec((tr, cc), lambda i: (i, 0)),
        out_shape=jax.ShapeDtypeStruct((r, cc), F32),
        compiler_params=_cp(dimension_semantics=("parallel",)),
    )(g)


def _silu_rows(c):
    def body(c_ref, o_ref):
        v = c_ref[...]
        o_ref[...] = v * jax.nn.sigmoid(v)

    return pl.pallas_call(body, name="cond_silu", out_shape=jax.ShapeDtypeStruct(c.shape, F32))(c)


def _pack(arrays):
    rows = []
    for a in arrays:
        flat = a.reshape(-1)
        rows.append(jnp.pad(flat, (0, (-flat.shape[0]) % (8 * LANES))).reshape(-1, LANES))
    n = sum(r.shape[0] for r in rows)
    if n % 256:
        rows.append(jnp.zeros((256 - n % 256, LANES), rows[0].dtype))
    return jnp.concatenate(rows, axis=0)


def _unpack(packed, shapes):
    out, off = [], 0
    for s in shapes:
        n = math.prod(s)
        nr = 8 * -(-n // (8 * LANES))
        out.append(packed[off:off + nr].reshape(-1)[:n].reshape(s))
        off += nr
    return out


def _as_rows(a):
    return a.reshape(1, -1) if a.ndim == 1 else a.reshape(-1, a.shape[-1])


def _adamw_many(ws, gs, ms, vs, name, steps=1):
    nt = len(ws)

    def body(*refs):
        for t in range(nt):
            w_ref, g_ref, m_ref, v_ref = (refs[k * nt + t] for k in range(4))
            d, m, v = _adamw_math(w_ref[...], g_ref[...], m_ref[...], v_ref[...])
            refs[4 * nt + t][...] = d
            refs[5 * nt + t][...] = m
            refs[6 * nt + t][...] = v

    shapes = [jax.ShapeDtypeStruct(a.shape, F32) for a in ws]
    if steps == 1:
        outs = pl.pallas_call(body, name=name, out_shape=shapes * 3, compiler_params=_cp())(*ws, *gs, *ms, *vs)
    else:
        specs = [pl.BlockSpec((a.shape[0] // steps, a.shape[1]), lambda i: (i, 0)) for a in ws]
        outs = pl.pallas_call(
            body, name=name, grid=(steps,), in_specs=specs * 4, out_specs=specs * 3, out_shape=shapes * 3,
            compiler_params=_cp(dimension_semantics=("parallel",)),
        )(*ws, *gs, *ms, *vs)
    return outs[:nt], outs[nt:2 * nt], outs[2 * nt:]


def _exchange_big_grads(grads, kinds, layer):
    views = []
    for g, kind in zip(grads, kinds):
        if kind == "col":
            views.append(g.reshape(2, 2, g.shape[1] // 2, g.shape[2]))
        else:
            views.append(g.reshape(N_CHIP, 2, g.shape[0] // (2 * N_CHIP), g.shape[1]))
    return _sibling_exchange_start(views, "grad_exchange_start_%d" % layer)


def _scatter_big_grads(exchanged, kinds, ck, after, layer):
    views, lands = _sibling_exchange_wait(exchanged, after, "grad_exchange_wait_%d" % layer)
    parts = _pair_sum(views, lands, ck)
    return _chip_scatter_start(parts, kinds, "grad_scatter_start_%d" % layer)


def _finish_big_grads(started, kinds, ck, after, layer):
    nt = len(kinds)
    send_sems, recv_sems = started[0], started[1]
    parts, lands = started[2:2 + nt], started[2 + nt:2 + 2 * nt]
    parts, lands = _chip_scatter_wait(send_sems, recv_sems, parts, lands, kinds, after, "grad_scatter_wait_%d" % layer)
    return _sibling_share_start(_chip_sum(parts, lands, kinds, ck), "grad_share_start_%d" % layer)


def _adamw_layer(ws, gs, ms, vs, stacks, layer, name, steps):
    nt = len(ws)
    stacks = [s if s is not None else tuple(lax.empty(w.shape, F32) for _ in range(4)) for s, w in zip(stacks, ws)]

    def body(*refs):
        for t in range(nt):
            w_ref, g_ref, m_ref, v_ref = (refs[k * nt + t] for k in range(4))
            outs = refs[8 * nt + 4 * t:8 * nt + 4 * t + 4]
            g = g_ref[...]
            outs[0][...] = g
            outs[1][...], outs[2][...], outs[3][...] = _adamw_math(w_ref[...], g, m_ref[...], v_ref[...])

    in_specs, g_specs, out_specs = [], [], []
    for w in ws:
        _, r, c = w.shape
        in_specs.append(pl.BlockSpec((None, r // steps, c), lambda i: (layer, i, 0)))
        g_specs.append(pl.BlockSpec((r // steps, c), lambda i: (i, 0)))
        out_specs += [pl.BlockSpec((None, r // steps, c), lambda i: (layer, i, 0))] * 4
    in_specs = in_specs + g_specs + in_specs * 2 + [ANY] * (4 * nt)
    flat = [a for s in stacks for a in s]
    outs = pl.pallas_call(
        body, name=name, grid=(steps,), in_specs=in_specs, out_specs=out_specs,
        out_shape=[jax.ShapeDtypeStruct(a.shape, F32) for a in flat],
        input_output_aliases={4 * nt + k: k for k in range(4 * nt)},
        compiler_params=_cp(dimension_semantics=("parallel",)),
    )(*ws, *gs, *ms, *vs, *flat)
    return [tuple(outs[4 * t:4 * t + 4]) for t in range(nt)]


SMALL_NAMES = ["ada_b", "norm1_g", "norm2_g", "sgu_w", "sgu_b", "pool_w", "pool_scale", "conv_w", "s5_lambda_re",
               "s5_lambda_im", "s5_b_re", "s5_b_im", "s5_c_re", "s5_c_im", "s5_d", "s5_log_dt", "s5_glu_w", "s5_glu_b",
               "mix_norm_g", "norm3_g", "final_norm_g"]
BIG_NAMES = ["ffn1_w_in", "ffn1_w_out", "w_mix_in", "w_mix_out", "ffn2_w_in", "ffn2_w_out"]
BIG_KINDS = ["col", "row", "row", "row", "col", "row"]
WEIGHT_ORDER = ["ada_w", "ada_b", "norm1_g", "ffn1_w_in", "ffn1_w_out", "norm2_g", "w_mix_in", "sgu_w", "sgu_b", "pool_w",
                "pool_scale", "conv_w", "s5_lambda_re", "s5_lambda_im", "s5_b_re", "s5_b_im", "s5_c_re", "s5_c_im", "s5_d",
                "s5_log_dt", "s5_glu_w", "s5_glu_b", "mix_norm_g", "w_mix_out", "norm3_g", "ffn2_w_in", "ffn2_w_out",
                "final_norm_g"]


def _local_step(x, target, cond, fetch_weights, prefetch_weights, p, emit_grads):
    nl, d = DEPTH, x.shape[1]
    row = lambda a: a.reshape(1, -1)
    saved = []
    for l in range(nl):
        (wi1, wo1, wmit, wmo, wi2, wo2), tok = fetch_weights(l, x)
        cl = cond[l] + tok
        mod1, mod2, mod3 = cl[0:3], cl[3:6], cl[6:9]
        lre, lim = p["s5_lambda_re"][l].reshape(-1), p["s5_lambda_im"][l].reshape(-1)
        ldt = jnp.repeat(p["s5_log_dt"][l], 64)
        rowp = jnp.stack([lre, lim, ldt])
        sp = (rowp, rowp.T, p["s5_b_re"][l].reshape(N_STATE, 16), p["s5_b_im"][l].reshape(N_STATE, 16),
              p["s5_c_re"][l].reshape(W_GRP, 64), p["s5_c_im"][l].reshape(W_GRP, 64), row(p["s5_d"][l]))
        glu = (p["s5_glu_w"][l], row(p["s5_glu_b"][l]))
        bias_full = jnp.repeat(p["sgu_b"][l].T, 64, axis=1)
        pw2 = p["pool_w"][l].reshape(W_GRP, 64)
        x1, h1, a1, b1, o1 = _ffn_fwd(x, mod1, row(p["norm1_g"][l]), wi1, wo1)
        z, h2 = _mix_in_fwd(x1, mod2, row(p["norm2_g"][l]), wmit)
        ya = _sgu_fwd(z, p["sgu_w"][l], bias_full)
        yb, yc = _poolconv_fwd(z, pw2, row(p["pool_scale"][l]), p["conv_w"][l])
        ypre = _s5_core_fwd(z, sp)
        yd = _s5_glu_fwd(ypre, *glu)
        ys = (ya, yb, yc, yd)
        x2, m = _mix_out_fwd(ys, row(p["mix_norm_g"][l]), wmo, x1, mod2[2:3])
        mod3 = mod3 + prefetch_weights(l + 1, x2)
        x3, h3, a3, b3, o3 = _ffn_fwd(x2, mod3, row(p["norm3_g"][l]), wi2, wo2)
        saved.append((x, x1, x2, h1, a1, b1, o1, z, h2, ys, m, h3, a3, b3, o3, sp, bias_full, pw2, ypre, glu,
                      (wi1, wo1, wmit, wmo, wi2, wo2), cl))
        x = x3

    loss, dx, dfg = _loss_head(x, row(p["final_norm_g"]), target)

    sg = {n: [None] * nl for n in SMALL_NAMES if n not in ("ada_b", "final_norm_g")}
    dcond = [None] * nl
    s5_da, s5_dk = [None] * nl, [None] * nl
    tok = 0.0
    for l in reversed(range(nl)):
        (x0, x1, x2, h1, a1, b1, o1, z, h2, ys, m, h3, a3, b3, o3, sp, bias_full, pw2, ypre, glu,
         (wi1, wo1, wmit, wmo, wi2, wo2), cl) = saved[l]
        cl = cl + tok
        mod1, mod2, mod3 = cl[0:3], cl[3:6], cl[6:9]
        dza, dzb, dwi2, dwo2, dgate3 = _ffn_bwd_main(dx, o3, mod3[2:3], h3, a3, b3, wo2)
        dx, rows3 = _ffn_bwd_in(dza, dzb, wi2, x2, dx, mod3, row(p["norm3_g"][l]))
        outs = _mix_out_bwd(dx, m, mod2[2:3], ys, row(p["mix_norm_g"][l]), wmo)
        dys, dgate2, dmng, dwmo = outs[0:4], outs[4], outs[5], outs[6]
        dza_, dsw, dsb = _sgu_bwd(z, dys[0], p["sgu_w"][l], bias_full)
        dzb_, dzc_, dwbd, dps, dcw = _poolconv_bwd(z, dys[1], dys[2], pw2, row(p["pool_scale"][l]), p["conv_w"][l])
        dypre, dgw, dgb = _s5_glu_bwd(ypre, dys[3], *glu)
        dzd_, dbr, dbi, dcr, dci, dd, da, dk = _s5_core_bwd(z, dypre, sp)
        dx, rows2, dwmit = _mix_in_bwd((dza_, dzb_, dzc_, dzd_), h2, wmit, x1, dx, mod2, row(p["norm2_g"][l]))
        dza, dzb, dwi1, dwo1, dgate1 = _ffn_bwd_main(dx, o1, mod1[2:3], h1, a1, b1, wo1)
        tok, layer_done = emit_grads(l, [dwi1, dwo1, dwmit, dwmo, dwi2, dwo2])
        dx, rows1 = _ffn_bwd_in(dza, dzb, wi1, x0, dx, mod1 + tok, row(p["norm1_g"][l]))
        tok = layer_done(dx)
        dcond[l] = jnp.concatenate([rows1[0:2], dgate1, rows2[0:2], dgate2, rows3[0:2], dgate3], axis=0)
        sg["norm1_g"][l], sg["norm2_g"][l], sg["norm3_g"][l] = rows1[2], rows2[2], rows3[2]
        sg["mix_norm_g"][l] = dmng[0]
        sg["sgu_w"][l] = dsw
        sg["sgu_b"][l] = dsb[:, 0:4].T
        g4 = dwbd.reshape(4, 64, 4, 64)
        sg["pool_w"][l] = jnp.stack([g4[k, :, k, :] for k in range(4)])
        sg["pool_scale"][l] = dps[0]
        sg["conv_w"][l] = dcw[0:3]
        sg["s5_b_re"][l], sg["s5_b_im"][l] = dbr.reshape(16, 64, 16), dbi.reshape(16, 64, 16)
        sg["s5_c_re"][l], sg["s5_c_im"][l] = dcr.reshape(16, 16, 64), dci.reshape(16, 16, 64)
        sg["s5_d"][l] = dd[0]
        sg["s5_glu_w"][l], sg["s5_glu_b"][l] = dgw, dgb[0]
        s5_da[l], s5_dk[l] = da, dk

    n16 = nl * 16
    dlre, dlim, dldt = _s5_param_bwd(
        p["s5_lambda_re"].reshape(n16, 64), p["s5_lambda_im"].reshape(n16, 64),
        jnp.repeat(p["s5_log_dt"].reshape(n16, 1), 64, axis=1),
        jnp.stack([a[0] for a in s5_da]).reshape(n16, 64), jnp.stack([a[1] for a in s5_da]).reshape(n16, 64),
        jnp.stack([k[:, 0] for k in s5_dk]).reshape(n16, 64), jnp.stack([k[:, 1] for k in s5_dk]).reshape(n16, 64))
    small = {n: jnp.stack(v) for n, v in sg.items() if v[0] is not None}
    small["s5_lambda_re"] = dlre.reshape(nl, 16, 64)
    small["s5_lambda_im"] = dlim.reshape(nl, 16, 64)
    small["s5_log_dt"] = dldt.reshape(nl, 16)
    small["final_norm_g"] = dfg[0]
    return loss, dx, small, jnp.stack(dcond) + tok


def kernel(x, c, ada_w, ada_b, norm1_g, ffn1_w_in, ffn1_w_out, norm2_g, w_mix_in, sgu_w, sgu_b, pool_w, pool_scale, conv_w, s5_lambda_re, s5_lambda_im, s5_b_re, s5_b_im, s5_c_re, s5_c_im, s5_d, s5_log_dt, s5_glu_w, s5_glu_b, mix_norm_g, w_mix_out, norm3_g, ffn2_w_in, ffn2_w_out, final_norm_g, loss_target, m_ada_w, m_ada_b, m_norm1_g, m_ffn1_w_in, m_ffn1_w_out, m_norm2_g, m_w_mix_in, m_sgu_w, m_sgu_b, m_pool_w, m_pool_scale, m_conv_w, m_s5_lambda_re, m_s5_lambda_im, m_s5_b_re, m_s5_b_im, m_s5_c_re, m_s5_c_im, m_s5_d, m_s5_log_dt, m_s5_glu_w, m_s5_glu_b, m_mix_norm_g, m_w_mix_out, m_norm3_g, m_ffn2_w_in, m_ffn2_w_out, m_final_norm_g, v_ada_w, v_ada_b, v_norm1_g, v_ffn1_w_in, v_ffn1_w_out, v_norm2_g, v_w_mix_in, v_sgu_w, v_sgu_b, v_pool_w, v_pool_scale, v_conv_w, v_s5_lambda_re, v_s5_lambda_im, v_s5_b_re, v_s5_b_im, v_s5_c_re, v_s5_c_im, v_s5_d, v_s5_log_dt, v_s5_glu_w, v_s5_glu_b, v_mix_norm_g, v_w_mix_out, v_norm3_g, v_ffn2_w_in, v_ffn2_w_out, v_final_norm_g):
    args = dict(locals())
    w = {n: args[n] for n in WEIGHT_ORDER}
    mom = {n: args["m_" + n] for n in WEIGHT_ORDER}
    vel = {n: args["v_" + n] for n in WEIGHT_ORDER}
    nl, d = DEPTH, x.shape[-1]
    s = x.shape[1]
    px, py, pc = lax.axis_index("x"), lax.axis_index("y"), lax.axis_index("c")
    kme = 2 * px + py
    me = 2 * kme + pc
    kidx = jnp.reshape(kme, (1,)).astype(jnp.int32)

    shards = [ffn1_w_in, ffn1_w_out, jnp.swapaxes(w_mix_in, 1, 2), w_mix_out, ffn2_w_in, ffn2_w_out]
    started_weights = {}

    def start_weights(l, after):
        placed = _place_weights(shards, l, kidx, after)
        views = [a.reshape(a.shape[0], 2, a.shape[1] // 2, a.shape[2]) for a in placed]
        *handles, token = _weight_send_start(views, BIG_KINDS, "weight_send_start_%d" % l)
        started_weights[l] = handles
        return token[0, 0]

    tok0 = start_weights(0, c)
    cact = _silu_rows(c + tok0)
    for l in range(1, nl):
        start_weights(l, cact)

    pre = _pack([cact, conv_w, s5_glu_w])
    pre_all = _all_gather8(pre, "gather_prelude")
    parts = [_unpack(pre_all[dev], [cact.shape, conv_w.shape, s5_glu_w.shape]) for dev in range(N_DEV)]
    cact_all = pre_all[:, :d // LANES, :].reshape(N_DEV, d)
    conv_full = jnp.concatenate([parts[2 * k][1] for k in range(N_CHIP)], axis=2)
    glu_full = jnp.concatenate([parts[2 * k][2] for k in range(N_CHIP)], axis=1)

    n_ada = ada_w.shape[2]
    ada_b_loc = lax.dynamic_slice_in_dim(ada_b, kme * n_ada, n_ada, axis=1).reshape(nl, 1, n_ada)
    cond_part = _cond_fwd(cact_all, ada_w, ada_b_loc)
    cond_all = _all_gather8(cond_part.reshape(nl * N_DEV, n_ada), "gather_cond").reshape(N_DEV, nl, N_DEV, n_ada)
    cond_me = jnp.concatenate(
        [lax.dynamic_index_in_dim(cond_all[2 * k], me, axis=1, keepdims=False) for k in range(N_CHIP)], axis=1)
    cond = cond_me.reshape(nl, 9, d)

    forwarding = {}

    def prefetch_weights(l, after):
        if l >= nl:
            return 0.0
        send_sems, recv_sems, *views = started_weights.pop(l)
        views = _weight_send_wait(send_sems, recv_sems, views, BIG_KINDS, after, "weight_send_wait_%d" % l)
        forwarding[l] = _split_start("weight_forward_start_%d" % l, views, 3 * len(views), _forward_copies(BIG_KINDS))
        return forwarding[l][-1][0, 0]

    def fetch_weights(l, after):
        if l not in forwarding:
            prefetch_weights(l, after)
        views = _split_wait("weight_forward_wait_%d" % l, forwarding.pop(l), _forward_copies(BIG_KINDS), after)
        full = [v.reshape(2, 2 * v.shape[2], v.shape[3]) if kind == "col" else v.reshape(-1, v.shape[3])
                for v, kind in zip(views, BIG_KINDS)]
        return full, 0.0

    ck = jnp.stack([pc, kme]).astype(jnp.int32)
    scattering, sharing = [], []
    stacks = {n: None for n in BIG_NAMES}
    groups = ((["ffn1_w_in", "ffn2_w_in"], 16, "adamw_w_in"),
              (["ffn1_w_out", "w_mix_in", "w_mix_out", "ffn2_w_out"], 8, "adamw_w_out"))

    def apply_adamw(l, fulls):
        g = {n: f.reshape(2 * f.shape[1], f.shape[2]) for n, f in zip(BIG_NAMES, fulls)}
        g["w_mix_in"] = g["w_mix_in"].T
        for names, steps, call in groups:
            outs = _adamw_layer([w[n] for n in names], [g[n] for n in names], [mom[n] for n in names],
                                [vel[n] for n in names], [stacks[n] for n in names], l, call, steps)
            stacks.update(zip(names, outs))

    def retire_share(after):
        l2, shared = sharing.pop(0)
        apply_adamw(l2, _sibling_share_wait(shared, after, "grad_share_wait_%d" % l2))

    def retire_scatter(after):
        l1, scattered = scattering.pop(0)
        sharing.append((l1, _finish_big_grads(scattered, BIG_KINDS, ck, after, l1)))

    def retire(after):
        if sharing:
            retire_share(after)
        if scattering:
            retire_scatter(after)

    def emit_grads(l, grads_l):
        exchanged = _exchange_big_grads(grads_l, BIG_KINDS, l)

        def layer_done(after):
            started = _scatter_big_grads(exchanged, BIG_KINDS, ck, after, l)
            retire(after)
            scattering.append((l, started))
            return started[-1][0, 0]

        return exchanged[-1][0, 0], layer_done

    p = {n: w[n] for n in SMALL_NAMES}
    p["conv_w"], p["s5_glu_w"] = conv_full, glu_full
    loss, dx, small, dcond = _local_step(x[0], loss_target[0], cond, fetch_weights, prefetch_weights, p, emit_grads)

    small_order = [n for n in SMALL_NAMES if n != "ada_b"]
    packed = _pack([dcond] + [small[n] for n in small_order]).astype(BF16)
    mine = lax.dynamic_update_slice(lax.empty((N_DEV,) + packed.shape, BF16), packed[None], (me, 0, 0))
    gathering = _split_start("small_grads_send_start", [mine], 4, _gather_first_copies())
    while sharing:
        retire_share(gathering[-1])
    arrived, = _split_wait("small_grads_send_wait", gathering, _gather_first_copies(), stacks[BIG_NAMES[0]][0])
    gathered_small = _gather_pass_on(arrived, "small_grads_pass_on")
    total = _sum8(gathered_small)
    shapes = [dcond.shape] + [small[n].shape for n in small_order]
    tot = dict(zip(["ada_b"] + small_order, _unpack(total, shapes)))
    grads = {n: tot[n] for n in SMALL_NAMES}
    grads["ada_b"] = tot["ada_b"].reshape(nl, 9 * d)
    grads["conv_w"] = lax.dynamic_slice_in_dim(tot["conv_w"], kme * conv_w.shape[2], conv_w.shape[2], axis=2)
    grads["s5_glu_w"] = lax.dynamic_slice_in_dim(tot["s5_glu_w"], kme * s5_glu_w.shape[1], s5_glu_w.shape[1], axis=1)

    dcond_all = gathered_small.reshape(N_DEV, -1)[:, :dcond.size].reshape(N_DEV, nl, 9 * d)
    dcond_loc = jnp.swapaxes(lax.dynamic_slice_in_dim(dcond_all, kme * n_ada, n_ada, axis=2), 0, 1)
    g_ada, d_ada, m_ada, v_ada = _ada_w_update(cact_all, dcond_loc, ada_w, m_ada_w, v_ada_w)

    while scattering or sharing:
        retire(g_ada)
    delta, new_m, new_v = {}, {}, {}
    for n in BIG_NAMES:
        grads[n], delta[n], new_m[n], new_v[n] = stacks[n]

    grads["ada_w"], delta["ada_w"], new_m["ada_w"], new_v["ada_w"] = g_ada, d_ada, m_ada, v_ada
    wide = ("s5_b_re", "s5_b_im")
    for names, call, steps in (([n for n in SMALL_NAMES if n not in wide], "adamw_small", 1),
                               (list(wide), "adamw_s5_b", DEPTH)):
        outs = _adamw_many(*[[_as_rows(t[n]) for n in names] for t in (w, grads, mom, vel)], call, steps)
        for res, o in zip((delta, new_m, new_v), outs):
            res.update({n: a.reshape(w[n].shape) for n, a in zip(names, o)})

    loss_total = lax.psum(loss[0, 0], ("x", "y", "c"))
    return (loss_total, dx[None], *[grads[n] for n in WEIGHT_ORDER], *[delta[n] for n in WEIGHT_ORDER],
            *[new_m[n] for n in WEIGHT_ORDER], *[new_v[n] for n in WEIGHT_ORDER])
```

```python
import functools
import math

import jax
import jax.numpy as jnp
from jax import lax
from jax.experimental import pallas as pl
from jax.experimental.pallas import tpu as pltpu

F32, BF16 = jnp.float32, jnp.bfloat16
EPS = 1e-6
DEPTH = 4
N_DEV = 8
N_CHIP = 4
W_GRP = 256
CHUNK = 128
N_SEG = 8
LANES = 128
FFN_TF = 256
FFN_TF_WIDE = 1408
FFN_TM_WIDE = 512
VMEM_LIMIT = 56 * 1024 * 1024
ADAM_LR, ADAM_B1, ADAM_B2, ADAM_EPS, ADAM_WD, ADAM_STEP = 0.001, 0.9, 0.999, 1e-08, 0.01, 10
MESH_ID = pl.DeviceIdType.MESH
HI = lax.Precision.HIGHEST
ANY = pl.BlockSpec(memory_space=pl.ANY)


def _cp(**kw):
    return pltpu.CompilerParams(vmem_limit_bytes=VMEM_LIMIT, **kw)


def _dot(a, b):
    return jnp.dot(a.astype(BF16), b.astype(BF16), preferred_element_type=F32)


def _dot_nt(a, b):
    return lax.dot_general(a.astype(BF16), b.astype(BF16), (((1,), (1,)), ((), ())), preferred_element_type=F32)


def _dot_tn(a, b):
    return lax.dot_general(a.astype(BF16), b.astype(BF16), (((0,), (0,)), ((), ())), preferred_element_type=F32)


def _dot_hi(a, b):
    return jnp.dot(a, b, preferred_element_type=F32, precision=HI)


def _gelu(x):
    k = 0.7978845608028654
    t = jnp.tanh(k * (x + 0.044715 * x * x * x))
    return 0.5 * x * (1.0 + t), t


def _gelu_grad(x, t):
    k = 0.7978845608028654
    return 0.5 * (1.0 + t) + 0.5 * x * (1.0 - t * t) * k * (1.0 + 3.0 * 0.044715 * x * x)


def _iota(shape, axis):
    return lax.broadcasted_iota(jnp.int32, shape, axis)


def _full(shape):
    nd = len(shape)
    return pl.BlockSpec(shape, lambda *_: (0,) * nd)


def _norm_mod(xv, g, shift, scale):
    r = lax.rsqrt(jnp.mean(xv * xv, axis=-1, keepdims=True) + EPS)
    return (xv * r * g) * (1.0 + scale) + shift


def _norm_mod_bwd(xv, g, scale, dh):
    r = lax.rsqrt(jnp.mean(xv * xv, axis=-1, keepdims=True) + EPS)
    xh = xv * r
    n = xh * g
    dsh = jnp.sum(dh, axis=0, keepdims=True)
    dsc = jnp.sum(dh * n, axis=0, keepdims=True)
    dn = dh * (1.0 + scale)
    dg = jnp.sum(dn * xh, axis=0, keepdims=True)
    dxh = dn * g
    dx = r * (dxh - xh * jnp.mean(dxh * xh, axis=-1, keepdims=True))
    return dx, dsh, dsc, dg


def _tm(s):
    return min(s, 1024)


def _ffn_fwd(x, mod, g, wi, wo):
    s, d = x.shape
    f = wo.shape[0]
    tf, tm = FFN_TF_WIDE, min(s, FFN_TM_WIDE)
    nf, nt = f // tf, s // tm

    def body(x_ref, mod_ref, g_ref, wa_ref, wb_ref, wo_ref, xn_ref, h_ref, a_ref, b_ref, o_ref, acc):
        j = pl.program_id(1)

        @pl.when(j == 0)
        def _():
            hh = _norm_mod(x_ref[...], g_ref[...], mod_ref[0:1, :], mod_ref[1:2, :])
            h_ref[...] = hh.astype(BF16)
            acc[...] = jnp.zeros_like(acc)

        h = h_ref[...]
        a = jnp.dot(h, wa_ref[...], preferred_element_type=F32)
        b = jnp.dot(h, wb_ref[...], preferred_element_type=F32)
        a_ref[...] = a.astype(BF16)
        b_ref[...] = b.astype(BF16)
        u = (a * jax.nn.sigmoid(a)) * b
        acc[...] += jnp.dot(u.astype(BF16), wo_ref[...], preferred_element_type=F32)

        @pl.when(j == nf - 1)
        def _():
            o = acc[...]
            o_ref[...] = o.astype(BF16)
            xn_ref[...] = x_ref[...] + 0.5 * mod_ref[2:3, :] * o

    row = pl.BlockSpec((tm, d), lambda i, j: (i, 0))
    chunk = pl.BlockSpec((tm, tf), lambda i, j: (i, j))
    return pl.pallas_call(
        body, name="ffn_fwd", grid=(nt, nf),
        in_specs=[row, _full((3, d)), _full((1, d)),
                  pl.BlockSpec((None, d, tf), lambda i, j: (0, 0, j)),
                  pl.BlockSpec((None, d, tf), lambda i, j: (1, 0, j)),
                  pl.BlockSpec((tf, d), lambda i, j: (j, 0))],
        out_specs=[row, row, chunk, chunk, row],
        out_shape=[jax.ShapeDtypeStruct((s, d), F32), jax.ShapeDtypeStruct((s, d), BF16),
                   jax.ShapeDtypeStruct((s, f), BF16), jax.ShapeDtypeStruct((s, f), BF16),
                   jax.ShapeDtypeStruct((s, d), BF16)],
        scratch_shapes=[pltpu.VMEM((tm, d), F32)],
        compiler_params=_cp(dimension_semantics=("parallel", "arbitrary")),
    )(x, mod, g, wi, wi, wo)


def _ffn_bwd_main(dxo, o, gate, h, a, b, wo):
    s, d = dxo.shape
    f = wo.shape[0]
    tf = FFN_TF
    nf = f // tf

    def body(dxo_ref, o_ref, gate_ref, h_ref, a_ref, b_ref, wo_ref, dza_ref, dzb_ref, dwi_ref, dwo_ref, dg_ref, do_s):
        @pl.when(pl.program_id(0) == 0)
        def _():
            dxv = dxo_ref[...]
            do_s[...] = (0.5 * gate_ref[...] * dxv).astype(BF16)
            dg_ref[...] = 0.5 * jnp.sum(o_ref[...].astype(F32) * dxv, axis=0, keepdims=True)

        dov = do_s[...]
        hv = h_ref[...]
        du = lax.dot_general(dov, wo_ref[...], (((1,), (1,)), ((), ())), preferred_element_type=F32)
        av = a_ref[...].astype(F32)
        bv = b_ref[...].astype(F32)
        sa = jax.nn.sigmoid(av)
        si = av * sa
        u = (si * bv).astype(BF16)
        da = (du * bv * (sa * (1.0 + av * (1.0 - sa)))).astype(BF16)
        db = (du * si).astype(BF16)
        dza_ref[...] = da
        dzb_ref[...] = db
        dwo_ref[...] = _dot_tn(u, dov).astype(BF16)
        dwi_ref[0] = _dot_tn(hv, da).astype(BF16)
        dwi_ref[1] = _dot_tn(hv, db).astype(BF16)

    chunk = pl.BlockSpec((s, tf), lambda j: (0, j))
    once = lambda: pl.BlockSpec((s, d), lambda j: (0, 0), pipeline_mode=pl.Buffered(1))
    return pl.pallas_call(
        body, name="ffn_bwd_main", grid=(nf,),
        in_specs=[once(), once(), _full((1, d)), once(), chunk, chunk, pl.BlockSpec((tf, d), lambda j: (j, 0))],
        out_specs=[chunk, chunk, pl.BlockSpec((2, d, tf), lambda j: (0, 0, j)),
                   pl.BlockSpec((tf, d), lambda j: (j, 0)), _full((1, d))],
        out_shape=[jax.ShapeDtypeStruct((s, f), BF16), jax.ShapeDtypeStruct((s, f), BF16),
                   jax.ShapeDtypeStruct((2, d, f), BF16), jax.ShapeDtypeStruct((f, d), BF16),
                   jax.ShapeDtypeStruct((1, d), F32)],
        scratch_shapes=[pltpu.VMEM((s, d), BF16)],
        compiler_params=_cp(dimension_semantics=("arbitrary",)),
    )(dxo, o, gate, h, a, b, wo)


def _ffn_bwd_in(dza, dzb, wi, x, dxo, mod, g):
    s, d = x.shape
    f = dza.shape[1]
    tf, tm = FFN_TF_WIDE, min(s, FFN_TM_WIDE)
    nf, nt = f // tf, s // tm

    def body(dza_ref, dzb_ref, wa_ref, wb_ref, x_ref, dxo_ref, mod_ref, g_ref, dx_ref, rows_ref, acc):
        i, j = pl.program_id(0), pl.program_id(1)

        @pl.when(jnp.logical_and(i == 0, j == 0))
        def _():
            rows_ref[...] = jnp.zeros_like(rows_ref)

        @pl.when(j == 0)
        def _():
            acc[...] = jnp.zeros_like(acc)

        acc[...] += (lax.dot_general(dza_ref[...], wa_ref[...], (((1,), (1,)), ((), ())), preferred_element_type=F32)
                     + lax.dot_general(dzb_ref[...], wb_ref[...], (((1,), (1,)), ((), ())), preferred_element_type=F32))

        @pl.when(j == nf - 1)
        def _():
            dx, dsh, dsc, dg = _norm_mod_bwd(x_ref[...], g_ref[...], mod_ref[1:2, :], acc[...])
            dx_ref[...] = dx + dxo_ref[...]
            rows_ref[0:1, :] += dsh
            rows_ref[1:2, :] += dsc
            rows_ref[2:3, :] += dg

    row = pl.BlockSpec((tm, d), lambda i, j: (i, 0))
    chunk = pl.BlockSpec((tm, tf), lambda i, j: (i, j))
    return pl.pallas_call(
        body, name="ffn_bwd_in", grid=(nt, nf),
        in_specs=[chunk, chunk,
                  pl.BlockSpec((None, d, tf), lambda i, j: (0, 0, j)),
                  pl.BlockSpec((None, d, tf), lambda i, j: (1, 0, j)),
                  row, row, _full((3, d)), _full((1, d))],
        out_specs=[row, _full((8, d))],
        out_shape=[jax.ShapeDtypeStruct((s, d), F32), jax.ShapeDtypeStruct((8, d), F32)],
        scratch_shapes=[pltpu.VMEM((tm, d), F32)],
        compiler_params=_cp(dimension_semantics=("arbitrary", "arbitrary")),
    )(dza, dzb, wi, wi, x, dxo, mod, g)


def _mix_in_fwd(x, mod, g, wmit):
    s, d = x.shape
    p = wmit.shape[0]
    tm = _tm(s)

    def body(x_ref, mod_ref, g_ref, w_ref, z_ref, h_ref):
        hh = _norm_mod(x_ref[...], g_ref[...], mod_ref[0:1, :], mod_ref[1:2, :]).astype(BF16)
        h_ref[...] = hh
        z_ref[...] = lax.dot_general(hh, w_ref[...], (((1,), (1,)), ((), ())), preferred_element_type=F32)

    row = pl.BlockSpec((tm, d), lambda i: (i, 0))
    return pl.pallas_call(
        body, name="mix_in_fwd", grid=(s // tm,),
        in_specs=[row, _full((3, d)), _full((1, d)), _full((p, d))],
        out_specs=[pl.BlockSpec((tm, p), lambda i: (i, 0)), row],
        out_shape=[jax.ShapeDtypeStruct((s, p), F32), jax.ShapeDtypeStruct((s, d), BF16)],
        compiler_params=_cp(dimension_semantics=("parallel",)),
    )(x, mod, g, wmit)


def _mix_in_bwd(dzs, h, wmit, x, dxo, mod, g):
    s, d = x.shape
    p = wmit.shape[0]
    tm = min(s, 512)
    nt = s // tm

    def body(za_ref, zb_ref, zc_ref, zd_ref, h_ref, w_ref, x_ref, dxo_ref, mod_ref, g_ref,
             dx_ref, rows_ref, dw_ref, acc):
        i = pl.program_id(0)

        @pl.when(i == 0)
        def _():
            rows_ref[...] = jnp.zeros_like(rows_ref)
            acc[...] = jnp.zeros_like(acc)

        dz = jnp.concatenate([za_ref[...], zb_ref[...], zc_ref[...], zd_ref[...]], axis=1).astype(BF16)
        acc[...] += _dot_tn(dz, h_ref[...])
        dh = jnp.dot(dz, w_ref[...], preferred_element_type=F32)
        dx, dsh, dsc, dg = _norm_mod_bwd(x_ref[...], g_ref[...], mod_ref[1:2, :], dh)
        dx_ref[...] = dx + dxo_ref[...]
        rows_ref[0:1, :] += dsh
        rows_ref[1:2, :] += dsc
        rows_ref[2:3, :] += dg

        @pl.when(i == nt - 1)
        def _():
            dw_ref[...] = acc[...].astype(BF16)

    row = pl.BlockSpec((tm, d), lambda i: (i, 0))
    zspecs = [pl.BlockSpec((tm, z.shape[1]), lambda i: (i, 0)) for z in dzs]
    return pl.pallas_call(
        body, name="mix_in_bwd", grid=(nt,),
        in_specs=zspecs + [row, _full((p, d)), row, row, _full((3, d)), _full((1, d))],
        out_specs=[row, _full((8, d)), _full((p, d))],
        out_shape=[jax.ShapeDtypeStruct((s, d), F32), jax.ShapeDtypeStruct((8, d), F32),
                   jax.ShapeDtypeStruct((p, d), BF16)],
        scratch_shapes=[pltpu.VMEM((p, d), F32)],
        compiler_params=_cp(dimension_semantics=("arbitrary",)),
    )(*dzs, h, wmit, x, dxo, mod, g)


def _group_norm(ys, mng):
    outs, hats, rs = [], [], []
    for k, y in enumerate(ys):
        r = lax.rsqrt(jnp.mean(y * y, axis=-1, keepdims=True) + EPS)
        yh = y * r
        hats.append(yh)
        rs.append(r)
        outs.append(yh * mng[:, k * W_GRP:(k + 1) * W_GRP])
    return jnp.concatenate(outs, axis=1), hats, rs


def _mix_out_fwd(ys, mng, wmo, x, gate):
    s, d = x.shape
    tm = _tm(s)

    def body(ya, yb, yc, yd, mng_ref, w_ref, x_ref, gate_ref, xn_ref, m_ref):
        yn, _, _ = _group_norm([ya[...], yb[...], yc[...], yd[...]], mng_ref[...])
        m = jnp.dot(yn.astype(BF16), w_ref[...], preferred_element_type=F32)
        m_ref[...] = m
        xn_ref[...] = x_ref[...] + gate_ref[...] * m

    row = pl.BlockSpec((tm, d), lambda i: (i, 0))
    grp = pl.BlockSpec((tm, W_GRP), lambda i: (i, 0))
    return pl.pallas_call(
        body, name="mix_out_fwd", grid=(s // tm,),
        in_specs=[grp, grp, grp, grp, _full((1, d)), _full((d, d)), row, _full((1, d))],
        out_specs=[row, row],
        out_shape=[jax.ShapeDtypeStruct((s, d), F32), jax.ShapeDtypeStruct((s, d), F32)],
        compiler_params=_cp(dimension_semantics=("parallel",)),
    )(*ys, mng, wmo, x, gate)


def _mix_out_bwd(dxo, m, gate, ys, mng, wmo):
    s, d = dxo.shape
    tm = min(s, 512)
    nt = s // tm

    def body(dxo_ref, m_ref, gate_ref, ya, yb, yc, yd, mng_ref, w_ref,
             dya, dyb, dyc, dyd, dgate_ref, dmng_ref, dw_ref, acc):
        i = pl.program_id(0)

        @pl.when(i == 0)
        def _():
            dgate_ref[...] = jnp.zeros_like(dgate_ref)
            dmng_ref[...] = jnp.zeros_like(dmng_ref)
            acc[...] = jnp.zeros_like(acc)

        dxv = dxo_ref[...]
        dgate_ref[...] += jnp.sum(m_ref[...] * dxv, axis=0, keepdims=True)
        dm = (gate_ref[...] * dxv).astype(BF16)
        mng = mng_ref[...]
        yn, hats, rs = _group_norm([ya[...], yb[...], yc[...], yd[...]], mng)
        acc[...] += _dot_tn(yn, dm)
        dyn = lax.dot_general(dm, w_ref[...], (((1,), (1,)), ((), ())), preferred_element_type=F32)
        dmng_parts = []
        for k, (yh, r, out) in enumerate(zip(hats, rs, (dya, dyb, dyc, dyd))):
            dk = dyn[:, k * W_GRP:(k + 1) * W_GRP]
            dmng_parts.append(jnp.sum(dk * yh, axis=0, keepdims=True))
            dyh = dk * mng[:, k * W_GRP:(k + 1) * W_GRP]
            out[...] = r * (dyh - yh * jnp.mean(dyh * yh, axis=-1, keepdims=True))
        dmng_ref[...] += jnp.concatenate(dmng_parts, axis=1)

        @pl.when(i == nt - 1)
        def _():
            dw_ref[...] = acc[...].astype(BF16)

    row = pl.BlockSpec((tm, d), lambda i: (i, 0))
    grp = pl.BlockSpec((tm, W_GRP), lambda i: (i, 0))
    return pl.pallas_call(
        body, name="mix_out_bwd", grid=(nt,),
        in_specs=[row, row, _full((1, d)), grp, grp, grp, grp, _full((1, d)), _full((d, d))],
        out_specs=[grp, grp, grp, grp, _full((1, d)), _full((1, d)), _full((d, d))],
        out_shape=[jax.ShapeDtypeStruct((s, W_GRP), F32)] * 4
        + [jax.ShapeDtypeStruct((1, d), F32), jax.ShapeDtypeStruct((1, d), F32), jax.ShapeDtypeStruct((d, d), BF16)],
        scratch_shapes=[pltpu.VMEM((d, d), F32)],
        compiler_params=_cp(dimension_semantics=("arbitrary",)),
    )(dxo, m, gate, *ys, mng, wmo)


def _sgu_consts():
    r = _iota((W_GRP, W_GRP), 0) >> 6
    c = _iota((W_GRP, W_GRP), 1) >> 6
    avg = jnp.where(r == c, 1.0 / 64.0, 0.0).astype(F32)
    tril = _iota((CHUNK, CHUNK), 0) >= _iota((CHUNK, CHUNK), 1)
    head = _iota((CHUNK, W_GRP), 1) >> 6
    return avg, tril, head


def _sgu_pre(za, avg):
    zg, t = _gelu(za)
    u, v = zg[:, :W_GRP], zg[:, W_GRP:]
    mu = _dot_hi(v, avg)
    vc = v - mu
    r = lax.rsqrt(_dot_hi(vc * vc, avg) + EPS)
    return t, u, vc * r, r


def _sgu_fwd(z, sgu_w, bias_full):
    s = z.shape[0]
    tm = min(s, 512)

    def body(za_ref, w_ref, bias_ref, ya_ref):
        avg, tril, head = _sgu_consts()
        _, u, vn, _ = _sgu_pre(za_ref[...], avg)
        wm = [jnp.where(tril, w_ref[h], 0.0).astype(BF16) for h in range(4)]
        vb = vn.astype(BF16)
        for n in range(tm // CHUNK):
            rows = slice(n * CHUNK, (n + 1) * CHUNK)
            mixed = bias_ref[...]
            for h in range(4):
                mixed = mixed + jnp.where(head == h, jnp.dot(wm[h], vb[rows], preferred_element_type=F32), 0.0)
            ya_ref[rows, :] = u[rows] * mixed

    return pl.pallas_call(
        body, name="sgu_fwd", grid=(s // tm,),
        in_specs=[pl.BlockSpec((tm, 2 * W_GRP), lambda i: (i, 0)), _full((4, CHUNK, CHUNK)), _full((CHUNK, W_GRP))],
        out_specs=pl.BlockSpec((tm, W_GRP), lambda i: (i, 0)),
        out_shape=jax.ShapeDtypeStruct((s, W_GRP), F32),
        compiler_params=_cp(dimension_semantics=("parallel",)),
    )(z, sgu_w, bias_full)


def _sgu_bwd(z, dya, sgu_w, bias_full):
    s = z.shape[0]
    tm = min(s, 512)
    nt = s // tm

    def body(za_ref, dya_ref, w_ref, bias_ref, dza_ref, dw_ref, db_ref, du_s, dvn_s):
        i = pl.program_id(0)

        @pl.when(i == 0)
        def _():
            dw_ref[...] = jnp.zeros_like(dw_ref)
            db_ref[...] = jnp.zeros_like(db_ref)

        avg, tril, head = _sgu_consts()
        za = za_ref[...]
        t, u, vn, r = _sgu_pre(za, avg)
        wm = [jnp.where(tril, w_ref[h], 0.0).astype(BF16) for h in range(4)]
        vb = vn.astype(BF16)
        dya = dya_ref[...]
        dw = [jnp.zeros((CHUNK, CHUNK), F32) for _ in range(4)]
        db = jnp.zeros((CHUNK, W_GRP), F32)
        for n in range(tm // CHUNK):
            rows = slice(n * CHUNK, (n + 1) * CHUNK)
            mixed = bias_ref[...]
            for h in range(4):
                mixed = mixed + jnp.where(head == h, jnp.dot(wm[h], vb[rows], preferred_element_type=F32), 0.0)
            dmix = dya[rows] * u[rows]
            du_s[rows, :] = dya[rows] * mixed
            db = db + dmix
            dmb = dmix.astype(BF16)
            dvn = jnp.zeros((CHUNK, W_GRP), F32)
            for h in range(4):
                dmh = jnp.where(head == h, dmix, 0.0)
                dw[h] = dw[h] + _dot_nt(dmh, vb[rows])
                dvn = dvn + jnp.where(head == h, _dot_tn(wm[h], dmb), 0.0)
            dvn_s[rows, :] = dvn
        for h in range(4):
            dw_ref[h] += jnp.where(tril, dw[h], 0.0)
        sel = ((_iota((W_GRP, CHUNK), 0) >> 6) == _iota((W_GRP, CHUNK), 1)).astype(F32)
        db_ref[...] += _dot_hi(db, sel)
        dvn = dvn_s[...]
        dv = r * (dvn - _dot_hi(dvn, avg) - vn * _dot_hi(dvn * vn, avg))
        dzg = jnp.concatenate([du_s[...], dv], axis=1)
        dza_ref[...] = dzg * _gelu_grad(za, t)

    return pl.pallas_call(
        body, name="sgu_bwd", grid=(nt,),
        in_specs=[pl.BlockSpec((tm, 2 * W_GRP), lambda i: (i, 0)), pl.BlockSpec((tm, W_GRP), lambda i: (i, 0)),
                  _full((4, CHUNK, CHUNK)), _full((CHUNK, W_GRP))],
        out_specs=[pl.BlockSpec((tm, 2 * W_GRP), lambda i: (i, 0)), _full((4, CHUNK, CHUNK)), _full((CHUNK, CHUNK))],
        out_shape=[jax.ShapeDtypeStruct((s, 2 * W_GRP), F32), jax.ShapeDtypeStruct((4, CHUNK, CHUNK), F32),
                   jax.ShapeDtypeStruct((CHUNK, CHUNK), F32)],
        scratch_shapes=[pltpu.VMEM((tm, W_GRP), F32), pltpu.VMEM((tm, W_GRP), F32)],
        compiler_params=_cp(dimension_semantics=("arbitrary",)),
    )(z, dya, sgu_w, bias_full)


def _shift_down(x, k):
    return jnp.where(_iota(x.shape, 0) < k, 0.0, pltpu.roll(x, k, 0))


def _shift_up(x, k):
    n = x.shape[0]
    return jnp.where(_iota(x.shape, 0) >= n - k, 0.0, pltpu.roll(x, n - k, 0))


def _by_pool_group(shape, v2, v4, v8, v16):
    col = _iota(shape, 1)
    return jnp.where(col < 64, v2, jnp.where(col < 128, v4, jnp.where(col < 192, v8, v16)))


def _pool_core(zb, pw2):
    s2 = zb + _shift_down(zb, 1)
    s4 = s2 + _shift_down(s2, 2)
    s8 = s4 + _shift_down(s4, 4)
    s16 = s8 + _shift_down(s8, 8)
    win = _by_pool_group(zb.shape, s2, s4, s8, s16)
    wlen = _by_pool_group(zb.shape, 2.0, 4.0, 8.0, 16.0)
    cnt = jnp.minimum((_iota(zb.shape, 0) + 1).astype(F32), wlen)
    p = win / cnt - zb
    wt = jnp.tile(pw2, (1, 4))
    wbd = jnp.where((_iota(wt.shape, 0) >> 6) == (_iota(wt.shape, 1) >> 6), wt, 0.0).astype(BF16)
    return p, cnt, wbd


def _conv_core(zc, cw):
    bg, cg, xh = zc[:, :W_GRP], zc[:, W_GRP:2 * W_GRP], zc[:, 2 * W_GRP:]
    y = cg * xh
    y1, y2 = _shift_down(y, 1), _shift_down(y, 2)
    out = cw[2:3, :] * y + cw[1:2, :] * y1 + cw[0:1, :] * y2
    return bg, cg, xh, y, y1, y2, out


def _poolconv_fwd(z, pw2, pscale, cw):
    s = z.shape[0]

    def body(zb_ref, zc_ref, pw_ref, ps_ref, cw_ref, yb_ref, yc_ref):
        p, _, wbd = _pool_core(zb_ref[...], pw_ref[...])
        yb_ref[...] = jnp.dot(p.astype(BF16), wbd, preferred_element_type=F32) * ps_ref[...]
        bg, _, _, _, _, _, out = _conv_core(zc_ref[...], cw_ref[...])
        yc_ref[...] = bg * out

    return pl.pallas_call(
        body, name="poolconv_fwd", grid=(1,),
        in_specs=[pl.BlockSpec((s, W_GRP), lambda i: (0, 2)), pl.BlockSpec((s, 3 * W_GRP), lambda i: (0, 1)),
                  _full((W_GRP, 64)), _full((1, W_GRP)), _full((3, W_GRP))],
        out_specs=[_full((s, W_GRP)), _full((s, W_GRP))],
        out_shape=[jax.ShapeDtypeStruct((s, W_GRP), F32)] * 2,
        compiler_params=_cp(dimension_semantics=("arbitrary",)),
    )(z, z, pw2, pscale, cw)


def _poolconv_bwd(z, dyb, dyc, pw2, pscale, cw):
    s = z.shape[0]

    def body(zb_ref, zc_ref, dyb_ref, dyc_ref, pw_ref, ps_ref, cw_ref, dzb_ref, dzc_ref, dw_ref, dps_ref, dcw_ref):
        zb = zb_ref[...]
        p, cnt, wbd = _pool_core(zb, pw_ref[...])
        pb = p.astype(BF16)
        out = jnp.dot(pb, wbd, preferred_element_type=F32)
        dyb = dyb_ref[...]
        dps_ref[...] = jnp.sum(dyb * out, axis=0, keepdims=True)
        dout = (dyb * ps_ref[...]).astype(BF16)
        dw = _dot_tn(pb, dout)
        dw_ref[...] = jnp.where((_iota(dw.shape, 0) >> 6) == (_iota(dw.shape, 1) >> 6), dw, 0.0)
        dp = lax.dot_general(dout, wbd, (((1,), (1,)), ((), ())), preferred_element_type=F32)
        dwin = dp / cnt
        t2 = dwin + _shift_up(dwin, 1)
        t4 = t2 + _shift_up(t2, 2)
        t8 = t4 + _shift_up(t4, 4)
        t16 = t8 + _shift_up(t8, 8)
        dzb_ref[...] = _by_pool_group(zb.shape, t2, t4, t8, t16) - dp

        cw = cw_ref[...]
        bg, cg, xh, y, y1, y2, out = _conv_core(zc_ref[...], cw)
        dyc = dyc_ref[...]
        dout = dyc * bg
        dcw_ref[...] = jnp.zeros_like(dcw_ref)
        dcw_ref[0:1, :] = jnp.sum(dout * y2, axis=0, keepdims=True)
        dcw_ref[1:2, :] = jnp.sum(dout * y1, axis=0, keepdims=True)
        dcw_ref[2:3, :] = jnp.sum(dout * y, axis=0, keepdims=True)
        dy = cw[2:3, :] * dout + cw[1:2, :] * _shift_up(dout, 1) + cw[0:1, :] * _shift_up(dout, 2)
        dzc_ref[...] = jnp.concatenate([dyc * out, dy * xh, dy * cg], axis=1)

    return pl.pallas_call(
        body, name="poolconv_bwd", grid=(1,),
        in_specs=[pl.BlockSpec((s, W_GRP), lambda i: (0, 2)), pl.BlockSpec((s, 3 * W_GRP), lambda i: (0, 1)),
                  _full((s, W_GRP)), _full((s, W_GRP)), _full((W_GRP, 64)), _full((1, W_GRP)), _full((3, W_GRP))],
        out_specs=[_full((s, W_GRP)), _full((s, 3 * W_GRP)), _full((W_GRP, W_GRP)), _full((1, W_GRP)), _full((8, W_GRP))],
        out_shape=[jax.ShapeDtypeStruct((s, W_GRP), F32), jax.ShapeDtypeStruct((s, 3 * W_GRP), F32),
                   jax.ShapeDtypeStruct((W_GRP, W_GRP), F32), jax.ShapeDtypeStruct((1, W_GRP), F32),
                   jax.ShapeDtypeStruct((8, W_GRP), F32)],
        compiler_params=_cp(dimension_semantics=("arbitrary",)),
    )(z, z, dyb, dyc, pw2, pscale, cw)


N_STATE = 1024
HALF_STATE = N_STATE // 2
HALF_CH = W_GRP // 2
N_SLAB = HALF_STATE // LANES


def _s5_disc(lre, lim, ldt):
    dt = jnp.exp(ldt)
    mag = jnp.exp(lre * dt)
    ang = lim * dt
    ar, ai = mag * jnp.cos(ang), mag * jnp.sin(ang)
    nr, ni = ar - 1.0, ai
    den = lre * lre + lim * lim
    kr = (nr * lre + ni * lim) / den
    ki = (ni * lre - nr * lim) / den
    return ar, ai, kr, ki


def _s5_mats(colp, br, bi, cr, ci):
    _, _, kr, ki = _s5_disc(colp[:, 0:1], colp[:, 1:2], colp[:, 2:3])
    bbr = kr * br - ki * bi
    bbi = kr * bi + ki * br
    bmask = (_iota((HALF_STATE, HALF_CH), 0) >> 6) == (_iota((HALF_STATE, HALF_CH), 1) >> 4)
    cmask = (_iota((HALF_CH, HALF_STATE), 0) >> 4) == (_iota((HALF_CH, HALF_STATE), 1) >> 6)
    btr = jnp.where(bmask, jnp.tile(bbr, (1, 8)), 0.0).astype(BF16)
    bti = jnp.where(bmask, jnp.tile(bbi, (1, 8)), 0.0).astype(BF16)
    ctr = jnp.where(cmask, jnp.tile(cr, (1, 8)), 0.0).astype(BF16)
    cti = jnp.where(cmask, jnp.tile(ci, (1, 8)), 0.0).astype(BF16)
    return kr, ki, btr, bti, ctr, cti, bmask, cmask


def _slab(q):
    return slice(q * LANES, (q + 1) * LANES)


def _cmul(ar, ai, br, bi):
    return ar * br - ai * bi, ar * bi + ai * br


def _sub_shift(x, k, up):
    row = _iota(x.shape, 0)
    if up:
        return jnp.where(row >= N_SEG - k, 0.0, pltpu.roll(x, N_SEG - k, 0))
    return jnp.where(row < k, 0.0, pltpu.roll(x, k, 0))


def _seg_rows(j):
    return pl.ds(pl.multiple_of(j * N_SEG, N_SEG), N_SEG)


def _interleave(src, dst, seg):
    def step(j, carry):
        dst[_seg_rows(j), :] = src[pl.ds(j, N_SEG, stride=seg), :]
        return carry
    lax.fori_loop(0, seg, step, 0)


def _deinterleave(src, dst, seg):
    def step(j, carry):
        dst[pl.ds(j, N_SEG, stride=seg), :] = src[_seg_rows(j), :]
        return carry
    lax.fori_loop(0, seg, step, 0)


def _scan(xr, xi, ar_row, ai_row, seg, reverse):
    nlog = int(math.log2(seg))
    assert (1 << nlog) == seg
    for q0 in range(0, N_SLAB, 4):
        qs = list(range(q0, q0 + 4))
        aq = [(jnp.broadcast_to(ar_row[:, _slab(q)], (N_SEG, LANES)),
               jnp.broadcast_to(ai_row[:, _slab(q)], (N_SEG, LANES))) for q in qs]
        zero = jnp.zeros((N_SEG, LANES), F32)

        def local(jj, carry, qs=qs, aq=aq):
            j = seg - 1 - jj if reverse else jj
            out = []
            for n, q in enumerate(qs):
                rows = _seg_rows(j)
                pr, pi = _cmul(aq[n][0], aq[n][1], carry[2 * n], carry[2 * n + 1])
                nr = pr + xr[q, rows, :]
                ni = pi + xi[q, rows, :]
                xr[q, rows, :] = nr
                xi[q, rows, :] = ni
                out += [nr, ni]
            return tuple(out)

        fin = lax.fori_loop(0, seg, local, (zero,) * 8)
        cins = []
        for n in range(4):
            er, ei = fin[2 * n], fin[2 * n + 1]
            pr, pi = aq[n]
            for _ in range(nlog):
                pr, pi = _cmul(pr, pi, pr, pi)
            yr, yi = er, ei
            for k in (1, 2, 4):
                sr, si = _cmul(pr, pi, _sub_shift(yr, k, reverse), _sub_shift(yi, k, reverse))
                yr, yi = yr + sr, yi + si
                pr, pi = _cmul(pr, pi, pr, pi)
            cins.append((_sub_shift(yr, 1, reverse), _sub_shift(yi, 1, reverse)))

        def fix(jj, carry, qs=qs, aq=aq, cins=cins):
            j = seg - 1 - jj if reverse else jj
            out = []
            for n, q in enumerate(qs):
                rows = _seg_rows(j)
                pwr, pwi = carry[2 * n], carry[2 * n + 1]
                cr, ci = _cmul(pwr, pwi, cins[n][0], cins[n][1])
                xr[q, rows, :] += cr
                xi[q, rows, :] += ci
                nr, ni = _cmul(pwr, pwi, aq[n][0], aq[n][1])
                out += [nr, ni]
            return tuple(out)

        lax.fori_loop(0, seg, fix, tuple(v for pair in aq for v in pair))


def _s5_forward_states(u, btr, bti, ar_row, ai_row, xr, xi, seg):
    ub = u.astype(BF16)
    for q in range(N_SLAB):
        xr[q] = _dot_nt(ub, btr[_slab(q), :])
        xi[q] = _dot_nt(ub, bti[_slab(q), :])
    _scan(xr, xi, ar_row, ai_row, seg, False)


def _s5_readout(u, xr, xi, ctr, cti, d):
    y = d * u
    for q in range(N_SLAB):
        y = y + _dot_nt(xr[q], ctr[:, _slab(q)]) - _dot_nt(xi[q], cti[:, _slab(q)])
    return y


def _s5_param_specs():
    return [pl.BlockSpec((3, HALF_STATE), lambda i: (0, i)), pl.BlockSpec((HALF_STATE, 3), lambda i: (i, 0)),
            pl.BlockSpec((HALF_STATE, 16), lambda i: (i, 0)), pl.BlockSpec((HALF_STATE, 16), lambda i: (i, 0)),
            pl.BlockSpec((HALF_CH, 64), lambda i: (i, 0)), pl.BlockSpec((HALF_CH, 64), lambda i: (i, 0)),
            pl.BlockSpec((1, HALF_CH), lambda i: (0, i))]


def _s5_core_fwd(z, sp):
    s = z.shape[0]
    seg = s // N_SEG

    def body(u_ref, rowp, colp, br, bi, cr, ci, d_ref, y_ref, xr, xi, us, ys):
        ar, ai, _, _ = _s5_disc(rowp[0:1, :], rowp[1:2, :], rowp[2:3, :])
        _, _, btr, bti, ctr, cti, _, _ = _s5_mats(colp[...], br[...], bi[...], cr[...], ci[...])
        _interleave(u_ref, us, seg)
        u = us[...]
        _s5_forward_states(u, btr, bti, ar, ai, xr, xi, seg)
        ys[...] = _s5_readout(u, xr, xi, ctr, cti, d_ref[...])
        _deinterleave(ys, y_ref, seg)

    return pl.pallas_call(
        body, name="s5_core_fwd", grid=(2,),
        in_specs=[pl.BlockSpec((s, HALF_CH), lambda i: (0, 12 + i))] + _s5_param_specs(),
        out_specs=pl.BlockSpec((s, HALF_CH), lambda i: (0, i)),
        out_shape=jax.ShapeDtypeStruct((s, W_GRP), F32),
        scratch_shapes=[pltpu.VMEM((N_SLAB, s, LANES), F32)] * 2 + [pltpu.VMEM((s, HALF_CH), F32)] * 2,
        compiler_params=_cp(dimension_semantics=("parallel",)),
    )(z, *sp)


def _s5_glu_fwd(y, gw, gb):
    s = y.shape[0]
    tm = _tm(s)

    def body(y_ref, gw_ref, gb_ref, o_ref):
        yg, _ = _gelu(y_ref[...])
        o_ref[...] = yg * jax.nn.sigmoid(_dot(yg, gw_ref[...]) + gb_ref[...])

    blk = pl.BlockSpec((tm, W_GRP), lambda i: (i, 0))
    return pl.pallas_call(
        body, name="s5_glu_fwd", grid=(s // tm,),
        in_specs=[blk, _full((W_GRP, W_GRP)), _full((1, W_GRP))], out_specs=blk,
        out_shape=jax.ShapeDtypeStruct((s, W_GRP), F32),
        compiler_params=_cp(dimension_semantics=("parallel",)),
    )(y, gw, gb)


def _s5_glu_bwd(y, dyd, gw, gb):
    s = y.shape[0]
    tm = _tm(s)

    def body(y_ref, dyd_ref, gw_ref, gb_ref, dy_ref, dgw_ref, dgb_ref):
        i = pl.program_id(0)

        @pl.when(i == 0)
        def _():
            dgw_ref[...] = jnp.zeros_like(dgw_ref)
            dgb_ref[...] = jnp.zeros_like(dgb_ref)

        y, gw, dyd = y_ref[...], gw_ref[...], dyd_ref[...]
        yg, t = _gelu(y)
        gate = jax.nn.sigmoid(_dot(yg, gw) + gb_ref[...])
        dlin = dyd * yg * gate * (1.0 - gate)
        dgw_ref[...] += _dot_tn(yg, dlin)
        dgb_ref[...] += jnp.sum(dlin, axis=0, keepdims=True)
        dy_ref[...] = (dyd * gate + _dot_nt(dlin, gw)) * _gelu_grad(y, t)

    blk = pl.BlockSpec((tm, W_GRP), lambda i: (i, 0))
    return pl.pallas_call(
        body, name="s5_glu_bwd", grid=(s // tm,),
        in_specs=[blk, blk, _full((W_GRP, W_GRP)), _full((1, W_GRP))],
        out_specs=[blk, _full((W_GRP, W_GRP)), _full((1, W_GRP))],
        out_shape=[jax.ShapeDtypeStruct((s, W_GRP), F32), jax.ShapeDtypeStruct((W_GRP, W_GRP), F32),
                   jax.ShapeDtypeStruct((1, W_GRP), F32)],
        compiler_params=_cp(dimension_semantics=("arbitrary",)),
    )(y, dyd, gw, gb)


def _s5_core_bwd(z, dy, sp):
    s = z.shape[0]
    seg = s // N_SEG

    def body(u_ref, dy_ref, rowp, colp, br_ref, bi_ref, cr_ref, ci_ref, d_ref,
             du_ref, dbr_ref, dbi_ref, dcr_ref, dci_ref, dd_ref, da_ref, dk_ref,
             xr, xi, gr, gi, us, dys):
        ar, ai, _, _ = _s5_disc(rowp[0:1, :], rowp[1:2, :], rowp[2:3, :])
        br, bi = br_ref[...], bi_ref[...]
        kr, ki, btr, bti, ctr, cti, bmask, cmask = _s5_mats(colp[...], br, bi, cr_ref[...], ci_ref[...])
        _interleave(u_ref, us, seg)
        _interleave(dy_ref, dys, seg)
        u = us[...]
        d = d_ref[...]
        _s5_forward_states(u, btr, bti, ar, ai, xr, xi, seg)

        dy = dys[...]
        dd_ref[...] = jnp.sum(dy * u, axis=0, keepdims=True)
        du = d * dy
        dyb = dy.astype(BF16)
        dctr, dcti = [], []
        for q in range(N_SLAB):
            gr[q] = jnp.dot(dyb, ctr[:, _slab(q)], preferred_element_type=F32)
            gi[q] = -jnp.dot(dyb, cti[:, _slab(q)], preferred_element_type=F32)
            dctr.append(_dot_tn(dyb, xr[q]))
            dcti.append(-_dot_tn(dyb, xi[q]))
        selp = ((_iota((HALF_STATE, 64), 0) & 63) == _iota((HALF_STATE, 64), 1)).astype(F32)
        dcr_ref[...] = _dot_hi(jnp.where(cmask, jnp.concatenate(dctr, axis=1), 0.0), selp)
        dci_ref[...] = _dot_hi(jnp.where(cmask, jnp.concatenate(dcti, axis=1), 0.0), selp)

        _scan(gr, gi, ar, -ai, seg, True)

        dar, dai = [], []
        for q in range(N_SLAB):
            def acc_step(j, carry, q=q):
                rows, prev = _seg_rows(j), _seg_rows(j - 1)
                g_r, g_i, p_r, p_i = gr[q, rows, :], gi[q, rows, :], xr[q, prev, :], xi[q, prev, :]
                return carry[0] + g_r * p_r + g_i * p_i, carry[1] - g_r * p_i + g_i * p_r
            first, last = _seg_rows(0), _seg_rows(seg - 1)
            p_r, p_i = _sub_shift(xr[q, last, :], 1, False), _sub_shift(xi[q, last, :], 1, False)
            g_r, g_i = gr[q, first, :], gi[q, first, :]
            s_r, s_i = lax.fori_loop(1, seg, acc_step, (g_r * p_r + g_i * p_i, -g_r * p_i + g_i * p_r))
            dar.append(jnp.sum(s_r, axis=0, keepdims=True))
            dai.append(jnp.sum(s_i, axis=0, keepdims=True))
        da_ref[...] = jnp.zeros_like(da_ref)
        da_ref[0:1, :] = jnp.concatenate(dar, axis=1)
        da_ref[1:2, :] = jnp.concatenate(dai, axis=1)

        ub = u.astype(BF16)
        dbtr, dbti = [], []
        for q in range(N_SLAB):
            g_r, g_i = gr[q].astype(BF16), gi[q].astype(BF16)
            du = du + jnp.dot(g_r, btr[_slab(q), :], preferred_element_type=F32) \
                + jnp.dot(g_i, bti[_slab(q), :], preferred_element_type=F32)
            dbtr.append(_dot_tn(g_r, ub))
            dbti.append(_dot_tn(g_i, ub))
        us[...] = du
        _deinterleave(us, du_ref, seg)
        selc =((_iota((HALF_CH, 16), 0) & 15) == _iota((HALF_CH, 16), 1)).astype(F32)
        dbbr = _dot_hi(jnp.where(bmask, jnp.concatenate(dbtr, axis=0), 0.0), selc)
        dbbi = _dot_hi(jnp.where(bmask, jnp.concatenate(dbti, axis=0), 0.0), selc)
        dbr_ref[...] = kr * dbbr + ki * dbbi
        dbi_ref[...] = kr * dbbi - ki * dbbr
        dk_ref[:, 0:1] = jnp.sum(dbbr * br + dbbi * bi, axis=1, keepdims=True)
        dk_ref[:, 1:2] = jnp.sum(dbbi * br - dbbr * bi, axis=1, keepdims=True)

    half = pl.BlockSpec((s, HALF_CH), lambda i: (0, i))
    return pl.pallas_call(
        body, name="s5_core_bwd", grid=(2,),
        in_specs=[pl.BlockSpec((s, HALF_CH), lambda i: (0, 12 + i)), half] + _s5_param_specs(),
        out_specs=[half, pl.BlockSpec((HALF_STATE, 16), lambda i: (i, 0)), pl.BlockSpec((HALF_STATE, 16), lambda i: (i, 0)),
                   pl.BlockSpec((HALF_CH, 64), lambda i: (i, 0)), pl.BlockSpec((HALF_CH, 64), lambda i: (i, 0)),
                   pl.BlockSpec((1, HALF_CH), lambda i: (0, i)), pl.BlockSpec((8, HALF_STATE), lambda i: (0, i)),
                   pl.BlockSpec((HALF_STATE, 2), lambda i: (i, 0))],
        out_shape=[jax.ShapeDtypeStruct((s, W_GRP), F32), jax.ShapeDtypeStruct((N_STATE, 16), F32),
                   jax.ShapeDtypeStruct((N_STATE, 16), F32), jax.ShapeDtypeStruct((W_GRP, 64), F32),
                   jax.ShapeDtypeStruct((W_GRP, 64), F32), jax.ShapeDtypeStruct((1, W_GRP), F32),
                   jax.ShapeDtypeStruct((8, N_STATE), F32), jax.ShapeDtypeStruct((N_STATE, 2), F32)],
        scratch_shapes=[pltpu.VMEM((N_SLAB, s, LANES), F32)] * 4 + [pltpu.VMEM((s, HALF_CH), F32)] * 2,
        compiler_params=_cp(dimension_semantics=("parallel",)),
    )(z, dy, *sp)


def _s5_param_bwd(lre, lim, ldt, da_r, da_i, dk_r, dk_i):
    n = lre.shape[0]

    def body(lre_ref, lim_ref, ldt_ref, dar_ref, dai_ref, dkr_ref, dki_ref, o_re, o_im, o_dt):
        lre, lim, ldt = lre_ref[...], lim_ref[...], ldt_ref[...]
        dt = jnp.exp(ldt)
        ar, ai, kr, ki = _s5_disc(lre, lim, ldt)
        mag = jnp.exp(lre * dt)
        den = lre * lre + lim * lim
        dkr, dki = dkr_ref[...], dki_ref[...]
        nr, ni = ar - 1.0, ai
        d_ar = dar_ref[...] + (dkr * lre - dki * lim) / den
        d_ai = dai_ref[...] + (dkr * lim + dki * lre) / den
        kk = (kr * dkr + ki * dki) * 2.0 / den
        d_lre = (dkr * nr + dki * ni) / den - kk * lre
        d_lim = (dkr * ni - dki * nr) / den - kk * lim
        d_mag = (d_ar * ar + d_ai * ai) / mag
        d_ang = d_ai * ar - d_ar * ai
        o_re[...] = d_lre + d_mag * mag * dt
        o_im[...] = d_lim + d_ang * dt
        o_dt[...] = jnp.sum((d_mag * mag * lre + d_ang * lim) * dt, axis=1, keepdims=True)

    return pl.pallas_call(
        body, name="s5_param_bwd",
        out_shape=[jax.ShapeDtypeStruct((n, 64), F32), jax.ShapeDtypeStruct((n, 64), F32),
                   jax.ShapeDtypeStruct((n, 1), F32)],
    )(lre, lim, ldt, da_r, da_i, dk_r, dk_i)


def _loss_head(x, fg, target):
    s, d = x.shape
    tm = _tm(s)

    def body(x_ref, fg_ref, t_ref, loss_ref, dx_ref, dfg_ref):
        i = pl.program_id(0)

        @pl.when(i == 0)
        def _():
            loss_ref[...] = jnp.zeros_like(loss_ref)
            dfg_ref[...] = jnp.zeros_like(dfg_ref)

        xv, g = x_ref[...], fg_ref[...]
        r = lax.rsqrt(jnp.mean(xv * xv, axis=-1, keepdims=True) + EPS)
        xh = xv * r
        err = xh * g - t_ref[...]
        loss_ref[...] += 0.5 * jnp.sum(jnp.mean(err * err, axis=-1, keepdims=True), axis=0, keepdims=True)
        dy = err * (1.0 / d)
        dfg_ref[...] += jnp.sum(dy * xh, axis=0, keepdims=True)
        dxh = dy * g
        dx_ref[...] = r * (dxh - xh * jnp.mean(dxh * xh, axis=-1, keepdims=True))

    row = pl.BlockSpec((tm, d), lambda i: (i, 0))
    return pl.pallas_call(
        body, name="loss_head", grid=(s // tm,),
        in_specs=[row, _full((1, d)), row], out_specs=[_full((1, 1)), row, _full((1, d))],
        out_shape=[jax.ShapeDtypeStruct((1, 1), F32), jax.ShapeDtypeStruct((s, d), F32),
                   jax.ShapeDtypeStruct((1, d), F32)],
        compiler_params=_cp(dimension_semantics=("arbitrary",)),
    )(x, fg, target)


ADA_TN = 384


def _cond_fwd(cact, ada_w, ada_b_loc):
    nl, d, n = ada_w.shape

    def body(c_ref, w_ref, b_ref, o_ref):
        o_ref[...] = _dot(c_ref[...], w_ref[...]) + b_ref[...]

    return pl.pallas_call(
        body, name="cond_fwd", grid=(nl, n // ADA_TN),
        in_specs=[_full((N_DEV, d)), pl.BlockSpec((None, d, ADA_TN), lambda l, j: (l, 0, j)),
                  pl.BlockSpec((None, 1, ADA_TN), lambda l, j: (l, 0, j))],
        out_specs=pl.BlockSpec((None, N_DEV, ADA_TN), lambda l, j: (l, 0, j)),
        out_shape=jax.ShapeDtypeStruct((nl, N_DEV, n), F32),
        compiler_params=_cp(dimension_semantics=("parallel", "parallel")),
    )(cact, ada_w, ada_b_loc)


ELEMENTWISE_BLOCK_BYTES = 1 << 20


def _row_tile(r, c, itemsize=4):
    best = None
    for t in range(8, r + 1, 8):
        if r % t == 0 and t * c * itemsize <= ELEMENTWISE_BLOCK_BYTES:
            best = t
    return best if best is not None else r


def _adamw_math(w, g, m, v):
    m = ADAM_B1 * m + (1.0 - ADAM_B1) * g
    v = ADAM_B2 * v + (1.0 - ADAM_B2) * (g * g)
    m_hat = m / (1.0 - ADAM_B1 ** ADAM_STEP)
    v_hat = v / (1.0 - ADAM_B2 ** ADAM_STEP)
    delta = -ADAM_LR * (m_hat / (jnp.sqrt(v_hat) + ADAM_EPS) + ADAM_WD * w)
    return delta, m, v


def _ada_w_update(cact, dcond_loc, w, m, v):
    nl, d, n = w.shape

    def body(c_ref, dc_ref, w_ref, m_ref, v_ref, g_out, d_out, m_out, v_out):
        g = _dot_tn(c_ref[...], dc_ref[...])
        g_out[...] = g
        d_out[...], m_out[...], v_out[...] = _adamw_math(w_ref[...], g, m_ref[...], v_ref[...])

    blk = pl.BlockSpec((None, d, ADA_TN), lambda l, j: (l, 0, j))
    return pl.pallas_call(
        body, name="ada_w_update", grid=(nl, n // ADA_TN),
        in_specs=[_full((N_DEV, d)), pl.BlockSpec((None, N_DEV, ADA_TN), lambda l, j: (l, 0, j)), blk, blk, blk],
        out_specs=[blk] * 4, out_shape=[jax.ShapeDtypeStruct((nl, d, n), F32)] * 4,
        compiler_params=_cp(dimension_semantics=("parallel", "parallel")),
    )(cact, dcond_loc, w, m, v)


def _place():
    x, y, c = lax.axis_index("x"), lax.axis_index("y"), lax.axis_index("c")
    chips = [(1 - x, y), (x, 1 - y), (1 - x, 1 - y)]
    return x, y, c, chips


def _remote(src, dst, send_sem, recv_sem, to):
    return pltpu.make_async_remote_copy(src_ref=src, dst_ref=dst, send_sem=send_sem, recv_sem=recv_sem,
                                        device_id=to, device_id_type=MESH_ID)


def _sems(n):
    return [pltpu.SemaphoreType.DMA((n,)), pltpu.SemaphoreType.DMA((n,))]


def _all_gather8(v, name):
    r, cdim = v.shape

    def body(x_ref, out_ref, stage, send_sems, recv_sems):
        x, y, c, chips = _place()
        sibling = (x, y, 1 - c)

        def slot(px, py, pc):
            return out_ref.at[4 * px + 2 * py + pc]

        first = [_remote(x_ref, slot(x, y, c), send_sems.at[0], recv_sems.at[0], sibling)]
        first += [_remote(x_ref, slot(x, y, c), send_sems.at[1 + j], recv_sems.at[1 + j], (*chip, c))
                  for j, chip in enumerate(chips)]
        for cp in first:
            cp.start()
        pltpu.sync_copy(x_ref, stage)
        pltpu.sync_copy(stage, slot(x, y, c))
        passed = []
        for j, chip in enumerate(chips):
            blk = slot(*chip, c)
            _remote(blk, blk, send_sems.at[1 + j], recv_sems.at[1 + j], (x, y, c)).wait_recv()
            fw = _remote(blk, blk, send_sems.at[4 + j], recv_sems.at[4 + j], sibling)
            fw.start()
            passed.append(fw)
        blk = slot(x, y, 1 - c)
        _remote(blk, blk, send_sems.at[0], recv_sems.at[0], (x, y, c)).wait_recv()
        for j, chip in enumerate(chips):
            blk = slot(*chip, 1 - c)
            _remote(blk, blk, send_sems.at[4 + j], recv_sems.at[4 + j], (x, y, c)).wait_recv()
        for cp in first + passed:
            cp.wait_send()

    return pl.pallas_call(
        body, name=name, out_shape=jax.ShapeDtypeStruct((N_DEV, r, cdim), v.dtype),
        in_specs=[ANY], out_specs=ANY,
        scratch_shapes=[pltpu.VMEM((r, cdim), v.dtype)] + _sems(7),
        compiler_params=_cp(),
    )(v)


def _gather_first_copies():
    def make(refs, send_sems, recv_sems):
        x, y, c, chips = _place()
        mine = refs[0].at[4 * x + 2 * y + c]
        to = [(x, y, 1 - c)] + [(*chip, c) for chip in chips]
        return [_remote(mine, mine, send_sems.at[k], recv_sems.at[k], dev) for k, dev in enumerate(to)]
    return make


def _gather_pass_on(buf, name):
    def body(in_ref, out_ref, send_sems, recv_sems):
        x, y, c, chips = _place()
        passed = []
        for j, chip in enumerate(chips):
            blk = out_ref.at[4 * chip[0] + 2 * chip[1] + c]
            fw = _remote(blk, blk, send_sems.at[j], recv_sems.at[j], (x, y, 1 - c))
            fw.start()
            passed.append(fw)
        for j, chip in enumerate(chips):
            blk = out_ref.at[4 * chip[0] + 2 * chip[1] + 1 - c]
            _remote(blk, blk, send_sems.at[j], recv_sems.at[j], (x, y, c)).wait_recv()
        for fw in passed:
            fw.wait_send()

    return pl.pallas_call(
        body, name=name, out_shape=jax.ShapeDtypeStruct(buf.shape, buf.dtype),
        in_specs=[ANY], out_specs=ANY, input_output_aliases={0: 0}, scratch_shapes=_sems(3),
    )(buf)


def _place_weights(ws, layer, kidx, after):
    steps = 4
    shapes, in_specs, out_specs = [], [], []
    for w, kind in zip(ws, BIG_KINDS):
        _, a, b = w.shape
        in_specs.append(pl.BlockSpec((None, a // steps, b), lambda i, k: (layer, i, 0)))
        if kind == "col":
            shapes.append((2, a, 2 * b))
            out_specs.append(pl.BlockSpec((None, a // steps, b), lambda i, k: (k[0] // 2, i, k[0] % 2)))
        else:
            shapes.append((N_CHIP, a, b))
            out_specs.append(pl.BlockSpec((None, a // steps, b), lambda i, k: (k[0], i, 0)))

    def body(k_ref, *refs):
        outs = refs[len(ws) + 1:]
        for t in range(len(ws)):
            outs[t][...] = refs[t][...].astype(BF16)

    return pl.pallas_call(
        body, name="place_weights", out_shape=[jax.ShapeDtypeStruct(s, BF16) for s in shapes],
        grid_spec=pltpu.PrefetchScalarGridSpec(num_scalar_prefetch=1, grid=(steps,), in_specs=in_specs + [ANY],
                                               out_specs=out_specs),
        compiler_params=_cp(dimension_semantics=("parallel",)),
    )(kidx, *ws, after)


HBM = pl.BlockSpec(memory_space=pltpu.HBM)
SEM = pl.BlockSpec(memory_space=pltpu.SEMAPHORE)
EFFECT = pltpu.SideEffectType.DATAFLOW_SIDE_EFFECTING


def _weight_block(ref, kind, k, h):
    if kind == "col":
        ncol = ref.shape[3] // 2
        return ref.at[k // 2, h, :, pl.ds(pl.multiple_of((k % 2) * ncol, LANES), ncol)]
    return ref.at[k, h]


def _in_hbm(a):
    return pltpu.with_memory_space_constraint(a, pltpu.HBM)


def _weight_send_start(placed, kinds, name):
    nt = len(placed)

    def body(*refs):
        send_sems, recv_sems = refs[nt], refs[nt + 1]
        dst = refs[nt + 2:2 * nt + 2]
        token = refs[2 * nt + 2]
        x, y, c, chips = _place()
        kme = 2 * x + y
        for t in range(nt):
            for j, chip in enumerate(chips):
                own = _weight_block(dst[t], kinds[t], kme, c)
                _remote(own, own, send_sems.at[3 * t + j], recv_sems.at[3 * t + j], (*chip, c)).start()
        token[...] = jnp.zeros_like(token)

    return pl.pallas_call(
        body, name=name,
        out_shape=(pltpu.SemaphoreType.DMA((3 * nt,)), pltpu.SemaphoreType.DMA((3 * nt,)),
                   *[pltpu.HBM(a.shape, a.dtype) for a in placed], jax.ShapeDtypeStruct((8, LANES), F32)),
        in_specs=[HBM] * nt, out_specs=(SEM, SEM, *[HBM] * nt, pl.BlockSpec(memory_space=pltpu.VMEM)),
        input_output_aliases={t: 2 + t for t in range(nt)},
        compiler_params=pltpu.CompilerParams(has_side_effects=EFFECT),
    )(*[_in_hbm(a) for a in placed])


def _weight_send_wait(send_sems, recv_sems, arrays, kinds, after, name):
    nt = len(arrays)

    def body(*refs):
        arr = refs[:nt]
        send_sems, recv_sems = refs[nt], refs[nt + 1]
        x, y, c, chips = _place()
        kme = 2 * x + y
        for t in range(nt):
            for j, chip in enumerate(chips):
                own = _weight_block(arr[t], kinds[t], kme, c)
                got = _weight_block(arr[t], kinds[t], 2 * chip[0] + chip[1], c)
                cp = _remote(own, got, send_sems.at[3 * t + j], recv_sems.at[3 * t + j], (*chip, c))
                cp.wait_send()
                cp.wait_recv()

    return pl.pallas_call(
        body, name=name, out_shape=[pltpu.HBM(a.shape, a.dtype) for a in arrays],
        in_specs=[HBM] * nt + [SEM, SEM, ANY], out_specs=[HBM] * nt,
        input_output_aliases={t: t for t in range(nt)},
        compiler_params=pltpu.CompilerParams(has_side_effects=EFFECT),
    )(*arrays, send_sems, recv_sems, after)


def _forward_copies(kinds):
    def make(refs, send_sems, recv_sems):
        x, y, c, chips = _place()
        cps = []
        for t in range(len(kinds)):
            for j, chip in enumerate(chips):
                blk = _weight_block(refs[t], kinds[t], 2 * chip[0] + chip[1], c)
                cps.append(_remote(blk, blk, send_sems.at[3 * t + j], recv_sems.at[3 * t + j], (x, y, 1 - c)))
        return cps
    return make


def _split_start(name, arrays, n_copies, make_copies):
    na = len(arrays)

    def body(*refs):
        send_sems, recv_sems = refs[na], refs[na + 1]
        for cp in make_copies(refs[na + 2:2 * na + 2], send_sems, recv_sems):
            cp.start()
        token = refs[2 * na + 2]
        token[...] = jnp.zeros_like(token)

    return pl.pallas_call(
        body, name=name,
        out_shape=(pltpu.SemaphoreType.DMA((n_copies,)), pltpu.SemaphoreType.DMA((n_copies,)),
                   *[pltpu.HBM(a.shape, a.dtype) for a in arrays], jax.ShapeDtypeStruct((8, LANES), F32)),
        in_specs=[HBM] * na, out_specs=(SEM, SEM, *[HBM] * na, pl.BlockSpec(memory_space=pltpu.VMEM)),
        input_output_aliases={t: 2 + t for t in range(na)},
        compiler_params=pltpu.CompilerParams(has_side_effects=EFFECT),
    )(*[_in_hbm(a) for a in arrays])


def _split_wait(name, started, make_copies, after):
    send_sems, recv_sems, *arrays, _ = started
    na = len(arrays)

    def body(*refs):
        send_sems, recv_sems = refs[na], refs[na + 1]
        for cp in make_copies(refs[:na], send_sems, recv_sems):
            cp.wait_send()
            cp.wait_recv()

    return pl.pallas_call(
        body, name=name, out_shape=[pltpu.HBM(a.shape, a.dtype) for a in arrays],
        in_specs=[HBM] * na + [SEM, SEM, ANY], out_specs=[HBM] * na,
        input_output_aliases={t: t for t in range(na)},
        compiler_params=pltpu.CompilerParams(has_side_effects=EFFECT),
    )(*arrays, send_sems, recv_sems, after)


def _exchange_copies(nt):
    def make(refs, send_sems, recv_sems):
        x, y, c, _ = _place()
        return [_remote(refs[t].at[:, 1 - c], refs[nt + t], send_sems.at[t], recv_sems.at[t], (x, y, 1 - c))
                for t in range(nt)]
    return make


def _sibling_exchange_start(views, name):
    lands = [lax.empty((v.shape[0],) + v.shape[2:], v.dtype) for v in views]
    return _split_start(name, list(views) + lands, len(views), _exchange_copies(len(views)))


def _sibling_exchange_wait(started, after, name):
    nt = (len(started) - 3) // 2
    outs = _split_wait(name, started, _exchange_copies(nt), after)
    return outs[:nt], outs[nt:]


def _scatter_copies(src, land, kinds, send_sems, recv_sems):
    x, y, c, chips = _place()
    cps = []
    for t in range(len(src)):
        for j, chip in enumerate(chips):
            k = 2 * chip[0] + chip[1]
            if kinds[t] == "col":
                ncol = land[t].shape[2]
                win = src[t].at[k // 2, :, pl.ds(pl.multiple_of((k % 2) * ncol, LANES), ncol)]
            else:
                win = src[t].at[k]
            cps.append(_remote(win, land[t].at[j], send_sems.at[3 * t + j], recv_sems.at[3 * t + j], (*chip, c)))
    return cps


def _chip_scatter_start(parts, kinds, name):
    nt = len(parts)
    shapes = []
    for p, kind in zip(parts, kinds):
        shapes.append((3, p.shape[1], p.shape[2] // 2) if kind == "col" else (3,) + p.shape[1:])

    def body(*refs):
        send_sems, recv_sems = refs[2 * nt], refs[2 * nt + 1]
        src, land = refs[2 * nt + 2:3 * nt + 2], refs[3 * nt + 2:4 * nt + 2]
        token = refs[4 * nt + 2]
        for cp in _scatter_copies(src, land, kinds, send_sems, recv_sems):
            cp.start()
        token[...] = jnp.zeros_like(token)

    lands = [lax.empty(s, BF16) for s in shapes]
    return pl.pallas_call(
        body, name=name,
        out_shape=(pltpu.SemaphoreType.DMA((3 * nt,)), pltpu.SemaphoreType.DMA((3 * nt,)),
                   *[pltpu.HBM(a.shape, a.dtype) for a in parts], *[pltpu.HBM(s, BF16) for s in shapes],
                   jax.ShapeDtypeStruct((8, LANES), F32)),
        in_specs=[HBM] * (2 * nt), out_specs=(SEM, SEM, *[HBM] * (2 * nt), pl.BlockSpec(memory_space=pltpu.VMEM)),
        input_output_aliases={t: 2 + t for t in range(2 * nt)},
        compiler_params=pltpu.CompilerParams(has_side_effects=EFFECT),
    )(*[_in_hbm(a) for a in parts], *[_in_hbm(a) for a in lands])


def _chip_scatter_wait(send_sems, recv_sems, parts, lands, kinds, after, name):
    nt = len(parts)

    def body(*refs):
        src, land = refs[:nt], refs[nt:2 * nt]
        send_sems, recv_sems = refs[2 * nt], refs[2 * nt + 1]
        for cp in _scatter_copies(src, land, kinds, send_sems, recv_sems):
            cp.wait_send()
            cp.wait_recv()

    outs = pl.pallas_call(
        body, name=name, out_shape=[pltpu.HBM(a.shape, a.dtype) for a in list(parts) + list(lands)],
        in_specs=[HBM] * (2 * nt) + [SEM, SEM, ANY], out_specs=[HBM] * (2 * nt),
        input_output_aliases={t: t for t in range(2 * nt)},
        compiler_params=pltpu.CompilerParams(has_side_effects=EFFECT),
    )(*parts, *lands, send_sems, recv_sems, after)
    return outs[:nt], outs[nt:]


def _share_copies(nt):
    def make(refs, send_sems, recv_sems):
        x, y, c, _ = _place()
        return [_remote(refs[t].at[c], refs[t].at[c], send_sems.at[t], recv_sems.at[t], (x, y, 1 - c))
                for t in range(nt)]
    return make


def _sibling_share_start(fulls, name):
    return _split_start(name, list(fulls), len(fulls), _share_copies(len(fulls)))


def _sibling_share_wait(started, after, name):
    return _split_wait(name, started, _share_copies(len(started) - 3), after)


SUM_STEPS = 4


def _pair_sum(views, lands, ck):
    nt = len(views)
    in_specs, out_specs, shapes = [], [], []
    for v in views:
        b, _, r, cc = v.shape
        per = SUM_STEPS // b
        tr = r // per
        in_specs.append(pl.BlockSpec((None, None, tr, cc), lambda i, s, per=per: (i // per, s[0], i % per, 0)))
        out_specs.append(pl.BlockSpec((None, tr, cc), lambda i, s, per=per: (i // per, i % per, 0)))
        shapes.append((b, r, cc))
    in_specs = in_specs + out_specs

    def body(s_ref, *refs):
        for t in range(nt):
            refs[2 * nt + t][...] = (refs[t][...].astype(F32) + refs[nt + t][...].astype(F32)).astype(BF16)

    return pl.pallas_call(
        body, name="grad_pair_sum", out_shape=[jax.ShapeDtypeStruct(s, BF16) for s in shapes],
        grid_spec=pltpu.PrefetchScalarGridSpec(num_scalar_prefetch=1, grid=(SUM_STEPS,), in_specs=in_specs,
                                               out_specs=out_specs),
        compiler_params=_cp(dimension_semantics=("parallel",)),
    )(ck, *views, *lands)


def _chip_sum(parts, lands, kinds, ck):
    nt = len(parts)
    steps = 2
    in_own, in_land, out_specs, shapes = [], [], [], []
    for ld, kind in zip(lands, kinds):
        _, r, cc = ld.shape
        tr = r // steps
        if kind == "col":
            in_own.append(pl.BlockSpec((None, tr, cc), lambda i, s: (s[1] // 2, i, s[1] % 2)))
        else:
            in_own.append(pl.BlockSpec((None, tr, cc), lambda i, s: (s[1], i, 0)))
        in_land.append(pl.BlockSpec((3, tr, cc), lambda i, s: (0, i, 0)))
        out_specs.append(pl.BlockSpec((None, tr, cc), lambda i, s: (s[0], i, 0)))
        shapes.append((2, r, cc))

    def body(s_ref, *refs):
        for t in range(nt):
            acc = refs[t][...].astype(F32)
            for j in range(3):
                acc = acc + refs[nt + t][j].astype(F32)
            refs[2 * nt + t][...] = acc

    return pl.pallas_call(
        body, name="grad_chip_sum", out_shape=[jax.ShapeDtypeStruct(s, F32) for s in shapes],
        grid_spec=pltpu.PrefetchScalarGridSpec(num_scalar_prefetch=1, grid=(steps,), in_specs=in_own + in_land,
                                               out_specs=out_specs),
        compiler_params=_cp(dimension_semantics=("parallel",)),
    )(ck, *parts, *lands)


def _sum8(g):
    _, r, cc = g.shape
    tr = _row_tile(r, N_DEV * cc)

    def body(g_ref, o_ref):
        acc = g_ref[0].astype(F32)
        for d in range(1, N_DEV):
            acc = acc + g_ref[d].astype(F32)
        o_ref[...] = acc

    return pl.pallas_call(
        body, name="small_grad_sum", grid=(r // tr,),
        in_specs=[pl.BlockSpec((N_DEV, tr, cc), lambda i: (0, i, 0))],
        out_specs=pl.BlockSpec((tr, cc), lambda i: (i, 0)),
        out_shape=jax.ShapeDtypeStruct((r, cc), F32),
        compiler_params=_cp(dimension_semantics=("parallel",)),
    )(g)


def _silu_rows(c):
    def body(c_ref, o_ref):
        v = c_ref[...]
        o_ref[...] = v * jax.nn.sigmoid(v)

    return pl.pallas_call(body, name="cond_silu", out_shape=jax.ShapeDtypeStruct(c.shape, F32))(c)


def _pack(arrays):
    rows = []
    for a in arrays:
        flat = a.reshape(-1)
        rows.append(jnp.pad(flat, (0, (-flat.shape[0]) % (8 * LANES))).reshape(-1, LANES))
    n = sum(r.shape[0] for r in rows)
    if n % 256:
        rows.append(jnp.zeros((256 - n % 256, LANES), rows[0].dtype))
    return jnp.concatenate(rows, axis=0)


def _unpack(packed, shapes):
    out, off = [], 0
    for s in shapes:
        n = math.prod(s)
        nr = 8 * -(-n // (8 * LANES))
        out.append(packed[off:off + nr].reshape(-1)[:n].reshape(s))
        off += nr
    return out


def _as_rows(a):
    return a.reshape(1, -1) if a.ndim == 1 else a.reshape(-1, a.shape[-1])


def _adamw_many(ws, gs, ms, vs, name, steps=1):
    nt = len(ws)

    def body(*refs):
        for t in range(nt):
            w_ref, g_ref, m_ref, v_ref = (refs[k * nt + t] for k in range(4))
            d, m, v = _adamw_math(w_ref[...], g_ref[...], m_ref[...], v_ref[...])
            refs[4 * nt + t][...] = d
            refs[5 * nt + t][...] = m
            refs[6 * nt + t][...] = v

    shapes = [jax.ShapeDtypeStruct(a.shape, F32) for a in ws]
    if steps == 1:
        outs = pl.pallas_call(body, name=name, out_shape=shapes * 3, compiler_params=_cp())(*ws, *gs, *ms, *vs)
    else:
        specs = [pl.BlockSpec((a.shape[0] // steps, a.shape[1]), lambda i: (i, 0)) for a in ws]
        outs = pl.pallas_call(
            body, name=name, grid=(steps,), in_specs=specs * 4, out_specs=specs * 3, out_shape=shapes * 3,
            compiler_params=_cp(dimension_semantics=("parallel",)),
        )(*ws, *gs, *ms, *vs)
    return outs[:nt], outs[nt:2 * nt], outs[2 * nt:]


def _exchange_big_grads(grads, kinds, layer):
    views = []
    for g, kind in zip(grads, kinds):
        if kind == "col":
            views.append(g.reshape(2, 2, g.shape[1] // 2, g.shape[2]))
        else:
            views.append(g.reshape(N_CHIP, 2, g.shape[0] // (2 * N_CHIP), g.shape[1]))
    return _sibling_exchange_start(views, "grad_exchange_start_%d" % layer)


def _scatter_big_grads(exchanged, kinds, ck, after, layer):
    views, lands = _sibling_exchange_wait(exchanged, after, "grad_exchange_wait_%d" % layer)
    parts = _pair_sum(views, lands, ck)
    return _chip_scatter_start(parts, kinds, "grad_scatter_start_%d" % layer)


def _finish_big_grads(started, kinds, ck, after, layer):
    nt = len(kinds)
    send_sems, recv_sems = started[0], started[1]
    parts, lands = started[2:2 + nt], started[2 + nt:2 + 2 * nt]
    parts, lands = _chip_scatter_wait(send_sems, recv_sems, parts, lands, kinds, after, "grad_scatter_wait_%d" % layer)
    return _sibling_share_start(_chip_sum(parts, lands, kinds, ck), "grad_share_start_%d" % layer)


def _adamw_layer(ws, gs, ms, vs, stacks, layer, name, steps):
    nt = len(ws)
    stacks = [s if s is not None else tuple(lax.empty(w.shape, F32) for _ in range(4)) for s, w in zip(stacks, ws)]

    def body(*refs):
        for t in range(nt):
            w_ref, g_ref, m_ref, v_ref = (refs[k * nt + t] for k in range(4))
            outs = refs[8 * nt + 4 * t:8 * nt + 4 * t + 4]
            g = g_ref[...]
            outs[0][...] = g
            outs[1][...], outs[2][...], outs[3][...] = _adamw_math(w_ref[...], g, m_ref[...], v_ref[...])

    in_specs, g_specs, out_specs = [], [], []
    for w in ws:
        _, r, c = w.shape
        in_specs.append(pl.BlockSpec((None, r // steps, c), lambda i: (layer, i, 0)))
        g_specs.append(pl.BlockSpec((r // steps, c), lambda i: (i, 0)))
        out_specs += [pl.BlockSpec((None, r // steps, c), lambda i: (layer, i, 0))] * 4
    in_specs = in_specs + g_specs + in_specs * 2 + [ANY] * (4 * nt)
    flat = [a for s in stacks for a in s]
    outs = pl.pallas_call(
        body, name=name, grid=(steps,), in_specs=in_specs, out_specs=out_specs,
        out_shape=[jax.ShapeDtypeStruct(a.shape, F32) for a in flat],
        input_output_aliases={4 * nt + k: k for k in range(4 * nt)},
        compiler_params=_cp(dimension_semantics=("parallel",)),
    )(*ws, *gs, *ms, *vs, *flat)
    return [tuple(outs[4 * t:4 * t + 4]) for t in range(nt)]


SMALL_NAMES = ["ada_b", "norm1_g", "norm2_g", "sgu_w", "sgu_b", "pool_w", "pool_scale", "conv_w", "s5_lambda_re",
               "s5_lambda_im", "s5_b_re", "s5_b_im", "s5_c_re", "s5_c_im", "s5_d", "s5_log_dt", "s5_glu_w", "s5_glu_b",
               "mix_norm_g", "norm3_g", "final_norm_g"]
BIG_NAMES = ["ffn1_w_in", "ffn1_w_out", "w_mix_in", "w_mix_out", "ffn2_w_in", "ffn2_w_out"]
BIG_KINDS = ["col", "row", "row", "row", "col", "row"]
WEIGHT_ORDER = ["ada_w", "ada_b", "norm1_g", "ffn1_w_in", "ffn1_w_out", "norm2_g", "w_mix_in", "sgu_w", "sgu_b", "pool_w",
                "pool_scale", "conv_w", "s5_lambda_re", "s5_lambda_im", "s5_b_re", "s5_b_im", "s5_c_re", "s5_c_im", "s5_d",
                "s5_log_dt", "s5_glu_w", "s5_glu_b", "mix_norm_g", "w_mix_out", "norm3_g", "ffn2_w_in", "ffn2_w_out",
                "final_norm_g"]


def _local_step(x, target, cond, fetch_weights, prefetch_weights, p, emit_grads):
    nl, d = DEPTH, x.shape[1]
    row = lambda a: a.reshape(1, -1)
    saved = []
    for l in range(nl):
        (wi1, wo1, wmit, wmo, wi2, wo2), tok = fetch_weights(l, x)
        cl = cond[l] + tok
        mod1, mod2, mod3 = cl[0:3], cl[3:6], cl[6:9]
        lre, lim = p["s5_lambda_re"][l].reshape(-1), p["s5_lambda_im"][l].reshape(-1)
        ldt = jnp.repeat(p["s5_log_dt"][l], 64)
        rowp = jnp.stack([lre, lim, ldt])
        sp = (rowp, rowp.T, p["s5_b_re"][l].reshape(N_STATE, 16), p["s5_b_im"][l].reshape(N_STATE, 16),
              p["s5_c_re"][l].reshape(W_GRP, 64), p["s5_c_im"][l].reshape(W_GRP, 64), row(p["s5_d"][l]))
        glu = (p["s5_glu_w"][l], row(p["s5_glu_b"][l]))
        bias_full = jnp.repeat(p["sgu_b"][l].T, 64, axis=1)
        pw2 = p["pool_w"][l].reshape(W_GRP, 64)
        x1, h1, a1, b1, o1 = _ffn_fwd(x, mod1, row(p["norm1_g"][l]), wi1, wo1)
        z, h2 = _mix_in_fwd(x1, mod2, row(p["norm2_g"][l]), wmit)
        ya = _sgu_fwd(z, p["sgu_w"][l], bias_full)
        yb, yc = _poolconv_fwd(z, pw2, row(p["pool_scale"][l]), p["conv_w"][l])
        ypre = _s5_core_fwd(z, sp)
        yd = _s5_glu_fwd(ypre, *glu)
        ys = (ya, yb, yc, yd)
        x2, m = _mix_out_fwd(ys, row(p["mix_norm_g"][l]), wmo, x1, mod2[2:3])
        mod3 = mod3 + prefetch_weights(l + 1, x2)
        x3, h3, a3, b3, o3 = _ffn_fwd(x2, mod3, row(p["norm3_g"][l]), wi2, wo2)
        saved.append((x, x1, x2, h1, a1, b1, o1, z, h2, ys, m, h3, a3, b3, o3, sp, bias_full, pw2, ypre, glu,
                      (wi1, wo1, wmit, wmo, wi2, wo2), cl))
        x = x3

    loss, dx, dfg = _loss_head(x, row(p["final_norm_g"]), target)

    sg = {n: [None] * nl for n in SMALL_NAMES if n not in ("ada_b", "final_norm_g")}
    dcond = [None] * nl
    s5_da, s5_dk = [None] * nl, [None] * nl
    tok = 0.0
    for l in reversed(range(nl)):
        (x0, x1, x2, h1, a1, b1, o1, z, h2, ys, m, h3, a3, b3, o3, sp, bias_full, pw2, ypre, glu,
         (wi1, wo1, wmit, wmo, wi2, wo2), cl) = saved[l]
        cl = cl + tok
        mod1, mod2, mod3 = cl[0:3], cl[3:6], cl[6:9]
        dza, dzb, dwi2, dwo2, dgate3 = _ffn_bwd_main(dx, o3, mod3[2:3], h3, a3, b3, wo2)
        dx, rows3 = _ffn_bwd_in(dza, dzb, wi2, x2, dx, mod3, row(p["norm3_g"][l]))
        outs = _mix_out_bwd(dx, m, mod2[2:3], ys, row(p["mix_norm_g"][l]), wmo)
        dys, dgate2, dmng, dwmo = outs[0:4], outs[4], outs[5], outs[6]
        dza_, dsw, dsb = _sgu_bwd(z, dys[0], p["sgu_w"][l], bias_full)
        dzb_, dzc_, dwbd, dps, dcw = _poolconv_bwd(z, dys[1], dys[2], pw2, row(p["pool_scale"][l]), p["conv_w"][l])
        dypre, dgw, dgb = _s5_glu_bwd(ypre, dys[3], *glu)
        dzd_, dbr, dbi, dcr, dci, dd, da, dk = _s5_core_bwd(z, dypre, sp)
        dx, rows2, dwmit = _mix_in_bwd((dza_, dzb_, dzc_, dzd_), h2, wmit, x1, dx, mod2, row(p["norm2_g"][l]))
        dza, dzb, dwi1, dwo1, dgate1 = _ffn_bwd_main(dx, o1, mod1[2:3], h1, a1, b1, wo1)
        tok, layer_done = emit_grads(l, [dwi1, dwo1, dwmit, dwmo, dwi2, dwo2])
        dx, rows1 = _ffn_bwd_in(dza, dzb, wi1, x0, dx, mod1 + tok, row(p["norm1_g"][l]))
        if l > 0:
            tok = layer_done(dx)[0, 0]
        dcond[l] = jnp.concatenate([rows1[0:2], dgate1, rows2[0:2], dgate2, rows3[0:2], dgate3], axis=0)
        sg["norm1_g"][l], sg["norm2_g"][l], sg["norm3_g"][l] = rows1[2], rows2[2], rows3[2]
        sg["mix_norm_g"][l] = dmng[0]
        sg["sgu_w"][l] = dsw
        sg["sgu_b"][l] = dsb[:, 0:4].T
        g4 = dwbd.reshape(4, 64, 4, 64)
        sg["pool_w"][l] = jnp.stack([g4[k, :, k, :] for k in range(4)])
        sg["pool_scale"][l] = dps[0]
        sg["conv_w"][l] = dcw[0:3]
        sg["s5_b_re"][l], sg["s5_b_im"][l] = dbr.reshape(16, 64, 16), dbi.reshape(16, 64, 16)
        sg["s5_c_re"][l], sg["s5_c_im"][l] = dcr.reshape(16, 16, 64), dci.reshape(16, 16, 64)
        sg["s5_d"][l] = dd[0]
        sg["s5_glu_w"][l], sg["s5_glu_b"][l] = dgw, dgb[0]
        s5_da[l], s5_dk[l] = da, dk

    n16 = nl * 16
    dlre, dlim, dldt = _s5_param_bwd(
        p["s5_lambda_re"].reshape(n16, 64), p["s5_lambda_im"].reshape(n16, 64),
        jnp.repeat(p["s5_log_dt"].reshape(n16, 1), 64, axis=1),
        jnp.stack([a[0] for a in s5_da]).reshape(n16, 64), jnp.stack([a[1] for a in s5_da]).reshape(n16, 64),
        jnp.stack([k[:, 0] for k in s5_dk]).reshape(n16, 64), jnp.stack([k[:, 1] for k in s5_dk]).reshape(n16, 64))
    small = {n: jnp.stack(v) for n, v in sg.items() if v[0] is not None}
    small["s5_lambda_re"] = dlre.reshape(nl, 16, 64)
    small["s5_lambda_im"] = dlim.reshape(nl, 16, 64)
    small["s5_log_dt"] = dldt.reshape(nl, 16)
    small["final_norm_g"] = dfg[0]
    return loss, dx, small, jnp.stack(dcond), layer_done


def kernel(x, c, ada_w, ada_b, norm1_g, ffn1_w_in, ffn1_w_out, norm2_g, w_mix_in, sgu_w, sgu_b, pool_w, pool_scale, conv_w, s5_lambda_re, s5_lambda_im, s5_b_re, s5_b_im, s5_c_re, s5_c_im, s5_d, s5_log_dt, s5_glu_w, s5_glu_b, mix_norm_g, w_mix_out, norm3_g, ffn2_w_in, ffn2_w_out, final_norm_g, loss_target, m_ada_w, m_ada_b, m_norm1_g, m_ffn1_w_in, m_ffn1_w_out, m_norm2_g, m_w_mix_in, m_sgu_w, m_sgu_b, m_pool_w, m_pool_scale, m_conv_w, m_s5_lambda_re, m_s5_lambda_im, m_s5_b_re, m_s5_b_im, m_s5_c_re, m_s5_c_im, m_s5_d, m_s5_log_dt, m_s5_glu_w, m_s5_glu_b, m_mix_norm_g, m_w_mix_out, m_norm3_g, m_ffn2_w_in, m_ffn2_w_out, m_final_norm_g, v_ada_w, v_ada_b, v_norm1_g, v_ffn1_w_in, v_ffn1_w_out, v_norm2_g, v_w_mix_in, v_sgu_w, v_sgu_b, v_pool_w, v_pool_scale, v_conv_w, v_s5_lambda_re, v_s5_lambda_im, v_s5_b_re, v_s5_b_im, v_s5_c_re, v_s5_c_im, v_s5_d, v_s5_log_dt, v_s5_glu_w, v_s5_glu_b, v_mix_norm_g, v_w_mix_out, v_norm3_g, v_ffn2_w_in, v_ffn2_w_out, v_final_norm_g):
    args = dict(locals())
    w = {n: args[n] for n in WEIGHT_ORDER}
    mom = {n: args["m_" + n] for n in WEIGHT_ORDER}
    vel = {n: args["v_" + n] for n in WEIGHT_ORDER}
    nl, d = DEPTH, x.shape[-1]
    s = x.shape[1]
    px, py, pc = lax.axis_index("x"), lax.axis_index("y"), lax.axis_index("c")
    kme = 2 * px + py
    me = 2 * kme + pc
    kidx = jnp.reshape(kme, (1,)).astype(jnp.int32)

    shards = [ffn1_w_in, ffn1_w_out, jnp.swapaxes(w_mix_in, 1, 2), w_mix_out, ffn2_w_in, ffn2_w_out]
    started_weights = {}

    def start_weights(l, after):
        placed = _place_weights(shards, l, kidx, after)
        views = [a.reshape(a.shape[0], 2, a.shape[1] // 2, a.shape[2]) for a in placed]
        *handles, token = _weight_send_start(views, BIG_KINDS, "weight_send_start_%d" % l)
        started_weights[l] = handles
        return token

    token = start_weights(0, c)
    cact = _silu_rows(c + token[0, 0])

    pre = _pack([cact, conv_w, s5_glu_w])
    pre_all = _all_gather8(pre, "gather_prelude")
    parts = [_unpack(pre_all[dev], [cact.shape, conv_w.shape, s5_glu_w.shape]) for dev in range(N_DEV)]
    cact_all = pre_all[:, :d // LANES, :].reshape(N_DEV, d)
    conv_full = jnp.concatenate([parts[2 * k][1] for k in range(N_CHIP)], axis=2)
    glu_full = jnp.concatenate([parts[2 * k][2] for k in range(N_CHIP)], axis=1)

    n_ada = ada_w.shape[2]
    ada_b_loc = lax.dynamic_slice_in_dim(ada_b, kme * n_ada, n_ada, axis=1).reshape(nl, 1, n_ada)
    cond_part = _cond_fwd(cact_all, ada_w, ada_b_loc)
    cond_all = _all_gather8(cond_part.reshape(nl * N_DEV, n_ada), "gather_cond").reshape(N_DEV, nl, N_DEV, n_ada)
    cond_me = jnp.concatenate(
        [lax.dynamic_index_in_dim(cond_all[2 * k], me, axis=1, keepdims=False) for k in range(N_CHIP)], axis=1)
    token = cond_all
    for l in range(1, nl):
        token = start_weights(l, token)
    cond = cond_me.reshape(nl, 9, d) + token[0, 0]

    forwarding = {}

    def prefetch_weights(l, after):
        if l >= nl:
            return 0.0
        send_sems, recv_sems, *views = started_weights.pop(l)
        views = _weight_send_wait(send_sems, recv_sems, views, BIG_KINDS, after, "weight_send_wait_%d" % l)
        forwarding[l] = _split_start("weight_forward_start_%d" % l, views, 3 * len(views), _forward_copies(BIG_KINDS))
        return forwarding[l][-1][0, 0]

    def fetch_weights(l, after):
        if l not in forwarding:
            prefetch_weights(l, after)
        views = _split_wait("weight_forward_wait_%d" % l, forwarding.pop(l), _forward_copies(BIG_KINDS), after)
        full = [v.reshape(2, 2 * v.shape[2], v.shape[3]) if kind == "col" else v.reshape(-1, v.shape[3])
                for v, kind in zip(views, BIG_KINDS)]
        return full, 0.0

    ck = jnp.stack([pc, kme]).astype(jnp.int32)
    scattering, sharing = [], []
    stacks = {n: None for n in BIG_NAMES}
    groups = ((["ffn1_w_in", "ffn2_w_in"], 16, "adamw_w_in"),
              (["ffn1_w_out", "w_mix_in", "w_mix_out", "ffn2_w_out"], 8, "adamw_w_out"))

    def apply_adamw(l, fulls):
        g = {n: f.reshape(2 * f.shape[1], f.shape[2]) for n, f in zip(BIG_NAMES, fulls)}
        g["w_mix_in"] = g["w_mix_in"].T
        for names, steps, call in groups:
            outs = _adamw_layer([w[n] for n in names], [g[n] for n in names], [mom[n] for n in names],
                                [vel[n] for n in names], [stacks[n] for n in names], l, call, steps)
            stacks.update(zip(names, outs))

    def retire_share(after):
        l2, shared = sharing.pop(0)
        apply_adamw(l2, _sibling_share_wait(shared, after, "grad_share_wait_%d" % l2))

    def retire_scatter(after):
        l1, scattered = scattering.pop(0)
        sharing.append((l1, _finish_big_grads(scattered, BIG_KINDS, ck, after, l1)))

    def retire(after):
        if sharing:
            retire_share(after)
        if scattering:
            retire_scatter(after)

    def emit_grads(l, grads_l):
        exchanged = _exchange_big_grads(grads_l, BIG_KINDS, l)

        def layer_done(after):
            started = _scatter_big_grads(exchanged, BIG_KINDS, ck, after, l)
            retire(after)
            scattering.append((l, started))
            return started[-1]

        return exchanged[-1][0, 0], layer_done

    p = {n: w[n] for n in SMALL_NAMES}
    p["conv_w"], p["s5_glu_w"] = conv_full, glu_full
    loss, dx, small, dcond, first_layer_done = _local_step(x[0], loss_target[0], cond, fetch_weights, prefetch_weights,
                                                           p, emit_grads)

    small_order = [n for n in SMALL_NAMES if n != "ada_b"]
    packed = _pack([dcond] + [small[n] for n in small_order]).astype(BF16)
    mine = lax.dynamic_update_slice(lax.empty((N_DEV,) + packed.shape, BF16), packed[None], (me, 0, 0))
    gathering = _split_start("small_grads_send_start", [mine], 4, _gather_first_copies())
    scatter_token = first_layer_done(gathering[-1])
    while sharing:
        retire_share(scatter_token)
    arrived, = _split_wait("small_grads_send_wait", gathering, _gather_first_copies(), stacks[BIG_NAMES[0]][0])
    gathered_small = _gather_pass_on(arrived, "small_grads_pass_on")
    total = _sum8(gathered_small)
    shapes = [dcond.shape] + [small[n].shape for n in small_order]
    tot = dict(zip(["ada_b"] + small_order, _unpack(total, shapes)))
    grads = {n: tot[n] for n in SMALL_NAMES}
    grads["ada_b"] = tot["ada_b"].reshape(nl, 9 * d)
    grads["conv_w"] = lax.dynamic_slice_in_dim(tot["conv_w"], kme * conv_w.shape[2], conv_w.shape[2], axis=2)
    grads["s5_glu_w"] = lax.dynamic_slice_in_dim(tot["s5_glu_w"], kme * s5_glu_w.shape[1], s5_glu_w.shape[1], axis=1)

    dcond_all = gathered_small.reshape(N_DEV, -1)[:, :dcond.size].reshape(N_DEV, nl, 9 * d)
    dcond_loc = jnp.swapaxes(lax.dynamic_slice_in_dim(dcond_all, kme * n_ada, n_ada, axis=2), 0, 1)
    g_ada, d_ada, m_ada, v_ada = _ada_w_update(cact_all, dcond_loc, ada_w, m_ada_w, v_ada_w)

    while scattering or sharing:
        retire(g_ada)
    delta, new_m, new_v = {}, {}, {}
    for n in BIG_NAMES:
        grads[n], delta[n], new_m[n], new_v[n] = stacks[n]

    grads["ada_w"], delta["ada_w"], new_m["ada_w"], new_v["ada_w"] = g_ada, d_ada, m_ada, v_ada
    wide = ("s5_b_re", "s5_b_im")
    for names, call, steps in (([n for n in SMALL_NAMES if n not in wide], "adamw_small", 1),
                               (list(wide), "adamw_s5_b", DEPTH)):
        outs = _adamw_many(*[[_as_rows(t[n]) for n in names] for t in (w, grads, mom, vel)], call, steps)
        for res, o in zip((delta, new_m, new_v), outs):
            res.update({n: a.reshape(w[n].shape) for n, a in zip(names, o)})

    loss_total = lax.psum(loss[0, 0], ("x", "y", "c"))
    return (loss_total, dx[None], *[grads[n] for n in WEIGHT_ORDER], *[delta[n] for n in WEIGHT_ORDER],
            *[new_m[n] for n in WEIGHT_ORDER], *[new_v[n] for n in WEIGHT_ORDER])
```

```python
import functools
import math

import jax
import jax.numpy as jnp
from jax import lax
from jax.experimental import pallas as pl
from jax.experimental.pallas import tpu as pltpu

F32, BF16 = jnp.float32, jnp.bfloat16
EPS = 1e-6
DEPTH = 4
N_DEV = 8
N_CHIP = 4
W_GRP = 256
CHUNK = 128
N_SEG = 8
LANES = 128
FFN_TF = 256
FFN_TF_WIDE = 1408
FFN_TM_WIDE = 512
VMEM_LIMIT = 56 * 1024 * 1024
ADAM_LR, ADAM_B1, ADAM_B2, ADAM_EPS, ADAM_WD, ADAM_STEP = 0.001, 0.9, 0.999, 1e-08, 0.01, 10
MESH_ID = pl.DeviceIdType.MESH
HI = lax.Precision.HIGHEST
ANY = pl.BlockSpec(memory_space=pl.ANY)


def _cp(**kw):
    return pltpu.CompilerParams(vmem_limit_bytes=VMEM_LIMIT, **kw)


def _dot(a, b):
    return jnp.dot(a.astype(BF16), b.astype(BF16), preferred_element_type=F32)


def _dot_nt(a, b):
    return lax.dot_general(a.astype(BF16), b.astype(BF16), (((1,), (1,)), ((), ())), preferred_element_type=F32)


def _dot_tn(a, b):
    return lax.dot_general(a.astype(BF16), b.astype(BF16), (((0,), (0,)), ((), ())), preferred_element_type=F32)


def _dot_hi(a, b):
    return jnp.dot(a, b, preferred_element_type=F32, precision=HI)


def _gelu(x):
    k = 0.7978845608028654
    t = jnp.tanh(k * (x + 0.044715 * x * x * x))
    return 0.5 * x * (1.0 + t), t


def _gelu_grad(x, t):
    k = 0.7978845608028654
    return 0.5 * (1.0 + t) + 0.5 * x * (1.0 - t * t) * k * (1.0 + 3.0 * 0.044715 * x * x)


def _iota(shape, axis):
    return lax.broadcasted_iota(jnp.int32, shape, axis)


def _full(shape):
    nd = len(shape)
    return pl.BlockSpec(shape, lambda *_: (0,) * nd)


def _norm_mod(xv, g, shift, scale):
    r = lax.rsqrt(jnp.mean(xv * xv, axis=-1, keepdims=True) + EPS)
    return (xv * r * g) * (1.0 + scale) + shift


def _norm_mod_bwd(xv, g, scale, dh):
    r = lax.rsqrt(jnp.mean(xv * xv, axis=-1, keepdims=True) + EPS)
    xh = xv * r
    n = xh * g
    dsh = jnp.sum(dh, axis=0, keepdims=True)
    dsc = jnp.sum(dh * n, axis=0, keepdims=True)
    dn = dh * (1.0 + scale)
    dg = jnp.sum(dn * xh, axis=0, keepdims=True)
    dxh = dn * g
    dx = r * (dxh - xh * jnp.mean(dxh * xh, axis=-1, keepdims=True))
    return dx, dsh, dsc, dg


def _tm(s):
    return min(s, 1024)


def _ffn_fwd(x, mod, g, wi, wo):
    s, d = x.shape
    f = wo.shape[0]
    tf, tm = FFN_TF_WIDE, min(s, FFN_TM_WIDE)
    nf, nt = f // tf, s // tm

    def body(x_ref, mod_ref, g_ref, wa_ref, wb_ref, wo_ref, xn_ref, h_ref, a_ref, b_ref, o_ref, acc):
        j = pl.program_id(1)

        @pl.when(j == 0)
        def _():
            hh = _norm_mod(x_ref[...], g_ref[...], mod_ref[0:1, :], mod_ref[1:2, :])
            h_ref[...] = hh.astype(BF16)
            acc[...] = jnp.zeros_like(acc)

        h = h_ref[...]
        a = jnp.dot(h, wa_ref[...], preferred_element_type=F32)
        b = jnp.dot(h, wb_ref[...], preferred_element_type=F32)
        a_ref[...] = a.astype(BF16)
        b_ref[...] = b.astype(BF16)
        u = (a * jax.nn.sigmoid(a)) * b
        acc[...] += jnp.dot(u.astype(BF16), wo_ref[...], preferred_element_type=F32)

        @pl.when(j == nf - 1)
        def _():
            o = acc[...]
            o_ref[...] = o.astype(BF16)
            xn_ref[...] = x_ref[...] + 0.5 * mod_ref[2:3, :] * o

    row = pl.BlockSpec((tm, d), lambda i, j: (i, 0))
    chunk = pl.BlockSpec((tm, tf), lambda i, j: (i, j))
    return pl.pallas_call(
        body, name="ffn_fwd", grid=(nt, nf),
        in_specs=[row, _full((3, d)), _full((1, d)),
                  pl.BlockSpec((None, d, tf), lambda i, j: (0, 0, j)),
                  pl.BlockSpec((None, d, tf), lambda i, j: (1, 0, j)),
                  pl.BlockSpec((tf, d), lambda i, j: (j, 0))],
        out_specs=[row, row, chunk, chunk, row],
        out_shape=[jax.ShapeDtypeStruct((s, d), F32), jax.ShapeDtypeStruct((s, d), BF16),
                   jax.ShapeDtypeStruct((s, f), BF16), jax.ShapeDtypeStruct((s, f), BF16),
                   jax.ShapeDtypeStruct((s, d), BF16)],
        scratch_shapes=[pltpu.VMEM((tm, d), F32)],
        compiler_params=_cp(dimension_semantics=("parallel", "arbitrary")),
    )(x, mod, g, wi, wi, wo)


def _ffn_bwd_main(dxo, o, gate, h, a, b, wo):
    s, d = dxo.shape
    f = wo.shape[0]
    tf = FFN_TF
    nf = f // tf

    def body(dxo_ref, o_ref, gate_ref, h_ref, a_ref, b_ref, wo_ref, dza_ref, dzb_ref, dwi_ref, dwo_ref, dg_ref, do_s):
        @pl.when(pl.program_id(0) == 0)
        def _():
            dxv = dxo_ref[...]
            do_s[...] = (0.5 * gate_ref[...] * dxv).astype(BF16)
            dg_ref[...] = 0.5 * jnp.sum(o_ref[...].astype(F32) * dxv, axis=0, keepdims=True)

        dov = do_s[...]
        hv = h_ref[...]
        du = lax.dot_general(dov, wo_ref[...], (((1,), (1,)), ((), ())), preferred_element_type=F32)
        av = a_ref[...].astype(F32)
        bv = b_ref[...].astype(F32)
        sa = jax.nn.sigmoid(av)
        si = av * sa
        u = (si * bv).astype(BF16)
        da = (du * bv * (sa * (1.0 + av * (1.0 - sa)))).astype(BF16)
        db = (du * si).astype(BF16)
        dza_ref[...] = da
        dzb_ref[...] = db
        dwo_ref[...] = _dot_tn(u, dov).astype(BF16)
        dwi_ref[0] = _dot_tn(hv, da).astype(BF16)
        dwi_ref[1] = _dot_tn(hv, db).astype(BF16)

    chunk = pl.BlockSpec((s, tf), lambda j: (0, j))
    once = lambda: pl.BlockSpec((s, d), lambda j: (0, 0), pipeline_mode=pl.Buffered(1))
    return pl.pallas_call(
        body, name="ffn_bwd_main", grid=(nf,),
        in_specs=[once(), once(), _full((1, d)), once(), chunk, chunk, pl.BlockSpec((tf, d), lambda j: (j, 0))],
        out_specs=[chunk, chunk, pl.BlockSpec((2, d, tf), lambda j: (0, 0, j)),
                   pl.BlockSpec((tf, d), lambda j: (j, 0)), _full((1, d))],
        out_shape=[jax.ShapeDtypeStruct((s, f), BF16), jax.ShapeDtypeStruct((s, f), BF16),
                   jax.ShapeDtypeStruct((2, d, f), BF16), jax.ShapeDtypeStruct((f, d), BF16),
                   jax.ShapeDtypeStruct((1, d), F32)],
        scratch_shapes=[pltpu.VMEM((s, d), BF16)],
        compiler_params=_cp(dimension_semantics=("arbitrary",)),
    )(dxo, o, gate, h, a, b, wo)


def _ffn_bwd_in(dza, dzb, wi, x, dxo, mod, g):
    s, d = x.shape
    f = dza.shape[1]
    tf, tm = FFN_TF_WIDE, min(s, FFN_TM_WIDE)
    nf, nt = f // tf, s // tm

    def body(dza_ref, dzb_ref, wa_ref, wb_ref, x_ref, dxo_ref, mod_ref, g_ref, dx_ref, rows_ref, acc):
        i, j = pl.program_id(0), pl.program_id(1)

        @pl.when(jnp.logical_and(i == 0, j == 0))
        def _():
            rows_ref[...] = jnp.zeros_like(rows_ref)

        @pl.when(j == 0)
        def _():
            acc[...] = jnp.zeros_like(acc)

        acc[...] += (lax.dot_general(dza_ref[...], wa_ref[...], (((1,), (1,)), ((), ())), preferred_element_type=F32)
                     + lax.dot_general(dzb_ref[...], wb_ref[...], (((1,), (1,)), ((), ())), preferred_element_type=F32))

        @pl.when(j == nf - 1)
        def _():
            dx, dsh, dsc, dg = _norm_mod_bwd(x_ref[...], g_ref[...], mod_ref[1:2, :], acc[...])
            dx_ref[...] = dx + dxo_ref[...]
            rows_ref[0:1, :] += dsh
            rows_ref[1:2, :] += dsc
            rows_ref[2:3, :] += dg

    row = pl.BlockSpec((tm, d), lambda i, j: (i, 0))
    chunk = pl.BlockSpec((tm, tf), lambda i, j: (i, j))
    return pl.pallas_call(
        body, name="ffn_bwd_in", grid=(nt, nf),
        in_specs=[chunk, chunk,
                  pl.BlockSpec((None, d, tf), lambda i, j: (0, 0, j)),
                  pl.BlockSpec((None, d, tf), lambda i, j: (1, 0, j)),
                  row, row, _full((3, d)), _full((1, d))],
        out_specs=[row, _full((8, d))],
        out_shape=[jax.ShapeDtypeStruct((s, d), F32), jax.ShapeDtypeStruct((8, d), F32)],
        scratch_shapes=[pltpu.VMEM((tm, d), F32)],
        compiler_params=_cp(dimension_semantics=("arbitrary", "arbitrary")),
    )(dza, dzb, wi, wi, x, dxo, mod, g)


def _mix_in_fwd(x, mod, g, wmit):
    s, d = x.shape
    p = wmit.shape[0]
    tm = _tm(s)

    def body(x_ref, mod_ref, g_ref, w_ref, z_ref, h_ref):
        hh = _norm_mod(x_ref[...], g_ref[...], mod_ref[0:1, :], mod_ref[1:2, :]).astype(BF16)
        h_ref[...] = hh
        z_ref[...] = lax.dot_general(hh, w_ref[...], (((1,), (1,)), ((), ())), preferred_element_type=F32)

    row = pl.BlockSpec((tm, d), lambda i: (i, 0))
    return pl.pallas_call(
        body, name="mix_in_fwd", grid=(s // tm,),
        in_specs=[row, _full((3, d)), _full((1, d)), _full((p, d))],
        out_specs=[pl.BlockSpec((tm, p), lambda i: (i, 0)), row],
        out_shape=[jax.ShapeDtypeStruct((s, p), F32), jax.ShapeDtypeStruct((s, d), BF16)],
        compiler_params=_cp(dimension_semantics=("parallel",)),
    )(x, mod, g, wmit)


def _mix_in_bwd(dzs, h, wmit, x, dxo, mod, g):
    s, d = x.shape
    p = wmit.shape[0]
    tm = min(s, 512)
    nt = s // tm

    def body(za_ref, zb_ref, zc_ref, zd_ref, h_ref, w_ref, x_ref, dxo_ref, mod_ref, g_ref,
             dx_ref, rows_ref, dw_ref, acc):
        i = pl.program_id(0)

        @pl.when(i == 0)
        def _():
            rows_ref[...] = jnp.zeros_like(rows_ref)
            acc[...] = jnp.zeros_like(acc)

        dz = jnp.concatenate([za_ref[...], zb_ref[...], zc_ref[...], zd_ref[...]], axis=1).astype(BF16)
        acc[...] += _dot_tn(dz, h_ref[...])
        dh = jnp.dot(dz, w_ref[...], preferred_element_type=F32)
        dx, dsh, dsc, dg = _norm_mod_bwd(x_ref[...], g_ref[...], mod_ref[1:2, :], dh)
        dx_ref[...] = dx + dxo_ref[...]
        rows_ref[0:1, :] += dsh
        rows_ref[1:2, :] += dsc
        rows_ref[2:3, :] += dg

        @pl.when(i == nt - 1)
        def _():
            dw_ref[...] = acc[...].astype(BF16)

    row = pl.BlockSpec((tm, d), lambda i: (i, 0))
    zspecs = [pl.BlockSpec((tm, z.shape[1]), lambda i: (i, 0)) for z in dzs]
    return pl.pallas_call(
        body, name="mix_in_bwd", grid=(nt,),
        in_specs=zspecs + [row, _full((p, d)), row, row, _full((3, d)), _full((1, d))],
        out_specs=[row, _full((8, d)), _full((p, d))],
        out_shape=[jax.ShapeDtypeStruct((s, d), F32), jax.ShapeDtypeStruct((8, d), F32),
                   jax.ShapeDtypeStruct((p, d), BF16)],
        scratch_shapes=[pltpu.VMEM((p, d), F32)],
        compiler_params=_cp(dimension_semantics=("arbitrary",)),
    )(*dzs, h, wmit, x, dxo, mod, g)


def _group_norm(ys, mng):
    outs, hats, rs = [], [], []
    for k, y in enumerate(ys):
        r = lax.rsqrt(jnp.mean(y * y, axis=-1, keepdims=True) + EPS)
        yh = y * r
        hats.append(yh)
        rs.append(r)
        outs.append(yh * mng[:, k * W_GRP:(k + 1) * W_GRP])
    return jnp.concatenate(outs, axis=1), hats, rs


def _mix_out_fwd(ys, mng, wmo, x, gate):
    s, d = x.shape
    tm = _tm(s)

    def body(ya, yb, yc, yd, mng_ref, w_ref, x_ref, gate_ref, xn_ref, m_ref):
        yn, _, _ = _group_norm([ya[...], yb[...], yc[...], yd[...]], mng_ref[...])
        m = jnp.dot(yn.astype(BF16), w_ref[...], preferred_element_type=F32)
        m_ref[...] = m
        xn_ref[...] = x_ref[...] + gate_ref[...] * m

    row = pl.BlockSpec((tm, d), lambda i: (i, 0))
    grp = pl.BlockSpec((tm, W_GRP), lambda i: (i, 0))
    return pl.pallas_call(
        body, name="mix_out_fwd", grid=(s // tm,),
        in_specs=[grp, grp, grp, grp, _full((1, d)), _full((d, d)), row, _full((1, d))],
        out_specs=[row, row],
        out_shape=[jax.ShapeDtypeStruct((s, d), F32), jax.ShapeDtypeStruct((s, d), F32)],
        compiler_params=_cp(dimension_semantics=("parallel",)),
    )(*ys, mng, wmo, x, gate)


def _mix_out_bwd(dxo, m, gate, ys, mng, wmo):
    s, d = dxo.shape
    tm = min(s, 512)
    nt = s // tm

    def body(dxo_ref, m_ref, gate_ref, ya, yb, yc, yd, mng_ref, w_ref,
             dya, dyb, dyc, dyd, dgate_ref, dmng_ref, dw_ref, acc):
        i = pl.program_id(0)

        @pl.when(i == 0)
        def _():
            dgate_ref[...] = jnp.zeros_like(dgate_ref)
            dmng_ref[...] = jnp.zeros_like(dmng_ref)
            acc[...] = jnp.zeros_like(acc)

        dxv = dxo_ref[...]
        dgate_ref[...] += jnp.sum(m_ref[...] * dxv, axis=0, keepdims=True)
        dm = (gate_ref[...] * dxv).astype(BF16)
        mng = mng_ref[...]
        yn, hats, rs = _group_norm([ya[...], yb[...], yc[...], yd[...]], mng)
        acc[...] += _dot_tn(yn, dm)
        dyn = lax.dot_general(dm, w_ref[...], (((1,), (1,)), ((), ())), preferred_element_type=F32)
        dmng_parts = []
        for k, (yh, r, out) in enumerate(zip(hats, rs, (dya, dyb, dyc, dyd))):
            dk = dyn[:, k * W_GRP:(k + 1) * W_GRP]
            dmng_parts.append(jnp.sum(dk * yh, axis=0, keepdims=True))
            dyh = dk * mng[:, k * W_GRP:(k + 1) * W_GRP]
            out[...] = r * (dyh - yh * jnp.mean(dyh * yh, axis=-1, keepdims=True))
        dmng_ref[...] += jnp.concatenate(dmng_parts, axis=1)

        @pl.when(i == nt - 1)
        def _():
            dw_ref[...] = acc[...].astype(BF16)

    row = pl.BlockSpec((tm, d), lambda i: (i, 0))
    grp = pl.BlockSpec((tm, W_GRP), lambda i: (i, 0))
    return pl.pallas_call(
        body, name="mix_out_bwd", grid=(nt,),
        in_specs=[row, row, _full((1, d)), grp, grp, grp, grp, _full((1, d)), _full((d, d))],
        out_specs=[grp, grp, grp, grp, _full((1, d)), _full((1, d)), _full((d, d))],
        out_shape=[jax.ShapeDtypeStruct((s, W_GRP), F32)] * 4
        + [jax.ShapeDtypeStruct((1, d), F32), jax.ShapeDtypeStruct((1, d), F32), jax.ShapeDtypeStruct((d, d), BF16)],
        scratch_shapes=[pltpu.VMEM((d, d), F32)],
        compiler_params=_cp(dimension_semantics=("arbitrary",)),
    )(dxo, m, gate, *ys, mng, wmo)


def _sgu_consts():
    r = _iota((W_GRP, W_GRP), 0) >> 6
    c = _iota((W_GRP, W_GRP), 1) >> 6
    avg = jnp.where(r == c, 1.0 / 64.0, 0.0).astype(F32)
    tril = _iota((CHUNK, CHUNK), 0) >= _iota((CHUNK, CHUNK), 1)
    head = _iota((CHUNK, W_GRP), 1) >> 6
    return avg, tril, head


def _sgu_pre(za, avg):
    zg, t = _gelu(za)
    u, v = zg[:, :W_GRP], zg[:, W_GRP:]
    mu = _dot_hi(v, avg)
    vc = v - mu
    r = lax.rsqrt(_dot_hi(vc * vc, avg) + EPS)
    return t, u, vc * r, r


def _sgu_fwd(z, sgu_w, bias_full):
    s = z.shape[0]
    tm = min(s, 512)

    def body(za_ref, w_ref, bias_ref, ya_ref):
        avg, tril, head = _sgu_consts()
        _, u, vn, _ = _sgu_pre(za_ref[...], avg)
        wm = [jnp.where(tril, w_ref[h], 0.0).astype(BF16) for h in range(4)]
        vb = vn.astype(BF16)
        for n in range(tm // CHUNK):
            rows = slice(n * CHUNK, (n + 1) * CHUNK)
            mixed = bias_ref[...]
            for h in range(4):
                mixed = mixed + jnp.where(head == h, jnp.dot(wm[h], vb[rows], preferred_element_type=F32), 0.0)
            ya_ref[rows, :] = u[rows] * mixed

    return pl.pallas_call(
        body, name="sgu_fwd", grid=(s // tm,),
        in_specs=[pl.BlockSpec((tm, 2 * W_GRP), lambda i: (i, 0)), _full((4, CHUNK, CHUNK)), _full((CHUNK, W_GRP))],
        out_specs=pl.BlockSpec((tm, W_GRP), lambda i: (i, 0)),
        out_shape=jax.ShapeDtypeStruct((s, W_GRP), F32),
        compiler_params=_cp(dimension_semantics=("parallel",)),
    )(z, sgu_w, bias_full)


def _sgu_bwd(z, dya, sgu_w, bias_full):
    s = z.shape[0]
    tm = min(s, 512)
    nt = s // tm

    def body(za_ref, dya_ref, w_ref, bias_ref, dza_ref, dw_ref, db_ref, du_s, dvn_s):
        i = pl.program_id(0)

        @pl.when(i == 0)
        def _():
            dw_ref[...] = jnp.zeros_like(dw_ref)
            db_ref[...] = jnp.zeros_like(db_ref)

        avg, tril, head = _sgu_consts()
        za = za_ref[...]
        t, u, vn, r = _sgu_pre(za, avg)
        wm = [jnp.where(tril, w_ref[h], 0.0).astype(BF16) for h in range(4)]
        vb = vn.astype(BF16)
        dya = dya_ref[...]
        dw = [jnp.zeros((CHUNK, CHUNK), F32) for _ in range(4)]
        db = jnp.zeros((CHUNK, W_GRP), F32)
        for n in range(tm // CHUNK):
            rows = slice(n * CHUNK, (n + 1) * CHUNK)
            mixed = bias_ref[...]
            for h in range(4):
                mixed = mixed + jnp.where(head == h, jnp.dot(wm[h], vb[rows], preferred_element_type=F32), 0.0)
            dmix = dya[rows] * u[rows]
            du_s[rows, :] = dya[rows] * mixed
            db = db + dmix
            dmb = dmix.astype(BF16)
            dvn = jnp.zeros((CHUNK, W_GRP), F32)
            for h in range(4):
                dmh = jnp.where(head == h, dmix, 0.0)
                dw[h] = dw[h] + _dot_nt(dmh, vb[rows])
                dvn = dvn + jnp.where(head == h, _dot_tn(wm[h], dmb), 0.0)
            dvn_s[rows, :] = dvn
        for h in range(4):
            dw_ref[h] += jnp.where(tril, dw[h], 0.0)
        sel = ((_iota((W_GRP, CHUNK), 0) >> 6) == _iota((W_GRP, CHUNK), 1)).astype(F32)
        db_ref[...] += _dot_hi(db, sel)
        dvn = dvn_s[...]
        dv = r * (dvn - _dot_hi(dvn, avg) - vn * _dot_hi(dvn * vn, avg))
        dzg = jnp.concatenate([du_s[...], dv], axis=1)
        dza_ref[...] = dzg * _gelu_grad(za, t)

    return pl.pallas_call(
        body, name="sgu_bwd", grid=(nt,),
        in_specs=[pl.BlockSpec((tm, 2 * W_GRP), lambda i: (i, 0)), pl.BlockSpec((tm, W_GRP), lambda i: (i, 0)),
                  _full((4, CHUNK, CHUNK)), _full((CHUNK, W_GRP))],
        out_specs=[pl.BlockSpec((tm, 2 * W_GRP), lambda i: (i, 0)), _full((4, CHUNK, CHUNK)), _full((CHUNK, CHUNK))],
        out_shape=[jax.ShapeDtypeStruct((s, 2 * W_GRP), F32), jax.ShapeDtypeStruct((4, CHUNK, CHUNK), F32),
                   jax.ShapeDtypeStruct((CHUNK, CHUNK), F32)],
        scratch_shapes=[pltpu.VMEM((tm, W_GRP), F32), pltpu.VMEM((tm, W_GRP), F32)],
        compiler_params=_cp(dimension_semantics=("arbitrary",)),
    )(z, dya, sgu_w, bias_full)


def _shift_down(x, k):
    return jnp.where(_iota(x.shape, 0) < k, 0.0, pltpu.roll(x, k, 0))


def _shift_up(x, k):
    n = x.shape[0]
    return jnp.where(_iota(x.shape, 0) >= n - k, 0.0, pltpu.roll(x, n - k, 0))


def _by_pool_group(shape, v2, v4, v8, v16):
    col = _iota(shape, 1)
    return jnp.where(col < 64, v2, jnp.where(col < 128, v4, jnp.where(col < 192, v8, v16)))


def _pool_core(zb, pw2):
    s2 = zb + _shift_down(zb, 1)
    s4 = s2 + _shift_down(s2, 2)
    s8 = s4 + _shift_down(s4, 4)
    s16 = s8 + _shift_down(s8, 8)
    win = _by_pool_group(zb.shape, s2, s4, s8, s16)
    wlen = _by_pool_group(zb.shape, 2.0, 4.0, 8.0, 16.0)
    cnt = jnp.minimum((_iota(zb.shape, 0) + 1).astype(F32), wlen)
    p = win / cnt - zb
    wt = jnp.tile(pw2, (1, 4))
    wbd = jnp.where((_iota(wt.shape, 0) >> 6) == (_iota(wt.shape, 1) >> 6), wt, 0.0).astype(BF16)
    return p, cnt, wbd


def _conv_core(zc, cw):
    bg, cg, xh = zc[:, :W_GRP], zc[:, W_GRP:2 * W_GRP], zc[:, 2 * W_GRP:]
    y = cg * xh
    y1, y2 = _shift_down(y, 1), _shift_down(y, 2)
    out = cw[2:3, :] * y + cw[1:2, :] * y1 + cw[0:1, :] * y2
    return bg, cg, xh, y, y1, y2, out


def _poolconv_fwd(z, pw2, pscale, cw):
    s = z.shape[0]

    def body(zb_ref, zc_ref, pw_ref, ps_ref, cw_ref, yb_ref, yc_ref):
        p, _, wbd = _pool_core(zb_ref[...], pw_ref[...])
        yb_ref[...] = jnp.dot(p.astype(BF16), wbd, preferred_element_type=F32) * ps_ref[...]
        bg, _, _, _, _, _, out = _conv_core(zc_ref[...], cw_ref[...])
        yc_ref[...] = bg * out

    return pl.pallas_call(
        body, name="poolconv_fwd", grid=(1,),
        in_specs=[pl.BlockSpec((s, W_GRP), lambda i: (0, 2)), pl.BlockSpec((s, 3 * W_GRP), lambda i: (0, 1)),
                  _full((W_GRP, 64)), _full((1, W_GRP)), _full((3, W_GRP))],
        out_specs=[_full((s, W_GRP)), _full((s, W_GRP))],
        out_shape=[jax.ShapeDtypeStruct((s, W_GRP), F32)] * 2,
        compiler_params=_cp(dimension_semantics=("arbitrary",)),
    )(z, z, pw2, pscale, cw)


def _poolconv_bwd(z, dyb, dyc, pw2, pscale, cw):
    s = z.shape[0]

    def body(zb_ref, zc_ref, dyb_ref, dyc_ref, pw_ref, ps_ref, cw_ref, dzb_ref, dzc_ref, dw_ref, dps_ref, dcw_ref):
        zb = zb_ref[...]
        p, cnt, wbd = _pool_core(zb, pw_ref[...])
        pb = p.astype(BF16)
        out = jnp.dot(pb, wbd, preferred_element_type=F32)
        dyb = dyb_ref[...]
        dps_ref[...] = jnp.sum(dyb * out, axis=0, keepdims=True)
        dout = (dyb * ps_ref[...]).astype(BF16)
        dw = _dot_tn(pb, dout)
        dw_ref[...] = jnp.where((_iota(dw.shape, 0) >> 6) == (_iota(dw.shape, 1) >> 6), dw, 0.0)
        dp = lax.dot_general(dout, wbd, (((1,), (1,)), ((), ())), preferred_element_type=F32)
        dwin = dp / cnt
        t2 = dwin + _shift_up(dwin, 1)
        t4 = t2 + _shift_up(t2, 2)
        t8 = t4 + _shift_up(t4, 4)
        t16 = t8 + _shift_up(t8, 8)
        dzb_ref[...] = _by_pool_group(zb.shape, t2, t4, t8, t16) - dp

        cw = cw_ref[...]
        bg, cg, xh, y, y1, y2, out = _conv_core(zc_ref[...], cw)
        dyc = dyc_ref[...]
        dout = dyc * bg
        dcw_ref[...] = jnp.zeros_like(dcw_ref)
        dcw_ref[0:1, :] = jnp.sum(dout * y2, axis=0, keepdims=True)
        dcw_ref[1:2, :] = jnp.sum(dout * y1, axis=0, keepdims=True)
        dcw_ref[2:3, :] = jnp.sum(dout * y, axis=0, keepdims=True)
        dy = cw[2:3, :] * dout + cw[1:2, :] * _shift_up(dout, 1) + cw[0:1, :] * _shift_up(dout, 2)
        dzc_ref[...] = jnp.concatenate([dyc * out, dy * xh, dy * cg], axis=1)

    return pl.pallas_call(
        body, name="poolconv_bwd", grid=(1,),
        in_specs=[pl.BlockSpec((s, W_GRP), lambda i: (0, 2)), pl.BlockSpec((s, 3 * W_GRP), lambda i: (0, 1)),
                  _full((s, W_GRP)), _full((s, W_GRP)), _full((W_GRP, 64)), _full((1, W_GRP)), _full((3, W_GRP))],
        out_specs=[_full((s, W_GRP)), _full((s, 3 * W_GRP)), _full((W_GRP, W_GRP)), _full((1, W_GRP)), _full((8, W_GRP))],
        out_shape=[jax.ShapeDtypeStruct((s, W_GRP), F32), jax.ShapeDtypeStruct((s, 3 * W_GRP), F32),
                   jax.ShapeDtypeStruct((W_GRP, W_GRP), F32), jax.ShapeDtypeStruct((1, W_GRP), F32),
                   jax.ShapeDtypeStruct((8, W_GRP), F32)],
        compiler_params=_cp(dimension_semantics=("arbitrary",)),
    )(z, z, dyb, dyc, pw2, pscale, cw)


N_STATE = 1024
HALF_STATE = N_STATE // 2
HALF_CH = W_GRP // 2
N_SLAB = HALF_STATE // LANES


def _s5_disc(lre, lim, ldt):
    dt = jnp.exp(ldt)
    mag = jnp.exp(lre * dt)
    ang = lim * dt
    ar, ai = mag * jnp.cos(ang), mag * jnp.sin(ang)
    nr, ni = ar - 1.0, ai
    den = lre * lre + lim * lim
    kr = (nr * lre + ni * lim) / den
    ki = (ni * lre - nr * lim) / den
    return ar, ai, kr, ki


def _s5_mats(colp, br, bi, cr, ci):
    _, _, kr, ki = _s5_disc(colp[:, 0:1], colp[:, 1:2], colp[:, 2:3])
    bbr = kr * br - ki * bi
    bbi = kr * bi + ki * br
    bmask = (_iota((HALF_STATE, HALF_CH), 0) >> 6) == (_iota((HALF_STATE, HALF_CH), 1) >> 4)
    cmask = (_iota((HALF_CH, HALF_STATE), 0) >> 4) == (_iota((HALF_CH, HALF_STATE), 1) >> 6)
    btr = jnp.where(bmask, jnp.tile(bbr, (1, 8)), 0.0).astype(BF16)
    bti = jnp.where(bmask, jnp.tile(bbi, (1, 8)), 0.0).astype(BF16)
    ctr = jnp.where(cmask, jnp.tile(cr, (1, 8)), 0.0).astype(BF16)
    cti = jnp.where(cmask, jnp.tile(ci, (1, 8)), 0.0).astype(BF16)
    return kr, ki, btr, bti, ctr, cti, bmask, cmask


def _slab(q):
    return slice(q * LANES, (q + 1) * LANES)


def _cmul(ar, ai, br, bi):
    return ar * br - ai * bi, ar * bi + ai * br


def _sub_shift(x, k, up):
    row = _iota(x.shape, 0)
    if up:
        return jnp.where(row >= N_SEG - k, 0.0, pltpu.roll(x, N_SEG - k, 0))
    return jnp.where(row < k, 0.0, pltpu.roll(x, k, 0))


def _seg_rows(j):
    return pl.ds(pl.multiple_of(j * N_SEG, N_SEG), N_SEG)


def _interleave(src, dst, seg):
    def step(j, carry):
        dst[_seg_rows(j), :] = src[pl.ds(j, N_SEG, stride=seg), :]
        return carry
    lax.fori_loop(0, seg, step, 0)


def _deinterleave(src, dst, seg):
    def step(j, carry):
        dst[pl.ds(j, N_SEG, stride=seg), :] = src[_seg_rows(j), :]
        return carry
    lax.fori_loop(0, seg, step, 0)


def _scan(xr, xi, ar_row, ai_row, seg, reverse, states=None):
    nlog = int(math.log2(seg))
    assert (1 << nlog) == seg
    grads = []
    for q0 in range(0, N_SLAB, 4):
        qs = list(range(q0, q0 + 4))
        aq = [(jnp.broadcast_to(ar_row[:, _slab(q)], (N_SEG, LANES)),
               jnp.broadcast_to(ai_row[:, _slab(q)], (N_SEG, LANES))) for q in qs]
        zero = jnp.zeros((N_SEG, LANES), F32)

        def local(jj, carry, qs=qs, aq=aq):
            j = seg - 1 - jj if reverse else jj
            out = []
            for n, q in enumerate(qs):
                rows = _seg_rows(j)
                pr, pi = _cmul(aq[n][0], aq[n][1], carry[2 * n], carry[2 * n + 1])
                nr = pr + xr[q, rows, :]
                ni = pi + xi[q, rows, :]
                xr[q, rows, :] = nr
                xi[q, rows, :] = ni
                out += [nr, ni]
            return tuple(out)

        fin = lax.fori_loop(0, seg, local, (zero,) * 8)
        cins = []
        for n in range(4):
            er, ei = fin[2 * n], fin[2 * n + 1]
            pr, pi = aq[n]
            for _ in range(nlog):
                pr, pi = _cmul(pr, pi, pr, pi)
            yr, yi = er, ei
            for k in (1, 2, 4):
                sr, si = _cmul(pr, pi, _sub_shift(yr, k, reverse), _sub_shift(yi, k, reverse))
                yr, yi = yr + sr, yi + si
                pr, pi = _cmul(pr, pi, pr, pi)
            cins.append((_sub_shift(yr, 1, reverse), _sub_shift(yi, 1, reverse)))

        def fix(jj, carry, qs=qs, aq=aq, cins=cins):
            j = seg - 1 - jj if reverse else jj
            out, sums = [], []
            for n, q in enumerate(qs):
                rows = _seg_rows(j)
                pwr, pwi = carry[2 * n], carry[2 * n + 1]
                cr, ci = _cmul(pwr, pwi, cins[n][0], cins[n][1])
                v_r, v_i = xr[q, rows, :] + cr, xi[q, rows, :] + ci
                xr[q, rows, :] = v_r
                xi[q, rows, :] = v_i
                nr, ni = _cmul(pwr, pwi, aq[n][0], aq[n][1])
                out += [nr, ni]
                if states is not None:
                    prev = _seg_rows(j - 1)
                    p_r, p_i = states[0][q, prev, :], states[1][q, prev, :]
                    sums += [carry[8 + 2 * n] + v_r * p_r + v_i * p_i, carry[9 + 2 * n] - v_r * p_i + v_i * p_r]
            return tuple(out + sums)

        powers = tuple(v for pair in aq for v in pair)
        if states is None:
            lax.fori_loop(0, seg, fix, powers)
            continue
        assert reverse
        fix_last = lax.fori_loop(0, seg - 1, fix, powers + (zero,) * 8)
        first = _seg_rows(0)
        for n, q in enumerate(qs):
            cr, ci = _cmul(fix_last[2 * n], fix_last[2 * n + 1], cins[n][0], cins[n][1])
            v_r, v_i = xr[q, first, :] + cr, xi[q, first, :] + ci
            xr[q, first, :] = v_r
            xi[q, first, :] = v_i
            p_r = _sub_shift(states[0][q, _seg_rows(seg - 1), :], 1, False)
            p_i = _sub_shift(states[1][q, _seg_rows(seg - 1), :], 1, False)
            grads.append((jnp.sum(fix_last[8 + 2 * n] + v_r * p_r + v_i * p_i, axis=0, keepdims=True),
                          jnp.sum(fix_last[9 + 2 * n] - v_r * p_i + v_i * p_r, axis=0, keepdims=True)))
    return grads


def _s5_forward_states(u, btr, bti, ar_row, ai_row, xr, xi, seg):
    ub = u.astype(BF16)
    for q in range(N_SLAB):
        xr[q] = _dot_nt(ub, btr[_slab(q), :])
        xi[q] = _dot_nt(ub, bti[_slab(q), :])
    _scan(xr, xi, ar_row, ai_row, seg, False)


def _s5_readout(u, xr, xi, ctr, cti, d):
    y = d * u
    for q in range(N_SLAB):
        y = y + _dot_nt(xr[q], ctr[:, _slab(q)]) - _dot_nt(xi[q], cti[:, _slab(q)])
    return y


def _s5_param_specs():
    return [pl.BlockSpec((3, HALF_STATE), lambda i: (0, i)), pl.BlockSpec((HALF_STATE, 3), lambda i: (i, 0)),
            pl.BlockSpec((HALF_STATE, 16), lambda i: (i, 0)), pl.BlockSpec((HALF_STATE, 16), lambda i: (i, 0)),
            pl.BlockSpec((HALF_CH, 64), lambda i: (i, 0)), pl.BlockSpec((HALF_CH, 64), lambda i: (i, 0)),
            pl.BlockSpec((1, HALF_CH), lambda i: (0, i))]


def _s5_core_fwd(z, sp):
    s = z.shape[0]
    seg = s // N_SEG

    def body(u_ref, rowp, colp, br, bi, cr, ci, d_ref, y_ref, xr, xi, us, ys):
        ar, ai, _, _ = _s5_disc(rowp[0:1, :], rowp[1:2, :], rowp[2:3, :])
        _, _, btr, bti, ctr, cti, _, _ = _s5_mats(colp[...], br[...], bi[...], cr[...], ci[...])
        _interleave(u_ref, us, seg)
        u = us[...]
        _s5_forward_states(u, btr, bti, ar, ai, xr, xi, seg)
        ys[...] = _s5_readout(u, xr, xi, ctr, cti, d_ref[...])
        _deinterleave(ys, y_ref, seg)

    return pl.pallas_call(
        body, name="s5_core_fwd", grid=(2,),
        in_specs=[pl.BlockSpec((s, HALF_CH), lambda i: (0, 12 + i))] + _s5_param_specs(),
        out_specs=pl.BlockSpec((s, HALF_CH), lambda i: (0, i)),
        out_shape=jax.ShapeDtypeStruct((s, W_GRP), F32),
        scratch_shapes=[pltpu.VMEM((N_SLAB, s, LANES), F32)] * 2 + [pltpu.VMEM((s, HALF_CH), F32)] * 2,
        compiler_params=_cp(dimension_semantics=("parallel",)),
    )(z, *sp)


def _s5_glu_fwd(y, gw, gb):
    s = y.shape[0]
    tm = _tm(s)

    def body(y_ref, gw_ref, gb_ref, o_ref):
        yg, _ = _gelu(y_ref[...])
        o_ref[...] = yg * jax.nn.sigmoid(_dot(yg, gw_ref[...]) + gb_ref[...])

    blk = pl.BlockSpec((tm, W_GRP), lambda i: (i, 0))
    return pl.pallas_call(
        body, name="s5_glu_fwd", grid=(s // tm,),
        in_specs=[blk, _full((W_GRP, W_GRP)), _full((1, W_GRP))], out_specs=blk,
        out_shape=jax.ShapeDtypeStruct((s, W_GRP), F32),
        compiler_params=_cp(dimension_semantics=("parallel",)),
    )(y, gw, gb)


def _s5_glu_bwd(y, dyd, gw, gb):
    s = y.shape[0]
    tm = _tm(s)

    def body(y_ref, dyd_ref, gw_ref, gb_ref, dy_ref, dgw_ref, dgb_ref):
        i = pl.program_id(0)

        @pl.when(i == 0)
        def _():
            dgw_ref[...] = jnp.zeros_like(dgw_ref)
            dgb_ref[...] = jnp.zeros_like(dgb_ref)

        y, gw, dyd = y_ref[...], gw_ref[...], dyd_ref[...]
        yg, t = _gelu(y)
        gate = jax.nn.sigmoid(_dot(yg, gw) + gb_ref[...])
        dlin = dyd * yg * gate * (1.0 - gate)
        dgw_ref[...] += _dot_tn(yg, dlin)
        dgb_ref[...] += jnp.sum(dlin, axis=0, keepdims=True)
        dy_ref[...] = (dyd * gate + _dot_nt(dlin, gw)) * _gelu_grad(y, t)

    blk = pl.BlockSpec((tm, W_GRP), lambda i: (i, 0))
    return pl.pallas_call(
        body, name="s5_glu_bwd", grid=(s // tm,),
        in_specs=[blk, blk, _full((W_GRP, W_GRP)), _full((1, W_GRP))],
        out_specs=[blk, _full((W_GRP, W_GRP)), _full((1, W_GRP))],
        out_shape=[jax.ShapeDtypeStruct((s, W_GRP), F32), jax.ShapeDtypeStruct((W_GRP, W_GRP), F32),
                   jax.ShapeDtypeStruct((1, W_GRP), F32)],
        compiler_params=_cp(dimension_semantics=("arbitrary",)),
    )(y, dyd, gw, gb)


def _s5_core_bwd(z, dy, sp):
    s = z.shape[0]
    seg = s // N_SEG

    def body(u_ref, dy_ref, rowp, colp, br_ref, bi_ref, cr_ref, ci_ref, d_ref,
             du_ref, dbr_ref, dbi_ref, dcr_ref, dci_ref, dd_ref, da_ref, dk_ref,
             xr, xi, gr, gi, us, dys):
        ar, ai, _, _ = _s5_disc(rowp[0:1, :], rowp[1:2, :], rowp[2:3, :])
        br, bi = br_ref[...], bi_ref[...]
        kr, ki, btr, bti, ctr, cti, bmask, cmask = _s5_mats(colp[...], br, bi, cr_ref[...], ci_ref[...])
        _interleave(u_ref, us, seg)
        _interleave(dy_ref, dys, seg)
        u = us[...]
        d = d_ref[...]
        _s5_forward_states(u, btr, bti, ar, ai, xr, xi, seg)

        dy = dys[...]
        dd_ref[...] = jnp.sum(dy * u, axis=0, keepdims=True)
        du = d * dy
        dyb = dy.astype(BF16)
        dctr, dcti = [], []
        for q in range(N_SLAB):
            gr[q] = jnp.dot(dyb, ctr[:, _slab(q)], preferred_element_type=F32)
            gi[q] = -jnp.dot(dyb, cti[:, _slab(q)], preferred_element_type=F32)
            dctr.append(_dot_tn(dyb, xr[q]))
            dcti.append(-_dot_tn(dyb, xi[q]))
        selp = ((_iota((HALF_STATE, 64), 0) & 63) == _iota((HALF_STATE, 64), 1)).astype(F32)
        dcr_ref[...] = _dot_hi(jnp.where(cmask, jnp.concatenate(dctr, axis=1), 0.0), selp)
        dci_ref[...] = _dot_hi(jnp.where(cmask, jnp.concatenate(dcti, axis=1), 0.0), selp)

        da = _scan(gr, gi, ar, -ai, seg, True, states=(xr, xi))
        dar, dai = [p[0] for p in da], [p[1] for p in da]
        da_ref[...] = jnp.zeros_like(da_ref)
        da_ref[0:1, :] = jnp.concatenate(dar, axis=1)
        da_ref[1:2, :] = jnp.concatenate(dai, axis=1)

        ub = u.astype(BF16)
        dbtr, dbti = [], []
        for q in range(N_SLAB):
            g_r, g_i = gr[q].astype(BF16), gi[q].astype(BF16)
            du = du + jnp.dot(g_r, btr[_slab(q), :], preferred_element_type=F32) \
                + jnp.dot(g_i, bti[_slab(q), :], preferred_element_type=F32)
            dbtr.append(_dot_tn(g_r, ub))
            dbti.append(_dot_tn(g_i, ub))
        us[...] = du
        _deinterleave(us, du_ref, seg)
        selc =((_iota((HALF_CH, 16), 0) & 15) == _iota((HALF_CH, 16), 1)).astype(F32)
        dbbr = _dot_hi(jnp.where(bmask, jnp.concatenate(dbtr, axis=0), 0.0), selc)
        dbbi = _dot_hi(jnp.where(bmask, jnp.concatenate(dbti, axis=0), 0.0), selc)
        dbr_ref[...] = kr * dbbr + ki * dbbi
        dbi_ref[...] = kr * dbbi - ki * dbbr
        dk_ref[:, 0:1] = jnp.sum(dbbr * br + dbbi * bi, axis=1, keepdims=True)
        dk_ref[:, 1:2] = jnp.sum(dbbi * br - dbbr * bi, axis=1, keepdims=True)

    half = pl.BlockSpec((s, HALF_CH), lambda i: (0, i))
    return pl.pallas_call(
        body, name="s5_core_bwd", grid=(2,),
        in_specs=[pl.BlockSpec((s, HALF_CH), lambda i: (0, 12 + i)), half] + _s5_param_specs(),
        out_specs=[half, pl.BlockSpec((HALF_STATE, 16), lambda i: (i, 0)), pl.BlockSpec((HALF_STATE, 16), lambda i: (i, 0)),
                   pl.BlockSpec((HALF_CH, 64), lambda i: (i, 0)), pl.BlockSpec((HALF_CH, 64), lambda i: (i, 0)),
                   pl.BlockSpec((1, HALF_CH), lambda i: (0, i)), pl.BlockSpec((8, HALF_STATE), lambda i: (0, i)),
                   pl.BlockSpec((HALF_STATE, 2), lambda i: (i, 0))],
        out_shape=[jax.ShapeDtypeStruct((s, W_GRP), F32), jax.ShapeDtypeStruct((N_STATE, 16), F32),
                   jax.ShapeDtypeStruct((N_STATE, 16), F32), jax.ShapeDtypeStruct((W_GRP, 64), F32),
                   jax.ShapeDtypeStruct((W_GRP, 64), F32), jax.ShapeDtypeStruct((1, W_GRP), F32),
                   jax.ShapeDtypeStruct((8, N_STATE), F32), jax.ShapeDtypeStruct((N_STATE, 2), F32)],
        scratch_shapes=[pltpu.VMEM((N_SLAB, s, LANES), F32)] * 4 + [pltpu.VMEM((s, HALF_CH), F32)] * 2,
        compiler_params=_cp(dimension_semantics=("parallel",)),
    )(z, dy, *sp)


def _s5_param_bwd(lre, lim, ldt, da_r, da_i, dk_r, dk_i):
    n = lre.shape[0]

    def body(lre_ref, lim_ref, ldt_ref, dar_ref, dai_ref, dkr_ref, dki_ref, o_re, o_im, o_dt):
        lre, lim, ldt = lre_ref[...], lim_ref[...], ldt_ref[...]
        dt = jnp.exp(ldt)
        ar, ai, kr, ki = _s5_disc(lre, lim, ldt)
        mag = jnp.exp(lre * dt)
        den = lre * lre + lim * lim
        dkr, dki = dkr_ref[...], dki_ref[...]
        nr, ni = ar - 1.0, ai
        d_ar = dar_ref[...] + (dkr * lre - dki * lim) / den
        d_ai = dai_ref[...] + (dkr * lim + dki * lre) / den
        kk = (kr * dkr + ki * dki) * 2.0 / den
        d_lre = (dkr * nr + dki * ni) / den - kk * lre
        d_lim = (dkr * ni - dki * nr) / den - kk * lim
        d_mag = (d_ar * ar + d_ai * ai) / mag
        d_ang = d_ai * ar - d_ar * ai
        o_re[...] = d_lre + d_mag * mag * dt
        o_im[...] = d_lim + d_ang * dt
        o_dt[...] = jnp.sum((d_mag * mag * lre + d_ang * lim) * dt, axis=1, keepdims=True)

    return pl.pallas_call(
        body, name="s5_param_bwd",
        out_shape=[jax.ShapeDtypeStruct((n, 64), F32), jax.ShapeDtypeStruct((n, 64), F32),
                   jax.ShapeDtypeStruct((n, 1), F32)],
    )(lre, lim, ldt, da_r, da_i, dk_r, dk_i)


def _loss_head(x, fg, target):
    s, d = x.shape
    tm = _tm(s)

    def body(x_ref, fg_ref, t_ref, loss_ref, dx_ref, dfg_ref):
        i = pl.program_id(0)

        @pl.when(i == 0)
        def _():
            loss_ref[...] = jnp.zeros_like(loss_ref)
            dfg_ref[...] = jnp.zeros_like(dfg_ref)

        xv, g = x_ref[...], fg_ref[...]
        r = lax.rsqrt(jnp.mean(xv * xv, axis=-1, keepdims=True) + EPS)
        xh = xv * r
        err = xh * g - t_ref[...]
        loss_ref[...] += 0.5 * jnp.sum(jnp.mean(err * err, axis=-1, keepdims=True), axis=0, keepdims=True)
        dy = err * (1.0 / d)
        dfg_ref[...] += jnp.sum(dy * xh, axis=0, keepdims=True)
        dxh = dy * g
        dx_ref[...] = r * (dxh - xh * jnp.mean(dxh * xh, axis=-1, keepdims=True))

    row = pl.BlockSpec((tm, d), lambda i: (i, 0))
    return pl.pallas_call(
        body, name="loss_head", grid=(s // tm,),
        in_specs=[row, _full((1, d)), row], out_specs=[_full((1, 1)), row, _full((1, d))],
        out_shape=[jax.ShapeDtypeStruct((1, 1), F32), jax.ShapeDtypeStruct((s, d), F32),
                   jax.ShapeDtypeStruct((1, d), F32)],
        compiler_params=_cp(dimension_semantics=("arbitrary",)),
    )(x, fg, target)


ADA_TN = 384


def _cond_fwd(cact, ada_w, ada_b_loc):
    nl, d, n = ada_w.shape

    def body(c_ref, w_ref, b_ref, o_ref):
        o_ref[...] = _dot(c_ref[...], w_ref[...]) + b_ref[...]

    return pl.pallas_call(
        body, name="cond_fwd", grid=(nl, n // ADA_TN),
        in_specs=[_full((N_DEV, d)), pl.BlockSpec((None, d, ADA_TN), lambda l, j: (l, 0, j)),
                  pl.BlockSpec((None, 1, ADA_TN), lambda l, j: (l, 0, j))],
        out_specs=pl.BlockSpec((None, N_DEV, ADA_TN), lambda l, j: (l, 0, j)),
        out_shape=jax.ShapeDtypeStruct((nl, N_DEV, n), F32),
        compiler_params=_cp(dimension_semantics=("parallel", "parallel")),
    )(cact, ada_w, ada_b_loc)


ELEMENTWISE_BLOCK_BYTES = 1 << 20


def _row_tile(r, c, itemsize=4):
    best = None
    for t in range(8, r + 1, 8):
        if r % t == 0 and t * c * itemsize <= ELEMENTWISE_BLOCK_BYTES:
            best = t
    return best if best is not None else r


def _adamw_math(w, g, m, v):
    m = ADAM_B1 * m + (1.0 - ADAM_B1) * g
    v = ADAM_B2 * v + (1.0 - ADAM_B2) * (g * g)
    m_hat = m / (1.0 - ADAM_B1 ** ADAM_STEP)
    v_hat = v / (1.0 - ADAM_B2 ** ADAM_STEP)
    delta = -ADAM_LR * (m_hat / (jnp.sqrt(v_hat) + ADAM_EPS) + ADAM_WD * w)
    return delta, m, v


def _ada_w_update(cact, dcond_loc, w, m, v):
    nl, d, n = w.shape

    def body(c_ref, dc_ref, w_ref, m_ref, v_ref, g_out, d_out, m_out, v_out):
        g = _dot_tn(c_ref[...], dc_ref[...])
        g_out[...] = g
        d_out[...], m_out[...], v_out[...] = _adamw_math(w_ref[...], g, m_ref[...], v_ref[...])

    blk = pl.BlockSpec((None, d, ADA_TN), lambda l, j: (l, 0, j))
    return pl.pallas_call(
        body, name="ada_w_update", grid=(nl, n // ADA_TN),
        in_specs=[_full((N_DEV, d)), pl.BlockSpec((None, N_DEV, ADA_TN), lambda l, j: (l, 0, j)), blk, blk, blk],
        out_specs=[blk] * 4, out_shape=[jax.ShapeDtypeStruct((nl, d, n), F32)] * 4,
        compiler_params=_cp(dimension_semantics=("parallel", "parallel")),
    )(cact, dcond_loc, w, m, v)


def _place():
    x, y, c = lax.axis_index("x"), lax.axis_index("y"), lax.axis_index("c")
    chips = [(1 - x, y), (x, 1 - y), (1 - x, 1 - y)]
    return x, y, c, chips


def _remote(src, dst, send_sem, recv_sem, to):
    return pltpu.make_async_remote_copy(src_ref=src, dst_ref=dst, send_sem=send_sem, recv_sem=recv_sem,
                                        device_id=to, device_id_type=MESH_ID)


def _sems(n):
    return [pltpu.SemaphoreType.DMA((n,)), pltpu.SemaphoreType.DMA((n,))]


def _all_gather8(v, name):
    r, cdim = v.shape

    def body(x_ref, out_ref, stage, send_sems, recv_sems):
        x, y, c, chips = _place()
        sibling = (x, y, 1 - c)

        def slot(px, py, pc):
            return out_ref.at[4 * px + 2 * py + pc]

        first = [_remote(x_ref, slot(x, y, c), send_sems.at[0], recv_sems.at[0], sibling)]
        first += [_remote(x_ref, slot(x, y, c), send_sems.at[1 + j], recv_sems.at[1 + j], (*chip, c))
                  for j, chip in enumerate(chips)]
        for cp in first:
            cp.start()
        pltpu.sync_copy(x_ref, stage)
        pltpu.sync_copy(stage, slot(x, y, c))
        passed = []
        for j, chip in enumerate(chips):
            blk = slot(*chip, c)
            _remote(blk, blk, send_sems.at[1 + j], recv_sems.at[1 + j], (x, y, c)).wait_recv()
            fw = _remote(blk, blk, send_sems.at[4 + j], recv_sems.at[4 + j], sibling)
            fw.start()
            passed.append(fw)
        blk = slot(x, y, 1 - c)
        _remote(blk, blk, send_sems.at[0], recv_sems.at[0], (x, y, c)).wait_recv()
        for j, chip in enumerate(chips):
            blk = slot(*chip, 1 - c)
            _remote(blk, blk, send_sems.at[4 + j], recv_sems.at[4 + j], (x, y, c)).wait_recv()
        for cp in first + passed:
            cp.wait_send()

    return pl.pallas_call(
        body, name=name, out_shape=jax.ShapeDtypeStruct((N_DEV, r, cdim), v.dtype),
        in_specs=[ANY], out_specs=ANY,
        scratch_shapes=[pltpu.VMEM((r, cdim), v.dtype)] + _sems(7),
        compiler_params=_cp(),
    )(v)


def _gather_first_copies():
    def make(refs, send_sems, recv_sems):
        x, y, c, chips = _place()
        mine = refs[0].at[4 * x + 2 * y + c]
        to = [(x, y, 1 - c)] + [(*chip, c) for chip in chips]
        return [_remote(mine, mine, send_sems.at[k], recv_sems.at[k], dev) for k, dev in enumerate(to)]
    return make


def _gather_pass_on(buf, name):
    def body(in_ref, out_ref, send_sems, recv_sems):
        x, y, c, chips = _place()
        passed = []
        for j, chip in enumerate(chips):
            blk = out_ref.at[4 * chip[0] + 2 * chip[1] + c]
            fw = _remote(blk, blk, send_sems.at[j], recv_sems.at[j], (x, y, 1 - c))
            fw.start()
            passed.append(fw)
        for j, chip in enumerate(chips):
            blk = out_ref.at[4 * chip[0] + 2 * chip[1] + 1 - c]
            _remote(blk, blk, send_sems.at[j], recv_sems.at[j], (x, y, c)).wait_recv()
        for fw in passed:
            fw.wait_send()

    return pl.pallas_call(
        body, name=name, out_shape=jax.ShapeDtypeStruct(buf.shape, buf.dtype),
        in_specs=[ANY], out_specs=ANY, input_output_aliases={0: 0}, scratch_shapes=_sems(3),
    )(buf)


def _place_weights(ws, layer, kidx, after):
    steps = 4
    shapes, in_specs, out_specs = [], [], []
    for w, kind in zip(ws, BIG_KINDS):
        _, a, b = w.shape
        in_specs.append(pl.BlockSpec((None, a // steps, b), lambda i, k: (layer, i, 0)))
        if kind == "col":
            shapes.append((2, a, 2 * b))
            out_specs.append(pl.BlockSpec((None, a // steps, b), lambda i, k: (k[0] // 2, i, k[0] % 2)))
        else:
            shapes.append((N_CHIP, a, b))
            out_specs.append(pl.BlockSpec((None, a // steps, b), lambda i, k: (k[0], i, 0)))

    def body(k_ref, *refs):
        outs = refs[len(ws) + 1:]
        for t in range(len(ws)):
            outs[t][...] = refs[t][...].astype(BF16)

    return pl.pallas_call(
        body, name="place_weights", out_shape=[jax.ShapeDtypeStruct(s, BF16) for s in shapes],
        grid_spec=pltpu.PrefetchScalarGridSpec(num_scalar_prefetch=1, grid=(steps,), in_specs=in_specs + [ANY],
                                               out_specs=out_specs),
        compiler_params=_cp(dimension_semantics=("parallel",)),
    )(kidx, *ws, after)


HBM = pl.BlockSpec(memory_space=pltpu.HBM)
SEM = pl.BlockSpec(memory_space=pltpu.SEMAPHORE)
EFFECT = pltpu.SideEffectType.DATAFLOW_SIDE_EFFECTING


def _weight_block(ref, kind, k, h):
    if kind == "col":
        ncol = ref.shape[3] // 2
        return ref.at[k // 2, h, :, pl.ds(pl.multiple_of((k % 2) * ncol, LANES), ncol)]
    return ref.at[k, h]


def _in_hbm(a):
    return pltpu.with_memory_space_constraint(a, pltpu.HBM)


def _weight_send_start(placed, kinds, name):
    nt = len(placed)

    def body(*refs):
        send_sems, recv_sems = refs[nt], refs[nt + 1]
        dst = refs[nt + 2:2 * nt + 2]
        token = refs[2 * nt + 2]
        x, y, c, chips = _place()
        kme = 2 * x + y
        for t in range(nt):
            for j, chip in enumerate(chips):
                own = _weight_block(dst[t], kinds[t], kme, c)
                _remote(own, own, send_sems.at[3 * t + j], recv_sems.at[3 * t + j], (*chip, c)).start()
        token[...] = jnp.zeros_like(token)

    return pl.pallas_call(
        body, name=name,
        out_shape=(pltpu.SemaphoreType.DMA((3 * nt,)), pltpu.SemaphoreType.DMA((3 * nt,)),
                   *[pltpu.HBM(a.shape, a.dtype) for a in placed], jax.ShapeDtypeStruct((8, LANES), F32)),
        in_specs=[HBM] * nt, out_specs=(SEM, SEM, *[HBM] * nt, pl.BlockSpec(memory_space=pltpu.VMEM)),
        input_output_aliases={t: 2 + t for t in range(nt)},
        compiler_params=pltpu.CompilerParams(has_side_effects=EFFECT),
    )(*[_in_hbm(a) for a in placed])


def _weight_send_wait(send_sems, recv_sems, arrays, kinds, after, name):
    nt = len(arrays)

    def body(*refs):
        arr = refs[:nt]
        send_sems, recv_sems = refs[nt], refs[nt + 1]
        x, y, c, chips = _place()
        kme = 2 * x + y
        for t in range(nt):
            for j, chip in enumerate(chips):
                own = _weight_block(arr[t], kinds[t], kme, c)
                got = _weight_block(arr[t], kinds[t], 2 * chip[0] + chip[1], c)
                cp = _remote(own, got, send_sems.at[3 * t + j], recv_sems.at[3 * t + j], (*chip, c))
                cp.wait_send()
                cp.wait_recv()

    return pl.pallas_call(
        body, name=name, out_shape=[pltpu.HBM(a.shape, a.dtype) for a in arrays],
        in_specs=[HBM] * nt + [SEM, SEM, ANY], out_specs=[HBM] * nt,
        input_output_aliases={t: t for t in range(nt)},
        compiler_params=pltpu.CompilerParams(has_side_effects=EFFECT),
    )(*arrays, send_sems, recv_sems, after)


def _forward_copies(kinds):
    def make(refs, send_sems, recv_sems):
        x, y, c, chips = _place()
        cps = []
        for t in range(len(kinds)):
            for j, chip in enumerate(chips):
                blk = _weight_block(refs[t], kinds[t], 2 * chip[0] + chip[1], c)
                cps.append(_remote(blk, blk, send_sems.at[3 * t + j], recv_sems.at[3 * t + j], (x, y, 1 - c)))
        return cps
    return make


def _split_start(name, arrays, n_copies, make_copies):
    na = len(arrays)

    def body(*refs):
        send_sems, recv_sems = refs[na], refs[na + 1]
        for cp in make_copies(refs[na + 2:2 * na + 2], send_sems, recv_sems):
            cp.start()
        token = refs[2 * na + 2]
        token[...] = jnp.zeros_like(token)

    return pl.pallas_call(
        body, name=name,
        out_shape=(pltpu.SemaphoreType.DMA((n_copies,)), pltpu.SemaphoreType.DMA((n_copies,)),
                   *[pltpu.HBM(a.shape, a.dtype) for a in arrays], jax.ShapeDtypeStruct((8, LANES), F32)),
        in_specs=[HBM] * na, out_specs=(SEM, SEM, *[HBM] * na, pl.BlockSpec(memory_space=pltpu.VMEM)),
        input_output_aliases={t: 2 + t for t in range(na)},
        compiler_params=pltpu.CompilerParams(has_side_effects=EFFECT),
    )(*[_in_hbm(a) for a in arrays])


def _split_wait(name, started, make_copies, after):
    send_sems, recv_sems, *arrays, _ = started
    na = len(arrays)

    def body(*refs):
        send_sems, recv_sems = refs[na], refs[na + 1]
        for cp in make_copies(refs[:na], send_sems, recv_sems):
            cp.wait_send()
            cp.wait_recv()

    return pl.pallas_call(
        body, name=name, out_shape=[pltpu.HBM(a.shape, a.dtype) for a in arrays],
        in_specs=[HBM] * na + [SEM, SEM, ANY], out_specs=[HBM] * na,
        input_output_aliases={t: t for t in range(na)},
        compiler_params=pltpu.CompilerParams(has_side_effects=EFFECT),
    )(*arrays, send_sems, recv_sems, after)


def _exchange_copies(nt):
    def make(refs, send_sems, recv_sems):
        x, y, c, _ = _place()
        return [_remote(refs[t].at[:, 1 - c], refs[nt + t], send_sems.at[t], recv_sems.at[t], (x, y, 1 - c))
                for t in range(nt)]
    return make


def _sibling_exchange_start(views, name):
    lands = [lax.empty((v.shape[0],) + v.shape[2:], v.dtype) for v in views]
    return _split_start(name, list(views) + lands, len(views), _exchange_copies(len(views)))


def _sibling_exchange_wait(started, after, name):
    nt = (len(started) - 3) // 2
    outs = _split_wait(name, started, _exchange_copies(nt), after)
    return outs[:nt], outs[nt:]


def _scatter_copies(src, land, kinds, send_sems, recv_sems):
    x, y, c, chips = _place()
    cps = []
    for t in range(len(src)):
        for j, chip in enumerate(chips):
            k = 2 * chip[0] + chip[1]
            if kinds[t] == "col":
                ncol = land[t].shape[2]
                win = src[t].at[k // 2, :, pl.ds(pl.multiple_of((k % 2) * ncol, LANES), ncol)]
            else:
                win = src[t].at[k]
            cps.append(_remote(win, land[t].at[j], send_sems.at[3 * t + j], recv_sems.at[3 * t + j], (*chip, c)))
    return cps


def _chip_scatter_start(parts, kinds, name):
    nt = len(parts)
    shapes = []
    for p, kind in zip(parts, kinds):
        shapes.append((3, p.shape[1], p.shape[2] // 2) if kind == "col" else (3,) + p.shape[1:])

    def body(*refs):
        send_sems, recv_sems = refs[2 * nt], refs[2 * nt + 1]
        src, land = refs[2 * nt + 2:3 * nt + 2], refs[3 * nt + 2:4 * nt + 2]
        token = refs[4 * nt + 2]
        for cp in _scatter_copies(src, land, kinds, send_sems, recv_sems):
            cp.start()
        token[...] = jnp.zeros_like(token)

    lands = [lax.empty(s, BF16) for s in shapes]
    return pl.pallas_call(
        body, name=name,
        out_shape=(pltpu.SemaphoreType.DMA((3 * nt,)), pltpu.SemaphoreType.DMA((3 * nt,)),
                   *[pltpu.HBM(a.shape, a.dtype) for a in parts], *[pltpu.HBM(s, BF16) for s in shapes],
                   jax.ShapeDtypeStruct((8, LANES), F32)),
        in_specs=[HBM] * (2 * nt), out_specs=(SEM, SEM, *[HBM] * (2 * nt), pl.BlockSpec(memory_space=pltpu.VMEM)),
        input_output_aliases={t: 2 + t for t in range(2 * nt)},
        compiler_params=pltpu.CompilerParams(has_side_effects=EFFECT),
    )(*[_in_hbm(a) for a in parts], *[_in_hbm(a) for a in lands])


def _chip_scatter_wait(send_sems, recv_sems, parts, lands, kinds, after, name):
    nt = len(parts)

    def body(*refs):
        src, land = refs[:nt], refs[nt:2 * nt]
        send_sems, recv_sems = refs[2 * nt], refs[2 * nt + 1]
        for cp in _scatter_copies(src, land, kinds, send_sems, recv_sems):
            cp.wait_send()
            cp.wait_recv()

    outs = pl.pallas_call(
        body, name=name, out_shape=[pltpu.HBM(a.shape, a.dtype) for a in list(parts) + list(lands)],
        in_specs=[HBM] * (2 * nt) + [SEM, SEM, ANY], out_specs=[HBM] * (2 * nt),
        input_output_aliases={t: t for t in range(2 * nt)},
        compiler_params=pltpu.CompilerParams(has_side_effects=EFFECT),
    )(*parts, *lands, send_sems, recv_sems, after)
    return outs[:nt], outs[nt:]


def _share_copies(nt):
    def make(refs, send_sems, recv_sems):
        x, y, c, _ = _place()
        return [_remote(refs[t].at[c], refs[t].at[c], send_sems.at[t], recv_sems.at[t], (x, y, 1 - c))
                for t in range(nt)]
    return make


def _sibling_share_start(fulls, name):
    return _split_start(name, list(fulls), len(fulls), _share_copies(len(fulls)))


def _sibling_share_wait(started, after, name):
    return _split_wait(name, started, _share_copies(len(started) - 3), after)


SUM_STEPS = 4


def _pair_sum(views, lands, ck):
    nt = len(views)
    in_specs, out_specs, shapes = [], [], []
    for v in views:
        b, _, r, cc = v.shape
        per = SUM_STEPS // b
        tr = r // per
        in_specs.append(pl.BlockSpec((None, None, tr, cc), lambda i, s, per=per: (i // per, s[0], i % per, 0)))
        out_specs.append(pl.BlockSpec((None, tr, cc), lambda i, s, per=per: (i // per, i % per, 0)))
        shapes.append((b, r, cc))
    in_specs = in_specs + out_specs

    def body(s_ref, *refs):
        for t in range(nt):
            refs[2 * nt + t][...] = (refs[t][...].astype(F32) + refs[nt + t][...].astype(F32)).astype(BF16)

    return pl.pallas_call(
        body, name="grad_pair_sum", out_shape=[jax.ShapeDtypeStruct(s, BF16) for s in shapes],
        grid_spec=pltpu.PrefetchScalarGridSpec(num_scalar_prefetch=1, grid=(SUM_STEPS,), in_specs=in_specs,
                                               out_specs=out_specs),
        compiler_params=_cp(dimension_semantics=("parallel",)),
    )(ck, *views, *lands)


def _chip_sum(parts, lands, kinds, ck):
    nt = len(parts)
    steps = 2
    in_own, in_land, out_specs, shapes = [], [], [], []
    for ld, kind in zip(lands, kinds):
        _, r, cc = ld.shape
        tr = r // steps
        if kind == "col":
            in_own.append(pl.BlockSpec((None, tr, cc), lambda i, s: (s[1] // 2, i, s[1] % 2)))
        else:
            in_own.append(pl.BlockSpec((None, tr, cc), lambda i, s: (s[1], i, 0)))
        in_land.append(pl.BlockSpec((3, tr, cc), lambda i, s: (0, i, 0)))
        out_specs.append(pl.BlockSpec((None, tr, cc), lambda i, s: (s[0], i, 0)))
        shapes.append((2, r, cc))

    def body(s_ref, *refs):
        for t in range(nt):
            acc = refs[t][...].astype(F32)
            for j in range(3):
                acc = acc + refs[nt + t][j].astype(F32)
            refs[2 * nt + t][...] = acc

    return pl.pallas_call(
        body, name="grad_chip_sum", out_shape=[jax.ShapeDtypeStruct(s, F32) for s in shapes],
        grid_spec=pltpu.PrefetchScalarGridSpec(num_scalar_prefetch=1, grid=(steps,), in_specs=in_own + in_land,
                                               out_specs=out_specs),
        compiler_params=_cp(dimension_semantics=("parallel",)),
    )(ck, *parts, *lands)


def _sum8(g):
    _, r, cc = g.shape
    tr = _row_tile(r, N_DEV * cc)

    def body(g_ref, o_ref):
        acc = g_ref[0].astype(F32)
        for d in range(1, N_DEV):
            acc = acc + g_ref[d].astype(F32)
        o_ref[...] = acc

    return pl.pallas_call(
        body, name="small_grad_sum", grid=(r // tr,),
        in_specs=[pl.BlockSpec((N_DEV, tr, cc), lambda i: (0, i, 0))],
        out_specs=pl.BlockSpec((tr, cc), lambda i: (i, 0)),
        out_shape=jax.ShapeDtypeStruct((r, cc), F32),
        compiler_params=_cp(dimension_semantics=("parallel",)),
    )(g)


def _silu_rows(c):
    def body(c_ref, o_ref):
        v = c_ref[...]
        o_ref[...] = v * jax.nn.sigmoid(v)

    return pl.pallas_call(body, name="cond_silu", out_shape=jax.ShapeDtypeStruct(c.shape, F32))(c)


def _pack(arrays):
    rows = []
    for a in arrays:
        flat = a.reshape(-1)
        rows.append(jnp.pad(flat, (0, (-flat.shape[0]) % (8 * LANES))).reshape(-1, LANES))
    n = sum(r.shape[0] for r in rows)
    if n % 256:
        rows.append(jnp.zeros((256 - n % 256, LANES), rows[0].dtype))
    return jnp.concatenate(rows, axis=0)


def _unpack(packed, shapes):
    out, off = [], 0
    for s in shapes:
        n = math.prod(s)
        nr = 8 * -(-n // (8 * LANES))
        out.append(packed[off:off + nr].reshape(-1)[:n].reshape(s))
        off += nr
    return out


def _as_rows(a):
    return a.reshape(1, -1) if a.ndim == 1 else a.reshape(-1, a.shape[-1])


def _adamw_many(ws, gs, ms, vs, name, steps=1):
    nt = len(ws)

    def body(*refs):
        for t in range(nt):
            w_ref, g_ref, m_ref, v_ref = (refs[k * nt + t] for k in range(4))
            d, m, v = _adamw_math(w_ref[...], g_ref[...], m_ref[...], v_ref[...])
            refs[4 * nt + t][...] = d
            refs[5 * nt + t][...] = m
            refs[6 * nt + t][...] = v

    shapes = [jax.ShapeDtypeStruct(a.shape, F32) for a in ws]
    if steps == 1:
        outs = pl.pallas_call(body, name=name, out_shape=shapes * 3, compiler_params=_cp())(*ws, *gs, *ms, *vs)
    else:
        specs = [pl.BlockSpec((a.shape[0] // steps, a.shape[1]), lambda i: (i, 0)) for a in ws]
        outs = pl.pallas_call(
            body, name=name, grid=(steps,), in_specs=specs * 4, out_specs=specs * 3, out_shape=shapes * 3,
            compiler_params=_cp(dimension_semantics=("parallel",)),
        )(*ws, *gs, *ms, *vs)
    return outs[:nt], outs[nt:2 * nt], outs[2 * nt:]


def _exchange_big_grads(grads, kinds, layer):
    views = []
    for g, kind in zip(grads, kinds):
        if kind == "col":
            views.append(g.reshape(2, 2, g.shape[1] // 2, g.shape[2]))
        else:
            views.append(g.reshape(N_CHIP, 2, g.shape[0] // (2 * N_CHIP), g.shape[1]))
    return _sibling_exchange_start(views, "grad_exchange_start_%d" % layer)


def _scatter_big_grads(exchanged, kinds, ck, after, layer):
    views, lands = _sibling_exchange_wait(exchanged, after, "grad_exchange_wait_%d" % layer)
    parts = _pair_sum(views, lands, ck)
    return _chip_scatter_start(parts, kinds, "grad_scatter_start_%d" % layer)


def _finish_big_grads(started, kinds, ck, after, layer):
    nt = len(kinds)
    send_sems, recv_sems = started[0], started[1]
    parts, lands = started[2:2 + nt], started[2 + nt:2 + 2 * nt]
    parts, lands = _chip_scatter_wait(send_sems, recv_sems, parts, lands, kinds, after, "grad_scatter_wait_%d" % layer)
    return _sibling_share_start(_chip_sum(parts, lands, kinds, ck), "grad_share_start_%d" % layer)


def _adamw_layer(ws, gs, ms, vs, stacks, layer, name, steps):
    nt = len(ws)
    stacks = [s if s is not None else tuple(lax.empty(w.shape, F32) for _ in range(4)) for s, w in zip(stacks, ws)]

    def body(*refs):
        for t in range(nt):
            w_ref, g_ref, m_ref, v_ref = (refs[k * nt + t] for k in range(4))
            outs = refs[8 * nt + 4 * t:8 * nt + 4 * t + 4]
            g = g_ref[...]
            outs[0][...] = g
            outs[1][...], outs[2][...], outs[3][...] = _adamw_math(w_ref[...], g, m_ref[...], v_ref[...])

    in_specs, g_specs, out_specs = [], [], []
    for w in ws:
        _, r, c = w.shape
        in_specs.append(pl.BlockSpec((None, r // steps, c), lambda i: (layer, i, 0)))
        g_specs.append(pl.BlockSpec((r // steps, c), lambda i: (i, 0)))
        out_specs += [pl.BlockSpec((None, r // steps, c), lambda i: (layer, i, 0))] * 4
    in_specs = in_specs + g_specs + in_specs * 2 + [ANY] * (4 * nt)
    flat = [a for s in stacks for a in s]
    outs = pl.pallas_call(
        body, name=name, grid=(steps,), in_specs=in_specs, out_specs=out_specs,
        out_shape=[jax.ShapeDtypeStruct(a.shape, F32) for a in flat],
        input_output_aliases={4 * nt + k: k for k in range(4 * nt)},
        compiler_params=_cp(dimension_semantics=("parallel",)),
    )(*ws, *gs, *ms, *vs, *flat)
    return [tuple(outs[4 * t:4 * t + 4]) for t in range(nt)]


SMALL_NAMES = ["ada_b", "norm1_g", "norm2_g", "sgu_w", "sgu_b", "pool_w", "pool_scale", "conv_w", "s5_lambda_re",
               "s5_lambda_im", "s5_b_re", "s5_b_im", "s5_c_re", "s5_c_im", "s5_d", "s5_log_dt", "s5_glu_w", "s5_glu_b",
               "mix_norm_g", "norm3_g", "final_norm_g"]
BIG_NAMES = ["ffn1_w_in", "ffn1_w_out", "w_mix_in", "w_mix_out", "ffn2_w_in", "ffn2_w_out"]
BIG_KINDS = ["col", "row", "row", "row", "col", "row"]
WEIGHT_ORDER = ["ada_w", "ada_b", "norm1_g", "ffn1_w_in", "ffn1_w_out", "norm2_g", "w_mix_in", "sgu_w", "sgu_b", "pool_w",
                "pool_scale", "conv_w", "s5_lambda_re", "s5_lambda_im", "s5_b_re", "s5_b_im", "s5_c_re", "s5_c_im", "s5_d",
                "s5_log_dt", "s5_glu_w", "s5_glu_b", "mix_norm_g", "w_mix_out", "norm3_g", "ffn2_w_in", "ffn2_w_out",
                "final_norm_g"]


def _local_step(x, target, cond, fetch_weights, prefetch_weights, p, emit_grads):
    nl, d = DEPTH, x.shape[1]
    row = lambda a: a.reshape(1, -1)
    saved = []
    for l in range(nl):
        (wi1, wo1, wmit, wmo, wi2, wo2), tok = fetch_weights(l, x)
        cl = cond[l] + tok
        mod1, mod2, mod3 = cl[0:3], cl[3:6], cl[6:9]
        lre, lim = p["s5_lambda_re"][l].reshape(-1), p["s5_lambda_im"][l].reshape(-1)
        ldt = jnp.repeat(p["s5_log_dt"][l], 64)
        rowp = jnp.stack([lre, lim, ldt])
        sp = (rowp, rowp.T, p["s5_b_re"][l].reshape(N_STATE, 16), p["s5_b_im"][l].reshape(N_STATE, 16),
              p["s5_c_re"][l].reshape(W_GRP, 64), p["s5_c_im"][l].reshape(W_GRP, 64), row(p["s5_d"][l]))
        glu = (p["s5_glu_w"][l], row(p["s5_glu_b"][l]))
        bias_full = jnp.repeat(p["sgu_b"][l].T, 64, axis=1)
        pw2 = p["pool_w"][l].reshape(W_GRP, 64)
        x1, h1, a1, b1, o1 = _ffn_fwd(x, mod1, row(p["norm1_g"][l]), wi1, wo1)
        z, h2 = _mix_in_fwd(x1, mod2, row(p["norm2_g"][l]), wmit)
        ya = _sgu_fwd(z, p["sgu_w"][l], bias_full)
        yb, yc = _poolconv_fwd(z, pw2, row(p["pool_scale"][l]), p["conv_w"][l])
        ypre = _s5_core_fwd(z, sp)
        yd = _s5_glu_fwd(ypre, *glu)
        ys = (ya, yb, yc, yd)
        x2, m = _mix_out_fwd(ys, row(p["mix_norm_g"][l]), wmo, x1, mod2[2:3])
        mod3 = mod3 + prefetch_weights(l + 1, x2)
        x3, h3, a3, b3, o3 = _ffn_fwd(x2, mod3, row(p["norm3_g"][l]), wi2, wo2)
        saved.append((x, x1, x2, h1, a1, b1, o1, z, h2, ys, m, h3, a3, b3, o3, sp, bias_full, pw2, ypre, glu,
                      (wi1, wo1, wmit, wmo, wi2, wo2), cl))
        x = x3

    loss, dx, dfg = _loss_head(x, row(p["final_norm_g"]), target)

    sg = {n: [None] * nl for n in SMALL_NAMES if n not in ("ada_b", "final_norm_g")}
    dcond = [None] * nl
    s5_da, s5_dk = [None] * nl, [None] * nl
    tok = 0.0
    for l in reversed(range(nl)):
        (x0, x1, x2, h1, a1, b1, o1, z, h2, ys, m, h3, a3, b3, o3, sp, bias_full, pw2, ypre, glu,
         (wi1, wo1, wmit, wmo, wi2, wo2), cl) = saved[l]
        cl = cl + tok
        mod1, mod2, mod3 = cl[0:3], cl[3:6], cl[6:9]
        dza, dzb, dwi2, dwo2, dgate3 = _ffn_bwd_main(dx, o3, mod3[2:3], h3, a3, b3, wo2)
        dx, rows3 = _ffn_bwd_in(dza, dzb, wi2, x2, dx, mod3, row(p["norm3_g"][l]))
        outs = _mix_out_bwd(dx, m, mod2[2:3], ys, row(p["mix_norm_g"][l]), wmo)
        dys, dgate2, dmng, dwmo = outs[0:4], outs[4], outs[5], outs[6]
        dza_, dsw, dsb = _sgu_bwd(z, dys[0], p["sgu_w"][l], bias_full)
        dzb_, dzc_, dwbd, dps, dcw = _poolconv_bwd(z, dys[1], dys[2], pw2, row(p["pool_scale"][l]), p["conv_w"][l])
        dypre, dgw, dgb = _s5_glu_bwd(ypre, dys[3], *glu)
        dzd_, dbr, dbi, dcr, dci, dd, da, dk = _s5_core_bwd(z, dypre, sp)
        dx, rows2, dwmit = _mix_in_bwd((dza_, dzb_, dzc_, dzd_), h2, wmit, x1, dx, mod2, row(p["norm2_g"][l]))
        dza, dzb, dwi1, dwo1, dgate1 = _ffn_bwd_main(dx, o1, mod1[2:3], h1, a1, b1, wo1)
        tok, layer_done = emit_grads(l, [dwi1, dwo1, dwmit, dwmo, dwi2, dwo2])
        dx, rows1 = _ffn_bwd_in(dza, dzb, wi1, x0, dx, mod1 + tok, row(p["norm1_g"][l]))
        if l > 0:
            tok = layer_done(dx)[0, 0]
        dcond[l] = jnp.concatenate([rows1[0:2], dgate1, rows2[0:2], dgate2, rows3[0:2], dgate3], axis=0)
        sg["norm1_g"][l], sg["norm2_g"][l], sg["norm3_g"][l] = rows1[2], rows2[2], rows3[2]
        sg["mix_norm_g"][l] = dmng[0]
        sg["sgu_w"][l] = dsw
        sg["sgu_b"][l] = dsb[:, 0:4].T
        g4 = dwbd.reshape(4, 64, 4, 64)
        sg["pool_w"][l] = jnp.stack([g4[k, :, k, :] for k in range(4)])
        sg["pool_scale"][l] = dps[0]
        sg["conv_w"][l] = dcw[0:3]
        sg["s5_b_re"][l], sg["s5_b_im"][l] = dbr.reshape(16, 64, 16), dbi.reshape(16, 64, 16)
        sg["s5_c_re"][l], sg["s5_c_im"][l] = dcr.reshape(16, 16, 64), dci.reshape(16, 16, 64)
        sg["s5_d"][l] = dd[0]
        sg["s5_glu_w"][l], sg["s5_glu_b"][l] = dgw, dgb[0]
        s5_da[l], s5_dk[l] = da, dk

    n16 = nl * 16
    dlre, dlim, dldt = _s5_param_bwd(
        p["s5_lambda_re"].reshape(n16, 64), p["s5_lambda_im"].reshape(n16, 64),
        jnp.repeat(p["s5_log_dt"].reshape(n16, 1), 64, axis=1),
        jnp.stack([a[0] for a in s5_da]).reshape(n16, 64), jnp.stack([a[1] for a in s5_da]).reshape(n16, 64),
        jnp.stack([k[:, 0] for k in s5_dk]).reshape(n16, 64), jnp.stack([k[:, 1] for k in s5_dk]).reshape(n16, 64))
    small = {n: jnp.stack(v) for n, v in sg.items() if v[0] is not None}
    small["s5_lambda_re"] = dlre.reshape(nl, 16, 64)
    small["s5_lambda_im"] = dlim.reshape(nl, 16, 64)
    small["s5_log_dt"] = dldt.reshape(nl, 16)
    small["final_norm_g"] = dfg[0]
    return loss, dx, small, jnp.stack(dcond), layer_done


def kernel(x, c, ada_w, ada_b, norm1_g, ffn1_w_in, ffn1_w_out, norm2_g, w_mix_in, sgu_w, sgu_b, pool_w, pool_scale, conv_w, s5_lambda_re, s5_lambda_im, s5_b_re, s5_b_im, s5_c_re, s5_c_im, s5_d, s5_log_dt, s5_glu_w, s5_glu_b, mix_norm_g, w_mix_out, norm3_g, ffn2_w_in, ffn2_w_out, final_norm_g, loss_target, m_ada_w, m_ada_b, m_norm1_g, m_ffn1_w_in, m_ffn1_w_out, m_norm2_g, m_w_mix_in, m_sgu_w, m_sgu_b, m_pool_w, m_pool_scale, m_conv_w, m_s5_lambda_re, m_s5_lambda_im, m_s5_b_re, m_s5_b_im, m_s5_c_re, m_s5_c_im, m_s5_d, m_s5_log_dt, m_s5_glu_w, m_s5_glu_b, m_mix_norm_g, m_w_mix_out, m_norm3_g, m_ffn2_w_in, m_ffn2_w_out, m_final_norm_g, v_ada_w, v_ada_b, v_norm1_g, v_ffn1_w_in, v_ffn1_w_out, v_norm2_g, v_w_mix_in, v_sgu_w, v_sgu_b, v_pool_w, v_pool_scale, v_conv_w, v_s5_lambda_re, v_s5_lambda_im, v_s5_b_re, v_s5_b_im, v_s5_c_re, v_s5_c_im, v_s5_d, v_s5_log_dt, v_s5_glu_w, v_s5_glu_b, v_mix_norm_g, v_w_mix_out, v_norm3_g, v_ffn2_w_in, v_ffn2_w_out, v_final_norm_g):
    args = dict(locals())
    w = {n: args[n] for n in WEIGHT_ORDER}
    mom = {n: args["m_" + n] for n in WEIGHT_ORDER}
    vel = {n: args["v_" + n] for n in WEIGHT_ORDER}
    nl, d = DEPTH, x.shape[-1]
    s = x.shape[1]
    px, py, pc = lax.axis_index("x"), lax.axis_index("y"), lax.axis_index("c")
    kme = 2 * px + py
    me = 2 * kme + pc
    kidx = jnp.reshape(kme, (1,)).astype(jnp.int32)

    shards = [ffn1_w_in, ffn1_w_out, jnp.swapaxes(w_mix_in, 1, 2), w_mix_out, ffn2_w_in, ffn2_w_out]
    started_weights = {}

    def start_weights(l, after):
        placed = _place_weights(shards, l, kidx, after)
        views = [a.reshape(a.shape[0], 2, a.shape[1] // 2, a.shape[2]) for a in placed]
        *handles, token = _weight_send_start(views, BIG_KINDS, "weight_send_start_%d" % l)
        started_weights[l] = handles
        return token

    token = start_weights(0, c)
    cact = _silu_rows(c + token[0, 0])

    pre = _pack([cact, conv_w, s5_glu_w])
    pre_all = _all_gather8(pre, "gather_prelude")
    parts = [_unpack(pre_all[dev], [cact.shape, conv_w.shape, s5_glu_w.shape]) for dev in range(N_DEV)]
    cact_all = pre_all[:, :d // LANES, :].reshape(N_DEV, d)
    conv_full = jnp.concatenate([parts[2 * k][1] for k in range(N_CHIP)], axis=2)
    glu_full = jnp.concatenate([parts[2 * k][2] for k in range(N_CHIP)], axis=1)

    n_ada = ada_w.shape[2]
    ada_b_loc = lax.dynamic_slice_in_dim(ada_b, kme * n_ada, n_ada, axis=1).reshape(nl, 1, n_ada)
    cond_part = _cond_fwd(cact_all, ada_w, ada_b_loc)
    cond_all = _all_gather8(cond_part.reshape(nl * N_DEV, n_ada), "gather_cond").reshape(N_DEV, nl, N_DEV, n_ada)
    cond_me = jnp.concatenate(
        [lax.dynamic_index_in_dim(cond_all[2 * k], me, axis=1, keepdims=False) for k in range(N_CHIP)], axis=1)
    token = cond_all
    for l in range(1, nl):
        token = start_weights(l, token)
    cond = cond_me.reshape(nl, 9, d) + token[0, 0]

    forwarding = {}

    def prefetch_weights(l, after):
        if l >= nl:
            return 0.0
        send_sems, recv_sems, *views = started_weights.pop(l)
        views = _weight_send_wait(send_sems, recv_sems, views, BIG_KINDS, after, "weight_send_wait_%d" % l)
        forwarding[l] = _split_start("weight_forward_start_%d" % l, views, 3 * len(views), _forward_copies(BIG_KINDS))
        return forwarding[l][-1][0, 0]

    def fetch_weights(l, after):
        if l not in forwarding:
            prefetch_weights(l, after)
        views = _split_wait("weight_forward_wait_%d" % l, forwarding.pop(l), _forward_copies(BIG_KINDS), after)
        full = [v.reshape(2, 2 * v.shape[2], v.shape[3]) if kind == "col" else v.reshape(-1, v.shape[3])
                for v, kind in zip(views, BIG_KINDS)]
        return full, 0.0

    ck = jnp.stack([pc, kme]).astype(jnp.int32)
    scattering, sharing = [], []
    stacks = {n: None for n in BIG_NAMES}
    groups = ((["ffn1_w_in", "ffn2_w_in"], 16, "adamw_w_in"),
              (["ffn1_w_out", "w_mix_in", "w_mix_out", "ffn2_w_out"], 8, "adamw_w_out"))

    def as_reduced(t):
        return {n: jnp.swapaxes(t[n], 1, 2) if n == "w_mix_in" else t[n] for n in BIG_NAMES}

    w_r, m_r, v_r = as_reduced(w), as_reduced(mom), as_reduced(vel)

    def apply_adamw(l, fulls):
        g = {n: f.reshape(2 * f.shape[1], f.shape[2]) for n, f in zip(BIG_NAMES, fulls)}
        for names, steps, call in groups:
            outs = _adamw_layer([w_r[n] for n in names], [g[n] for n in names], [m_r[n] for n in names],
                                [v_r[n] for n in names], [stacks[n] for n in names], l, call, steps)
            stacks.update(zip(names, outs))

    def retire_share(after):
        l2, shared = sharing.pop(0)
        apply_adamw(l2, _sibling_share_wait(shared, after, "grad_share_wait_%d" % l2))

    def retire_scatter(after):
        l1, scattered = scattering.pop(0)
        sharing.append((l1, _finish_big_grads(scattered, BIG_KINDS, ck, after, l1)))

    def retire(after):
        if sharing:
            retire_share(after)
        if scattering:
            retire_scatter(after)

    def emit_grads(l, grads_l):
        exchanged = _exchange_big_grads(grads_l, BIG_KINDS, l)

        def layer_done(after):
            started = _scatter_big_grads(exchanged, BIG_KINDS, ck, after, l)
            retire(after)
            scattering.append((l, started))
            return started[-1]

        return exchanged[-1][0, 0], layer_done

    p = {n: w[n] for n in SMALL_NAMES}
    p["conv_w"], p["s5_glu_w"] = conv_full, glu_full
    loss, dx, small, dcond, first_layer_done = _local_step(x[0], loss_target[0], cond, fetch_weights, prefetch_weights,
                                                           p, emit_grads)

    small_order = [n for n in SMALL_NAMES if n != "ada_b"]
    packed = _pack([dcond] + [small[n] for n in small_order]).astype(BF16)
    mine = lax.dynamic_update_slice(lax.empty((N_DEV,) + packed.shape, BF16), packed[None], (me, 0, 0))
    gathering = _split_start("small_grads_send_start", [mine], 4, _gather_first_copies())
    scatter_token = first_layer_done(gathering[-1])
    while sharing:
        retire_share(scatter_token)
    arrived, = _split_wait("small_grads_send_wait", gathering, _gather_first_copies(), stacks[BIG_NAMES[0]][0])
    gathered_small = _gather_pass_on(arrived, "small_grads_pass_on")
    total = _sum8(gathered_small)
    shapes = [dcond.shape] + [small[n].shape for n in small_order]
    tot = dict(zip(["ada_b"] + small_order, _unpack(total, shapes)))
    grads = {n: tot[n] for n in SMALL_NAMES}
    grads["ada_b"] = tot["ada_b"].reshape(nl, 9 * d)
    grads["conv_w"] = lax.dynamic_slice_in_dim(tot["conv_w"], kme * conv_w.shape[2], conv_w.shape[2], axis=2)
    grads["s5_glu_w"] = lax.dynamic_slice_in_dim(tot["s5_glu_w"], kme * s5_glu_w.shape[1], s5_glu_w.shape[1], axis=1)

    dcond_all = gathered_small.reshape(N_DEV, -1)[:, :dcond.size].reshape(N_DEV, nl, 9 * d)
    dcond_loc = jnp.swapaxes(lax.dynamic_slice_in_dim(dcond_all, kme * n_ada, n_ada, axis=2), 0, 1)
    g_ada, d_ada, m_ada, v_ada = _ada_w_update(cact_all, dcond_loc, ada_w, m_ada_w, v_ada_w)

    while scattering or sharing:
        retire(g_ada)
    delta, new_m, new_v = {}, {}, {}
    for n in BIG_NAMES:
        grads[n], delta[n], new_m[n], new_v[n] = (jnp.swapaxes(a, 1, 2) if n == "w_mix_in" else a for a in stacks[n])

    grads["ada_w"], delta["ada_w"], new_m["ada_w"], new_v["ada_w"] = g_ada, d_ada, m_ada, v_ada
    wide = ("s5_b_re", "s5_b_im")
    for names, call, steps in (([n for n in SMALL_NAMES if n not in wide], "adamw_small", 1),
                               (list(wide), "adamw_s5_b", DEPTH)):
        outs = _adamw_many(*[[_as_rows(t[n]) for n in names] for t in (w, grads, mom, vel)], call, steps)
        for res, o in zip((delta, new_m, new_v), outs):
            res.update({n: a.reshape(w[n].shape) for n, a in zip(names, o)})

    loss_total = lax.psum(loss[0, 0], ("x", "y", "c"))
    return (loss_total, dx[None], *[grads[n] for n in WEIGHT_ORDER], *[delta[n] for n in WEIGHT_ORDER],
            *[new_m[n] for n in WEIGHT_ORDER], *[new_v[n] for n in WEIGHT_ORDER])
```

```python
import math

import jax
import jax.numpy as jnp
from jax import lax
from jax.experimental import pallas as pl
from jax.experimental.pallas import tpu as pltpu

F32, BF16 = jnp.float32, jnp.bfloat16
EPS = 1e-6
DEPTH = 4
N_DEV = 8
N_CHIP = 4
W_GRP = 256
CHUNK = 128
N_SEG = 8
LANES = 128
FFN_TF = 256
FFN_TF_WIDE = 1408
FFN_TM_WIDE = 512
VMEM_LIMIT = 56 * 1024 * 1024
ADAM_LR, ADAM_B1, ADAM_B2, ADAM_EPS, ADAM_WD, ADAM_STEP = 0.001, 0.9, 0.999, 1e-08, 0.01, 10
MESH_ID = pl.DeviceIdType.MESH
HI = lax.Precision.HIGHEST
ANY = pl.BlockSpec(memory_space=pl.ANY)


def _cp(**kw):
    return pltpu.CompilerParams(vmem_limit_bytes=VMEM_LIMIT, **kw)


def _dot(a, b):
    return jnp.dot(a.astype(BF16), b.astype(BF16), preferred_element_type=F32)


def _dot_nt(a, b):
    return lax.dot_general(a.astype(BF16), b.astype(BF16), (((1,), (1,)), ((), ())), preferred_element_type=F32)


def _dot_tn(a, b):
    return lax.dot_general(a.astype(BF16), b.astype(BF16), (((0,), (0,)), ((), ())), preferred_element_type=F32)


def _dot_hi(a, b):
    return jnp.dot(a, b, preferred_element_type=F32, precision=HI)


def _gelu(x):
    k = 0.7978845608028654
    t = jnp.tanh(k * (x + 0.044715 * x * x * x))
    return 0.5 * x * (1.0 + t), t


def _gelu_grad(x, t):
    k = 0.7978845608028654
    return 0.5 * (1.0 + t) + 0.5 * x * (1.0 - t * t) * k * (1.0 + 3.0 * 0.044715 * x * x)


def _iota(shape, axis):
    return lax.broadcasted_iota(jnp.int32, shape, axis)


def _full(shape):
    nd = len(shape)
    return pl.BlockSpec(shape, lambda *_: (0,) * nd)


def _norm_mod(xv, g, shift, scale):
    r = lax.rsqrt(jnp.mean(xv * xv, axis=-1, keepdims=True) + EPS)
    return (xv * r * g) * (1.0 + scale) + shift


def _norm_mod_bwd(xv, g, scale, dh):
    r = lax.rsqrt(jnp.mean(xv * xv, axis=-1, keepdims=True) + EPS)
    xh = xv * r
    n = xh * g
    dsh = jnp.sum(dh, axis=0, keepdims=True)
    dsc = jnp.sum(dh * n, axis=0, keepdims=True)
    dn = dh * (1.0 + scale)
    dg = jnp.sum(dn * xh, axis=0, keepdims=True)
    dxh = dn * g
    dx = r * (dxh - xh * jnp.mean(dxh * xh, axis=-1, keepdims=True))
    return dx, dsh, dsc, dg


def _tm(s):
    return min(s, 1024)


def _ffn_fwd(x, mod, g, wi, wo):
    s, d = x.shape
    f = wo.shape[0]
    tf, tm = FFN_TF_WIDE, min(s, FFN_TM_WIDE)
    nf, nt = f // tf, s // tm

    def body(x_ref, mod_ref, g_ref, wa_ref, wb_ref, wo_ref, xn_ref, h_ref, a_ref, b_ref, o_ref, h_all, acc):
        j, i = pl.program_id(0), pl.program_id(1)
        rows = pl.ds(pl.multiple_of(i * tm, tm), tm)

        @pl.when(j == 0)
        def _():
            hh = _norm_mod(x_ref[...], g_ref[...], mod_ref[0:1, :], mod_ref[1:2, :]).astype(BF16)
            h_ref[...] = hh
            h_all[rows, :] = hh

        h = h_all[rows, :]
        a = jnp.dot(h, wa_ref[...], preferred_element_type=F32)
        b = jnp.dot(h, wb_ref[...], preferred_element_type=F32)
        a_ref[...] = a.astype(BF16)
        b_ref[...] = b.astype(BF16)
        u = (a * jax.nn.sigmoid(a)) * b
        part = jnp.dot(u.astype(BF16), wo_ref[...], preferred_element_type=F32)

        @pl.when(j == 0)
        def _():
            acc[rows, :] = part

        @pl.when(jnp.logical_and(j > 0, j < nf - 1))
        def _():
            acc[rows, :] += part

        @pl.when(j == nf - 1)
        def _():
            o = part + acc[rows, :] if nf > 1 else part
            o_ref[...] = o.astype(BF16)
            xn_ref[...] = x_ref[...] + 0.5 * mod_ref[2:3, :] * o

    row = pl.BlockSpec((tm, d), lambda j, i: (i, 0))
    early = pl.BlockSpec((tm, d), lambda j, i: (jnp.where(j == 0, i, nt - 1), 0))
    late = pl.BlockSpec((tm, d), lambda j, i: (jnp.where(j == nf - 1, i, 0), 0))
    chunk = pl.BlockSpec((tm, tf), lambda j, i: (i, j))
    once = dict(pipeline_mode=pl.Buffered(1)) if nf > 1 else {}
    return pl.pallas_call(
        body, name="ffn_fwd", grid=(nf, nt),
        in_specs=[row, _full((3, d)), _full((1, d)),
                  pl.BlockSpec((None, d, tf), lambda j, i: (0, 0, j), **once),
                  pl.BlockSpec((None, d, tf), lambda j, i: (1, 0, j), **once),
                  pl.BlockSpec((tf, d), lambda j, i: (j, 0), **once)],
        out_specs=[late, early, chunk, chunk, late],
        out_shape=[jax.ShapeDtypeStruct((s, d), F32), jax.ShapeDtypeStruct((s, d), BF16),
                   jax.ShapeDtypeStruct((s, f), BF16), jax.ShapeDtypeStruct((s, f), BF16),
                   jax.ShapeDtypeStruct((s, d), BF16)],
        scratch_shapes=[pltpu.VMEM((s, d), BF16), pltpu.VMEM((s, d), F32)],
        compiler_params=_cp(dimension_semantics=("arbitrary", "arbitrary")),
    )(x, mod, g, wi, wi, wo)


def _ffn_bwd_main(dxo, o, gate, h, a, b, wo):
    s, d = dxo.shape
    f = wo.shape[0]
    tf = FFN_TF
    nf = f // tf

    def body(dxo_ref, o_ref, gate_ref, h_ref, a_ref, b_ref, wo_ref, dza_ref, dzb_ref, dwi_ref, dwo_ref, dg_ref, do_s):
        @pl.when(pl.program_id(0) == 0)
        def _():
            dxv = dxo_ref[...]
            do_s[...] = (0.5 * gate_ref[...] * dxv).astype(BF16)
            dg_ref[...] = 0.5 * jnp.sum(o_ref[...].astype(F32) * dxv, axis=0, keepdims=True)

        dov = do_s[...]
        hv = h_ref[...]
        du = lax.dot_general(dov, wo_ref[...], (((1,), (1,)), ((), ())), preferred_element_type=F32)
        av = a_ref[...].astype(F32)
        bv = b_ref[...].astype(F32)
        sa = jax.nn.sigmoid(av)
        si = av * sa
        u = (si * bv).astype(BF16)
        da = (du * bv * (sa * (1.0 + av * (1.0 - sa)))).astype(BF16)
        db = (du * si).astype(BF16)
        dza_ref[...] = da
        dzb_ref[...] = db
        dwo_ref[...] = _dot_tn(u, dov).astype(BF16)
        dwi_ref[0] = _dot_tn(hv, da).astype(BF16)
        dwi_ref[1] = _dot_tn(hv, db).astype(BF16)

    chunk = pl.BlockSpec((s, tf), lambda j: (0, j))
    once = lambda: pl.BlockSpec((s, d), lambda j: (0, 0), pipeline_mode=pl.Buffered(1))
    return pl.pallas_call(
        body, name="ffn_bwd_main", grid=(nf,),
        in_specs=[once(), once(), _full((1, d)), once(), chunk, chunk, pl.BlockSpec((tf, d), lambda j: (j, 0))],
        out_specs=[chunk, chunk, pl.BlockSpec((2, d, tf), lambda j: (0, 0, j)),
                   pl.BlockSpec((tf, d), lambda j: (j, 0)), _full((1, d))],
        out_shape=[jax.ShapeDtypeStruct((s, f), BF16), jax.ShapeDtypeStruct((s, f), BF16),
                   jax.ShapeDtypeStruct((2, d, f), BF16), jax.ShapeDtypeStruct((f, d), BF16),
                   jax.ShapeDtypeStruct((1, d), F32)],
        scratch_shapes=[pltpu.VMEM((s, d), BF16)],
        compiler_params=_cp(dimension_semantics=("arbitrary",)),
    )(dxo, o, gate, h, a, b, wo)


def _ffn_bwd_in(dza, dzb, wi, x, dxo, mod, g):
    s, d = x.shape
    f = dza.shape[1]
    tf, tm = FFN_TF_WIDE, min(s, FFN_TM_WIDE)
    nf, nt = f // tf, s // tm

    def body(dza_ref, dzb_ref, wa_ref, wb_ref, x_ref, dxo_ref, mod_ref, g_ref, dx_ref, rows_ref, acc):
        j, i = pl.program_id(0), pl.program_id(1)
        rows = pl.ds(pl.multiple_of(i * tm, tm), tm)

        @pl.when(jnp.logical_and(i == 0, j == 0))
        def _():
            rows_ref[...] = jnp.zeros_like(rows_ref)

        part = (lax.dot_general(dza_ref[...], wa_ref[...], (((1,), (1,)), ((), ())), preferred_element_type=F32)
                + lax.dot_general(dzb_ref[...], wb_ref[...], (((1,), (1,)), ((), ())), preferred_element_type=F32))

        @pl.when(j == 0)
        def _():
            acc[rows, :] = part

        @pl.when(jnp.logical_and(j > 0, j < nf - 1))
        def _():
            acc[rows, :] += part

        @pl.when(j == nf - 1)
        def _():
            dh = part + acc[rows, :] if nf > 1 else part
            dx, dsh, dsc, dg = _norm_mod_bwd(x_ref[...], g_ref[...], mod_ref[1:2, :], dh)
            dx_ref[...] = dx + dxo_ref[...]
            rows_ref[0:1, :] += dsh
            rows_ref[1:2, :] += dsc
            rows_ref[2:3, :] += dg

    late = pl.BlockSpec((tm, d), lambda j, i: (jnp.where(j == nf - 1, i, 0), 0))
    chunk = pl.BlockSpec((tm, tf), lambda j, i: (i, j))
    return pl.pallas_call(
        body, name="ffn_bwd_in", grid=(nf, nt),
        in_specs=[chunk, chunk,
                  pl.BlockSpec((None, d, tf), lambda j, i: (0, 0, j)),
                  pl.BlockSpec((None, d, tf), lambda j, i: (1, 0, j)),
                  late, late, _full((3, d)), _full((1, d))],
        out_specs=[late, _full((8, d))],
        out_shape=[jax.ShapeDtypeStruct((s, d), F32), jax.ShapeDtypeStruct((8, d), F32)],
        scratch_shapes=[pltpu.VMEM((s, d), F32)],
        compiler_params=_cp(dimension_semantics=("arbitrary", "arbitrary")),
    )(dza, dzb, wi, wi, x, dxo, mod, g)


def _mix_in_fwd(x, mod, g, wmit):
    s, d = x.shape
    p = wmit.shape[0]
    tm = _tm(s)

    def body(x_ref, mod_ref, g_ref, w_ref, z_ref, h_ref):
        hh = _norm_mod(x_ref[...], g_ref[...], mod_ref[0:1, :], mod_ref[1:2, :]).astype(BF16)
        h_ref[...] = hh
        z_ref[...] = lax.dot_general(hh, w_ref[...], (((1,), (1,)), ((), ())), preferred_element_type=F32)

    row = pl.BlockSpec((tm, d), lambda i: (i, 0))
    return pl.pallas_call(
        body, name="mix_in_fwd", grid=(s // tm,),
        in_specs=[row, _full((3, d)), _full((1, d)), _full((p, d))],
        out_specs=[pl.BlockSpec((tm, p), lambda i: (i, 0)), row],
        out_shape=[jax.ShapeDtypeStruct((s, p), F32), jax.ShapeDtypeStruct((s, d), BF16)],
        compiler_params=_cp(dimension_semantics=("parallel",)),
    )(x, mod, g, wmit)


def _mix_in_bwd(dzs, h, wmit, x, dxo, mod, g):
    s, d = x.shape
    p = wmit.shape[0]
    tm = min(s, 512)
    nt = s // tm

    def body(za_ref, zb_ref, zc_ref, zd_ref, h_ref, w_ref, x_ref, dxo_ref, mod_ref, g_ref,
             dx_ref, rows_ref, dw_ref, acc):
        i = pl.program_id(0)

        @pl.when(i == 0)
        def _():
            rows_ref[...] = jnp.zeros_like(rows_ref)
            acc[...] = jnp.zeros_like(acc)

        dz = jnp.concatenate([za_ref[...], zb_ref[...], zc_ref[...], zd_ref[...]], axis=1).astype(BF16)
        acc[...] += _dot_tn(dz, h_ref[...])
        dh = jnp.dot(dz, w_ref[...], preferred_element_type=F32)
        dx, dsh, dsc, dg = _norm_mod_bwd(x_ref[...], g_ref[...], mod_ref[1:2, :], dh)
        dx_ref[...] = dx + dxo_ref[...]
        rows_ref[0:1, :] += dsh
        rows_ref[1:2, :] += dsc
        rows_ref[2:3, :] += dg

        @pl.when(i == nt - 1)
        def _():
            dw_ref[...] = acc[...].astype(BF16)

    row = pl.BlockSpec((tm, d), lambda i: (i, 0))
    zspecs = [pl.BlockSpec((tm, z.shape[1]), lambda i: (i, 0)) for z in dzs]
    return pl.pallas_call(
        body, name="mix_in_bwd", grid=(nt,),
        in_specs=zspecs + [row, _full((p, d)), row, row, _full((3, d)), _full((1, d))],
        out_specs=[row, _full((8, d)), _full((p, d))],
        out_shape=[jax.ShapeDtypeStruct((s, d), F32), jax.ShapeDtypeStruct((8, d), F32),
                   jax.ShapeDtypeStruct((p, d), BF16)],
        scratch_shapes=[pltpu.VMEM((p, d), F32)],
        compiler_params=_cp(dimension_semantics=("arbitrary",)),
    )(*dzs, h, wmit, x, dxo, mod, g)


def _group_norm(ys, mng):
    outs, hats, rs = [], [], []
    for k, y in enumerate(ys):
        r = lax.rsqrt(jnp.mean(y * y, axis=-1, keepdims=True) + EPS)
        yh = y * r
        hats.append(yh)
        rs.append(r)
        outs.append(yh * mng[:, k * W_GRP:(k + 1) * W_GRP])
    return jnp.concatenate(outs, axis=1), hats, rs


def _mix_out_fwd(ys, mng, wmo, x, gate):
    s, d = x.shape
    tm = _tm(s)

    def body(ya, yb, yc, yd, mng_ref, w_ref, x_ref, gate_ref, xn_ref, m_ref):
        yn, _, _ = _group_norm([ya[...], yb[...], yc[...], yd[...]], mng_ref[...])
        m = jnp.dot(yn.astype(BF16), w_ref[...], preferred_element_type=F32)
        m_ref[...] = m
        xn_ref[...] = x_ref[...] + gate_ref[...] * m

    row = pl.BlockSpec((tm, d), lambda i: (i, 0))
    grp = pl.BlockSpec((tm, W_GRP), lambda i: (i, 0))
    return pl.pallas_call(
        body, name="mix_out_fwd", grid=(s // tm,),
        in_specs=[grp, grp, grp, grp, _full((1, d)), _full((d, d)), row, _full((1, d))],
        out_specs=[row, row],
        out_shape=[jax.ShapeDtypeStruct((s, d), F32), jax.ShapeDtypeStruct((s, d), F32)],
        compiler_params=_cp(dimension_semantics=("parallel",)),
    )(*ys, mng, wmo, x, gate)


def _mix_out_bwd(dxo, m, gate, ys, mng, wmo):
    s, d = dxo.shape
    tm = min(s, 512)
    nt = s // tm

    def body(dxo_ref, m_ref, gate_ref, ya, yb, yc, yd, mng_ref, w_ref,
             dya, dyb, dyc, dyd, dgate_ref, dmng_ref, dw_ref, acc):
        i = pl.program_id(0)

        @pl.when(i == 0)
        def _():
            dgate_ref[...] = jnp.zeros_like(dgate_ref)
            dmng_ref[...] = jnp.zeros_like(dmng_ref)
            acc[...] = jnp.zeros_like(acc)

        dxv = dxo_ref[...]
        dgate_ref[...] += jnp.sum(m_ref[...] * dxv, axis=0, keepdims=True)
        dm = (gate_ref[...] * dxv).astype(BF16)
        mng = mng_ref[...]
        yn, hats, rs = _group_norm([ya[...], yb[...], yc[...], yd[...]], mng)
        acc[...] += _dot_tn(yn, dm)
        dyn = lax.dot_general(dm, w_ref[...], (((1,), (1,)), ((), ())), preferred_element_type=F32)
        dmng_parts = []
        for k, (yh, r, out) in enumerate(zip(hats, rs, (dya, dyb, dyc, dyd))):
            dk = dyn[:, k * W_GRP:(k + 1) * W_GRP]
            dmng_parts.append(jnp.sum(dk * yh, axis=0, keepdims=True))
            dyh = dk * mng[:, k * W_GRP:(k + 1) * W_GRP]
            out[...] = r * (dyh - yh * jnp.mean(dyh * yh, axis=-1, keepdims=True))
        dmng_ref[...] += jnp.concatenate(dmng_parts, axis=1)

        @pl.when(i == nt - 1)
        def _():
            dw_ref[...] = acc[...].astype(BF16)

    row = pl.BlockSpec((tm, d), lambda i: (i, 0))
    grp = pl.BlockSpec((tm, W_GRP), lambda i: (i, 0))
    return pl.pallas_call(
        body, name="mix_out_bwd", grid=(nt,),
        in_specs=[row, row, _full((1, d)), grp, grp, grp, grp, _full((1, d)), _full((d, d))],
        out_specs=[grp, grp, grp, grp, _full((1, d)), _full((1, d)), _full((d, d))],
        out_shape=[jax.ShapeDtypeStruct((s, W_GRP), F32)] * 4
        + [jax.ShapeDtypeStruct((1, d), F32), jax.ShapeDtypeStruct((1, d), F32), jax.ShapeDtypeStruct((d, d), BF16)],
        scratch_shapes=[pltpu.VMEM((d, d), F32)],
        compiler_params=_cp(dimension_semantics=("arbitrary",)),
    )(dxo, m, gate, *ys, mng, wmo)


def _sgu_consts():
    r = _iota((W_GRP, W_GRP), 0) >> 6
    c = _iota((W_GRP, W_GRP), 1) >> 6
    avg = jnp.where(r == c, 1.0 / 64.0, 0.0).astype(F32)
    tril = _iota((CHUNK, CHUNK), 0) >= _iota((CHUNK, CHUNK), 1)
    head = _iota((CHUNK, W_GRP), 1) >> 6
    return avg, tril, head


def _sgu_pre(za, avg):
    zg, t = _gelu(za)
    u, v = zg[:, :W_GRP], zg[:, W_GRP:]
    mu = _dot_hi(v, avg)
    vc = v - mu
    r = lax.rsqrt(_dot_hi(vc * vc, avg) + EPS)
    return t, u, vc * r, r


def _sgu_fwd(z, sgu_w, bias_full):
    s = z.shape[0]
    tm = min(s, 512)

    def body(za_ref, w_ref, bias_ref, ya_ref):
        avg, tril, head = _sgu_consts()
        _, u, vn, _ = _sgu_pre(za_ref[...], avg)
        wm = [jnp.where(tril, w_ref[h], 0.0).astype(BF16) for h in range(4)]
        vb = vn.astype(BF16)
        for n in range(tm // CHUNK):
            rows = slice(n * CHUNK, (n + 1) * CHUNK)
            mixed = bias_ref[...]
            for h in range(4):
                mixed = mixed + jnp.where(head == h, jnp.dot(wm[h], vb[rows], preferred_element_type=F32), 0.0)
            ya_ref[rows, :] = u[rows] * mixed

    return pl.pallas_call(
        body, name="sgu_fwd", grid=(s // tm,),
        in_specs=[pl.BlockSpec((tm, 2 * W_GRP), lambda i: (i, 0)), _full((4, CHUNK, CHUNK)), _full((CHUNK, W_GRP))],
        out_specs=pl.BlockSpec((tm, W_GRP), lambda i: (i, 0)),
        out_shape=jax.ShapeDtypeStruct((s, W_GRP), F32),
        compiler_params=_cp(dimension_semantics=("parallel",)),
    )(z, sgu_w, bias_full)


def _sgu_bwd(z, dya, sgu_w, bias_full):
    s = z.shape[0]
    tm = min(s, 512)
    nt = s // tm

    def body(za_ref, dya_ref, w_ref, bias_ref, dza_ref, dw_ref, db_ref, du_s, dvn_s):
        i = pl.program_id(0)

        @pl.when(i == 0)
        def _():
            dw_ref[...] = jnp.zeros_like(dw_ref)
            db_ref[...] = jnp.zeros_like(db_ref)

        avg, tril, head = _sgu_consts()
        za = za_ref[...]
        t, u, vn, r = _sgu_pre(za, avg)
        wm = [jnp.where(tril, w_ref[h], 0.0).astype(BF16) for h in range(4)]
        vb = vn.astype(BF16)
        dya = dya_ref[...]
        dw = [jnp.zeros((CHUNK, CHUNK), F32) for _ in range(4)]
        db = jnp.zeros((CHUNK, W_GRP), F32)
        for n in range(tm // CHUNK):
            rows = slice(n * CHUNK, (n + 1) * CHUNK)
            mixed = bias_ref[...]
            for h in range(4):
                mixed = mixed + jnp.where(head == h, jnp.dot(wm[h], vb[rows], preferred_element_type=F32), 0.0)
            dmix = dya[rows] * u[rows]
            du_s[rows, :] = dya[rows] * mixed
            db = db + dmix
            dmb = dmix.astype(BF16)
            dvn = jnp.zeros((CHUNK, W_GRP), F32)
            for h in range(4):
                dmh = jnp.where(head == h, dmix, 0.0)
                dw[h] = dw[h] + _dot_nt(dmh, vb[rows])
                dvn = dvn + jnp.where(head == h, _dot_tn(wm[h], dmb), 0.0)
            dvn_s[rows, :] = dvn
        for h in range(4):
            dw_ref[h] += jnp.where(tril, dw[h], 0.0)
        sel = ((_iota((W_GRP, CHUNK), 0) >> 6) == _iota((W_GRP, CHUNK), 1)).astype(F32)
        db_ref[...] += _dot_hi(db, sel)
        dvn = dvn_s[...]
        dv = r * (dvn - _dot_hi(dvn, avg) - vn * _dot_hi(dvn * vn, avg))
        dzg = jnp.concatenate([du_s[...], dv], axis=1)
        dza_ref[...] = dzg * _gelu_grad(za, t)

    return pl.pallas_call(
        body, name="sgu_bwd", grid=(nt,),
        in_specs=[pl.BlockSpec((tm, 2 * W_GRP), lambda i: (i, 0)), pl.BlockSpec((tm, W_GRP), lambda i: (i, 0)),
                  _full((4, CHUNK, CHUNK)), _full((CHUNK, W_GRP))],
        out_specs=[pl.BlockSpec((tm, 2 * W_GRP), lambda i: (i, 0)), _full((4, CHUNK, CHUNK)), _full((CHUNK, CHUNK))],
        out_shape=[jax.ShapeDtypeStruct((s, 2 * W_GRP), F32), jax.ShapeDtypeStruct((4, CHUNK, CHUNK), F32),
                   jax.ShapeDtypeStruct((CHUNK, CHUNK), F32)],
        scratch_shapes=[pltpu.VMEM((tm, W_GRP), F32), pltpu.VMEM((tm, W_GRP), F32)],
        compiler_params=_cp(dimension_semantics=("arbitrary",)),
    )(z, dya, sgu_w, bias_full)


def _shift_down(x, k):
    return jnp.where(_iota(x.shape, 0) < k, 0.0, pltpu.roll(x, k, 0))


def _shift_up(x, k):
    n = x.shape[0]
    return jnp.where(_iota(x.shape, 0) >= n - k, 0.0, pltpu.roll(x, n - k, 0))


def _by_pool_group(shape, v2, v4, v8, v16):
    col = _iota(shape, 1)
    return jnp.where(col < 64, v2, jnp.where(col < 128, v4, jnp.where(col < 192, v8, v16)))


def _pool_core(zb, pw2):
    s2 = zb + _shift_down(zb, 1)
    s4 = s2 + _shift_down(s2, 2)
    s8 = s4 + _shift_down(s4, 4)
    s16 = s8 + _shift_down(s8, 8)
    win = _by_pool_group(zb.shape, s2, s4, s8, s16)
    wlen = _by_pool_group(zb.shape, 2.0, 4.0, 8.0, 16.0)
    cnt = jnp.minimum((_iota(zb.shape, 0) + 1).astype(F32), wlen)
    p = win / cnt - zb
    wt = jnp.tile(pw2, (1, 4))
    wbd = jnp.where((_iota(wt.shape, 0) >> 6) == (_iota(wt.shape, 1) >> 6), wt, 0.0).astype(BF16)
    return p, cnt, wbd


def _conv_core(zc, cw):
    bg, cg, xh = zc[:, :W_GRP], zc[:, W_GRP:2 * W_GRP], zc[:, 2 * W_GRP:]
    y = cg * xh
    y1, y2 = _shift_down(y, 1), _shift_down(y, 2)
    out = cw[2:3, :] * y + cw[1:2, :] * y1 + cw[0:1, :] * y2
    return bg, cg, xh, y, y1, y2, out


def _poolconv_fwd(z, pw2, pscale, cw):
    s = z.shape[0]

    def body(zb_ref, zc_ref, pw_ref, ps_ref, cw_ref, yb_ref, yc_ref):
        p, _, wbd = _pool_core(zb_ref[...], pw_ref[...])
        yb_ref[...] = jnp.dot(p.astype(BF16), wbd, preferred_element_type=F32) * ps_ref[...]
        bg, _, _, _, _, _, out = _conv_core(zc_ref[...], cw_ref[...])
        yc_ref[...] = bg * out

    return pl.pallas_call(
        body, name="poolconv_fwd", grid=(1,),
        in_specs=[pl.BlockSpec((s, W_GRP), lambda i: (0, 2)), pl.BlockSpec((s, 3 * W_GRP), lambda i: (0, 1)),
                  _full((W_GRP, 64)), _full((1, W_GRP)), _full((3, W_GRP))],
        out_specs=[_full((s, W_GRP)), _full((s, W_GRP))],
        out_shape=[jax.ShapeDtypeStruct((s, W_GRP), F32)] * 2,
        compiler_params=_cp(dimension_semantics=("arbitrary",)),
    )(z, z, pw2, pscale, cw)


def _poolconv_bwd(z, dyb, dyc, pw2, pscale, cw):
    s = z.shape[0]

    def body(zb_ref, zc_ref, dyb_ref, dyc_ref, pw_ref, ps_ref, cw_ref, dzb_ref, dzc_ref, dw_ref, dps_ref, dcw_ref):
        zb = zb_ref[...]
        p, cnt, wbd = _pool_core(zb, pw_ref[...])
        pb = p.astype(BF16)
        out = jnp.dot(pb, wbd, preferred_element_type=F32)
        dyb = dyb_ref[...]
        dps_ref[...] = jnp.sum(dyb * out, axis=0, keepdims=True)
        dout = (dyb * ps_ref[...]).astype(BF16)
        dw = _dot_tn(pb, dout)
        dw_ref[...] = jnp.where((_iota(dw.shape, 0) >> 6) == (_iota(dw.shape, 1) >> 6), dw, 0.0)
        dp = lax.dot_general(dout, wbd, (((1,), (1,)), ((), ())), preferred_element_type=F32)
        dwin = dp / cnt
        t2 = dwin + _shift_up(dwin, 1)
        t4 = t2 + _shift_up(t2, 2)
        t8 = t4 + _shift_up(t4, 4)
        t16 = t8 + _shift_up(t8, 8)
        dzb_ref[...] = _by_pool_group(zb.shape, t2, t4, t8, t16) - dp

        cw = cw_ref[...]
        bg, cg, xh, y, y1, y2, out = _conv_core(zc_ref[...], cw)
        dyc = dyc_ref[...]
        dout = dyc * bg
        dcw_ref[...] = jnp.zeros_like(dcw_ref)
        dcw_ref[0:1, :] = jnp.sum(dout * y2, axis=0, keepdims=True)
        dcw_ref[1:2, :] = jnp.sum(dout * y1, axis=0, keepdims=True)
        dcw_ref[2:3, :] = jnp.sum(dout * y, axis=0, keepdims=True)
        dy = cw[2:3, :] * dout + cw[1:2, :] * _shift_up(dout, 1) + cw[0:1, :] * _shift_up(dout, 2)
        dzc_ref[...] = jnp.concatenate([dyc * out, dy * xh, dy * cg], axis=1)

    return pl.pallas_call(
        body, name="poolconv_bwd", grid=(1,),
        in_specs=[pl.BlockSpec((s, W_GRP), lambda i: (0, 2)), pl.BlockSpec((s, 3 * W_GRP), lambda i: (0, 1)),
                  _full((s, W_GRP)), _full((s, W_GRP)), _full((W_GRP, 64)), _full((1, W_GRP)), _full((3, W_GRP))],
        out_specs=[_full((s, W_GRP)), _full((s, 3 * W_GRP)), _full((W_GRP, W_GRP)), _full((1, W_GRP)), _full((8, W_GRP))],
        out_shape=[jax.ShapeDtypeStruct((s, W_GRP), F32), jax.ShapeDtypeStruct((s, 3 * W_GRP), F32),
                   jax.ShapeDtypeStruct((W_GRP, W_GRP), F32), jax.ShapeDtypeStruct((1, W_GRP), F32),
                   jax.ShapeDtypeStruct((8, W_GRP), F32)],
        compiler_params=_cp(dimension_semantics=("arbitrary",)),
    )(z, z, dyb, dyc, pw2, pscale, cw)


N_STATE = 1024
HALF_STATE = N_STATE // 2
HALF_CH = W_GRP // 2
N_SLAB = HALF_STATE // LANES


def _s5_disc(lre, lim, ldt):
    dt = jnp.exp(ldt)
    mag = jnp.exp(lre * dt)
    ang = lim * dt
    ar, ai = mag * jnp.cos(ang), mag * jnp.sin(ang)
    nr, ni = ar - 1.0, ai
    den = lre * lre + lim * lim
    kr = (nr * lre + ni * lim) / den
    ki = (ni * lre - nr * lim) / den
    return ar, ai, kr, ki


def _s5_mats(colp, br, bi, cr, ci):
    _, _, kr, ki = _s5_disc(colp[:, 0:1], colp[:, 1:2], colp[:, 2:3])
    bbr = kr * br - ki * bi
    bbi = kr * bi + ki * br
    bmask = (_iota((HALF_STATE, HALF_CH), 0) >> 6) == (_iota((HALF_STATE, HALF_CH), 1) >> 4)
    cmask = (_iota((HALF_CH, HALF_STATE), 0) >> 4) == (_iota((HALF_CH, HALF_STATE), 1) >> 6)
    btr = jnp.where(bmask, jnp.tile(bbr, (1, 8)), 0.0).astype(BF16)
    bti = jnp.where(bmask, jnp.tile(bbi, (1, 8)), 0.0).astype(BF16)
    ctr = jnp.where(cmask, jnp.tile(cr, (1, 8)), 0.0).astype(BF16)
    cti = jnp.where(cmask, jnp.tile(ci, (1, 8)), 0.0).astype(BF16)
    return kr, ki, btr, bti, ctr, cti, bmask, cmask


def _slab(q):
    return slice(q * LANES, (q + 1) * LANES)


def _cmul(ar, ai, br, bi):
    return ar * br - ai * bi, ar * bi + ai * br


def _sub_shift(x, k, up):
    row = _iota(x.shape, 0)
    if up:
        return jnp.where(row >= N_SEG - k, 0.0, pltpu.roll(x, N_SEG - k, 0))
    return jnp.where(row < k, 0.0, pltpu.roll(x, k, 0))


def _seg_rows(j):
    return pl.ds(pl.multiple_of(j * N_SEG, N_SEG), N_SEG)


def _interleave(src, dst, seg):
    def step(j, carry):
        dst[_seg_rows(j), :] = src[pl.ds(j, N_SEG, stride=seg), :]
        return carry
    lax.fori_loop(0, seg, step, 0)


def _deinterleave(src, dst, seg):
    def step(j, carry):
        dst[pl.ds(j, N_SEG, stride=seg), :] = src[_seg_rows(j), :]
        return carry
    lax.fori_loop(0, seg, step, 0)


def _scan(xr, xi, ar_row, ai_row, seg, reverse, states=None):
    nlog = int(math.log2(seg))
    assert (1 << nlog) == seg
    grads = []
    for q0 in range(0, N_SLAB, 4):
        qs = list(range(q0, q0 + 4))
        aq = [(jnp.broadcast_to(ar_row[:, _slab(q)], (N_SEG, LANES)),
               jnp.broadcast_to(ai_row[:, _slab(q)], (N_SEG, LANES))) for q in qs]
        zero = jnp.zeros((N_SEG, LANES), F32)

        def local(jj, carry, qs=qs, aq=aq):
            j = seg - 1 - jj if reverse else jj
            out = []
            for n, q in enumerate(qs):
                rows = _seg_rows(j)
                pr, pi = _cmul(aq[n][0], aq[n][1], carry[2 * n], carry[2 * n + 1])
                nr = pr + xr[q, rows, :]
                ni = pi + xi[q, rows, :]
                xr[q, rows, :] = nr
                xi[q, rows, :] = ni
                out += [nr, ni]
            return tuple(out)

        fin = lax.fori_loop(0, seg, local, (zero,) * 8)
        cins = []
        for n in range(4):
            er, ei = fin[2 * n], fin[2 * n + 1]
            pr, pi = aq[n]
            for _ in range(nlog):
                pr, pi = _cmul(pr, pi, pr, pi)
            yr, yi = er, ei
            for k in (1, 2, 4):
                sr, si = _cmul(pr, pi, _sub_shift(yr, k, reverse), _sub_shift(yi, k, reverse))
                yr, yi = yr + sr, yi + si
                pr, pi = _cmul(pr, pi, pr, pi)
            cins.append((_sub_shift(yr, 1, reverse), _sub_shift(yi, 1, reverse)))

        def fix(jj, carry, qs=qs, aq=aq, cins=cins):
            j = seg - 1 - jj if reverse else jj
            out, sums = [], []
            for n, q in enumerate(qs):
                rows = _seg_rows(j)
                pwr, pwi = carry[2 * n], carry[2 * n + 1]
                cr, ci = _cmul(pwr, pwi, cins[n][0], cins[n][1])
                v_r, v_i = xr[q, rows, :] + cr, xi[q, rows, :] + ci
                xr[q, rows, :] = v_r
                xi[q, rows, :] = v_i
                nr, ni = _cmul(pwr, pwi, aq[n][0], aq[n][1])
                out += [nr, ni]
                if states is not None:
                    prev = _seg_rows(j - 1)
                    p_r, p_i = states[0][q, prev, :], states[1][q, prev, :]
                    sums += [carry[8 + 2 * n] + v_r * p_r + v_i * p_i, carry[9 + 2 * n] - v_r * p_i + v_i * p_r]
            return tuple(out + sums)

        powers = tuple(v for pair in aq for v in pair)
        if states is None:
            lax.fori_loop(0, seg, fix, powers)
            continue
        assert reverse
        fix_last = lax.fori_loop(0, seg - 1, fix, powers + (zero,) * 8)
        first = _seg_rows(0)
        for n, q in enumerate(qs):
            cr, ci = _cmul(fix_last[2 * n], fix_last[2 * n + 1], cins[n][0], cins[n][1])
            v_r, v_i = xr[q, first, :] + cr, xi[q, first, :] + ci
            xr[q, first, :] = v_r
            xi[q, first, :] = v_i
            p_r = _sub_shift(states[0][q, _seg_rows(seg - 1), :], 1, False)
            p_i = _sub_shift(states[1][q, _seg_rows(seg - 1), :], 1, False)
            grads.append((jnp.sum(fix_last[8 + 2 * n] + v_r * p_r + v_i * p_i, axis=0, keepdims=True),
                          jnp.sum(fix_last[9 + 2 * n] - v_r * p_i + v_i * p_r, axis=0, keepdims=True)))
    return grads


def _s5_forward_states(u, btr, bti, ar_row, ai_row, xr, xi, seg):
    ub = u.astype(BF16)
    for q in range(N_SLAB):
        xr[q] = _dot_nt(ub, btr[_slab(q), :])
        xi[q] = _dot_nt(ub, bti[_slab(q), :])
    _scan(xr, xi, ar_row, ai_row, seg, False)


def _s5_readout(u, xr, xi, ctr, cti, d):
    y = d * u
    for q in range(N_SLAB):
        y = y + _dot_nt(xr[q], ctr[:, _slab(q)]) - _dot_nt(xi[q], cti[:, _slab(q)])
    return y


def _s5_param_specs():
    return [pl.BlockSpec((3, HALF_STATE), lambda i: (0, i)), pl.BlockSpec((HALF_STATE, 3), lambda i: (i, 0)),
            pl.BlockSpec((HALF_STATE, 16), lambda i: (i, 0)), pl.BlockSpec((HALF_STATE, 16), lambda i: (i, 0)),
            pl.BlockSpec((HALF_CH, 64), lambda i: (i, 0)), pl.BlockSpec((HALF_CH, 64), lambda i: (i, 0)),
            pl.BlockSpec((1, HALF_CH), lambda i: (0, i))]


def _s5_core_fwd(z, sp):
    s = z.shape[0]
    seg = s // N_SEG

    def body(u_ref, rowp, colp, br, bi, cr, ci, d_ref, y_ref, xr, xi, us, ys):
        ar, ai, _, _ = _s5_disc(rowp[0:1, :], rowp[1:2, :], rowp[2:3, :])
        _, _, btr, bti, ctr, cti, _, _ = _s5_mats(colp[...], br[...], bi[...], cr[...], ci[...])
        _interleave(u_ref, us, seg)
        u = us[...]
        _s5_forward_states(u, btr, bti, ar, ai, xr, xi, seg)
        ys[...] = _s5_readout(u, xr, xi, ctr, cti, d_ref[...])
        _deinterleave(ys, y_ref, seg)

    return pl.pallas_call(
        body, name="s5_core_fwd", grid=(2,),
        in_specs=[pl.BlockSpec((s, HALF_CH), lambda i: (0, 12 + i))] + _s5_param_specs(),
        out_specs=pl.BlockSpec((s, HALF_CH), lambda i: (0, i)),
        out_shape=jax.ShapeDtypeStruct((s, W_GRP), F32),
        scratch_shapes=[pltpu.VMEM((N_SLAB, s, LANES), F32)] * 2 + [pltpu.VMEM((s, HALF_CH), F32)] * 2,
        compiler_params=_cp(dimension_semantics=("parallel",)),
    )(z, *sp)


def _s5_glu_fwd(y, gw, gb):
    s = y.shape[0]
    tm = _tm(s)

    def body(y_ref, gw_ref, gb_ref, o_ref):
        yg, _ = _gelu(y_ref[...])
        o_ref[...] = yg * jax.nn.sigmoid(_dot(yg, gw_ref[...]) + gb_ref[...])

    blk = pl.BlockSpec((tm, W_GRP), lambda i: (i, 0))
    return pl.pallas_call(
        body, name="s5_glu_fwd", grid=(s // tm,),
        in_specs=[blk, _full((W_GRP, W_GRP)), _full((1, W_GRP))], out_specs=blk,
        out_shape=jax.ShapeDtypeStruct((s, W_GRP), F32),
        compiler_params=_cp(dimension_semantics=("parallel",)),
    )(y, gw, gb)


def _s5_glu_bwd(y, dyd, gw, gb):
    s = y.shape[0]
    tm = _tm(s)

    def body(y_ref, dyd_ref, gw_ref, gb_ref, dy_ref, dgw_ref, dgb_ref):
        i = pl.program_id(0)

        @pl.when(i == 0)
        def _():
            dgw_ref[...] = jnp.zeros_like(dgw_ref)
            dgb_ref[...] = jnp.zeros_like(dgb_ref)

        y, gw, dyd = y_ref[...], gw_ref[...], dyd_ref[...]
        yg, t = _gelu(y)
        gate = jax.nn.sigmoid(_dot(yg, gw) + gb_ref[...])
        dlin = dyd * yg * gate * (1.0 - gate)
        dgw_ref[...] += _dot_tn(yg, dlin)
        dgb_ref[...] += jnp.sum(dlin, axis=0, keepdims=True)
        dy_ref[...] = (dyd * gate + _dot_nt(dlin, gw)) * _gelu_grad(y, t)

    blk = pl.BlockSpec((tm, W_GRP), lambda i: (i, 0))
    return pl.pallas_call(
        body, name="s5_glu_bwd", grid=(s // tm,),
        in_specs=[blk, blk, _full((W_GRP, W_GRP)), _full((1, W_GRP))],
        out_specs=[blk, _full((W_GRP, W_GRP)), _full((1, W_GRP))],
        out_shape=[jax.ShapeDtypeStruct((s, W_GRP), F32), jax.ShapeDtypeStruct((W_GRP, W_GRP), F32),
                   jax.ShapeDtypeStruct((1, W_GRP), F32)],
        compiler_params=_cp(dimension_semantics=("arbitrary",)),
    )(y, dyd, gw, gb)


def _s5_core_bwd(z, dy, sp):
    s = z.shape[0]
    seg = s // N_SEG

    def body(u_ref, dy_ref, rowp, colp, br_ref, bi_ref, cr_ref, ci_ref, d_ref,
             du_ref, dbr_ref, dbi_ref, dcr_ref, dci_ref, dd_ref, da_ref, dk_ref,
             xr, xi, gr, gi, us, dys):
        ar, ai, _, _ = _s5_disc(rowp[0:1, :], rowp[1:2, :], rowp[2:3, :])
        br, bi = br_ref[...], bi_ref[...]
        kr, ki, btr, bti, ctr, cti, bmask, cmask = _s5_mats(colp[...], br, bi, cr_ref[...], ci_ref[...])
        _interleave(u_ref, us, seg)
        _interleave(dy_ref, dys, seg)
        u = us[...]
        d = d_ref[...]
        _s5_forward_states(u, btr, bti, ar, ai, xr, xi, seg)

        dy = dys[...]
        dd_ref[...] = jnp.sum(dy * u, axis=0, keepdims=True)
        du = d * dy
        dyb = dy.astype(BF16)
        dctr, dcti = [], []
        for q in range(N_SLAB):
            gr[q] = jnp.dot(dyb, ctr[:, _slab(q)], preferred_element_type=F32)
            gi[q] = -jnp.dot(dyb, cti[:, _slab(q)], preferred_element_type=F32)
            dctr.append(_dot_tn(dyb, xr[q]))
            dcti.append(-_dot_tn(dyb, xi[q]))
        selp = ((_iota((HALF_STATE, 64), 0) & 63) == _iota((HALF_STATE, 64), 1)).astype(F32)
        dcr_ref[...] = _dot_hi(jnp.where(cmask, jnp.concatenate(dctr, axis=1), 0.0), selp)
        dci_ref[...] = _dot_hi(jnp.where(cmask, jnp.concatenate(dcti, axis=1), 0.0), selp)

        da = _scan(gr, gi, ar, -ai, seg, True, states=(xr, xi))
        dar, dai = [p[0] for p in da], [p[1] for p in da]
        da_ref[...] = jnp.zeros_like(da_ref)
        da_ref[0:1, :] = jnp.concatenate(dar, axis=1)
        da_ref[1:2, :] = jnp.concatenate(dai, axis=1)

        ub = u.astype(BF16)
        dbtr, dbti = [], []
        for q in range(N_SLAB):
            g_r, g_i = gr[q].astype(BF16), gi[q].astype(BF16)
            du = du + jnp.dot(g_r, btr[_slab(q), :], preferred_element_type=F32) \
                + jnp.dot(g_i, bti[_slab(q), :], preferred_element_type=F32)
            dbtr.append(_dot_tn(g_r, ub))
            dbti.append(_dot_tn(g_i, ub))
        us[...] = du
        _deinterleave(us, du_ref, seg)
        selc =((_iota((HALF_CH, 16), 0) & 15) == _iota((HALF_CH, 16), 1)).astype(F32)
        dbbr = _dot_hi(jnp.where(bmask, jnp.concatenate(dbtr, axis=0), 0.0), selc)
        dbbi = _dot_hi(jnp.where(bmask, jnp.concatenate(dbti, axis=0), 0.0), selc)
        dbr_ref[...] = kr * dbbr + ki * dbbi
        dbi_ref[...] = kr * dbbi - ki * dbbr
        dk_ref[:, 0:1] = jnp.sum(dbbr * br + dbbi * bi, axis=1, keepdims=True)
        dk_ref[:, 1:2] = jnp.sum(dbbi * br - dbbr * bi, axis=1, keepdims=True)

    half = pl.BlockSpec((s, HALF_CH), lambda i: (0, i))
    return pl.pallas_call(
        body, name="s5_core_bwd", grid=(2,),
        in_specs=[pl.BlockSpec((s, HALF_CH), lambda i: (0, 12 + i)), half] + _s5_param_specs(),
        out_specs=[half, pl.BlockSpec((HALF_STATE, 16), lambda i: (i, 0)), pl.BlockSpec((HALF_STATE, 16), lambda i: (i, 0)),
                   pl.BlockSpec((HALF_CH, 64), lambda i: (i, 0)), pl.BlockSpec((HALF_CH, 64), lambda i: (i, 0)),
                   pl.BlockSpec((1, HALF_CH), lambda i: (0, i)), pl.BlockSpec((8, HALF_STATE), lambda i: (0, i)),
                   pl.BlockSpec((HALF_STATE, 2), lambda i: (i, 0))],
        out_shape=[jax.ShapeDtypeStruct((s, W_GRP), F32), jax.ShapeDtypeStruct((N_STATE, 16), F32),
                   jax.ShapeDtypeStruct((N_STATE, 16), F32), jax.ShapeDtypeStruct((W_GRP, 64), F32),
                   jax.ShapeDtypeStruct((W_GRP, 64), F32), jax.ShapeDtypeStruct((1, W_GRP), F32),
                   jax.ShapeDtypeStruct((8, N_STATE), F32), jax.ShapeDtypeStruct((N_STATE, 2), F32)],
        scratch_shapes=[pltpu.VMEM((N_SLAB, s, LANES), F32)] * 4 + [pltpu.VMEM((s, HALF_CH), F32)] * 2,
        compiler_params=_cp(dimension_semantics=("parallel",)),
    )(z, dy, *sp)


def _s5_param_bwd(lre, lim, ldt, da_r, da_i, dk_r, dk_i):
    n = lre.shape[0]

    def body(lre_ref, lim_ref, ldt_ref, dar_ref, dai_ref, dkr_ref, dki_ref, o_re, o_im, o_dt):
        lre, lim, ldt = lre_ref[...], lim_ref[...], ldt_ref[...]
        dt = jnp.exp(ldt)
        ar, ai, kr, ki = _s5_disc(lre, lim, ldt)
        mag = jnp.exp(lre * dt)
        den = lre * lre + lim * lim
        dkr, dki = dkr_ref[...], dki_ref[...]
        nr, ni = ar - 1.0, ai
        d_ar = dar_ref[...] + (dkr * lre - dki * lim) / den
        d_ai = dai_ref[...] + (dkr * lim + dki * lre) / den
        kk = (kr * dkr + ki * dki) * 2.0 / den
        d_lre = (dkr * nr + dki * ni) / den - kk * lre
        d_lim = (dkr * ni - dki * nr) / den - kk * lim
        d_mag = (d_ar * ar + d_ai * ai) / mag
        d_ang = d_ai * ar - d_ar * ai
        o_re[...] = d_lre + d_mag * mag * dt
        o_im[...] = d_lim + d_ang * dt
        o_dt[...] = jnp.sum((d_mag * mag * lre + d_ang * lim) * dt, axis=1, keepdims=True)

    return pl.pallas_call(
        body, name="s5_param_bwd",
        out_shape=[jax.ShapeDtypeStruct((n, 64), F32), jax.ShapeDtypeStruct((n, 64), F32),
                   jax.ShapeDtypeStruct((n, 1), F32)],
    )(lre, lim, ldt, da_r, da_i, dk_r, dk_i)


def _loss_head(x, fg, target):
    s, d = x.shape
    tm = _tm(s)

    def body(x_ref, fg_ref, t_ref, loss_ref, dx_ref, dfg_ref):
        i = pl.program_id(0)

        @pl.when(i == 0)
        def _():
            loss_ref[...] = jnp.zeros_like(loss_ref)
            dfg_ref[...] = jnp.zeros_like(dfg_ref)

        xv, g = x_ref[...], fg_ref[...]
        r = lax.rsqrt(jnp.mean(xv * xv, axis=-1, keepdims=True) + EPS)
        xh = xv * r
        err = xh * g - t_ref[...]
        loss_ref[...] += 0.5 * jnp.sum(jnp.mean(err * err, axis=-1, keepdims=True), axis=0, keepdims=True)
        dy = err * (1.0 / d)
        dfg_ref[...] += jnp.sum(dy * xh, axis=0, keepdims=True)
        dxh = dy * g
        dx_ref[...] = r * (dxh - xh * jnp.mean(dxh * xh, axis=-1, keepdims=True))

    row = pl.BlockSpec((tm, d), lambda i: (i, 0))
    return pl.pallas_call(
        body, name="loss_head", grid=(s // tm,),
        in_specs=[row, _full((1, d)), row], out_specs=[_full((1, 1)), row, _full((1, d))],
        out_shape=[jax.ShapeDtypeStruct((1, 1), F32), jax.ShapeDtypeStruct((s, d), F32),
                   jax.ShapeDtypeStruct((1, d), F32)],
        compiler_params=_cp(dimension_semantics=("arbitrary",)),
    )(x, fg, target)


ADA_TN = 384


def _cond_fwd(cact, ada_w, ada_b_loc):
    nl, d, n = ada_w.shape

    def body(c_ref, w_ref, b_ref, o_ref):
        o_ref[...] = _dot(c_ref[...], w_ref[...]) + b_ref[...]

    return pl.pallas_call(
        body, name="cond_fwd", grid=(nl, n // ADA_TN),
        in_specs=[_full((N_DEV, d)), pl.BlockSpec((None, d, ADA_TN), lambda l, j: (l, 0, j)),
                  pl.BlockSpec((None, 1, ADA_TN), lambda l, j: (l, 0, j))],
        out_specs=pl.BlockSpec((None, N_DEV, ADA_TN), lambda l, j: (l, 0, j)),
        out_shape=jax.ShapeDtypeStruct((nl, N_DEV, n), F32),
        compiler_params=_cp(dimension_semantics=("parallel", "parallel")),
    )(cact, ada_w, ada_b_loc)


ELEMENTWISE_BLOCK_BYTES = 1 << 20


def _row_tile(r, c, itemsize=4):
    best = None
    for t in range(8, r + 1, 8):
        if r % t == 0 and t * c * itemsize <= ELEMENTWISE_BLOCK_BYTES:
            best = t
    return best if best is not None else r


def _adamw_math(w, g, m, v):
    m = ADAM_B1 * m + (1.0 - ADAM_B1) * g
    v = ADAM_B2 * v + (1.0 - ADAM_B2) * (g * g)
    m_hat = m / (1.0 - ADAM_B1 ** ADAM_STEP)
    v_hat = v / (1.0 - ADAM_B2 ** ADAM_STEP)
    delta = -ADAM_LR * (m_hat / (jnp.sqrt(v_hat) + ADAM_EPS) + ADAM_WD * w)
    return delta, m, v


def _ada_w_update(cact, dcond_loc, w, m, v):
    nl, d, n = w.shape

    def body(c_ref, dc_ref, w_ref, m_ref, v_ref, g_out, d_out, m_out, v_out):
        g = _dot_tn(c_ref[...], dc_ref[...])
        g_out[...] = g
        d_out[...], m_out[...], v_out[...] = _adamw_math(w_ref[...], g, m_ref[...], v_ref[...])

    blk = pl.BlockSpec((None, d, ADA_TN), lambda l, j: (l, 0, j))
    return pl.pallas_call(
        body, name="ada_w_update", grid=(nl, n // ADA_TN),
        in_specs=[_full((N_DEV, d)), pl.BlockSpec((None, N_DEV, ADA_TN), lambda l, j: (l, 0, j)), blk, blk, blk],
        out_specs=[blk] * 4, out_shape=[jax.ShapeDtypeStruct((nl, d, n), F32)] * 4,
        compiler_params=_cp(dimension_semantics=("parallel", "parallel")),
    )(cact, dcond_loc, w, m, v)


def _place():
    x, y, c = lax.axis_index("x"), lax.axis_index("y"), lax.axis_index("c")
    chips = [(1 - x, y), (x, 1 - y), (1 - x, 1 - y)]
    return x, y, c, chips


def _remote(src, dst, send_sem, recv_sem, to):
    return pltpu.make_async_remote_copy(src_ref=src, dst_ref=dst, send_sem=send_sem, recv_sem=recv_sem,
                                        device_id=to, device_id_type=MESH_ID)


def _sems(n):
    return [pltpu.SemaphoreType.DMA((n,)), pltpu.SemaphoreType.DMA((n,))]


def _all_gather8(v, name):
    r, cdim = v.shape

    def body(x_ref, out_ref, stage, send_sems, recv_sems):
        x, y, c, chips = _place()
        sibling = (x, y, 1 - c)

        def slot(px, py, pc):
            return out_ref.at[4 * px + 2 * py + pc]

        first = [_remote(x_ref, slot(x, y, c), send_sems.at[0], recv_sems.at[0], sibling)]
        first += [_remote(x_ref, slot(x, y, c), send_sems.at[1 + j], recv_sems.at[1 + j], (*chip, c))
                  for j, chip in enumerate(chips)]
        for cp in first:
            cp.start()
        pltpu.sync_copy(x_ref, stage)
        pltpu.sync_copy(stage, slot(x, y, c))
        passed = []
        for j, chip in enumerate(chips):
            blk = slot(*chip, c)
            _remote(blk, blk, send_sems.at[1 + j], recv_sems.at[1 + j], (x, y, c)).wait_recv()
            fw = _remote(blk, blk, send_sems.at[4 + j], recv_sems.at[4 + j], sibling)
            fw.start()
            passed.append(fw)
        blk = slot(x, y, 1 - c)
        _remote(blk, blk, send_sems.at[0], recv_sems.at[0], (x, y, c)).wait_recv()
        for j, chip in enumerate(chips):
            blk = slot(*chip, 1 - c)
            _remote(blk, blk, send_sems.at[4 + j], recv_sems.at[4 + j], (x, y, c)).wait_recv()
        for cp in first + passed:
            cp.wait_send()

    return pl.pallas_call(
        body, name=name, out_shape=jax.ShapeDtypeStruct((N_DEV, r, cdim), v.dtype),
        in_specs=[ANY], out_specs=ANY,
        scratch_shapes=[pltpu.VMEM((r, cdim), v.dtype)] + _sems(7),
        compiler_params=_cp(),
    )(v)


def _gather_first_copies():
    def make(refs, send_sems, recv_sems):
        x, y, c, chips = _place()
        mine = refs[0].at[4 * x + 2 * y + c]
        to = [(x, y, 1 - c)] + [(*chip, c) for chip in chips]
        return [_remote(mine, mine, send_sems.at[k], recv_sems.at[k], dev) for k, dev in enumerate(to)]
    return make


def _gather_pass_on(buf, name):
    def body(in_ref, out_ref, send_sems, recv_sems):
        x, y, c, chips = _place()
        passed = []
        for j, chip in enumerate(chips):
            blk = out_ref.at[4 * chip[0] + 2 * chip[1] + c]
            fw = _remote(blk, blk, send_sems.at[j], recv_sems.at[j], (x, y, 1 - c))
            fw.start()
            passed.append(fw)
        for j, chip in enumerate(chips):
            blk = out_ref.at[4 * chip[0] + 2 * chip[1] + 1 - c]
            _remote(blk, blk, send_sems.at[j], recv_sems.at[j], (x, y, c)).wait_recv()
        for fw in passed:
            fw.wait_send()

    return pl.pallas_call(
        body, name=name, out_shape=jax.ShapeDtypeStruct(buf.shape, buf.dtype),
        in_specs=[ANY], out_specs=ANY, input_output_aliases={0: 0}, scratch_shapes=_sems(3),
    )(buf)


def _place_weights(ws, layer, kidx, after):
    steps = 4
    shapes, in_specs, out_specs = [], [], []
    for w, kind in zip(ws, BIG_KINDS):
        _, a, b = w.shape
        in_specs.append(pl.BlockSpec((None, a // steps, b), lambda i, k: (layer, i, 0)))
        if kind == "col":
            shapes.append((2, a, 2 * b))
            out_specs.append(pl.BlockSpec((None, a // steps, b), lambda i, k: (k[0] // 2, i, k[0] % 2)))
        else:
            shapes.append((N_CHIP, a, b))
            out_specs.append(pl.BlockSpec((None, a // steps, b), lambda i, k: (k[0], i, 0)))

    def body(k_ref, *refs):
        outs = refs[len(ws) + 1:]
        for t in range(len(ws)):
            outs[t][...] = refs[t][...].astype(BF16)

    return pl.pallas_call(
        body, name="place_weights", out_shape=[jax.ShapeDtypeStruct(s, BF16) for s in shapes],
        grid_spec=pltpu.PrefetchScalarGridSpec(num_scalar_prefetch=1, grid=(steps,), in_specs=in_specs + [ANY],
                                               out_specs=out_specs),
        compiler_params=_cp(dimension_semantics=("parallel",)),
    )(kidx, *ws, after)


HBM = pl.BlockSpec(memory_space=pltpu.HBM)
SEM = pl.BlockSpec(memory_space=pltpu.SEMAPHORE)
EFFECT = pltpu.SideEffectType.DATAFLOW_SIDE_EFFECTING


def _weight_block(ref, kind, k, h):
    if kind == "col":
        ncol = ref.shape[3] // 2
        return ref.at[k // 2, h, :, pl.ds(pl.multiple_of((k % 2) * ncol, LANES), ncol)]
    return ref.at[k, h]


def _in_hbm(a):
    return pltpu.with_memory_space_constraint(a, pltpu.HBM)


def _weight_send_start(placed, kinds, name):
    nt = len(placed)

    def body(*refs):
        send_sems, recv_sems = refs[nt], refs[nt + 1]
        dst = refs[nt + 2:2 * nt + 2]
        token = refs[2 * nt + 2]
        x, y, c, chips = _place()
        kme = 2 * x + y
        for t in range(nt):
            for j, chip in enumerate(chips):
                own = _weight_block(dst[t], kinds[t], kme, c)
                _remote(own, own, send_sems.at[3 * t + j], recv_sems.at[3 * t + j], (*chip, c)).start()
        token[...] = jnp.zeros_like(token)

    return pl.pallas_call(
        body, name=name,
        out_shape=(pltpu.SemaphoreType.DMA((3 * nt,)), pltpu.SemaphoreType.DMA((3 * nt,)),
                   *[pltpu.HBM(a.shape, a.dtype) for a in placed], jax.ShapeDtypeStruct((8, LANES), F32)),
        in_specs=[HBM] * nt, out_specs=(SEM, SEM, *[HBM] * nt, pl.BlockSpec(memory_space=pltpu.VMEM)),
        input_output_aliases={t: 2 + t for t in range(nt)},
        compiler_params=pltpu.CompilerParams(has_side_effects=EFFECT),
    )(*[_in_hbm(a) for a in placed])


def _weight_send_wait(send_sems, recv_sems, arrays, kinds, after, name):
    nt = len(arrays)

    def body(*refs):
        arr = refs[:nt]
        send_sems, recv_sems = refs[nt], refs[nt + 1]
        x, y, c, chips = _place()
        kme = 2 * x + y
        for t in range(nt):
            for j, chip in enumerate(chips):
                own = _weight_block(arr[t], kinds[t], kme, c)
                got = _weight_block(arr[t], kinds[t], 2 * chip[0] + chip[1], c)
                cp = _remote(own, got, send_sems.at[3 * t + j], recv_sems.at[3 * t + j], (*chip, c))
                cp.wait_send()
                cp.wait_recv()

    return pl.pallas_call(
        body, name=name, out_shape=[pltpu.HBM(a.shape, a.dtype) for a in arrays],
        in_specs=[HBM] * nt + [SEM, SEM, ANY], out_specs=[HBM] * nt,
        input_output_aliases={t: t for t in range(nt)},
        compiler_params=pltpu.CompilerParams(has_side_effects=EFFECT),
    )(*arrays, send_sems, recv_sems, after)


def _forward_copies(kinds):
    def make(refs, send_sems, recv_sems):
        x, y, c, chips = _place()
        cps = []
        for t in range(len(kinds)):
            for j, chip in enumerate(chips):
                blk = _weight_block(refs[t], kinds[t], 2 * chip[0] + chip[1], c)
                cps.append(_remote(blk, blk, send_sems.at[3 * t + j], recv_sems.at[3 * t + j], (x, y, 1 - c)))
        return cps
    return make


def _split_start(name, arrays, n_copies, make_copies):
    na = len(arrays)

    def body(*refs):
        send_sems, recv_sems = refs[na], refs[na + 1]
        for cp in make_copies(refs[na + 2:2 * na + 2], send_sems, recv_sems):
            cp.start()
        token = refs[2 * na + 2]
        token[...] = jnp.zeros_like(token)

    return pl.pallas_call(
        body, name=name,
        out_shape=(pltpu.SemaphoreType.DMA((n_copies,)), pltpu.SemaphoreType.DMA((n_copies,)),
                   *[pltpu.HBM(a.shape, a.dtype) for a in arrays], jax.ShapeDtypeStruct((8, LANES), F32)),
        in_specs=[HBM] * na, out_specs=(SEM, SEM, *[HBM] * na, pl.BlockSpec(memory_space=pltpu.VMEM)),
        input_output_aliases={t: 2 + t for t in range(na)},
        compiler_params=pltpu.CompilerParams(has_side_effects=EFFECT),
    )(*[_in_hbm(a) for a in arrays])


def _split_wait(name, started, make_copies, after):
    send_sems, recv_sems, *arrays, _ = started
    na = len(arrays)

    def body(*refs):
        send_sems, recv_sems = refs[na], refs[na + 1]
        for cp in make_copies(refs[:na], send_sems, recv_sems):
            cp.wait_send()
            cp.wait_recv()

    return pl.pallas_call(
        body, name=name, out_shape=[pltpu.HBM(a.shape, a.dtype) for a in arrays],
        in_specs=[HBM] * na + [SEM, SEM, ANY], out_specs=[HBM] * na,
        input_output_aliases={t: t for t in range(na)},
        compiler_params=pltpu.CompilerParams(has_side_effects=EFFECT),
    )(*arrays, send_sems, recv_sems, after)


def _exchange_copies(nt):
    def make(refs, send_sems, recv_sems):
        x, y, c, _ = _place()
        return [_remote(refs[t].at[:, 1 - c], refs[nt + t], send_sems.at[t], recv_sems.at[t], (x, y, 1 - c))
                for t in range(nt)]
    return make


def _sibling_exchange_start(views, name):
    lands = [lax.empty((v.shape[0],) + v.shape[2:], v.dtype) for v in views]
    return _split_start(name, list(views) + lands, len(views), _exchange_copies(len(views)))


def _sibling_exchange_wait(started, after, name):
    nt = (len(started) - 3) // 2
    outs = _split_wait(name, started, _exchange_copies(nt), after)
    return outs[:nt], outs[nt:]


def _scatter_copies(src, land, kinds, send_sems, recv_sems):
    x, y, c, chips = _place()
    cps = []
    for t in range(len(src)):
        for j, chip in enumerate(chips):
            k = 2 * chip[0] + chip[1]
            if kinds[t] == "col":
                ncol = land[t].shape[2]
                win = src[t].at[k // 2, :, pl.ds(pl.multiple_of((k % 2) * ncol, LANES), ncol)]
            else:
                win = src[t].at[k]
            cps.append(_remote(win, land[t].at[j], send_sems.at[3 * t + j], recv_sems.at[3 * t + j], (*chip, c)))
    return cps


def _chip_scatter_start(parts, kinds, name):
    nt = len(parts)
    shapes = []
    for p, kind in zip(parts, kinds):
        shapes.append((3, p.shape[1], p.shape[2] // 2) if kind == "col" else (3,) + p.shape[1:])

    def body(*refs):
        send_sems, recv_sems = refs[2 * nt], refs[2 * nt + 1]
        src, land = refs[2 * nt + 2:3 * nt + 2], refs[3 * nt + 2:4 * nt + 2]
        token = refs[4 * nt + 2]
        for cp in _scatter_copies(src, land, kinds, send_sems, recv_sems):
            cp.start()
        token[...] = jnp.zeros_like(token)

    lands = [lax.empty(s, BF16) for s in shapes]
    return pl.pallas_call(
        body, name=name,
        out_shape=(pltpu.SemaphoreType.DMA((3 * nt,)), pltpu.SemaphoreType.DMA((3 * nt,)),
                   *[pltpu.HBM(a.shape, a.dtype) for a in parts], *[pltpu.HBM(s, BF16) for s in shapes],
                   jax.ShapeDtypeStruct((8, LANES), F32)),
        in_specs=[HBM] * (2 * nt), out_specs=(SEM, SEM, *[HBM] * (2 * nt), pl.BlockSpec(memory_space=pltpu.VMEM)),
        input_output_aliases={t: 2 + t for t in range(2 * nt)},
        compiler_params=pltpu.CompilerParams(has_side_effects=EFFECT),
    )(*[_in_hbm(a) for a in parts], *[_in_hbm(a) for a in lands])


def _chip_scatter_wait(send_sems, recv_sems, parts, lands, kinds, after, name):
    nt = len(parts)

    def body(*refs):
        src, land = refs[:nt], refs[nt:2 * nt]
        send_sems, recv_sems = refs[2 * nt], refs[2 * nt + 1]
        for cp in _scatter_copies(src, land, kinds, send_sems, recv_sems):
            cp.wait_send()
            cp.wait_recv()

    outs = pl.pallas_call(
        body, name=name, out_shape=[pltpu.HBM(a.shape, a.dtype) for a in list(parts) + list(lands)],
        in_specs=[HBM] * (2 * nt) + [SEM, SEM, ANY], out_specs=[HBM] * (2 * nt),
        input_output_aliases={t: t for t in range(2 * nt)},
        compiler_params=pltpu.CompilerParams(has_side_effects=EFFECT),
    )(*parts, *lands, send_sems, recv_sems, after)
    return outs[:nt], outs[nt:]


def _share_copies(nt):
    def make(refs, send_sems, recv_sems):
        x, y, c, _ = _place()
        return [_remote(refs[t].at[c], refs[t].at[c], send_sems.at[t], recv_sems.at[t], (x, y, 1 - c))
                for t in range(nt)]
    return make


def _sibling_share_start(fulls, name):
    return _split_start(name, list(fulls), len(fulls), _share_copies(len(fulls)))


def _sibling_share_wait(started, after, name):
    return _split_wait(name, started, _share_copies(len(started) - 3), after)


SUM_STEPS = 4


def _pair_sum(views, lands, ck):
    nt = len(views)
    in_specs, out_specs, shapes = [], [], []
    for v in views:
        b, _, r, cc = v.shape
        per = SUM_STEPS // b
        tr = r // per
        in_specs.append(pl.BlockSpec((None, None, tr, cc), lambda i, s, per=per: (i // per, s[0], i % per, 0)))
        out_specs.append(pl.BlockSpec((None, tr, cc), lambda i, s, per=per: (i // per, i % per, 0)))
        shapes.append((b, r, cc))
    in_specs = in_specs + out_specs

    def body(s_ref, *refs):
        for t in range(nt):
            refs[2 * nt + t][...] = (refs[t][...].astype(F32) + refs[nt + t][...].astype(F32)).astype(BF16)

    return pl.pallas_call(
        body, name="grad_pair_sum", out_shape=[jax.ShapeDtypeStruct(s, BF16) for s in shapes],
        grid_spec=pltpu.PrefetchScalarGridSpec(num_scalar_prefetch=1, grid=(SUM_STEPS,), in_specs=in_specs,
                                               out_specs=out_specs),
        compiler_params=_cp(dimension_semantics=("parallel",)),
    )(ck, *views, *lands)


def _chip_sum(parts, lands, kinds, ck):
    nt = len(parts)
    steps = 2
    in_own, in_land, out_specs, shapes = [], [], [], []
    for ld, kind in zip(lands, kinds):
        _, r, cc = ld.shape
        tr = r // steps
        if kind == "col":
            in_own.append(pl.BlockSpec((None, tr, cc), lambda i, s: (s[1] // 2, i, s[1] % 2)))
        else:
            in_own.append(pl.BlockSpec((None, tr, cc), lambda i, s: (s[1], i, 0)))
        in_land.append(pl.BlockSpec((3, tr, cc), lambda i, s: (0, i, 0)))
        out_specs.append(pl.BlockSpec((None, tr, cc), lambda i, s: (s[0], i, 0)))
        shapes.append((2, r, cc))

    def body(s_ref, *refs):
        for t in range(nt):
            acc = refs[t][...].astype(F32)
            for j in range(3):
                acc = acc + refs[nt + t][j].astype(F32)
            refs[2 * nt + t][...] = acc

    return pl.pallas_call(
        body, name="grad_chip_sum", out_shape=[jax.ShapeDtypeStruct(s, F32) for s in shapes],
        grid_spec=pltpu.PrefetchScalarGridSpec(num_scalar_prefetch=1, grid=(steps,), in_specs=in_own + in_land,
                                               out_specs=out_specs),
        compiler_params=_cp(dimension_semantics=("parallel",)),
    )(ck, *parts, *lands)


def _sum8(g):
    _, r, cc = g.shape
    tr = _row_tile(r, N_DEV * cc)

    def body(g_ref, o_ref):
        acc = g_ref[0].astype(F32)
        for d in range(1, N_DEV):
            acc = acc + g_ref[d].astype(F32)
        o_ref[...] = acc

    return pl.pallas_call(
        body, name="small_grad_sum", grid=(r // tr,),
        in_specs=[pl.BlockSpec((N_DEV, tr, cc), lambda i: (0, i, 0))],
        out_specs=pl.BlockSpec((tr, cc), lambda i: (i, 0)),
        out_shape=jax.ShapeDtypeStruct((r, cc), F32),
        compiler_params=_cp(dimension_semantics=("parallel",)),
    )(g)


def _silu_rows(c):
    def body(c_ref, o_ref):
        v = c_ref[...]
        o_ref[...] = v * jax.nn.sigmoid(v)

    return pl.pallas_call(body, name="cond_silu", out_shape=jax.ShapeDtypeStruct(c.shape, F32))(c)


def _pack(arrays):
    rows = []
    for a in arrays:
        flat = a.reshape(-1)
        rows.append(jnp.pad(flat, (0, (-flat.shape[0]) % (8 * LANES))).reshape(-1, LANES))
    n = sum(r.shape[0] for r in rows)
    if n % 256:
        rows.append(jnp.zeros((256 - n % 256, LANES), rows[0].dtype))
    return jnp.concatenate(rows, axis=0)


def _unpack(packed, shapes):
    out, off = [], 0
    for s in shapes:
        n = math.prod(s)
        nr = 8 * -(-n // (8 * LANES))
        out.append(packed[off:off + nr].reshape(-1)[:n].reshape(s))
        off += nr
    return out


def _as_rows(a):
    return a.reshape(1, -1) if a.ndim == 1 else a.reshape(-1, a.shape[-1])


def _adamw_many(ws, gs, ms, vs, name, steps=1):
    nt = len(ws)

    def body(*refs):
        for t in range(nt):
            w_ref, g_ref, m_ref, v_ref = (refs[k * nt + t] for k in range(4))
            d, m, v = _adamw_math(w_ref[...], g_ref[...], m_ref[...], v_ref[...])
            refs[4 * nt + t][...] = d
            refs[5 * nt + t][...] = m
            refs[6 * nt + t][...] = v

    shapes = [jax.ShapeDtypeStruct(a.shape, F32) for a in ws]
    if steps == 1:
        outs = pl.pallas_call(body, name=name, out_shape=shapes * 3, compiler_params=_cp())(*ws, *gs, *ms, *vs)
    else:
        specs = [pl.BlockSpec((a.shape[0] // steps, a.shape[1]), lambda i: (i, 0)) for a in ws]
        outs = pl.pallas_call(
            body, name=name, grid=(steps,), in_specs=specs * 4, out_specs=specs * 3, out_shape=shapes * 3,
            compiler_params=_cp(dimension_semantics=("parallel",)),
        )(*ws, *gs, *ms, *vs)
    return outs[:nt], outs[nt:2 * nt], outs[2 * nt:]


def _exchange_big_grads(grads, kinds, layer):
    views = []
    for g, kind in zip(grads, kinds):
        if kind == "col":
            views.append(g.reshape(2, 2, g.shape[1] // 2, g.shape[2]))
        else:
            views.append(g.reshape(N_CHIP, 2, g.shape[0] // (2 * N_CHIP), g.shape[1]))
    return _sibling_exchange_start(views, "grad_exchange_start_%d" % layer)


def _scatter_big_grads(exchanged, kinds, ck, after, layer):
    views, lands = _sibling_exchange_wait(exchanged, after, "grad_exchange_wait_%d" % layer)
    parts = _pair_sum(views, lands, ck)
    return _chip_scatter_start(parts, kinds, "grad_scatter_start_%d" % layer)


def _finish_big_grads(started, kinds, ck, after, layer):
    nt = len(kinds)
    send_sems, recv_sems = started[0], started[1]
    parts, lands = started[2:2 + nt], started[2 + nt:2 + 2 * nt]
    parts, lands = _chip_scatter_wait(send_sems, recv_sems, parts, lands, kinds, after, "grad_scatter_wait_%d" % layer)
    return _sibling_share_start(_chip_sum(parts, lands, kinds, ck), "grad_share_start_%d" % layer)


def _adamw_layer(ws, gs, ms, vs, stacks, layer, name, steps):
    nt = len(ws)
    stacks = [s if s is not None else tuple(lax.empty(w.shape, F32) for _ in range(4)) for s, w in zip(stacks, ws)]

    def body(*refs):
        for t in range(nt):
            w_ref, g_ref, m_ref, v_ref = (refs[k * nt + t] for k in range(4))
            outs = refs[8 * nt + 4 * t:8 * nt + 4 * t + 4]
            g = g_ref[...]
            outs[0][...] = g
            outs[1][...], outs[2][...], outs[3][...] = _adamw_math(w_ref[...], g, m_ref[...], v_ref[...])

    in_specs, g_specs, out_specs = [], [], []
    for w in ws:
        _, r, c = w.shape
        in_specs.append(pl.BlockSpec((None, r // steps, c), lambda i: (layer, i, 0)))
        g_specs.append(pl.BlockSpec((r // steps, c), lambda i: (i, 0)))
        out_specs += [pl.BlockSpec((None, r // steps, c), lambda i: (layer, i, 0))] * 4
    in_specs = in_specs + g_specs + in_specs * 2 + [ANY] * (4 * nt)
    flat = [a for s in stacks for a in s]
    outs = pl.pallas_call(
        body, name=name, grid=(steps,), in_specs=in_specs, out_specs=out_specs,
        out_shape=[jax.ShapeDtypeStruct(a.shape, F32) for a in flat],
        input_output_aliases={4 * nt + k: k for k in range(4 * nt)},
        compiler_params=_cp(dimension_semantics=("parallel",)),
    )(*ws, *gs, *ms, *vs, *flat)
    return [tuple(outs[4 * t:4 * t + 4]) for t in range(nt)]


SMALL_NAMES = ["ada_b", "norm1_g", "norm2_g", "sgu_w", "sgu_b", "pool_w", "pool_scale", "conv_w", "s5_lambda_re",
               "s5_lambda_im", "s5_b_re", "s5_b_im", "s5_c_re", "s5_c_im", "s5_d", "s5_log_dt", "s5_glu_w", "s5_glu_b",
               "mix_norm_g", "norm3_g", "final_norm_g"]
BIG_NAMES = ["ffn1_w_in", "ffn1_w_out", "w_mix_in", "w_mix_out", "ffn2_w_in", "ffn2_w_out"]
BIG_KINDS = ["col", "row", "row", "row", "col", "row"]
WEIGHT_ORDER = ["ada_w", "ada_b", "norm1_g", "ffn1_w_in", "ffn1_w_out", "norm2_g", "w_mix_in", "sgu_w", "sgu_b", "pool_w",
                "pool_scale", "conv_w", "s5_lambda_re", "s5_lambda_im", "s5_b_re", "s5_b_im", "s5_c_re", "s5_c_im", "s5_d",
                "s5_log_dt", "s5_glu_w", "s5_glu_b", "mix_norm_g", "w_mix_out", "norm3_g", "ffn2_w_in", "ffn2_w_out",
                "final_norm_g"]


def _local_step(x, target, cond, fetch_weights, prefetch_weights, p, emit_grads):
    nl, d = DEPTH, x.shape[1]
    row = lambda a: a.reshape(1, -1)
    saved = []
    for l in range(nl):
        (wi1, wo1, wmit, wmo, wi2, wo2), tok = fetch_weights(l, x)
        cl = cond[l] + tok
        mod1, mod2, mod3 = cl[0:3], cl[3:6], cl[6:9]
        lre, lim = p["s5_lambda_re"][l].reshape(-1), p["s5_lambda_im"][l].reshape(-1)
        ldt = jnp.repeat(p["s5_log_dt"][l], 64)
        rowp = jnp.stack([lre, lim, ldt])
        sp = (rowp, rowp.T, p["s5_b_re"][l].reshape(N_STATE, 16), p["s5_b_im"][l].reshape(N_STATE, 16),
              p["s5_c_re"][l].reshape(W_GRP, 64), p["s5_c_im"][l].reshape(W_GRP, 64), row(p["s5_d"][l]))
        glu = (p["s5_glu_w"][l], row(p["s5_glu_b"][l]))
        bias_full = jnp.repeat(p["sgu_b"][l].T, 64, axis=1)
        pw2 = p["pool_w"][l].reshape(W_GRP, 64)
        x1, h1, a1, b1, o1 = _ffn_fwd(x, mod1, row(p["norm1_g"][l]), wi1, wo1)
        z, h2 = _mix_in_fwd(x1, mod2, row(p["norm2_g"][l]), wmit)
        ya = _sgu_fwd(z, p["sgu_w"][l], bias_full)
        yb, yc = _poolconv_fwd(z, pw2, row(p["pool_scale"][l]), p["conv_w"][l])
        ypre = _s5_core_fwd(z, sp)
        yd = _s5_glu_fwd(ypre, *glu)
        ys = (ya, yb, yc, yd)
        x2, m = _mix_out_fwd(ys, row(p["mix_norm_g"][l]), wmo, x1, mod2[2:3])
        mod3 = mod3 + prefetch_weights(l + 1, x2)
        x3, h3, a3, b3, o3 = _ffn_fwd(x2, mod3, row(p["norm3_g"][l]), wi2, wo2)
        saved.append((x, x1, x2, h1, a1, b1, o1, z, h2, ys, m, h3, a3, b3, o3, sp, bias_full, pw2, ypre, glu,
                      (wi1, wo1, wmit, wmo, wi2, wo2), cl))
        x = x3

    loss, dx, dfg = _loss_head(x, row(p["final_norm_g"]), target)

    sg = {n: [None] * nl for n in SMALL_NAMES if n not in ("ada_b", "final_norm_g")}
    dcond = [None] * nl
    s5_da, s5_dk = [None] * nl, [None] * nl
    tok = 0.0
    for l in reversed(range(nl)):
        (x0, x1, x2, h1, a1, b1, o1, z, h2, ys, m, h3, a3, b3, o3, sp, bias_full, pw2, ypre, glu,
         (wi1, wo1, wmit, wmo, wi2, wo2), cl) = saved[l]
        cl = cl + tok
        mod1, mod2, mod3 = cl[0:3], cl[3:6], cl[6:9]
        dza, dzb, dwi2, dwo2, dgate3 = _ffn_bwd_main(dx, o3, mod3[2:3], h3, a3, b3, wo2)
        dx, rows3 = _ffn_bwd_in(dza, dzb, wi2, x2, dx, mod3, row(p["norm3_g"][l]))
        outs = _mix_out_bwd(dx, m, mod2[2:3], ys, row(p["mix_norm_g"][l]), wmo)
        dys, dgate2, dmng, dwmo = outs[0:4], outs[4], outs[5], outs[6]
        dza_, dsw, dsb = _sgu_bwd(z, dys[0], p["sgu_w"][l], bias_full)
        dzb_, dzc_, dwbd, dps, dcw = _poolconv_bwd(z, dys[1], dys[2], pw2, row(p["pool_scale"][l]), p["conv_w"][l])
        dypre, dgw, dgb = _s5_glu_bwd(ypre, dys[3], *glu)
        dzd_, dbr, dbi, dcr, dci, dd, da, dk = _s5_core_bwd(z, dypre, sp)
        dx, rows2, dwmit = _mix_in_bwd((dza_, dzb_, dzc_, dzd_), h2, wmit, x1, dx, mod2, row(p["norm2_g"][l]))
        dza, dzb, dwi1, dwo1, dgate1 = _ffn_bwd_main(dx, o1, mod1[2:3], h1, a1, b1, wo1)
        tok, layer_done = emit_grads(l, [dwi1, dwo1, dwmit, dwmo, dwi2, dwo2])
        dx, rows1 = _ffn_bwd_in(dza, dzb, wi1, x0, dx, mod1 + tok, row(p["norm1_g"][l]))
        if l > 0:
            tok = layer_done(dx)[0, 0]
        dcond[l] = jnp.concatenate([rows1[0:2], dgate1, rows2[0:2], dgate2, rows3[0:2], dgate3], axis=0)
        sg["norm1_g"][l], sg["norm2_g"][l], sg["norm3_g"][l] = rows1[2], rows2[2], rows3[2]
        sg["mix_norm_g"][l] = dmng[0]
        sg["sgu_w"][l] = dsw
        sg["sgu_b"][l] = dsb[:, 0:4].T
        g4 = dwbd.reshape(4, 64, 4, 64)
        sg["pool_w"][l] = jnp.stack([g4[k, :, k, :] for k in range(4)])
        sg["pool_scale"][l] = dps[0]
        sg["conv_w"][l] = dcw[0:3]
        sg["s5_b_re"][l], sg["s5_b_im"][l] = dbr.reshape(16, 64, 16), dbi.reshape(16, 64, 16)
        sg["s5_c_re"][l], sg["s5_c_im"][l] = dcr.reshape(16, 16, 64), dci.reshape(16, 16, 64)
        sg["s5_d"][l] = dd[0]
        sg["s5_glu_w"][l], sg["s5_glu_b"][l] = dgw, dgb[0]
        s5_da[l], s5_dk[l] = da, dk

    n16 = nl * 16
    dlre, dlim, dldt = _s5_param_bwd(
        p["s5_lambda_re"].reshape(n16, 64), p["s5_lambda_im"].reshape(n16, 64),
        jnp.repeat(p["s5_log_dt"].reshape(n16, 1), 64, axis=1),
        jnp.stack([a[0] for a in s5_da]).reshape(n16, 64), jnp.stack([a[1] for a in s5_da]).reshape(n16, 64),
        jnp.stack([k[:, 0] for k in s5_dk]).reshape(n16, 64), jnp.stack([k[:, 1] for k in s5_dk]).reshape(n16, 64))
    small = {n: jnp.stack(v) for n, v in sg.items() if v[0] is not None}
    small["s5_lambda_re"] = dlre.reshape(nl, 16, 64)
    small["s5_lambda_im"] = dlim.reshape(nl, 16, 64)
    small["s5_log_dt"] = dldt.reshape(nl, 16)
    small["final_norm_g"] = dfg[0]
    return loss, dx, small, jnp.stack(dcond), layer_done


def kernel(x, c, ada_w, ada_b, norm1_g, ffn1_w_in, ffn1_w_out, norm2_g, w_mix_in, sgu_w, sgu_b, pool_w, pool_scale, conv_w, s5_lambda_re, s5_lambda_im, s5_b_re, s5_b_im, s5_c_re, s5_c_im, s5_d, s5_log_dt, s5_glu_w, s5_glu_b, mix_norm_g, w_mix_out, norm3_g, ffn2_w_in, ffn2_w_out, final_norm_g, loss_target, m_ada_w, m_ada_b, m_norm1_g, m_ffn1_w_in, m_ffn1_w_out, m_norm2_g, m_w_mix_in, m_sgu_w, m_sgu_b, m_pool_w, m_pool_scale, m_conv_w, m_s5_lambda_re, m_s5_lambda_im, m_s5_b_re, m_s5_b_im, m_s5_c_re, m_s5_c_im, m_s5_d, m_s5_log_dt, m_s5_glu_w, m_s5_glu_b, m_mix_norm_g, m_w_mix_out, m_norm3_g, m_ffn2_w_in, m_ffn2_w_out, m_final_norm_g, v_ada_w, v_ada_b, v_norm1_g, v_ffn1_w_in, v_ffn1_w_out, v_norm2_g, v_w_mix_in, v_sgu_w, v_sgu_b, v_pool_w, v_pool_scale, v_conv_w, v_s5_lambda_re, v_s5_lambda_im, v_s5_b_re, v_s5_b_im, v_s5_c_re, v_s5_c_im, v_s5_d, v_s5_log_dt, v_s5_glu_w, v_s5_glu_b, v_mix_norm_g, v_w_mix_out, v_norm3_g, v_ffn2_w_in, v_ffn2_w_out, v_final_norm_g):
    args = dict(locals())
    w = {n: args[n] for n in WEIGHT_ORDER}
    mom = {n: args["m_" + n] for n in WEIGHT_ORDER}
    vel = {n: args["v_" + n] for n in WEIGHT_ORDER}
    nl, d = DEPTH, x.shape[-1]
    s = x.shape[1]
    px, py, pc = lax.axis_index("x"), lax.axis_index("y"), lax.axis_index("c")
    kme = 2 * px + py
    me = 2 * kme + pc
    kidx = jnp.reshape(kme, (1,)).astype(jnp.int32)

    shards = [ffn1_w_in, ffn1_w_out, jnp.swapaxes(w_mix_in, 1, 2), w_mix_out, ffn2_w_in, ffn2_w_out]
    started_weights = {}

    def start_weights(l, after):
        placed = _place_weights(shards, l, kidx, after)
        views = [a.reshape(a.shape[0], 2, a.shape[1] // 2, a.shape[2]) for a in placed]
        *handles, token = _weight_send_start(views, BIG_KINDS, "weight_send_start_%d" % l)
        started_weights[l] = handles
        return token

    token = start_weights(0, c)
    cact = _silu_rows(c + token[0, 0])

    pre = _pack([cact, conv_w, s5_glu_w])
    pre_all = _all_gather8(pre, "gather_prelude")
    parts = [_unpack(pre_all[dev], [cact.shape, conv_w.shape, s5_glu_w.shape]) for dev in range(N_DEV)]
    cact_all = pre_all[:, :d // LANES, :].reshape(N_DEV, d)
    conv_full = jnp.concatenate([parts[2 * k][1] for k in range(N_CHIP)], axis=2)
    glu_full = jnp.concatenate([parts[2 * k][2] for k in range(N_CHIP)], axis=1)

    n_ada = ada_w.shape[2]
    ada_b_loc = lax.dynamic_slice_in_dim(ada_b, kme * n_ada, n_ada, axis=1).reshape(nl, 1, n_ada)
    cond_part = _cond_fwd(cact_all, ada_w, ada_b_loc)
    cond_all = _all_gather8(cond_part.reshape(nl * N_DEV, n_ada), "gather_cond").reshape(N_DEV, nl, N_DEV, n_ada)
    cond_me = jnp.concatenate(
        [lax.dynamic_index_in_dim(cond_all[2 * k], me, axis=1, keepdims=False) for k in range(N_CHIP)], axis=1)
    token = cond_all
    for l in range(1, nl):
        token = start_weights(l, token)
    cond = cond_me.reshape(nl, 9, d) + token[0, 0]

    forwarding = {}

    def prefetch_weights(l, after):
        if l >= nl:
            return 0.0
        send_sems, recv_sems, *views = started_weights.pop(l)
        views = _weight_send_wait(send_sems, recv_sems, views, BIG_KINDS, after, "weight_send_wait_%d" % l)
        forwarding[l] = _split_start("weight_forward_start_%d" % l, views, 3 * len(views), _forward_copies(BIG_KINDS))
        return forwarding[l][-1][0, 0]

    def fetch_weights(l, after):
        if l not in forwarding:
            prefetch_weights(l, after)
        views = _split_wait("weight_forward_wait_%d" % l, forwarding.pop(l), _forward_copies(BIG_KINDS), after)
        full = [v.reshape(2, 2 * v.shape[2], v.shape[3]) if kind == "col" else v.reshape(-1, v.shape[3])
                for v, kind in zip(views, BIG_KINDS)]
        return full, 0.0

    ck = jnp.stack([pc, kme]).astype(jnp.int32)
    scattering, sharing = [], []
    stacks = {n: None for n in BIG_NAMES}
    groups = ((["ffn1_w_in", "ffn2_w_in"], 16, "adamw_w_in"),
              (["ffn1_w_out", "w_mix_in", "w_mix_out", "ffn2_w_out"], 8, "adamw_w_out"))

    def as_reduced(t):
        return {n: jnp.swapaxes(t[n], 1, 2) if n == "w_mix_in" else t[n] for n in BIG_NAMES}

    w_r, m_r, v_r = as_reduced(w), as_reduced(mom), as_reduced(vel)

    def apply_adamw(l, fulls):
        g = {n: f.reshape(2 * f.shape[1], f.shape[2]) for n, f in zip(BIG_NAMES, fulls)}
        for names, steps, call in groups:
            outs = _adamw_layer([w_r[n] for n in names], [g[n] for n in names], [m_r[n] for n in names],
                                [v_r[n] for n in names], [stacks[n] for n in names], l, call, steps)
            stacks.update(zip(names, outs))

    def retire_share(after):
        l2, shared = sharing.pop(0)
        apply_adamw(l2, _sibling_share_wait(shared, after, "grad_share_wait_%d" % l2))

    def retire_scatter(after):
        l1, scattered = scattering.pop(0)
        sharing.append((l1, _finish_big_grads(scattered, BIG_KINDS, ck, after, l1)))

    def retire(after):
        if sharing:
            retire_share(after)
        if scattering:
            retire_scatter(after)

    def emit_grads(l, grads_l):
        exchanged = _exchange_big_grads(grads_l, BIG_KINDS, l)

        def layer_done(after):
            started = _scatter_big_grads(exchanged, BIG_KINDS, ck, after, l)
            retire(after)
            scattering.append((l, started))
            return started[-1]

        return exchanged[-1][0, 0], layer_done

    p = {n: w[n] for n in SMALL_NAMES}
    p["conv_w"], p["s5_glu_w"] = conv_full, glu_full
    loss, dx, small, dcond, first_layer_done = _local_step(x[0], loss_target[0], cond, fetch_weights, prefetch_weights,
                                                           p, emit_grads)

    small_order = [n for n in SMALL_NAMES if n != "ada_b"]
    packed = _pack([dcond] + [small[n] for n in small_order]).astype(BF16)
    mine = lax.dynamic_update_slice(lax.empty((N_DEV,) + packed.shape, BF16), packed[None], (me, 0, 0))
    gathering = _split_start("small_grads_send_start", [mine], 4, _gather_first_copies())
    scatter_token = first_layer_done(gathering[-1])
    while sharing:
        retire_share(scatter_token)
    arrived, = _split_wait("small_grads_send_wait", gathering, _gather_first_copies(), stacks[BIG_NAMES[0]][0])
    gathered_small = _gather_pass_on(arrived, "small_grads_pass_on")
    total = _sum8(gathered_small)
    shapes = [dcond.shape] + [small[n].shape for n in small_order]
    tot = dict(zip(["ada_b"] + small_order, _unpack(total, shapes)))
    grads = {n: tot[n] for n in SMALL_NAMES}
    grads["ada_b"] = tot["ada_b"].reshape(nl, 9 * d)
    grads["conv_w"] = lax.dynamic_slice_in_dim(tot["conv_w"], kme * conv_w.shape[2], conv_w.shape[2], axis=2)
    grads["s5_glu_w"] = lax.dynamic_slice_in_dim(tot["s5_glu_w"], kme * s5_glu_w.shape[1], s5_glu_w.shape[1], axis=1)

    dcond_all = gathered_small.reshape(N_DEV, -1)[:, :dcond.size].reshape(N_DEV, nl, 9 * d)
    dcond_loc = jnp.swapaxes(lax.dynamic_slice_in_dim(dcond_all, kme * n_ada, n_ada, axis=2), 0, 1)
    g_ada, d_ada, m_ada, v_ada = _ada_w_update(cact_all, dcond_loc, ada_w, m_ada_w, v_ada_w)

    while scattering or sharing:
        retire(g_ada)
    delta, new_m, new_v = {}, {}, {}
    for n in BIG_NAMES:
        grads[n], delta[n], new_m[n], new_v[n] = (jnp.swapaxes(a, 1, 2) if n == "w_mix_in" else a for a in stacks[n])

    grads["ada_w"], delta["ada_w"], new_m["ada_w"], new_v["ada_w"] = g_ada, d_ada, m_ada, v_ada
    wide = ("s5_b_re", "s5_b_im")
    for names, call, steps in (([n for n in SMALL_NAMES if n not in wide], "adamw_small", 1),
                               (list(wide), "adamw_s5_b", DEPTH)):
        outs = _adamw_many(*[[_as_rows(t[n]) for n in names] for t in (w, grads, mom, vel)], call, steps)
        for res, o in zip((delta, new_m, new_v), outs):
            res.update({n: a.reshape(w[n].shape) for n, a in zip(names, o)})

    loss_total = lax.psum(loss[0, 0], ("x", "y", "c"))
    return (loss_total, dx[None], *[grads[n] for n in WEIGHT_ORDER], *[delta[n] for n in WEIGHT_ORDER],
            *[new_m[n] for n in WEIGHT_ORDER], *[new_v[n] for n in WEIGHT_ORDER])
```

```python
import math

import jax
import jax.numpy as jnp
from jax import lax
from jax.experimental import pallas as pl
from jax.experimental.pallas import tpu as pltpu

F32, BF16 = jnp.float32, jnp.bfloat16
EPS = 1e-6
DEPTH = 4
N_DEV = 8
N_CHIP = 4
W_GRP = 256
CHUNK = 128
N_SEG = 8
LANES = 128
FFN_TF = 256
FFN_TF_WIDE = 1408
FFN_TM_WIDE = 512
VMEM_LIMIT = 56 * 1024 * 1024
ADAM_LR, ADAM_B1, ADAM_B2, ADAM_EPS, ADAM_WD, ADAM_STEP = 0.001, 0.9, 0.999, 1e-08, 0.01, 10
MESH_ID = pl.DeviceIdType.MESH
HI = lax.Precision.HIGHEST
ANY = pl.BlockSpec(memory_space=pl.ANY)


def _cp(**kw):
    return pltpu.CompilerParams(vmem_limit_bytes=VMEM_LIMIT, **kw)


def _dot(a, b):
    return jnp.dot(a.astype(BF16), b.astype(BF16), preferred_element_type=F32)


def _dot_nt(a, b):
    return lax.dot_general(a.astype(BF16), b.astype(BF16), (((1,), (1,)), ((), ())), preferred_element_type=F32)


def _dot_tn(a, b):
    return lax.dot_general(a.astype(BF16), b.astype(BF16), (((0,), (0,)), ((), ())), preferred_element_type=F32)


def _dot_hi(a, b):
    return jnp.dot(a, b, preferred_element_type=F32, precision=HI)


def _gelu(x):
    k = 0.7978845608028654
    t = jnp.tanh(k * (x + 0.044715 * x * x * x))
    return 0.5 * x * (1.0 + t), t


def _gelu_grad(x, t):
    k = 0.7978845608028654
    return 0.5 * (1.0 + t) + 0.5 * x * (1.0 - t * t) * k * (1.0 + 3.0 * 0.044715 * x * x)


def _iota(shape, axis):
    return lax.broadcasted_iota(jnp.int32, shape, axis)


def _full(shape):
    nd = len(shape)
    return pl.BlockSpec(shape, lambda *_: (0,) * nd)


def _norm_mod(xv, g, shift, scale):
    r = lax.rsqrt(jnp.mean(xv * xv, axis=-1, keepdims=True) + EPS)
    return (xv * r * g) * (1.0 + scale) + shift


def _norm_mod_bwd(xv, g, scale, dh):
    r = lax.rsqrt(jnp.mean(xv * xv, axis=-1, keepdims=True) + EPS)
    xh = xv * r
    n = xh * g
    dsh = jnp.sum(dh, axis=0, keepdims=True)
    dsc = jnp.sum(dh * n, axis=0, keepdims=True)
    dn = dh * (1.0 + scale)
    dg = jnp.sum(dn * xh, axis=0, keepdims=True)
    dxh = dn * g
    dx = r * (dxh - xh * jnp.mean(dxh * xh, axis=-1, keepdims=True))
    return dx, dsh, dsc, dg


def _tm(s):
    return min(s, 1024)


def _ffn_fwd(x, mod, g, wi, wo):
    s, d = x.shape
    f = wo.shape[0]
    tf, tm = FFN_TF_WIDE, min(s, FFN_TM_WIDE)
    nf, nt = f // tf, s // tm

    def body(x_ref, mod_ref, g_ref, wa_ref, wb_ref, wo_ref, xn_ref, h_ref, a_ref, b_ref, o_ref, acc):
        j = pl.program_id(1)

        @pl.when(j == 0)
        def _():
            hh = _norm_mod(x_ref[...], g_ref[...], mod_ref[0:1, :], mod_ref[1:2, :])
            h_ref[...] = hh.astype(BF16)
            acc[...] = jnp.zeros_like(acc)

        h = h_ref[...]
        a = jnp.dot(h, wa_ref[...], preferred_element_type=F32)
        b = jnp.dot(h, wb_ref[...], preferred_element_type=F32)
        a_ref[...] = a.astype(BF16)
        b_ref[...] = b.astype(BF16)
        u = (a * jax.nn.sigmoid(a)) * b
        acc[...] += jnp.dot(u.astype(BF16), wo_ref[...], preferred_element_type=F32)

        @pl.when(j == nf - 1)
        def _():
            o = acc[...]
            o_ref[...] = o.astype(BF16)
            xn_ref[...] = x_ref[...] + 0.5 * mod_ref[2:3, :] * o

    row = pl.BlockSpec((tm, d), lambda i, j: (i, 0))
    chunk = pl.BlockSpec((tm, tf), lambda i, j: (i, j))
    return pl.pallas_call(
        body, name="ffn_fwd", grid=(nt, nf),
        in_specs=[row, _full((3, d)), _full((1, d)),
                  pl.BlockSpec((None, d, tf), lambda i, j: (0, 0, j)),
                  pl.BlockSpec((None, d, tf), lambda i, j: (1, 0, j)),
                  pl.BlockSpec((tf, d), lambda i, j: (j, 0))],
        out_specs=[row, row, chunk, chunk, row],
        out_shape=[jax.ShapeDtypeStruct((s, d), F32), jax.ShapeDtypeStruct((s, d), BF16),
                   jax.ShapeDtypeStruct((s, f), BF16), jax.ShapeDtypeStruct((s, f), BF16),
                   jax.ShapeDtypeStruct((s, d), BF16)],
        scratch_shapes=[pltpu.VMEM((tm, d), F32)],
        compiler_params=_cp(dimension_semantics=("parallel", "arbitrary")),
    )(x, mod, g, wi, wi, wo)


def _ffn_bwd_main(dxo, o, gate, h, a, b, wo):
    s, d = dxo.shape
    f = wo.shape[0]
    tf = FFN_TF
    nf = f // tf

    def body(dxo_ref, o_ref, gate_ref, h_ref, a_ref, b_ref, wo_ref, dza_ref, dzb_ref, dwi_ref, dwo_ref, dg_ref, do_s):
        @pl.when(pl.program_id(0) == 0)
        def _():
            dxv = dxo_ref[...]
            do_s[...] = (0.5 * gate_ref[...] * dxv).astype(BF16)
            dg_ref[...] = 0.5 * jnp.sum(o_ref[...].astype(F32) * dxv, axis=0, keepdims=True)

        dov = do_s[...]
        hv = h_ref[...]
        du = lax.dot_general(dov, wo_ref[...], (((1,), (1,)), ((), ())), preferred_element_type=F32)
        av = a_ref[...].astype(F32)
        bv = b_ref[...].astype(F32)
        sa = jax.nn.sigmoid(av)
        si = av * sa
        u = (si * bv).astype(BF16)
        da = (du * bv * (sa * (1.0 + av * (1.0 - sa)))).astype(BF16)
        db = (du * si).astype(BF16)
        dza_ref[...] = da
        dzb_ref[...] = db
        dwo_ref[...] = _dot_tn(u, dov).astype(BF16)
        dwi_ref[0] = _dot_tn(hv, da).astype(BF16)
        dwi_ref[1] = _dot_tn(hv, db).astype(BF16)

    chunk = pl.BlockSpec((s, tf), lambda j: (0, j))
    once = lambda: pl.BlockSpec((s, d), lambda j: (0, 0), pipeline_mode=pl.Buffered(1))
    return pl.pallas_call(
        body, name="ffn_bwd_main", grid=(nf,),
        in_specs=[once(), once(), _full((1, d)), once(), chunk, chunk, pl.BlockSpec((tf, d), lambda j: (j, 0))],
        out_specs=[chunk, chunk, pl.BlockSpec((2, d, tf), lambda j: (0, 0, j)),
                   pl.BlockSpec((tf, d), lambda j: (j, 0)), _full((1, d))],
        out_shape=[jax.ShapeDtypeStruct((s, f), BF16), jax.ShapeDtypeStruct((s, f), BF16),
                   jax.ShapeDtypeStruct((2, d, f), BF16), jax.ShapeDtypeStruct((f, d), BF16),
                   jax.ShapeDtypeStruct((1, d), F32)],
        scratch_shapes=[pltpu.VMEM((s, d), BF16)],
        compiler_params=_cp(dimension_semantics=("arbitrary",)),
    )(dxo, o, gate, h, a, b, wo)


def _ffn_bwd_in(dza, dzb, wi, x, dxo, mod, g):
    s, d = x.shape
    f = dza.shape[1]
    tf, tm = FFN_TF_WIDE, min(s, FFN_TM_WIDE)
    nf, nt = f // tf, s // tm

    def body(dza_ref, dzb_ref, wa_ref, wb_ref, x_ref, dxo_ref, mod_ref, g_ref, dx_ref, rows_ref, acc):
        j, i = pl.program_id(0), pl.program_id(1)
        rows = pl.ds(pl.multiple_of(i * tm, tm), tm)

        @pl.when(jnp.logical_and(i == 0, j == 0))
        def _():
            rows_ref[...] = jnp.zeros_like(rows_ref)

        part = (lax.dot_general(dza_ref[...], wa_ref[...], (((1,), (1,)), ((), ())), preferred_element_type=F32)
                + lax.dot_general(dzb_ref[...], wb_ref[...], (((1,), (1,)), ((), ())), preferred_element_type=F32))

        @pl.when(j == 0)
        def _():
            acc[rows, :] = part

        @pl.when(jnp.logical_and(j > 0, j < nf - 1))
        def _():
            acc[rows, :] += part

        @pl.when(j == nf - 1)
        def _():
            dh = part + acc[rows, :] if nf > 1 else part
            dx, dsh, dsc, dg = _norm_mod_bwd(x_ref[...], g_ref[...], mod_ref[1:2, :], dh)
            dx_ref[...] = dx + dxo_ref[...]
            rows_ref[0:1, :] += dsh
            rows_ref[1:2, :] += dsc
            rows_ref[2:3, :] += dg

    late = pl.BlockSpec((tm, d), lambda j, i: (jnp.where(j == nf - 1, i, 0), 0))
    chunk = pl.BlockSpec((tm, tf), lambda j, i: (i, j))
    return pl.pallas_call(
        body, name="ffn_bwd_in", grid=(nf, nt),
        in_specs=[chunk, chunk,
                  pl.BlockSpec((None, d, tf), lambda j, i: (0, 0, j)),
                  pl.BlockSpec((None, d, tf), lambda j, i: (1, 0, j)),
                  late, late, _full((3, d)), _full((1, d))],
        out_specs=[late, _full((8, d))],
        out_shape=[jax.ShapeDtypeStruct((s, d), F32), jax.ShapeDtypeStruct((8, d), F32)],
        scratch_shapes=[pltpu.VMEM((s, d), F32)],
        compiler_params=_cp(dimension_semantics=("arbitrary", "arbitrary")),
    )(dza, dzb, wi, wi, x, dxo, mod, g)


def _mix_in_fwd(x, mod, g, wmit):
    s, d = x.shape
    p = wmit.shape[0]
    tm = _tm(s)

    def body(x_ref, mod_ref, g_ref, w_ref, z_ref, h_ref):
        hh = _norm_mod(x_ref[...], g_ref[...], mod_ref[0:1, :], mod_ref[1:2, :]).astype(BF16)
        h_ref[...] = hh
        z_ref[...] = lax.dot_general(hh, w_ref[...], (((1,), (1,)), ((), ())), preferred_element_type=F32)

    row = pl.BlockSpec((tm, d), lambda i: (i, 0))
    return pl.pallas_call(
        body, name="mix_in_fwd", grid=(s // tm,),
        in_specs=[row, _full((3, d)), _full((1, d)), _full((p, d))],
        out_specs=[pl.BlockSpec((tm, p), lambda i: (i, 0)), row],
        out_shape=[jax.ShapeDtypeStruct((s, p), F32), jax.ShapeDtypeStruct((s, d), BF16)],
        compiler_params=_cp(dimension_semantics=("parallel",)),
    )(x, mod, g, wmit)


def _mix_in_bwd(dzs, h, wmit, x, dxo, mod, g):
    s, d = x.shape
    p = wmit.shape[0]
    tm = min(s, 512)
    nt = s // tm

    def body(za_ref, zb_ref, zc_ref, zd_ref, h_ref, w_ref, x_ref, dxo_ref, mod_ref, g_ref,
             dx_ref, rows_ref, dw_ref, acc):
        i = pl.program_id(0)

        @pl.when(i == 0)
        def _():
            rows_ref[...] = jnp.zeros_like(rows_ref)
            acc[...] = jnp.zeros_like(acc)

        dz = jnp.concatenate([za_ref[...], zb_ref[...], zc_ref[...], zd_ref[...]], axis=1).astype(BF16)
        acc[...] += _dot_tn(dz, h_ref[...])
        dh = jnp.dot(dz, w_ref[...], preferred_element_type=F32)
        dx, dsh, dsc, dg = _norm_mod_bwd(x_ref[...], g_ref[...], mod_ref[1:2, :], dh)
        dx_ref[...] = dx + dxo_ref[...]
        rows_ref[0:1, :] += dsh
        rows_ref[1:2, :] += dsc
        rows_ref[2:3, :] += dg

        @pl.when(i == nt - 1)
        def _():
            dw_ref[...] = acc[...].astype(BF16)

    row = pl.BlockSpec((tm, d), lambda i: (i, 0))
    zspecs = [pl.BlockSpec((tm, z.shape[1]), lambda i: (i, 0)) for z in dzs]
    return pl.pallas_call(
        body, name="mix_in_bwd", grid=(nt,),
        in_specs=zspecs + [row, _full((p, d)), row, row, _full((3, d)), _full((1, d))],
        out_specs=[row, _full((8, d)), _full((p, d))],
        out_shape=[jax.ShapeDtypeStruct((s, d), F32), jax.ShapeDtypeStruct((8, d), F32),
                   jax.ShapeDtypeStruct((p, d), BF16)],
        scratch_shapes=[pltpu.VMEM((p, d), F32)],
        compiler_params=_cp(dimension_semantics=("arbitrary",)),
    )(*dzs, h, wmit, x, dxo, mod, g)


def _group_norm(ys, mng):
    outs, hats, rs = [], [], []
    for k, y in enumerate(ys):
        r = lax.rsqrt(jnp.mean(y * y, axis=-1, keepdims=True) + EPS)
        yh = y * r
        hats.append(yh)
        rs.append(r)
        outs.append(yh * mng[:, k * W_GRP:(k + 1) * W_GRP])
    return jnp.concatenate(outs, axis=1), hats, rs


def _s5_glu(y, gw, gb):
    yg, t = _gelu(y)
    gate = jax.nn.sigmoid(_dot(yg, gw) + gb)
    return yg * gate, yg, t, gate


def _mix_out_fwd(ys, glu, mng, wmo, x, gate):
    s, d = x.shape
    tm = _tm(s)

    def body(ya, yb, yc, ypre, gw_ref, gb_ref, mng_ref, w_ref, x_ref, gate_ref, xn_ref, m_ref):
        yd = _s5_glu(ypre[...], gw_ref[...], gb_ref[...])[0]
        yn, _, _ = _group_norm([ya[...], yb[...], yc[...], yd], mng_ref[...])
        m = jnp.dot(yn.astype(BF16), w_ref[...], preferred_element_type=F32)
        m_ref[...] = m
        xn_ref[...] = x_ref[...] + gate_ref[...] * m

    row = pl.BlockSpec((tm, d), lambda i: (i, 0))
    grp = pl.BlockSpec((tm, W_GRP), lambda i: (i, 0))
    return pl.pallas_call(
        body, name="mix_out_fwd", grid=(s // tm,),
        in_specs=[grp, grp, grp, grp, _full((W_GRP, W_GRP)), _full((1, W_GRP)), _full((1, d)), _full((d, d)), row,
                  _full((1, d))],
        out_specs=[row, row],
        out_shape=[jax.ShapeDtypeStruct((s, d), F32), jax.ShapeDtypeStruct((s, d), F32)],
        compiler_params=_cp(dimension_semantics=("parallel",)),
    )(*ys, *glu, mng, wmo, x, gate)


def _mix_out_bwd(dxo, m, gate, ys, glu, mng, wmo):
    s, d = dxo.shape
    tm = min(s, 512)
    nt = s // tm

    def body(dxo_ref, m_ref, gate_ref, ya, yb, yc, ypre, gw_ref, gb_ref, mng_ref, w_ref,
             dya, dyb, dyc, dypre, dgate_ref, dmng_ref, dw_ref, dgw_ref, dgb_ref, acc):
        i = pl.program_id(0)

        @pl.when(i == 0)
        def _():
            dgate_ref[...] = jnp.zeros_like(dgate_ref)
            dmng_ref[...] = jnp.zeros_like(dmng_ref)
            dgw_ref[...] = jnp.zeros_like(dgw_ref)
            dgb_ref[...] = jnp.zeros_like(dgb_ref)
            acc[...] = jnp.zeros_like(acc)

        dxv = dxo_ref[...]
        dgate_ref[...] += jnp.sum(m_ref[...] * dxv, axis=0, keepdims=True)
        dm = (gate_ref[...] * dxv).astype(BF16)
        mng = mng_ref[...]
        gw = gw_ref[...]
        yp = ypre[...]
        yd, yg, t, glu_gate = _s5_glu(yp, gw, gb_ref[...])
        yn, hats, rs = _group_norm([ya[...], yb[...], yc[...], yd], mng)
        acc[...] += _dot_tn(yn, dm)
        dyn = lax.dot_general(dm, w_ref[...], (((1,), (1,)), ((), ())), preferred_element_type=F32)
        dmng_parts, dys = [], []
        for k, (yh, r) in enumerate(zip(hats, rs)):
            dk = dyn[:, k * W_GRP:(k + 1) * W_GRP]
            dmng_parts.append(jnp.sum(dk * yh, axis=0, keepdims=True))
            dyh = dk * mng[:, k * W_GRP:(k + 1) * W_GRP]
            dys.append(r * (dyh - yh * jnp.mean(dyh * yh, axis=-1, keepdims=True)))
        dmng_ref[...] += jnp.concatenate(dmng_parts, axis=1)
        dya[...], dyb[...], dyc[...] = dys[0], dys[1], dys[2]
        dyd = dys[3]
        dlin = dyd * yg * glu_gate * (1.0 - glu_gate)
        dgw_ref[...] += _dot_tn(yg, dlin)
        dgb_ref[...] += jnp.sum(dlin, axis=0, keepdims=True)
        dypre[...] = (dyd * glu_gate + _dot_nt(dlin, gw)) * _gelu_grad(yp, t)

        @pl.when(i == nt - 1)
        def _():
            dw_ref[...] = acc[...].astype(BF16)

    row = pl.BlockSpec((tm, d), lambda i: (i, 0))
    grp = pl.BlockSpec((tm, W_GRP), lambda i: (i, 0))
    return pl.pallas_call(
        body, name="mix_out_bwd", grid=(nt,),
        in_specs=[row, row, _full((1, d)), grp, grp, grp, grp, _full((W_GRP, W_GRP)), _full((1, W_GRP)), _full((1, d)),
                  _full((d, d))],
        out_specs=[grp, grp, grp, grp, _full((1, d)), _full((1, d)), _full((d, d)), _full((W_GRP, W_GRP)),
                   _full((1, W_GRP))],
        out_shape=[jax.ShapeDtypeStruct((s, W_GRP), F32)] * 4
        + [jax.ShapeDtypeStruct((1, d), F32), jax.ShapeDtypeStruct((1, d), F32), jax.ShapeDtypeStruct((d, d), BF16),
           jax.ShapeDtypeStruct((W_GRP, W_GRP), F32), jax.ShapeDtypeStruct((1, W_GRP), F32)],
        scratch_shapes=[pltpu.VMEM((d, d), F32)],
        compiler_params=_cp(dimension_semantics=("arbitrary",)),
    )(dxo, m, gate, *ys, *glu, mng, wmo)


def _sgu_consts():
    r = _iota((W_GRP, W_GRP), 0) >> 6
    c = _iota((W_GRP, W_GRP), 1) >> 6
    avg = jnp.where(r == c, 1.0 / 64.0, 0.0).astype(F32)
    tril = _iota((CHUNK, CHUNK), 0) >= _iota((CHUNK, CHUNK), 1)
    head = _iota((CHUNK, W_GRP), 1) >> 6
    return avg, tril, head


def _sgu_pre(za, avg):
    zg, t = _gelu(za)
    u, v = zg[:, :W_GRP], zg[:, W_GRP:]
    mu = _dot_hi(v, avg)
    vc = v - mu
    r = lax.rsqrt(_dot_hi(vc * vc, avg) + EPS)
    return t, u, vc * r, r


def _sgu_fwd(z, sgu_w, bias_full):
    s = z.shape[0]
    tm = min(s, 512)

    def body(za_ref, w_ref, bias_ref, ya_ref):
        avg, tril, head = _sgu_consts()
        _, u, vn, _ = _sgu_pre(za_ref[...], avg)
        wm = [jnp.where(tril, w_ref[h], 0.0).astype(BF16) for h in range(4)]
        vb = vn.astype(BF16)
        for n in range(tm // CHUNK):
            rows = slice(n * CHUNK, (n + 1) * CHUNK)
            mixed = bias_ref[...]
            for h in range(4):
                mixed = mixed + jnp.where(head == h, jnp.dot(wm[h], vb[rows], preferred_element_type=F32), 0.0)
            ya_ref[rows, :] = u[rows] * mixed

    return pl.pallas_call(
        body, name="sgu_fwd", grid=(s // tm,),
        in_specs=[pl.BlockSpec((tm, 2 * W_GRP), lambda i: (i, 0)), _full((4, CHUNK, CHUNK)), _full((CHUNK, W_GRP))],
        out_specs=pl.BlockSpec((tm, W_GRP), lambda i: (i, 0)),
        out_shape=jax.ShapeDtypeStruct((s, W_GRP), F32),
        compiler_params=_cp(dimension_semantics=("parallel",)),
    )(z, sgu_w, bias_full)


def _sgu_bwd(z, dya, sgu_w, bias_full):
    s = z.shape[0]
    tm = min(s, 512)
    nt = s // tm

    def body(za_ref, dya_ref, w_ref, bias_ref, dza_ref, dw_ref, db_ref, du_s, dvn_s):
        i = pl.program_id(0)

        @pl.when(i == 0)
        def _():
            dw_ref[...] = jnp.zeros_like(dw_ref)
            db_ref[...] = jnp.zeros_like(db_ref)

        avg, tril, head = _sgu_consts()
        za = za_ref[...]
        t, u, vn, r = _sgu_pre(za, avg)
        wm = [jnp.where(tril, w_ref[h], 0.0).astype(BF16) for h in range(4)]
        vb = vn.astype(BF16)
        dya = dya_ref[...]
        dw = [jnp.zeros((CHUNK, CHUNK), F32) for _ in range(4)]
        db = jnp.zeros((CHUNK, W_GRP), F32)
        for n in range(tm // CHUNK):
            rows = slice(n * CHUNK, (n + 1) * CHUNK)
            mixed = bias_ref[...]
            for h in range(4):
                mixed = mixed + jnp.where(head == h, jnp.dot(wm[h], vb[rows], preferred_element_type=F32), 0.0)
            dmix = dya[rows] * u[rows]
            du_s[rows, :] = dya[rows] * mixed
            db = db + dmix
            dmb = dmix.astype(BF16)
            dvn = jnp.zeros((CHUNK, W_GRP), F32)
            for h in range(4):
                dmh = jnp.where(head == h, dmix, 0.0)
                dw[h] = dw[h] + _dot_nt(dmh, vb[rows])
                dvn = dvn + jnp.where(head == h, _dot_tn(wm[h], dmb), 0.0)
            dvn_s[rows, :] = dvn
        for h in range(4):
            dw_ref[h] += jnp.where(tril, dw[h], 0.0)
        sel = ((_iota((W_GRP, CHUNK), 0) >> 6) == _iota((W_GRP, CHUNK), 1)).astype(F32)
        db_ref[...] += _dot_hi(db, sel)
        dvn = dvn_s[...]
        dv = r * (dvn - _dot_hi(dvn, avg) - vn * _dot_hi(dvn * vn, avg))
        dzg = jnp.concatenate([du_s[...], dv], axis=1)
        dza_ref[...] = dzg * _gelu_grad(za, t)

    return pl.pallas_call(
        body, name="sgu_bwd", grid=(nt,),
        in_specs=[pl.BlockSpec((tm, 2 * W_GRP), lambda i: (i, 0)), pl.BlockSpec((tm, W_GRP), lambda i: (i, 0)),
                  _full((4, CHUNK, CHUNK)), _full((CHUNK, W_GRP))],
        out_specs=[pl.BlockSpec((tm, 2 * W_GRP), lambda i: (i, 0)), _full((4, CHUNK, CHUNK)), _full((CHUNK, CHUNK))],
        out_shape=[jax.ShapeDtypeStruct((s, 2 * W_GRP), F32), jax.ShapeDtypeStruct((4, CHUNK, CHUNK), F32),
                   jax.ShapeDtypeStruct((CHUNK, CHUNK), F32)],
        scratch_shapes=[pltpu.VMEM((tm, W_GRP), F32), pltpu.VMEM((tm, W_GRP), F32)],
        compiler_params=_cp(dimension_semantics=("arbitrary",)),
    )(z, dya, sgu_w, bias_full)


def _shift_down(x, k):
    return jnp.where(_iota(x.shape, 0) < k, 0.0, pltpu.roll(x, k, 0))


def _shift_up(x, k):
    n = x.shape[0]
    return jnp.where(_iota(x.shape, 0) >= n - k, 0.0, pltpu.roll(x, n - k, 0))


def _by_pool_group(shape, v2, v4, v8, v16):
    col = _iota(shape, 1)
    return jnp.where(col < 64, v2, jnp.where(col < 128, v4, jnp.where(col < 192, v8, v16)))


def _pool_core(zb, pw2):
    s2 = zb + _shift_down(zb, 1)
    s4 = s2 + _shift_down(s2, 2)
    s8 = s4 + _shift_down(s4, 4)
    s16 = s8 + _shift_down(s8, 8)
    win = _by_pool_group(zb.shape, s2, s4, s8, s16)
    wlen = _by_pool_group(zb.shape, 2.0, 4.0, 8.0, 16.0)
    cnt = jnp.minimum((_iota(zb.shape, 0) + 1).astype(F32), wlen)
    p = win / cnt - zb
    wt = jnp.tile(pw2, (1, 4))
    wbd = jnp.where((_iota(wt.shape, 0) >> 6) == (_iota(wt.shape, 1) >> 6), wt, 0.0).astype(BF16)
    return p, cnt, wbd


def _conv_core(zc, cw):
    bg, cg, xh = zc[:, :W_GRP], zc[:, W_GRP:2 * W_GRP], zc[:, 2 * W_GRP:]
    y = cg * xh
    y1, y2 = _shift_down(y, 1), _shift_down(y, 2)
    out = cw[2:3, :] * y + cw[1:2, :] * y1 + cw[0:1, :] * y2
    return bg, cg, xh, y, y1, y2, out


def _poolconv_fwd(z, pw2, pscale, cw):
    s = z.shape[0]

    def body(zb_ref, zc_ref, pw_ref, ps_ref, cw_ref, yb_ref, yc_ref):
        p, _, wbd = _pool_core(zb_ref[...], pw_ref[...])
        yb_ref[...] = jnp.dot(p.astype(BF16), wbd, preferred_element_type=F32) * ps_ref[...]
        bg, _, _, _, _, _, out = _conv_core(zc_ref[...], cw_ref[...])
        yc_ref[...] = bg * out

    return pl.pallas_call(
        body, name="poolconv_fwd", grid=(1,),
        in_specs=[pl.BlockSpec((s, W_GRP), lambda i: (0, 2)), pl.BlockSpec((s, 3 * W_GRP), lambda i: (0, 1)),
                  _full((W_GRP, 64)), _full((1, W_GRP)), _full((3, W_GRP))],
        out_specs=[_full((s, W_GRP)), _full((s, W_GRP))],
        out_shape=[jax.ShapeDtypeStruct((s, W_GRP), F32)] * 2,
        compiler_params=_cp(dimension_semantics=("arbitrary",)),
    )(z, z, pw2, pscale, cw)


def _poolconv_bwd(z, dyb, dyc, pw2, pscale, cw):
    s = z.shape[0]

    def body(zb_ref, zc_ref, dyb_ref, dyc_ref, pw_ref, ps_ref, cw_ref, dzb_ref, dzc_ref, dw_ref, dps_ref, dcw_ref):
        zb = zb_ref[...]
        p, cnt, wbd = _pool_core(zb, pw_ref[...])
        pb = p.astype(BF16)
        out = jnp.dot(pb, wbd, preferred_element_type=F32)
        dyb = dyb_ref[...]
        dps_ref[...] = jnp.sum(dyb * out, axis=0, keepdims=True)
        dout = (dyb * ps_ref[...]).astype(BF16)
        dw = _dot_tn(pb, dout)
        dw_ref[...] = jnp.where((_iota(dw.shape, 0) >> 6) == (_iota(dw.shape, 1) >> 6), dw, 0.0)
        dp = lax.dot_general(dout, wbd, (((1,), (1,)), ((), ())), preferred_element_type=F32)
        dwin = dp / cnt
        t2 = dwin + _shift_up(dwin, 1)
        t4 = t2 + _shift_up(t2, 2)
        t8 = t4 + _shift_up(t4, 4)
        t16 = t8 + _shift_up(t8, 8)
        dzb_ref[...] = _by_pool_group(zb.shape, t2, t4, t8, t16) - dp

        cw = cw_ref[...]
        bg, cg, xh, y, y1, y2, out = _conv_core(zc_ref[...], cw)
        dyc = dyc_ref[...]
        dout = dyc * bg
        dcw_ref[...] = jnp.zeros_like(dcw_ref)
        dcw_ref[0:1, :] = jnp.sum(dout * y2, axis=0, keepdims=True)
        dcw_ref[1:2, :] = jnp.sum(dout * y1, axis=0, keepdims=True)
        dcw_ref[2:3, :] = jnp.sum(dout * y, axis=0, keepdims=True)
        dy = cw[2:3, :] * dout + cw[1:2, :] * _shift_up(dout, 1) + cw[0:1, :] * _shift_up(dout, 2)
        dzc_ref[...] = jnp.concatenate([dyc * out, dy * xh, dy * cg], axis=1)

    return pl.pallas_call(
        body, name="poolconv_bwd", grid=(1,),
        in_specs=[pl.BlockSpec((s, W_GRP), lambda i: (0, 2)), pl.BlockSpec((s, 3 * W_GRP), lambda i: (0, 1)),
                  _full((s, W_GRP)), _full((s, W_GRP)), _full((W_GRP, 64)), _full((1, W_GRP)), _full((3, W_GRP))],
        out_specs=[_full((s, W_GRP)), _full((s, 3 * W_GRP)), _full((W_GRP, W_GRP)), _full((1, W_GRP)), _full((8, W_GRP))],
        out_shape=[jax.ShapeDtypeStruct((s, W_GRP), F32), jax.ShapeDtypeStruct((s, 3 * W_GRP), F32),
                   jax.ShapeDtypeStruct((W_GRP, W_GRP), F32), jax.ShapeDtypeStruct((1, W_GRP), F32),
                   jax.ShapeDtypeStruct((8, W_GRP), F32)],
        compiler_params=_cp(dimension_semantics=("arbitrary",)),
    )(z, z, dyb, dyc, pw2, pscale, cw)


N_STATE = 1024
HALF_STATE = N_STATE // 2
HALF_CH = W_GRP // 2
N_SLAB = HALF_STATE // LANES


def _s5_disc(lre, lim, ldt):
    dt = jnp.exp(ldt)
    mag = jnp.exp(lre * dt)
    ang = lim * dt
    ar, ai = mag * jnp.cos(ang), mag * jnp.sin(ang)
    nr, ni = ar - 1.0, ai
    den = lre * lre + lim * lim
    kr = (nr * lre + ni * lim) / den
    ki = (ni * lre - nr * lim) / den
    return ar, ai, kr, ki


def _s5_mats(colp, br, bi, cr, ci):
    _, _, kr, ki = _s5_disc(colp[:, 0:1], colp[:, 1:2], colp[:, 2:3])
    bbr = kr * br - ki * bi
    bbi = kr * bi + ki * br
    bmask = (_iota((HALF_STATE, HALF_CH), 0) >> 6) == (_iota((HALF_STATE, HALF_CH), 1) >> 4)
    cmask = (_iota((HALF_CH, HALF_STATE), 0) >> 4) == (_iota((HALF_CH, HALF_STATE), 1) >> 6)
    btr = jnp.where(bmask, jnp.tile(bbr, (1, 8)), 0.0).astype(BF16)
    bti = jnp.where(bmask, jnp.tile(bbi, (1, 8)), 0.0).astype(BF16)
    ctr = jnp.where(cmask, jnp.tile(cr, (1, 8)), 0.0).astype(BF16)
    cti = jnp.where(cmask, jnp.tile(ci, (1, 8)), 0.0).astype(BF16)
    return kr, ki, btr, bti, ctr, cti, bmask, cmask


def _slab(q):
    return slice(q * LANES, (q + 1) * LANES)


def _cmul(ar, ai, br, bi):
    return ar * br - ai * bi, ar * bi + ai * br


def _sub_shift(x, k, up):
    row = _iota(x.shape, 0)
    if up:
        return jnp.where(row >= N_SEG - k, 0.0, pltpu.roll(x, N_SEG - k, 0))
    return jnp.where(row < k, 0.0, pltpu.roll(x, k, 0))


def _seg_rows(j):
    return pl.ds(pl.multiple_of(j * N_SEG, N_SEG), N_SEG)


def _interleave(src, dst, seg):
    def step(j, carry):
        dst[_seg_rows(j), :] = src[pl.ds(j, N_SEG, stride=seg), :]
        return carry
    lax.fori_loop(0, seg, step, 0)


def _deinterleave(src, dst, seg):
    def step(j, carry):
        dst[pl.ds(j, N_SEG, stride=seg), :] = src[_seg_rows(j), :]
        return carry
    lax.fori_loop(0, seg, step, 0)


def _scan(xr, xi, ar_row, ai_row, seg, reverse, states=None):
    nlog = int(math.log2(seg))
    assert (1 << nlog) == seg
    grads = []
    for q0 in range(0, N_SLAB, 4):
        qs = list(range(q0, q0 + 4))
        aq = [(jnp.broadcast_to(ar_row[:, _slab(q)], (N_SEG, LANES)),
               jnp.broadcast_to(ai_row[:, _slab(q)], (N_SEG, LANES))) for q in qs]
        zero = jnp.zeros((N_SEG, LANES), F32)

        def local(jj, carry, qs=qs, aq=aq):
            j = seg - 1 - jj if reverse else jj
            out = []
            for n, q in enumerate(qs):
                rows = _seg_rows(j)
                pr, pi = _cmul(aq[n][0], aq[n][1], carry[2 * n], carry[2 * n + 1])
                nr = pr + xr[q, rows, :]
                ni = pi + xi[q, rows, :]
                xr[q, rows, :] = nr
                xi[q, rows, :] = ni
                out += [nr, ni]
            return tuple(out)

        fin = lax.fori_loop(0, seg, local, (zero,) * 8)
        cins = []
        for n in range(4):
            er, ei = fin[2 * n], fin[2 * n + 1]
            pr, pi = aq[n]
            for _ in range(nlog):
                pr, pi = _cmul(pr, pi, pr, pi)
            yr, yi = er, ei
            for k in (1, 2, 4):
                sr, si = _cmul(pr, pi, _sub_shift(yr, k, reverse), _sub_shift(yi, k, reverse))
                yr, yi = yr + sr, yi + si
                pr, pi = _cmul(pr, pi, pr, pi)
            cins.append((_sub_shift(yr, 1, reverse), _sub_shift(yi, 1, reverse)))

        def fix(jj, carry, qs=qs, aq=aq, cins=cins):
            j = seg - 1 - jj if reverse else jj
            out, sums = [], []
            for n, q in enumerate(qs):
                rows = _seg_rows(j)
                pwr, pwi = carry[2 * n], carry[2 * n + 1]
                cr, ci = _cmul(pwr, pwi, cins[n][0], cins[n][1])
                v_r, v_i = xr[q, rows, :] + cr, xi[q, rows, :] + ci
                xr[q, rows, :] = v_r
                xi[q, rows, :] = v_i
                nr, ni = _cmul(pwr, pwi, aq[n][0], aq[n][1])
                out += [nr, ni]
                if states is not None:
                    prev = _seg_rows(j - 1)
                    p_r, p_i = states[0][q, prev, :], states[1][q, prev, :]
                    sums += [carry[8 + 2 * n] + v_r * p_r + v_i * p_i, carry[9 + 2 * n] - v_r * p_i + v_i * p_r]
            return tuple(out + sums)

        powers = tuple(v for pair in aq for v in pair)
        if states is None:
            lax.fori_loop(0, seg, fix, powers)
            continue
        assert reverse
        fix_last = lax.fori_loop(0, seg - 1, fix, powers + (zero,) * 8)
        first = _seg_rows(0)
        for n, q in enumerate(qs):
            cr, ci = _cmul(fix_last[2 * n], fix_last[2 * n + 1], cins[n][0], cins[n][1])
            v_r, v_i = xr[q, first, :] + cr, xi[q, first, :] + ci
            xr[q, first, :] = v_r
            xi[q, first, :] = v_i
            p_r = _sub_shift(states[0][q, _seg_rows(seg - 1), :], 1, False)
            p_i = _sub_shift(states[1][q, _seg_rows(seg - 1), :], 1, False)
            grads.append((jnp.sum(fix_last[8 + 2 * n] + v_r * p_r + v_i * p_i, axis=0, keepdims=True),
                          jnp.sum(fix_last[9 + 2 * n] - v_r * p_i + v_i * p_r, axis=0, keepdims=True)))
    return grads


def _s5_forward_states(u, btr, bti, ar_row, ai_row, xr, xi, seg):
    ub = u.astype(BF16)
    for q in range(N_SLAB):
        xr[q] = _dot_nt(ub, btr[_slab(q), :])
        xi[q] = _dot_nt(ub, bti[_slab(q), :])
    _scan(xr, xi, ar_row, ai_row, seg, False)


def _s5_readout(u, xr, xi, ctr, cti, d):
    y = d * u
    for q in range(N_SLAB):
        y = y + _dot_nt(xr[q], ctr[:, _slab(q)]) - _dot_nt(xi[q], cti[:, _slab(q)])
    return y


def _s5_param_specs():
    return [pl.BlockSpec((3, HALF_STATE), lambda i: (0, i)), pl.BlockSpec((HALF_STATE, 3), lambda i: (i, 0)),
            pl.BlockSpec((HALF_STATE, 16), lambda i: (i, 0)), pl.BlockSpec((HALF_STATE, 16), lambda i: (i, 0)),
            pl.BlockSpec((HALF_CH, 64), lambda i: (i, 0)), pl.BlockSpec((HALF_CH, 64), lambda i: (i, 0)),
            pl.BlockSpec((1, HALF_CH), lambda i: (0, i))]


def _s5_core_fwd(z, sp):
    s = z.shape[0]
    seg = s // N_SEG

    def body(u_ref, rowp, colp, br, bi, cr, ci, d_ref, y_ref, xr, xi, us, ys):
        ar, ai, _, _ = _s5_disc(rowp[0:1, :], rowp[1:2, :], rowp[2:3, :])
        _, _, btr, bti, ctr, cti, _, _ = _s5_mats(colp[...], br[...], bi[...], cr[...], ci[...])
        _interleave(u_ref, us, seg)
        u = us[...]
        _s5_forward_states(u, btr, bti, ar, ai, xr, xi, seg)
        ys[...] = _s5_readout(u, xr, xi, ctr, cti, d_ref[...])
        _deinterleave(ys, y_ref, seg)

    return pl.pallas_call(
        body, name="s5_core_fwd", grid=(2,),
        in_specs=[pl.BlockSpec((s, HALF_CH), lambda i: (0, 12 + i))] + _s5_param_specs(),
        out_specs=pl.BlockSpec((s, HALF_CH), lambda i: (0, i)),
        out_shape=jax.ShapeDtypeStruct((s, W_GRP), F32),
        scratch_shapes=[pltpu.VMEM((N_SLAB, s, LANES), F32)] * 2 + [pltpu.VMEM((s, HALF_CH), F32)] * 2,
        compiler_params=_cp(dimension_semantics=("parallel",)),
    )(z, *sp)


def _s5_core_bwd(z, dy, sp):
    s = z.shape[0]
    seg = s // N_SEG

    def body(u_ref, dy_ref, rowp, colp, br_ref, bi_ref, cr_ref, ci_ref, d_ref,
             du_ref, dbr_ref, dbi_ref, dcr_ref, dci_ref, dd_ref, da_ref, dk_ref,
             xr, xi, gr, gi, us, dys):
        ar, ai, _, _ = _s5_disc(rowp[0:1, :], rowp[1:2, :], rowp[2:3, :])
        br, bi = br_ref[...], bi_ref[...]
        kr, ki, btr, bti, ctr, cti, bmask, cmask = _s5_mats(colp[...], br, bi, cr_ref[...], ci_ref[...])
        _interleave(u_ref, us, seg)
        _interleave(dy_ref, dys, seg)
        u = us[...]
        d = d_ref[...]
        _s5_forward_states(u, btr, bti, ar, ai, xr, xi, seg)

        dy = dys[...]
        dd_ref[...] = jnp.sum(dy * u, axis=0, keepdims=True)
        du = d * dy
        dyb = dy.astype(BF16)
        dctr, dcti = [], []
        for q in range(N_SLAB):
            gr[q] = jnp.dot(dyb, ctr[:, _slab(q)], preferred_element_type=F32)
            gi[q] = -jnp.dot(dyb, cti[:, _slab(q)], preferred_element_type=F32)
            dctr.append(_dot_tn(dyb, xr[q]))
            dcti.append(-_dot_tn(dyb, xi[q]))
        selp = ((_iota((HALF_STATE, 64), 0) & 63) == _iota((HALF_STATE, 64), 1)).astype(F32)
        dcr_ref[...] = _dot_hi(jnp.where(cmask, jnp.concatenate(dctr, axis=1), 0.0), selp)
        dci_ref[...] = _dot_hi(jnp.where(cmask, jnp.concatenate(dcti, axis=1), 0.0), selp)

        da = _scan(gr, gi, ar, -ai, seg, True, states=(xr, xi))
        dar, dai = [p[0] for p in da], [p[1] for p in da]
        da_ref[...] = jnp.zeros_like(da_ref)
        da_ref[0:1, :] = jnp.concatenate(dar, axis=1)
        da_ref[1:2, :] = jnp.concatenate(dai, axis=1)

        ub = u.astype(BF16)
        dbtr, dbti = [], []
        for q in range(N_SLAB):
            g_r, g_i = gr[q].astype(BF16), gi[q].astype(BF16)
            du = du + jnp.dot(g_r, btr[_slab(q), :], preferred_element_type=F32) \
                + jnp.dot(g_i, bti[_slab(q), :], preferred_element_type=F32)
            dbtr.append(_dot_tn(g_r, ub))
            dbti.append(_dot_tn(g_i, ub))
        us[...] = du
        _deinterleave(us, du_ref, seg)
        selc =((_iota((HALF_CH, 16), 0) & 15) == _iota((HALF_CH, 16), 1)).astype(F32)
        dbbr = _dot_hi(jnp.where(bmask, jnp.concatenate(dbtr, axis=0), 0.0), selc)
        dbbi = _dot_hi(jnp.where(bmask, jnp.concatenate(dbti, axis=0), 0.0), selc)
        dbr_ref[...] = kr * dbbr + ki * dbbi
        dbi_ref[...] = kr * dbbi - ki * dbbr
        dk_ref[:, 0:1] = jnp.sum(dbbr * br + dbbi * bi, axis=1, keepdims=True)
        dk_ref[:, 1:2] = jnp.sum(dbbi * br - dbbr * bi, axis=1, keepdims=True)

    half = pl.BlockSpec((s, HALF_CH), lambda i: (0, i))
    return pl.pallas_call(
        body, name="s5_core_bwd", grid=(2,),
        in_specs=[pl.BlockSpec((s, HALF_CH), lambda i: (0, 12 + i)), half] + _s5_param_specs(),
        out_specs=[half, pl.BlockSpec((HALF_STATE, 16), lambda i: (i, 0)), pl.BlockSpec((HALF_STATE, 16), lambda i: (i, 0)),
                   pl.BlockSpec((HALF_CH, 64), lambda i: (i, 0)), pl.BlockSpec((HALF_CH, 64), lambda i: (i, 0)),
                   pl.BlockSpec((1, HALF_CH), lambda i: (0, i)), pl.BlockSpec((8, HALF_STATE), lambda i: (0, i)),
                   pl.BlockSpec((HALF_STATE, 2), lambda i: (i, 0))],
        out_shape=[jax.ShapeDtypeStruct((s, W_GRP), F32), jax.ShapeDtypeStruct((N_STATE, 16), F32),
                   jax.ShapeDtypeStruct((N_STATE, 16), F32), jax.ShapeDtypeStruct((W_GRP, 64), F32),
                   jax.ShapeDtypeStruct((W_GRP, 64), F32), jax.ShapeDtypeStruct((1, W_GRP), F32),
                   jax.ShapeDtypeStruct((8, N_STATE), F32), jax.ShapeDtypeStruct((N_STATE, 2), F32)],
        scratch_shapes=[pltpu.VMEM((N_SLAB, s, LANES), F32)] * 4 + [pltpu.VMEM((s, HALF_CH), F32)] * 2,
        compiler_params=_cp(dimension_semantics=("parallel",)),
    )(z, dy, *sp)


def _s5_param_bwd(lre, lim, ldt, da_r, da_i, dk_r, dk_i):
    n = lre.shape[0]

    def body(lre_ref, lim_ref, ldt_ref, dar_ref, dai_ref, dkr_ref, dki_ref, o_re, o_im, o_dt):
        lre, lim, ldt = lre_ref[...], lim_ref[...], ldt_ref[...]
        dt = jnp.exp(ldt)
        ar, ai, kr, ki = _s5_disc(lre, lim, ldt)
        mag = jnp.exp(lre * dt)
        den = lre * lre + lim * lim
        dkr, dki = dkr_ref[...], dki_ref[...]
        nr, ni = ar - 1.0, ai
        d_ar = dar_ref[...] + (dkr * lre - dki * lim) / den
        d_ai = dai_ref[...] + (dkr * lim + dki * lre) / den
        kk = (kr * dkr + ki * dki) * 2.0 / den
        d_lre = (dkr * nr + dki * ni) / den - kk * lre
        d_lim = (dkr * ni - dki * nr) / den - kk * lim
        d_mag = (d_ar * ar + d_ai * ai) / mag
        d_ang = d_ai * ar - d_ar * ai
        o_re[...] = d_lre + d_mag * mag * dt
        o_im[...] = d_lim + d_ang * dt
        o_dt[...] = jnp.sum((d_mag * mag * lre + d_ang * lim) * dt, axis=1, keepdims=True)

    return pl.pallas_call(
        body, name="s5_param_bwd",
        out_shape=[jax.ShapeDtypeStruct((n, 64), F32), jax.ShapeDtypeStruct((n, 64), F32),
                   jax.ShapeDtypeStruct((n, 1), F32)],
    )(lre, lim, ldt, da_r, da_i, dk_r, dk_i)


def _loss_head(x, fg, target):
    s, d = x.shape
    tm = _tm(s)

    def body(x_ref, fg_ref, t_ref, loss_ref, dx_ref, dfg_ref):
        i = pl.program_id(0)

        @pl.when(i == 0)
        def _():
            loss_ref[...] = jnp.zeros_like(loss_ref)
            dfg_ref[...] = jnp.zeros_like(dfg_ref)

        xv, g = x_ref[...], fg_ref[...]
        r = lax.rsqrt(jnp.mean(xv * xv, axis=-1, keepdims=True) + EPS)
        xh = xv * r
        err = xh * g - t_ref[...]
        loss_ref[...] += 0.5 * jnp.sum(jnp.mean(err * err, axis=-1, keepdims=True), axis=0, keepdims=True)
        dy = err * (1.0 / d)
        dfg_ref[...] += jnp.sum(dy * xh, axis=0, keepdims=True)
        dxh = dy * g
        dx_ref[...] = r * (dxh - xh * jnp.mean(dxh * xh, axis=-1, keepdims=True))

    row = pl.BlockSpec((tm, d), lambda i: (i, 0))
    return pl.pallas_call(
        body, name="loss_head", grid=(s // tm,),
        in_specs=[row, _full((1, d)), row], out_specs=[_full((1, 1)), row, _full((1, d))],
        out_shape=[jax.ShapeDtypeStruct((1, 1), F32), jax.ShapeDtypeStruct((s, d), F32),
                   jax.ShapeDtypeStruct((1, d), F32)],
        compiler_params=_cp(dimension_semantics=("arbitrary",)),
    )(x, fg, target)


ADA_TN = 384


def _cond_fwd(cact, ada_w, ada_b_loc):
    nl, d, n = ada_w.shape

    def body(c_ref, w_ref, b_ref, o_ref):
        o_ref[...] = _dot(c_ref[...], w_ref[...]) + b_ref[...]

    return pl.pallas_call(
        body, name="cond_fwd", grid=(nl, n // ADA_TN),
        in_specs=[_full((N_DEV, d)), pl.BlockSpec((None, d, ADA_TN), lambda l, j: (l, 0, j)),
                  pl.BlockSpec((None, 1, ADA_TN), lambda l, j: (l, 0, j))],
        out_specs=pl.BlockSpec((None, N_DEV, ADA_TN), lambda l, j: (l, 0, j)),
        out_shape=jax.ShapeDtypeStruct((nl, N_DEV, n), F32),
        compiler_params=_cp(dimension_semantics=("parallel", "parallel")),
    )(cact, ada_w, ada_b_loc)


ELEMENTWISE_BLOCK_BYTES = 1 << 20


def _row_tile(r, c, itemsize=4):
    best = None
    for t in range(8, r + 1, 8):
        if r % t == 0 and t * c * itemsize <= ELEMENTWISE_BLOCK_BYTES:
            best = t
    return best if best is not None else r


def _adamw_math(w, g, m, v):
    m = ADAM_B1 * m + (1.0 - ADAM_B1) * g
    v = ADAM_B2 * v + (1.0 - ADAM_B2) * (g * g)
    m_hat = m / (1.0 - ADAM_B1 ** ADAM_STEP)
    v_hat = v / (1.0 - ADAM_B2 ** ADAM_STEP)
    delta = -ADAM_LR * (m_hat / (jnp.sqrt(v_hat) + ADAM_EPS) + ADAM_WD * w)
    return delta, m, v


def _ada_w_update(cact, dcond_loc, w, m, v):
    nl, d, n = w.shape

    def body(c_ref, dc_ref, w_ref, m_ref, v_ref, g_out, d_out, m_out, v_out):
        g = _dot_tn(c_ref[...], dc_ref[...])
        g_out[...] = g
        d_out[...], m_out[...], v_out[...] = _adamw_math(w_ref[...], g, m_ref[...], v_ref[...])

    blk = pl.BlockSpec((None, d, ADA_TN), lambda l, j: (l, 0, j))
    return pl.pallas_call(
        body, name="ada_w_update", grid=(nl, n // ADA_TN),
        in_specs=[_full((N_DEV, d)), pl.BlockSpec((None, N_DEV, ADA_TN), lambda l, j: (l, 0, j)), blk, blk, blk],
        out_specs=[blk] * 4, out_shape=[jax.ShapeDtypeStruct((nl, d, n), F32)] * 4,
        compiler_params=_cp(dimension_semantics=("parallel", "parallel")),
    )(cact, dcond_loc, w, m, v)


def _place():
    x, y, c = lax.axis_index("x"), lax.axis_index("y"), lax.axis_index("c")
    chips = [(1 - x, y), (x, 1 - y), (1 - x, 1 - y)]
    return x, y, c, chips


def _remote(src, dst, send_sem, recv_sem, to):
    return pltpu.make_async_remote_copy(src_ref=src, dst_ref=dst, send_sem=send_sem, recv_sem=recv_sem,
                                        device_id=to, device_id_type=MESH_ID)


def _sems(n):
    return [pltpu.SemaphoreType.DMA((n,)), pltpu.SemaphoreType.DMA((n,))]


def _all_gather8(v, name):
    r, cdim = v.shape

    def body(x_ref, out_ref, stage, send_sems, recv_sems):
        x, y, c, chips = _place()
        sibling = (x, y, 1 - c)

        def slot(px, py, pc):
            return out_ref.at[4 * px + 2 * py + pc]

        first = [_remote(x_ref, slot(x, y, c), send_sems.at[0], recv_sems.at[0], sibling)]
        first += [_remote(x_ref, slot(x, y, c), send_sems.at[1 + j], recv_sems.at[1 + j], (*chip, c))
                  for j, chip in enumerate(chips)]
        for cp in first:
            cp.start()
        pltpu.sync_copy(x_ref, stage)
        pltpu.sync_copy(stage, slot(x, y, c))
        passed = []
        for j, chip in enumerate(chips):
            blk = slot(*chip, c)
            _remote(blk, blk, send_sems.at[1 + j], recv_sems.at[1 + j], (x, y, c)).wait_recv()
            fw = _remote(blk, blk, send_sems.at[4 + j], recv_sems.at[4 + j], sibling)
            fw.start()
            passed.append(fw)
        blk = slot(x, y, 1 - c)
        _remote(blk, blk, send_sems.at[0], recv_sems.at[0], (x, y, c)).wait_recv()
        for j, chip in enumerate(chips):
            blk = slot(*chip, 1 - c)
            _remote(blk, blk, send_sems.at[4 + j], recv_sems.at[4 + j], (x, y, c)).wait_recv()
        for cp in first + passed:
            cp.wait_send()

    return pl.pallas_call(
        body, name=name, out_shape=jax.ShapeDtypeStruct((N_DEV, r, cdim), v.dtype),
        in_specs=[ANY], out_specs=ANY,
        scratch_shapes=[pltpu.VMEM((r, cdim), v.dtype)] + _sems(7),
        compiler_params=_cp(),
    )(v)


def _gather_first_copies():
    def make(refs, send_sems, recv_sems):
        x, y, c, chips = _place()
        mine = refs[0].at[4 * x + 2 * y + c]
        to = [(x, y, 1 - c)] + [(*chip, c) for chip in chips]
        return [_remote(mine, mine, send_sems.at[k], recv_sems.at[k], dev) for k, dev in enumerate(to)]
    return make


def _gather_pass_on(buf, name):
    def body(in_ref, out_ref, send_sems, recv_sems):
        x, y, c, chips = _place()
        passed = []
        for j, chip in enumerate(chips):
            blk = out_ref.at[4 * chip[0] + 2 * chip[1] + c]
            fw = _remote(blk, blk, send_sems.at[j], recv_sems.at[j], (x, y, 1 - c))
            fw.start()
            passed.append(fw)
        for j, chip in enumerate(chips):
            blk = out_ref.at[4 * chip[0] + 2 * chip[1] + 1 - c]
            _remote(blk, blk, send_sems.at[j], recv_sems.at[j], (x, y, c)).wait_recv()
        for fw in passed:
            fw.wait_send()

    return pl.pallas_call(
        body, name=name, out_shape=jax.ShapeDtypeStruct(buf.shape, buf.dtype),
        in_specs=[ANY], out_specs=ANY, input_output_aliases={0: 0}, scratch_shapes=_sems(3),
    )(buf)


def _place_weights(ws, layer, kidx, after):
    steps = 4
    shapes, in_specs, out_specs = [], [], []
    for w, kind in zip(ws, BIG_KINDS):
        _, a, b = w.shape
        in_specs.append(pl.BlockSpec((None, a // steps, b), lambda i, k: (layer, i, 0)))
        if kind == "col":
            shapes.append((2, a, 2 * b))
            out_specs.append(pl.BlockSpec((None, a // steps, b), lambda i, k: (k[0] // 2, i, k[0] % 2)))
        else:
            shapes.append((N_CHIP, a, b))
            out_specs.append(pl.BlockSpec((None, a // steps, b), lambda i, k: (k[0], i, 0)))

    def body(k_ref, *refs):
        outs = refs[len(ws) + 1:]
        for t in range(len(ws)):
            outs[t][...] = refs[t][...].astype(BF16)

    return pl.pallas_call(
        body, name="place_weights", out_shape=[jax.ShapeDtypeStruct(s, BF16) for s in shapes],
        grid_spec=pltpu.PrefetchScalarGridSpec(num_scalar_prefetch=1, grid=(steps,), in_specs=in_specs + [ANY],
                                               out_specs=out_specs),
        compiler_params=_cp(dimension_semantics=("parallel",)),
    )(kidx, *ws, after)


HBM = pl.BlockSpec(memory_space=pltpu.HBM)
SEM = pl.BlockSpec(memory_space=pltpu.SEMAPHORE)
EFFECT = pltpu.SideEffectType.DATAFLOW_SIDE_EFFECTING


def _weight_block(ref, kind, k, h):
    if kind == "col":
        ncol = ref.shape[3] // 2
        return ref.at[k // 2, h, :, pl.ds(pl.multiple_of((k % 2) * ncol, LANES), ncol)]
    return ref.at[k, h]


def _in_hbm(a):
    return pltpu.with_memory_space_constraint(a, pltpu.HBM)


def _weight_send_start(placed, kinds, name):
    nt = len(placed)

    def body(*refs):
        send_sems, recv_sems = refs[nt], refs[nt + 1]
        dst = refs[nt + 2:2 * nt + 2]
        token = refs[2 * nt + 2]
        x, y, c, chips = _place()
        kme = 2 * x + y
        for t in range(nt):
            for j, chip in enumerate(chips):
                own = _weight_block(dst[t], kinds[t], kme, c)
                _remote(own, own, send_sems.at[3 * t + j], recv_sems.at[3 * t + j], (*chip, c)).start()
        token[...] = jnp.zeros_like(token)

    return pl.pallas_call(
        body, name=name,
        out_shape=(pltpu.SemaphoreType.DMA((3 * nt,)), pltpu.SemaphoreType.DMA((3 * nt,)),
                   *[pltpu.HBM(a.shape, a.dtype) for a in placed], jax.ShapeDtypeStruct((8, LANES), F32)),
        in_specs=[HBM] * nt, out_specs=(SEM, SEM, *[HBM] * nt, pl.BlockSpec(memory_space=pltpu.VMEM)),
        input_output_aliases={t: 2 + t for t in range(nt)},
        compiler_params=pltpu.CompilerParams(has_side_effects=EFFECT),
    )(*[_in_hbm(a) for a in placed])


def _weight_send_wait(send_sems, recv_sems, arrays, kinds, after, name):
    nt = len(arrays)

    def body(*refs):
        arr = refs[:nt]
        send_sems, recv_sems = refs[nt], refs[nt + 1]
        x, y, c, chips = _place()
        kme = 2 * x + y
        for t in range(nt):
            for j, chip in enumerate(chips):
                own = _weight_block(arr[t], kinds[t], kme, c)
                got = _weight_block(arr[t], kinds[t], 2 * chip[0] + chip[1], c)
                cp = _remote(own, got, send_sems.at[3 * t + j], recv_sems.at[3 * t + j], (*chip, c))
                cp.wait_send()
                cp.wait_recv()

    return pl.pallas_call(
        body, name=name, out_shape=[pltpu.HBM(a.shape, a.dtype) for a in arrays],
        in_specs=[HBM] * nt + [SEM, SEM, ANY], out_specs=[HBM] * nt,
        input_output_aliases={t: t for t in range(nt)},
        compiler_params=pltpu.CompilerParams(has_side_effects=EFFECT),
    )(*arrays, send_sems, recv_sems, after)


def _forward_copies(kinds):
    def make(refs, send_sems, recv_sems):
        x, y, c, chips = _place()
        cps = []
        for t in range(len(kinds)):
            for j, chip in enumerate(chips):
                blk = _weight_block(refs[t], kinds[t], 2 * chip[0] + chip[1], c)
                cps.append(_remote(blk, blk, send_sems.at[3 * t + j], recv_sems.at[3 * t + j], (x, y, 1 - c)))
        return cps
    return make


def _split_start(name, arrays, n_copies, make_copies):
    na = len(arrays)

    def body(*refs):
        send_sems, recv_sems = refs[na], refs[na + 1]
        for cp in make_copies(refs[na + 2:2 * na + 2], send_sems, recv_sems):
            cp.start()
        token = refs[2 * na + 2]
        token[...] = jnp.zeros_like(token)

    return pl.pallas_call(
        body, name=name,
        out_shape=(pltpu.SemaphoreType.DMA((n_copies,)), pltpu.SemaphoreType.DMA((n_copies,)),
                   *[pltpu.HBM(a.shape, a.dtype) for a in arrays], jax.ShapeDtypeStruct((8, LANES), F32)),
        in_specs=[HBM] * na, out_specs=(SEM, SEM, *[HBM] * na, pl.BlockSpec(memory_space=pltpu.VMEM)),
        input_output_aliases={t: 2 + t for t in range(na)},
        compiler_params=pltpu.CompilerParams(has_side_effects=EFFECT),
    )(*[_in_hbm(a) for a in arrays])


def _split_wait(name, started, make_copies, after):
    send_sems, recv_sems, *arrays, _ = started
    na = len(arrays)

    def body(*refs):
        send_sems, recv_sems = refs[na], refs[na + 1]
        for cp in make_copies(refs[:na], send_sems, recv_sems):
            cp.wait_send()
            cp.wait_recv()

    return pl.pallas_call(
        body, name=name, out_shape=[pltpu.HBM(a.shape, a.dtype) for a in arrays],
        in_specs=[HBM] * na + [SEM, SEM, ANY], out_specs=[HBM] * na,
        input_output_aliases={t: t for t in range(na)},
        compiler_params=pltpu.CompilerParams(has_side_effects=EFFECT),
    )(*arrays, send_sems, recv_sems, after)


def _exchange_copies(nt):
    def make(refs, send_sems, recv_sems):
        x, y, c, _ = _place()
        return [_remote(refs[t].at[:, 1 - c], refs[nt + t], send_sems.at[t], recv_sems.at[t], (x, y, 1 - c))
                for t in range(nt)]
    return make


def _sibling_exchange_start(views, name):
    lands = [lax.empty((v.shape[0],) + v.shape[2:], v.dtype) for v in views]
    return _split_start(name, list(views) + lands, len(views), _exchange_copies(len(views)))


def _sibling_exchange_wait(started, after, name):
    nt = (len(started) - 3) // 2
    outs = _split_wait(name, started, _exchange_copies(nt), after)
    return outs[:nt], outs[nt:]


def _scatter_copies(src, land, kinds, send_sems, recv_sems):
    x, y, c, chips = _place()
    cps = []
    for t in range(len(src)):
        for j, chip in enumerate(chips):
            k = 2 * chip[0] + chip[1]
            if kinds[t] == "col":
                ncol = land[t].shape[2]
                win = src[t].at[k // 2, :, pl.ds(pl.multiple_of((k % 2) * ncol, LANES), ncol)]
            else:
                win = src[t].at[k]
            cps.append(_remote(win, land[t].at[j], send_sems.at[3 * t + j], recv_sems.at[3 * t + j], (*chip, c)))
    return cps


def _chip_scatter_start(parts, kinds, name):
    nt = len(parts)
    shapes = []
    for p, kind in zip(parts, kinds):
        shapes.append((3, p.shape[1], p.shape[2] // 2) if kind == "col" else (3,) + p.shape[1:])

    def body(*refs):
        send_sems, recv_sems = refs[2 * nt], refs[2 * nt + 1]
        src, land = refs[2 * nt + 2:3 * nt + 2], refs[3 * nt + 2:4 * nt + 2]
        token = refs[4 * nt + 2]
        for cp in _scatter_copies(src, land, kinds, send_sems, recv_sems):
            cp.start()
        token[...] = jnp.zeros_like(token)

    lands = [lax.empty(s, BF16) for s in shapes]
    return pl.pallas_call(
        body, name=name,
        out_shape=(pltpu.SemaphoreType.DMA((3 * nt,)), pltpu.SemaphoreType.DMA((3 * nt,)),
                   *[pltpu.HBM(a.shape, a.dtype) for a in parts], *[pltpu.HBM(s, BF16) for s in shapes],
                   jax.ShapeDtypeStruct((8, LANES), F32)),
        in_specs=[HBM] * (2 * nt), out_specs=(SEM, SEM, *[HBM] * (2 * nt), pl.BlockSpec(memory_space=pltpu.VMEM)),
        input_output_aliases={t: 2 + t for t in range(2 * nt)},
        compiler_params=pltpu.CompilerParams(has_side_effects=EFFECT),
    )(*[_in_hbm(a) for a in parts], *[_in_hbm(a) for a in lands])


def _chip_scatter_wait(send_sems, recv_sems, parts, lands, kinds, after, name):
    nt = len(parts)

    def body(*refs):
        src, land = refs[:nt], refs[nt:2 * nt]
        send_sems, recv_sems = refs[2 * nt], refs[2 * nt + 1]
        for cp in _scatter_copies(src, land, kinds, send_sems, recv_sems):
            cp.wait_send()
            cp.wait_recv()

    outs = pl.pallas_call(
        body, name=name, out_shape=[pltpu.HBM(a.shape, a.dtype) for a in list(parts) + list(lands)],
        in_specs=[HBM] * (2 * nt) + [SEM, SEM, ANY], out_specs=[HBM] * (2 * nt),
        input_output_aliases={t: t for t in range(2 * nt)},
        compiler_params=pltpu.CompilerParams(has_side_effects=EFFECT),
    )(*parts, *lands, send_sems, recv_sems, after)
    return outs[:nt], outs[nt:]


def _share_copies(nt):
    def make(refs, send_sems, recv_sems):
        x, y, c, _ = _place()
        return [_remote(refs[t].at[c], refs[t].at[c], send_sems.at[t], recv_sems.at[t], (x, y, 1 - c))
                for t in range(nt)]
    return make


def _sibling_share_start(fulls, name):
    return _split_start(name, list(fulls), len(fulls), _share_copies(len(fulls)))


def _sibling_share_wait(started, after, name):
    return _split_wait(name, started, _share_copies(len(started) - 3), after)


SUM_STEPS = 4


def _pair_sum(views, lands, ck):
    nt = len(views)
    in_specs, out_specs, shapes = [], [], []
    for v in views:
        b, _, r, cc = v.shape
        per = SUM_STEPS // b
        tr = r // per
        in_specs.append(pl.BlockSpec((None, None, tr, cc), lambda i, s, per=per: (i // per, s[0], i % per, 0)))
        out_specs.append(pl.BlockSpec((None, tr, cc), lambda i, s, per=per: (i // per, i % per, 0)))
        shapes.append((b, r, cc))
    in_specs = in_specs + out_specs

    def body(s_ref, *refs):
        for t in range(nt):
            refs[2 * nt + t][...] = (refs[t][...].astype(F32) + refs[nt + t][...].astype(F32)).astype(BF16)

    return pl.pallas_call(
        body, name="grad_pair_sum", out_shape=[jax.ShapeDtypeStruct(s, BF16) for s in shapes],
        grid_spec=pltpu.PrefetchScalarGridSpec(num_scalar_prefetch=1, grid=(SUM_STEPS,), in_specs=in_specs,
                                               out_specs=out_specs),
        compiler_params=_cp(dimension_semantics=("parallel",)),
    )(ck, *views, *lands)


def _chip_sum(parts, lands, kinds, ck):
    nt = len(parts)
    steps = 2
    in_own, in_land, out_specs, shapes = [], [], [], []
    for ld, kind in zip(lands, kinds):
        _, r, cc = ld.shape
        tr = r // steps
        if kind == "col":
            in_own.append(pl.BlockSpec((None, tr, cc), lambda i, s: (s[1] // 2, i, s[1] % 2)))
        else:
            in_own.append(pl.BlockSpec((None, tr, cc), lambda i, s: (s[1], i, 0)))
        in_land.append(pl.BlockSpec((3, tr, cc), lambda i, s: (0, i, 0)))
        out_specs.append(pl.BlockSpec((None, tr, cc), lambda i, s: (s[0], i, 0)))
        shapes.append((2, r, cc))

    def body(s_ref, *refs):
        for t in range(nt):
            acc = refs[t][...].astype(F32)
            for j in range(3):
                acc = acc + refs[nt + t][j].astype(F32)
            refs[2 * nt + t][...] = acc

    return pl.pallas_call(
        body, name="grad_chip_sum", out_shape=[jax.ShapeDtypeStruct(s, F32) for s in shapes],
        grid_spec=pltpu.PrefetchScalarGridSpec(num_scalar_prefetch=1, grid=(steps,), in_specs=in_own + in_land,
                                               out_specs=out_specs),
        compiler_params=_cp(dimension_semantics=("parallel",)),
    )(ck, *parts, *lands)


def _sum8(g):
    _, r, cc = g.shape
    tr = _row_tile(r, N_DEV * cc)

    def body(g_ref, o_ref):
        acc = g_ref[0].astype(F32)
        for d in range(1, N_DEV):
            acc = acc + g_ref[d].astype(F32)
        o_ref[...] = acc

    return pl.pallas_call(
        body, name="small_grad_sum", grid=(r // tr,),
        in_specs=[pl.BlockSpec((N_DEV, tr, cc), lambda i: (0, i, 0))],
        out_specs=pl.BlockSpec((tr, cc), lambda i: (i, 0)),
        out_shape=jax.ShapeDtypeStruct((r, cc), F32),
        compiler_params=_cp(dimension_semantics=("parallel",)),
    )(g)


def _silu_rows(c):
    def body(c_ref, o_ref):
        v = c_ref[...]
        o_ref[...] = v * jax.nn.sigmoid(v)

    return pl.pallas_call(body, name="cond_silu", out_shape=jax.ShapeDtypeStruct(c.shape, F32))(c)


def _pack(arrays):
    rows = []
    for a in arrays:
        flat = a.reshape(-1)
        rows.append(jnp.pad(flat, (0, (-flat.shape[0]) % (8 * LANES))).reshape(-1, LANES))
    n = sum(r.shape[0] for r in rows)
    if n % 256:
        rows.append(jnp.zeros((256 - n % 256, LANES), rows[0].dtype))
    return jnp.concatenate(rows, axis=0)


def _unpack(packed, shapes):
    out, off = [], 0
    for s in shapes:
        n = math.prod(s)
        nr = 8 * -(-n // (8 * LANES))
        out.append(packed[off:off + nr].reshape(-1)[:n].reshape(s))
        off += nr
    return out


def _as_rows(a):
    return a.reshape(1, -1) if a.ndim == 1 else a.reshape(-1, a.shape[-1])


def _adamw_many(ws, gs, ms, vs, name, steps=1):
    nt = len(ws)

    def body(*refs):
        for t in range(nt):
            w_ref, g_ref, m_ref, v_ref = (refs[k * nt + t] for k in range(4))
            d, m, v = _adamw_math(w_ref[...], g_ref[...], m_ref[...], v_ref[...])
            refs[4 * nt + t][...] = d
            refs[5 * nt + t][...] = m
            refs[6 * nt + t][...] = v

    shapes = [jax.ShapeDtypeStruct(a.shape, F32) for a in ws]
    if steps == 1:
        outs = pl.pallas_call(body, name=name, out_shape=shapes * 3, compiler_params=_cp())(*ws, *gs, *ms, *vs)
    else:
        specs = [pl.BlockSpec((a.shape[0] // steps, a.shape[1]), lambda i: (i, 0)) for a in ws]
        outs = pl.pallas_call(
            body, name=name, grid=(steps,), in_specs=specs * 4, out_specs=specs * 3, out_shape=shapes * 3,
            compiler_params=_cp(dimension_semantics=("parallel",)),
        )(*ws, *gs, *ms, *vs)
    return outs[:nt], outs[nt:2 * nt], outs[2 * nt:]


def _exchange_big_grads(grads, kinds, layer):
    views = []
    for g, kind in zip(grads, kinds):
        if kind == "col":
            views.append(g.reshape(2, 2, g.shape[1] // 2, g.shape[2]))
        else:
            views.append(g.reshape(N_CHIP, 2, g.shape[0] // (2 * N_CHIP), g.shape[1]))
    return _sibling_exchange_start(views, "grad_exchange_start_%d" % layer)


def _scatter_big_grads(exchanged, kinds, ck, after, layer):
    views, lands = _sibling_exchange_wait(exchanged, after, "grad_exchange_wait_%d" % layer)
    parts = _pair_sum(views, lands, ck)
    return _chip_scatter_start(parts, kinds, "grad_scatter_start_%d" % layer)


def _finish_big_grads(started, kinds, ck, after, layer):
    nt = len(kinds)
    send_sems, recv_sems = started[0], started[1]
    parts, lands = started[2:2 + nt], started[2 + nt:2 + 2 * nt]
    parts, lands = _chip_scatter_wait(send_sems, recv_sems, parts, lands, kinds, after, "grad_scatter_wait_%d" % layer)
    return _sibling_share_start(_chip_sum(parts, lands, kinds, ck), "grad_share_start_%d" % layer)


def _adamw_layer(ws, gs, ms, vs, stacks, layer, name, steps):
    nt = len(ws)
    stacks = [s if s is not None else tuple(lax.empty(w.shape, F32) for _ in range(4)) for s, w in zip(stacks, ws)]

    def body(*refs):
        for t in range(nt):
            w_ref, g_ref, m_ref, v_ref = (refs[k * nt + t] for k in range(4))
            outs = refs[8 * nt + 4 * t:8 * nt + 4 * t + 4]
            g = g_ref[...]
            outs[0][...] = g
            outs[1][...], outs[2][...], outs[3][...] = _adamw_math(w_ref[...], g, m_ref[...], v_ref[...])

    in_specs, g_specs, out_specs = [], [], []
    for w in ws:
        _, r, c = w.shape
        in_specs.append(pl.BlockSpec((None, r // steps, c), lambda i: (layer, i, 0)))
        g_specs.append(pl.BlockSpec((r // steps, c), lambda i: (i, 0)))
        out_specs += [pl.BlockSpec((None, r // steps, c), lambda i: (layer, i, 0))] * 4
    in_specs = in_specs + g_specs + in_specs * 2 + [ANY] * (4 * nt)
    flat = [a for s in stacks for a in s]
    outs = pl.pallas_call(
        body, name=name, grid=(steps,), in_specs=in_specs, out_specs=out_specs,
        out_shape=[jax.ShapeDtypeStruct(a.shape, F32) for a in flat],
        input_output_aliases={4 * nt + k: k for k in range(4 * nt)},
        compiler_params=_cp(dimension_semantics=("parallel",)),
    )(*ws, *gs, *ms, *vs, *flat)
    return [tuple(outs[4 * t:4 * t + 4]) for t in range(nt)]


SMALL_NAMES = ["ada_b", "norm1_g", "norm2_g", "sgu_w", "sgu_b", "pool_w", "pool_scale", "conv_w", "s5_lambda_re",
               "s5_lambda_im", "s5_b_re", "s5_b_im", "s5_c_re", "s5_c_im", "s5_d", "s5_log_dt", "s5_glu_w", "s5_glu_b",
               "mix_norm_g", "norm3_g", "final_norm_g"]
BIG_NAMES = ["ffn1_w_in", "ffn1_w_out", "w_mix_in", "w_mix_out", "ffn2_w_in", "ffn2_w_out"]
BIG_KINDS = ["col", "row", "row", "row", "col", "row"]
WEIGHT_ORDER = ["ada_w", "ada_b", "norm1_g", "ffn1_w_in", "ffn1_w_out", "norm2_g", "w_mix_in", "sgu_w", "sgu_b", "pool_w",
                "pool_scale", "conv_w", "s5_lambda_re", "s5_lambda_im", "s5_b_re", "s5_b_im", "s5_c_re", "s5_c_im", "s5_d",
                "s5_log_dt", "s5_glu_w", "s5_glu_b", "mix_norm_g", "w_mix_out", "norm3_g", "ffn2_w_in", "ffn2_w_out",
                "final_norm_g"]


def _local_step(x, target, cond, fetch_weights, prefetch_weights, p, emit_grads):
    nl, d = DEPTH, x.shape[1]
    row = lambda a: a.reshape(1, -1)
    saved = []
    for l in range(nl):
        (wi1, wo1, wmit, wmo, wi2, wo2), tok = fetch_weights(l, x)
        cl = cond[l] + tok
        mod1, mod2, mod3 = cl[0:3], cl[3:6], cl[6:9]
        lre, lim = p["s5_lambda_re"][l].reshape(-1), p["s5_lambda_im"][l].reshape(-1)
        ldt = jnp.repeat(p["s5_log_dt"][l], 64)
        rowp = jnp.stack([lre, lim, ldt])
        sp = (rowp, rowp.T, p["s5_b_re"][l].reshape(N_STATE, 16), p["s5_b_im"][l].reshape(N_STATE, 16),
              p["s5_c_re"][l].reshape(W_GRP, 64), p["s5_c_im"][l].reshape(W_GRP, 64), row(p["s5_d"][l]))
        glu = (p["s5_glu_w"][l], row(p["s5_glu_b"][l]))
        bias_full = jnp.repeat(p["sgu_b"][l].T, 64, axis=1)
        pw2 = p["pool_w"][l].reshape(W_GRP, 64)
        x1, h1, a1, b1, o1 = _ffn_fwd(x, mod1, row(p["norm1_g"][l]), wi1, wo1)
        z, h2 = _mix_in_fwd(x1, mod2, row(p["norm2_g"][l]), wmit)
        ya = _sgu_fwd(z, p["sgu_w"][l], bias_full)
        yb, yc = _poolconv_fwd(z, pw2, row(p["pool_scale"][l]), p["conv_w"][l])
        ys = (ya, yb, yc, _s5_core_fwd(z, sp))
        x2, m = _mix_out_fwd(ys, glu, row(p["mix_norm_g"][l]), wmo, x1, mod2[2:3])
        mod3 = mod3 + prefetch_weights(l + 1, x2)
        x3, h3, a3, b3, o3 = _ffn_fwd(x2, mod3, row(p["norm3_g"][l]), wi2, wo2)
        saved.append((x, x1, x2, h1, a1, b1, o1, z, h2, ys, m, h3, a3, b3, o3, sp, bias_full, pw2, glu,
                      (wi1, wo1, wmit, wmo, wi2, wo2), cl))
        x = x3

    loss, dx, dfg = _loss_head(x, row(p["final_norm_g"]), target)

    sg = {n: [None] * nl for n in SMALL_NAMES if n not in ("ada_b", "final_norm_g")}
    dcond = [None] * nl
    s5_da, s5_dk = [None] * nl, [None] * nl
    tok = 0.0
    for l in reversed(range(nl)):
        (x0, x1, x2, h1, a1, b1, o1, z, h2, ys, m, h3, a3, b3, o3, sp, bias_full, pw2, glu,
         (wi1, wo1, wmit, wmo, wi2, wo2), cl) = saved[l]
        cl = cl + tok
        mod1, mod2, mod3 = cl[0:3], cl[3:6], cl[6:9]
        dza, dzb, dwi2, dwo2, dgate3 = _ffn_bwd_main(dx, o3, mod3[2:3], h3, a3, b3, wo2)
        dx, rows3 = _ffn_bwd_in(dza, dzb, wi2, x2, dx, mod3, row(p["norm3_g"][l]))
        outs = _mix_out_bwd(dx, m, mod2[2:3], ys, glu, row(p["mix_norm_g"][l]), wmo)
        dys, dgate2, dmng, dwmo, dgw, dgb = outs[0:4], outs[4], outs[5], outs[6], outs[7], outs[8]
        dza_, dsw, dsb = _sgu_bwd(z, dys[0], p["sgu_w"][l], bias_full)
        dzb_, dzc_, dwbd, dps, dcw = _poolconv_bwd(z, dys[1], dys[2], pw2, row(p["pool_scale"][l]), p["conv_w"][l])
        dzd_, dbr, dbi, dcr, dci, dd, da, dk = _s5_core_bwd(z, dys[3], sp)
        dx, rows2, dwmit = _mix_in_bwd((dza_, dzb_, dzc_, dzd_), h2, wmit, x1, dx, mod2, row(p["norm2_g"][l]))
        dza, dzb, dwi1, dwo1, dgate1 = _ffn_bwd_main(dx, o1, mod1[2:3], h1, a1, b1, wo1)
        tok, layer_done = emit_grads(l, [dwi1, dwo1, dwmit, dwmo, dwi2, dwo2])
        dx, rows1 = _ffn_bwd_in(dza, dzb, wi1, x0, dx, mod1 + tok, row(p["norm1_g"][l]))
        if l > 0:
            tok = layer_done(dx)[0, 0]
        dcond[l] = jnp.concatenate([rows1[0:2], dgate1, rows2[0:2], dgate2, rows3[0:2], dgate3], axis=0)
        sg["norm1_g"][l], sg["norm2_g"][l], sg["norm3_g"][l] = rows1[2], rows2[2], rows3[2]
        sg["mix_norm_g"][l] = dmng[0]
        sg["sgu_w"][l] = dsw
        sg["sgu_b"][l] = dsb[:, 0:4].T
        g4 = dwbd.reshape(4, 64, 4, 64)
        sg["pool_w"][l] = jnp.stack([g4[k, :, k, :] for k in range(4)])
        sg["pool_scale"][l] = dps[0]
        sg["conv_w"][l] = dcw[0:3]
        sg["s5_b_re"][l], sg["s5_b_im"][l] = dbr.reshape(16, 64, 16), dbi.reshape(16, 64, 16)
        sg["s5_c_re"][l], sg["s5_c_im"][l] = dcr.reshape(16, 16, 64), dci.reshape(16, 16, 64)
        sg["s5_d"][l] = dd[0]
        sg["s5_glu_w"][l], sg["s5_glu_b"][l] = dgw, dgb[0]
        s5_da[l], s5_dk[l] = da, dk

    n16 = nl * 16
    dlre, dlim, dldt = _s5_param_bwd(
        p["s5_lambda_re"].reshape(n16, 64), p["s5_lambda_im"].reshape(n16, 64),
        jnp.repeat(p["s5_log_dt"].reshape(n16, 1), 64, axis=1),
        jnp.stack([a[0] for a in s5_da]).reshape(n16, 64), jnp.stack([a[1] for a in s5_da]).reshape(n16, 64),
        jnp.stack([k[:, 0] for k in s5_dk]).reshape(n16, 64), jnp.stack([k[:, 1] for k in s5_dk]).reshape(n16, 64))
    small = {n: jnp.stack(v) for n, v in sg.items() if v[0] is not None}
    small["s5_lambda_re"] = dlre.reshape(nl, 16, 64)
    small["s5_lambda_im"] = dlim.reshape(nl, 16, 64)
    small["s5_log_dt"] = dldt.reshape(nl, 16)
    small["final_norm_g"] = dfg[0]
    return loss, dx, small, jnp.stack(dcond), layer_done


def kernel(x, c, ada_w, ada_b, norm1_g, ffn1_w_in, ffn1_w_out, norm2_g, w_mix_in, sgu_w, sgu_b, pool_w, pool_scale, conv_w, s5_lambda_re, s5_lambda_im, s5_b_re, s5_b_im, s5_c_re, s5_c_im, s5_d, s5_log_dt, s5_glu_w, s5_glu_b, mix_norm_g, w_mix_out, norm3_g, ffn2_w_in, ffn2_w_out, final_norm_g, loss_target, m_ada_w, m_ada_b, m_norm1_g, m_ffn1_w_in, m_ffn1_w_out, m_norm2_g, m_w_mix_in, m_sgu_w, m_sgu_b, m_pool_w, m_pool_scale, m_conv_w, m_s5_lambda_re, m_s5_lambda_im, m_s5_b_re, m_s5_b_im, m_s5_c_re, m_s5_c_im, m_s5_d, m_s5_log_dt, m_s5_glu_w, m_s5_glu_b, m_mix_norm_g, m_w_mix_out, m_norm3_g, m_ffn2_w_in, m_ffn2_w_out, m_final_norm_g, v_ada_w, v_ada_b, v_norm1_g, v_ffn1_w_in, v_ffn1_w_out, v_norm2_g, v_w_mix_in, v_sgu_w, v_sgu_b, v_pool_w, v_pool_scale, v_conv_w, v_s5_lambda_re, v_s5_lambda_im, v_s5_b_re, v_s5_b_im, v_s5_c_re, v_s5_c_im, v_s5_d, v_s5_log_dt, v_s5_glu_w, v_s5_glu_b, v_mix_norm_g, v_w_mix_out, v_norm3_g, v_ffn2_w_in, v_ffn2_w_out, v_final_norm_g):
    args = dict(locals())
    w = {n: args[n] for n in WEIGHT_ORDER}
    mom = {n: args["m_" + n] for n in WEIGHT_ORDER}
    vel = {n: args["v_" + n] for n in WEIGHT_ORDER}
    nl, d = DEPTH, x.shape[-1]
    s = x.shape[1]
    px, py, pc = lax.axis_index("x"), lax.axis_index("y"), lax.axis_index("c")
    kme = 2 * px + py
    me = 2 * kme + pc
    kidx = jnp.reshape(kme, (1,)).astype(jnp.int32)

    shards = [ffn1_w_in, ffn1_w_out, jnp.swapaxes(w_mix_in, 1, 2), w_mix_out, ffn2_w_in, ffn2_w_out]
    started_weights = {}

    def start_weights(l, after):
        placed = _place_weights(shards, l, kidx, after)
        views = [a.reshape(a.shape[0], 2, a.shape[1] // 2, a.shape[2]) for a in placed]
        *handles, token = _weight_send_start(views, BIG_KINDS, "weight_send_start_%d" % l)
        started_weights[l] = handles
        return token

    cact = _silu_rows(c)
    pre = _pack([cact, conv_w, s5_glu_w])
    pre_all = _all_gather8(pre, "gather_prelude")
    token = start_weights(0, pre_all)
    parts = [_unpack(pre_all[dev], [cact.shape, conv_w.shape, s5_glu_w.shape]) for dev in range(N_DEV)]
    cact_all = pre_all[:, :d // LANES, :].reshape(N_DEV, d)
    conv_full = jnp.concatenate([parts[2 * k][1] for k in range(N_CHIP)], axis=2)
    glu_full = jnp.concatenate([parts[2 * k][2] for k in range(N_CHIP)], axis=1)

    n_ada = ada_w.shape[2]
    ada_b_loc = lax.dynamic_slice_in_dim(ada_b, kme * n_ada, n_ada, axis=1).reshape(nl, 1, n_ada) + token[0, 0]
    cond_part = _cond_fwd(cact_all, ada_w, ada_b_loc)
    cond_all = _all_gather8(cond_part.reshape(nl * N_DEV, n_ada), "gather_cond").reshape(N_DEV, nl, N_DEV, n_ada)
    cond_me = jnp.concatenate(
        [lax.dynamic_index_in_dim(cond_all[2 * k], me, axis=1, keepdims=False) for k in range(N_CHIP)], axis=1)
    token = cond_all
    for l in range(1, nl):
        token = start_weights(l, token)
    cond = cond_me.reshape(nl, 9, d) + token[0, 0]

    forwarding = {}

    def prefetch_weights(l, after):
        if l >= nl:
            return 0.0
        send_sems, recv_sems, *views = started_weights.pop(l)
        views = _weight_send_wait(send_sems, recv_sems, views, BIG_KINDS, after, "weight_send_wait_%d" % l)
        forwarding[l] = _split_start("weight_forward_start_%d" % l, views, 3 * len(views), _forward_copies(BIG_KINDS))
        return forwarding[l][-1][0, 0]

    def fetch_weights(l, after):
        if l not in forwarding:
            prefetch_weights(l, after)
        views = _split_wait("weight_forward_wait_%d" % l, forwarding.pop(l), _forward_copies(BIG_KINDS), after)
        full = [v.reshape(2, 2 * v.shape[2], v.shape[3]) if kind == "col" else v.reshape(-1, v.shape[3])
                for v, kind in zip(views, BIG_KINDS)]
        return full, 0.0

    ck = jnp.stack([pc, kme]).astype(jnp.int32)
    scattering, sharing = [], []
    stacks = {n: None for n in BIG_NAMES}
    groups = ((["ffn1_w_in", "ffn2_w_in"], 16, "adamw_w_in"),
              (["ffn1_w_out", "w_mix_in", "w_mix_out", "ffn2_w_out"], 8, "adamw_w_out"))

    def as_reduced(t):
        return {n: jnp.swapaxes(t[n], 1, 2) if n == "w_mix_in" else t[n] for n in BIG_NAMES}

    w_r, m_r, v_r = as_reduced(w), as_reduced(mom), as_reduced(vel)

    def apply_adamw(l, fulls):
        g = {n: f.reshape(2 * f.shape[1], f.shape[2]) for n, f in zip(BIG_NAMES, fulls)}
        for names, steps, call in groups:
            outs = _adamw_layer([w_r[n] for n in names], [g[n] for n in names], [m_r[n] for n in names],
                                [v_r[n] for n in names], [stacks[n] for n in names], l, call, steps)
            stacks.update(zip(names, outs))

    def retire_share(after):
        l2, shared = sharing.pop(0)
        apply_adamw(l2, _sibling_share_wait(shared, after, "grad_share_wait_%d" % l2))

    def retire_scatter(after):
        l1, scattered = scattering.pop(0)
        sharing.append((l1, _finish_big_grads(scattered, BIG_KINDS, ck, after, l1)))

    def retire(after):
        if sharing:
            retire_share(after)
        if scattering:
            retire_scatter(after)

    def emit_grads(l, grads_l):
        exchanged = _exchange_big_grads(grads_l, BIG_KINDS, l)

        def layer_done(after):
            started = _scatter_big_grads(exchanged, BIG_KINDS, ck, after, l)
            retire(after)
            scattering.append((l, started))
            return started[-1]

        return exchanged[-1][0, 0], layer_done

    p = {n: w[n] for n in SMALL_NAMES}
    p["conv_w"], p["s5_glu_w"] = conv_full, glu_full
    loss, dx, small, dcond, first_layer_done = _local_step(x[0], loss_target[0], cond, fetch_weights, prefetch_weights,
                                                           p, emit_grads)

    small_order = [n for n in SMALL_NAMES if n != "ada_b"]
    packed = _pack([dcond] + [small[n] for n in small_order]).astype(BF16)
    mine = lax.dynamic_update_slice(lax.empty((N_DEV,) + packed.shape, BF16), packed[None], (me, 0, 0))
    gathering = _split_start("small_grads_send_start", [mine], 4, _gather_first_copies())
    scatter_token = first_layer_done(gathering[-1])
    while sharing:
        retire_share(scatter_token)
    arrived, = _split_wait("small_grads_send_wait", gathering, _gather_first_copies(), stacks[BIG_NAMES[0]][0])
    gathered_small = _gather_pass_on(arrived, "small_grads_pass_on")
    total = _sum8(gathered_small)
    shapes = [dcond.shape] + [small[n].shape for n in small_order]
    tot = dict(zip(["ada_b"] + small_order, _unpack(total, shapes)))
    grads = {n: tot[n] for n in SMALL_NAMES}
    grads["ada_b"] = tot["ada_b"].reshape(nl, 9 * d)
    grads["conv_w"] = lax.dynamic_slice_in_dim(tot["conv_w"], kme * conv_w.shape[2], conv_w.shape[2], axis=2)
    grads["s5_glu_w"] = lax.dynamic_slice_in_dim(tot["s5_glu_w"], kme * s5_glu_w.shape[1], s5_glu_w.shape[1], axis=1)

    dcond_all = gathered_small.reshape(N_DEV, -1)[:, :dcond.size].reshape(N_DEV, nl, 9 * d)
    dcond_loc = jnp.swapaxes(lax.dynamic_slice_in_dim(dcond_all, kme * n_ada, n_ada, axis=2), 0, 1)
    g_ada, d_ada, m_ada, v_ada = _ada_w_update(cact_all, dcond_loc, ada_w, m_ada_w, v_ada_w)

    while scattering or sharing:
        retire(g_ada)
    delta, new_m, new_v = {}, {}, {}
    for n in BIG_NAMES:
        grads[n], delta[n], new_m[n], new_v[n] = (jnp.swapaxes(a, 1, 2) if n == "w_mix_in" else a for a in stacks[n])

    grads["ada_w"], delta["ada_w"], new_m["ada_w"], new_v["ada_w"] = g_ada, d_ada, m_ada, v_ada
    wide = ("s5_b_re", "s5_b_im")
    for names, call, steps in (([n for n in SMALL_NAMES if n not in wide], "adamw_small", 1),
                               (list(wide), "adamw_s5_b", DEPTH)):
        outs = _adamw_many(*[[_as_rows(t[n]) for n in names] for t in (w, grads, mom, vel)], call, steps)
        for res, o in zip((delta, new_m, new_v), outs):
            res.update({n: a.reshape(w[n].shape) for n, a in zip(names, o)})

    loss_total = lax.psum(loss[0, 0], ("x", "y", "c"))
    return (loss_total, dx[None], *[grads[n] for n in WEIGHT_ORDER], *[delta[n] for n in WEIGHT_ORDER],
            *[new_m[n] for n in WEIGHT_ORDER], *[new_v[n] for n in WEIGHT_ORDER])
```

```python
import math

import jax
import jax.numpy as jnp
from jax import lax
from jax.experimental import pallas as pl
from jax.experimental.pallas import tpu as pltpu

F32, BF16 = jnp.float32, jnp.bfloat16
EPS = 1e-6
DEPTH = 4
N_DEV = 8
N_CHIP = 4
W_GRP = 256
CHUNK = 128
N_SEG = 8
LANES = 128
FFN_TF = 256
FFN_TF_WIDE = 1408
FFN_TM_WIDE = 512
VMEM_LIMIT = 56 * 1024 * 1024
ADAM_LR, ADAM_B1, ADAM_B2, ADAM_EPS, ADAM_WD, ADAM_STEP = 0.001, 0.9, 0.999, 1e-08, 0.01, 10
MESH_ID = pl.DeviceIdType.MESH
HI = lax.Precision.HIGHEST
ANY = pl.BlockSpec(memory_space=pl.ANY)


def _cp(**kw):
    return pltpu.CompilerParams(vmem_limit_bytes=VMEM_LIMIT, **kw)


def _dot(a, b):
    return jnp.dot(a.astype(BF16), b.astype(BF16), preferred_element_type=F32)


def _dot_nt(a, b):
    return lax.dot_general(a.astype(BF16), b.astype(BF16), (((1,), (1,)), ((), ())), preferred_element_type=F32)


def _dot_tn(a, b):
    return lax.dot_general(a.astype(BF16), b.astype(BF16), (((0,), (0,)), ((), ())), preferred_element_type=F32)


def _dot_hi(a, b):
    return jnp.dot(a, b, preferred_element_type=F32, precision=HI)


def _gelu(x):
    k = 0.7978845608028654
    t = jnp.tanh(k * (x + 0.044715 * x * x * x))
    return 0.5 * x * (1.0 + t), t


def _gelu_grad(x, t):
    k = 0.7978845608028654
    return 0.5 * (1.0 + t) + 0.5 * x * (1.0 - t * t) * k * (1.0 + 3.0 * 0.044715 * x * x)


def _iota(shape, axis):
    return lax.broadcasted_iota(jnp.int32, shape, axis)


def _full(shape):
    nd = len(shape)
    return pl.BlockSpec(shape, lambda *_: (0,) * nd)


def _norm_mod(xv, g, shift, scale):
    r = lax.rsqrt(jnp.mean(xv * xv, axis=-1, keepdims=True) + EPS)
    return (xv * r * g) * (1.0 + scale) + shift


def _norm_mod_bwd(xv, g, scale, dh):
    r = lax.rsqrt(jnp.mean(xv * xv, axis=-1, keepdims=True) + EPS)
    xh = xv * r
    n = xh * g
    dsh = jnp.sum(dh, axis=0, keepdims=True)
    dsc = jnp.sum(dh * n, axis=0, keepdims=True)
    dn = dh * (1.0 + scale)
    dg = jnp.sum(dn * xh, axis=0, keepdims=True)
    dxh = dn * g
    dx = r * (dxh - xh * jnp.mean(dxh * xh, axis=-1, keepdims=True))
    return dx, dsh, dsc, dg


def _tm(s):
    return min(s, 1024)


def _ffn_fwd(x, mod, g, wi, wo):
    s, d = x.shape
    f = wo.shape[0]
    tf, tm = FFN_TF_WIDE, min(s, FFN_TM_WIDE)
    nf, nt = f // tf, s // tm

    def body(x_ref, mod_ref, g_ref, wa_ref, wb_ref, wo_ref, xn_ref, h_ref, a_ref, b_ref, o_ref, acc):
        j = pl.program_id(1)

        @pl.when(j == 0)
        def _():
            hh = _norm_mod(x_ref[...], g_ref[...], mod_ref[0:1, :], mod_ref[1:2, :])
            h_ref[...] = hh.astype(BF16)
            acc[...] = jnp.zeros_like(acc)

        h = h_ref[...]
        a = jnp.dot(h, wa_ref[...], preferred_element_type=F32)
        b = jnp.dot(h, wb_ref[...], preferred_element_type=F32)
        a_ref[...] = a.astype(BF16)
        b_ref[...] = b.astype(BF16)
        u = (a * jax.nn.sigmoid(a)) * b
        acc[...] += jnp.dot(u.astype(BF16), wo_ref[...], preferred_element_type=F32)

        @pl.when(j == nf - 1)
        def _():
            o = acc[...]
            o_ref[...] = o.astype(BF16)
            xn_ref[...] = x_ref[...] + 0.5 * mod_ref[2:3, :] * o

    row = pl.BlockSpec((tm, d), lambda i, j: (i, 0))
    chunk = pl.BlockSpec((tm, tf), lambda i, j: (i, j))
    return pl.pallas_call(
        body, name="ffn_fwd", grid=(nt, nf),
        in_specs=[row, _full((3, d)), _full((1, d)),
                  pl.BlockSpec((None, d, tf), lambda i, j: (0, 0, j)),
                  pl.BlockSpec((None, d, tf), lambda i, j: (1, 0, j)),
                  pl.BlockSpec((tf, d), lambda i, j: (j, 0))],
        out_specs=[row, row, chunk, chunk, row],
        out_shape=[jax.ShapeDtypeStruct((s, d), F32), jax.ShapeDtypeStruct((s, d), BF16),
                   jax.ShapeDtypeStruct((s, f), BF16), jax.ShapeDtypeStruct((s, f), BF16),
                   jax.ShapeDtypeStruct((s, d), BF16)],
        scratch_shapes=[pltpu.VMEM((tm, d), F32)],
        compiler_params=_cp(dimension_semantics=("parallel", "arbitrary")),
    )(x, mod, g, wi, wi, wo)


def _ffn_bwd_main(dxo, o, gate, h, a, b, wo):
    s, d = dxo.shape
    f = wo.shape[0]
    tf = FFN_TF
    nf = f // tf

    def body(dxo_ref, o_ref, gate_ref, h_ref, a_ref, b_ref, wo_ref, dza_ref, dzb_ref, dwi_ref, dwo_ref, dg_ref, do_s):
        @pl.when(pl.program_id(0) == 0)
        def _():
            dxv = dxo_ref[...]
            do_s[...] = (0.5 * gate_ref[...] * dxv).astype(BF16)
            dg_ref[...] = 0.5 * jnp.sum(o_ref[...].astype(F32) * dxv, axis=0, keepdims=True)

        dov = do_s[...]
        hv = h_ref[...]
        du = lax.dot_general(dov, wo_ref[...], (((1,), (1,)), ((), ())), preferred_element_type=F32)
        av = a_ref[...].astype(F32)
        bv = b_ref[...].astype(F32)
        sa = jax.nn.sigmoid(av)
        si = av * sa
        u = (si * bv).astype(BF16)
        da = (du * bv * (sa * (1.0 + av * (1.0 - sa)))).astype(BF16)
        db = (du * si).astype(BF16)
        dza_ref[...] = da
        dzb_ref[...] = db
        dwo_ref[...] = _dot_tn(u, dov).astype(BF16)
        dwi_ref[0] = _dot_tn(hv, da).astype(BF16)
        dwi_ref[1] = _dot_tn(hv, db).astype(BF16)

    chunk = pl.BlockSpec((s, tf), lambda j: (0, j))
    once = lambda: pl.BlockSpec((s, d), lambda j: (0, 0), pipeline_mode=pl.Buffered(1))
    return pl.pallas_call(
        body, name="ffn_bwd_main", grid=(nf,),
        in_specs=[once(), once(), _full((1, d)), once(), chunk, chunk, pl.BlockSpec((tf, d), lambda j: (j, 0))],
        out_specs=[chunk, chunk, pl.BlockSpec((2, d, tf), lambda j: (0, 0, j)),
                   pl.BlockSpec((tf, d), lambda j: (j, 0)), _full((1, d))],
        out_shape=[jax.ShapeDtypeStruct((s, f), BF16), jax.ShapeDtypeStruct((s, f), BF16),
                   jax.ShapeDtypeStruct((2, d, f), BF16), jax.ShapeDtypeStruct((f, d), BF16),
                   jax.ShapeDtypeStruct((1, d), F32)],
        scratch_shapes=[pltpu.VMEM((s, d), BF16)],
        compiler_params=_cp(dimension_semantics=("arbitrary",)),
    )(dxo, o, gate, h, a, b, wo)


def _ffn_bwd_in(dza, dzb, wi, x, dxo, mod, g):
    s, d = x.shape
    f = dza.shape[1]
    tf, tm = FFN_TF_WIDE, min(s, FFN_TM_WIDE)
    nf, nt = f // tf, s // tm

    def body(dza_ref, dzb_ref, wa_ref, wb_ref, x_ref, dxo_ref, mod_ref, g_ref, dx_ref, rows_ref, acc):
        j, i = pl.program_id(0), pl.program_id(1)
        rows = pl.ds(pl.multiple_of(i * tm, tm), tm)

        @pl.when(jnp.logical_and(i == 0, j == 0))
        def _():
            rows_ref[...] = jnp.zeros_like(rows_ref)

        part = (lax.dot_general(dza_ref[...], wa_ref[...], (((1,), (1,)), ((), ())), preferred_element_type=F32)
                + lax.dot_general(dzb_ref[...], wb_ref[...], (((1,), (1,)), ((), ())), preferred_element_type=F32))

        @pl.when(j == 0)
        def _():
            acc[rows, :] = part

        @pl.when(jnp.logical_and(j > 0, j < nf - 1))
        def _():
            acc[rows, :] += part

        @pl.when(j == nf - 1)
        def _():
            dh = part + acc[rows, :] if nf > 1 else part
            dx, dsh, dsc, dg = _norm_mod_bwd(x_ref[...], g_ref[...], mod_ref[1:2, :], dh)
            dx_ref[...] = dx + dxo_ref[...]
            rows_ref[0:1, :] += dsh
            rows_ref[1:2, :] += dsc
            rows_ref[2:3, :] += dg

    late = pl.BlockSpec((tm, d), lambda j, i: (jnp.where(j == nf - 1, i, 0), 0))
    chunk = pl.BlockSpec((tm, tf), lambda j, i: (i, j))
    return pl.pallas_call(
        body, name="ffn_bwd_in", grid=(nf, nt),
        in_specs=[chunk, chunk,
                  pl.BlockSpec((None, d, tf), lambda j, i: (0, 0, j)),
                  pl.BlockSpec((None, d, tf), lambda j, i: (1, 0, j)),
                  late, late, _full((3, d)), _full((1, d))],
        out_specs=[late, _full((8, d))],
        out_shape=[jax.ShapeDtypeStruct((s, d), F32), jax.ShapeDtypeStruct((8, d), F32)],
        scratch_shapes=[pltpu.VMEM((s, d), F32)],
        compiler_params=_cp(dimension_semantics=("arbitrary", "arbitrary")),
    )(dza, dzb, wi, wi, x, dxo, mod, g)


def _mix_in_fwd(x, mod, g, wmit):
    s, d = x.shape
    p = wmit.shape[0]
    tm = _tm(s)

    def body(x_ref, mod_ref, g_ref, w_ref, z_ref, h_ref):
        hh = _norm_mod(x_ref[...], g_ref[...], mod_ref[0:1, :], mod_ref[1:2, :]).astype(BF16)
        h_ref[...] = hh
        z_ref[...] = lax.dot_general(hh, w_ref[...], (((1,), (1,)), ((), ())), preferred_element_type=F32)

    row = pl.BlockSpec((tm, d), lambda i: (i, 0))
    return pl.pallas_call(
        body, name="mix_in_fwd", grid=(s // tm,),
        in_specs=[row, _full((3, d)), _full((1, d)), _full((p, d))],
        out_specs=[pl.BlockSpec((tm, p), lambda i: (i, 0)), row],
        out_shape=[jax.ShapeDtypeStruct((s, p), F32), jax.ShapeDtypeStruct((s, d), BF16)],
        compiler_params=_cp(dimension_semantics=("parallel",)),
    )(x, mod, g, wmit)


def _mix_in_bwd(dzs, h, wmit, x, dxo, mod, g):
    s, d = x.shape
    p = wmit.shape[0]
    tm = min(s, 512)
    nt = s // tm

    def body(za_ref, zb_ref, zc_ref, zd_ref, h_ref, w_ref, x_ref, dxo_ref, mod_ref, g_ref,
             dx_ref, rows_ref, dw_ref, acc):
        i = pl.program_id(0)

        @pl.when(i == 0)
        def _():
            rows_ref[...] = jnp.zeros_like(rows_ref)
            acc[...] = jnp.zeros_like(acc)

        dz = jnp.concatenate([r[...].astype(BF16) for r in (za_ref, zb_ref, zc_ref, zd_ref)], axis=1)
        acc[...] += _dot_tn(dz, h_ref[...])
        dh = jnp.dot(dz, w_ref[...], preferred_element_type=F32)
        dx, dsh, dsc, dg = _norm_mod_bwd(x_ref[...], g_ref[...], mod_ref[1:2, :], dh)
        dx_ref[...] = dx + dxo_ref[...]
        rows_ref[0:1, :] += dsh
        rows_ref[1:2, :] += dsc
        rows_ref[2:3, :] += dg

        @pl.when(i == nt - 1)
        def _():
            dw_ref[...] = acc[...].astype(BF16)

    row = pl.BlockSpec((tm, d), lambda i: (i, 0))
    zspecs = [pl.BlockSpec((tm, z.shape[1]), lambda i: (i, 0)) for z in dzs]
    return pl.pallas_call(
        body, name="mix_in_bwd", grid=(nt,),
        in_specs=zspecs + [row, _full((p, d)), row, row, _full((3, d)), _full((1, d))],
        out_specs=[row, _full((8, d)), _full((p, d))],
        out_shape=[jax.ShapeDtypeStruct((s, d), F32), jax.ShapeDtypeStruct((8, d), F32),
                   jax.ShapeDtypeStruct((p, d), BF16)],
        scratch_shapes=[pltpu.VMEM((p, d), F32)],
        compiler_params=_cp(dimension_semantics=("arbitrary",)),
    )(*dzs, h, wmit, x, dxo, mod, g)


def _group_norm(ys, mng):
    outs, hats, rs = [], [], []
    for k, y in enumerate(ys):
        r = lax.rsqrt(jnp.mean(y * y, axis=-1, keepdims=True) + EPS)
        yh = y * r
        hats.append(yh)
        rs.append(r)
        outs.append(yh * mng[:, k * W_GRP:(k + 1) * W_GRP])
    return jnp.concatenate(outs, axis=1), hats, rs


def _s5_glu(y, gw, gb):
    yg, t = _gelu(y)
    gate = jax.nn.sigmoid(_dot(yg, gw) + gb)
    return yg * gate, yg, t, gate


def _mix_out_fwd(ys, glu, mng, wmo, x, gate):
    s, d = x.shape
    tm = _tm(s)

    def body(ya, yb, yc, ypre, gw_ref, gb_ref, mng_ref, w_ref, x_ref, gate_ref, xn_ref, m_ref):
        yd = _s5_glu(ypre[...], gw_ref[...], gb_ref[...])[0]
        yn, _, _ = _group_norm([ya[...], yb[...], yc[...], yd], mng_ref[...])
        m = jnp.dot(yn.astype(BF16), w_ref[...], preferred_element_type=F32)
        m_ref[...] = m.astype(BF16)
        xn_ref[...] = x_ref[...] + gate_ref[...] * m

    row = pl.BlockSpec((tm, d), lambda i: (i, 0))
    grp = pl.BlockSpec((tm, W_GRP), lambda i: (i, 0))
    return pl.pallas_call(
        body, name="mix_out_fwd", grid=(s // tm,),
        in_specs=[grp, grp, grp, grp, _full((W_GRP, W_GRP)), _full((1, W_GRP)), _full((1, d)), _full((d, d)), row,
                  _full((1, d))],
        out_specs=[row, row],
        out_shape=[jax.ShapeDtypeStruct((s, d), F32), jax.ShapeDtypeStruct((s, d), BF16)],
        compiler_params=_cp(dimension_semantics=("parallel",)),
    )(*ys, *glu, mng, wmo, x, gate)


def _mix_out_bwd(dxo, m, gate, ys, glu, mng, wmo):
    s, d = dxo.shape
    tm = min(s, 512)
    nt = s // tm

    def body(dxo_ref, m_ref, gate_ref, ya, yb, yc, ypre, gw_ref, gb_ref, mng_ref, w_ref,
             dya, dyb, dyc, dypre, dgate_ref, dmng_ref, dw_ref, dgw_ref, dgb_ref, acc):
        i = pl.program_id(0)

        @pl.when(i == 0)
        def _():
            dgate_ref[...] = jnp.zeros_like(dgate_ref)
            dmng_ref[...] = jnp.zeros_like(dmng_ref)
            dgw_ref[...] = jnp.zeros_like(dgw_ref)
            dgb_ref[...] = jnp.zeros_like(dgb_ref)
            acc[...] = jnp.zeros_like(acc)

        dxv = dxo_ref[...]
        dgate_ref[...] += jnp.sum(m_ref[...].astype(F32) * dxv, axis=0, keepdims=True)
        dm = (gate_ref[...] * dxv).astype(BF16)
        mng = mng_ref[...]
        gw = gw_ref[...]
        yp = ypre[...]
        yd, yg, t, glu_gate = _s5_glu(yp, gw, gb_ref[...])
        yn, hats, rs = _group_norm([ya[...], yb[...], yc[...], yd], mng)
        acc[...] += _dot_tn(yn, dm)
        dyn = lax.dot_general(dm, w_ref[...], (((1,), (1,)), ((), ())), preferred_element_type=F32)
        dmng_parts, dys = [], []
        for k, (yh, r) in enumerate(zip(hats, rs)):
            dk = dyn[:, k * W_GRP:(k + 1) * W_GRP]
            dmng_parts.append(jnp.sum(dk * yh, axis=0, keepdims=True))
            dyh = dk * mng[:, k * W_GRP:(k + 1) * W_GRP]
            dys.append(r * (dyh - yh * jnp.mean(dyh * yh, axis=-1, keepdims=True)))
        dmng_ref[...] += jnp.concatenate(dmng_parts, axis=1)
        dya[...], dyb[...], dyc[...] = dys[0], dys[1], dys[2]
        dyd = dys[3]
        dlin = dyd * yg * glu_gate * (1.0 - glu_gate)
        dgw_ref[...] += _dot_tn(yg, dlin)
        dgb_ref[...] += jnp.sum(dlin, axis=0, keepdims=True)
        dypre[...] = (dyd * glu_gate + _dot_nt(dlin, gw)) * _gelu_grad(yp, t)

        @pl.when(i == nt - 1)
        def _():
            dw_ref[...] = acc[...].astype(BF16)

    row = pl.BlockSpec((tm, d), lambda i: (i, 0))
    grp = pl.BlockSpec((tm, W_GRP), lambda i: (i, 0))
    return pl.pallas_call(
        body, name="mix_out_bwd", grid=(nt,),
        in_specs=[row, row, _full((1, d)), grp, grp, grp, grp, _full((W_GRP, W_GRP)), _full((1, W_GRP)), _full((1, d)),
                  _full((d, d))],
        out_specs=[grp, grp, grp, grp, _full((1, d)), _full((1, d)), _full((d, d)), _full((W_GRP, W_GRP)),
                   _full((1, W_GRP))],
        out_shape=[jax.ShapeDtypeStruct((s, W_GRP), F32)] * 4
        + [jax.ShapeDtypeStruct((1, d), F32), jax.ShapeDtypeStruct((1, d), F32), jax.ShapeDtypeStruct((d, d), BF16),
           jax.ShapeDtypeStruct((W_GRP, W_GRP), F32), jax.ShapeDtypeStruct((1, W_GRP), F32)],
        scratch_shapes=[pltpu.VMEM((d, d), F32)],
        compiler_params=_cp(dimension_semantics=("arbitrary",)),
    )(dxo, m, gate, *ys, *glu, mng, wmo)


def _sgu_consts():
    r = _iota((W_GRP, W_GRP), 0) >> 6
    c = _iota((W_GRP, W_GRP), 1) >> 6
    avg = jnp.where(r == c, 1.0 / 64.0, 0.0).astype(F32)
    tril = _iota((CHUNK, CHUNK), 0) >= _iota((CHUNK, CHUNK), 1)
    head = _iota((CHUNK, W_GRP), 1) >> 6
    return avg, tril, head


def _sgu_pre(za, avg):
    zg, t = _gelu(za)
    u, v = zg[:, :W_GRP], zg[:, W_GRP:]
    mu = _dot_hi(v, avg)
    vc = v - mu
    r = lax.rsqrt(_dot_hi(vc * vc, avg) + EPS)
    return t, u, vc * r, r


def _sgu_fwd(z, sgu_w, bias_full):
    s = z.shape[0]
    tm = min(s, 512)

    def body(za_ref, w_ref, bias_ref, ya_ref):
        avg, tril, head = _sgu_consts()
        _, u, vn, _ = _sgu_pre(za_ref[...], avg)
        wm = [jnp.where(tril, w_ref[h], 0.0).astype(BF16) for h in range(4)]
        vb = vn.astype(BF16)
        for n in range(tm // CHUNK):
            rows = slice(n * CHUNK, (n + 1) * CHUNK)
            mixed = bias_ref[...]
            for h in range(4):
                mixed = mixed + jnp.where(head == h, jnp.dot(wm[h], vb[rows], preferred_element_type=F32), 0.0)
            ya_ref[rows, :] = u[rows] * mixed

    return pl.pallas_call(
        body, name="sgu_fwd", grid=(s // tm,),
        in_specs=[pl.BlockSpec((tm, 2 * W_GRP), lambda i: (i, 0)), _full((4, CHUNK, CHUNK)), _full((CHUNK, W_GRP))],
        out_specs=pl.BlockSpec((tm, W_GRP), lambda i: (i, 0)),
        out_shape=jax.ShapeDtypeStruct((s, W_GRP), F32),
        compiler_params=_cp(dimension_semantics=("parallel",)),
    )(z, sgu_w, bias_full)


def _sgu_bwd(z, dya, sgu_w, bias_full):
    s = z.shape[0]
    tm = min(s, 512)
    nt = s // tm

    def body(za_ref, dya_ref, w_ref, bias_ref, dza_ref, dw_ref, db_ref, du_s, dvn_s):
        i = pl.program_id(0)

        @pl.when(i == 0)
        def _():
            dw_ref[...] = jnp.zeros_like(dw_ref)
            db_ref[...] = jnp.zeros_like(db_ref)

        avg, tril, head = _sgu_consts()
        za = za_ref[...]
        t, u, vn, r = _sgu_pre(za, avg)
        wm = [jnp.where(tril, w_ref[h], 0.0).astype(BF16) for h in range(4)]
        vb = vn.astype(BF16)
        dya = dya_ref[...]
        dw = [jnp.zeros((CHUNK, CHUNK), F32) for _ in range(4)]
        db = jnp.zeros((CHUNK, W_GRP), F32)
        for n in range(tm // CHUNK):
            rows = slice(n * CHUNK, (n + 1) * CHUNK)
            mixed = bias_ref[...]
            for h in range(4):
                mixed = mixed + jnp.where(head == h, jnp.dot(wm[h], vb[rows], preferred_element_type=F32), 0.0)
            dmix = dya[rows] * u[rows]
            du_s[rows, :] = dya[rows] * mixed
            db = db + dmix
            dmb = dmix.astype(BF16)
            dvn = jnp.zeros((CHUNK, W_GRP), F32)
            for h in range(4):
                dmh = jnp.where(head == h, dmix, 0.0)
                dw[h] = dw[h] + _dot_nt(dmh, vb[rows])
                dvn = dvn + jnp.where(head == h, _dot_tn(wm[h], dmb), 0.0)
            dvn_s[rows, :] = dvn
        for h in range(4):
            dw_ref[h] += jnp.where(tril, dw[h], 0.0)
        sel = ((_iota((W_GRP, CHUNK), 0) >> 6) == _iota((W_GRP, CHUNK), 1)).astype(F32)
        db_ref[...] += _dot_hi(db, sel)
        dvn = dvn_s[...]
        dv = r * (dvn - _dot_hi(dvn, avg) - vn * _dot_hi(dvn * vn, avg))
        dzg = jnp.concatenate([du_s[...], dv], axis=1)
        dza_ref[...] = (dzg * _gelu_grad(za, t)).astype(BF16)

    return pl.pallas_call(
        body, name="sgu_bwd", grid=(nt,),
        in_specs=[pl.BlockSpec((tm, 2 * W_GRP), lambda i: (i, 0)), pl.BlockSpec((tm, W_GRP), lambda i: (i, 0)),
                  _full((4, CHUNK, CHUNK)), _full((CHUNK, W_GRP))],
        out_specs=[pl.BlockSpec((tm, 2 * W_GRP), lambda i: (i, 0)), _full((4, CHUNK, CHUNK)), _full((CHUNK, CHUNK))],
        out_shape=[jax.ShapeDtypeStruct((s, 2 * W_GRP), BF16), jax.ShapeDtypeStruct((4, CHUNK, CHUNK), F32),
                   jax.ShapeDtypeStruct((CHUNK, CHUNK), F32)],
        scratch_shapes=[pltpu.VMEM((tm, W_GRP), F32), pltpu.VMEM((tm, W_GRP), F32)],
        compiler_params=_cp(dimension_semantics=("arbitrary",)),
    )(z, dya, sgu_w, bias_full)


def _shift_down(x, k):
    return jnp.where(_iota(x.shape, 0) < k, 0.0, pltpu.roll(x, k, 0))


def _shift_up(x, k):
    n = x.shape[0]
    return jnp.where(_iota(x.shape, 0) >= n - k, 0.0, pltpu.roll(x, n - k, 0))


def _by_pool_group(shape, v2, v4, v8, v16):
    col = _iota(shape, 1)
    return jnp.where(col < 64, v2, jnp.where(col < 128, v4, jnp.where(col < 192, v8, v16)))


def _pool_core(zb, pw2):
    s2 = zb + _shift_down(zb, 1)
    s4 = s2 + _shift_down(s2, 2)
    s8 = s4 + _shift_down(s4, 4)
    s16 = s8 + _shift_down(s8, 8)
    win = _by_pool_group(zb.shape, s2, s4, s8, s16)
    wlen = _by_pool_group(zb.shape, 2.0, 4.0, 8.0, 16.0)
    cnt = jnp.minimum((_iota(zb.shape, 0) + 1).astype(F32), wlen)
    p = win / cnt - zb
    wt = jnp.tile(pw2, (1, 4))
    wbd = jnp.where((_iota(wt.shape, 0) >> 6) == (_iota(wt.shape, 1) >> 6), wt, 0.0).astype(BF16)
    return p, cnt, wbd


def _conv_core(zc, cw):
    bg, cg, xh = zc[:, :W_GRP], zc[:, W_GRP:2 * W_GRP], zc[:, 2 * W_GRP:]
    y = cg * xh
    y1, y2 = _shift_down(y, 1), _shift_down(y, 2)
    out = cw[2:3, :] * y + cw[1:2, :] * y1 + cw[0:1, :] * y2
    return bg, cg, xh, y, y1, y2, out


def _poolconv_fwd(z, pw2, pscale, cw):
    s = z.shape[0]

    def body(zb_ref, zc_ref, pw_ref, ps_ref, cw_ref, yb_ref, yc_ref):
        p, _, wbd = _pool_core(zb_ref[...], pw_ref[...])
        yb_ref[...] = jnp.dot(p.astype(BF16), wbd, preferred_element_type=F32) * ps_ref[...]
        bg, _, _, _, _, _, out = _conv_core(zc_ref[...], cw_ref[...])
        yc_ref[...] = bg * out

    return pl.pallas_call(
        body, name="poolconv_fwd", grid=(1,),
        in_specs=[pl.BlockSpec((s, W_GRP), lambda i: (0, 2)), pl.BlockSpec((s, 3 * W_GRP), lambda i: (0, 1)),
                  _full((W_GRP, 64)), _full((1, W_GRP)), _full((3, W_GRP))],
        out_specs=[_full((s, W_GRP)), _full((s, W_GRP))],
        out_shape=[jax.ShapeDtypeStruct((s, W_GRP), F32)] * 2,
        compiler_params=_cp(dimension_semantics=("arbitrary",)),
    )(z, z, pw2, pscale, cw)


def _poolconv_bwd(z, dyb, dyc, pw2, pscale, cw):
    s = z.shape[0]

    def body(zb_ref, zc_ref, dyb_ref, dyc_ref, pw_ref, ps_ref, cw_ref, dzb_ref, dzc_ref, dw_ref, dps_ref, dcw_ref):
        zb = zb_ref[...]
        p, cnt, wbd = _pool_core(zb, pw_ref[...])
        pb = p.astype(BF16)
        out = jnp.dot(pb, wbd, preferred_element_type=F32)
        dyb = dyb_ref[...]
        dps_ref[...] = jnp.sum(dyb * out, axis=0, keepdims=True)
        dout = (dyb * ps_ref[...]).astype(BF16)
        dw = _dot_tn(pb, dout)
        dw_ref[...] = jnp.where((_iota(dw.shape, 0) >> 6) == (_iota(dw.shape, 1) >> 6), dw, 0.0)
        dp = lax.dot_general(dout, wbd, (((1,), (1,)), ((), ())), preferred_element_type=F32)
        dwin = dp / cnt
        t2 = dwin + _shift_up(dwin, 1)
        t4 = t2 + _shift_up(t2, 2)
        t8 = t4 + _shift_up(t4, 4)
        t16 = t8 + _shift_up(t8, 8)
        dzb_ref[...] = (_by_pool_group(zb.shape, t2, t4, t8, t16) - dp).astype(BF16)

        cw = cw_ref[...]
        bg, cg, xh, y, y1, y2, out = _conv_core(zc_ref[...], cw)
        dyc = dyc_ref[...]
        dout = dyc * bg
        dcw_ref[...] = jnp.zeros_like(dcw_ref)
        dcw_ref[0:1, :] = jnp.sum(dout * y2, axis=0, keepdims=True)
        dcw_ref[1:2, :] = jnp.sum(dout * y1, axis=0, keepdims=True)
        dcw_ref[2:3, :] = jnp.sum(dout * y, axis=0, keepdims=True)
        dy = cw[2:3, :] * dout + cw[1:2, :] * _shift_up(dout, 1) + cw[0:1, :] * _shift_up(dout, 2)
        dzc_ref[...] = jnp.concatenate([dyc * out, dy * xh, dy * cg], axis=1).astype(BF16)

    return pl.pallas_call(
        body, name="poolconv_bwd", grid=(1,),
        in_specs=[pl.BlockSpec((s, W_GRP), lambda i: (0, 2)), pl.BlockSpec((s, 3 * W_GRP), lambda i: (0, 1)),
                  _full((s, W_GRP)), _full((s, W_GRP)), _full((W_GRP, 64)), _full((1, W_GRP)), _full((3, W_GRP))],
        out_specs=[_full((s, W_GRP)), _full((s, 3 * W_GRP)), _full((W_GRP, W_GRP)), _full((1, W_GRP)), _full((8, W_GRP))],
        out_shape=[jax.ShapeDtypeStruct((s, W_GRP), BF16), jax.ShapeDtypeStruct((s, 3 * W_GRP), BF16),
                   jax.ShapeDtypeStruct((W_GRP, W_GRP), F32), jax.ShapeDtypeStruct((1, W_GRP), F32),
                   jax.ShapeDtypeStruct((8, W_GRP), F32)],
        compiler_params=_cp(dimension_semantics=("arbitrary",)),
    )(z, z, dyb, dyc, pw2, pscale, cw)


N_STATE = 1024
HALF_STATE = N_STATE // 2
HALF_CH = W_GRP // 2
N_SLAB = HALF_STATE // LANES


def _s5_disc(lre, lim, ldt):
    dt = jnp.exp(ldt)
    mag = jnp.exp(lre * dt)
    ang = lim * dt
    ar, ai = mag * jnp.cos(ang), mag * jnp.sin(ang)
    nr, ni = ar - 1.0, ai
    den = lre * lre + lim * lim
    kr = (nr * lre + ni * lim) / den
    ki = (ni * lre - nr * lim) / den
    return ar, ai, kr, ki


def _s5_mats(colp, br, bi, cr, ci):
    _, _, kr, ki = _s5_disc(colp[:, 0:1], colp[:, 1:2], colp[:, 2:3])
    bbr = kr * br - ki * bi
    bbi = kr * bi + ki * br
    bmask = (_iota((HALF_STATE, HALF_CH), 0) >> 6) == (_iota((HALF_STATE, HALF_CH), 1) >> 4)
    cmask = (_iota((HALF_CH, HALF_STATE), 0) >> 4) == (_iota((HALF_CH, HALF_STATE), 1) >> 6)
    btr = jnp.where(bmask, jnp.tile(bbr, (1, 8)), 0.0).astype(BF16)
    bti = jnp.where(bmask, jnp.tile(bbi, (1, 8)), 0.0).astype(BF16)
    ctr = jnp.where(cmask, jnp.tile(cr, (1, 8)), 0.0).astype(BF16)
    cti = jnp.where(cmask, jnp.tile(ci, (1, 8)), 0.0).astype(BF16)
    return kr, ki, btr, bti, ctr, cti, bmask, cmask


def _slab(q):
    return slice(q * LANES, (q + 1) * LANES)


def _cmul(ar, ai, br, bi):
    return ar * br - ai * bi, ar * bi + ai * br


def _sub_shift(x, k, up):
    row = _iota(x.shape, 0)
    if up:
        return jnp.where(row >= N_SEG - k, 0.0, pltpu.roll(x, N_SEG - k, 0))
    return jnp.where(row < k, 0.0, pltpu.roll(x, k, 0))


def _seg_rows(j):
    return pl.ds(pl.multiple_of(j * N_SEG, N_SEG), N_SEG)


def _interleave(src, dst, seg):
    def step(j, carry):
        dst[_seg_rows(j), :] = src[pl.ds(j, N_SEG, stride=seg), :]
        return carry
    lax.fori_loop(0, seg, step, 0)


def _deinterleave(src, dst, seg):
    def step(j, carry):
        dst[pl.ds(j, N_SEG, stride=seg), :] = src[_seg_rows(j), :]
        return carry
    lax.fori_loop(0, seg, step, 0)


def _scan(xr, xi, ar_row, ai_row, seg, reverse, states=None):
    nlog = int(math.log2(seg))
    assert (1 << nlog) == seg
    grads = []
    for q0 in range(0, N_SLAB, 4):
        qs = list(range(q0, q0 + 4))
        aq = [(jnp.broadcast_to(ar_row[:, _slab(q)], (N_SEG, LANES)),
               jnp.broadcast_to(ai_row[:, _slab(q)], (N_SEG, LANES))) for q in qs]
        zero = jnp.zeros((N_SEG, LANES), F32)

        def local(jj, carry, qs=qs, aq=aq):
            j = seg - 1 - jj if reverse else jj
            out = []
            for n, q in enumerate(qs):
                rows = _seg_rows(j)
                pr, pi = _cmul(aq[n][0], aq[n][1], carry[2 * n], carry[2 * n + 1])
                nr = pr + xr[q, rows, :]
                ni = pi + xi[q, rows, :]
                xr[q, rows, :] = nr
                xi[q, rows, :] = ni
                out += [nr, ni]
            return tuple(out)

        fin = lax.fori_loop(0, seg, local, (zero,) * 8)
        cins = []
        for n in range(4):
            er, ei = fin[2 * n], fin[2 * n + 1]
            pr, pi = aq[n]
            for _ in range(nlog):
                pr, pi = _cmul(pr, pi, pr, pi)
            yr, yi = er, ei
            for k in (1, 2, 4):
                sr, si = _cmul(pr, pi, _sub_shift(yr, k, reverse), _sub_shift(yi, k, reverse))
                yr, yi = yr + sr, yi + si
                pr, pi = _cmul(pr, pi, pr, pi)
            cins.append((_sub_shift(yr, 1, reverse), _sub_shift(yi, 1, reverse)))

        def fix(jj, carry, qs=qs, aq=aq, cins=cins):
            j = seg - 1 - jj if reverse else jj
            out, sums = [], []
            for n, q in enumerate(qs):
                rows = _seg_rows(j)
                pwr, pwi = carry[2 * n], carry[2 * n + 1]
                cr, ci = _cmul(pwr, pwi, cins[n][0], cins[n][1])
                v_r, v_i = xr[q, rows, :] + cr, xi[q, rows, :] + ci
                xr[q, rows, :] = v_r
                xi[q, rows, :] = v_i
                nr, ni = _cmul(pwr, pwi, aq[n][0], aq[n][1])
                out += [nr, ni]
                if states is not None:
                    prev = _seg_rows(j - 1)
                    p_r, p_i = states[0][q, prev, :], states[1][q, prev, :]
                    sums += [carry[8 + 2 * n] + v_r * p_r + v_i * p_i, carry[9 + 2 * n] - v_r * p_i + v_i * p_r]
            return tuple(out + sums)

        powers = tuple(v for pair in aq for v in pair)
        if states is None:
            lax.fori_loop(0, seg, fix, powers)
            continue
        assert reverse
        fix_last = lax.fori_loop(0, seg - 1, fix, powers + (zero,) * 8)
        first = _seg_rows(0)
        for n, q in enumerate(qs):
            cr, ci = _cmul(fix_last[2 * n], fix_last[2 * n + 1], cins[n][0], cins[n][1])
            v_r, v_i = xr[q, first, :] + cr, xi[q, first, :] + ci
            xr[q, first, :] = v_r
            xi[q, first, :] = v_i
            p_r = _sub_shift(states[0][q, _seg_rows(seg - 1), :], 1, False)
            p_i = _sub_shift(states[1][q, _seg_rows(seg - 1), :], 1, False)
            grads.append((jnp.sum(fix_last[8 + 2 * n] + v_r * p_r + v_i * p_i, axis=0, keepdims=True),
                          jnp.sum(fix_last[9 + 2 * n] - v_r * p_i + v_i * p_r, axis=0, keepdims=True)))
    return grads


def _s5_forward_states(u, btr, bti, ar_row, ai_row, xr, xi, seg):
    ub = u.astype(BF16)
    for q in range(N_SLAB):
        xr[q] = _dot_nt(ub, btr[_slab(q), :])
        xi[q] = _dot_nt(ub, bti[_slab(q), :])
    _scan(xr, xi, ar_row, ai_row, seg, False)


def _s5_readout(u, xr, xi, ctr, cti, d):
    y = d * u
    for q in range(N_SLAB):
        y = y + _dot_nt(xr[q], ctr[:, _slab(q)]) - _dot_nt(xi[q], cti[:, _slab(q)])
    return y


def _s5_param_specs():
    return [pl.BlockSpec((3, HALF_STATE), lambda i: (0, i)), pl.BlockSpec((HALF_STATE, 3), lambda i: (i, 0)),
            pl.BlockSpec((HALF_STATE, 16), lambda i: (i, 0)), pl.BlockSpec((HALF_STATE, 16), lambda i: (i, 0)),
            pl.BlockSpec((HALF_CH, 64), lambda i: (i, 0)), pl.BlockSpec((HALF_CH, 64), lambda i: (i, 0)),
            pl.BlockSpec((1, HALF_CH), lambda i: (0, i))]


def _s5_core_fwd(z, sp):
    s = z.shape[0]
    seg = s // N_SEG

    def body(u_ref, rowp, colp, br, bi, cr, ci, d_ref, y_ref, xr, xi, us, ys):
        ar, ai, _, _ = _s5_disc(rowp[0:1, :], rowp[1:2, :], rowp[2:3, :])
        _, _, btr, bti, ctr, cti, _, _ = _s5_mats(colp[...], br[...], bi[...], cr[...], ci[...])
        _interleave(u_ref, us, seg)
        u = us[...]
        _s5_forward_states(u, btr, bti, ar, ai, xr, xi, seg)
        ys[...] = _s5_readout(u, xr, xi, ctr, cti, d_ref[...])
        _deinterleave(ys, y_ref, seg)

    return pl.pallas_call(
        body, name="s5_core_fwd", grid=(2,),
        in_specs=[pl.BlockSpec((s, HALF_CH), lambda i: (0, 12 + i))] + _s5_param_specs(),
        out_specs=pl.BlockSpec((s, HALF_CH), lambda i: (0, i)),
        out_shape=jax.ShapeDtypeStruct((s, W_GRP), F32),
        scratch_shapes=[pltpu.VMEM((N_SLAB, s, LANES), F32)] * 2 + [pltpu.VMEM((s, HALF_CH), F32)] * 2,
        compiler_params=_cp(dimension_semantics=("parallel",)),
    )(z, *sp)


def _s5_core_bwd(z, dy, sp):
    s = z.shape[0]
    seg = s // N_SEG

    def body(u_ref, dy_ref, rowp, colp, br_ref, bi_ref, cr_ref, ci_ref, d_ref,
             du_ref, dbr_ref, dbi_ref, dcr_ref, dci_ref, dd_ref, da_ref, dk_ref,
             xr, xi, gr, gi, us, dys):
        ar, ai, _, _ = _s5_disc(rowp[0:1, :], rowp[1:2, :], rowp[2:3, :])
        br, bi = br_ref[...], bi_ref[...]
        kr, ki, btr, bti, ctr, cti, bmask, cmask = _s5_mats(colp[...], br, bi, cr_ref[...], ci_ref[...])
        _interleave(u_ref, us, seg)
        _interleave(dy_ref, dys, seg)
        u = us[...]
        d = d_ref[...]
        _s5_forward_states(u, btr, bti, ar, ai, xr, xi, seg)

        dy = dys[...]
        dd_ref[...] = jnp.sum(dy * u, axis=0, keepdims=True)
        du = d * dy
        dyb = dy.astype(BF16)
        dctr, dcti = [], []
        for q in range(N_SLAB):
            gr[q] = jnp.dot(dyb, ctr[:, _slab(q)], preferred_element_type=F32)
            gi[q] = -jnp.dot(dyb, cti[:, _slab(q)], preferred_element_type=F32)
            dctr.append(_dot_tn(dyb, xr[q]))
            dcti.append(-_dot_tn(dyb, xi[q]))
        selp = ((_iota((HALF_STATE, 64), 0) & 63) == _iota((HALF_STATE, 64), 1)).astype(F32)
        dcr_ref[...] = _dot_hi(jnp.where(cmask, jnp.concatenate(dctr, axis=1), 0.0), selp)
        dci_ref[...] = _dot_hi(jnp.where(cmask, jnp.concatenate(dcti, axis=1), 0.0), selp)

        da = _scan(gr, gi, ar, -ai, seg, True, states=(xr, xi))
        dar, dai = [p[0] for p in da], [p[1] for p in da]
        da_ref[...] = jnp.zeros_like(da_ref)
        da_ref[0:1, :] = jnp.concatenate(dar, axis=1)
        da_ref[1:2, :] = jnp.concatenate(dai, axis=1)

        ub = u.astype(BF16)
        dbtr, dbti = [], []
        for q in range(N_SLAB):
            g_r, g_i = gr[q].astype(BF16), gi[q].astype(BF16)
            du = du + jnp.dot(g_r, btr[_slab(q), :], preferred_element_type=F32) \
                + jnp.dot(g_i, bti[_slab(q), :], preferred_element_type=F32)
            dbtr.append(_dot_tn(g_r, ub))
            dbti.append(_dot_tn(g_i, ub))
        us[...] = du
        _deinterleave(us, du_ref, seg)
        selc =((_iota((HALF_CH, 16), 0) & 15) == _iota((HALF_CH, 16), 1)).astype(F32)
        dbbr = _dot_hi(jnp.where(bmask, jnp.concatenate(dbtr, axis=0), 0.0), selc)
        dbbi = _dot_hi(jnp.where(bmask, jnp.concatenate(dbti, axis=0), 0.0), selc)
        dbr_ref[...] = kr * dbbr + ki * dbbi
        dbi_ref[...] = kr * dbbi - ki * dbbr
        dk_ref[:, 0:1] = jnp.sum(dbbr * br + dbbi * bi, axis=1, keepdims=True)
        dk_ref[:, 1:2] = jnp.sum(dbbi * br - dbbr * bi, axis=1, keepdims=True)

    half = pl.BlockSpec((s, HALF_CH), lambda i: (0, i))
    return pl.pallas_call(
        body, name="s5_core_bwd", grid=(2,),
        in_specs=[pl.BlockSpec((s, HALF_CH), lambda i: (0, 12 + i)), half] + _s5_param_specs(),
        out_specs=[half, pl.BlockSpec((HALF_STATE, 16), lambda i: (i, 0)), pl.BlockSpec((HALF_STATE, 16), lambda i: (i, 0)),
                   pl.BlockSpec((HALF_CH, 64), lambda i: (i, 0)), pl.BlockSpec((HALF_CH, 64), lambda i: (i, 0)),
                   pl.BlockSpec((1, HALF_CH), lambda i: (0, i)), pl.BlockSpec((8, HALF_STATE), lambda i: (0, i)),
                   pl.BlockSpec((HALF_STATE, 2), lambda i: (i, 0))],
        out_shape=[jax.ShapeDtypeStruct((s, W_GRP), F32), jax.ShapeDtypeStruct((N_STATE, 16), F32),
                   jax.ShapeDtypeStruct((N_STATE, 16), F32), jax.ShapeDtypeStruct((W_GRP, 64), F32),
                   jax.ShapeDtypeStruct((W_GRP, 64), F32), jax.ShapeDtypeStruct((1, W_GRP), F32),
                   jax.ShapeDtypeStruct((8, N_STATE), F32), jax.ShapeDtypeStruct((N_STATE, 2), F32)],
        scratch_shapes=[pltpu.VMEM((N_SLAB, s, LANES), F32)] * 4 + [pltpu.VMEM((s, HALF_CH), F32)] * 2,
        compiler_params=_cp(dimension_semantics=("parallel",)),
    )(z, dy, *sp)


def _s5_param_bwd(lre, lim, ldt, da_r, da_i, dk_r, dk_i):
    n = lre.shape[0]

    def body(lre_ref, lim_ref, ldt_ref, dar_ref, dai_ref, dkr_ref, dki_ref, o_re, o_im, o_dt):
        lre, lim, ldt = lre_ref[...], lim_ref[...], ldt_ref[...]
        dt = jnp.exp(ldt)
        ar, ai, kr, ki = _s5_disc(lre, lim, ldt)
        mag = jnp.exp(lre * dt)
        den = lre * lre + lim * lim
        dkr, dki = dkr_ref[...], dki_ref[...]
        nr, ni = ar - 1.0, ai
        d_ar = dar_ref[...] + (dkr * lre - dki * lim) / den
        d_ai = dai_ref[...] + (dkr * lim + dki * lre) / den
        kk = (kr * dkr + ki * dki) * 2.0 / den
        d_lre = (dkr * nr + dki * ni) / den - kk * lre
        d_lim = (dkr * ni - dki * nr) / den - kk * lim
        d_mag = (d_ar * ar + d_ai * ai) / mag
        d_ang = d_ai * ar - d_ar * ai
        o_re[...] = d_lre + d_mag * mag * dt
        o_im[...] = d_lim + d_ang * dt
        o_dt[...] = jnp.sum((d_mag * mag * lre + d_ang * lim) * dt, axis=1, keepdims=True)

    return pl.pallas_call(
        body, name="s5_param_bwd",
        out_shape=[jax.ShapeDtypeStruct((n, 64), F32), jax.ShapeDtypeStruct((n, 64), F32),
                   jax.ShapeDtypeStruct((n, 1), F32)],
    )(lre, lim, ldt, da_r, da_i, dk_r, dk_i)


def _loss_head(x, fg, target):
    s, d = x.shape
    tm = _tm(s)

    def body(x_ref, fg_ref, t_ref, loss_ref, dx_ref, dfg_ref):
        i = pl.program_id(0)

        @pl.when(i == 0)
        def _():
            loss_ref[...] = jnp.zeros_like(loss_ref)
            dfg_ref[...] = jnp.zeros_like(dfg_ref)

        xv, g = x_ref[...], fg_ref[...]
        r = lax.rsqrt(jnp.mean(xv * xv, axis=-1, keepdims=True) + EPS)
        xh = xv * r
        err = xh * g - t_ref[...]
        loss_ref[...] += 0.5 * jnp.sum(jnp.mean(err * err, axis=-1, keepdims=True), axis=0, keepdims=True)
        dy = err * (1.0 / d)
        dfg_ref[...] += jnp.sum(dy * xh, axis=0, keepdims=True)
        dxh = dy * g
        dx_ref[...] = r * (dxh - xh * jnp.mean(dxh * xh, axis=-1, keepdims=True))

    row = pl.BlockSpec((tm, d), lambda i: (i, 0))
    return pl.pallas_call(
        body, name="loss_head", grid=(s // tm,),
        in_specs=[row, _full((1, d)), row], out_specs=[_full((1, 1)), row, _full((1, d))],
        out_shape=[jax.ShapeDtypeStruct((1, 1), F32), jax.ShapeDtypeStruct((s, d), F32),
                   jax.ShapeDtypeStruct((1, d), F32)],
        compiler_params=_cp(dimension_semantics=("arbitrary",)),
    )(x, fg, target)


ADA_TN = 384


def _cond_fwd(cact, ada_w, ada_b_loc):
    nl, d, n = ada_w.shape

    def body(c_ref, w_ref, b_ref, o_ref):
        o_ref[...] = _dot(c_ref[...], w_ref[...]) + b_ref[...]

    return pl.pallas_call(
        body, name="cond_fwd", grid=(nl, n // ADA_TN),
        in_specs=[_full((N_DEV, d)), pl.BlockSpec((None, d, ADA_TN), lambda l, j: (l, 0, j)),
                  pl.BlockSpec((None, 1, ADA_TN), lambda l, j: (l, 0, j))],
        out_specs=pl.BlockSpec((None, N_DEV, ADA_TN), lambda l, j: (l, 0, j)),
        out_shape=jax.ShapeDtypeStruct((nl, N_DEV, n), F32),
        compiler_params=_cp(dimension_semantics=("parallel", "parallel")),
    )(cact, ada_w, ada_b_loc)


ELEMENTWISE_BLOCK_BYTES = 1 << 20


def _row_tile(r, c, itemsize=4):
    best = None
    for t in range(8, r + 1, 8):
        if r % t == 0 and t * c * itemsize <= ELEMENTWISE_BLOCK_BYTES:
            best = t
    return best if best is not None else r


def _adamw_math(w, g, m, v):
    m = ADAM_B1 * m + (1.0 - ADAM_B1) * g
    v = ADAM_B2 * v + (1.0 - ADAM_B2) * (g * g)
    m_hat = m / (1.0 - ADAM_B1 ** ADAM_STEP)
    v_hat = v / (1.0 - ADAM_B2 ** ADAM_STEP)
    delta = -ADAM_LR * (m_hat / (jnp.sqrt(v_hat) + ADAM_EPS) + ADAM_WD * w)
    return delta, m, v


def _ada_w_update(cact, dcond_loc, w, m, v):
    nl, d, n = w.shape

    def body(c_ref, dc_ref, w_ref, m_ref, v_ref, g_out, d_out, m_out, v_out):
        g = _dot_tn(c_ref[...], dc_ref[...])
        g_out[...] = g
        d_out[...], m_out[...], v_out[...] = _adamw_math(w_ref[...], g, m_ref[...], v_ref[...])

    blk = pl.BlockSpec((None, d, ADA_TN), lambda l, j: (l, 0, j))
    return pl.pallas_call(
        body, name="ada_w_update", grid=(nl, n // ADA_TN),
        in_specs=[_full((N_DEV, d)), pl.BlockSpec((None, N_DEV, ADA_TN), lambda l, j: (l, 0, j)), blk, blk, blk],
        out_specs=[blk] * 4, out_shape=[jax.ShapeDtypeStruct((nl, d, n), F32)] * 4,
        compiler_params=_cp(dimension_semantics=("parallel", "parallel")),
    )(cact, dcond_loc, w, m, v)


def _place():
    x, y, c = lax.axis_index("x"), lax.axis_index("y"), lax.axis_index("c")
    chips = [(1 - x, y), (x, 1 - y), (1 - x, 1 - y)]
    return x, y, c, chips


def _remote(src, dst, send_sem, recv_sem, to):
    return pltpu.make_async_remote_copy(src_ref=src, dst_ref=dst, send_sem=send_sem, recv_sem=recv_sem,
                                        device_id=to, device_id_type=MESH_ID)


def _sems(n):
    return [pltpu.SemaphoreType.DMA((n,)), pltpu.SemaphoreType.DMA((n,))]


def _all_gather8(v, name):
    r, cdim = v.shape

    def body(x_ref, out_ref, stage, send_sems, recv_sems):
        x, y, c, chips = _place()
        sibling = (x, y, 1 - c)

        def slot(px, py, pc):
            return out_ref.at[4 * px + 2 * py + pc]

        first = [_remote(x_ref, slot(x, y, c), send_sems.at[0], recv_sems.at[0], sibling)]
        first += [_remote(x_ref, slot(x, y, c), send_sems.at[1 + j], recv_sems.at[1 + j], (*chip, c))
                  for j, chip in enumerate(chips)]
        for cp in first:
            cp.start()
        pltpu.sync_copy(x_ref, stage)
        pltpu.sync_copy(stage, slot(x, y, c))
        passed = []
        for j, chip in enumerate(chips):
            blk = slot(*chip, c)
            _remote(blk, blk, send_sems.at[1 + j], recv_sems.at[1 + j], (x, y, c)).wait_recv()
            fw = _remote(blk, blk, send_sems.at[4 + j], recv_sems.at[4 + j], sibling)
            fw.start()
            passed.append(fw)
        blk = slot(x, y, 1 - c)
        _remote(blk, blk, send_sems.at[0], recv_sems.at[0], (x, y, c)).wait_recv()
        for j, chip in enumerate(chips):
            blk = slot(*chip, 1 - c)
            _remote(blk, blk, send_sems.at[4 + j], recv_sems.at[4 + j], (x, y, c)).wait_recv()
        for cp in first + passed:
            cp.wait_send()

    return pl.pallas_call(
        body, name=name, out_shape=jax.ShapeDtypeStruct((N_DEV, r, cdim), v.dtype),
        in_specs=[ANY], out_specs=ANY,
        scratch_shapes=[pltpu.VMEM((r, cdim), v.dtype)] + _sems(7),
        compiler_params=_cp(),
    )(v)


def _gather_first_copies():
    def make(refs, send_sems, recv_sems):
        x, y, c, chips = _place()
        mine = refs[0].at[4 * x + 2 * y + c]
        to = [(x, y, 1 - c)] + [(*chip, c) for chip in chips]
        return [_remote(mine, mine, send_sems.at[k], recv_sems.at[k], dev) for k, dev in enumerate(to)]
    return make


def _gather_pass_on(buf, name):
    def body(in_ref, out_ref, send_sems, recv_sems):
        x, y, c, chips = _place()
        passed = []
        for j, chip in enumerate(chips):
            blk = out_ref.at[4 * chip[0] + 2 * chip[1] + c]
            fw = _remote(blk, blk, send_sems.at[j], recv_sems.at[j], (x, y, 1 - c))
            fw.start()
            passed.append(fw)
        for j, chip in enumerate(chips):
            blk = out_ref.at[4 * chip[0] + 2 * chip[1] + 1 - c]
            _remote(blk, blk, send_sems.at[j], recv_sems.at[j], (x, y, c)).wait_recv()
        for fw in passed:
            fw.wait_send()

    return pl.pallas_call(
        body, name=name, out_shape=jax.ShapeDtypeStruct(buf.shape, buf.dtype),
        in_specs=[ANY], out_specs=ANY, input_output_aliases={0: 0}, scratch_shapes=_sems(3),
    )(buf)


def _place_weights(ws, layer, kidx, after):
    steps = 4
    shapes, in_specs, out_specs = [], [], []
    for w, kind in zip(ws, BIG_KINDS):
        _, a, b = w.shape
        in_specs.append(pl.BlockSpec((None, a // steps, b), lambda i, k: (layer, i, 0)))
        if kind == "col":
            shapes.append((2, a, 2 * b))
            out_specs.append(pl.BlockSpec((None, a // steps, b), lambda i, k: (k[0] // 2, i, k[0] % 2)))
        else:
            shapes.append((N_CHIP, a, b))
            out_specs.append(pl.BlockSpec((None, a // steps, b), lambda i, k: (k[0], i, 0)))

    def body(k_ref, *refs):
        outs = refs[len(ws) + 1:]
        for t in range(len(ws)):
            outs[t][...] = refs[t][...].astype(BF16)

    return pl.pallas_call(
        body, name="place_weights", out_shape=[jax.ShapeDtypeStruct(s, BF16) for s in shapes],
        grid_spec=pltpu.PrefetchScalarGridSpec(num_scalar_prefetch=1, grid=(steps,), in_specs=in_specs + [ANY],
                                               out_specs=out_specs),
        compiler_params=_cp(dimension_semantics=("parallel",)),
    )(kidx, *ws, after)


HBM = pl.BlockSpec(memory_space=pltpu.HBM)
SEM = pl.BlockSpec(memory_space=pltpu.SEMAPHORE)
EFFECT = pltpu.SideEffectType.DATAFLOW_SIDE_EFFECTING


def _weight_block(ref, kind, k, h):
    if kind == "col":
        ncol = ref.shape[3] // 2
        return ref.at[k // 2, h, :, pl.ds(pl.multiple_of((k % 2) * ncol, LANES), ncol)]
    return ref.at[k, h]


def _in_hbm(a):
    return pltpu.with_memory_space_constraint(a, pltpu.HBM)


def _weight_send_start(placed, kinds, name):
    nt = len(placed)

    def body(*refs):
        send_sems, recv_sems = refs[nt], refs[nt + 1]
        dst = refs[nt + 2:2 * nt + 2]
        token = refs[2 * nt + 2]
        x, y, c, chips = _place()
        kme = 2 * x + y
        for t in range(nt):
            for j, chip in enumerate(chips):
                own = _weight_block(dst[t], kinds[t], kme, c)
                _remote(own, own, send_sems.at[3 * t + j], recv_sems.at[3 * t + j], (*chip, c)).start()
        token[...] = jnp.zeros_like(token)

    return pl.pallas_call(
        body, name=name,
        out_shape=(pltpu.SemaphoreType.DMA((3 * nt,)), pltpu.SemaphoreType.DMA((3 * nt,)),
                   *[pltpu.HBM(a.shape, a.dtype) for a in placed], jax.ShapeDtypeStruct((8, LANES), F32)),
        in_specs=[HBM] * nt, out_specs=(SEM, SEM, *[HBM] * nt, pl.BlockSpec(memory_space=pltpu.VMEM)),
        input_output_aliases={t: 2 + t for t in range(nt)},
        compiler_params=pltpu.CompilerParams(has_side_effects=EFFECT),
    )(*[_in_hbm(a) for a in placed])


def _weight_send_wait(send_sems, recv_sems, arrays, kinds, after, name):
    nt = len(arrays)

    def body(*refs):
        arr = refs[:nt]
        send_sems, recv_sems = refs[nt], refs[nt + 1]
        x, y, c, chips = _place()
        kme = 2 * x + y
        for t in range(nt):
            for j, chip in enumerate(chips):
                own = _weight_block(arr[t], kinds[t], kme, c)
                got = _weight_block(arr[t], kinds[t], 2 * chip[0] + chip[1], c)
                cp = _remote(own, got, send_sems.at[3 * t + j], recv_sems.at[3 * t + j], (*chip, c))
                cp.wait_send()
                cp.wait_recv()

    return pl.pallas_call(
        body, name=name, out_shape=[pltpu.HBM(a.shape, a.dtype) for a in arrays],
        in_specs=[HBM] * nt + [SEM, SEM, ANY], out_specs=[HBM] * nt,
        input_output_aliases={t: t for t in range(nt)},
        compiler_params=pltpu.CompilerParams(has_side_effects=EFFECT),
    )(*arrays, send_sems, recv_sems, after)


def _forward_copies(kinds):
    def make(refs, send_sems, recv_sems):
        x, y, c, chips = _place()
        cps = []
        for t in range(len(kinds)):
            for j, chip in enumerate(chips):
                blk = _weight_block(refs[t], kinds[t], 2 * chip[0] + chip[1], c)
                cps.append(_remote(blk, blk, send_sems.at[3 * t + j], recv_sems.at[3 * t + j], (x, y, 1 - c)))
        return cps
    return make


def _split_start(name, arrays, n_copies, make_copies):
    na = len(arrays)

    def body(*refs):
        send_sems, recv_sems = refs[na], refs[na + 1]
        for cp in make_copies(refs[na + 2:2 * na + 2], send_sems, recv_sems):
            cp.start()
        token = refs[2 * na + 2]
        token[...] = jnp.zeros_like(token)

    return pl.pallas_call(
        body, name=name,
        out_shape=(pltpu.SemaphoreType.DMA((n_copies,)), pltpu.SemaphoreType.DMA((n_copies,)),
                   *[pltpu.HBM(a.shape, a.dtype) for a in arrays], jax.ShapeDtypeStruct((8, LANES), F32)),
        in_specs=[HBM] * na, out_specs=(SEM, SEM, *[HBM] * na, pl.BlockSpec(memory_space=pltpu.VMEM)),
        input_output_aliases={t: 2 + t for t in range(na)},
        compiler_params=pltpu.CompilerParams(has_side_effects=EFFECT),
    )(*[_in_hbm(a) for a in arrays])


def _split_wait(name, started, make_copies, after):
    send_sems, recv_sems, *arrays, _ = started
    na = len(arrays)

    def body(*refs):
        send_sems, recv_sems = refs[na], refs[na + 1]
        for cp in make_copies(refs[:na], send_sems, recv_sems):
            cp.wait_send()
            cp.wait_recv()

    return pl.pallas_call(
        body, name=name, out_shape=[pltpu.HBM(a.shape, a.dtype) for a in arrays],
        in_specs=[HBM] * na + [SEM, SEM, ANY], out_specs=[HBM] * na,
        input_output_aliases={t: t for t in range(na)},
        compiler_params=pltpu.CompilerParams(has_side_effects=EFFECT),
    )(*arrays, send_sems, recv_sems, after)


def _exchange_copies(nt):
    def make(refs, send_sems, recv_sems):
        x, y, c, _ = _place()
        return [_remote(refs[t].at[:, 1 - c], refs[nt + t], send_sems.at[t], recv_sems.at[t], (x, y, 1 - c))
                for t in range(nt)]
    return make


def _sibling_exchange_start(views, name):
    lands = [lax.empty((v.shape[0],) + v.shape[2:], v.dtype) for v in views]
    return _split_start(name, list(views) + lands, len(views), _exchange_copies(len(views)))


def _sibling_exchange_wait(started, after, name):
    nt = (len(started) - 3) // 2
    outs = _split_wait(name, started, _exchange_copies(nt), after)
    return outs[:nt], outs[nt:]


def _scatter_copies(src, land, kinds, send_sems, recv_sems):
    x, y, c, chips = _place()
    cps = []
    for t in range(len(src)):
        for j, chip in enumerate(chips):
            k = 2 * chip[0] + chip[1]
            if kinds[t] == "col":
                ncol = land[t].shape[2]
                win = src[t].at[k // 2, :, pl.ds(pl.multiple_of((k % 2) * ncol, LANES), ncol)]
            else:
                win = src[t].at[k]
            cps.append(_remote(win, land[t].at[j], send_sems.at[3 * t + j], recv_sems.at[3 * t + j], (*chip, c)))
    return cps


def _chip_scatter_start(parts, kinds, name):
    nt = len(parts)
    shapes = []
    for p, kind in zip(parts, kinds):
        shapes.append((3, p.shape[1], p.shape[2] // 2) if kind == "col" else (3,) + p.shape[1:])

    def body(*refs):
        send_sems, recv_sems = refs[2 * nt], refs[2 * nt + 1]
        src, land = refs[2 * nt + 2:3 * nt + 2], refs[3 * nt + 2:4 * nt + 2]
        token = refs[4 * nt + 2]
        for cp in _scatter_copies(src, land, kinds, send_sems, recv_sems):
            cp.start()
        token[...] = jnp.zeros_like(token)

    lands = [lax.empty(s, BF16) for s in shapes]
    return pl.pallas_call(
        body, name=name,
        out_shape=(pltpu.SemaphoreType.DMA((3 * nt,)), pltpu.SemaphoreType.DMA((3 * nt,)),
                   *[pltpu.HBM(a.shape, a.dtype) for a in parts], *[pltpu.HBM(s, BF16) for s in shapes],
                   jax.ShapeDtypeStruct((8, LANES), F32)),
        in_specs=[HBM] * (2 * nt), out_specs=(SEM, SEM, *[HBM] * (2 * nt), pl.BlockSpec(memory_space=pltpu.VMEM)),
        input_output_aliases={t: 2 + t for t in range(2 * nt)},
        compiler_params=pltpu.CompilerParams(has_side_effects=EFFECT),
    )(*[_in_hbm(a) for a in parts], *[_in_hbm(a) for a in lands])


def _chip_scatter_wait(send_sems, recv_sems, parts, lands, kinds, after, name):
    nt = len(parts)

    def body(*refs):
        src, land = refs[:nt], refs[nt:2 * nt]
        send_sems, recv_sems = refs[2 * nt], refs[2 * nt + 1]
        for cp in _scatter_copies(src, land, kinds, send_sems, recv_sems):
            cp.wait_send()
            cp.wait_recv()

    outs = pl.pallas_call(
        body, name=name, out_shape=[pltpu.HBM(a.shape, a.dtype) for a in list(parts) + list(lands)],
        in_specs=[HBM] * (2 * nt) + [SEM, SEM, ANY], out_specs=[HBM] * (2 * nt),
        input_output_aliases={t: t for t in range(2 * nt)},
        compiler_params=pltpu.CompilerParams(has_side_effects=EFFECT),
    )(*parts, *lands, send_sems, recv_sems, after)
    return outs[:nt], outs[nt:]


def _share_copies(nt):
    def make(refs, send_sems, recv_sems):
        x, y, c, _ = _place()
        return [_remote(refs[t].at[c], refs[t].at[c], send_sems.at[t], recv_sems.at[t], (x, y, 1 - c))
                for t in range(nt)]
    return make


def _sibling_share_start(fulls, name):
    return _split_start(name, list(fulls), len(fulls), _share_copies(len(fulls)))


def _sibling_share_wait(started, after, name):
    return _split_wait(name, started, _share_copies(len(started) - 3), after)


SUM_STEPS = 4


def _pair_sum(views, lands, ck):
    nt = len(views)
    in_specs, out_specs, shapes = [], [], []
    for v in views:
        b, _, r, cc = v.shape
        per = SUM_STEPS // b
        tr = r // per
        in_specs.append(pl.BlockSpec((None, None, tr, cc), lambda i, s, per=per: (i // per, s[0], i % per, 0)))
        out_specs.append(pl.BlockSpec((None, tr, cc), lambda i, s, per=per: (i // per, i % per, 0)))
        shapes.append((b, r, cc))
    in_specs = in_specs + out_specs

    def body(s_ref, *refs):
        for t in range(nt):
            refs[2 * nt + t][...] = (refs[t][...].astype(F32) + refs[nt + t][...].astype(F32)).astype(BF16)

    return pl.pallas_call(
        body, name="grad_pair_sum", out_shape=[jax.ShapeDtypeStruct(s, BF16) for s in shapes],
        grid_spec=pltpu.PrefetchScalarGridSpec(num_scalar_prefetch=1, grid=(SUM_STEPS,), in_specs=in_specs,
                                               out_specs=out_specs),
        compiler_params=_cp(dimension_semantics=("parallel",)),
    )(ck, *views, *lands)


def _chip_sum(parts, lands, kinds, ck):
    nt = len(parts)
    steps = 2
    in_own, in_land, out_specs, shapes = [], [], [], []
    for ld, kind in zip(lands, kinds):
        _, r, cc = ld.shape
        tr = r // steps
        if kind == "col":
            in_own.append(pl.BlockSpec((None, tr, cc), lambda i, s: (s[1] // 2, i, s[1] % 2)))
        else:
            in_own.append(pl.BlockSpec((None, tr, cc), lambda i, s: (s[1], i, 0)))
        in_land.append(pl.BlockSpec((3, tr, cc), lambda i, s: (0, i, 0)))
        out_specs.append(pl.BlockSpec((None, tr, cc), lambda i, s: (s[0], i, 0)))
        shapes.append((2, r, cc))

    def body(s_ref, *refs):
        for t in range(nt):
            acc = refs[t][...].astype(F32)
            for j in range(3):
                acc = acc + refs[nt + t][j].astype(F32)
            refs[2 * nt + t][...] = acc

    return pl.pallas_call(
        body, name="grad_chip_sum", out_shape=[jax.ShapeDtypeStruct(s, F32) for s in shapes],
        grid_spec=pltpu.PrefetchScalarGridSpec(num_scalar_prefetch=1, grid=(steps,), in_specs=in_own + in_land,
                                               out_specs=out_specs),
        compiler_params=_cp(dimension_semantics=("parallel",)),
    )(ck, *parts, *lands)


def _sum8(g):
    _, r, cc = g.shape
    tr = _row_tile(r, N_DEV * cc)

    def body(g_ref, o_ref):
        acc = g_ref[0].astype(F32)
        for d in range(1, N_DEV):
            acc = acc + g_ref[d].astype(F32)
        o_ref[...] = acc

    return pl.pallas_call(
        body, name="small_grad_sum", grid=(r // tr,),
        in_specs=[pl.BlockSpec((N_DEV, tr, cc), lambda i: (0, i, 0))],
        out_specs=pl.BlockSpec((tr, cc), lambda i: (i, 0)),
        out_shape=jax.ShapeDtypeStruct((r, cc), F32),
        compiler_params=_cp(dimension_semantics=("parallel",)),
    )(g)


def _silu_rows(c):
    def body(c_ref, o_ref):
        v = c_ref[...]
        o_ref[...] = v * jax.nn.sigmoid(v)

    return pl.pallas_call(body, name="cond_silu", out_shape=jax.ShapeDtypeStruct(c.shape, F32))(c)


def _pack(arrays):
    rows = []
    for a in arrays:
        flat = a.reshape(-1)
        rows.append(jnp.pad(flat, (0, (-flat.shape[0]) % (8 * LANES))).reshape(-1, LANES))
    n = sum(r.shape[0] for r in rows)
    if n % 256:
        rows.append(jnp.zeros((256 - n % 256, LANES), rows[0].dtype))
    return jnp.concatenate(rows, axis=0)


def _unpack(packed, shapes):
    out, off = [], 0
    for s in shapes:
        n = math.prod(s)
        nr = 8 * -(-n // (8 * LANES))
        out.append(packed[off:off + nr].reshape(-1)[:n].reshape(s))
        off += nr
    return out


def _as_rows(a):
    return a.reshape(1, -1) if a.ndim == 1 else a.reshape(-1, a.shape[-1])


def _adamw_many(ws, gs, ms, vs, name, steps=1):
    nt = len(ws)

    def body(*refs):
        for t in range(nt):
            w_ref, g_ref, m_ref, v_ref = (refs[k * nt + t] for k in range(4))
            d, m, v = _adamw_math(w_ref[...], g_ref[...], m_ref[...], v_ref[...])
            refs[4 * nt + t][...] = d
            refs[5 * nt + t][...] = m
            refs[6 * nt + t][...] = v

    shapes = [jax.ShapeDtypeStruct(a.shape, F32) for a in ws]
    if steps == 1:
        outs = pl.pallas_call(body, name=name, out_shape=shapes * 3, compiler_params=_cp())(*ws, *gs, *ms, *vs)
    else:
        specs = [pl.BlockSpec((a.shape[0] // steps, a.shape[1]), lambda i: (i, 0)) for a in ws]
        outs = pl.pallas_call(
            body, name=name, grid=(steps,), in_specs=specs * 4, out_specs=specs * 3, out_shape=shapes * 3,
            compiler_params=_cp(dimension_semantics=("parallel",)),
        )(*ws, *gs, *ms, *vs)
    return outs[:nt], outs[nt:2 * nt], outs[2 * nt:]


def _exchange_big_grads(grads, kinds, layer):
    views = []
    for g, kind in zip(grads, kinds):
        if kind == "col":
            views.append(g.reshape(2, 2, g.shape[1] // 2, g.shape[2]))
        else:
            views.append(g.reshape(N_CHIP, 2, g.shape[0] // (2 * N_CHIP), g.shape[1]))
    return _sibling_exchange_start(views, "grad_exchange_start_%d" % layer)


def _scatter_big_grads(exchanged, kinds, ck, after, layer):
    views, lands = _sibling_exchange_wait(exchanged, after, "grad_exchange_wait_%d" % layer)
    parts = _pair_sum(views, lands, ck)
    return _chip_scatter_start(parts, kinds, "grad_scatter_start_%d" % layer)


def _finish_big_grads(started, kinds, ck, after, layer):
    nt = len(kinds)
    send_sems, recv_sems = started[0], started[1]
    parts, lands = started[2:2 + nt], started[2 + nt:2 + 2 * nt]
    parts, lands = _chip_scatter_wait(send_sems, recv_sems, parts, lands, kinds, after, "grad_scatter_wait_%d" % layer)
    return _sibling_share_start(_chip_sum(parts, lands, kinds, ck), "grad_share_start_%d" % layer)


def _adamw_layer(ws, gs, ms, vs, stacks, layer, name, steps):
    nt = len(ws)
    stacks = [s if s is not None else tuple(lax.empty(w.shape, F32) for _ in range(4)) for s, w in zip(stacks, ws)]

    def body(*refs):
        for t in range(nt):
            w_ref, g_ref, m_ref, v_ref = (refs[k * nt + t] for k in range(4))
            outs = refs[8 * nt + 4 * t:8 * nt + 4 * t + 4]
            g = g_ref[...]
            outs[0][...] = g
            outs[1][...], outs[2][...], outs[3][...] = _adamw_math(w_ref[...], g, m_ref[...], v_ref[...])

    in_specs, g_specs, out_specs = [], [], []
    for w in ws:
        _, r, c = w.shape
        in_specs.append(pl.BlockSpec((None, r // steps, c), lambda i: (layer, i, 0)))
        g_specs.append(pl.BlockSpec((r // steps, c), lambda i: (i, 0)))
        out_specs += [pl.BlockSpec((None, r // steps, c), lambda i: (layer, i, 0))] * 4
    in_specs = in_specs + g_specs + in_specs * 2 + [ANY] * (4 * nt)
    flat = [a for s in stacks for a in s]
    outs = pl.pallas_call(
        body, name=name, grid=(steps,), in_specs=in_specs, out_specs=out_specs,
        out_shape=[jax.ShapeDtypeStruct(a.shape, F32) for a in flat],
        input_output_aliases={4 * nt + k: k for k in range(4 * nt)},
        compiler_params=_cp(dimension_semantics=("parallel",)),
    )(*ws, *gs, *ms, *vs, *flat)
    return [tuple(outs[4 * t:4 * t + 4]) for t in range(nt)]


SMALL_NAMES = ["ada_b", "norm1_g", "norm2_g", "sgu_w", "sgu_b", "pool_w", "pool_scale", "conv_w", "s5_lambda_re",
               "s5_lambda_im", "s5_b_re", "s5_b_im", "s5_c_re", "s5_c_im", "s5_d", "s5_log_dt", "s5_glu_w", "s5_glu_b",
               "mix_norm_g", "norm3_g", "final_norm_g"]
BIG_NAMES = ["ffn1_w_in", "ffn1_w_out", "w_mix_in", "w_mix_out", "ffn2_w_in", "ffn2_w_out"]
BIG_KINDS = ["col", "row", "row", "row", "col", "row"]
WEIGHT_ORDER = ["ada_w", "ada_b", "norm1_g", "ffn1_w_in", "ffn1_w_out", "norm2_g", "w_mix_in", "sgu_w", "sgu_b", "pool_w",
                "pool_scale", "conv_w", "s5_lambda_re", "s5_lambda_im", "s5_b_re", "s5_b_im", "s5_c_re", "s5_c_im", "s5_d",
                "s5_log_dt", "s5_glu_w", "s5_glu_b", "mix_norm_g", "w_mix_out", "norm3_g", "ffn2_w_in", "ffn2_w_out",
                "final_norm_g"]


def _local_step(x, target, cond, fetch_weights, prefetch_weights, p, emit_grads):
    nl, d = DEPTH, x.shape[1]
    row = lambda a: a.reshape(1, -1)
    saved = []
    for l in range(nl):
        (wi1, wo1, wmit, wmo, wi2, wo2), tok = fetch_weights(l, x)
        cl = cond[l] + tok
        mod1, mod2, mod3 = cl[0:3], cl[3:6], cl[6:9]
        lre, lim = p["s5_lambda_re"][l].reshape(-1), p["s5_lambda_im"][l].reshape(-1)
        ldt = jnp.repeat(p["s5_log_dt"][l], 64)
        rowp = jnp.stack([lre, lim, ldt])
        sp = (rowp, rowp.T, p["s5_b_re"][l].reshape(N_STATE, 16), p["s5_b_im"][l].reshape(N_STATE, 16),
              p["s5_c_re"][l].reshape(W_GRP, 64), p["s5_c_im"][l].reshape(W_GRP, 64), row(p["s5_d"][l]))
        glu = (p["s5_glu_w"][l], row(p["s5_glu_b"][l]))
        bias_full = jnp.repeat(p["sgu_b"][l].T, 64, axis=1)
        pw2 = p["pool_w"][l].reshape(W_GRP, 64)
        x1, h1, a1, b1, o1 = _ffn_fwd(x, mod1, row(p["norm1_g"][l]), wi1, wo1)
        z, h2 = _mix_in_fwd(x1, mod2, row(p["norm2_g"][l]), wmit)
        ya = _sgu_fwd(z, p["sgu_w"][l], bias_full)
        yb, yc = _poolconv_fwd(z, pw2, row(p["pool_scale"][l]), p["conv_w"][l])
        ys = (ya, yb, yc, _s5_core_fwd(z, sp))
        x2, m = _mix_out_fwd(ys, glu, row(p["mix_norm_g"][l]), wmo, x1, mod2[2:3])
        mod3 = mod3 + prefetch_weights(l + 1, x2)
        x3, h3, a3, b3, o3 = _ffn_fwd(x2, mod3, row(p["norm3_g"][l]), wi2, wo2)
        saved.append((x, x1, x2, h1, a1, b1, o1, z, h2, ys, m, h3, a3, b3, o3, sp, bias_full, pw2, glu,
                      (wi1, wo1, wmit, wmo, wi2, wo2), cl))
        x = x3

    loss, dx, dfg = _loss_head(x, row(p["final_norm_g"]), target)

    sg = {n: [None] * nl for n in SMALL_NAMES if n not in ("ada_b", "final_norm_g")}
    dcond = [None] * nl
    s5_da, s5_dk = [None] * nl, [None] * nl
    tok = 0.0
    for l in reversed(range(nl)):
        (x0, x1, x2, h1, a1, b1, o1, z, h2, ys, m, h3, a3, b3, o3, sp, bias_full, pw2, glu,
         (wi1, wo1, wmit, wmo, wi2, wo2), cl) = saved[l]
        cl = cl + tok
        mod1, mod2, mod3 = cl[0:3], cl[3:6], cl[6:9]
        dza, dzb, dwi2, dwo2, dgate3 = _ffn_bwd_main(dx, o3, mod3[2:3], h3, a3, b3, wo2)
        dx, rows3 = _ffn_bwd_in(dza, dzb, wi2, x2, dx, mod3, row(p["norm3_g"][l]))
        outs = _mix_out_bwd(dx, m, mod2[2:3], ys, glu, row(p["mix_norm_g"][l]), wmo)
        dys, dgate2, dmng, dwmo, dgw, dgb = outs[0:4], outs[4], outs[5], outs[6], outs[7], outs[8]
        dza_, dsw, dsb = _sgu_bwd(z, dys[0], p["sgu_w"][l], bias_full)
        dzb_, dzc_, dwbd, dps, dcw = _poolconv_bwd(z, dys[1], dys[2], pw2, row(p["pool_scale"][l]), p["conv_w"][l])
        dzd_, dbr, dbi, dcr, dci, dd, da, dk = _s5_core_bwd(z, dys[3], sp)
        dx, rows2, dwmit = _mix_in_bwd((dza_, dzb_, dzc_, dzd_), h2, wmit, x1, dx, mod2, row(p["norm2_g"][l]))
        dza, dzb, dwi1, dwo1, dgate1 = _ffn_bwd_main(dx, o1, mod1[2:3], h1, a1, b1, wo1)
        tok, layer_done = emit_grads(l, [dwi1, dwo1, dwmit, dwmo, dwi2, dwo2])
        dx, rows1 = _ffn_bwd_in(dza, dzb, wi1, x0, dx, mod1 + tok, row(p["norm1_g"][l]))
        if l > 0:
            tok = layer_done(dx)[0, 0]
        dcond[l] = jnp.concatenate([rows1[0:2], dgate1, rows2[0:2], dgate2, rows3[0:2], dgate3], axis=0)
        sg["norm1_g"][l], sg["norm2_g"][l], sg["norm3_g"][l] = rows1[2], rows2[2], rows3[2]
        sg["mix_norm_g"][l] = dmng[0]
        sg["sgu_w"][l] = dsw
        sg["sgu_b"][l] = dsb[:, 0:4].T
        g4 = dwbd.reshape(4, 64, 4, 64)
        sg["pool_w"][l] = jnp.stack([g4[k, :, k, :] for k in range(4)])
        sg["pool_scale"][l] = dps[0]
        sg["conv_w"][l] = dcw[0:3]
        sg["s5_b_re"][l], sg["s5_b_im"][l] = dbr.reshape(16, 64, 16), dbi.reshape(16, 64, 16)
        sg["s5_c_re"][l], sg["s5_c_im"][l] = dcr.reshape(16, 16, 64), dci.reshape(16, 16, 64)
        sg["s5_d"][l] = dd[0]
        sg["s5_glu_w"][l], sg["s5_glu_b"][l] = dgw, dgb[0]
        s5_da[l], s5_dk[l] = da, dk

    n16 = nl * 16
    dlre, dlim, dldt = _s5_param_bwd(
        p["s5_lambda_re"].reshape(n16, 64), p["s5_lambda_im"].reshape(n16, 64),
        jnp.repeat(p["s5_log_dt"].reshape(n16, 1), 64, axis=1),
        jnp.stack([a[0] for a in s5_da]).reshape(n16, 64), jnp.stack([a[1] for a in s5_da]).reshape(n16, 64),
        jnp.stack([k[:, 0] for k in s5_dk]).reshape(n16, 64), jnp.stack([k[:, 1] for k in s5_dk]).reshape(n16, 64))
    small = {n: jnp.stack(v) for n, v in sg.items() if v[0] is not None}
    small["s5_lambda_re"] = dlre.reshape(nl, 16, 64)
    small["s5_lambda_im"] = dlim.reshape(nl, 16, 64)
    small["s5_log_dt"] = dldt.reshape(nl, 16)
    small["final_norm_g"] = dfg[0]
    return loss, dx, small, jnp.stack(dcond), layer_done


def kernel(x, c, ada_w, ada_b, norm1_g, ffn1_w_in, ffn1_w_out, norm2_g, w_mix_in, sgu_w, sgu_b, pool_w, pool_scale, conv_w, s5_lambda_re, s5_lambda_im, s5_b_re, s5_b_im, s5_c_re, s5_c_im, s5_d, s5_log_dt, s5_glu_w, s5_glu_b, mix_norm_g, w_mix_out, norm3_g, ffn2_w_in, ffn2_w_out, final_norm_g, loss_target, m_ada_w, m_ada_b, m_norm1_g, m_ffn1_w_in, m_ffn1_w_out, m_norm2_g, m_w_mix_in, m_sgu_w, m_sgu_b, m_pool_w, m_pool_scale, m_conv_w, m_s5_lambda_re, m_s5_lambda_im, m_s5_b_re, m_s5_b_im, m_s5_c_re, m_s5_c_im, m_s5_d, m_s5_log_dt, m_s5_glu_w, m_s5_glu_b, m_mix_norm_g, m_w_mix_out, m_norm3_g, m_ffn2_w_in, m_ffn2_w_out, m_final_norm_g, v_ada_w, v_ada_b, v_norm1_g, v_ffn1_w_in, v_ffn1_w_out, v_norm2_g, v_w_mix_in, v_sgu_w, v_sgu_b, v_pool_w, v_pool_scale, v_conv_w, v_s5_lambda_re, v_s5_lambda_im, v_s5_b_re, v_s5_b_im, v_s5_c_re, v_s5_c_im, v_s5_d, v_s5_log_dt, v_s5_glu_w, v_s5_glu_b, v_mix_norm_g, v_w_mix_out, v_norm3_g, v_ffn2_w_in, v_ffn2_w_out, v_final_norm_g):
    args = dict(locals())
    w = {n: args[n] for n in WEIGHT_ORDER}
    mom = {n: args["m_" + n] for n in WEIGHT_ORDER}
    vel = {n: args["v_" + n] for n in WEIGHT_ORDER}
    nl, d = DEPTH, x.shape[-1]
    s = x.shape[1]
    px, py, pc = lax.axis_index("x"), lax.axis_index("y"), lax.axis_index("c")
    kme = 2 * px + py
    me = 2 * kme + pc
    kidx = jnp.reshape(kme, (1,)).astype(jnp.int32)

    shards = [ffn1_w_in, ffn1_w_out, jnp.swapaxes(w_mix_in, 1, 2), w_mix_out, ffn2_w_in, ffn2_w_out]
    started_weights = {}

    def start_weights(l, after):
        placed = _place_weights(shards, l, kidx, after)
        views = [a.reshape(a.shape[0], 2, a.shape[1] // 2, a.shape[2]) for a in placed]
        *handles, token = _weight_send_start(views, BIG_KINDS, "weight_send_start_%d" % l)
        started_weights[l] = handles
        return token

    cact = _silu_rows(c)
    pre = _pack([cact, conv_w, s5_glu_w])
    pre_all = _all_gather8(pre, "gather_prelude")
    token = start_weights(0, pre_all)
    parts = [_unpack(pre_all[dev], [cact.shape, conv_w.shape, s5_glu_w.shape]) for dev in range(N_DEV)]
    cact_all = pre_all[:, :d // LANES, :].reshape(N_DEV, d)
    conv_full = jnp.concatenate([parts[2 * k][1] for k in range(N_CHIP)], axis=2)
    glu_full = jnp.concatenate([parts[2 * k][2] for k in range(N_CHIP)], axis=1)

    n_ada = ada_w.shape[2]
    ada_b_loc = lax.dynamic_slice_in_dim(ada_b, kme * n_ada, n_ada, axis=1).reshape(nl, 1, n_ada) + token[0, 0]
    cond_part = _cond_fwd(cact_all, ada_w, ada_b_loc)
    cond_all = _all_gather8(cond_part.reshape(nl * N_DEV, n_ada), "gather_cond").reshape(N_DEV, nl, N_DEV, n_ada)
    cond_me = jnp.concatenate(
        [lax.dynamic_index_in_dim(cond_all[2 * k], me, axis=1, keepdims=False) for k in range(N_CHIP)], axis=1)
    token = cond_all
    for l in range(1, nl):
        token = start_weights(l, token)
    cond = cond_me.reshape(nl, 9, d) + token[0, 0]

    forwarding = {}

    def prefetch_weights(l, after):
        if l >= nl:
            return 0.0
        send_sems, recv_sems, *views = started_weights.pop(l)
        views = _weight_send_wait(send_sems, recv_sems, views, BIG_KINDS, after, "weight_send_wait_%d" % l)
        forwarding[l] = _split_start("weight_forward_start_%d" % l, views, 3 * len(views), _forward_copies(BIG_KINDS))
        return forwarding[l][-1][0, 0]

    def fetch_weights(l, after):
        if l not in forwarding:
            prefetch_weights(l, after)
        views = _split_wait("weight_forward_wait_%d" % l, forwarding.pop(l), _forward_copies(BIG_KINDS), after)
        full = [v.reshape(2, 2 * v.shape[2], v.shape[3]) if kind == "col" else v.reshape(-1, v.shape[3])
                for v, kind in zip(views, BIG_KINDS)]
        return full, 0.0

    ck = jnp.stack([pc, kme]).astype(jnp.int32)
    scattering, sharing = [], []
    stacks = {n: None for n in BIG_NAMES}
    groups = ((["ffn1_w_in", "ffn2_w_in"], 16, "adamw_w_in"),
              (["ffn1_w_out", "w_mix_in", "w_mix_out", "ffn2_w_out"], 8, "adamw_w_out"))

    def as_reduced(t):
        return {n: jnp.swapaxes(t[n], 1, 2) if n == "w_mix_in" else t[n] for n in BIG_NAMES}

    w_r, m_r, v_r = as_reduced(w), as_reduced(mom), as_reduced(vel)

    def apply_adamw(l, fulls):
        g = {n: f.reshape(2 * f.shape[1], f.shape[2]) for n, f in zip(BIG_NAMES, fulls)}
        for names, steps, call in groups:
            outs = _adamw_layer([w_r[n] for n in names], [g[n] for n in names], [m_r[n] for n in names],
                                [v_r[n] for n in names], [stacks[n] for n in names], l, call, steps)
            stacks.update(zip(names, outs))

    def retire_share(after):
        l2, shared = sharing.pop(0)
        apply_adamw(l2, _sibling_share_wait(shared, after, "grad_share_wait_%d" % l2))

    def retire_scatter(after):
        l1, scattered = scattering.pop(0)
        sharing.append((l1, _finish_big_grads(scattered, BIG_KINDS, ck, after, l1)))

    def retire(after):
        if sharing:
            retire_share(after)
        if scattering:
            retire_scatter(after)

    def emit_grads(l, grads_l):
        exchanged = _exchange_big_grads(grads_l, BIG_KINDS, l)

        def layer_done(after):
            started = _scatter_big_grads(exchanged, BIG_KINDS, ck, after, l)
            retire(after)
            scattering.append((l, started))
            return started[-1]

        return exchanged[-1][0, 0], layer_done

    p = {n: w[n] for n in SMALL_NAMES}
    p["conv_w"], p["s5_glu_w"] = conv_full, glu_full
    loss, dx, small, dcond, first_layer_done = _local_step(x[0], loss_target[0], cond, fetch_weights, prefetch_weights,
                                                           p, emit_grads)

    small_order = [n for n in SMALL_NAMES if n != "ada_b"]
    packed = _pack([dcond] + [small[n] for n in small_order]).astype(BF16)
    mine = lax.dynamic_update_slice(lax.empty((N_DEV,) + packed.shape, BF16), packed[None], (me, 0, 0))
    gathering = _split_start("small_grads_send_start", [mine], 4, _gather_first_copies())
    scatter_token = first_layer_done(gathering[-1])
    while sharing:
        retire_share(scatter_token)
    arrived, = _split_wait("small_grads_send_wait", gathering, _gather_first_copies(), stacks[BIG_NAMES[0]][0])
    gathered_small = _gather_pass_on(arrived, "small_grads_pass_on")
    total = _sum8(gathered_small)
    shapes = [dcond.shape] + [small[n].shape for n in small_order]
    tot = dict(zip(["ada_b"] + small_order, _unpack(total, shapes)))
    grads = {n: tot[n] for n in SMALL_NAMES}
    grads["ada_b"] = tot["ada_b"].reshape(nl, 9 * d)
    grads["conv_w"] = lax.dynamic_slice_in_dim(tot["conv_w"], kme * conv_w.shape[2], conv_w.shape[2], axis=2)
    grads["s5_glu_w"] = lax.dynamic_slice_in_dim(tot["s5_glu_w"], kme * s5_glu_w.shape[1], s5_glu_w.shape[1], axis=1)

    dcond_all = gathered_small.reshape(N_DEV, -1)[:, :dcond.size].reshape(N_DEV, nl, 9 * d)
    dcond_loc = jnp.swapaxes(lax.dynamic_slice_in_dim(dcond_all, kme * n_ada, n_ada, axis=2), 0, 1)
    g_ada, d_ada, m_ada, v_ada = _ada_w_update(cact_all, dcond_loc, ada_w, m_ada_w, v_ada_w)

    while scattering or sharing:
        retire(g_ada)
    delta, new_m, new_v = {}, {}, {}
    for n in BIG_NAMES:
        grads[n], delta[n], new_m[n], new_v[n] = (jnp.swapaxes(a, 1, 2) if n == "w_mix_in" else a for a in stacks[n])

    grads["ada_w"], delta["ada_w"], new_m["ada_w"], new_v["ada_w"] = g_ada, d_ada, m_ada, v_ada
    wide = ("s5_b_re", "s5_b_im")
    for names, call, steps in (([n for n in SMALL_NAMES if n not in wide], "adamw_small", 1),
                               (list(wide), "adamw_s5_b", DEPTH)):
        outs = _adamw_many(*[[_as_rows(t[n]) for n in names] for t in (w, grads, mom, vel)], call, steps)
        for res, o in zip((delta, new_m, new_v), outs):
            res.update({n: a.reshape(w[n].shape) for n, a in zip(names, o)})

    loss_total = lax.psum(loss[0, 0], ("x", "y", "c"))
    return (loss_total, dx[None], *[grads[n] for n in WEIGHT_ORDER], *[delta[n] for n in WEIGHT_ORDER],
            *[new_m[n] for n in WEIGHT_ORDER], *[new_v[n] for n in WEIGHT_ORDER])
```

```python
import math

import jax
import jax.numpy as jnp
from jax import lax
from jax.experimental import pallas as pl
from jax.experimental.pallas import tpu as pltpu

F32, BF16 = jnp.float32, jnp.bfloat16
EPS = 1e-6
DEPTH = 4
N_DEV = 8
N_CHIP = 4
W_GRP = 256
CHUNK = 128
N_SEG = 8
LANES = 128
FFN_TF = 256
FFN_TF_WIDE = 1408
FFN_TM_WIDE = 512
VMEM_LIMIT = 56 * 1024 * 1024
ADAM_LR, ADAM_B1, ADAM_B2, ADAM_EPS, ADAM_WD, ADAM_STEP = 0.001, 0.9, 0.999, 1e-08, 0.01, 10
MESH_ID = pl.DeviceIdType.MESH
HI = lax.Precision.HIGHEST
ANY = pl.BlockSpec(memory_space=pl.ANY)


def _cp(**kw):
    return pltpu.CompilerParams(vmem_limit_bytes=VMEM_LIMIT, **kw)


def _dot(a, b):
    return jnp.dot(a.astype(BF16), b.astype(BF16), preferred_element_type=F32)


def _dot_nt(a, b):
    return lax.dot_general(a.astype(BF16), b.astype(BF16), (((1,), (1,)), ((), ())), preferred_element_type=F32)


def _dot_tn(a, b):
    return lax.dot_general(a.astype(BF16), b.astype(BF16), (((0,), (0,)), ((), ())), preferred_element_type=F32)


def _dot_hi(a, b):
    return jnp.dot(a, b, preferred_element_type=F32, precision=HI)


def _gelu(x):
    k = 0.7978845608028654
    t = jnp.tanh(k * (x + 0.044715 * x * x * x))
    return 0.5 * x * (1.0 + t), t


def _gelu_grad(x, t):
    k = 0.7978845608028654
    return 0.5 * (1.0 + t) + 0.5 * x * (1.0 - t * t) * k * (1.0 + 3.0 * 0.044715 * x * x)


def _iota(shape, axis):
    return lax.broadcasted_iota(jnp.int32, shape, axis)


def _full(shape):
    nd = len(shape)
    return pl.BlockSpec(shape, lambda *_: (0,) * nd)


def _norm_mod(xv, g, shift, scale):
    r = lax.rsqrt(jnp.mean(xv * xv, axis=-1, keepdims=True) + EPS)
    return (xv * r * g) * (1.0 + scale) + shift


def _norm_mod_bwd(xv, g, scale, dh):
    r = lax.rsqrt(jnp.mean(xv * xv, axis=-1, keepdims=True) + EPS)
    xh = xv * r
    n = xh * g
    dsh = jnp.sum(dh, axis=0, keepdims=True)
    dsc = jnp.sum(dh * n, axis=0, keepdims=True)
    dn = dh * (1.0 + scale)
    dg = jnp.sum(dn * xh, axis=0, keepdims=True)
    dxh = dn * g
    dx = r * (dxh - xh * jnp.mean(dxh * xh, axis=-1, keepdims=True))
    return dx, dsh, dsc, dg


def _tm(s):
    return min(s, 1024)


def _ffn_fwd(x, mod, g, wi, wo):
    s, d = x.shape
    f = wo.shape[0]
    tf, tm = FFN_TF_WIDE, min(s, FFN_TM_WIDE)
    nf, nt = f // tf, s // tm

    def body(x_ref, mod_ref, g_ref, wa_ref, wb_ref, wo_ref, xn_ref, h_ref, a_ref, b_ref, o_ref, acc):
        j = pl.program_id(1)

        @pl.when(j == 0)
        def _():
            hh = _norm_mod(x_ref[...], g_ref[...], mod_ref[0:1, :], mod_ref[1:2, :])
            h_ref[...] = hh.astype(BF16)
            acc[...] = jnp.zeros_like(acc)

        h = h_ref[...]
        a = jnp.dot(h, wa_ref[...], preferred_element_type=F32)
        b = jnp.dot(h, wb_ref[...], preferred_element_type=F32)
        a_ref[...] = a.astype(BF16)
        b_ref[...] = b.astype(BF16)
        u = (a * jax.nn.sigmoid(a)) * b
        acc[...] += jnp.dot(u.astype(BF16), wo_ref[...], preferred_element_type=F32)

        @pl.when(j == nf - 1)
        def _():
            o = acc[...]
            o_ref[...] = o.astype(BF16)
            xn_ref[...] = x_ref[...] + 0.5 * mod_ref[2:3, :] * o

    row = pl.BlockSpec((tm, d), lambda i, j: (i, 0))
    chunk = pl.BlockSpec((tm, tf), lambda i, j: (i, j))
    return pl.pallas_call(
        body, name="ffn_fwd", grid=(nt, nf),
        in_specs=[row, _full((3, d)), _full((1, d)),
                  pl.BlockSpec((None, d, tf), lambda i, j: (0, 0, j)),
                  pl.BlockSpec((None, d, tf), lambda i, j: (1, 0, j)),
                  pl.BlockSpec((tf, d), lambda i, j: (j, 0))],
        out_specs=[row, row, chunk, chunk, row],
        out_shape=[jax.ShapeDtypeStruct((s, d), F32), jax.ShapeDtypeStruct((s, d), BF16),
                   jax.ShapeDtypeStruct((s, f), BF16), jax.ShapeDtypeStruct((s, f), BF16),
                   jax.ShapeDtypeStruct((s, d), BF16)],
        scratch_shapes=[pltpu.VMEM((tm, d), F32)],
        compiler_params=_cp(dimension_semantics=("parallel", "arbitrary")),
    )(x, mod, g, wi, wi, wo)


def _ffn_bwd_main(dxo, o, gate, h, a, b, wo):
    s, d = dxo.shape
    f = wo.shape[0]
    tf = FFN_TF
    nf = f // tf

    def body(dxo_ref, o_ref, gate_ref, h_ref, a_ref, b_ref, wo_ref, dza_ref, dzb_ref, dwi_ref, dwo_ref, dg_ref, do_s):
        @pl.when(pl.program_id(0) == 0)
        def _():
            dxv = dxo_ref[...]
            do_s[...] = (0.5 * gate_ref[...] * dxv).astype(BF16)
            dg_ref[...] = 0.5 * jnp.sum(o_ref[...].astype(F32) * dxv, axis=0, keepdims=True)

        dov = do_s[...]
        hv = h_ref[...]
        du = lax.dot_general(dov, wo_ref[...], (((1,), (1,)), ((), ())), preferred_element_type=F32)
        av = a_ref[...].astype(F32)
        bv = b_ref[...].astype(F32)
        sa = jax.nn.sigmoid(av)
        si = av * sa
        u = (si * bv).astype(BF16)
        da = (du * bv * (sa * (1.0 + av * (1.0 - sa)))).astype(BF16)
        db = (du * si).astype(BF16)
        dza_ref[...] = da
        dzb_ref[...] = db
        dwo_ref[...] = _dot_tn(u, dov).astype(BF16)
        dwi_ref[0] = _dot_tn(hv, da).astype(BF16)
        dwi_ref[1] = _dot_tn(hv, db).astype(BF16)

    chunk = pl.BlockSpec((s, tf), lambda j: (0, j))
    once = lambda: pl.BlockSpec((s, d), lambda j: (0, 0), pipeline_mode=pl.Buffered(1))
    return pl.pallas_call(
        body, name="ffn_bwd_main", grid=(nf,),
        in_specs=[once(), once(), _full((1, d)), once(), chunk, chunk, pl.BlockSpec((tf, d), lambda j: (j, 0))],
        out_specs=[chunk, chunk, pl.BlockSpec((2, d, tf), lambda j: (0, 0, j)),
                   pl.BlockSpec((tf, d), lambda j: (j, 0)), _full((1, d))],
        out_shape=[jax.ShapeDtypeStruct((s, f), BF16), jax.ShapeDtypeStruct((s, f), BF16),
                   jax.ShapeDtypeStruct((2, d, f), BF16), jax.ShapeDtypeStruct((f, d), BF16),
                   jax.ShapeDtypeStruct((1, d), F32)],
        scratch_shapes=[pltpu.VMEM((s, d), BF16)],
        compiler_params=_cp(dimension_semantics=("arbitrary",)),
    )(dxo, o, gate, h, a, b, wo)


def _ffn_bwd_in(dza, dzb, wi, x, dxo, mod, g):
    s, d = x.shape
    f = dza.shape[1]
    tf, tm = FFN_TF_WIDE, min(s, FFN_TM_WIDE)
    nf, nt = f // tf, s // tm

    def body(dza_ref, dzb_ref, wa_ref, wb_ref, x_ref, dxo_ref, mod_ref, g_ref, dx_ref, rows_ref, acc):
        j, i = pl.program_id(0), pl.program_id(1)
        rows = pl.ds(pl.multiple_of(i * tm, tm), tm)

        @pl.when(jnp.logical_and(i == 0, j == 0))
        def _():
            rows_ref[...] = jnp.zeros_like(rows_ref)

        part = (lax.dot_general(dza_ref[...], wa_ref[...], (((1,), (1,)), ((), ())), preferred_element_type=F32)
                + lax.dot_general(dzb_ref[...], wb_ref[...], (((1,), (1,)), ((), ())), preferred_element_type=F32))

        @pl.when(j == 0)
        def _():
            acc[rows, :] = part

        @pl.when(jnp.logical_and(j > 0, j < nf - 1))
        def _():
            acc[rows, :] += part

        @pl.when(j == nf - 1)
        def _():
            dh = part + acc[rows, :] if nf > 1 else part
            dx, dsh, dsc, dg = _norm_mod_bwd(x_ref[...], g_ref[...], mod_ref[1:2, :], dh)
            dx_ref[...] = dx + dxo_ref[...]
            rows_ref[0:1, :] += dsh
            rows_ref[1:2, :] += dsc
            rows_ref[2:3, :] += dg

    late = pl.BlockSpec((tm, d), lambda j, i: (jnp.where(j == nf - 1, i, 0), 0))
    chunk = pl.BlockSpec((tm, tf), lambda j, i: (i, j))
    return pl.pallas_call(
        body, name="ffn_bwd_in", grid=(nf, nt),
        in_specs=[chunk, chunk,
                  pl.BlockSpec((None, d, tf), lambda j, i: (0, 0, j)),
                  pl.BlockSpec((None, d, tf), lambda j, i: (1, 0, j)),
                  late, late, _full((3, d)), _full((1, d))],
        out_specs=[late, _full((8, d))],
        out_shape=[jax.ShapeDtypeStruct((s, d), F32), jax.ShapeDtypeStruct((8, d), F32)],
        scratch_shapes=[pltpu.VMEM((s, d), F32)],
        compiler_params=_cp(dimension_semantics=("arbitrary", "arbitrary")),
    )(dza, dzb, wi, wi, x, dxo, mod, g)


def _mix_in_fwd(x, mod, g, wmit):
    s, d = x.shape
    p = wmit.shape[0]
    tm = _tm(s)

    def body(x_ref, mod_ref, g_ref, w_ref, z_ref, h_ref):
        hh = _norm_mod(x_ref[...], g_ref[...], mod_ref[0:1, :], mod_ref[1:2, :]).astype(BF16)
        h_ref[...] = hh
        z_ref[...] = lax.dot_general(hh, w_ref[...], (((1,), (1,)), ((), ())), preferred_element_type=F32)

    row = pl.BlockSpec((tm, d), lambda i: (i, 0))
    return pl.pallas_call(
        body, name="mix_in_fwd", grid=(s // tm,),
        in_specs=[row, _full((3, d)), _full((1, d)), _full((p, d))],
        out_specs=[pl.BlockSpec((tm, p), lambda i: (i, 0)), row],
        out_shape=[jax.ShapeDtypeStruct((s, p), F32), jax.ShapeDtypeStruct((s, d), BF16)],
        compiler_params=_cp(dimension_semantics=("parallel",)),
    )(x, mod, g, wmit)


def _mix_in_bwd(dzs, h, wmit, x, dxo, mod, g):
    s, d = x.shape
    p = wmit.shape[0]
    tm = min(s, 512)
    nt = s // tm

    def body(za_ref, zb_ref, zc_ref, zd_ref, h_ref, w_ref, x_ref, dxo_ref, mod_ref, g_ref,
             dx_ref, rows_ref, dw_ref, acc):
        i = pl.program_id(0)

        @pl.when(i == 0)
        def _():
            rows_ref[...] = jnp.zeros_like(rows_ref)
            acc[...] = jnp.zeros_like(acc)

        dz = jnp.concatenate([za_ref[...], zb_ref[...], zc_ref[...], zd_ref[...]], axis=1).astype(BF16)
        acc[...] += _dot_tn(dz, h_ref[...])
        dh = jnp.dot(dz, w_ref[...], preferred_element_type=F32)
        dx, dsh, dsc, dg = _norm_mod_bwd(x_ref[...], g_ref[...], mod_ref[1:2, :], dh)
        dx_ref[...] = dx + dxo_ref[...]
        rows_ref[0:1, :] += dsh
        rows_ref[1:2, :] += dsc
        rows_ref[2:3, :] += dg

        @pl.when(i == nt - 1)
        def _():
            dw_ref[...] = acc[...].astype(BF16)

    row = pl.BlockSpec((tm, d), lambda i: (i, 0))
    zspecs = [pl.BlockSpec((tm, z.shape[1]), lambda i: (i, 0)) for z in dzs]
    return pl.pallas_call(
        body, name="mix_in_bwd", grid=(nt,),
        in_specs=zspecs + [row, _full((p, d)), row, row, _full((3, d)), _full((1, d))],
        out_specs=[row, _full((8, d)), _full((p, d))],
        out_shape=[jax.ShapeDtypeStruct((s, d), F32), jax.ShapeDtypeStruct((8, d), F32),
                   jax.ShapeDtypeStruct((p, d), BF16)],
        scratch_shapes=[pltpu.VMEM((p, d), F32)],
        compiler_params=_cp(dimension_semantics=("arbitrary",)),
    )(*dzs, h, wmit, x, dxo, mod, g)


def _group_norm(ys, mng):
    outs, hats, rs = [], [], []
    for k, y in enumerate(ys):
        r = lax.rsqrt(jnp.mean(y * y, axis=-1, keepdims=True) + EPS)
        yh = y * r
        hats.append(yh)
        rs.append(r)
        outs.append(yh * mng[:, k * W_GRP:(k + 1) * W_GRP])
    return jnp.concatenate(outs, axis=1), hats, rs


def _s5_glu(y, gw, gb):
    yg, t = _gelu(y)
    gate = jax.nn.sigmoid(_dot(yg, gw) + gb)
    return yg * gate, yg, t, gate


def _mix_out_fwd(ys, glu, mng, wmo, x, gate):
    s, d = x.shape
    tm = _tm(s)

    def body(ya, yb, yc, ypre, gw_ref, gb_ref, mng_ref, w_ref, x_ref, gate_ref, xn_ref, m_ref):
        yd = _s5_glu(ypre[...], gw_ref[...], gb_ref[...])[0]
        yn, _, _ = _group_norm([ya[...], yb[...], yc[...], yd], mng_ref[...])
        m = jnp.dot(yn.astype(BF16), w_ref[...], preferred_element_type=F32)
        m_ref[...] = m
        xn_ref[...] = x_ref[...] + gate_ref[...] * m

    row = pl.BlockSpec((tm, d), lambda i: (i, 0))
    grp = pl.BlockSpec((tm, W_GRP), lambda i: (i, 0))
    return pl.pallas_call(
        body, name="mix_out_fwd", grid=(s // tm,),
        in_specs=[grp, grp, grp, grp, _full((W_GRP, W_GRP)), _full((1, W_GRP)), _full((1, d)), _full((d, d)), row,
                  _full((1, d))],
        out_specs=[row, row],
        out_shape=[jax.ShapeDtypeStruct((s, d), F32), jax.ShapeDtypeStruct((s, d), F32)],
        compiler_params=_cp(dimension_semantics=("parallel",)),
    )(*ys, *glu, mng, wmo, x, gate)


def _mix_out_bwd(dxo, m, gate, ys, glu, mng, wmo):
    s, d = dxo.shape
    tm = min(s, 512)
    nt = s // tm

    def body(dxo_ref, m_ref, gate_ref, ya, yb, yc, ypre, gw_ref, gb_ref, mng_ref, w_ref,
             dya, dyb, dyc, dypre, dgate_ref, dmng_ref, dw_ref, dgw_ref, dgb_ref, acc):
        i = pl.program_id(0)

        @pl.when(i == 0)
        def _():
            dgate_ref[...] = jnp.zeros_like(dgate_ref)
            dmng_ref[...] = jnp.zeros_like(dmng_ref)
            dgw_ref[...] = jnp.zeros_like(dgw_ref)
            dgb_ref[...] = jnp.zeros_like(dgb_ref)
            acc[...] = jnp.zeros_like(acc)

        dxv = dxo_ref[...]
        dgate_ref[...] += jnp.sum(m_ref[...] * dxv, axis=0, keepdims=True)
        dm = (gate_ref[...] * dxv).astype(BF16)
        mng = mng_ref[...]
        gw = gw_ref[...]
        yp = ypre[...]
        yd, yg, t, glu_gate = _s5_glu(yp, gw, gb_ref[...])
        yn, hats, rs = _group_norm([ya[...], yb[...], yc[...], yd], mng)
        acc[...] += _dot_tn(yn, dm)
        dyn = lax.dot_general(dm, w_ref[...], (((1,), (1,)), ((), ())), preferred_element_type=F32)
        dmng_parts, dys = [], []
        for k, (yh, r) in enumerate(zip(hats, rs)):
            dk = dyn[:, k * W_GRP:(k + 1) * W_GRP]
            dmng_parts.append(jnp.sum(dk * yh, axis=0, keepdims=True))
            dyh = dk * mng[:, k * W_GRP:(k + 1) * W_GRP]
            dys.append(r * (dyh - yh * jnp.mean(dyh * yh, axis=-1, keepdims=True)))
        dmng_ref[...] += jnp.concatenate(dmng_parts, axis=1)
        dya[...], dyb[...], dyc[...] = dys[0], dys[1], dys[2]
        dyd = dys[3]
        dlin = dyd * yg * glu_gate * (1.0 - glu_gate)
        dgw_ref[...] += _dot_tn(yg, dlin)
        dgb_ref[...] += jnp.sum(dlin, axis=0, keepdims=True)
        dypre[...] = (dyd * glu_gate + _dot_nt(dlin, gw)) * _gelu_grad(yp, t)

        @pl.when(i == nt - 1)
        def _():
            dw_ref[...] = acc[...].astype(BF16)

    row = pl.BlockSpec((tm, d), lambda i: (i, 0))
    grp = pl.BlockSpec((tm, W_GRP), lambda i: (i, 0))
    return pl.pallas_call(
        body, name="mix_out_bwd", grid=(nt,),
        in_specs=[row, row, _full((1, d)), grp, grp, grp, grp, _full((W_GRP, W_GRP)), _full((1, W_GRP)), _full((1, d)),
                  _full((d, d))],
        out_specs=[grp, grp, grp, grp, _full((1, d)), _full((1, d)), _full((d, d)), _full((W_GRP, W_GRP)),
                   _full((1, W_GRP))],
        out_shape=[jax.ShapeDtypeStruct((s, W_GRP), F32)] * 4
        + [jax.ShapeDtypeStruct((1, d), F32), jax.ShapeDtypeStruct((1, d), F32), jax.ShapeDtypeStruct((d, d), BF16),
           jax.ShapeDtypeStruct((W_GRP, W_GRP), F32), jax.ShapeDtypeStruct((1, W_GRP), F32)],
        scratch_shapes=[pltpu.VMEM((d, d), F32)],
        compiler_params=_cp(dimension_semantics=("arbitrary",)),
    )(dxo, m, gate, *ys, *glu, mng, wmo)


def _sgu_consts():
    r = _iota((W_GRP, W_GRP), 0) >> 6
    c = _iota((W_GRP, W_GRP), 1) >> 6
    avg = jnp.where(r == c, 1.0 / 64.0, 0.0).astype(F32)
    tril = _iota((CHUNK, CHUNK), 0) >= _iota((CHUNK, CHUNK), 1)
    head = _iota((CHUNK, W_GRP), 1) >> 6
    return avg, tril, head


def _sgu_pre(za, avg):
    zg, t = _gelu(za)
    u, v = zg[:, :W_GRP], zg[:, W_GRP:]
    mu = _dot_hi(v, avg)
    vc = v - mu
    r = lax.rsqrt(_dot_hi(vc * vc, avg) + EPS)
    return t, u, vc * r, r


def _sgu_fwd(z, sgu_w, bias_full):
    s = z.shape[0]
    tm = min(s, 512)

    def body(za_ref, w_ref, bias_ref, ya_ref):
        avg, tril, head = _sgu_consts()
        _, u, vn, _ = _sgu_pre(za_ref[...], avg)
        wm = [jnp.where(tril, w_ref[h], 0.0).astype(BF16) for h in range(4)]
        vb = vn.astype(BF16)
        for n in range(tm // CHUNK):
            rows = slice(n * CHUNK, (n + 1) * CHUNK)
            mixed = bias_ref[...]
            for h in range(4):
                mixed = mixed + jnp.where(head == h, jnp.dot(wm[h], vb[rows], preferred_element_type=F32), 0.0)
            ya_ref[rows, :] = u[rows] * mixed

    return pl.pallas_call(
        body, name="sgu_fwd", grid=(s // tm,),
        in_specs=[pl.BlockSpec((tm, 2 * W_GRP), lambda i: (i, 0)), _full((4, CHUNK, CHUNK)), _full((CHUNK, W_GRP))],
        out_specs=pl.BlockSpec((tm, W_GRP), lambda i: (i, 0)),
        out_shape=jax.ShapeDtypeStruct((s, W_GRP), F32),
        compiler_params=_cp(dimension_semantics=("parallel",)),
    )(z, sgu_w, bias_full)


def _sgu_bwd(z, dya, sgu_w, bias_full):
    s = z.shape[0]
    tm = min(s, 512)
    nt = s // tm

    def body(za_ref, dya_ref, w_ref, bias_ref, dza_ref, dw_ref, db_ref, du_s, dvn_s):
        i = pl.program_id(0)

        @pl.when(i == 0)
        def _():
            dw_ref[...] = jnp.zeros_like(dw_ref)
            db_ref[...] = jnp.zeros_like(db_ref)

        avg, tril, head = _sgu_consts()
        za = za_ref[...]
        t, u, vn, r = _sgu_pre(za, avg)
        wm = [jnp.where(tril, w_ref[h], 0.0).astype(BF16) for h in range(4)]
        vb = vn.astype(BF16)
        dya = dya_ref[...]
        dw = [jnp.zeros((CHUNK, CHUNK), F32) for _ in range(4)]
        db = jnp.zeros((CHUNK, W_GRP), F32)
        for n in range(tm // CHUNK):
            rows = slice(n * CHUNK, (n + 1) * CHUNK)
            mixed = bias_ref[...]
            for h in range(4):
                mixed = mixed + jnp.where(head == h, jnp.dot(wm[h], vb[rows], preferred_element_type=F32), 0.0)
            dmix = dya[rows] * u[rows]
            du_s[rows, :] = dya[rows] * mixed
            db = db + dmix
            dmb = dmix.astype(BF16)
            dvn = jnp.zeros((CHUNK, W_GRP), F32)
            for h in range(4):
                dmh = jnp.where(head == h, dmix, 0.0)
                dw[h] = dw[h] + _dot_nt(dmh, vb[rows])
                dvn = dvn + jnp.where(head == h, _dot_tn(wm[h], dmb), 0.0)
            dvn_s[rows, :] = dvn
        for h in range(4):
            dw_ref[h] += jnp.where(tril, dw[h], 0.0)
        sel = ((_iota((W_GRP, CHUNK), 0) >> 6) == _iota((W_GRP, CHUNK), 1)).astype(F32)
        db_ref[...] += _dot_hi(db, sel)
        dvn = dvn_s[...]
        dv = r * (dvn - _dot_hi(dvn, avg) - vn * _dot_hi(dvn * vn, avg))
        dzg = jnp.concatenate([du_s[...], dv], axis=1)
        dza_ref[...] = dzg * _gelu_grad(za, t)

    return pl.pallas_call(
        body, name="sgu_bwd", grid=(nt,),
        in_specs=[pl.BlockSpec((tm, 2 * W_GRP), lambda i: (i, 0)), pl.BlockSpec((tm, W_GRP), lambda i: (i, 0)),
                  _full((4, CHUNK, CHUNK)), _full((CHUNK, W_GRP))],
        out_specs=[pl.BlockSpec((tm, 2 * W_GRP), lambda i: (i, 0)), _full((4, CHUNK, CHUNK)), _full((CHUNK, CHUNK))],
        out_shape=[jax.ShapeDtypeStruct((s, 2 * W_GRP), F32), jax.ShapeDtypeStruct((4, CHUNK, CHUNK), F32),
                   jax.ShapeDtypeStruct((CHUNK, CHUNK), F32)],
        scratch_shapes=[pltpu.VMEM((tm, W_GRP), F32), pltpu.VMEM((tm, W_GRP), F32)],
        compiler_params=_cp(dimension_semantics=("arbitrary",)),
    )(z, dya, sgu_w, bias_full)


def _shift_down(x, k):
    return jnp.where(_iota(x.shape, 0) < k, 0.0, pltpu.roll(x, k, 0))


def _shift_up(x, k):
    n = x.shape[0]
    return jnp.where(_iota(x.shape, 0) >= n - k, 0.0, pltpu.roll(x, n - k, 0))


def _by_pool_group(shape, v2, v4, v8, v16):
    col = _iota(shape, 1)
    return jnp.where(col < 64, v2, jnp.where(col < 128, v4, jnp.where(col < 192, v8, v16)))


def _pool_core(zb, pw2):
    s2 = zb + _shift_down(zb, 1)
    s4 = s2 + _shift_down(s2, 2)
    s8 = s4 + _shift_down(s4, 4)
    s16 = s8 + _shift_down(s8, 8)
    win = _by_pool_group(zb.shape, s2, s4, s8, s16)
    wlen = _by_pool_group(zb.shape, 2.0, 4.0, 8.0, 16.0)
    cnt = jnp.minimum((_iota(zb.shape, 0) + 1).astype(F32), wlen)
    p = win / cnt - zb
    wt = jnp.tile(pw2, (1, 4))
    wbd = jnp.where((_iota(wt.shape, 0) >> 6) == (_iota(wt.shape, 1) >> 6), wt, 0.0).astype(BF16)
    return p, cnt, wbd


def _conv_core(zc, cw):
    bg, cg, xh = zc[:, :W_GRP], zc[:, W_GRP:2 * W_GRP], zc[:, 2 * W_GRP:]
    y = cg * xh
    y1, y2 = _shift_down(y, 1), _shift_down(y, 2)
    out = cw[2:3, :] * y + cw[1:2, :] * y1 + cw[0:1, :] * y2
    return bg, cg, xh, y, y1, y2, out


def _poolconv_fwd(z, pw2, pscale, cw):
    s = z.shape[0]

    def body(zb_ref, zc_ref, pw_ref, ps_ref, cw_ref, yb_ref, yc_ref):
        p, _, wbd = _pool_core(zb_ref[...], pw_ref[...])
        yb_ref[...] = jnp.dot(p.astype(BF16), wbd, preferred_element_type=F32) * ps_ref[...]
        bg, _, _, _, _, _, out = _conv_core(zc_ref[...], cw_ref[...])
        yc_ref[...] = bg * out

    return pl.pallas_call(
        body, name="poolconv_fwd", grid=(1,),
        in_specs=[pl.BlockSpec((s, W_GRP), lambda i: (0, 2)), pl.BlockSpec((s, 3 * W_GRP), lambda i: (0, 1)),
                  _full((W_GRP, 64)), _full((1, W_GRP)), _full((3, W_GRP))],
        out_specs=[_full((s, W_GRP)), _full((s, W_GRP))],
        out_shape=[jax.ShapeDtypeStruct((s, W_GRP), F32)] * 2,
        compiler_params=_cp(dimension_semantics=("arbitrary",)),
    )(z, z, pw2, pscale, cw)


def _poolconv_bwd(z, dyb, dyc, pw2, pscale, cw):
    s = z.shape[0]

    def body(zb_ref, zc_ref, dyb_ref, dyc_ref, pw_ref, ps_ref, cw_ref, dzb_ref, dzc_ref, dw_ref, dps_ref, dcw_ref):
        zb = zb_ref[...]
        p, cnt, wbd = _pool_core(zb, pw_ref[...])
        pb = p.astype(BF16)
        out = jnp.dot(pb, wbd, preferred_element_type=F32)
        dyb = dyb_ref[...]
        dps_ref[...] = jnp.sum(dyb * out, axis=0, keepdims=True)
        dout = (dyb * ps_ref[...]).astype(BF16)
        dw = _dot_tn(pb, dout)
        dw_ref[...] = jnp.where((_iota(dw.shape, 0) >> 6) == (_iota(dw.shape, 1) >> 6), dw, 0.0)
        dp = lax.dot_general(dout, wbd, (((1,), (1,)), ((), ())), preferred_element_type=F32)
        dwin = dp / cnt
        t2 = dwin + _shift_up(dwin, 1)
        t4 = t2 + _shift_up(t2, 2)
        t8 = t4 + _shift_up(t4, 4)
        t16 = t8 + _shift_up(t8, 8)
        dzb_ref[...] = _by_pool_group(zb.shape, t2, t4, t8, t16) - dp

        cw = cw_ref[...]
        bg, cg, xh, y, y1, y2, out = _conv_core(zc_ref[...], cw)
        dyc = dyc_ref[...]
        dout = dyc * bg
        dcw_ref[...] = jnp.zeros_like(dcw_ref)
        dcw_ref[0:1, :] = jnp.sum(dout * y2, axis=0, keepdims=True)
        dcw_ref[1:2, :] = jnp.sum(dout * y1, axis=0, keepdims=True)
        dcw_ref[2:3, :] = jnp.sum(dout * y, axis=0, keepdims=True)
        dy = cw[2:3, :] * dout + cw[1:2, :] * _shift_up(dout, 1) + cw[0:1, :] * _shift_up(dout, 2)
        dzc_ref[...] = jnp.concatenate([dyc * out, dy * xh, dy * cg], axis=1)

    return pl.pallas_call(
        body, name="poolconv_bwd", grid=(1,),
        in_specs=[pl.BlockSpec((s, W_GRP), lambda i: (0, 2)), pl.BlockSpec((s, 3 * W_GRP), lambda i: (0, 1)),
                  _full((s, W_GRP)), _full((s, W_GRP)), _full((W_GRP, 64)), _full((1, W_GRP)), _full((3, W_GRP))],
        out_specs=[_full((s, W_GRP)), _full((s, 3 * W_GRP)), _full((W_GRP, W_GRP)), _full((1, W_GRP)), _full((8, W_GRP))],
        out_shape=[jax.ShapeDtypeStruct((s, W_GRP), F32), jax.ShapeDtypeStruct((s, 3 * W_GRP), F32),
                   jax.ShapeDtypeStruct((W_GRP, W_GRP), F32), jax.ShapeDtypeStruct((1, W_GRP), F32),
                   jax.ShapeDtypeStruct((8, W_GRP), F32)],
        compiler_params=_cp(dimension_semantics=("arbitrary",)),
    )(z, z, dyb, dyc, pw2, pscale, cw)


N_STATE = 1024
HALF_STATE = N_STATE // 2
HALF_CH = W_GRP // 2
N_SLAB = HALF_STATE // LANES


def _s5_disc(lre, lim, ldt):
    dt = jnp.exp(ldt)
    mag = jnp.exp(lre * dt)
    ang = lim * dt
    ar, ai = mag * jnp.cos(ang), mag * jnp.sin(ang)
    nr, ni = ar - 1.0, ai
    den = lre * lre + lim * lim
    kr = (nr * lre + ni * lim) / den
    ki = (ni * lre - nr * lim) / den
    return ar, ai, kr, ki


def _s5_mats(colp, br, bi, cr, ci):
    _, _, kr, ki = _s5_disc(colp[:, 0:1], colp[:, 1:2], colp[:, 2:3])
    bbr = kr * br - ki * bi
    bbi = kr * bi + ki * br
    bmask = (_iota((HALF_STATE, HALF_CH), 0) >> 6) == (_iota((HALF_STATE, HALF_CH), 1) >> 4)
    cmask = (_iota((HALF_CH, HALF_STATE), 0) >> 4) == (_iota((HALF_CH, HALF_STATE), 1) >> 6)
    btr = jnp.where(bmask, jnp.tile(bbr, (1, 8)), 0.0).astype(BF16)
    bti = jnp.where(bmask, jnp.tile(bbi, (1, 8)), 0.0).astype(BF16)
    ctr = jnp.where(cmask, jnp.tile(cr, (1, 8)), 0.0).astype(BF16)
    cti = jnp.where(cmask, jnp.tile(ci, (1, 8)), 0.0).astype(BF16)
    return kr, ki, btr, bti, ctr, cti, bmask, cmask


def _slab(q):
    return slice(q * LANES, (q + 1) * LANES)


def _cmul(ar, ai, br, bi):
    return ar * br - ai * bi, ar * bi + ai * br


def _sub_shift(x, k, up):
    row = _iota(x.shape, 0)
    if up:
        return jnp.where(row >= N_SEG - k, 0.0, pltpu.roll(x, N_SEG - k, 0))
    return jnp.where(row < k, 0.0, pltpu.roll(x, k, 0))


def _seg_rows(j):
    return pl.ds(pl.multiple_of(j * N_SEG, N_SEG), N_SEG)


def _interleave(src, dst, seg):
    def step(j, carry):
        dst[_seg_rows(j), :] = src[pl.ds(j, N_SEG, stride=seg), :]
        return carry
    lax.fori_loop(0, seg, step, 0)


def _deinterleave(src, dst, seg):
    def step(j, carry):
        dst[pl.ds(j, N_SEG, stride=seg), :] = src[_seg_rows(j), :]
        return carry
    lax.fori_loop(0, seg, step, 0)


def _scan(xr, xi, ar_row, ai_row, seg, reverse, states=None):
    nlog = int(math.log2(seg))
    assert (1 << nlog) == seg
    grads = []
    for q0 in range(0, N_SLAB, 4):
        qs = list(range(q0, q0 + 4))
        aq = [(jnp.broadcast_to(ar_row[:, _slab(q)], (N_SEG, LANES)),
               jnp.broadcast_to(ai_row[:, _slab(q)], (N_SEG, LANES))) for q in qs]
        zero = jnp.zeros((N_SEG, LANES), F32)

        def local(jj, carry, qs=qs, aq=aq):
            j = seg - 1 - jj if reverse else jj
            out = []
            for n, q in enumerate(qs):
                rows = _seg_rows(j)
                pr, pi = _cmul(aq[n][0], aq[n][1], carry[2 * n], carry[2 * n + 1])
                nr = pr + xr[q, rows, :]
                ni = pi + xi[q, rows, :]
                xr[q, rows, :] = nr
                xi[q, rows, :] = ni
                out += [nr, ni]
            return tuple(out)

        fin = lax.fori_loop(0, seg, local, (zero,) * 8)
        cins = []
        for n in range(4):
            er, ei = fin[2 * n], fin[2 * n + 1]
            pr, pi = aq[n]
            for _ in range(nlog):
                pr, pi = _cmul(pr, pi, pr, pi)
            yr, yi = er, ei
            for k in (1, 2, 4):
                sr, si = _cmul(pr, pi, _sub_shift(yr, k, reverse), _sub_shift(yi, k, reverse))
                yr, yi = yr + sr, yi + si
                pr, pi = _cmul(pr, pi, pr, pi)
            cins.append((_sub_shift(yr, 1, reverse), _sub_shift(yi, 1, reverse)))

        def fix(jj, carry, qs=qs, aq=aq, cins=cins):
            j = seg - 1 - jj if reverse else jj
            out, sums = [], []
            for n, q in enumerate(qs):
                rows = _seg_rows(j)
                pwr, pwi = carry[2 * n], carry[2 * n + 1]
                cr, ci = _cmul(pwr, pwi, cins[n][0], cins[n][1])
                v_r, v_i = xr[q, rows, :] + cr, xi[q, rows, :] + ci
                xr[q, rows, :] = v_r
                xi[q, rows, :] = v_i
                nr, ni = _cmul(pwr, pwi, aq[n][0], aq[n][1])
                out += [nr, ni]
                if states is not None:
                    prev = _seg_rows(j - 1)
                    p_r, p_i = states[0][q, prev, :], states[1][q, prev, :]
                    sums += [carry[8 + 2 * n] + v_r * p_r + v_i * p_i, carry[9 + 2 * n] - v_r * p_i + v_i * p_r]
            return tuple(out + sums)

        powers = tuple(v for pair in aq for v in pair)
        if states is None:
            lax.fori_loop(0, seg, fix, powers)
            continue
        assert reverse
        fix_last = lax.fori_loop(0, seg - 1, fix, powers + (zero,) * 8)
        first = _seg_rows(0)
        for n, q in enumerate(qs):
            cr, ci = _cmul(fix_last[2 * n], fix_last[2 * n + 1], cins[n][0], cins[n][1])
            v_r, v_i = xr[q, first, :] + cr, xi[q, first, :] + ci
            xr[q, first, :] = v_r
            xi[q, first, :] = v_i
            p_r = _sub_shift(states[0][q, _seg_rows(seg - 1), :], 1, False)
            p_i = _sub_shift(states[1][q, _seg_rows(seg - 1), :], 1, False)
            grads.append((jnp.sum(fix_last[8 + 2 * n] + v_r * p_r + v_i * p_i, axis=0, keepdims=True),
                          jnp.sum(fix_last[9 + 2 * n] - v_r * p_i + v_i * p_r, axis=0, keepdims=True)))
    return grads


def _s5_forward_states(u, btr, bti, ar_row, ai_row, xr, xi, seg):
    ub = u.astype(BF16)
    for q in range(N_SLAB):
        xr[q] = _dot_nt(ub, btr[_slab(q), :])
        xi[q] = _dot_nt(ub, bti[_slab(q), :])
    _scan(xr, xi, ar_row, ai_row, seg, False)


def _s5_readout(u, xr, xi, ctr, cti, d):
    y = d * u
    for q in range(N_SLAB):
        y = y + _dot_nt(xr[q], ctr[:, _slab(q)]) - _dot_nt(xi[q], cti[:, _slab(q)])
    return y


def _s5_param_specs():
    return [pl.BlockSpec((3, HALF_STATE), lambda i: (0, i)), pl.BlockSpec((HALF_STATE, 3), lambda i: (i, 0)),
            pl.BlockSpec((HALF_STATE, 16), lambda i: (i, 0)), pl.BlockSpec((HALF_STATE, 16), lambda i: (i, 0)),
            pl.BlockSpec((HALF_CH, 64), lambda i: (i, 0)), pl.BlockSpec((HALF_CH, 64), lambda i: (i, 0)),
            pl.BlockSpec((1, HALF_CH), lambda i: (0, i))]


def _s5_core_fwd(z, sp):
    s = z.shape[0]
    seg = s // N_SEG

    def body(u_ref, rowp, colp, br, bi, cr, ci, d_ref, y_ref, xr, xi, us, ys):
        ar, ai, _, _ = _s5_disc(rowp[0:1, :], rowp[1:2, :], rowp[2:3, :])
        _, _, btr, bti, ctr, cti, _, _ = _s5_mats(colp[...], br[...], bi[...], cr[...], ci[...])
        _interleave(u_ref, us, seg)
        u = us[...]
        _s5_forward_states(u, btr, bti, ar, ai, xr, xi, seg)
        ys[...] = _s5_readout(u, xr, xi, ctr, cti, d_ref[...])
        _deinterleave(ys, y_ref, seg)

    return pl.pallas_call(
        body, name="s5_core_fwd", grid=(2,),
        in_specs=[pl.BlockSpec((s, HALF_CH), lambda i: (0, 12 + i))] + _s5_param_specs(),
        out_specs=pl.BlockSpec((s, HALF_CH), lambda i: (0, i)),
        out_shape=jax.ShapeDtypeStruct((s, W_GRP), F32),
        scratch_shapes=[pltpu.VMEM((N_SLAB, s, LANES), F32)] * 2 + [pltpu.VMEM((s, HALF_CH), F32)] * 2,
        compiler_params=_cp(dimension_semantics=("parallel",)),
    )(z, *sp)


def _s5_core_bwd(z, dy, sp):
    s = z.shape[0]
    seg = s // N_SEG

    def body(u_ref, dy_ref, rowp, colp, br_ref, bi_ref, cr_ref, ci_ref, d_ref,
             du_ref, dbr_ref, dbi_ref, dcr_ref, dci_ref, dd_ref, da_ref, dk_ref,
             xr, xi, gr, gi, us, dys):
        ar, ai, _, _ = _s5_disc(rowp[0:1, :], rowp[1:2, :], rowp[2:3, :])
        br, bi = br_ref[...], bi_ref[...]
        kr, ki, btr, bti, ctr, cti, bmask, cmask = _s5_mats(colp[...], br, bi, cr_ref[...], ci_ref[...])
        _interleave(u_ref, us, seg)
        _interleave(dy_ref, dys, seg)
        u = us[...]
        d = d_ref[...]
        _s5_forward_states(u, btr, bti, ar, ai, xr, xi, seg)

        dy = dys[...]
        dd_ref[...] = jnp.sum(dy * u, axis=0, keepdims=True)
        du = d * dy
        dyb = dy.astype(BF16)
        dctr, dcti = [], []
        for q in range(N_SLAB):
            gr[q] = jnp.dot(dyb, ctr[:, _slab(q)], preferred_element_type=F32)
            gi[q] = -jnp.dot(dyb, cti[:, _slab(q)], preferred_element_type=F32)
            dctr.append(_dot_tn(dyb, xr[q]))
            dcti.append(-_dot_tn(dyb, xi[q]))
        selp = ((_iota((HALF_STATE, 64), 0) & 63) == _iota((HALF_STATE, 64), 1)).astype(F32)
        dcr_ref[...] = _dot_hi(jnp.where(cmask, jnp.concatenate(dctr, axis=1), 0.0), selp)
        dci_ref[...] = _dot_hi(jnp.where(cmask, jnp.concatenate(dcti, axis=1), 0.0), selp)

        da = _scan(gr, gi, ar, -ai, seg, True, states=(xr, xi))
        dar, dai = [p[0] for p in da], [p[1] for p in da]
        da_ref[...] = jnp.zeros_like(da_ref)
        da_ref[0:1, :] = jnp.concatenate(dar, axis=1)
        da_ref[1:2, :] = jnp.concatenate(dai, axis=1)

        ub = u.astype(BF16)
        dbtr, dbti = [], []
        for q in range(N_SLAB):
            g_r, g_i = gr[q].astype(BF16), gi[q].astype(BF16)
            du = du + jnp.dot(g_r, btr[_slab(q), :], preferred_element_type=F32) \
                + jnp.dot(g_i, bti[_slab(q), :], preferred_element_type=F32)
            dbtr.append(_dot_tn(g_r, ub))
            dbti.append(_dot_tn(g_i, ub))
        us[...] = du
        _deinterleave(us, du_ref, seg)
        selc =((_iota((HALF_CH, 16), 0) & 15) == _iota((HALF_CH, 16), 1)).astype(F32)
        dbbr = _dot_hi(jnp.where(bmask, jnp.concatenate(dbtr, axis=0), 0.0), selc)
        dbbi = _dot_hi(jnp.where(bmask, jnp.concatenate(dbti, axis=0), 0.0), selc)
        dbr_ref[...] = kr * dbbr + ki * dbbi
        dbi_ref[...] = kr * dbbi - ki * dbbr
        dk_ref[:, 0:1] = jnp.sum(dbbr * br + dbbi * bi, axis=1, keepdims=True)
        dk_ref[:, 1:2] = jnp.sum(dbbi * br - dbbr * bi, axis=1, keepdims=True)

    half = pl.BlockSpec((s, HALF_CH), lambda i: (0, i))
    return pl.pallas_call(
        body, name="s5_core_bwd", grid=(2,),
        in_specs=[pl.BlockSpec((s, HALF_CH), lambda i: (0, 12 + i)), half] + _s5_param_specs(),
        out_specs=[half, pl.BlockSpec((HALF_STATE, 16), lambda i: (i, 0)), pl.BlockSpec((HALF_STATE, 16), lambda i: (i, 0)),
                   pl.BlockSpec((HALF_CH, 64), lambda i: (i, 0)), pl.BlockSpec((HALF_CH, 64), lambda i: (i, 0)),
                   pl.BlockSpec((1, HALF_CH), lambda i: (0, i)), pl.BlockSpec((8, HALF_STATE), lambda i: (0, i)),
                   pl.BlockSpec((HALF_STATE, 2), lambda i: (i, 0))],
        out_shape=[jax.ShapeDtypeStruct((s, W_GRP), F32), jax.ShapeDtypeStruct((N_STATE, 16), F32),
                   jax.ShapeDtypeStruct((N_STATE, 16), F32), jax.ShapeDtypeStruct((W_GRP, 64), F32),
                   jax.ShapeDtypeStruct((W_GRP, 64), F32), jax.ShapeDtypeStruct((1, W_GRP), F32),
                   jax.ShapeDtypeStruct((8, N_STATE), F32), jax.ShapeDtypeStruct((N_STATE, 2), F32)],
        scratch_shapes=[pltpu.VMEM((N_SLAB, s, LANES), F32)] * 4 + [pltpu.VMEM((s, HALF_CH), F32)] * 2,
        compiler_params=_cp(dimension_semantics=("parallel",)),
    )(z, dy, *sp)


def _s5_param_bwd(lre, lim, ldt, da_r, da_i, dk_r, dk_i):
    n = lre.shape[0]

    def body(lre_ref, lim_ref, ldt_ref, dar_ref, dai_ref, dkr_ref, dki_ref, o_re, o_im, o_dt):
        lre, lim, ldt = lre_ref[...], lim_ref[...], ldt_ref[...]
        dt = jnp.exp(ldt)
        ar, ai, kr, ki = _s5_disc(lre, lim, ldt)
        mag = jnp.exp(lre * dt)
        den = lre * lre + lim * lim
        dkr, dki = dkr_ref[...], dki_ref[...]
        nr, ni = ar - 1.0, ai
        d_ar = dar_ref[...] + (dkr * lre - dki * lim) / den
        d_ai = dai_ref[...] + (dkr * lim + dki * lre) / den
        kk = (kr * dkr + ki * dki) * 2.0 / den
        d_lre = (dkr * nr + dki * ni) / den - kk * lre
        d_lim = (dkr * ni - dki * nr) / den - kk * lim
        d_mag = (d_ar * ar + d_ai * ai) / mag
        d_ang = d_ai * ar - d_ar * ai
        o_re[...] = d_lre + d_mag * mag * dt
        o_im[...] = d_lim + d_ang * dt
        o_dt[...] = jnp.sum((d_mag * mag * lre + d_ang * lim) * dt, axis=1, keepdims=True)

    return pl.pallas_call(
        body, name="s5_param_bwd",
        out_shape=[jax.ShapeDtypeStruct((n, 64), F32), jax.ShapeDtypeStruct((n, 64), F32),
                   jax.ShapeDtypeStruct((n, 1), F32)],
    )(lre, lim, ldt, da_r, da_i, dk_r, dk_i)


def _loss_head(x, fg, target):
    s, d = x.shape
    tm = _tm(s)

    def body(x_ref, fg_ref, t_ref, loss_ref, dx_ref, dfg_ref):
        i = pl.program_id(0)

        @pl.when(i == 0)
        def _():
            loss_ref[...] = jnp.zeros_like(loss_ref)
            dfg_ref[...] = jnp.zeros_like(dfg_ref)

        xv, g = x_ref[...], fg_ref[...]
        r = lax.rsqrt(jnp.mean(xv * xv, axis=-1, keepdims=True) + EPS)
        xh = xv * r
        err = xh * g - t_ref[...]
        loss_ref[...] += 0.5 * jnp.sum(jnp.mean(err * err, axis=-1, keepdims=True), axis=0, keepdims=True)
        dy = err * (1.0 / d)
        dfg_ref[...] += jnp.sum(dy * xh, axis=0, keepdims=True)
        dxh = dy * g
        dx_ref[...] = r * (dxh - xh * jnp.mean(dxh * xh, axis=-1, keepdims=True))

    row = pl.BlockSpec((tm, d), lambda i: (i, 0))
    return pl.pallas_call(
        body, name="loss_head", grid=(s // tm,),
        in_specs=[row, _full((1, d)), row], out_specs=[_full((1, 1)), row, _full((1, d))],
        out_shape=[jax.ShapeDtypeStruct((1, 1), F32), jax.ShapeDtypeStruct((s, d), F32),
                   jax.ShapeDtypeStruct((1, d), F32)],
        compiler_params=_cp(dimension_semantics=("arbitrary",)),
    )(x, fg, target)


ADA_TN = 384


def _cond_fwd(cact, ada_w, ada_b_loc):
    nl, d, n = ada_w.shape

    def body(c_ref, w_ref, b_ref, o_ref):
        o_ref[...] = _dot(c_ref[...], w_ref[...]) + b_ref[...]

    return pl.pallas_call(
        body, name="cond_fwd", grid=(nl, n // ADA_TN),
        in_specs=[_full((N_DEV, d)), pl.BlockSpec((None, d, ADA_TN), lambda l, j: (l, 0, j)),
                  pl.BlockSpec((None, 1, ADA_TN), lambda l, j: (l, 0, j))],
        out_specs=pl.BlockSpec((None, N_DEV, ADA_TN), lambda l, j: (l, 0, j)),
        out_shape=jax.ShapeDtypeStruct((nl, N_DEV, n), F32),
        compiler_params=_cp(dimension_semantics=("parallel", "parallel")),
    )(cact, ada_w, ada_b_loc)


ELEMENTWISE_BLOCK_BYTES = 1 << 20


def _row_tile(r, c, itemsize=4):
    best = None
    for t in range(8, r + 1, 8):
        if r % t == 0 and t * c * itemsize <= ELEMENTWISE_BLOCK_BYTES:
            best = t
    return best if best is not None else r


def _adamw_math(w, g, m, v):
    m = ADAM_B1 * m + (1.0 - ADAM_B1) * g
    v = ADAM_B2 * v + (1.0 - ADAM_B2) * (g * g)
    m_hat = m / (1.0 - ADAM_B1 ** ADAM_STEP)
    v_hat = v / (1.0 - ADAM_B2 ** ADAM_STEP)
    delta = -ADAM_LR * (m_hat / (jnp.sqrt(v_hat) + ADAM_EPS) + ADAM_WD * w)
    return delta, m, v


def _ada_w_update(cact, dcond_loc, w, m, v):
    nl, d, n = w.shape

    def body(c_ref, dc_ref, w_ref, m_ref, v_ref, g_out, d_out, m_out, v_out):
        g = _dot_tn(c_ref[...], dc_ref[...])
        g_out[...] = g
        d_out[...], m_out[...], v_out[...] = _adamw_math(w_ref[...], g, m_ref[...], v_ref[...])

    blk = pl.BlockSpec((None, d, ADA_TN), lambda l, j: (l, 0, j))
    return pl.pallas_call(
        body, name="ada_w_update", grid=(nl, n // ADA_TN),
        in_specs=[_full((N_DEV, d)), pl.BlockSpec((None, N_DEV, ADA_TN), lambda l, j: (l, 0, j)), blk, blk, blk],
        out_specs=[blk] * 4, out_shape=[jax.ShapeDtypeStruct((nl, d, n), F32)] * 4,
        compiler_params=_cp(dimension_semantics=("parallel", "parallel")),
    )(cact, dcond_loc, w, m, v)


def _place():
    x, y, c = lax.axis_index("x"), lax.axis_index("y"), lax.axis_index("c")
    chips = [(1 - x, y), (x, 1 - y), (1 - x, 1 - y)]
    return x, y, c, chips


def _remote(src, dst, send_sem, recv_sem, to):
    return pltpu.make_async_remote_copy(src_ref=src, dst_ref=dst, send_sem=send_sem, recv_sem=recv_sem,
                                        device_id=to, device_id_type=MESH_ID)


def _sems(n):
    return [pltpu.SemaphoreType.DMA((n,)), pltpu.SemaphoreType.DMA((n,))]


def _all_gather8(v, name):
    r, cdim = v.shape

    def body(x_ref, out_ref, stage, send_sems, recv_sems):
        x, y, c, chips = _place()
        sibling = (x, y, 1 - c)

        def slot(px, py, pc):
            return out_ref.at[4 * px + 2 * py + pc]

        first = [_remote(x_ref, slot(x, y, c), send_sems.at[0], recv_sems.at[0], sibling)]
        first += [_remote(x_ref, slot(x, y, c), send_sems.at[1 + j], recv_sems.at[1 + j], (*chip, c))
                  for j, chip in enumerate(chips)]
        for cp in first:
            cp.start()
        pltpu.sync_copy(x_ref, stage)
        pltpu.sync_copy(stage, slot(x, y, c))
        passed = []
        for j, chip in enumerate(chips):
            blk = slot(*chip, c)
            _remote(blk, blk, send_sems.at[1 + j], recv_sems.at[1 + j], (x, y, c)).wait_recv()
            fw = _remote(blk, blk, send_sems.at[4 + j], recv_sems.at[4 + j], sibling)
            fw.start()
            passed.append(fw)
        blk = slot(x, y, 1 - c)
        _remote(blk, blk, send_sems.at[0], recv_sems.at[0], (x, y, c)).wait_recv()
        for j, chip in enumerate(chips):
            blk = slot(*chip, 1 - c)
            _remote(blk, blk, send_sems.at[4 + j], recv_sems.at[4 + j], (x, y, c)).wait_recv()
        for cp in first + passed:
            cp.wait_send()

    return pl.pallas_call(
        body, name=name, out_shape=jax.ShapeDtypeStruct((N_DEV, r, cdim), v.dtype),
        in_specs=[ANY], out_specs=ANY,
        scratch_shapes=[pltpu.VMEM((r, cdim), v.dtype)] + _sems(7),
        compiler_params=_cp(),
    )(v)


def _gather_first_copies():
    def make(refs, send_sems, recv_sems):
        x, y, c, chips = _place()
        mine = refs[0].at[4 * x + 2 * y + c]
        to = [(x, y, 1 - c)] + [(*chip, c) for chip in chips]
        return [_remote(mine, mine, send_sems.at[k], recv_sems.at[k], dev) for k, dev in enumerate(to)]
    return make


def _gather_pass_on(buf, name):
    def body(in_ref, out_ref, send_sems, recv_sems):
        x, y, c, chips = _place()
        passed = []
        for j, chip in enumerate(chips):
            blk = out_ref.at[4 * chip[0] + 2 * chip[1] + c]
            fw = _remote(blk, blk, send_sems.at[j], recv_sems.at[j], (x, y, 1 - c))
            fw.start()
            passed.append(fw)
        for j, chip in enumerate(chips):
            blk = out_ref.at[4 * chip[0] + 2 * chip[1] + 1 - c]
            _remote(blk, blk, send_sems.at[j], recv_sems.at[j], (x, y, c)).wait_recv()
        for fw in passed:
            fw.wait_send()

    return pl.pallas_call(
        body, name=name, out_shape=jax.ShapeDtypeStruct(buf.shape, buf.dtype),
        in_specs=[ANY], out_specs=ANY, input_output_aliases={0: 0}, scratch_shapes=_sems(3),
    )(buf)


def _place_weights(ws, layer, kidx, after):
    steps = 4
    shapes, in_specs, out_specs = [], [], []
    for w, kind in zip(ws, BIG_KINDS):
        _, a, b = w.shape
        in_specs.append(pl.BlockSpec((None, a // steps, b), lambda i, k: (layer, i, 0)))
        if kind == "col":
            shapes.append((2, a, 2 * b))
            out_specs.append(pl.BlockSpec((None, a // steps, b), lambda i, k: (k[0] // 2, i, k[0] % 2)))
        else:
            shapes.append((N_CHIP, a, b))
            out_specs.append(pl.BlockSpec((None, a // steps, b), lambda i, k: (k[0], i, 0)))

    def body(k_ref, *refs):
        outs = refs[len(ws) + 1:]
        for t in range(len(ws)):
            outs[t][...] = refs[t][...].astype(BF16)

    return pl.pallas_call(
        body, name="place_weights", out_shape=[jax.ShapeDtypeStruct(s, BF16) for s in shapes],
        grid_spec=pltpu.PrefetchScalarGridSpec(num_scalar_prefetch=1, grid=(steps,), in_specs=in_specs + [ANY],
                                               out_specs=out_specs),
        compiler_params=_cp(dimension_semantics=("parallel",)),
    )(kidx, *ws, after)


HBM = pl.BlockSpec(memory_space=pltpu.HBM)
SEM = pl.BlockSpec(memory_space=pltpu.SEMAPHORE)
EFFECT = pltpu.SideEffectType.DATAFLOW_SIDE_EFFECTING


def _weight_block(ref, kind, k, h):
    if kind == "col":
        ncol = ref.shape[3] // 2
        return ref.at[k // 2, h, :, pl.ds(pl.multiple_of((k % 2) * ncol, LANES), ncol)]
    return ref.at[k, h]


def _in_hbm(a):
    return pltpu.with_memory_space_constraint(a, pltpu.HBM)


def _weight_send_start(placed, kinds, name):
    nt = len(placed)

    def body(*refs):
        send_sems, recv_sems = refs[nt], refs[nt + 1]
        dst = refs[nt + 2:2 * nt + 2]
        token = refs[2 * nt + 2]
        x, y, c, chips = _place()
        kme = 2 * x + y
        for t in range(nt):
            for j, chip in enumerate(chips):
                own = _weight_block(dst[t], kinds[t], kme, c)
                _remote(own, own, send_sems.at[3 * t + j], recv_sems.at[3 * t + j], (*chip, c)).start()
        token[...] = jnp.zeros_like(token)

    return pl.pallas_call(
        body, name=name,
        out_shape=(pltpu.SemaphoreType.DMA((3 * nt,)), pltpu.SemaphoreType.DMA((3 * nt,)),
                   *[pltpu.HBM(a.shape, a.dtype) for a in placed], jax.ShapeDtypeStruct((8, LANES), F32)),
        in_specs=[HBM] * nt, out_specs=(SEM, SEM, *[HBM] * nt, pl.BlockSpec(memory_space=pltpu.VMEM)),
        input_output_aliases={t: 2 + t for t in range(nt)},
        compiler_params=pltpu.CompilerParams(has_side_effects=EFFECT),
    )(*[_in_hbm(a) for a in placed])


def _weight_send_wait(send_sems, recv_sems, arrays, kinds, after, name):
    nt = len(arrays)

    def body(*refs):
        arr = refs[:nt]
        send_sems, recv_sems = refs[nt], refs[nt + 1]
        x, y, c, chips = _place()
        kme = 2 * x + y
        for t in range(nt):
            for j, chip in enumerate(chips):
                own = _weight_block(arr[t], kinds[t], kme, c)
                got = _weight_block(arr[t], kinds[t], 2 * chip[0] + chip[1], c)
                cp = _remote(own, got, send_sems.at[3 * t + j], recv_sems.at[3 * t + j], (*chip, c))
                cp.wait_send()
                cp.wait_recv()

    return pl.pallas_call(
        body, name=name, out_shape=[pltpu.HBM(a.shape, a.dtype) for a in arrays],
        in_specs=[HBM] * nt + [SEM, SEM, ANY], out_specs=[HBM] * nt,
        input_output_aliases={t: t for t in range(nt)},
        compiler_params=pltpu.CompilerParams(has_side_effects=EFFECT),
    )(*arrays, send_sems, recv_sems, after)


def _forward_copies(kinds):
    def make(refs, send_sems, recv_sems):
        x, y, c, chips = _place()
        cps = []
        for t in range(len(kinds)):
            for j, chip in enumerate(chips):
                blk = _weight_block(refs[t], kinds[t], 2 * chip[0] + chip[1], c)
                cps.append(_remote(blk, blk, send_sems.at[3 * t + j], recv_sems.at[3 * t + j], (x, y, 1 - c)))
        return cps
    return make


def _split_start(name, arrays, n_copies, make_copies):
    na = len(arrays)

    def body(*refs):
        send_sems, recv_sems = refs[na], refs[na + 1]
        for cp in make_copies(refs[na + 2:2 * na + 2], send_sems, recv_sems):
            cp.start()
        token = refs[2 * na + 2]
        token[...] = jnp.zeros_like(token)

    return pl.pallas_call(
        body, name=name,
        out_shape=(pltpu.SemaphoreType.DMA((n_copies,)), pltpu.SemaphoreType.DMA((n_copies,)),
                   *[pltpu.HBM(a.shape, a.dtype) for a in arrays], jax.ShapeDtypeStruct((8, LANES), F32)),
        in_specs=[HBM] * na, out_specs=(SEM, SEM, *[HBM] * na, pl.BlockSpec(memory_space=pltpu.VMEM)),
        input_output_aliases={t: 2 + t for t in range(na)},
        compiler_params=pltpu.CompilerParams(has_side_effects=EFFECT),
    )(*[_in_hbm(a) for a in arrays])


def _split_wait(name, started, make_copies, after):
    send_sems, recv_sems, *arrays, _ = started
    na = len(arrays)

    def body(*refs):
        send_sems, recv_sems = refs[na], refs[na + 1]
        for cp in make_copies(refs[:na], send_sems, recv_sems):
            cp.wait_send()
            cp.wait_recv()

    return pl.pallas_call(
        body, name=name, out_shape=[pltpu.HBM(a.shape, a.dtype) for a in arrays],
        in_specs=[HBM] * na + [SEM, SEM, ANY], out_specs=[HBM] * na,
        input_output_aliases={t: t for t in range(na)},
        compiler_params=pltpu.CompilerParams(has_side_effects=EFFECT),
    )(*arrays, send_sems, recv_sems, after)


def _exchange_copies(nt):
    def make(refs, send_sems, recv_sems):
        x, y, c, _ = _place()
        return [_remote(refs[t].at[:, 1 - c], refs[nt + t], send_sems.at[t], recv_sems.at[t], (x, y, 1 - c))
                for t in range(nt)]
    return make


def _sibling_exchange_start(views, name):
    lands = [lax.empty((v.shape[0],) + v.shape[2:], v.dtype) for v in views]
    return _split_start(name, list(views) + lands, len(views), _exchange_copies(len(views)))


def _sibling_exchange_wait(started, after, name):
    nt = (len(started) - 3) // 2
    outs = _split_wait(name, started, _exchange_copies(nt), after)
    return outs[:nt], outs[nt:]


def _scatter_copies(src, land, kinds, send_sems, recv_sems):
    x, y, c, chips = _place()
    cps = []
    for t in range(len(src)):
        for j, chip in enumerate(chips):
            k = 2 * chip[0] + chip[1]
            if kinds[t] == "col":
                ncol = land[t].shape[2]
                win = src[t].at[k // 2, :, pl.ds(pl.multiple_of((k % 2) * ncol, LANES), ncol)]
            else:
                win = src[t].at[k]
            cps.append(_remote(win, land[t].at[j], send_sems.at[3 * t + j], recv_sems.at[3 * t + j], (*chip, c)))
    return cps


def _chip_scatter_start(parts, kinds, name):
    nt = len(parts)
    shapes = []
    for p, kind in zip(parts, kinds):
        shapes.append((3, p.shape[1], p.shape[2] // 2) if kind == "col" else (3,) + p.shape[1:])

    def body(*refs):
        send_sems, recv_sems = refs[2 * nt], refs[2 * nt + 1]
        src, land = refs[2 * nt + 2:3 * nt + 2], refs[3 * nt + 2:4 * nt + 2]
        token = refs[4 * nt + 2]
        for cp in _scatter_copies(src, land, kinds, send_sems, recv_sems):
            cp.start()
        token[...] = jnp.zeros_like(token)

    lands = [lax.empty(s, BF16) for s in shapes]
    return pl.pallas_call(
        body, name=name,
        out_shape=(pltpu.SemaphoreType.DMA((3 * nt,)), pltpu.SemaphoreType.DMA((3 * nt,)),
                   *[pltpu.HBM(a.shape, a.dtype) for a in parts], *[pltpu.HBM(s, BF16) for s in shapes],
                   jax.ShapeDtypeStruct((8, LANES), F32)),
        in_specs=[HBM] * (2 * nt), out_specs=(SEM, SEM, *[HBM] * (2 * nt), pl.BlockSpec(memory_space=pltpu.VMEM)),
        input_output_aliases={t: 2 + t for t in range(2 * nt)},
        compiler_params=pltpu.CompilerParams(has_side_effects=EFFECT),
    )(*[_in_hbm(a) for a in parts], *[_in_hbm(a) for a in lands])


def _chip_scatter_wait(send_sems, recv_sems, parts, lands, kinds, after, name):
    nt = len(parts)

    def body(*refs):
        src, land = refs[:nt], refs[nt:2 * nt]
        send_sems, recv_sems = refs[2 * nt], refs[2 * nt + 1]
        for cp in _scatter_copies(src, land, kinds, send_sems, recv_sems):
            cp.wait_send()
            cp.wait_recv()

    outs = pl.pallas_call(
        body, name=name, out_shape=[pltpu.HBM(a.shape, a.dtype) for a in list(parts) + list(lands)],
        in_specs=[HBM] * (2 * nt) + [SEM, SEM, ANY], out_specs=[HBM] * (2 * nt),
        input_output_aliases={t: t for t in range(2 * nt)},
        compiler_params=pltpu.CompilerParams(has_side_effects=EFFECT),
    )(*parts, *lands, send_sems, recv_sems, after)
    return outs[:nt], outs[nt:]


def _share_copies(nt):
    def make(refs, send_sems, recv_sems):
        x, y, c, _ = _place()
        return [_remote(refs[t].at[c], refs[t].at[c], send_sems.at[t], recv_sems.at[t], (x, y, 1 - c))
                for t in range(nt)]
    return make


def _sibling_share_start(fulls, name):
    return _split_start(name, list(fulls), len(fulls), _share_copies(len(fulls)))


def _sibling_share_wait(started, after, name):
    return _split_wait(name, started, _share_copies(len(started) - 3), after)


SUM_STEPS = 4


def _pair_sum(views, lands, ck):
    nt = len(views)
    in_specs, out_specs, shapes = [], [], []
    for v in views:
        b, _, r, cc = v.shape
        per = SUM_STEPS // b
        tr = r // per
        in_specs.append(pl.BlockSpec((None, None, tr, cc), lambda i, s, per=per: (i // per, s[0], i % per, 0)))
        out_specs.append(pl.BlockSpec((None, tr, cc), lambda i, s, per=per: (i // per, i % per, 0)))
        shapes.append((b, r, cc))
    in_specs = in_specs + out_specs

    def body(s_ref, *refs):
        for t in range(nt):
            refs[2 * nt + t][...] = (refs[t][...].astype(F32) + refs[nt + t][...].astype(F32)).astype(BF16)

    return pl.pallas_call(
        body, name="grad_pair_sum", out_shape=[jax.ShapeDtypeStruct(s, BF16) for s in shapes],
        grid_spec=pltpu.PrefetchScalarGridSpec(num_scalar_prefetch=1, grid=(SUM_STEPS,), in_specs=in_specs,
                                               out_specs=out_specs),
        compiler_params=_cp(dimension_semantics=("parallel",)),
    )(ck, *views, *lands)


def _chip_sum(parts, lands, kinds, ck):
    nt = len(parts)
    steps = 2
    in_own, in_land, out_specs, shapes = [], [], [], []
    for ld, kind in zip(lands, kinds):
        _, r, cc = ld.shape
        tr = r // steps
        if kind == "col":
            in_own.append(pl.BlockSpec((None, tr, cc), lambda i, s: (s[1] // 2, i, s[1] % 2)))
        else:
            in_own.append(pl.BlockSpec((None, tr, cc), lambda i, s: (s[1], i, 0)))
        in_land.append(pl.BlockSpec((3, tr, cc), lambda i, s: (0, i, 0)))
        out_specs.append(pl.BlockSpec((None, tr, cc), lambda i, s: (s[0], i, 0)))
        shapes.append((2, r, cc))

    def body(s_ref, *refs):
        for t in range(nt):
            acc = refs[t][...].astype(F32)
            for j in range(3):
                acc = acc + refs[nt + t][j].astype(F32)
            refs[2 * nt + t][...] = acc

    return pl.pallas_call(
        body, name="grad_chip_sum", out_shape=[jax.ShapeDtypeStruct(s, F32) for s in shapes],
        grid_spec=pltpu.PrefetchScalarGridSpec(num_scalar_prefetch=1, grid=(steps,), in_specs=in_own + in_land,
                                               out_specs=out_specs),
        compiler_params=_cp(dimension_semantics=("parallel",)),
    )(ck, *parts, *lands)


def _sum8(g):
    _, r, cc = g.shape
    tr = _row_tile(r, N_DEV * cc)

    def body(g_ref, o_ref):
        acc = g_ref[0].astype(F32)
        for d in range(1, N_DEV):
            acc = acc + g_ref[d].astype(F32)
        o_ref[...] = acc

    return pl.pallas_call(
        body, name="small_grad_sum", grid=(r // tr,),
        in_specs=[pl.BlockSpec((N_DEV, tr, cc), lambda i: (0, i, 0))],
        out_specs=pl.BlockSpec((tr, cc), lambda i: (i, 0)),
        out_shape=jax.ShapeDtypeStruct((r, cc), F32),
        compiler_params=_cp(dimension_semantics=("parallel",)),
    )(g)


def _silu_rows(c):
    def body(c_ref, o_ref):
        v = c_ref[...]
        o_ref[...] = v * jax.nn.sigmoid(v)

    return pl.pallas_call(body, name="cond_silu", out_shape=jax.ShapeDtypeStruct(c.shape, F32))(c)


def _pack(arrays):
    rows = []
    for a in arrays:
        flat = a.reshape(-1)
        rows.append(jnp.pad(flat, (0, (-flat.shape[0]) % (8 * LANES))).reshape(-1, LANES))
    n = sum(r.shape[0] for r in rows)
    if n % 256:
        rows.append(jnp.zeros((256 - n % 256, LANES), rows[0].dtype))
    return jnp.concatenate(rows, axis=0)


def _unpack(packed, shapes):
    out, off = [], 0
    for s in shapes:
        n = math.prod(s)
        nr = 8 * -(-n // (8 * LANES))
        out.append(packed[off:off + nr].reshape(-1)[:n].reshape(s))
        off += nr
    return out


def _as_rows(a):
    return a.reshape(1, -1) if a.ndim == 1 else a.reshape(-1, a.shape[-1])


def _adamw_many(ws, gs, ms, vs, name, steps=1):
    nt = len(ws)

    def body(*refs):
        for t in range(nt):
            w_ref, g_ref, m_ref, v_ref = (refs[k * nt + t] for k in range(4))
            d, m, v = _adamw_math(w_ref[...], g_ref[...], m_ref[...], v_ref[...])
            refs[4 * nt + t][...] = d
            refs[5 * nt + t][...] = m
            refs[6 * nt + t][...] = v

    shapes = [jax.ShapeDtypeStruct(a.shape, F32) for a in ws]
    if steps == 1:
        outs = pl.pallas_call(body, name=name, out_shape=shapes * 3, compiler_params=_cp())(*ws, *gs, *ms, *vs)
    else:
        specs = [pl.BlockSpec((a.shape[0] // steps, a.shape[1]), lambda i: (i, 0)) for a in ws]
        outs = pl.pallas_call(
            body, name=name, grid=(steps,), in_specs=specs * 4, out_specs=specs * 3, out_shape=shapes * 3,
            compiler_params=_cp(dimension_semantics=("parallel",)),
        )(*ws, *gs, *ms, *vs)
    return outs[:nt], outs[nt:2 * nt], outs[2 * nt:]


def _exchange_big_grads(grads, kinds, layer):
    views = []
    for g, kind in zip(grads, kinds):
        if kind == "col":
            views.append(g.reshape(2, 2, g.shape[1] // 2, g.shape[2]))
        else:
            views.append(g.reshape(N_CHIP, 2, g.shape[0] // (2 * N_CHIP), g.shape[1]))
    return _sibling_exchange_start(views, "grad_exchange_start_%d" % layer)


def _scatter_big_grads(exchanged, kinds, ck, after, layer):
    views, lands = _sibling_exchange_wait(exchanged, after, "grad_exchange_wait_%d" % layer)
    parts = _pair_sum(views, lands, ck)
    return _chip_scatter_start(parts, kinds, "grad_scatter_start_%d" % layer)


def _finish_big_grads(started, kinds, ck, after, layer):
    nt = len(kinds)
    send_sems, recv_sems = started[0], started[1]
    parts, lands = started[2:2 + nt], started[2 + nt:2 + 2 * nt]
    parts, lands = _chip_scatter_wait(send_sems, recv_sems, parts, lands, kinds, after, "grad_scatter_wait_%d" % layer)
    return _sibling_share_start(_chip_sum(parts, lands, kinds, ck), "grad_share_start_%d" % layer)


def _adamw_layer(ws, gs, ms, vs, stacks, layer, name, steps):
    nt = len(ws)
    stacks = [s if s is not None else tuple(lax.empty(w.shape, F32) for _ in range(4)) for s, w in zip(stacks, ws)]

    def body(*refs):
        for t in range(nt):
            w_ref, g_ref, m_ref, v_ref = (refs[k * nt + t] for k in range(4))
            outs = refs[8 * nt + 4 * t:8 * nt + 4 * t + 4]
            g = g_ref[...]
            outs[0][...] = g
            outs[1][...], outs[2][...], outs[3][...] = _adamw_math(w_ref[...], g, m_ref[...], v_ref[...])

    in_specs, g_specs, out_specs = [], [], []
    for w in ws:
        _, r, c = w.shape
        in_specs.append(pl.BlockSpec((None, r // steps, c), lambda i: (layer, i, 0)))
        g_specs.append(pl.BlockSpec((r // steps, c), lambda i: (i, 0)))
        out_specs += [pl.BlockSpec((None, r // steps, c), lambda i: (layer, i, 0))] * 4
    in_specs = in_specs + g_specs + in_specs * 2 + [ANY] * (4 * nt)
    flat = [a for s in stacks for a in s]
    outs = pl.pallas_call(
        body, name=name, grid=(steps,), in_specs=in_specs, out_specs=out_specs,
        out_shape=[jax.ShapeDtypeStruct(a.shape, F32) for a in flat],
        input_output_aliases={4 * nt + k: k for k in range(4 * nt)},
        compiler_params=_cp(dimension_semantics=("parallel",)),
    )(*ws, *gs, *ms, *vs, *flat)
    return [tuple(outs[4 * t:4 * t + 4]) for t in range(nt)]


SMALL_NAMES = ["ada_b", "norm1_g", "norm2_g", "sgu_w", "sgu_b", "pool_w", "pool_scale", "conv_w", "s5_lambda_re",
               "s5_lambda_im", "s5_b_re", "s5_b_im", "s5_c_re", "s5_c_im", "s5_d", "s5_log_dt", "s5_glu_w", "s5_glu_b",
               "mix_norm_g", "norm3_g", "final_norm_g"]
BIG_NAMES = ["ffn1_w_in", "ffn1_w_out", "w_mix_in", "w_mix_out", "ffn2_w_in", "ffn2_w_out"]
BIG_KINDS = ["col", "row", "row", "row", "col", "row"]
WEIGHT_ORDER = ["ada_w", "ada_b", "norm1_g", "ffn1_w_in", "ffn1_w_out", "norm2_g", "w_mix_in", "sgu_w", "sgu_b", "pool_w",
                "pool_scale", "conv_w", "s5_lambda_re", "s5_lambda_im", "s5_b_re", "s5_b_im", "s5_c_re", "s5_c_im", "s5_d",
                "s5_log_dt", "s5_glu_w", "s5_glu_b", "mix_norm_g", "w_mix_out", "norm3_g", "ffn2_w_in", "ffn2_w_out",
                "final_norm_g"]


def _local_step(x, target, cond, fetch_weights, prefetch_weights, p, emit_grads):
    nl, d = DEPTH, x.shape[1]
    row = lambda a: a.reshape(1, -1)
    saved = []
    for l in range(nl):
        (wi1, wo1, wmit, wmo, wi2, wo2), tok = fetch_weights(l, x)
        cl = cond[l] + tok
        mod1, mod2, mod3 = cl[0:3], cl[3:6], cl[6:9]
        lre, lim = p["s5_lambda_re"][l].reshape(-1), p["s5_lambda_im"][l].reshape(-1)
        ldt = jnp.repeat(p["s5_log_dt"][l], 64)
        rowp = jnp.stack([lre, lim, ldt])
        sp = (rowp, rowp.T, p["s5_b_re"][l].reshape(N_STATE, 16), p["s5_b_im"][l].reshape(N_STATE, 16),
              p["s5_c_re"][l].reshape(W_GRP, 64), p["s5_c_im"][l].reshape(W_GRP, 64), row(p["s5_d"][l]))
        glu = (p["s5_glu_w"][l], row(p["s5_glu_b"][l]))
        bias_full = jnp.repeat(p["sgu_b"][l].T, 64, axis=1)
        pw2 = p["pool_w"][l].reshape(W_GRP, 64)
        x1, h1, a1, b1, o1 = _ffn_fwd(x, mod1, row(p["norm1_g"][l]), wi1, wo1)
        z, h2 = _mix_in_fwd(x1, mod2, row(p["norm2_g"][l]), wmit)
        ya = _sgu_fwd(z, p["sgu_w"][l], bias_full)
        yb, yc = _poolconv_fwd(z, pw2, row(p["pool_scale"][l]), p["conv_w"][l])
        ys = (ya, yb, yc, _s5_core_fwd(z, sp))
        x2, m = _mix_out_fwd(ys, glu, row(p["mix_norm_g"][l]), wmo, x1, mod2[2:3])
        mod3 = mod3 + prefetch_weights(l + 1, x2)
        x3, h3, a3, b3, o3 = _ffn_fwd(x2, mod3, row(p["norm3_g"][l]), wi2, wo2)
        saved.append((x, x1, x2, h1, a1, b1, o1, z, h2, ys, m, h3, a3, b3, o3, sp, bias_full, pw2, glu,
                      (wi1, wo1, wmit, wmo, wi2, wo2), cl))
        x = x3

    loss, dx, dfg = _loss_head(x, row(p["final_norm_g"]), target)

    sg = {n: [None] * nl for n in SMALL_NAMES if n not in ("ada_b", "final_norm_g")}
    dcond = [None] * nl
    s5_da, s5_dk = [None] * nl, [None] * nl
    tok = 0.0
    for l in reversed(range(nl)):
        (x0, x1, x2, h1, a1, b1, o1, z, h2, ys, m, h3, a3, b3, o3, sp, bias_full, pw2, glu,
         (wi1, wo1, wmit, wmo, wi2, wo2), cl) = saved[l]
        cl = cl + tok
        mod1, mod2, mod3 = cl[0:3], cl[3:6], cl[6:9]
        dza, dzb, dwi2, dwo2, dgate3 = _ffn_bwd_main(dx, o3, mod3[2:3], h3, a3, b3, wo2)
        dx, rows3 = _ffn_bwd_in(dza, dzb, wi2, x2, dx, mod3, row(p["norm3_g"][l]))
        outs = _mix_out_bwd(dx, m, mod2[2:3], ys, glu, row(p["mix_norm_g"][l]), wmo)
        dys, dgate2, dmng, dwmo, dgw, dgb = outs[0:4], outs[4], outs[5], outs[6], outs[7], outs[8]
        dza_, dsw, dsb = _sgu_bwd(z, dys[0], p["sgu_w"][l], bias_full)
        dzb_, dzc_, dwbd, dps, dcw = _poolconv_bwd(z, dys[1], dys[2], pw2, row(p["pool_scale"][l]), p["conv_w"][l])
        dzd_, dbr, dbi, dcr, dci, dd, da, dk = _s5_core_bwd(z, dys[3], sp)
        dx, rows2, dwmit = _mix_in_bwd((dza_, dzb_, dzc_, dzd_), h2, wmit, x1, dx, mod2, row(p["norm2_g"][l]))
        dza, dzb, dwi1, dwo1, dgate1 = _ffn_bwd_main(dx, o1, mod1[2:3], h1, a1, b1, wo1)
        tok, layer_done = emit_grads(l, [dwi1, dwo1, dwmit, dwmo, dwi2, dwo2])
        dx, rows1 = _ffn_bwd_in(dza, dzb, wi1, x0, dx, mod1 + tok, row(p["norm1_g"][l]))
        if l > 0:
            tok = layer_done(dx)[0, 0]
        dcond[l] = jnp.concatenate([rows1[0:2], dgate1, rows2[0:2], dgate2, rows3[0:2], dgate3], axis=0)
        sg["norm1_g"][l], sg["norm2_g"][l], sg["norm3_g"][l] = rows1[2], rows2[2], rows3[2]
        sg["mix_norm_g"][l] = dmng[0]
        sg["sgu_w"][l] = dsw
        sg["sgu_b"][l] = dsb[:, 0:4].T
        g4 = dwbd.reshape(4, 64, 4, 64)
        sg["pool_w"][l] = jnp.stack([g4[k, :, k, :] for k in range(4)])
        sg["pool_scale"][l] = dps[0]
        sg["conv_w"][l] = dcw[0:3]
        sg["s5_b_re"][l], sg["s5_b_im"][l] = dbr.reshape(16, 64, 16), dbi.reshape(16, 64, 16)
        sg["s5_c_re"][l], sg["s5_c_im"][l] = dcr.reshape(16, 16, 64), dci.reshape(16, 16, 64)
        sg["s5_d"][l] = dd[0]
        sg["s5_glu_w"][l], sg["s5_glu_b"][l] = dgw, dgb[0]
        s5_da[l], s5_dk[l] = da, dk

    n16 = nl * 16
    dlre, dlim, dldt = _s5_param_bwd(
        p["s5_lambda_re"].reshape(n16, 64), p["s5_lambda_im"].reshape(n16, 64),
        jnp.repeat(p["s5_log_dt"].reshape(n16, 1), 64, axis=1),
        jnp.stack([a[0] for a in s5_da]).reshape(n16, 64), jnp.stack([a[1] for a in s5_da]).reshape(n16, 64),
        jnp.stack([k[:, 0] for k in s5_dk]).reshape(n16, 64), jnp.stack([k[:, 1] for k in s5_dk]).reshape(n16, 64))
    small = {n: jnp.stack(v) for n, v in sg.items() if v[0] is not None}
    small["s5_lambda_re"] = dlre.reshape(nl, 16, 64)
    small["s5_lambda_im"] = dlim.reshape(nl, 16, 64)
    small["s5_log_dt"] = dldt.reshape(nl, 16)
    small["final_norm_g"] = dfg[0]
    return loss, dx, small, jnp.stack(dcond), layer_done


def kernel(x, c, ada_w, ada_b, norm1_g, ffn1_w_in, ffn1_w_out, norm2_g, w_mix_in, sgu_w, sgu_b, pool_w, pool_scale, conv_w, s5_lambda_re, s5_lambda_im, s5_b_re, s5_b_im, s5_c_re, s5_c_im, s5_d, s5_log_dt, s5_glu_w, s5_glu_b, mix_norm_g, w_mix_out, norm3_g, ffn2_w_in, ffn2_w_out, final_norm_g, loss_target, m_ada_w, m_ada_b, m_norm1_g, m_ffn1_w_in, m_ffn1_w_out, m_norm2_g, m_w_mix_in, m_sgu_w, m_sgu_b, m_pool_w, m_pool_scale, m_conv_w, m_s5_lambda_re, m_s5_lambda_im, m_s5_b_re, m_s5_b_im, m_s5_c_re, m_s5_c_im, m_s5_d, m_s5_log_dt, m_s5_glu_w, m_s5_glu_b, m_mix_norm_g, m_w_mix_out, m_norm3_g, m_ffn2_w_in, m_ffn2_w_out, m_final_norm_g, v_ada_w, v_ada_b, v_norm1_g, v_ffn1_w_in, v_ffn1_w_out, v_norm2_g, v_w_mix_in, v_sgu_w, v_sgu_b, v_pool_w, v_pool_scale, v_conv_w, v_s5_lambda_re, v_s5_lambda_im, v_s5_b_re, v_s5_b_im, v_s5_c_re, v_s5_c_im, v_s5_d, v_s5_log_dt, v_s5_glu_w, v_s5_glu_b, v_mix_norm_g, v_w_mix_out, v_norm3_g, v_ffn2_w_in, v_ffn2_w_out, v_final_norm_g):
    args = dict(locals())
    w = {n: args[n] for n in WEIGHT_ORDER}
    mom = {n: args["m_" + n] for n in WEIGHT_ORDER}
    vel = {n: args["v_" + n] for n in WEIGHT_ORDER}
    nl, d = DEPTH, x.shape[-1]
    s = x.shape[1]
    px, py, pc = lax.axis_index("x"), lax.axis_index("y"), lax.axis_index("c")
    kme = 2 * px + py
    me = 2 * kme + pc
    kidx = jnp.reshape(kme, (1,)).astype(jnp.int32)

    shards = [ffn1_w_in, ffn1_w_out, jnp.swapaxes(w_mix_in, 1, 2), w_mix_out, ffn2_w_in, ffn2_w_out]
    started_weights = {}

    def start_weights(l, after):
        placed = _place_weights(shards, l, kidx, after)
        views = [a.reshape(a.shape[0], 2, a.shape[1] // 2, a.shape[2]) for a in placed]
        *handles, token = _weight_send_start(views, BIG_KINDS, "weight_send_start_%d" % l)
        started_weights[l] = handles
        return token

    cact = _silu_rows(c)
    pre = _pack([cact, conv_w, s5_glu_w])
    pre_all = _all_gather8(pre, "gather_prelude")
    token = start_weights(0, pre_all)
    parts = [_unpack(pre_all[dev], [cact.shape, conv_w.shape, s5_glu_w.shape]) for dev in range(N_DEV)]
    cact_all = pre_all[:, :d // LANES, :].reshape(N_DEV, d)
    conv_full = jnp.concatenate([parts[2 * k][1] for k in range(N_CHIP)], axis=2)
    glu_full = jnp.concatenate([parts[2 * k][2] for k in range(N_CHIP)], axis=1)

    n_ada = ada_w.shape[2]
    ada_b_loc = lax.dynamic_slice_in_dim(ada_b, kme * n_ada, n_ada, axis=1).reshape(nl, 1, n_ada) + token[0, 0]
    cond_part = _cond_fwd(cact_all, ada_w, ada_b_loc)
    cond_mine = lax.dynamic_update_slice(lax.empty((N_DEV, nl * N_DEV, n_ada), F32),
                                         cond_part.reshape(1, nl * N_DEV, n_ada), (me, 0, 0))
    cond_gathering = _split_start("cond_send_start", [cond_mine], 4, _gather_first_copies())
    token = cond_gathering[-1]
    for l in range(1, nl):
        token = start_weights(l, token)
    cond_arrived, = _split_wait("cond_send_wait", cond_gathering, _gather_first_copies(), token)
    cond_all = _gather_pass_on(cond_arrived, "cond_pass_on").reshape(N_DEV, nl, N_DEV, n_ada)
    cond_me = jnp.concatenate(
        [lax.dynamic_index_in_dim(cond_all[2 * k], me, axis=1, keepdims=False) for k in range(N_CHIP)], axis=1)
    cond = cond_me.reshape(nl, 9, d)

    forwarding = {}

    def prefetch_weights(l, after):
        if l >= nl:
            return 0.0
        send_sems, recv_sems, *views = started_weights.pop(l)
        views = _weight_send_wait(send_sems, recv_sems, views, BIG_KINDS, after, "weight_send_wait_%d" % l)
        forwarding[l] = _split_start("weight_forward_start_%d" % l, views, 3 * len(views), _forward_copies(BIG_KINDS))
        return forwarding[l][-1][0, 0]

    def fetch_weights(l, after):
        if l not in forwarding:
            prefetch_weights(l, after)
        views = _split_wait("weight_forward_wait_%d" % l, forwarding.pop(l), _forward_copies(BIG_KINDS), after)
        full = [v.reshape(2, 2 * v.shape[2], v.shape[3]) if kind == "col" else v.reshape(-1, v.shape[3])
                for v, kind in zip(views, BIG_KINDS)]
        return full, 0.0

    ck = jnp.stack([pc, kme]).astype(jnp.int32)
    scattering, sharing = [], []
    stacks = {n: None for n in BIG_NAMES}
    groups = ((["ffn1_w_in", "ffn2_w_in"], 16, "adamw_w_in"),
              (["ffn1_w_out", "w_mix_in", "w_mix_out", "ffn2_w_out"], 8, "adamw_w_out"))

    def as_reduced(t):
        return {n: jnp.swapaxes(t[n], 1, 2) if n == "w_mix_in" else t[n] for n in BIG_NAMES}

    w_r, m_r, v_r = as_reduced(w), as_reduced(mom), as_reduced(vel)

    def apply_adamw(l, fulls):
        g = {n: f.reshape(2 * f.shape[1], f.shape[2]) for n, f in zip(BIG_NAMES, fulls)}
        for names, steps, call in groups:
            outs = _adamw_layer([w_r[n] for n in names], [g[n] for n in names], [m_r[n] for n in names],
                                [v_r[n] for n in names], [stacks[n] for n in names], l, call, steps)
            stacks.update(zip(names, outs))

    def retire_share(after):
        l2, shared = sharing.pop(0)
        apply_adamw(l2, _sibling_share_wait(shared, after, "grad_share_wait_%d" % l2))

    def retire_scatter(after):
        l1, scattered = scattering.pop(0)
        sharing.append((l1, _finish_big_grads(scattered, BIG_KINDS, ck, after, l1)))

    def retire(after):
        if sharing:
            retire_share(after)
        if scattering:
            retire_scatter(after)

    def emit_grads(l, grads_l):
        exchanged = _exchange_big_grads(grads_l, BIG_KINDS, l)

        def layer_done(after):
            started = _scatter_big_grads(exchanged, BIG_KINDS, ck, after, l)
            retire(after)
            scattering.append((l, started))
            return started[-1]

        return exchanged[-1][0, 0], layer_done

    p = {n: w[n] for n in SMALL_NAMES}
    p["conv_w"], p["s5_glu_w"] = conv_full, glu_full
    loss, dx, small, dcond, first_layer_done = _local_step(x[0], loss_target[0], cond, fetch_weights, prefetch_weights,
                                                           p, emit_grads)

    small_order = [n for n in SMALL_NAMES if n != "ada_b"]
    packed = _pack([dcond] + [small[n] for n in small_order]).astype(BF16)
    mine = lax.dynamic_update_slice(lax.empty((N_DEV,) + packed.shape, BF16), packed[None], (me, 0, 0))
    gathering = _split_start("small_grads_send_start", [mine], 4, _gather_first_copies())
    scatter_token = first_layer_done(gathering[-1])
    while sharing:
        retire_share(scatter_token)
    arrived, = _split_wait("small_grads_send_wait", gathering, _gather_first_copies(), stacks[BIG_NAMES[0]][0])
    gathered_small = _gather_pass_on(arrived, "small_grads_pass_on")
    total = _sum8(gathered_small)
    shapes = [dcond.shape] + [small[n].shape for n in small_order]
    tot = dict(zip(["ada_b"] + small_order, _unpack(total, shapes)))
    grads = {n: tot[n] for n in SMALL_NAMES}
    grads["ada_b"] = tot["ada_b"].reshape(nl, 9 * d)
    grads["conv_w"] = lax.dynamic_slice_in_dim(tot["conv_w"], kme * conv_w.shape[2], conv_w.shape[2], axis=2)
    grads["s5_glu_w"] = lax.dynamic_slice_in_dim(tot["s5_glu_w"], kme * s5_glu_w.shape[1], s5_glu_w.shape[1], axis=1)

    dcond_all = gathered_small.reshape(N_DEV, -1)[:, :dcond.size].reshape(N_DEV, nl, 9 * d)
    dcond_loc = jnp.swapaxes(lax.dynamic_slice_in_dim(dcond_all, kme * n_ada, n_ada, axis=2), 0, 1)
    g_ada, d_ada, m_ada, v_ada = _ada_w_update(cact_all, dcond_loc, ada_w, m_ada_w, v_ada_w)

    while scattering or sharing:
        retire(g_ada)
    delta, new_m, new_v = {}, {}, {}
    for n in BIG_NAMES:
        grads[n], delta[n], new_m[n], new_v[n] = (jnp.swapaxes(a, 1, 2) if n == "w_mix_in" else a for a in stacks[n])

    grads["ada_w"], delta["ada_w"], new_m["ada_w"], new_v["ada_w"] = g_ada, d_ada, m_ada, v_ada
    wide = ("s5_b_re", "s5_b_im")
    for names, call, steps in (([n for n in SMALL_NAMES if n not in wide], "adamw_small", 1),
                               (list(wide), "adamw_s5_b", DEPTH)):
        outs = _adamw_many(*[[_as_rows(t[n]) for n in names] for t in (w, grads, mom, vel)], call, steps)
        for res, o in zip((delta, new_m, new_v), outs):
            res.update({n: a.reshape(w[n].shape) for n, a in zip(names, o)})

    loss_total = lax.psum(loss[0, 0], ("x", "y", "c"))
    return (loss_total, dx[None], *[grads[n] for n in WEIGHT_ORDER], *[delta[n] for n in WEIGHT_ORDER],
            *[new_m[n] for n in WEIGHT_ORDER], *[new_v[n] for n in WEIGHT_ORDER])
```

```python
import math

import jax
import jax.numpy as jnp
from jax import lax
from jax.experimental import pallas as pl
from jax.experimental.pallas import tpu as pltpu

F32, BF16 = jnp.float32, jnp.bfloat16
EPS = 1e-6
DEPTH = 4
N_DEV = 8
N_CHIP = 4
W_GRP = 256
CHUNK = 128
N_SEG = 8
LANES = 128
FFN_TF = 256
FFN_TF_WIDE = 1408
FFN_TM_WIDE = 512
VMEM_LIMIT = 56 * 1024 * 1024
ADAM_LR, ADAM_B1, ADAM_B2, ADAM_EPS, ADAM_WD, ADAM_STEP = 0.001, 0.9, 0.999, 1e-08, 0.01, 10
MESH_ID = pl.DeviceIdType.MESH
HI = lax.Precision.HIGHEST
ANY = pl.BlockSpec(memory_space=pl.ANY)


def _cp(**kw):
    return pltpu.CompilerParams(vmem_limit_bytes=VMEM_LIMIT, **kw)


def _dot(a, b):
    return jnp.dot(a.astype(BF16), b.astype(BF16), preferred_element_type=F32)


def _dot_nt(a, b):
    return lax.dot_general(a.astype(BF16), b.astype(BF16), (((1,), (1,)), ((), ())), preferred_element_type=F32)


def _dot_tn(a, b):
    return lax.dot_general(a.astype(BF16), b.astype(BF16), (((0,), (0,)), ((), ())), preferred_element_type=F32)


def _dot_hi(a, b):
    return jnp.dot(a, b, preferred_element_type=F32, precision=HI)


def _gelu(x):
    k = 0.7978845608028654
    t = jnp.tanh(k * (x + 0.044715 * x * x * x))
    return 0.5 * x * (1.0 + t), t


def _gelu_grad(x, t):
    k = 0.7978845608028654
    return 0.5 * (1.0 + t) + 0.5 * x * (1.0 - t * t) * k * (1.0 + 3.0 * 0.044715 * x * x)


def _iota(shape, axis):
    return lax.broadcasted_iota(jnp.int32, shape, axis)


def _full(shape):
    nd = len(shape)
    return pl.BlockSpec(shape, lambda *_: (0,) * nd)


def _norm_mod(xv, g, shift, scale):
    r = lax.rsqrt(jnp.mean(xv * xv, axis=-1, keepdims=True) + EPS)
    return (xv * r * g) * (1.0 + scale) + shift


def _norm_mod_bwd(xv, g, scale, dh):
    r = lax.rsqrt(jnp.mean(xv * xv, axis=-1, keepdims=True) + EPS)
    xh = xv * r
    n = xh * g
    dsh = jnp.sum(dh, axis=0, keepdims=True)
    dsc = jnp.sum(dh * n, axis=0, keepdims=True)
    dn = dh * (1.0 + scale)
    dg = jnp.sum(dn * xh, axis=0, keepdims=True)
    dxh = dn * g
    dx = r * (dxh - xh * jnp.mean(dxh * xh, axis=-1, keepdims=True))
    return dx, dsh, dsc, dg


def _tm(s):
    return min(s, 1024)


def _ffn_fwd(x, mod, g, wi, wo):
    s, d = x.shape
    f = wo.shape[0]
    tf, tm = FFN_TF_WIDE, min(s, FFN_TM_WIDE)
    nf, nt = f // tf, s // tm

    def body(x_ref, mod_ref, g_ref, wa_ref, wb_ref, wo_ref, xn_ref, h_ref, a_ref, b_ref, o_ref, acc):
        j = pl.program_id(1)

        @pl.when(j == 0)
        def _():
            hh = _norm_mod(x_ref[...], g_ref[...], mod_ref[0:1, :], mod_ref[1:2, :])
            h_ref[...] = hh.astype(BF16)
            acc[...] = jnp.zeros_like(acc)

        h = h_ref[...]
        a = jnp.dot(h, wa_ref[...], preferred_element_type=F32)
        b = jnp.dot(h, wb_ref[...], preferred_element_type=F32)
        a_ref[...] = a.astype(BF16)
        b_ref[...] = b.astype(BF16)
        u = (a * jax.nn.sigmoid(a)) * b
        acc[...] += jnp.dot(u.astype(BF16), wo_ref[...], preferred_element_type=F32)

        @pl.when(j == nf - 1)
        def _():
            o = acc[...]
            o_ref[...] = o.astype(BF16)
            xn_ref[...] = x_ref[...] + 0.5 * mod_ref[2:3, :] * o

    row = pl.BlockSpec((tm, d), lambda i, j: (i, 0))
    chunk = pl.BlockSpec((tm, tf), lambda i, j: (i, j))
    return pl.pallas_call(
        body, name="ffn_fwd", grid=(nt, nf),
        in_specs=[row, _full((3, d)), _full((1, d)),
                  pl.BlockSpec((None, d, tf), lambda i, j: (0, 0, j)),
                  pl.BlockSpec((None, d, tf), lambda i, j: (1, 0, j)),
                  pl.BlockSpec((tf, d), lambda i, j: (j, 0))],
        out_specs=[row, row, chunk, chunk, row],
        out_shape=[jax.ShapeDtypeStruct((s, d), F32), jax.ShapeDtypeStruct((s, d), BF16),
                   jax.ShapeDtypeStruct((s, f), BF16), jax.ShapeDtypeStruct((s, f), BF16),
                   jax.ShapeDtypeStruct((s, d), BF16)],
        scratch_shapes=[pltpu.VMEM((tm, d), F32)],
        compiler_params=_cp(dimension_semantics=("parallel", "arbitrary")),
    )(x, mod, g, wi, wi, wo)


def _ffn_bwd_main(dxo, o, gate, h, a, b, wo):
    s, d = dxo.shape
    f = wo.shape[0]
    tf = FFN_TF
    nf = f // tf

    def body(dxo_ref, o_ref, gate_ref, h_ref, a_ref, b_ref, wo_ref, dza_ref, dzb_ref, dwi_ref, dwo_ref, dg_ref, do_s):
        @pl.when(pl.program_id(0) == 0)
        def _():
            dxv = dxo_ref[...]
            do_s[...] = (0.5 * gate_ref[...] * dxv).astype(BF16)
            dg_ref[...] = 0.5 * jnp.sum(o_ref[...].astype(F32) * dxv, axis=0, keepdims=True)

        dov = do_s[...]
        hv = h_ref[...]
        du = lax.dot_general(dov, wo_ref[...], (((1,), (1,)), ((), ())), preferred_element_type=F32)
        av = a_ref[...].astype(F32)
        bv = b_ref[...].astype(F32)
        sa = jax.nn.sigmoid(av)
        si = av * sa
        u = (si * bv).astype(BF16)
        da = (du * bv * (sa * (1.0 + av * (1.0 - sa)))).astype(BF16)
        db = (du * si).astype(BF16)
        dza_ref[...] = da
        dzb_ref[...] = db
        dwo_ref[...] = _dot_tn(u, dov).astype(BF16)
        dwi_ref[0] = _dot_tn(hv, da).astype(BF16)
        dwi_ref[1] = _dot_tn(hv, db).astype(BF16)

    chunk = pl.BlockSpec((s, tf), lambda j: (0, j))
    once = lambda: pl.BlockSpec((s, d), lambda j: (0, 0), pipeline_mode=pl.Buffered(1))
    return pl.pallas_call(
        body, name="ffn_bwd_main", grid=(nf,),
        in_specs=[once(), once(), _full((1, d)), once(), chunk, chunk, pl.BlockSpec((tf, d), lambda j: (j, 0))],
        out_specs=[chunk, chunk, pl.BlockSpec((2, d, tf), lambda j: (0, 0, j)),
                   pl.BlockSpec((tf, d), lambda j: (j, 0)), _full((1, d))],
        out_shape=[jax.ShapeDtypeStruct((s, f), BF16), jax.ShapeDtypeStruct((s, f), BF16),
                   jax.ShapeDtypeStruct((2, d, f), BF16), jax.ShapeDtypeStruct((f, d), BF16),
                   jax.ShapeDtypeStruct((1, d), F32)],
        scratch_shapes=[pltpu.VMEM((s, d), BF16)],
        compiler_params=_cp(dimension_semantics=("arbitrary",)),
    )(dxo, o, gate, h, a, b, wo)


def _ffn_bwd_in(dza, dzb, wi, x, dxo, mod, g):
    s, d = x.shape
    f = dza.shape[1]
    tf, tm = FFN_TF_WIDE, min(s, FFN_TM_WIDE)
    nf, nt = f // tf, s // tm

    def body(dza_ref, dzb_ref, wa_ref, wb_ref, x_ref, dxo_ref, mod_ref, g_ref, dx_ref, rows_ref, acc):
        j, i = pl.program_id(0), pl.program_id(1)
        rows = pl.ds(pl.multiple_of(i * tm, tm), tm)

        @pl.when(jnp.logical_and(i == 0, j == 0))
        def _():
            rows_ref[...] = jnp.zeros_like(rows_ref)

        part = (lax.dot_general(dza_ref[...], wa_ref[...], (((1,), (1,)), ((), ())), preferred_element_type=F32)
                + lax.dot_general(dzb_ref[...], wb_ref[...], (((1,), (1,)), ((), ())), preferred_element_type=F32))

        @pl.when(j == 0)
        def _():
            acc[rows, :] = part

        @pl.when(jnp.logical_and(j > 0, j < nf - 1))
        def _():
            acc[rows, :] += part

        @pl.when(j == nf - 1)
        def _():
            dh = part + acc[rows, :] if nf > 1 else part
            dx, dsh, dsc, dg = _norm_mod_bwd(x_ref[...], g_ref[...], mod_ref[1:2, :], dh)
            dx_ref[...] = dx + dxo_ref[...]
            rows_ref[0:1, :] += dsh
            rows_ref[1:2, :] += dsc
            rows_ref[2:3, :] += dg

    late = pl.BlockSpec((tm, d), lambda j, i: (jnp.where(j == nf - 1, i, 0), 0))
    chunk = pl.BlockSpec((tm, tf), lambda j, i: (i, j))
    return pl.pallas_call(
        body, name="ffn_bwd_in", grid=(nf, nt),
        in_specs=[chunk, chunk,
                  pl.BlockSpec((None, d, tf), lambda j, i: (0, 0, j)),
                  pl.BlockSpec((None, d, tf), lambda j, i: (1, 0, j)),
                  late, late, _full((3, d)), _full((1, d))],
        out_specs=[late, _full((8, d))],
        out_shape=[jax.ShapeDtypeStruct((s, d), F32), jax.ShapeDtypeStruct((8, d), F32)],
        scratch_shapes=[pltpu.VMEM((s, d), F32)],
        compiler_params=_cp(dimension_semantics=("arbitrary", "arbitrary")),
    )(dza, dzb, wi, wi, x, dxo, mod, g)


def _mix_in_fwd(x, mod, g, wmit):
    s, d = x.shape
    p = wmit.shape[0]
    tm = _tm(s)

    def body(x_ref, mod_ref, g_ref, w_ref, z_ref, h_ref):
        hh = _norm_mod(x_ref[...], g_ref[...], mod_ref[0:1, :], mod_ref[1:2, :]).astype(BF16)
        h_ref[...] = hh
        z_ref[...] = lax.dot_general(hh, w_ref[...], (((1,), (1,)), ((), ())), preferred_element_type=F32)

    row = pl.BlockSpec((tm, d), lambda i: (i, 0))
    return pl.pallas_call(
        body, name="mix_in_fwd", grid=(s // tm,),
        in_specs=[row, _full((3, d)), _full((1, d)), _full((p, d))],
        out_specs=[pl.BlockSpec((tm, p), lambda i: (i, 0)), row],
        out_shape=[jax.ShapeDtypeStruct((s, p), F32), jax.ShapeDtypeStruct((s, d), BF16)],
        compiler_params=_cp(dimension_semantics=("parallel",)),
    )(x, mod, g, wmit)


def _mix_in_bwd(dzs, h, wmit, x, dxo, mod, g):
    s, d = x.shape
    p = wmit.shape[0]
    tm = min(s, 512)
    nt = s // tm

    def body(za_ref, zb_ref, zc_ref, zd_ref, h_ref, w_ref, x_ref, dxo_ref, mod_ref, g_ref,
             dx_ref, rows_ref, dw_ref, acc):
        i = pl.program_id(0)

        @pl.when(i == 0)
        def _():
            rows_ref[...] = jnp.zeros_like(rows_ref)
            acc[...] = jnp.zeros_like(acc)

        dz = jnp.concatenate([za_ref[...], zb_ref[...], zc_ref[...], zd_ref[...]], axis=1).astype(BF16)
        acc[...] += _dot_tn(dz, h_ref[...])
        dh = jnp.dot(dz, w_ref[...], preferred_element_type=F32)
        dx, dsh, dsc, dg = _norm_mod_bwd(x_ref[...], g_ref[...], mod_ref[1:2, :], dh)
        dx_ref[...] = dx + dxo_ref[...]
        rows_ref[0:1, :] += dsh
        rows_ref[1:2, :] += dsc
        rows_ref[2:3, :] += dg

        @pl.when(i == nt - 1)
        def _():
            dw_ref[...] = acc[...].astype(BF16)

    row = pl.BlockSpec((tm, d), lambda i: (i, 0))
    zspecs = [pl.BlockSpec((tm, z.shape[1]), lambda i: (i, 0)) for z in dzs]
    return pl.pallas_call(
        body, name="mix_in_bwd", grid=(nt,),
        in_specs=zspecs + [row, _full((p, d)), row, row, _full((3, d)), _full((1, d))],
        out_specs=[row, _full((8, d)), _full((p, d))],
        out_shape=[jax.ShapeDtypeStruct((s, d), F32), jax.ShapeDtypeStruct((8, d), F32),
                   jax.ShapeDtypeStruct((p, d), BF16)],
        scratch_shapes=[pltpu.VMEM((p, d), F32)],
        compiler_params=_cp(dimension_semantics=("arbitrary",)),
    )(*dzs, h, wmit, x, dxo, mod, g)


def _group_norm(ys, mng):
    outs, hats, rs = [], [], []
    for k, y in enumerate(ys):
        r = lax.rsqrt(jnp.mean(y * y, axis=-1, keepdims=True) + EPS)
        yh = y * r
        hats.append(yh)
        rs.append(r)
        outs.append(yh * mng[:, k * W_GRP:(k + 1) * W_GRP])
    return jnp.concatenate(outs, axis=1), hats, rs


def _s5_glu(y, gw, gb):
    yg, t = _gelu(y)
    gate = jax.nn.sigmoid(_dot(yg, gw) + gb)
    return yg * gate, yg, t, gate


def _mix_out_fwd(ys, glu, mng, wmo, x, gate):
    s, d = x.shape
    tm = _tm(s)

    def body(ya, yb, yc, ypre, gw_ref, gb_ref, mng_ref, w_ref, x_ref, gate_ref, xn_ref, m_ref):
        yd = _s5_glu(ypre[...], gw_ref[...], gb_ref[...])[0]
        yn, _, _ = _group_norm([ya[...], yb[...], yc[...], yd], mng_ref[...])
        m = jnp.dot(yn.astype(BF16), w_ref[...], preferred_element_type=F32)
        m_ref[...] = m
        xn_ref[...] = x_ref[...] + gate_ref[...] * m

    row = pl.BlockSpec((tm, d), lambda i: (i, 0))
    grp = pl.BlockSpec((tm, W_GRP), lambda i: (i, 0))
    return pl.pallas_call(
        body, name="mix_out_fwd", grid=(s // tm,),
        in_specs=[grp, grp, grp, grp, _full((W_GRP, W_GRP)), _full((1, W_GRP)), _full((1, d)), _full((d, d)), row,
                  _full((1, d))],
        out_specs=[row, row],
        out_shape=[jax.ShapeDtypeStruct((s, d), F32), jax.ShapeDtypeStruct((s, d), F32)],
        compiler_params=_cp(dimension_semantics=("parallel",)),
    )(*ys, *glu, mng, wmo, x, gate)


def _mix_out_bwd(dxo, m, gate, ys, glu, mng, wmo):
    s, d = dxo.shape
    tm = min(s, 512)
    nt = s // tm

    def body(dxo_ref, m_ref, gate_ref, ya, yb, yc, ypre, gw_ref, gb_ref, mng_ref, w_ref,
             dya, dyb, dyc, dypre, dgate_ref, dmng_ref, dw_ref, dgw_ref, dgb_ref, acc):
        i = pl.program_id(0)

        @pl.when(i == 0)
        def _():
            dgate_ref[...] = jnp.zeros_like(dgate_ref)
            dmng_ref[...] = jnp.zeros_like(dmng_ref)
            dgw_ref[...] = jnp.zeros_like(dgw_ref)
            dgb_ref[...] = jnp.zeros_like(dgb_ref)
            acc[...] = jnp.zeros_like(acc)

        dxv = dxo_ref[...]
        dgate_ref[...] += jnp.sum(m_ref[...] * dxv, axis=0, keepdims=True)
        dm = (gate_ref[...] * dxv).astype(BF16)
        mng = mng_ref[...]
        gw = gw_ref[...]
        yp = ypre[...]
        yd, yg, t, glu_gate = _s5_glu(yp, gw, gb_ref[...])
        yn, hats, rs = _group_norm([ya[...], yb[...], yc[...], yd], mng)
        acc[...] += _dot_tn(yn, dm)
        dyn = lax.dot_general(dm, w_ref[...], (((1,), (1,)), ((), ())), preferred_element_type=F32)
        dmng_parts, dys = [], []
        for k, (yh, r) in enumerate(zip(hats, rs)):
            dk = dyn[:, k * W_GRP:(k + 1) * W_GRP]
            dmng_parts.append(jnp.sum(dk * yh, axis=0, keepdims=True))
            dyh = dk * mng[:, k * W_GRP:(k + 1) * W_GRP]
            dys.append(r * (dyh - yh * jnp.mean(dyh * yh, axis=-1, keepdims=True)))
        dmng_ref[...] += jnp.concatenate(dmng_parts, axis=1)
        dya[...], dyb[...], dyc[...] = dys[0], dys[1], dys[2]
        dyd = dys[3]
        dlin = dyd * yg * glu_gate * (1.0 - glu_gate)
        dgw_ref[...] += _dot_tn(yg, dlin)
        dgb_ref[...] += jnp.sum(dlin, axis=0, keepdims=True)
        dypre[...] = (dyd * glu_gate + _dot_nt(dlin, gw)) * _gelu_grad(yp, t)

        @pl.when(i == nt - 1)
        def _():
            dw_ref[...] = acc[...].astype(BF16)

    row = pl.BlockSpec((tm, d), lambda i: (i, 0))
    grp = pl.BlockSpec((tm, W_GRP), lambda i: (i, 0))
    return pl.pallas_call(
        body, name="mix_out_bwd", grid=(nt,),
        in_specs=[row, row, _full((1, d)), grp, grp, grp, grp, _full((W_GRP, W_GRP)), _full((1, W_GRP)), _full((1, d)),
                  _full((d, d))],
        out_specs=[grp, grp, grp, grp, _full((1, d)), _full((1, d)), _full((d, d)), _full((W_GRP, W_GRP)),
                   _full((1, W_GRP))],
        out_shape=[jax.ShapeDtypeStruct((s, W_GRP), F32)] * 4
        + [jax.ShapeDtypeStruct((1, d), F32), jax.ShapeDtypeStruct((1, d), F32), jax.ShapeDtypeStruct((d, d), BF16),
           jax.ShapeDtypeStruct((W_GRP, W_GRP), F32), jax.ShapeDtypeStruct((1, W_GRP), F32)],
        scratch_shapes=[pltpu.VMEM((d, d), F32)],
        compiler_params=_cp(dimension_semantics=("arbitrary",)),
    )(dxo, m, gate, *ys, *glu, mng, wmo)


def _sgu_consts():
    r = _iota((W_GRP, W_GRP), 0) >> 6
    c = _iota((W_GRP, W_GRP), 1) >> 6
    avg = jnp.where(r == c, 1.0 / 64.0, 0.0).astype(F32)
    tril = _iota((CHUNK, CHUNK), 0) >= _iota((CHUNK, CHUNK), 1)
    head = _iota((CHUNK, W_GRP), 1) >> 6
    return avg, tril, head


def _sgu_pre(za, avg):
    zg, t = _gelu(za)
    u, v = zg[:, :W_GRP], zg[:, W_GRP:]
    mu = _dot_hi(v, avg)
    vc = v - mu
    r = lax.rsqrt(_dot_hi(vc * vc, avg) + EPS)
    return t, u, vc * r, r


def _sgu_fwd(z, sgu_w, bias_full):
    s = z.shape[0]
    tm = min(s, 512)

    def body(za_ref, w_ref, bias_ref, ya_ref):
        avg, tril, head = _sgu_consts()
        _, u, vn, _ = _sgu_pre(za_ref[...], avg)
        wm = [jnp.where(tril, w_ref[h], 0.0).astype(BF16) for h in range(4)]
        vb = vn.astype(BF16)
        for n in range(tm // CHUNK):
            rows = slice(n * CHUNK, (n + 1) * CHUNK)
            mixed = bias_ref[...]
            for h in range(4):
                mixed = mixed + jnp.where(head == h, jnp.dot(wm[h], vb[rows], preferred_element_type=F32), 0.0)
            ya_ref[rows, :] = u[rows] * mixed

    return pl.pallas_call(
        body, name="sgu_fwd", grid=(s // tm,),
        in_specs=[pl.BlockSpec((tm, 2 * W_GRP), lambda i: (i, 0)), _full((4, CHUNK, CHUNK)), _full((CHUNK, W_GRP))],
        out_specs=pl.BlockSpec((tm, W_GRP), lambda i: (i, 0)),
        out_shape=jax.ShapeDtypeStruct((s, W_GRP), F32),
        compiler_params=_cp(dimension_semantics=("parallel",)),
    )(z, sgu_w, bias_full)


def _sgu_bwd(z, dya, sgu_w, bias_full):
    s = z.shape[0]
    tm = min(s, 512)
    nt = s // tm

    def body(za_ref, dya_ref, w_ref, bias_ref, dza_ref, dw_ref, db_ref, du_s, dvn_s):
        i = pl.program_id(0)

        @pl.when(i == 0)
        def _():
            dw_ref[...] = jnp.zeros_like(dw_ref)
            db_ref[...] = jnp.zeros_like(db_ref)

        avg, tril, head = _sgu_consts()
        za = za_ref[...]
        t, u, vn, r = _sgu_pre(za, avg)
        wm = [jnp.where(tril, w_ref[h], 0.0).astype(BF16) for h in range(4)]
        vb = vn.astype(BF16)
        dya = dya_ref[...]
        dw = [jnp.zeros((CHUNK, CHUNK), F32) for _ in range(4)]
        db = jnp.zeros((CHUNK, W_GRP), F32)
        for n in range(tm // CHUNK):
            rows = slice(n * CHUNK, (n + 1) * CHUNK)
            mixed = bias_ref[...]
            for h in range(4):
                mixed = mixed + jnp.where(head == h, jnp.dot(wm[h], vb[rows], preferred_element_type=F32), 0.0)
            dmix = dya[rows] * u[rows]
            du_s[rows, :] = dya[rows] * mixed
            db = db + dmix
            dmb = dmix.astype(BF16)
            dvn = jnp.zeros((CHUNK, W_GRP), F32)
            for h in range(4):
                dmh = jnp.where(head == h, dmix, 0.0)
                dw[h] = dw[h] + _dot_nt(dmh, vb[rows])
                dvn = dvn + jnp.where(head == h, _dot_tn(wm[h], dmb), 0.0)
            dvn_s[rows, :] = dvn
        for h in range(4):
            dw_ref[h] += jnp.where(tril, dw[h], 0.0)
        sel = ((_iota((W_GRP, CHUNK), 0) >> 6) == _iota((W_GRP, CHUNK), 1)).astype(F32)
        db_ref[...] += _dot_hi(db, sel)
        dvn = dvn_s[...]
        dv = r * (dvn - _dot_hi(dvn, avg) - vn * _dot_hi(dvn * vn, avg))
        dzg = jnp.concatenate([du_s[...], dv], axis=1)
        dza_ref[...] = dzg * _gelu_grad(za, t)

    return pl.pallas_call(
        body, name="sgu_bwd", grid=(nt,),
        in_specs=[pl.BlockSpec((tm, 2 * W_GRP), lambda i: (i, 0)), pl.BlockSpec((tm, W_GRP), lambda i: (i, 0)),
                  _full((4, CHUNK, CHUNK)), _full((CHUNK, W_GRP))],
        out_specs=[pl.BlockSpec((tm, 2 * W_GRP), lambda i: (i, 0)), _full((4, CHUNK, CHUNK)), _full((CHUNK, CHUNK))],
        out_shape=[jax.ShapeDtypeStruct((s, 2 * W_GRP), F32), jax.ShapeDtypeStruct((4, CHUNK, CHUNK), F32),
                   jax.ShapeDtypeStruct((CHUNK, CHUNK), F32)],
        scratch_shapes=[pltpu.VMEM((tm, W_GRP), F32), pltpu.VMEM((tm, W_GRP), F32)],
        compiler_params=_cp(dimension_semantics=("arbitrary",)),
    )(z, dya, sgu_w, bias_full)


def _shift_down(x, k):
    return jnp.where(_iota(x.shape, 0) < k, 0.0, pltpu.roll(x, k, 0))


def _shift_up(x, k):
    n = x.shape[0]
    return jnp.where(_iota(x.shape, 0) >= n - k, 0.0, pltpu.roll(x, n - k, 0))


def _by_pool_group(shape, v2, v4, v8, v16):
    col = _iota(shape, 1)
    return jnp.where(col < 64, v2, jnp.where(col < 128, v4, jnp.where(col < 192, v8, v16)))


def _pool_core(zb, pw2):
    s2 = zb + _shift_down(zb, 1)
    s4 = s2 + _shift_down(s2, 2)
    s8 = s4 + _shift_down(s4, 4)
    s16 = s8 + _shift_down(s8, 8)
    win = _by_pool_group(zb.shape, s2, s4, s8, s16)
    wlen = _by_pool_group(zb.shape, 2.0, 4.0, 8.0, 16.0)
    cnt = jnp.minimum((_iota(zb.shape, 0) + 1).astype(F32), wlen)
    p = win / cnt - zb
    wt = jnp.tile(pw2, (1, 4))
    wbd = jnp.where((_iota(wt.shape, 0) >> 6) == (_iota(wt.shape, 1) >> 6), wt, 0.0).astype(BF16)
    return p, cnt, wbd


def _conv_core(zc, cw):
    bg, cg, xh = zc[:, :W_GRP], zc[:, W_GRP:2 * W_GRP], zc[:, 2 * W_GRP:]
    y = cg * xh
    y1, y2 = _shift_down(y, 1), _shift_down(y, 2)
    out = cw[2:3, :] * y + cw[1:2, :] * y1 + cw[0:1, :] * y2
    return bg, cg, xh, y, y1, y2, out


def _poolconv_fwd(z, pw2, pscale, cw):
    s = z.shape[0]

    def body(zb_ref, zc_ref, pw_ref, ps_ref, cw_ref, yb_ref, yc_ref):
        p, _, wbd = _pool_core(zb_ref[...], pw_ref[...])
        yb_ref[...] = jnp.dot(p.astype(BF16), wbd, preferred_element_type=F32) * ps_ref[...]
        bg, _, _, _, _, _, out = _conv_core(zc_ref[...], cw_ref[...])
        yc_ref[...] = bg * out

    return pl.pallas_call(
        body, name="poolconv_fwd", grid=(1,),
        in_specs=[pl.BlockSpec((s, W_GRP), lambda i: (0, 2)), pl.BlockSpec((s, 3 * W_GRP), lambda i: (0, 1)),
                  _full((W_GRP, 64)), _full((1, W_GRP)), _full((3, W_GRP))],
        out_specs=[_full((s, W_GRP)), _full((s, W_GRP))],
        out_shape=[jax.ShapeDtypeStruct((s, W_GRP), F32)] * 2,
        compiler_params=_cp(dimension_semantics=("arbitrary",)),
    )(z, z, pw2, pscale, cw)


def _poolconv_bwd(z, dyb, dyc, pw2, pscale, cw):
    s = z.shape[0]

    def body(zb_ref, zc_ref, dyb_ref, dyc_ref, pw_ref, ps_ref, cw_ref, dzb_ref, dzc_ref, dw_ref, dps_ref, dcw_ref):
        zb = zb_ref[...]
        p, cnt, wbd = _pool_core(zb, pw_ref[...])
        pb = p.astype(BF16)
        out = jnp.dot(pb, wbd, preferred_element_type=F32)
        dyb = dyb_ref[...]
        dps_ref[...] = jnp.sum(dyb * out, axis=0, keepdims=True)
        dout = (dyb * ps_ref[...]).astype(BF16)
        dw = _dot_tn(pb, dout)
        dw_ref[...] = jnp.where((_iota(dw.shape, 0) >> 6) == (_iota(dw.shape, 1) >> 6), dw, 0.0)
        dp = lax.dot_general(dout, wbd, (((1,), (1,)), ((), ())), preferred_element_type=F32)
        dwin = dp / cnt
        t2 = dwin + _shift_up(dwin, 1)
        t4 = t2 + _shift_up(t2, 2)
        t8 = t4 + _shift_up(t4, 4)
        t16 = t8 + _shift_up(t8, 8)
        dzb_ref[...] = _by_pool_group(zb.shape, t2, t4, t8, t16) - dp

        cw = cw_ref[...]
        bg, cg, xh, y, y1, y2, out = _conv_core(zc_ref[...], cw)
        dyc = dyc_ref[...]
        dout = dyc * bg
        dcw_ref[...] = jnp.zeros_like(dcw_ref)
        dcw_ref[0:1, :] = jnp.sum(dout * y2, axis=0, keepdims=True)
        dcw_ref[1:2, :] = jnp.sum(dout * y1, axis=0, keepdims=True)
        dcw_ref[2:3, :] = jnp.sum(dout * y, axis=0, keepdims=True)
        dy = cw[2:3, :] * dout + cw[1:2, :] * _shift_up(dout, 1) + cw[0:1, :] * _shift_up(dout, 2)
        dzc_ref[...] = jnp.concatenate([dyc * out, dy * xh, dy * cg], axis=1)

    return pl.pallas_call(
        body, name="poolconv_bwd", grid=(1,),
        in_specs=[pl.BlockSpec((s, W_GRP), lambda i: (0, 2)), pl.BlockSpec((s, 3 * W_GRP), lambda i: (0, 1)),
                  _full((s, W_GRP)), _full((s, W_GRP)), _full((W_GRP, 64)), _full((1, W_GRP)), _full((3, W_GRP))],
        out_specs=[_full((s, W_GRP)), _full((s, 3 * W_GRP)), _full((W_GRP, W_GRP)), _full((1, W_GRP)), _full((8, W_GRP))],
        out_shape=[jax.ShapeDtypeStruct((s, W_GRP), F32), jax.ShapeDtypeStruct((s, 3 * W_GRP), F32),
                   jax.ShapeDtypeStruct((W_GRP, W_GRP), F32), jax.ShapeDtypeStruct((1, W_GRP), F32),
                   jax.ShapeDtypeStruct((8, W_GRP), F32)],
        compiler_params=_cp(dimension_semantics=("arbitrary",)),
    )(z, z, dyb, dyc, pw2, pscale, cw)


N_STATE = 1024
HALF_STATE = N_STATE // 2
HALF_CH = W_GRP // 2
N_SLAB = HALF_STATE // LANES


def _s5_disc(lre, lim, ldt):
    dt = jnp.exp(ldt)
    mag = jnp.exp(lre * dt)
    ang = lim * dt
    ar, ai = mag * jnp.cos(ang), mag * jnp.sin(ang)
    nr, ni = ar - 1.0, ai
    den = lre * lre + lim * lim
    kr = (nr * lre + ni * lim) / den
    ki = (ni * lre - nr * lim) / den
    return ar, ai, kr, ki


def _s5_mats(colp, br, bi, cr, ci):
    _, _, kr, ki = _s5_disc(colp[:, 0:1], colp[:, 1:2], colp[:, 2:3])
    bbr = kr * br - ki * bi
    bbi = kr * bi + ki * br
    bmask = (_iota((HALF_STATE, HALF_CH), 0) >> 6) == (_iota((HALF_STATE, HALF_CH), 1) >> 4)
    cmask = (_iota((HALF_CH, HALF_STATE), 0) >> 4) == (_iota((HALF_CH, HALF_STATE), 1) >> 6)
    btr = jnp.where(bmask, jnp.tile(bbr, (1, 8)), 0.0).astype(BF16)
    bti = jnp.where(bmask, jnp.tile(bbi, (1, 8)), 0.0).astype(BF16)
    ctr = jnp.where(cmask, jnp.tile(cr, (1, 8)), 0.0).astype(BF16)
    cti = jnp.where(cmask, jnp.tile(ci, (1, 8)), 0.0).astype(BF16)
    return kr, ki, btr, bti, ctr, cti, bmask, cmask


def _slab(q):
    return slice(q * LANES, (q + 1) * LANES)


def _cmul(ar, ai, br, bi):
    return ar * br - ai * bi, ar * bi + ai * br


def _sub_shift(x, k, up):
    row = _iota(x.shape, 0)
    if up:
        return jnp.where(row >= N_SEG - k, 0.0, pltpu.roll(x, N_SEG - k, 0))
    return jnp.where(row < k, 0.0, pltpu.roll(x, k, 0))


def _seg_rows(j):
    return pl.ds(pl.multiple_of(j * N_SEG, N_SEG), N_SEG)


def _interleave(src, dst, seg):
    def step(j, carry):
        dst[_seg_rows(j), :] = src[pl.ds(j, N_SEG, stride=seg), :]
        return carry
    lax.fori_loop(0, seg, step, 0)


def _deinterleave(src, dst, seg):
    def step(j, carry):
        dst[pl.ds(j, N_SEG, stride=seg), :] = src[_seg_rows(j), :]
        return carry
    lax.fori_loop(0, seg, step, 0)


def _scan(xr, xi, ar_row, ai_row, seg, reverse, states=None):
    nlog = int(math.log2(seg))
    assert (1 << nlog) == seg
    grads = []
    for q0 in range(0, N_SLAB, 4):
        qs = list(range(q0, q0 + 4))
        aq = [(jnp.broadcast_to(ar_row[:, _slab(q)], (N_SEG, LANES)),
               jnp.broadcast_to(ai_row[:, _slab(q)], (N_SEG, LANES))) for q in qs]
        zero = jnp.zeros((N_SEG, LANES), F32)

        def local(jj, carry, qs=qs, aq=aq):
            j = seg - 1 - jj if reverse else jj
            out = []
            for n, q in enumerate(qs):
                rows = _seg_rows(j)
                pr, pi = _cmul(aq[n][0], aq[n][1], carry[2 * n], carry[2 * n + 1])
                nr = pr + xr[q, rows, :]
                ni = pi + xi[q, rows, :]
                xr[q, rows, :] = nr
                xi[q, rows, :] = ni
                out += [nr, ni]
            return tuple(out)

        fin = lax.fori_loop(0, seg, local, (zero,) * 8)
        cins = []
        for n in range(4):
            er, ei = fin[2 * n], fin[2 * n + 1]
            pr, pi = aq[n]
            for _ in range(nlog):
                pr, pi = _cmul(pr, pi, pr, pi)
            yr, yi = er, ei
            for k in (1, 2, 4):
                sr, si = _cmul(pr, pi, _sub_shift(yr, k, reverse), _sub_shift(yi, k, reverse))
                yr, yi = yr + sr, yi + si
                pr, pi = _cmul(pr, pi, pr, pi)
            cins.append((_sub_shift(yr, 1, reverse), _sub_shift(yi, 1, reverse)))

        def fix(jj, carry, qs=qs, aq=aq, cins=cins):
            j = seg - 1 - jj if reverse else jj
            out, sums = [], []
            for n, q in enumerate(qs):
                rows = _seg_rows(j)
                pwr, pwi = carry[2 * n], carry[2 * n + 1]
                cr, ci = _cmul(pwr, pwi, cins[n][0], cins[n][1])
                v_r, v_i = xr[q, rows, :] + cr, xi[q, rows, :] + ci
                xr[q, rows, :] = v_r
                xi[q, rows, :] = v_i
                nr, ni = _cmul(pwr, pwi, aq[n][0], aq[n][1])
                out += [nr, ni]
                if states is not None:
                    prev = _seg_rows(j - 1)
                    p_r, p_i = states[0][q, prev, :], states[1][q, prev, :]
                    sums += [carry[8 + 2 * n] + v_r * p_r + v_i * p_i, carry[9 + 2 * n] - v_r * p_i + v_i * p_r]
            return tuple(out + sums)

        powers = tuple(v for pair in aq for v in pair)
        if states is None:
            lax.fori_loop(0, seg, fix, powers)
            continue
        assert reverse
        fix_last = lax.fori_loop(0, seg - 1, fix, powers + (zero,) * 8)
        first = _seg_rows(0)
        for n, q in enumerate(qs):
            cr, ci = _cmul(fix_last[2 * n], fix_last[2 * n + 1], cins[n][0], cins[n][1])
            v_r, v_i = xr[q, first, :] + cr, xi[q, first, :] + ci
            xr[q, first, :] = v_r
            xi[q, first, :] = v_i
            p_r = _sub_shift(states[0][q, _seg_rows(seg - 1), :], 1, False)
            p_i = _sub_shift(states[1][q, _seg_rows(seg - 1), :], 1, False)
            grads.append((jnp.sum(fix_last[8 + 2 * n] + v_r * p_r + v_i * p_i, axis=0, keepdims=True),
                          jnp.sum(fix_last[9 + 2 * n] - v_r * p_i + v_i * p_r, axis=0, keepdims=True)))
    return grads


def _s5_forward_states(u, btr, bti, ar_row, ai_row, xr, xi, seg):
    ub = u.astype(BF16)
    for q in range(N_SLAB):
        xr[q] = _dot_nt(ub, btr[_slab(q), :])
        xi[q] = _dot_nt(ub, bti[_slab(q), :])
    _scan(xr, xi, ar_row, ai_row, seg, False)


def _s5_readout(u, xr, xi, ctr, cti, d):
    y = d * u
    for q in range(N_SLAB):
        y = y + _dot_nt(xr[q], ctr[:, _slab(q)]) - _dot_nt(xi[q], cti[:, _slab(q)])
    return y


def _s5_param_specs():
    return [pl.BlockSpec((3, HALF_STATE), lambda i: (0, i)), pl.BlockSpec((HALF_STATE, 3), lambda i: (i, 0)),
            pl.BlockSpec((HALF_STATE, 16), lambda i: (i, 0)), pl.BlockSpec((HALF_STATE, 16), lambda i: (i, 0)),
            pl.BlockSpec((HALF_CH, 64), lambda i: (i, 0)), pl.BlockSpec((HALF_CH, 64), lambda i: (i, 0)),
            pl.BlockSpec((1, HALF_CH), lambda i: (0, i))]


def _s5_core_fwd(z, sp):
    s = z.shape[0]
    seg = s // N_SEG

    def body(u_ref, rowp, colp, br, bi, cr, ci, d_ref, y_ref, xr, xi, us, ys):
        ar, ai, _, _ = _s5_disc(rowp[0:1, :], rowp[1:2, :], rowp[2:3, :])
        _, _, btr, bti, ctr, cti, _, _ = _s5_mats(colp[...], br[...], bi[...], cr[...], ci[...])
        _interleave(u_ref, us, seg)
        u = us[...]
        _s5_forward_states(u, btr, bti, ar, ai, xr, xi, seg)
        ys[...] = _s5_readout(u, xr, xi, ctr, cti, d_ref[...])
        _deinterleave(ys, y_ref, seg)

    return pl.pallas_call(
        body, name="s5_core_fwd", grid=(2,),
        in_specs=[pl.BlockSpec((s, HALF_CH), lambda i: (0, 12 + i))] + _s5_param_specs(),
        out_specs=pl.BlockSpec((s, HALF_CH), lambda i: (0, i)),
        out_shape=jax.ShapeDtypeStruct((s, W_GRP), F32),
        scratch_shapes=[pltpu.VMEM((N_SLAB, s, LANES), F32)] * 2 + [pltpu.VMEM((s, HALF_CH), F32)] * 2,
        compiler_params=_cp(dimension_semantics=("parallel",)),
    )(z, *sp)


def _s5_core_bwd(z, dy, sp):
    s = z.shape[0]
    seg = s // N_SEG

    def body(u_ref, dy_ref, rowp, colp, br_ref, bi_ref, cr_ref, ci_ref, d_ref,
             du_ref, dbr_ref, dbi_ref, dcr_ref, dci_ref, dd_ref, da_ref, dk_ref,
             xr, xi, gr, gi, us, dys):
        ar, ai, _, _ = _s5_disc(rowp[0:1, :], rowp[1:2, :], rowp[2:3, :])
        br, bi = br_ref[...], bi_ref[...]
        kr, ki, btr, bti, ctr, cti, bmask, cmask = _s5_mats(colp[...], br, bi, cr_ref[...], ci_ref[...])
        _interleave(u_ref, us, seg)
        _interleave(dy_ref, dys, seg)
        u = us[...]
        d = d_ref[...]
        _s5_forward_states(u, btr, bti, ar, ai, xr, xi, seg)

        dy = dys[...]
        dd_ref[...] = jnp.sum(dy * u, axis=0, keepdims=True)
        du = d * dy
        dyb = dy.astype(BF16)
        dctr, dcti = [], []
        for q in range(N_SLAB):
            gr[q] = jnp.dot(dyb, ctr[:, _slab(q)], preferred_element_type=F32)
            gi[q] = -jnp.dot(dyb, cti[:, _slab(q)], preferred_element_type=F32)
            dctr.append(_dot_tn(dyb, xr[q]))
            dcti.append(-_dot_tn(dyb, xi[q]))
        selp = ((_iota((HALF_STATE, 64), 0) & 63) == _iota((HALF_STATE, 64), 1)).astype(F32)
        dcr_ref[...] = _dot_hi(jnp.where(cmask, jnp.concatenate(dctr, axis=1), 0.0), selp)
        dci_ref[...] = _dot_hi(jnp.where(cmask, jnp.concatenate(dcti, axis=1), 0.0), selp)

        da = _scan(gr, gi, ar, -ai, seg, True, states=(xr, xi))
        dar, dai = [p[0] for p in da], [p[1] for p in da]
        da_ref[...] = jnp.zeros_like(da_ref)
        da_ref[0:1, :] = jnp.concatenate(dar, axis=1)
        da_ref[1:2, :] = jnp.concatenate(dai, axis=1)

        ub = u.astype(BF16)
        dbtr, dbti = [], []
        for q in range(N_SLAB):
            g_r, g_i = gr[q].astype(BF16), gi[q].astype(BF16)
            du = du + jnp.dot(g_r, btr[_slab(q), :], preferred_element_type=F32) \
                + jnp.dot(g_i, bti[_slab(q), :], preferred_element_type=F32)
            dbtr.append(_dot_tn(g_r, ub))
            dbti.append(_dot_tn(g_i, ub))
        us[...] = du
        _deinterleave(us, du_ref, seg)
        selc =((_iota((HALF_CH, 16), 0) & 15) == _iota((HALF_CH, 16), 1)).astype(F32)
        dbbr = _dot_hi(jnp.where(bmask, jnp.concatenate(dbtr, axis=0), 0.0), selc)
        dbbi = _dot_hi(jnp.where(bmask, jnp.concatenate(dbti, axis=0), 0.0), selc)
        dbr_ref[...] = kr * dbbr + ki * dbbi
        dbi_ref[...] = kr * dbbi - ki * dbbr
        dk_ref[:, 0:1] = jnp.sum(dbbr * br + dbbi * bi, axis=1, keepdims=True)
        dk_ref[:, 1:2] = jnp.sum(dbbi * br - dbbr * bi, axis=1, keepdims=True)

    half = pl.BlockSpec((s, HALF_CH), lambda i: (0, i))
    return pl.pallas_call(
        body, name="s5_core_bwd", grid=(2,),
        in_specs=[pl.BlockSpec((s, HALF_CH), lambda i: (0, 12 + i)), half] + _s5_param_specs(),
        out_specs=[half, pl.BlockSpec((HALF_STATE, 16), lambda i: (i, 0)), pl.BlockSpec((HALF_STATE, 16), lambda i: (i, 0)),
                   pl.BlockSpec((HALF_CH, 64), lambda i: (i, 0)), pl.BlockSpec((HALF_CH, 64), lambda i: (i, 0)),
                   pl.BlockSpec((1, HALF_CH), lambda i: (0, i)), pl.BlockSpec((8, HALF_STATE), lambda i: (0, i)),
                   pl.BlockSpec((HALF_STATE, 2), lambda i: (i, 0))],
        out_shape=[jax.ShapeDtypeStruct((s, W_GRP), F32), jax.ShapeDtypeStruct((N_STATE, 16), F32),
                   jax.ShapeDtypeStruct((N_STATE, 16), F32), jax.ShapeDtypeStruct((W_GRP, 64), F32),
                   jax.ShapeDtypeStruct((W_GRP, 64), F32), jax.ShapeDtypeStruct((1, W_GRP), F32),
                   jax.ShapeDtypeStruct((8, N_STATE), F32), jax.ShapeDtypeStruct((N_STATE, 2), F32)],
        scratch_shapes=[pltpu.VMEM((N_SLAB, s, LANES), F32)] * 4 + [pltpu.VMEM((s, HALF_CH), F32)] * 2,
        compiler_params=_cp(dimension_semantics=("parallel",)),
    )(z, dy, *sp)


def _s5_param_bwd(lre, lim, ldt, da_r, da_i, dk_r, dk_i):
    n = lre.shape[0]

    def body(lre_ref, lim_ref, ldt_ref, dar_ref, dai_ref, dkr_ref, dki_ref, o_re, o_im, o_dt):
        lre, lim, ldt = lre_ref[...], lim_ref[...], ldt_ref[...]
        dt = jnp.exp(ldt)
        ar, ai, kr, ki = _s5_disc(lre, lim, ldt)
        mag = jnp.exp(lre * dt)
        den = lre * lre + lim * lim
        dkr, dki = dkr_ref[...], dki_ref[...]
        nr, ni = ar - 1.0, ai
        d_ar = dar_ref[...] + (dkr * lre - dki * lim) / den
        d_ai = dai_ref[...] + (dkr * lim + dki * lre) / den
        kk = (kr * dkr + ki * dki) * 2.0 / den
        d_lre = (dkr * nr + dki * ni) / den - kk * lre
        d_lim = (dkr * ni - dki * nr) / den - kk * lim
        d_mag = (d_ar * ar + d_ai * ai) / mag
        d_ang = d_ai * ar - d_ar * ai
        o_re[...] = d_lre + d_mag * mag * dt
        o_im[...] = d_lim + d_ang * dt
        o_dt[...] = jnp.sum((d_mag * mag * lre + d_ang * lim) * dt, axis=1, keepdims=True)

    return pl.pallas_call(
        body, name="s5_param_bwd",
        out_shape=[jax.ShapeDtypeStruct((n, 64), F32), jax.ShapeDtypeStruct((n, 64), F32),
                   jax.ShapeDtypeStruct((n, 1), F32)],
    )(lre, lim, ldt, da_r, da_i, dk_r, dk_i)


def _loss_head(x, fg, target):
    s, d = x.shape
    tm = _tm(s)

    def body(x_ref, fg_ref, t_ref, loss_ref, dx_ref, dfg_ref):
        i = pl.program_id(0)

        @pl.when(i == 0)
        def _():
            loss_ref[...] = jnp.zeros_like(loss_ref)
            dfg_ref[...] = jnp.zeros_like(dfg_ref)

        xv, g = x_ref[...], fg_ref[...]
        r = lax.rsqrt(jnp.mean(xv * xv, axis=-1, keepdims=True) + EPS)
        xh = xv * r
        err = xh * g - t_ref[...]
        loss_ref[...] += 0.5 * jnp.sum(jnp.mean(err * err, axis=-1, keepdims=True), axis=0, keepdims=True)
        dy = err * (1.0 / d)
        dfg_ref[...] += jnp.sum(dy * xh, axis=0, keepdims=True)
        dxh = dy * g
        dx_ref[...] = r * (dxh - xh * jnp.mean(dxh * xh, axis=-1, keepdims=True))

    row = pl.BlockSpec((tm, d), lambda i: (i, 0))
    return pl.pallas_call(
        body, name="loss_head", grid=(s // tm,),
        in_specs=[row, _full((1, d)), row], out_specs=[_full((1, 1)), row, _full((1, d))],
        out_shape=[jax.ShapeDtypeStruct((1, 1), F32), jax.ShapeDtypeStruct((s, d), F32),
                   jax.ShapeDtypeStruct((1, d), F32)],
        compiler_params=_cp(dimension_semantics=("arbitrary",)),
    )(x, fg, target)


ADA_TN = 384


def _cond_fwd(cact, ada_w, ada_b_loc):
    nl, d, n = ada_w.shape

    def body(c_ref, w_ref, b_ref, o_ref):
        o_ref[...] = _dot(c_ref[...], w_ref[...]) + b_ref[...]

    return pl.pallas_call(
        body, name="cond_fwd", grid=(nl, n // ADA_TN),
        in_specs=[_full((N_DEV, d)), pl.BlockSpec((None, d, ADA_TN), lambda l, j: (l, 0, j)),
                  pl.BlockSpec((None, 1, ADA_TN), lambda l, j: (l, 0, j))],
        out_specs=pl.BlockSpec((None, N_DEV, ADA_TN), lambda l, j: (l, 0, j)),
        out_shape=jax.ShapeDtypeStruct((nl, N_DEV, n), F32),
        compiler_params=_cp(dimension_semantics=("parallel", "parallel")),
    )(cact, ada_w, ada_b_loc)


ELEMENTWISE_BLOCK_BYTES = 1 << 20


def _row_tile(r, c, itemsize=4):
    best = None
    for t in range(8, r + 1, 8):
        if r % t == 0 and t * c * itemsize <= ELEMENTWISE_BLOCK_BYTES:
            best = t
    return best if best is not None else r


def _adamw_math(w, g, m, v):
    m = ADAM_B1 * m + (1.0 - ADAM_B1) * g
    v = ADAM_B2 * v + (1.0 - ADAM_B2) * (g * g)
    m_hat = m / (1.0 - ADAM_B1 ** ADAM_STEP)
    v_hat = v / (1.0 - ADAM_B2 ** ADAM_STEP)
    delta = -ADAM_LR * (m_hat / (jnp.sqrt(v_hat) + ADAM_EPS) + ADAM_WD * w)
    return delta, m, v


def _ada_w_update(cact, dcond_loc, w, m, v):
    nl, d, n = w.shape

    def body(c_ref, dc_ref, w_ref, m_ref, v_ref, g_out, d_out, m_out, v_out):
        g = _dot_tn(c_ref[...], dc_ref[...])
        g_out[...] = g
        d_out[...], m_out[...], v_out[...] = _adamw_math(w_ref[...], g, m_ref[...], v_ref[...])

    blk = pl.BlockSpec((None, d, ADA_TN), lambda l, j: (l, 0, j))
    return pl.pallas_call(
        body, name="ada_w_update", grid=(nl, n // ADA_TN),
        in_specs=[_full((N_DEV, d)), pl.BlockSpec((None, N_DEV, ADA_TN), lambda l, j: (l, 0, j)), blk, blk, blk],
        out_specs=[blk] * 4, out_shape=[jax.ShapeDtypeStruct((nl, d, n), F32)] * 4,
        compiler_params=_cp(dimension_semantics=("parallel", "parallel")),
    )(cact, dcond_loc, w, m, v)


def _place():
    x, y, c = lax.axis_index("x"), lax.axis_index("y"), lax.axis_index("c")
    chips = [(1 - x, y), (x, 1 - y), (1 - x, 1 - y)]
    return x, y, c, chips


def _remote(src, dst, send_sem, recv_sem, to):
    return pltpu.make_async_remote_copy(src_ref=src, dst_ref=dst, send_sem=send_sem, recv_sem=recv_sem,
                                        device_id=to, device_id_type=MESH_ID)


def _sems(n):
    return [pltpu.SemaphoreType.DMA((n,)), pltpu.SemaphoreType.DMA((n,))]


def _all_gather8(v, name):
    r, cdim = v.shape

    def body(x_ref, out_ref, stage, send_sems, recv_sems):
        x, y, c, chips = _place()
        sibling = (x, y, 1 - c)

        def slot(px, py, pc):
            return out_ref.at[4 * px + 2 * py + pc]

        first = [_remote(x_ref, slot(x, y, c), send_sems.at[0], recv_sems.at[0], sibling)]
        first += [_remote(x_ref, slot(x, y, c), send_sems.at[1 + j], recv_sems.at[1 + j], (*chip, c))
                  for j, chip in enumerate(chips)]
        for cp in first:
            cp.start()
        pltpu.sync_copy(x_ref, stage)
        pltpu.sync_copy(stage, slot(x, y, c))
        passed = []
        for j, chip in enumerate(chips):
            blk = slot(*chip, c)
            _remote(blk, blk, send_sems.at[1 + j], recv_sems.at[1 + j], (x, y, c)).wait_recv()
            fw = _remote(blk, blk, send_sems.at[4 + j], recv_sems.at[4 + j], sibling)
            fw.start()
            passed.append(fw)
        blk = slot(x, y, 1 - c)
        _remote(blk, blk, send_sems.at[0], recv_sems.at[0], (x, y, c)).wait_recv()
        for j, chip in enumerate(chips):
            blk = slot(*chip, 1 - c)
            _remote(blk, blk, send_sems.at[4 + j], recv_sems.at[4 + j], (x, y, c)).wait_recv()
        for cp in first + passed:
            cp.wait_send()

    return pl.pallas_call(
        body, name=name, out_shape=jax.ShapeDtypeStruct((N_DEV, r, cdim), v.dtype),
        in_specs=[ANY], out_specs=ANY,
        scratch_shapes=[pltpu.VMEM((r, cdim), v.dtype)] + _sems(7),
        compiler_params=_cp(),
    )(v)


def _gather_first_copies():
    def make(refs, send_sems, recv_sems):
        x, y, c, chips = _place()
        mine = refs[0].at[4 * x + 2 * y + c]
        to = [(x, y, 1 - c)] + [(*chip, c) for chip in chips]
        return [_remote(mine, mine, send_sems.at[k], recv_sems.at[k], dev) for k, dev in enumerate(to)]
    return make


def _gather_pass_on(buf, name):
    def body(in_ref, out_ref, send_sems, recv_sems):
        x, y, c, chips = _place()
        passed = []
        for j, chip in enumerate(chips):
            blk = out_ref.at[4 * chip[0] + 2 * chip[1] + c]
            fw = _remote(blk, blk, send_sems.at[j], recv_sems.at[j], (x, y, 1 - c))
            fw.start()
            passed.append(fw)
        for j, chip in enumerate(chips):
            blk = out_ref.at[4 * chip[0] + 2 * chip[1] + 1 - c]
            _remote(blk, blk, send_sems.at[j], recv_sems.at[j], (x, y, c)).wait_recv()
        for fw in passed:
            fw.wait_send()

    return pl.pallas_call(
        body, name=name, out_shape=jax.ShapeDtypeStruct(buf.shape, buf.dtype),
        in_specs=[ANY], out_specs=ANY, input_output_aliases={0: 0}, scratch_shapes=_sems(3),
    )(buf)


def _place_weights(ws, layer, kidx, after):
    steps = 4
    shapes, in_specs, out_specs = [], [], []
    for w, kind in zip(ws, BIG_KINDS):
        _, a, b = w.shape
        in_specs.append(pl.BlockSpec((None, a // steps, b), lambda i, k: (layer, i, 0)))
        if kind == "col":
            shapes.append((2, a, 2 * b))
            out_specs.append(pl.BlockSpec((None, a // steps, b), lambda i, k: (k[0] // 2, i, k[0] % 2)))
        else:
            shapes.append((N_CHIP, a, b))
            out_specs.append(pl.BlockSpec((None, a // steps, b), lambda i, k: (k[0], i, 0)))

    def body(k_ref, *refs):
        outs = refs[len(ws) + 1:]
        for t in range(len(ws)):
            outs[t][...] = refs[t][...].astype(BF16)

    return pl.pallas_call(
        body, name="place_weights", out_shape=[jax.ShapeDtypeStruct(s, BF16) for s in shapes],
        grid_spec=pltpu.PrefetchScalarGridSpec(num_scalar_prefetch=1, grid=(steps,), in_specs=in_specs + [ANY],
                                               out_specs=out_specs),
        compiler_params=_cp(dimension_semantics=("parallel",)),
    )(kidx, *ws, after)


HBM = pl.BlockSpec(memory_space=pltpu.HBM)
SEM = pl.BlockSpec(memory_space=pltpu.SEMAPHORE)
EFFECT = pltpu.SideEffectType.DATAFLOW_SIDE_EFFECTING


def _weight_block(ref, kind, k, h):
    if kind == "col":
        ncol = ref.shape[3] // 2
        return ref.at[k // 2, h, :, pl.ds(pl.multiple_of((k % 2) * ncol, LANES), ncol)]
    return ref.at[k, h]


def _in_hbm(a):
    return pltpu.with_memory_space_constraint(a, pltpu.HBM)


def _weight_send_start(placed, kinds, name):
    nt = len(placed)

    def body(*refs):
        send_sems, recv_sems = refs[nt], refs[nt + 1]
        dst = refs[nt + 2:2 * nt + 2]
        token = refs[2 * nt + 2]
        x, y, c, chips = _place()
        kme = 2 * x + y
        for t in range(nt):
            for j, chip in enumerate(chips):
                own = _weight_block(dst[t], kinds[t], kme, c)
                _remote(own, own, send_sems.at[3 * t + j], recv_sems.at[3 * t + j], (*chip, c)).start()
        token[...] = jnp.zeros_like(token)

    return pl.pallas_call(
        body, name=name,
        out_shape=(pltpu.SemaphoreType.DMA((3 * nt,)), pltpu.SemaphoreType.DMA((3 * nt,)),
                   *[pltpu.HBM(a.shape, a.dtype) for a in placed], jax.ShapeDtypeStruct((8, LANES), F32)),
        in_specs=[HBM] * nt, out_specs=(SEM, SEM, *[HBM] * nt, pl.BlockSpec(memory_space=pltpu.VMEM)),
        input_output_aliases={t: 2 + t for t in range(nt)},
        compiler_params=pltpu.CompilerParams(has_side_effects=EFFECT),
    )(*[_in_hbm(a) for a in placed])


def _weight_send_wait(send_sems, recv_sems, arrays, kinds, after, name):
    nt = len(arrays)

    def body(*refs):
        arr = refs[:nt]
        send_sems, recv_sems = refs[nt], refs[nt + 1]
        x, y, c, chips = _place()
        kme = 2 * x + y
        for t in range(nt):
            for j, chip in enumerate(chips):
                own = _weight_block(arr[t], kinds[t], kme, c)
                got = _weight_block(arr[t], kinds[t], 2 * chip[0] + chip[1], c)
                cp = _remote(own, got, send_sems.at[3 * t + j], recv_sems.at[3 * t + j], (*chip, c))
                cp.wait_send()
                cp.wait_recv()

    return pl.pallas_call(
        body, name=name, out_shape=[pltpu.HBM(a.shape, a.dtype) for a in arrays],
        in_specs=[HBM] * nt + [SEM, SEM, ANY], out_specs=[HBM] * nt,
        input_output_aliases={t: t for t in range(nt)},
        compiler_params=pltpu.CompilerParams(has_side_effects=EFFECT),
    )(*arrays, send_sems, recv_sems, after)


def _forward_copies(kinds):
    def make(refs, send_sems, recv_sems):
        x, y, c, chips = _place()
        cps = []
        for t in range(len(kinds)):
            for j, chip in enumerate(chips):
                blk = _weight_block(refs[t], kinds[t], 2 * chip[0] + chip[1], c)
                cps.append(_remote(blk, blk, send_sems.at[3 * t + j], recv_sems.at[3 * t + j], (x, y, 1 - c)))
        return cps
    return make


def _split_start(name, arrays, n_copies, make_copies):
    na = len(arrays)

    def body(*refs):
        send_sems, recv_sems = refs[na], refs[na + 1]
        for cp in make_copies(refs[na + 2:2 * na + 2], send_sems, recv_sems):
            cp.start()
        token = refs[2 * na + 2]
        token[...] = jnp.zeros_like(token)

    return pl.pallas_call(
        body, name=name,
        out_shape=(pltpu.SemaphoreType.DMA((n_copies,)), pltpu.SemaphoreType.DMA((n_copies,)),
                   *[pltpu.HBM(a.shape, a.dtype) for a in arrays], jax.ShapeDtypeStruct((8, LANES), F32)),
        in_specs=[HBM] * na, out_specs=(SEM, SEM, *[HBM] * na, pl.BlockSpec(memory_space=pltpu.VMEM)),
        input_output_aliases={t: 2 + t for t in range(na)},
        compiler_params=pltpu.CompilerParams(has_side_effects=EFFECT),
    )(*[_in_hbm(a) for a in arrays])


def _split_wait(name, started, make_copies, after):
    send_sems, recv_sems, *arrays, _ = started
    na = len(arrays)

    def body(*refs):
        send_sems, recv_sems = refs[na], refs[na + 1]
        for cp in make_copies(refs[:na], send_sems, recv_sems):
            cp.wait_send()
            cp.wait_recv()

    return pl.pallas_call(
        body, name=name, out_shape=[pltpu.HBM(a.shape, a.dtype) for a in arrays],
        in_specs=[HBM] * na + [SEM, SEM, ANY], out_specs=[HBM] * na,
        input_output_aliases={t: t for t in range(na)},
        compiler_params=pltpu.CompilerParams(has_side_effects=EFFECT),
    )(*arrays, send_sems, recv_sems, after)


def _exchange_copies(nt):
    def make(refs, send_sems, recv_sems):
        x, y, c, _ = _place()
        return [_remote(refs[t].at[:, 1 - c], refs[nt + t], send_sems.at[t], recv_sems.at[t], (x, y, 1 - c))
                for t in range(nt)]
    return make


def _sibling_exchange_start(views, name):
    lands = [lax.empty((v.shape[0],) + v.shape[2:], v.dtype) for v in views]
    return _split_start(name, list(views) + lands, len(views), _exchange_copies(len(views)))


def _sibling_exchange_wait(started, after, name):
    nt = (len(started) - 3) // 2
    outs = _split_wait(name, started, _exchange_copies(nt), after)
    return outs[:nt], outs[nt:]


def _scatter_copies(src, land, kinds, send_sems, recv_sems):
    x, y, c, chips = _place()
    cps = []
    for t in range(len(src)):
        for j, chip in enumerate(chips):
            k = 2 * chip[0] + chip[1]
            if kinds[t] == "col":
                ncol = land[t].shape[2]
                win = src[t].at[k // 2, :, pl.ds(pl.multiple_of((k % 2) * ncol, LANES), ncol)]
            else:
                win = src[t].at[k]
            cps.append(_remote(win, land[t].at[j], send_sems.at[3 * t + j], recv_sems.at[3 * t + j], (*chip, c)))
    return cps


def _chip_scatter_start(parts, kinds, name):
    nt = len(parts)
    shapes = []
    for p, kind in zip(parts, kinds):
        shapes.append((3, p.shape[1], p.shape[2] // 2) if kind == "col" else (3,) + p.shape[1:])

    def body(*refs):
        send_sems, recv_sems = refs[2 * nt], refs[2 * nt + 1]
        src, land = refs[2 * nt + 2:3 * nt + 2], refs[3 * nt + 2:4 * nt + 2]
        token = refs[4 * nt + 2]
        for cp in _scatter_copies(src, land, kinds, send_sems, recv_sems):
            cp.start()
        token[...] = jnp.zeros_like(token)

    lands = [lax.empty(s, BF16) for s in shapes]
    return pl.pallas_call(
        body, name=name,
        out_shape=(pltpu.SemaphoreType.DMA((3 * nt,)), pltpu.SemaphoreType.DMA((3 * nt,)),
                   *[pltpu.HBM(a.shape, a.dtype) for a in parts], *[pltpu.HBM(s, BF16) for s in shapes],
                   jax.ShapeDtypeStruct((8, LANES), F32)),
        in_specs=[HBM] * (2 * nt), out_specs=(SEM, SEM, *[HBM] * (2 * nt), pl.BlockSpec(memory_space=pltpu.VMEM)),
        input_output_aliases={t: 2 + t for t in range(2 * nt)},
        compiler_params=pltpu.CompilerParams(has_side_effects=EFFECT),
    )(*[_in_hbm(a) for a in parts], *[_in_hbm(a) for a in lands])


def _chip_scatter_wait(send_sems, recv_sems, parts, lands, kinds, after, name):
    nt = len(parts)

    def body(*refs):
        src, land = refs[:nt], refs[nt:2 * nt]
        send_sems, recv_sems = refs[2 * nt], refs[2 * nt + 1]
        for cp in _scatter_copies(src, land, kinds, send_sems, recv_sems):
            cp.wait_send()
            cp.wait_recv()

    outs = pl.pallas_call(
        body, name=name, out_shape=[pltpu.HBM(a.shape, a.dtype) for a in list(parts) + list(lands)],
        in_specs=[HBM] * (2 * nt) + [SEM, SEM, ANY], out_specs=[HBM] * (2 * nt),
        input_output_aliases={t: t for t in range(2 * nt)},
        compiler_params=pltpu.CompilerParams(has_side_effects=EFFECT),
    )(*parts, *lands, send_sems, recv_sems, after)
    return outs[:nt], outs[nt:]


def _share_copies(nt):
    def make(refs, send_sems, recv_sems):
        x, y, c, _ = _place()
        return [_remote(refs[t].at[c], refs[t].at[c], send_sems.at[t], recv_sems.at[t], (x, y, 1 - c))
                for t in range(nt)]
    return make


def _sibling_share_start(fulls, name):
    return _split_start(name, list(fulls), len(fulls), _share_copies(len(fulls)))


def _sibling_share_wait(started, after, name):
    return _split_wait(name, started, _share_copies(len(started) - 3), after)


SUM_STEPS = 8


def _pair_sum(views, lands, ck):
    nt = len(views)
    in_specs, out_specs, shapes = [], [], []
    for v in views:
        b, _, r, cc = v.shape
        per = SUM_STEPS // b
        tr = r // per
        in_specs.append(pl.BlockSpec((None, None, tr, cc), lambda i, s, per=per: (i // per, s[0], i % per, 0)))
        out_specs.append(pl.BlockSpec((None, tr, cc), lambda i, s, per=per: (i // per, i % per, 0)))
        shapes.append((b, r, cc))
    in_specs = in_specs + out_specs

    def body(s_ref, *refs):
        for t in range(nt):
            refs[2 * nt + t][...] = (refs[t][...].astype(F32) + refs[nt + t][...].astype(F32)).astype(BF16)

    return pl.pallas_call(
        body, name="grad_pair_sum", out_shape=[jax.ShapeDtypeStruct(s, BF16) for s in shapes],
        grid_spec=pltpu.PrefetchScalarGridSpec(num_scalar_prefetch=1, grid=(SUM_STEPS,), in_specs=in_specs,
                                               out_specs=out_specs),
        compiler_params=_cp(dimension_semantics=("parallel",)),
    )(ck, *views, *lands)


def _chip_sum(parts, lands, kinds, ck):
    nt = len(parts)
    steps = 2
    in_own, in_land, out_specs, shapes = [], [], [], []
    for ld, kind in zip(lands, kinds):
        _, r, cc = ld.shape
        tr = r // steps
        if kind == "col":
            in_own.append(pl.BlockSpec((None, tr, cc), lambda i, s: (s[1] // 2, i, s[1] % 2)))
        else:
            in_own.append(pl.BlockSpec((None, tr, cc), lambda i, s: (s[1], i, 0)))
        in_land.append(pl.BlockSpec((3, tr, cc), lambda i, s: (0, i, 0)))
        out_specs.append(pl.BlockSpec((None, tr, cc), lambda i, s: (s[0], i, 0)))
        shapes.append((2, r, cc))

    def body(s_ref, *refs):
        for t in range(nt):
            acc = refs[t][...].astype(F32)
            for j in range(3):
                acc = acc + refs[nt + t][j].astype(F32)
            refs[2 * nt + t][...] = acc

    return pl.pallas_call(
        body, name="grad_chip_sum", out_shape=[jax.ShapeDtypeStruct(s, F32) for s in shapes],
        grid_spec=pltpu.PrefetchScalarGridSpec(num_scalar_prefetch=1, grid=(steps,), in_specs=in_own + in_land,
                                               out_specs=out_specs),
        compiler_params=_cp(dimension_semantics=("parallel",)),
    )(ck, *parts, *lands)


def _sum8(g):
    _, r, cc = g.shape
    tr = _row_tile(r, N_DEV * cc)

    def body(g_ref, o_ref):
        acc = g_ref[0].astype(F32)
        for d in range(1, N_DEV):
            acc = acc + g_ref[d].astype(F32)
        o_ref[...] = acc

    return pl.pallas_call(
        body, name="small_grad_sum", grid=(r // tr,),
        in_specs=[pl.BlockSpec((N_DEV, tr, cc), lambda i: (0, i, 0))],
        out_specs=pl.BlockSpec((tr, cc), lambda i: (i, 0)),
        out_shape=jax.ShapeDtypeStruct((r, cc), F32),
        compiler_params=_cp(dimension_semantics=("parallel",)),
    )(g)


def _silu_rows(c):
    def body(c_ref, o_ref):
        v = c_ref[...]
        o_ref[...] = v * jax.nn.sigmoid(v)

    return pl.pallas_call(body, name="cond_silu", out_shape=jax.ShapeDtypeStruct(c.shape, F32))(c)


def _pack(arrays):
    rows = []
    for a in arrays:
        flat = a.reshape(-1)
        rows.append(jnp.pad(flat, (0, (-flat.shape[0]) % (8 * LANES))).reshape(-1, LANES))
    n = sum(r.shape[0] for r in rows)
    if n % 256:
        rows.append(jnp.zeros((256 - n % 256, LANES), rows[0].dtype))
    return jnp.concatenate(rows, axis=0)


def _unpack(packed, shapes):
    out, off = [], 0
    for s in shapes:
        n = math.prod(s)
        nr = 8 * -(-n // (8 * LANES))
        out.append(packed[off:off + nr].reshape(-1)[:n].reshape(s))
        off += nr
    return out


def _as_rows(a):
    return a.reshape(1, -1) if a.ndim == 1 else a.reshape(-1, a.shape[-1])


def _adamw_many(ws, gs, ms, vs, name, steps=1):
    nt = len(ws)

    def body(*refs):
        for t in range(nt):
            w_ref, g_ref, m_ref, v_ref = (refs[k * nt + t] for k in range(4))
            d, m, v = _adamw_math(w_ref[...], g_ref[...], m_ref[...], v_ref[...])
            refs[4 * nt + t][...] = d
            refs[5 * nt + t][...] = m
            refs[6 * nt + t][...] = v

    shapes = [jax.ShapeDtypeStruct(a.shape, F32) for a in ws]
    if steps == 1:
        outs = pl.pallas_call(body, name=name, out_shape=shapes * 3, compiler_params=_cp())(*ws, *gs, *ms, *vs)
    else:
        specs = [pl.BlockSpec((a.shape[0] // steps, a.shape[1]), lambda i: (i, 0)) for a in ws]
        outs = pl.pallas_call(
            body, name=name, grid=(steps,), in_specs=specs * 4, out_specs=specs * 3, out_shape=shapes * 3,
            compiler_params=_cp(dimension_semantics=("parallel",)),
        )(*ws, *gs, *ms, *vs)
    return outs[:nt], outs[nt:2 * nt], outs[2 * nt:]


def _exchange_big_grads(grads, kinds, layer):
    views = []
    for g, kind in zip(grads, kinds):
        if kind == "col":
            views.append(g.reshape(2, 2, g.shape[1] // 2, g.shape[2]))
        else:
            views.append(g.reshape(N_CHIP, 2, g.shape[0] // (2 * N_CHIP), g.shape[1]))
    return _sibling_exchange_start(views, "grad_exchange_start_%d" % layer)


def _scatter_big_grads(exchanged, kinds, ck, after, layer):
    views, lands = _sibling_exchange_wait(exchanged, after, "grad_exchange_wait_%d" % layer)
    parts = _pair_sum(views, lands, ck)
    return _chip_scatter_start(parts, kinds, "grad_scatter_start_%d" % layer)


def _finish_big_grads(started, kinds, ck, after, layer):
    nt = len(kinds)
    send_sems, recv_sems = started[0], started[1]
    parts, lands = started[2:2 + nt], started[2 + nt:2 + 2 * nt]
    parts, lands = _chip_scatter_wait(send_sems, recv_sems, parts, lands, kinds, after, "grad_scatter_wait_%d" % layer)
    return _sibling_share_start(_chip_sum(parts, lands, kinds, ck), "grad_share_start_%d" % layer)


def _adamw_layer(ws, gs, ms, vs, stacks, layer, name, steps):
    nt = len(ws)
    stacks = [s if s is not None else tuple(lax.empty(w.shape, F32) for _ in range(4)) for s, w in zip(stacks, ws)]

    def body(*refs):
        for t in range(nt):
            w_ref, g_ref, m_ref, v_ref = (refs[k * nt + t] for k in range(4))
            outs = refs[8 * nt + 4 * t:8 * nt + 4 * t + 4]
            g = g_ref[...]
            outs[0][...] = g
            outs[1][...], outs[2][...], outs[3][...] = _adamw_math(w_ref[...], g, m_ref[...], v_ref[...])

    in_specs, g_specs, out_specs = [], [], []
    for w in ws:
        _, r, c = w.shape
        in_specs.append(pl.BlockSpec((None, r // steps, c), lambda i: (layer, i, 0)))
        g_specs.append(pl.BlockSpec((r // steps, c), lambda i: (i, 0)))
        out_specs += [pl.BlockSpec((None, r // steps, c), lambda i: (layer, i, 0))] * 4
    in_specs = in_specs + g_specs + in_specs * 2 + [ANY] * (4 * nt)
    flat = [a for s in stacks for a in s]
    outs = pl.pallas_call(
        body, name=name, grid=(steps,), in_specs=in_specs, out_specs=out_specs,
        out_shape=[jax.ShapeDtypeStruct(a.shape, F32) for a in flat],
        input_output_aliases={4 * nt + k: k for k in range(4 * nt)},
        compiler_params=_cp(dimension_semantics=("parallel",)),
    )(*ws, *gs, *ms, *vs, *flat)
    return [tuple(outs[4 * t:4 * t + 4]) for t in range(nt)]


SMALL_NAMES = ["ada_b", "norm1_g", "norm2_g", "sgu_w", "sgu_b", "pool_w", "pool_scale", "conv_w", "s5_lambda_re",
               "s5_lambda_im", "s5_b_re", "s5_b_im", "s5_c_re", "s5_c_im", "s5_d", "s5_log_dt", "s5_glu_w", "s5_glu_b",
               "mix_norm_g", "norm3_g", "final_norm_g"]
BIG_NAMES = ["ffn1_w_in", "ffn1_w_out", "w_mix_in", "w_mix_out", "ffn2_w_in", "ffn2_w_out"]
BIG_KINDS = ["col", "row", "row", "row", "col", "row"]
WEIGHT_ORDER = ["ada_w", "ada_b", "norm1_g", "ffn1_w_in", "ffn1_w_out", "norm2_g", "w_mix_in", "sgu_w", "sgu_b", "pool_w",
                "pool_scale", "conv_w", "s5_lambda_re", "s5_lambda_im", "s5_b_re", "s5_b_im", "s5_c_re", "s5_c_im", "s5_d",
                "s5_log_dt", "s5_glu_w", "s5_glu_b", "mix_norm_g", "w_mix_out", "norm3_g", "ffn2_w_in", "ffn2_w_out",
                "final_norm_g"]


def _local_step(x, target, cond, fetch_weights, prefetch_weights, p, emit_grads):
    nl, d = DEPTH, x.shape[1]
    row = lambda a: a.reshape(1, -1)
    saved = []
    for l in range(nl):
        (wi1, wo1, wmit, wmo, wi2, wo2), tok = fetch_weights(l, x)
        cl = cond[l] + tok
        mod1, mod2, mod3 = cl[0:3], cl[3:6], cl[6:9]
        lre, lim = p["s5_lambda_re"][l].reshape(-1), p["s5_lambda_im"][l].reshape(-1)
        ldt = jnp.repeat(p["s5_log_dt"][l], 64)
        rowp = jnp.stack([lre, lim, ldt])
        sp = (rowp, rowp.T, p["s5_b_re"][l].reshape(N_STATE, 16), p["s5_b_im"][l].reshape(N_STATE, 16),
              p["s5_c_re"][l].reshape(W_GRP, 64), p["s5_c_im"][l].reshape(W_GRP, 64), row(p["s5_d"][l]))
        glu = (p["s5_glu_w"][l], row(p["s5_glu_b"][l]))
        bias_full = jnp.repeat(p["sgu_b"][l].T, 64, axis=1)
        pw2 = p["pool_w"][l].reshape(W_GRP, 64)
        x1, h1, a1, b1, o1 = _ffn_fwd(x, mod1, row(p["norm1_g"][l]), wi1, wo1)
        z, h2 = _mix_in_fwd(x1, mod2, row(p["norm2_g"][l]), wmit)
        ya = _sgu_fwd(z, p["sgu_w"][l], bias_full)
        yb, yc = _poolconv_fwd(z, pw2, row(p["pool_scale"][l]), p["conv_w"][l])
        ys = (ya, yb, yc, _s5_core_fwd(z, sp))
        x2, m = _mix_out_fwd(ys, glu, row(p["mix_norm_g"][l]), wmo, x1, mod2[2:3])
        mod3 = mod3 + prefetch_weights(l + 1, x2)
        x3, h3, a3, b3, o3 = _ffn_fwd(x2, mod3, row(p["norm3_g"][l]), wi2, wo2)
        saved.append((x, x1, x2, h1, a1, b1, o1, z, h2, ys, m, h3, a3, b3, o3, sp, bias_full, pw2, glu,
                      (wi1, wo1, wmit, wmo, wi2, wo2), cl))
        x = x3

    loss, dx, dfg = _loss_head(x, row(p["final_norm_g"]), target)

    sg = {n: [None] * nl for n in SMALL_NAMES if n not in ("ada_b", "final_norm_g")}
    dcond = [None] * nl
    s5_da, s5_dk = [None] * nl, [None] * nl
    tok = 0.0
    for l in reversed(range(nl)):
        (x0, x1, x2, h1, a1, b1, o1, z, h2, ys, m, h3, a3, b3, o3, sp, bias_full, pw2, glu,
         (wi1, wo1, wmit, wmo, wi2, wo2), cl) = saved[l]
        cl = cl + tok
        mod1, mod2, mod3 = cl[0:3], cl[3:6], cl[6:9]
        dza, dzb, dwi2, dwo2, dgate3 = _ffn_bwd_main(dx, o3, mod3[2:3], h3, a3, b3, wo2)
        dx, rows3 = _ffn_bwd_in(dza, dzb, wi2, x2, dx, mod3, row(p["norm3_g"][l]))
        outs = _mix_out_bwd(dx, m, mod2[2:3], ys, glu, row(p["mix_norm_g"][l]), wmo)
        dys, dgate2, dmng, dwmo, dgw, dgb = outs[0:4], outs[4], outs[5], outs[6], outs[7], outs[8]
        dza_, dsw, dsb = _sgu_bwd(z, dys[0], p["sgu_w"][l], bias_full)
        dzb_, dzc_, dwbd, dps, dcw = _poolconv_bwd(z, dys[1], dys[2], pw2, row(p["pool_scale"][l]), p["conv_w"][l])
        dzd_, dbr, dbi, dcr, dci, dd, da, dk = _s5_core_bwd(z, dys[3], sp)
        dx, rows2, dwmit = _mix_in_bwd((dza_, dzb_, dzc_, dzd_), h2, wmit, x1, dx, mod2, row(p["norm2_g"][l]))
        dza, dzb, dwi1, dwo1, dgate1 = _ffn_bwd_main(dx, o1, mod1[2:3], h1, a1, b1, wo1)
        tok, layer_done = emit_grads(l, [dwi1, dwo1, dwmit, dwmo, dwi2, dwo2])
        dx, rows1 = _ffn_bwd_in(dza, dzb, wi1, x0, dx, mod1 + tok, row(p["norm1_g"][l]))
        if l > 0:
            tok = layer_done(dx)[0, 0]
        dcond[l] = jnp.concatenate([rows1[0:2], dgate1, rows2[0:2], dgate2, rows3[0:2], dgate3], axis=0)
        sg["norm1_g"][l], sg["norm2_g"][l], sg["norm3_g"][l] = rows1[2], rows2[2], rows3[2]
        sg["mix_norm_g"][l] = dmng[0]
        sg["sgu_w"][l] = dsw
        sg["sgu_b"][l] = dsb[:, 0:4].T
        g4 = dwbd.reshape(4, 64, 4, 64)
        sg["pool_w"][l] = jnp.stack([g4[k, :, k, :] for k in range(4)])
        sg["pool_scale"][l] = dps[0]
        sg["conv_w"][l] = dcw[0:3]
        sg["s5_b_re"][l], sg["s5_b_im"][l] = dbr.reshape(16, 64, 16), dbi.reshape(16, 64, 16)
        sg["s5_c_re"][l], sg["s5_c_im"][l] = dcr.reshape(16, 16, 64), dci.reshape(16, 16, 64)
        sg["s5_d"][l] = dd[0]
        sg["s5_glu_w"][l], sg["s5_glu_b"][l] = dgw, dgb[0]
        s5_da[l], s5_dk[l] = da, dk

    n16 = nl * 16
    dlre, dlim, dldt = _s5_param_bwd(
        p["s5_lambda_re"].reshape(n16, 64), p["s5_lambda_im"].reshape(n16, 64),
        jnp.repeat(p["s5_log_dt"].reshape(n16, 1), 64, axis=1),
        jnp.stack([a[0] for a in s5_da]).reshape(n16, 64), jnp.stack([a[1] for a in s5_da]).reshape(n16, 64),
        jnp.stack([k[:, 0] for k in s5_dk]).reshape(n16, 64), jnp.stack([k[:, 1] for k in s5_dk]).reshape(n16, 64))
    small = {n: jnp.stack(v) for n, v in sg.items() if v[0] is not None}
    small["s5_lambda_re"] = dlre.reshape(nl, 16, 64)
    small["s5_lambda_im"] = dlim.reshape(nl, 16, 64)
    small["s5_log_dt"] = dldt.reshape(nl, 16)
    small["final_norm_g"] = dfg[0]
    return loss, dx, small, jnp.stack(dcond), layer_done


def kernel(x, c, ada_w, ada_b, norm1_g, ffn1_w_in, ffn1_w_out, norm2_g, w_mix_in, sgu_w, sgu_b, pool_w, pool_scale, conv_w, s5_lambda_re, s5_lambda_im, s5_b_re, s5_b_im, s5_c_re, s5_c_im, s5_d, s5_log_dt, s5_glu_w, s5_glu_b, mix_norm_g, w_mix_out, norm3_g, ffn2_w_in, ffn2_w_out, final_norm_g, loss_target, m_ada_w, m_ada_b, m_norm1_g, m_ffn1_w_in, m_ffn1_w_out, m_norm2_g, m_w_mix_in, m_sgu_w, m_sgu_b, m_pool_w, m_pool_scale, m_conv_w, m_s5_lambda_re, m_s5_lambda_im, m_s5_b_re, m_s5_b_im, m_s5_c_re, m_s5_c_im, m_s5_d, m_s5_log_dt, m_s5_glu_w, m_s5_glu_b, m_mix_norm_g, m_w_mix_out, m_norm3_g, m_ffn2_w_in, m_ffn2_w_out, m_final_norm_g, v_ada_w, v_ada_b, v_norm1_g, v_ffn1_w_in, v_ffn1_w_out, v_norm2_g, v_w_mix_in, v_sgu_w, v_sgu_b, v_pool_w, v_pool_scale, v_conv_w, v_s5_lambda_re, v_s5_lambda_im, v_s5_b_re, v_s5_b_im, v_s5_c_re, v_s5_c_im, v_s5_d, v_s5_log_dt, v_s5_glu_w, v_s5_glu_b, v_mix_norm_g, v_w_mix_out, v_norm3_g, v_ffn2_w_in, v_ffn2_w_out, v_final_norm_g):
    args = dict(locals())
    w = {n: args[n] for n in WEIGHT_ORDER}
    mom = {n: args["m_" + n] for n in WEIGHT_ORDER}
    vel = {n: args["v_" + n] for n in WEIGHT_ORDER}
    nl, d = DEPTH, x.shape[-1]
    s = x.shape[1]
    px, py, pc = lax.axis_index("x"), lax.axis_index("y"), lax.axis_index("c")
    kme = 2 * px + py
    me = 2 * kme + pc
    kidx = jnp.reshape(kme, (1,)).astype(jnp.int32)

    shards = [ffn1_w_in, ffn1_w_out, jnp.swapaxes(w_mix_in, 1, 2), w_mix_out, ffn2_w_in, ffn2_w_out]
    started_weights = {}

    def start_weights(l, after):
        placed = _place_weights(shards, l, kidx, after)
        views = [a.reshape(a.shape[0], 2, a.shape[1] // 2, a.shape[2]) for a in placed]
        *handles, token = _weight_send_start(views, BIG_KINDS, "weight_send_start_%d" % l)
        started_weights[l] = handles
        return token

    cact = _silu_rows(c)
    pre = _pack([cact, conv_w, s5_glu_w])
    pre_all = _all_gather8(pre, "gather_prelude")
    token = start_weights(0, pre_all)
    parts = [_unpack(pre_all[dev], [cact.shape, conv_w.shape, s5_glu_w.shape]) for dev in range(N_DEV)]
    cact_all = pre_all[:, :d // LANES, :].reshape(N_DEV, d)
    conv_full = jnp.concatenate([parts[2 * k][1] for k in range(N_CHIP)], axis=2)
    glu_full = jnp.concatenate([parts[2 * k][2] for k in range(N_CHIP)], axis=1)

    n_ada = ada_w.shape[2]
    ada_b_loc = lax.dynamic_slice_in_dim(ada_b, kme * n_ada, n_ada, axis=1).reshape(nl, 1, n_ada) + token[0, 0]
    cond_part = _cond_fwd(cact_all, ada_w, ada_b_loc)
    cond_mine = lax.dynamic_update_slice(lax.empty((N_DEV, nl * N_DEV, n_ada), F32),
                                         cond_part.reshape(1, nl * N_DEV, n_ada), (me, 0, 0))
    cond_gathering = _split_start("cond_send_start", [cond_mine], 4, _gather_first_copies())
    token = cond_gathering[-1]
    for l in range(1, nl):
        token = start_weights(l, token)
    cond_arrived, = _split_wait("cond_send_wait", cond_gathering, _gather_first_copies(), token)
    cond_all = _gather_pass_on(cond_arrived, "cond_pass_on").reshape(N_DEV, nl, N_DEV, n_ada)
    cond_me = jnp.concatenate(
        [lax.dynamic_index_in_dim(cond_all[2 * k], me, axis=1, keepdims=False) for k in range(N_CHIP)], axis=1)
    cond = cond_me.reshape(nl, 9, d)

    forwarding = {}

    def prefetch_weights(l, after):
        if l >= nl:
            return 0.0
        send_sems, recv_sems, *views = started_weights.pop(l)
        views = _weight_send_wait(send_sems, recv_sems, views, BIG_KINDS, after, "weight_send_wait_%d" % l)
        forwarding[l] = _split_start("weight_forward_start_%d" % l, views, 3 * len(views), _forward_copies(BIG_KINDS))
        return forwarding[l][-1][0, 0]

    def fetch_weights(l, after):
        if l not in forwarding:
            prefetch_weights(l, after)
        views = _split_wait("weight_forward_wait_%d" % l, forwarding.pop(l), _forward_copies(BIG_KINDS), after)
        full = [v.reshape(2, 2 * v.shape[2], v.shape[3]) if kind == "col" else v.reshape(-1, v.shape[3])
                for v, kind in zip(views, BIG_KINDS)]
        return full, 0.0

    ck = jnp.stack([pc, kme]).astype(jnp.int32)
    scattering, sharing = [], []
    stacks = {n: None for n in BIG_NAMES}
    groups = ((["ffn1_w_in", "ffn2_w_in"], 16, "adamw_w_in"),
              (["ffn1_w_out", "w_mix_in", "w_mix_out", "ffn2_w_out"], 8, "adamw_w_out"))

    def as_reduced(t):
        return {n: jnp.swapaxes(t[n], 1, 2) if n == "w_mix_in" else t[n] for n in BIG_NAMES}

    w_r, m_r, v_r = as_reduced(w), as_reduced(mom), as_reduced(vel)

    def apply_adamw(l, fulls):
        g = {n: f.reshape(2 * f.shape[1], f.shape[2]) for n, f in zip(BIG_NAMES, fulls)}
        for names, steps, call in groups:
            outs = _adamw_layer([w_r[n] for n in names], [g[n] for n in names], [m_r[n] for n in names],
                                [v_r[n] for n in names], [stacks[n] for n in names], l, call, steps)
            stacks.update(zip(names, outs))

    def retire_share(after):
        l2, shared = sharing.pop(0)
        apply_adamw(l2, _sibling_share_wait(shared, after, "grad_share_wait_%d" % l2))

    def retire_scatter(after):
        l1, scattered = scattering.pop(0)
        sharing.append((l1, _finish_big_grads(scattered, BIG_KINDS, ck, after, l1)))

    def retire(after):
        if sharing:
            retire_share(after)
        if scattering:
            retire_scatter(after)

    def emit_grads(l, grads_l):
        exchanged = _exchange_big_grads(grads_l, BIG_KINDS, l)

        def layer_done(after):
            started = _scatter_big_grads(exchanged, BIG_KINDS, ck, after, l)
            retire(after)
            scattering.append((l, started))
            return started[-1]

        return exchanged[-1][0, 0], layer_done

    p = {n: w[n] for n in SMALL_NAMES}
    p["conv_w"], p["s5_glu_w"] = conv_full, glu_full
    loss, dx, small, dcond, first_layer_done = _local_step(x[0], loss_target[0], cond, fetch_weights, prefetch_weights,
                                                           p, emit_grads)

    small_order = [n for n in SMALL_NAMES if n != "ada_b"]
    packed = _pack([dcond] + [small[n] for n in small_order]).astype(BF16)
    mine = lax.dynamic_update_slice(lax.empty((N_DEV,) + packed.shape, BF16), packed[None], (me, 0, 0))
    gathering = _split_start("small_grads_send_start", [mine], 4, _gather_first_copies())
    scatter_token = first_layer_done(gathering[-1])
    while sharing:
        retire_share(scatter_token)
    arrived, = _split_wait("small_grads_send_wait", gathering, _gather_first_copies(), stacks[BIG_NAMES[0]][0])
    gathered_small = _gather_pass_on(arrived, "small_grads_pass_on")
    total = _sum8(gathered_small)
    shapes = [dcond.shape] + [small[n].shape for n in small_order]
    tot = dict(zip(["ada_b"] + small_order, _unpack(total, shapes)))
    grads = {n: tot[n] for n in SMALL_NAMES}
    grads["ada_b"] = tot["ada_b"].reshape(nl, 9 * d)
    grads["conv_w"] = lax.dynamic_slice_in_dim(tot["conv_w"], kme * conv_w.shape[2], conv_w.shape[2], axis=2)
    grads["s5_glu_w"] = lax.dynamic_slice_in_dim(tot["s5_glu_w"], kme * s5_glu_w.shape[1], s5_glu_w.shape[1], axis=1)

    dcond_all = gathered_small.reshape(N_DEV, -1)[:, :dcond.size].reshape(N_DEV, nl, 9 * d)
    dcond_loc = jnp.swapaxes(lax.dynamic_slice_in_dim(dcond_all, kme * n_ada, n_ada, axis=2), 0, 1)
    g_ada, d_ada, m_ada, v_ada = _ada_w_update(cact_all, dcond_loc, ada_w, m_ada_w, v_ada_w)

    while scattering or sharing:
        retire(g_ada)
    delta, new_m, new_v = {}, {}, {}
    for n in BIG_NAMES:
        grads[n], delta[n], new_m[n], new_v[n] = (jnp.swapaxes(a, 1, 2) if n == "w_mix_in" else a for a in stacks[n])

    grads["ada_w"], delta["ada_w"], new_m["ada_w"], new_v["ada_w"] = g_ada, d_ada, m_ada, v_ada
    wide = ("s5_b_re", "s5_b_im")
    for names, call, steps in (([n for n in SMALL_NAMES if n not in wide], "adamw_small", 1),
                               (list(wide), "adamw_s5_b", DEPTH)):
        outs = _adamw_many(*[[_as_rows(t[n]) for n in names] for t in (w, grads, mom, vel)], call, steps)
        for res, o in zip((delta, new_m, new_v), outs):
            res.update({n: a.reshape(w[n].shape) for n, a in zip(names, o)})

    loss_total = lax.psum(loss[0, 0], ("x", "y", "c"))
    return (loss_total, dx[None], *[grads[n] for n in WEIGHT_ORDER], *[delta[n] for n in WEIGHT_ORDER],
            *[new_m[n] for n in WEIGHT_ORDER], *[new_v[n] for n in WEIGHT_ORDER])
```

```python
import math

import jax
import jax.numpy as jnp
from jax import lax
from jax.experimental import pallas as pl
from jax.experimental.pallas import tpu as pltpu

F32, BF16 = jnp.float32, jnp.bfloat16
EPS = 1e-6
DEPTH = 4
N_DEV = 8
N_CHIP = 4
W_GRP = 256
CHUNK = 128
N_SEG = 8
LANES = 128
FFN_TF = 256
FFN_TF_WIDE = 1408
FFN_TM_WIDE = 512
VMEM_LIMIT = 56 * 1024 * 1024
ADAM_LR, ADAM_B1, ADAM_B2, ADAM_EPS, ADAM_WD, ADAM_STEP = 0.001, 0.9, 0.999, 1e-08, 0.01, 10
MESH_ID = pl.DeviceIdType.MESH
HI = lax.Precision.HIGHEST
ANY = pl.BlockSpec(memory_space=pl.ANY)


def _cp(**kw):
    return pltpu.CompilerParams(vmem_limit_bytes=VMEM_LIMIT, **kw)


def _dot(a, b):
    return jnp.dot(a.astype(BF16), b.astype(BF16), preferred_element_type=F32)


def _dot_nt(a, b):
    return lax.dot_general(a.astype(BF16), b.astype(BF16), (((1,), (1,)), ((), ())), preferred_element_type=F32)


def _dot_tn(a, b):
    return lax.dot_general(a.astype(BF16), b.astype(BF16), (((0,), (0,)), ((), ())), preferred_element_type=F32)


def _dot_hi(a, b):
    return jnp.dot(a, b, preferred_element_type=F32, precision=HI)


def _gelu(x):
    k = 0.7978845608028654
    t = jnp.tanh(k * (x + 0.044715 * x * x * x))
    return 0.5 * x * (1.0 + t), t


def _gelu_grad(x, t):
    k = 0.7978845608028654
    return 0.5 * (1.0 + t) + 0.5 * x * (1.0 - t * t) * k * (1.0 + 3.0 * 0.044715 * x * x)


def _iota(shape, axis):
    return lax.broadcasted_iota(jnp.int32, shape, axis)


def _full(shape):
    nd = len(shape)
    return pl.BlockSpec(shape, lambda *_: (0,) * nd)


def _norm_mod(xv, g, shift, scale):
    r = lax.rsqrt(jnp.mean(xv * xv, axis=-1, keepdims=True) + EPS)
    return (xv * r * g) * (1.0 + scale) + shift


def _norm_mod_bwd(xv, g, scale, dh):
    r = lax.rsqrt(jnp.mean(xv * xv, axis=-1, keepdims=True) + EPS)
    xh = xv * r
    n = xh * g
    dsh = jnp.sum(dh, axis=0, keepdims=True)
    dsc = jnp.sum(dh * n, axis=0, keepdims=True)
    dn = dh * (1.0 + scale)
    dg = jnp.sum(dn * xh, axis=0, keepdims=True)
    dxh = dn * g
    dx = r * (dxh - xh * jnp.mean(dxh * xh, axis=-1, keepdims=True))
    return dx, dsh, dsc, dg


def _tm(s):
    return min(s, 1024)


def _ffn_fwd(x, mod, g, wi, wo):
    s, d = x.shape
    f = wo.shape[0]
    tf, tm = FFN_TF_WIDE, min(s, FFN_TM_WIDE)
    nf, nt = f // tf, s // tm

    def body(x_ref, mod_ref, g_ref, wa_ref, wb_ref, wo_ref, xn_ref, h_ref, a_ref, b_ref, o_ref, acc):
        j = pl.program_id(1)

        @pl.when(j == 0)
        def _():
            hh = _norm_mod(x_ref[...], g_ref[...], mod_ref[0:1, :], mod_ref[1:2, :])
            h_ref[...] = hh.astype(BF16)
            acc[...] = jnp.zeros_like(acc)

        h = h_ref[...]
        a = jnp.dot(h, wa_ref[...], preferred_element_type=F32)
        b = jnp.dot(h, wb_ref[...], preferred_element_type=F32)
        a_ref[...] = a.astype(BF16)
        b_ref[...] = b.astype(BF16)
        u = (a * jax.nn.sigmoid(a)) * b
        acc[...] += jnp.dot(u.astype(BF16), wo_ref[...], preferred_element_type=F32)

        @pl.when(j == nf - 1)
        def _():
            o = acc[...]
            o_ref[...] = o.astype(BF16)
            xn_ref[...] = x_ref[...] + 0.5 * mod_ref[2:3, :] * o

    row = pl.BlockSpec((tm, d), lambda i, j: (i, 0))
    chunk = pl.BlockSpec((tm, tf), lambda i, j: (i, j))
    return pl.pallas_call(
        body, name="ffn_fwd", grid=(nt, nf),
        in_specs=[row, _full((3, d)), _full((1, d)),
                  pl.BlockSpec((None, d, tf), lambda i, j: (0, 0, j)),
                  pl.BlockSpec((None, d, tf), lambda i, j: (1, 0, j)),
                  pl.BlockSpec((tf, d), lambda i, j: (j, 0))],
        out_specs=[row, row, chunk, chunk, row],
        out_shape=[jax.ShapeDtypeStruct((s, d), F32), jax.ShapeDtypeStruct((s, d), BF16),
                   jax.ShapeDtypeStruct((s, f), BF16), jax.ShapeDtypeStruct((s, f), BF16),
                   jax.ShapeDtypeStruct((s, d), BF16)],
        scratch_shapes=[pltpu.VMEM((tm, d), F32)],
        compiler_params=_cp(dimension_semantics=("parallel", "arbitrary")),
    )(x, mod, g, wi, wi, wo)


def _ffn_bwd_main(dxo, o, gate, h, a, b, wo):
    s, d = dxo.shape
    f = wo.shape[0]
    tf = FFN_TF
    nf = f // tf

    def body(dxo_ref, o_hbm, gate_ref, h_hbm, a_ref, b_ref, wo_ref, dza_ref, dzb_ref, dwi_ref, dwo_ref, dg_ref,
             do_s, o_s, h_s, sems):
        j = pl.program_id(0)
        o_copy = pltpu.make_async_copy(o_hbm, o_s, sems.at[0])
        h_copy = pltpu.make_async_copy(h_hbm, h_s, sems.at[1])

        @pl.when(j == 0)
        def _():
            h_copy.start()
            o_copy.start()
            do_s[...] = (0.5 * gate_ref[...] * dxo_ref[...]).astype(BF16)

        dov = do_s[...]
        du = lax.dot_general(dov, wo_ref[...], (((1,), (1,)), ((), ())), preferred_element_type=F32)

        @pl.when(j == 0)
        def _():
            h_copy.wait()

        @pl.when(j == nf - 1)
        def _():
            o_copy.wait()
            dg_ref[...] = 0.5 * jnp.sum(o_s[...].astype(F32) * dxo_ref[...], axis=0, keepdims=True)

        hv = h_s[...]
        av = a_ref[...].astype(F32)
        bv = b_ref[...].astype(F32)
        sa = jax.nn.sigmoid(av)
        si = av * sa
        u = (si * bv).astype(BF16)
        da = (du * bv * (sa * (1.0 + av * (1.0 - sa)))).astype(BF16)
        db = (du * si).astype(BF16)
        dza_ref[...] = da
        dzb_ref[...] = db
        dwo_ref[...] = _dot_tn(u, dov).astype(BF16)
        dwi_ref[0] = _dot_tn(hv, da).astype(BF16)
        dwi_ref[1] = _dot_tn(hv, db).astype(BF16)

    chunk = pl.BlockSpec((s, tf), lambda j: (0, j))
    once = lambda: pl.BlockSpec((s, d), lambda j: (0, 0), pipeline_mode=pl.Buffered(1))
    return pl.pallas_call(
        body, name="ffn_bwd_main", grid=(nf,),
        in_specs=[once(), ANY, _full((1, d)), ANY, chunk, chunk, pl.BlockSpec((tf, d), lambda j: (j, 0))],
        out_specs=[chunk, chunk, pl.BlockSpec((2, d, tf), lambda j: (0, 0, j)),
                   pl.BlockSpec((tf, d), lambda j: (j, 0)), _full((1, d))],
        out_shape=[jax.ShapeDtypeStruct((s, f), BF16), jax.ShapeDtypeStruct((s, f), BF16),
                   jax.ShapeDtypeStruct((2, d, f), BF16), jax.ShapeDtypeStruct((f, d), BF16),
                   jax.ShapeDtypeStruct((1, d), F32)],
        scratch_shapes=[pltpu.VMEM((s, d), BF16), pltpu.VMEM((s, d), BF16), pltpu.VMEM((s, d), BF16),
                        pltpu.SemaphoreType.DMA((2,))],
        compiler_params=_cp(dimension_semantics=("arbitrary",)),
    )(dxo, o, gate, h, a, b, wo)


def _ffn_bwd_in(dza, dzb, wi, x, dxo, mod, g):
    s, d = x.shape
    f = dza.shape[1]
    tf, tm = FFN_TF_WIDE, min(s, FFN_TM_WIDE)
    nf, nt = f // tf, s // tm

    def body(dza_ref, dzb_ref, wa_ref, wb_ref, x_ref, dxo_ref, mod_ref, g_ref, dx_ref, rows_ref, acc):
        j, i = pl.program_id(0), pl.program_id(1)
        rows = pl.ds(pl.multiple_of(i * tm, tm), tm)

        @pl.when(jnp.logical_and(i == 0, j == 0))
        def _():
            rows_ref[...] = jnp.zeros_like(rows_ref)

        part = (lax.dot_general(dza_ref[...], wa_ref[...], (((1,), (1,)), ((), ())), preferred_element_type=F32)
                + lax.dot_general(dzb_ref[...], wb_ref[...], (((1,), (1,)), ((), ())), preferred_element_type=F32))

        @pl.when(j == 0)
        def _():
            acc[rows, :] = part

        @pl.when(jnp.logical_and(j > 0, j < nf - 1))
        def _():
            acc[rows, :] += part

        @pl.when(j == nf - 1)
        def _():
            dh = part + acc[rows, :] if nf > 1 else part
            dx, dsh, dsc, dg = _norm_mod_bwd(x_ref[...], g_ref[...], mod_ref[1:2, :], dh)
            dx_ref[...] = dx + dxo_ref[...]
            rows_ref[0:1, :] += dsh
            rows_ref[1:2, :] += dsc
            rows_ref[2:3, :] += dg

    late = pl.BlockSpec((tm, d), lambda j, i: (jnp.where(j == nf - 1, i, 0), 0))
    chunk = pl.BlockSpec((tm, tf), lambda j, i: (i, j))
    return pl.pallas_call(
        body, name="ffn_bwd_in", grid=(nf, nt),
        in_specs=[chunk, chunk,
                  pl.BlockSpec((None, d, tf), lambda j, i: (0, 0, j)),
                  pl.BlockSpec((None, d, tf), lambda j, i: (1, 0, j)),
                  late, late, _full((3, d)), _full((1, d))],
        out_specs=[late, _full((8, d))],
        out_shape=[jax.ShapeDtypeStruct((s, d), F32), jax.ShapeDtypeStruct((8, d), F32)],
        scratch_shapes=[pltpu.VMEM((s, d), F32)],
        compiler_params=_cp(dimension_semantics=("arbitrary", "arbitrary")),
    )(dza, dzb, wi, wi, x, dxo, mod, g)


def _mix_in_fwd(x, mod, g, wmit):
    s, d = x.shape
    p = wmit.shape[0]
    tm = _tm(s)

    def body(x_ref, mod_ref, g_ref, w_ref, z_ref, h_ref):
        hh = _norm_mod(x_ref[...], g_ref[...], mod_ref[0:1, :], mod_ref[1:2, :]).astype(BF16)
        h_ref[...] = hh
        z_ref[...] = lax.dot_general(hh, w_ref[...], (((1,), (1,)), ((), ())), preferred_element_type=F32)

    row = pl.BlockSpec((tm, d), lambda i: (i, 0))
    return pl.pallas_call(
        body, name="mix_in_fwd", grid=(s // tm,),
        in_specs=[row, _full((3, d)), _full((1, d)), _full((p, d))],
        out_specs=[pl.BlockSpec((tm, p), lambda i: (i, 0)), row],
        out_shape=[jax.ShapeDtypeStruct((s, p), F32), jax.ShapeDtypeStruct((s, d), BF16)],
        compiler_params=_cp(dimension_semantics=("parallel",)),
    )(x, mod, g, wmit)


def _mix_in_bwd(dzs, h, wmit, x, dxo, mod, g):
    s, d = x.shape
    p = wmit.shape[0]
    tm = min(s, 512)
    nt = s // tm

    def body(za_ref, zb_ref, zc_ref, zd_ref, h_ref, w_ref, x_ref, dxo_ref, mod_ref, g_ref,
             dx_ref, rows_ref, dw_ref, acc):
        i = pl.program_id(0)

        @pl.when(i == 0)
        def _():
            rows_ref[...] = jnp.zeros_like(rows_ref)
            acc[...] = jnp.zeros_like(acc)

        dz = jnp.concatenate([za_ref[...], zb_ref[...], zc_ref[...], zd_ref[...]], axis=1).astype(BF16)
        acc[...] += _dot_tn(dz, h_ref[...])
        dh = jnp.dot(dz, w_ref[...], preferred_element_type=F32)
        dx, dsh, dsc, dg = _norm_mod_bwd(x_ref[...], g_ref[...], mod_ref[1:2, :], dh)
        dx_ref[...] = dx + dxo_ref[...]
        rows_ref[0:1, :] += dsh
        rows_ref[1:2, :] += dsc
        rows_ref[2:3, :] += dg

        @pl.when(i == nt - 1)
        def _():
            dw_ref[...] = acc[...].astype(BF16)

    row = pl.BlockSpec((tm, d), lambda i: (i, 0))
    zspecs = [pl.BlockSpec((tm, z.shape[1]), lambda i: (i, 0)) for z in dzs]
    return pl.pallas_call(
        body, name="mix_in_bwd", grid=(nt,),
        in_specs=zspecs + [row, _full((p, d)), row, row, _full((3, d)), _full((1, d))],
        out_specs=[row, _full((8, d)), _full((p, d))],
        out_shape=[jax.ShapeDtypeStruct((s, d), F32), jax.ShapeDtypeStruct((8, d), F32),
                   jax.ShapeDtypeStruct((p, d), BF16)],
        scratch_shapes=[pltpu.VMEM((p, d), F32)],
        compiler_params=_cp(dimension_semantics=("arbitrary",)),
    )(*dzs, h, wmit, x, dxo, mod, g)


def _group_norm(ys, mng):
    outs, hats, rs = [], [], []
    for k, y in enumerate(ys):
        r = lax.rsqrt(jnp.mean(y * y, axis=-1, keepdims=True) + EPS)
        yh = y * r
        hats.append(yh)
        rs.append(r)
        outs.append(yh * mng[:, k * W_GRP:(k + 1) * W_GRP])
    return jnp.concatenate(outs, axis=1), hats, rs


def _s5_glu(y, gw, gb):
    yg, t = _gelu(y)
    gate = jax.nn.sigmoid(_dot(yg, gw) + gb)
    return yg * gate, yg, t, gate


def _mix_out_fwd(ys, glu, mng, wmo, x, gate):
    s, d = x.shape
    tm = _tm(s)

    def body(ya, yb, yc, ypre, gw_ref, gb_ref, mng_ref, w_ref, x_ref, gate_ref, xn_ref, m_ref):
        yd = _s5_glu(ypre[...], gw_ref[...], gb_ref[...])[0]
        yn, _, _ = _group_norm([ya[...], yb[...], yc[...], yd], mng_ref[...])
        m = jnp.dot(yn.astype(BF16), w_ref[...], preferred_element_type=F32)
        m_ref[...] = m
        xn_ref[...] = x_ref[...] + gate_ref[...] * m

    row = pl.BlockSpec((tm, d), lambda i: (i, 0))
    grp = pl.BlockSpec((tm, W_GRP), lambda i: (i, 0))
    return pl.pallas_call(
        body, name="mix_out_fwd", grid=(s // tm,),
        in_specs=[grp, grp, grp, grp, _full((W_GRP, W_GRP)), _full((1, W_GRP)), _full((1, d)), _full((d, d)), row,
                  _full((1, d))],
        out_specs=[row, row],
        out_shape=[jax.ShapeDtypeStruct((s, d), F32), jax.ShapeDtypeStruct((s, d), F32)],
        compiler_params=_cp(dimension_semantics=("parallel",)),
    )(*ys, *glu, mng, wmo, x, gate)


def _mix_out_bwd(dxo, m, gate, ys, glu, mng, wmo):
    s, d = dxo.shape
    tm = min(s, 512)
    nt = s // tm

    def body(dxo_ref, m_ref, gate_ref, ya, yb, yc, ypre, gw_ref, gb_ref, mng_ref, w_ref,
             dya, dyb, dyc, dypre, dgate_ref, dmng_ref, dw_ref, dgw_ref, dgb_ref, acc):
        i = pl.program_id(0)

        @pl.when(i == 0)
        def _():
            dgate_ref[...] = jnp.zeros_like(dgate_ref)
            dmng_ref[...] = jnp.zeros_like(dmng_ref)
            dgw_ref[...] = jnp.zeros_like(dgw_ref)
            dgb_ref[...] = jnp.zeros_like(dgb_ref)
            acc[...] = jnp.zeros_like(acc)

        dxv = dxo_ref[...]
        dgate_ref[...] += jnp.sum(m_ref[...] * dxv, axis=0, keepdims=True)
        dm = (gate_ref[...] * dxv).astype(BF16)
        mng = mng_ref[...]
        gw = gw_ref[...]
        yp = ypre[...]
        yd, yg, t, glu_gate = _s5_glu(yp, gw, gb_ref[...])
        yn, hats, rs = _group_norm([ya[...], yb[...], yc[...], yd], mng)
        acc[...] += _dot_tn(yn, dm)
        dyn = lax.dot_general(dm, w_ref[...], (((1,), (1,)), ((), ())), preferred_element_type=F32)
        dmng_parts, dys = [], []
        for k, (yh, r) in enumerate(zip(hats, rs)):
            dk = dyn[:, k * W_GRP:(k + 1) * W_GRP]
            dmng_parts.append(jnp.sum(dk * yh, axis=0, keepdims=True))
            dyh = dk * mng[:, k * W_GRP:(k + 1) * W_GRP]
            dys.append(r * (dyh - yh * jnp.mean(dyh * yh, axis=-1, keepdims=True)))
        dmng_ref[...] += jnp.concatenate(dmng_parts, axis=1)
        dya[...], dyb[...], dyc[...] = dys[0], dys[1], dys[2]
        dyd = dys[3]
        dlin = dyd * yg * glu_gate * (1.0 - glu_gate)
        dgw_ref[...] += _dot_tn(yg, dlin)
        dgb_ref[...] += jnp.sum(dlin, axis=0, keepdims=True)
        dypre[...] = (dyd * glu_gate + _dot_nt(dlin, gw)) * _gelu_grad(yp, t)

        @pl.when(i == nt - 1)
        def _():
            dw_ref[...] = acc[...].astype(BF16)

    row = pl.BlockSpec((tm, d), lambda i: (i, 0))
    grp = pl.BlockSpec((tm, W_GRP), lambda i: (i, 0))
    return pl.pallas_call(
        body, name="mix_out_bwd", grid=(nt,),
        in_specs=[row, row, _full((1, d)), grp, grp, grp, grp, _full((W_GRP, W_GRP)), _full((1, W_GRP)), _full((1, d)),
                  _full((d, d))],
        out_specs=[grp, grp, grp, grp, _full((1, d)), _full((1, d)), _full((d, d)), _full((W_GRP, W_GRP)),
                   _full((1, W_GRP))],
        out_shape=[jax.ShapeDtypeStruct((s, W_GRP), F32)] * 4
        + [jax.ShapeDtypeStruct((1, d), F32), jax.ShapeDtypeStruct((1, d), F32), jax.ShapeDtypeStruct((d, d), BF16),
           jax.ShapeDtypeStruct((W_GRP, W_GRP), F32), jax.ShapeDtypeStruct((1, W_GRP), F32)],
        scratch_shapes=[pltpu.VMEM((d, d), F32)],
        compiler_params=_cp(dimension_semantics=("arbitrary",)),
    )(dxo, m, gate, *ys, *glu, mng, wmo)


def _sgu_consts():
    r = _iota((W_GRP, W_GRP), 0) >> 6
    c = _iota((W_GRP, W_GRP), 1) >> 6
    avg = jnp.where(r == c, 1.0 / 64.0, 0.0).astype(F32)
    tril = _iota((CHUNK, CHUNK), 0) >= _iota((CHUNK, CHUNK), 1)
    head = _iota((CHUNK, W_GRP), 1) >> 6
    return avg, tril, head


def _sgu_pre(za, avg):
    zg, t = _gelu(za)
    u, v = zg[:, :W_GRP], zg[:, W_GRP:]
    mu = _dot_hi(v, avg)
    vc = v - mu
    r = lax.rsqrt(_dot_hi(vc * vc, avg) + EPS)
    return t, u, vc * r, r


def _sgu_fwd(z, sgu_w, bias_full):
    s = z.shape[0]
    tm = min(s, 512)

    def body(za_ref, w_ref, bias_ref, ya_ref):
        avg, tril, head = _sgu_consts()
        _, u, vn, _ = _sgu_pre(za_ref[...], avg)
        wm = [jnp.where(tril, w_ref[h], 0.0).astype(BF16) for h in range(4)]
        vb = vn.astype(BF16)
        for n in range(tm // CHUNK):
            rows = slice(n * CHUNK, (n + 1) * CHUNK)
            mixed = bias_ref[...]
            for h in range(4):
                mixed = mixed + jnp.where(head == h, jnp.dot(wm[h], vb[rows], preferred_element_type=F32), 0.0)
            ya_ref[rows, :] = u[rows] * mixed

    return pl.pallas_call(
        body, name="sgu_fwd", grid=(s // tm,),
        in_specs=[pl.BlockSpec((tm, 2 * W_GRP), lambda i: (i, 0)), _full((4, CHUNK, CHUNK)), _full((CHUNK, W_GRP))],
        out_specs=pl.BlockSpec((tm, W_GRP), lambda i: (i, 0)),
        out_shape=jax.ShapeDtypeStruct((s, W_GRP), F32),
        compiler_params=_cp(dimension_semantics=("parallel",)),
    )(z, sgu_w, bias_full)


def _sgu_bwd(z, dya, sgu_w, bias_full):
    s = z.shape[0]
    tm = min(s, 512)
    nt = s // tm

    def body(za_ref, dya_ref, w_ref, bias_ref, dza_ref, dw_ref, db_ref, du_s, dvn_s):
        i = pl.program_id(0)

        @pl.when(i == 0)
        def _():
            dw_ref[...] = jnp.zeros_like(dw_ref)
            db_ref[...] = jnp.zeros_like(db_ref)

        avg, tril, head = _sgu_consts()
        za = za_ref[...]
        t, u, vn, r = _sgu_pre(za, avg)
        wm = [jnp.where(tril, w_ref[h], 0.0).astype(BF16) for h in range(4)]
        vb = vn.astype(BF16)
        dya = dya_ref[...]
        dw = [jnp.zeros((CHUNK, CHUNK), F32) for _ in range(4)]
        db = jnp.zeros((CHUNK, W_GRP), F32)
        for n in range(tm // CHUNK):
            rows = slice(n * CHUNK, (n + 1) * CHUNK)
            mixed = bias_ref[...]
            for h in range(4):
                mixed = mixed + jnp.where(head == h, jnp.dot(wm[h], vb[rows], preferred_element_type=F32), 0.0)
            dmix = dya[rows] * u[rows]
            du_s[rows, :] = dya[rows] * mixed
            db = db + dmix
            dmb = dmix.astype(BF16)
            dvn = jnp.zeros((CHUNK, W_GRP), F32)
            for h in range(4):
                dmh = jnp.where(head == h, dmix, 0.0)
                dw[h] = dw[h] + _dot_nt(dmh, vb[rows])
                dvn = dvn + jnp.where(head == h, _dot_tn(wm[h], dmb), 0.0)
            dvn_s[rows, :] = dvn
        for h in range(4):
            dw_ref[h] += jnp.where(tril, dw[h], 0.0)
        sel = ((_iota((W_GRP, CHUNK), 0) >> 6) == _iota((W_GRP, CHUNK), 1)).astype(F32)
        db_ref[...] += _dot_hi(db, sel)
        dvn = dvn_s[...]
        dv = r * (dvn - _dot_hi(dvn, avg) - vn * _dot_hi(dvn * vn, avg))
        dzg = jnp.concatenate([du_s[...], dv], axis=1)
        dza_ref[...] = dzg * _gelu_grad(za, t)

    return pl.pallas_call(
        body, name="sgu_bwd", grid=(nt,),
        in_specs=[pl.BlockSpec((tm, 2 * W_GRP), lambda i: (i, 0)), pl.BlockSpec((tm, W_GRP), lambda i: (i, 0)),
                  _full((4, CHUNK, CHUNK)), _full((CHUNK, W_GRP))],
        out_specs=[pl.BlockSpec((tm, 2 * W_GRP), lambda i: (i, 0)), _full((4, CHUNK, CHUNK)), _full((CHUNK, CHUNK))],
        out_shape=[jax.ShapeDtypeStruct((s, 2 * W_GRP), F32), jax.ShapeDtypeStruct((4, CHUNK, CHUNK), F32),
                   jax.ShapeDtypeStruct((CHUNK, CHUNK), F32)],
        scratch_shapes=[pltpu.VMEM((tm, W_GRP), F32), pltpu.VMEM((tm, W_GRP), F32)],
        compiler_params=_cp(dimension_semantics=("arbitrary",)),
    )(z, dya, sgu_w, bias_full)


def _shift_down(x, k):
    return jnp.where(_iota(x.shape, 0) < k, 0.0, pltpu.roll(x, k, 0))


def _shift_up(x, k):
    n = x.shape[0]
    return jnp.where(_iota(x.shape, 0) >= n - k, 0.0, pltpu.roll(x, n - k, 0))


def _by_pool_group(shape, v2, v4, v8, v16):
    col = _iota(shape, 1)
    return jnp.where(col < 64, v2, jnp.where(col < 128, v4, jnp.where(col < 192, v8, v16)))


def _pool_core(zb, pw2):
    s2 = zb + _shift_down(zb, 1)
    s4 = s2 + _shift_down(s2, 2)
    s8 = s4 + _shift_down(s4, 4)
    s16 = s8 + _shift_down(s8, 8)
    win = _by_pool_group(zb.shape, s2, s4, s8, s16)
    wlen = _by_pool_group(zb.shape, 2.0, 4.0, 8.0, 16.0)
    cnt = jnp.minimum((_iota(zb.shape, 0) + 1).astype(F32), wlen)
    p = win / cnt - zb
    wt = jnp.tile(pw2, (1, 4))
    wbd = jnp.where((_iota(wt.shape, 0) >> 6) == (_iota(wt.shape, 1) >> 6), wt, 0.0).astype(BF16)
    return p, cnt, wbd


def _conv_core(zc, cw):
    bg, cg, xh = zc[:, :W_GRP], zc[:, W_GRP:2 * W_GRP], zc[:, 2 * W_GRP:]
    y = cg * xh
    y1, y2 = _shift_down(y, 1), _shift_down(y, 2)
    out = cw[2:3, :] * y + cw[1:2, :] * y1 + cw[0:1, :] * y2
    return bg, cg, xh, y, y1, y2, out


def _poolconv_fwd(z, pw2, pscale, cw):
    s = z.shape[0]

    def body(zb_ref, zc_ref, pw_ref, ps_ref, cw_ref, yb_ref, yc_ref):
        p, _, wbd = _pool_core(zb_ref[...], pw_ref[...])
        yb_ref[...] = jnp.dot(p.astype(BF16), wbd, preferred_element_type=F32) * ps_ref[...]
        bg, _, _, _, _, _, out = _conv_core(zc_ref[...], cw_ref[...])
        yc_ref[...] = bg * out

    return pl.pallas_call(
        body, name="poolconv_fwd", grid=(1,),
        in_specs=[pl.BlockSpec((s, W_GRP), lambda i: (0, 2)), pl.BlockSpec((s, 3 * W_GRP), lambda i: (0, 1)),
                  _full((W_GRP, 64)), _full((1, W_GRP)), _full((3, W_GRP))],
        out_specs=[_full((s, W_GRP)), _full((s, W_GRP))],
        out_shape=[jax.ShapeDtypeStruct((s, W_GRP), F32)] * 2,
        compiler_params=_cp(dimension_semantics=("arbitrary",)),
    )(z, z, pw2, pscale, cw)


def _poolconv_bwd(z, dyb, dyc, pw2, pscale, cw):
    s = z.shape[0]

    def body(zb_ref, zc_ref, dyb_ref, dyc_ref, pw_ref, ps_ref, cw_ref, dzb_ref, dzc_ref, dw_ref, dps_ref, dcw_ref):
        zb = zb_ref[...]
        p, cnt, wbd = _pool_core(zb, pw_ref[...])
        pb = p.astype(BF16)
        out = jnp.dot(pb, wbd, preferred_element_type=F32)
        dyb = dyb_ref[...]
        dps_ref[...] = jnp.sum(dyb * out, axis=0, keepdims=True)
        dout = (dyb * ps_ref[...]).astype(BF16)
        dw = _dot_tn(pb, dout)
        dw_ref[...] = jnp.where((_iota(dw.shape, 0) >> 6) == (_iota(dw.shape, 1) >> 6), dw, 0.0)
        dp = lax.dot_general(dout, wbd, (((1,), (1,)), ((), ())), preferred_element_type=F32)
        dwin = dp / cnt
        t2 = dwin + _shift_up(dwin, 1)
        t4 = t2 + _shift_up(t2, 2)
        t8 = t4 + _shift_up(t4, 4)
        t16 = t8 + _shift_up(t8, 8)
        dzb_ref[...] = _by_pool_group(zb.shape, t2, t4, t8, t16) - dp

        cw = cw_ref[...]
        bg, cg, xh, y, y1, y2, out = _conv_core(zc_ref[...], cw)
        dyc = dyc_ref[...]
        dout = dyc * bg
        dcw_ref[...] = jnp.zeros_like(dcw_ref)
        dcw_ref[0:1, :] = jnp.sum(dout * y2, axis=0, keepdims=True)
        dcw_ref[1:2, :] = jnp.sum(dout * y1, axis=0, keepdims=True)
        dcw_ref[2:3, :] = jnp.sum(dout * y, axis=0, keepdims=True)
        dy = cw[2:3, :] * dout + cw[1:2, :] * _shift_up(dout, 1) + cw[0:1, :] * _shift_up(dout, 2)
        dzc_ref[...] = jnp.concatenate([dyc * out, dy * xh, dy * cg], axis=1)

    return pl.pallas_call(
        body, name="poolconv_bwd", grid=(1,),
        in_specs=[pl.BlockSpec((s, W_GRP), lambda i: (0, 2)), pl.BlockSpec((s, 3 * W_GRP), lambda i: (0, 1)),
                  _full((s, W_GRP)), _full((s, W_GRP)), _full((W_GRP, 64)), _full((1, W_GRP)), _full((3, W_GRP))],
        out_specs=[_full((s, W_GRP)), _full((s, 3 * W_GRP)), _full((W_GRP, W_GRP)), _full((1, W_GRP)), _full((8, W_GRP))],
        out_shape=[jax.ShapeDtypeStruct((s, W_GRP), F32), jax.ShapeDtypeStruct((s, 3 * W_GRP), F32),
                   jax.ShapeDtypeStruct((W_GRP, W_GRP), F32), jax.ShapeDtypeStruct((1, W_GRP), F32),
                   jax.ShapeDtypeStruct((8, W_GRP), F32)],
        compiler_params=_cp(dimension_semantics=("arbitrary",)),
    )(z, z, dyb, dyc, pw2, pscale, cw)


N_STATE = 1024
HALF_STATE = N_STATE // 2
HALF_CH = W_GRP // 2
N_SLAB = HALF_STATE // LANES


def _s5_disc(lre, lim, ldt):
    dt = jnp.exp(ldt)
    mag = jnp.exp(lre * dt)
    ang = lim * dt
    ar, ai = mag * jnp.cos(ang), mag * jnp.sin(ang)
    nr, ni = ar - 1.0, ai
    den = lre * lre + lim * lim
    kr = (nr * lre + ni * lim) / den
    ki = (ni * lre - nr * lim) / den
    return ar, ai, kr, ki


def _s5_mats(colp, br, bi, cr, ci):
    _, _, kr, ki = _s5_disc(colp[:, 0:1], colp[:, 1:2], colp[:, 2:3])
    bbr = kr * br - ki * bi
    bbi = kr * bi + ki * br
    bmask = (_iota((HALF_STATE, HALF_CH), 0) >> 6) == (_iota((HALF_STATE, HALF_CH), 1) >> 4)
    cmask = (_iota((HALF_CH, HALF_STATE), 0) >> 4) == (_iota((HALF_CH, HALF_STATE), 1) >> 6)
    btr = jnp.where(bmask, jnp.tile(bbr, (1, 8)), 0.0).astype(BF16)
    bti = jnp.where(bmask, jnp.tile(bbi, (1, 8)), 0.0).astype(BF16)
    ctr = jnp.where(cmask, jnp.tile(cr, (1, 8)), 0.0).astype(BF16)
    cti = jnp.where(cmask, jnp.tile(ci, (1, 8)), 0.0).astype(BF16)
    return kr, ki, btr, bti, ctr, cti, bmask, cmask


def _slab(q):
    return slice(q * LANES, (q + 1) * LANES)


def _cmul(ar, ai, br, bi):
    return ar * br - ai * bi, ar * bi + ai * br


def _sub_shift(x, k, up):
    row = _iota(x.shape, 0)
    if up:
        return jnp.where(row >= N_SEG - k, 0.0, pltpu.roll(x, N_SEG - k, 0))
    return jnp.where(row < k, 0.0, pltpu.roll(x, k, 0))


def _seg_rows(j):
    return pl.ds(pl.multiple_of(j * N_SEG, N_SEG), N_SEG)


def _interleave(src, dst, seg):
    def step(j, carry):
        dst[_seg_rows(j), :] = src[pl.ds(j, N_SEG, stride=seg), :]
        return carry
    lax.fori_loop(0, seg, step, 0)


def _deinterleave(src, dst, seg):
    def step(j, carry):
        dst[pl.ds(j, N_SEG, stride=seg), :] = src[_seg_rows(j), :]
        return carry
    lax.fori_loop(0, seg, step, 0)


def _scan(xr, xi, ar_row, ai_row, seg, reverse, states=None):
    nlog = int(math.log2(seg))
    assert (1 << nlog) == seg
    grads = []
    for q0 in range(0, N_SLAB, 4):
        qs = list(range(q0, q0 + 4))
        aq = [(jnp.broadcast_to(ar_row[:, _slab(q)], (N_SEG, LANES)),
               jnp.broadcast_to(ai_row[:, _slab(q)], (N_SEG, LANES))) for q in qs]
        zero = jnp.zeros((N_SEG, LANES), F32)

        def local(jj, carry, qs=qs, aq=aq):
            j = seg - 1 - jj if reverse else jj
            out = []
            for n, q in enumerate(qs):
                rows = _seg_rows(j)
                pr, pi = _cmul(aq[n][0], aq[n][1], carry[2 * n], carry[2 * n + 1])
                nr = pr + xr[q, rows, :]
                ni = pi + xi[q, rows, :]
                xr[q, rows, :] = nr
                xi[q, rows, :] = ni
                out += [nr, ni]
            return tuple(out)

        fin = lax.fori_loop(0, seg, local, (zero,) * 8)
        cins = []
        for n in range(4):
            er, ei = fin[2 * n], fin[2 * n + 1]
            pr, pi = aq[n]
            for _ in range(nlog):
                pr, pi = _cmul(pr, pi, pr, pi)
            yr, yi = er, ei
            for k in (1, 2, 4):
                sr, si = _cmul(pr, pi, _sub_shift(yr, k, reverse), _sub_shift(yi, k, reverse))
                yr, yi = yr + sr, yi + si
                pr, pi = _cmul(pr, pi, pr, pi)
            cins.append((_sub_shift(yr, 1, reverse), _sub_shift(yi, 1, reverse)))

        def fix(jj, carry, qs=qs, aq=aq, cins=cins):
            j = seg - 1 - jj if reverse else jj
            out, sums = [], []
            for n, q in enumerate(qs):
                rows = _seg_rows(j)
                pwr, pwi = carry[2 * n], carry[2 * n + 1]
                cr, ci = _cmul(pwr, pwi, cins[n][0], cins[n][1])
                v_r, v_i = xr[q, rows, :] + cr, xi[q, rows, :] + ci
                xr[q, rows, :] = v_r
                xi[q, rows, :] = v_i
                nr, ni = _cmul(pwr, pwi, aq[n][0], aq[n][1])
                out += [nr, ni]
                if states is not None:
                    prev = _seg_rows(j - 1)
                    p_r, p_i = states[0][q, prev, :], states[1][q, prev, :]
                    sums += [carry[8 + 2 * n] + v_r * p_r + v_i * p_i, carry[9 + 2 * n] - v_r * p_i + v_i * p_r]
            return tuple(out + sums)

        powers = tuple(v for pair in aq for v in pair)
        if states is None:
            lax.fori_loop(0, seg, fix, powers)
            continue
        assert reverse
        fix_last = lax.fori_loop(0, seg - 1, fix, powers + (zero,) * 8)
        first = _seg_rows(0)
        for n, q in enumerate(qs):
            cr, ci = _cmul(fix_last[2 * n], fix_last[2 * n + 1], cins[n][0], cins[n][1])
            v_r, v_i = xr[q, first, :] + cr, xi[q, first, :] + ci
            xr[q, first, :] = v_r
            xi[q, first, :] = v_i
            p_r = _sub_shift(states[0][q, _seg_rows(seg - 1), :], 1, False)
            p_i = _sub_shift(states[1][q, _seg_rows(seg - 1), :], 1, False)
            grads.append((jnp.sum(fix_last[8 + 2 * n] + v_r * p_r + v_i * p_i, axis=0, keepdims=True),
                          jnp.sum(fix_last[9 + 2 * n] - v_r * p_i + v_i * p_r, axis=0, keepdims=True)))
    return grads


def _s5_forward_states(u, btr, bti, ar_row, ai_row, xr, xi, seg):
    ub = u.astype(BF16)
    for q in range(N_SLAB):
        xr[q] = _dot_nt(ub, btr[_slab(q), :])
        xi[q] = _dot_nt(ub, bti[_slab(q), :])
    _scan(xr, xi, ar_row, ai_row, seg, False)


def _s5_readout(u, xr, xi, ctr, cti, d):
    y = d * u
    for q in range(N_SLAB):
        y = y + _dot_nt(xr[q], ctr[:, _slab(q)]) - _dot_nt(xi[q], cti[:, _slab(q)])
    return y


def _s5_param_specs():
    return [pl.BlockSpec((3, HALF_STATE), lambda i: (0, i)), pl.BlockSpec((HALF_STATE, 3), lambda i: (i, 0)),
            pl.BlockSpec((HALF_STATE, 16), lambda i: (i, 0)), pl.BlockSpec((HALF_STATE, 16), lambda i: (i, 0)),
            pl.BlockSpec((HALF_CH, 64), lambda i: (i, 0)), pl.BlockSpec((HALF_CH, 64), lambda i: (i, 0)),
            pl.BlockSpec((1, HALF_CH), lambda i: (0, i))]


def _s5_core_fwd(z, sp):
    s = z.shape[0]
    seg = s // N_SEG

    def body(u_ref, rowp, colp, br, bi, cr, ci, d_ref, y_ref, xr, xi, us, ys):
        ar, ai, _, _ = _s5_disc(rowp[0:1, :], rowp[1:2, :], rowp[2:3, :])
        _, _, btr, bti, ctr, cti, _, _ = _s5_mats(colp[...], br[...], bi[...], cr[...], ci[...])
        _interleave(u_ref, us, seg)
        u = us[...]
        _s5_forward_states(u, btr, bti, ar, ai, xr, xi, seg)
        ys[...] = _s5_readout(u, xr, xi, ctr, cti, d_ref[...])
        _deinterleave(ys, y_ref, seg)

    return pl.pallas_call(
        body, name="s5_core_fwd", grid=(2,),
        in_specs=[pl.BlockSpec((s, HALF_CH), lambda i: (0, 12 + i))] + _s5_param_specs(),
        out_specs=pl.BlockSpec((s, HALF_CH), lambda i: (0, i)),
        out_shape=jax.ShapeDtypeStruct((s, W_GRP), F32),
        scratch_shapes=[pltpu.VMEM((N_SLAB, s, LANES), F32)] * 2 + [pltpu.VMEM((s, HALF_CH), F32)] * 2,
        compiler_params=_cp(dimension_semantics=("parallel",)),
    )(z, *sp)


def _s5_core_bwd(z, dy, sp):
    s = z.shape[0]
    seg = s // N_SEG

    def body(u_ref, dy_ref, rowp, colp, br_ref, bi_ref, cr_ref, ci_ref, d_ref,
             du_ref, dbr_ref, dbi_ref, dcr_ref, dci_ref, dd_ref, da_ref, dk_ref,
             xr, xi, gr, gi, us, dys):
        ar, ai, _, _ = _s5_disc(rowp[0:1, :], rowp[1:2, :], rowp[2:3, :])
        br, bi = br_ref[...], bi_ref[...]
        kr, ki, btr, bti, ctr, cti, bmask, cmask = _s5_mats(colp[...], br, bi, cr_ref[...], ci_ref[...])
        _interleave(u_ref, us, seg)
        _interleave(dy_ref, dys, seg)
        u = us[...]
        d = d_ref[...]
        _s5_forward_states(u, btr, bti, ar, ai, xr, xi, seg)

        dy = dys[...]
        dd_ref[...] = jnp.sum(dy * u, axis=0, keepdims=True)
        du = d * dy
        dyb = dy.astype(BF16)
        dctr, dcti = [], []
        for q in range(N_SLAB):
            gr[q] = jnp.dot(dyb, ctr[:, _slab(q)], preferred_element_type=F32)
            gi[q] = -jnp.dot(dyb, cti[:, _slab(q)], preferred_element_type=F32)
            dctr.append(_dot_tn(dyb, xr[q]))
            dcti.append(-_dot_tn(dyb, xi[q]))
        selp = ((_iota((HALF_STATE, 64), 0) & 63) == _iota((HALF_STATE, 64), 1)).astype(F32)
        dcr_ref[...] = _dot_hi(jnp.where(cmask, jnp.concatenate(dctr, axis=1), 0.0), selp)
        dci_ref[...] = _dot_hi(jnp.where(cmask, jnp.concatenate(dcti, axis=1), 0.0), selp)

        da = _scan(gr, gi, ar, -ai, seg, True, states=(xr, xi))
        dar, dai = [p[0] for p in da], [p[1] for p in da]
        da_ref[...] = jnp.zeros_like(da_ref)
        da_ref[0:1, :] = jnp.concatenate(dar, axis=1)
        da_ref[1:2, :] = jnp.concatenate(dai, axis=1)

        ub = u.astype(BF16)
        dbtr, dbti = [], []
        for q in range(N_SLAB):
            g_r, g_i = gr[q].astype(BF16), gi[q].astype(BF16)
            du = du + jnp.dot(g_r, btr[_slab(q), :], preferred_element_type=F32) \
                + jnp.dot(g_i, bti[_slab(q), :], preferred_element_type=F32)
            dbtr.append(_dot_tn(g_r, ub))
            dbti.append(_dot_tn(g_i, ub))
        us[...] = du
        _deinterleave(us, du_ref, seg)
        selc =((_iota((HALF_CH, 16), 0) & 15) == _iota((HALF_CH, 16), 1)).astype(F32)
        dbbr = _dot_hi(jnp.where(bmask, jnp.concatenate(dbtr, axis=0), 0.0), selc)
        dbbi = _dot_hi(jnp.where(bmask, jnp.concatenate(dbti, axis=0), 0.0), selc)
        dbr_ref[...] = kr * dbbr + ki * dbbi
        dbi_ref[...] = kr * dbbi - ki * dbbr
        dk_ref[:, 0:1] = jnp.sum(dbbr * br + dbbi * bi, axis=1, keepdims=True)
        dk_ref[:, 1:2] = jnp.sum(dbbi * br - dbbr * bi, axis=1, keepdims=True)

    half = pl.BlockSpec((s, HALF_CH), lambda i: (0, i))
    return pl.pallas_call(
        body, name="s5_core_bwd", grid=(2,),
        in_specs=[pl.BlockSpec((s, HALF_CH), lambda i: (0, 12 + i)), half] + _s5_param_specs(),
        out_specs=[half, pl.BlockSpec((HALF_STATE, 16), lambda i: (i, 0)), pl.BlockSpec((HALF_STATE, 16), lambda i: (i, 0)),
                   pl.BlockSpec((HALF_CH, 64), lambda i: (i, 0)), pl.BlockSpec((HALF_CH, 64), lambda i: (i, 0)),
                   pl.BlockSpec((1, HALF_CH), lambda i: (0, i)), pl.BlockSpec((8, HALF_STATE), lambda i: (0, i)),
                   pl.BlockSpec((HALF_STATE, 2), lambda i: (i, 0))],
        out_shape=[jax.ShapeDtypeStruct((s, W_GRP), F32), jax.ShapeDtypeStruct((N_STATE, 16), F32),
                   jax.ShapeDtypeStruct((N_STATE, 16), F32), jax.ShapeDtypeStruct((W_GRP, 64), F32),
                   jax.ShapeDtypeStruct((W_GRP, 64), F32), jax.ShapeDtypeStruct((1, W_GRP), F32),
                   jax.ShapeDtypeStruct((8, N_STATE), F32), jax.ShapeDtypeStruct((N_STATE, 2), F32)],
        scratch_shapes=[pltpu.VMEM((N_SLAB, s, LANES), F32)] * 4 + [pltpu.VMEM((s, HALF_CH), F32)] * 2,
        compiler_params=_cp(dimension_semantics=("parallel",)),
    )(z, dy, *sp)


def _s5_param_bwd(lre, lim, ldt, da_r, da_i, dk_r, dk_i):
    n = lre.shape[0]

    def body(lre_ref, lim_ref, ldt_ref, dar_ref, dai_ref, dkr_ref, dki_ref, o_re, o_im, o_dt):
        lre, lim, ldt = lre_ref[...], lim_ref[...], ldt_ref[...]
        dt = jnp.exp(ldt)
        ar, ai, kr, ki = _s5_disc(lre, lim, ldt)
        mag = jnp.exp(lre * dt)
        den = lre * lre + lim * lim
        dkr, dki = dkr_ref[...], dki_ref[...]
        nr, ni = ar - 1.0, ai
        d_ar = dar_ref[...] + (dkr * lre - dki * lim) / den
        d_ai = dai_ref[...] + (dkr * lim + dki * lre) / den
        kk = (kr * dkr + ki * dki) * 2.0 / den
        d_lre = (dkr * nr + dki * ni) / den - kk * lre
        d_lim = (dkr * ni - dki * nr) / den - kk * lim
        d_mag = (d_ar * ar + d_ai * ai) / mag
        d_ang = d_ai * ar - d_ar * ai
        o_re[...] = d_lre + d_mag * mag * dt
        o_im[...] = d_lim + d_ang * dt
        o_dt[...] = jnp.sum((d_mag * mag * lre + d_ang * lim) * dt, axis=1, keepdims=True)

    return pl.pallas_call(
        body, name="s5_param_bwd",
        out_shape=[jax.ShapeDtypeStruct((n, 64), F32), jax.ShapeDtypeStruct((n, 64), F32),
                   jax.ShapeDtypeStruct((n, 1), F32)],
    )(lre, lim, ldt, da_r, da_i, dk_r, dk_i)


def _loss_head(x, fg, target):
    s, d = x.shape
    tm = _tm(s)

    def body(x_ref, fg_ref, t_ref, loss_ref, dx_ref, dfg_ref):
        i = pl.program_id(0)

        @pl.when(i == 0)
        def _():
            loss_ref[...] = jnp.zeros_like(loss_ref)
            dfg_ref[...] = jnp.zeros_like(dfg_ref)

        xv, g = x_ref[...], fg_ref[...]
        r = lax.rsqrt(jnp.mean(xv * xv, axis=-1, keepdims=True) + EPS)
        xh = xv * r
        err = xh * g - t_ref[...]
        loss_ref[...] += 0.5 * jnp.sum(jnp.mean(err * err, axis=-1, keepdims=True), axis=0, keepdims=True)
        dy = err * (1.0 / d)
        dfg_ref[...] += jnp.sum(dy * xh, axis=0, keepdims=True)
        dxh = dy * g
        dx_ref[...] = r * (dxh - xh * jnp.mean(dxh * xh, axis=-1, keepdims=True))

    row = pl.BlockSpec((tm, d), lambda i: (i, 0))
    return pl.pallas_call(
        body, name="loss_head", grid=(s // tm,),
        in_specs=[row, _full((1, d)), row], out_specs=[_full((1, 1)), row, _full((1, d))],
        out_shape=[jax.ShapeDtypeStruct((1, 1), F32), jax.ShapeDtypeStruct((s, d), F32),
                   jax.ShapeDtypeStruct((1, d), F32)],
        compiler_params=_cp(dimension_semantics=("arbitrary",)),
    )(x, fg, target)


ADA_TN = 384


def _cond_fwd(cact, ada_w, ada_b_loc):
    nl, d, n = ada_w.shape

    def body(c_ref, w_ref, b_ref, o_ref):
        o_ref[...] = _dot(c_ref[...], w_ref[...]) + b_ref[...]

    return pl.pallas_call(
        body, name="cond_fwd", grid=(nl, n // ADA_TN),
        in_specs=[_full((N_DEV, d)), pl.BlockSpec((None, d, ADA_TN), lambda l, j: (l, 0, j)),
                  pl.BlockSpec((None, 1, ADA_TN), lambda l, j: (l, 0, j))],
        out_specs=pl.BlockSpec((None, N_DEV, ADA_TN), lambda l, j: (l, 0, j)),
        out_shape=jax.ShapeDtypeStruct((nl, N_DEV, n), F32),
        compiler_params=_cp(dimension_semantics=("parallel", "parallel")),
    )(cact, ada_w, ada_b_loc)


ELEMENTWISE_BLOCK_BYTES = 1 << 20


def _row_tile(r, c, itemsize=4):
    best = None
    for t in range(8, r + 1, 8):
        if r % t == 0 and t * c * itemsize <= ELEMENTWISE_BLOCK_BYTES:
            best = t
    return best if best is not None else r


def _adamw_math(w, g, m, v):
    m = ADAM_B1 * m + (1.0 - ADAM_B1) * g
    v = ADAM_B2 * v + (1.0 - ADAM_B2) * (g * g)
    m_hat = m / (1.0 - ADAM_B1 ** ADAM_STEP)
    v_hat = v / (1.0 - ADAM_B2 ** ADAM_STEP)
    delta = -ADAM_LR * (m_hat / (jnp.sqrt(v_hat) + ADAM_EPS) + ADAM_WD * w)
    return delta, m, v


def _ada_w_update(cact, dcond_loc, w, m, v):
    nl, d, n = w.shape

    def body(c_ref, dc_ref, w_ref, m_ref, v_ref, g_out, d_out, m_out, v_out):
        g = _dot_tn(c_ref[...], dc_ref[...])
        g_out[...] = g
        d_out[...], m_out[...], v_out[...] = _adamw_math(w_ref[...], g, m_ref[...], v_ref[...])

    blk = pl.BlockSpec((None, d, ADA_TN), lambda l, j: (l, 0, j))
    return pl.pallas_call(
        body, name="ada_w_update", grid=(nl, n // ADA_TN),
        in_specs=[_full((N_DEV, d)), pl.BlockSpec((None, N_DEV, ADA_TN), lambda l, j: (l, 0, j)), blk, blk, blk],
        out_specs=[blk] * 4, out_shape=[jax.ShapeDtypeStruct((nl, d, n), F32)] * 4,
        compiler_params=_cp(dimension_semantics=("parallel", "parallel")),
    )(cact, dcond_loc, w, m, v)


def _place():
    x, y, c = lax.axis_index("x"), lax.axis_index("y"), lax.axis_index("c")
    chips = [(1 - x, y), (x, 1 - y), (1 - x, 1 - y)]
    return x, y, c, chips


def _remote(src, dst, send_sem, recv_sem, to):
    return pltpu.make_async_remote_copy(src_ref=src, dst_ref=dst, send_sem=send_sem, recv_sem=recv_sem,
                                        device_id=to, device_id_type=MESH_ID)


def _sems(n):
    return [pltpu.SemaphoreType.DMA((n,)), pltpu.SemaphoreType.DMA((n,))]


def _all_gather8(v, name):
    r, cdim = v.shape

    def body(x_ref, out_ref, stage, send_sems, recv_sems):
        x, y, c, chips = _place()
        sibling = (x, y, 1 - c)

        def slot(px, py, pc):
            return out_ref.at[4 * px + 2 * py + pc]

        first = [_remote(x_ref, slot(x, y, c), send_sems.at[0], recv_sems.at[0], sibling)]
        first += [_remote(x_ref, slot(x, y, c), send_sems.at[1 + j], recv_sems.at[1 + j], (*chip, c))
                  for j, chip in enumerate(chips)]
        for cp in first:
            cp.start()
        pltpu.sync_copy(x_ref, stage)
        pltpu.sync_copy(stage, slot(x, y, c))
        passed = []
        for j, chip in enumerate(chips):
            blk = slot(*chip, c)
            _remote(blk, blk, send_sems.at[1 + j], recv_sems.at[1 + j], (x, y, c)).wait_recv()
            fw = _remote(blk, blk, send_sems.at[4 + j], recv_sems.at[4 + j], sibling)
            fw.start()
            passed.append(fw)
        blk = slot(x, y, 1 - c)
        _remote(blk, blk, send_sems.at[0], recv_sems.at[0], (x, y, c)).wait_recv()
        for j, chip in enumerate(chips):
            blk = slot(*chip, 1 - c)
            _remote(blk, blk, send_sems.at[4 + j], recv_sems.at[4 + j], (x, y, c)).wait_recv()
        for cp in first + passed:
            cp.wait_send()

    return pl.pallas_call(
        body, name=name, out_shape=jax.ShapeDtypeStruct((N_DEV, r, cdim), v.dtype),
        in_specs=[ANY], out_specs=ANY,
        scratch_shapes=[pltpu.VMEM((r, cdim), v.dtype)] + _sems(7),
        compiler_params=_cp(),
    )(v)


def _gather_first_copies():
    def make(refs, send_sems, recv_sems):
        x, y, c, chips = _place()
        mine = refs[0].at[4 * x + 2 * y + c]
        to = [(x, y, 1 - c)] + [(*chip, c) for chip in chips]
        return [_remote(mine, mine, send_sems.at[k], recv_sems.at[k], dev) for k, dev in enumerate(to)]
    return make


def _gather_pass_on(buf, name):
    def body(in_ref, out_ref, send_sems, recv_sems):
        x, y, c, chips = _place()
        passed = []
        for j, chip in enumerate(chips):
            blk = out_ref.at[4 * chip[0] + 2 * chip[1] + c]
            fw = _remote(blk, blk, send_sems.at[j], recv_sems.at[j], (x, y, 1 - c))
            fw.start()
            passed.append(fw)
        for j, chip in enumerate(chips):
            blk = out_ref.at[4 * chip[0] + 2 * chip[1] + 1 - c]
            _remote(blk, blk, send_sems.at[j], recv_sems.at[j], (x, y, c)).wait_recv()
        for fw in passed:
            fw.wait_send()

    return pl.pallas_call(
        body, name=name, out_shape=jax.ShapeDtypeStruct(buf.shape, buf.dtype),
        in_specs=[ANY], out_specs=ANY, input_output_aliases={0: 0}, scratch_shapes=_sems(3),
    )(buf)


def _place_weights(ws, layer, kidx, after):
    steps = 4
    shapes, in_specs, out_specs = [], [], []
    for w, kind in zip(ws, BIG_KINDS):
        _, a, b = w.shape
        in_specs.append(pl.BlockSpec((None, a // steps, b), lambda i, k: (layer, i, 0)))
        if kind == "col":
            shapes.append((2, a, 2 * b))
            out_specs.append(pl.BlockSpec((None, a // steps, b), lambda i, k: (k[0] // 2, i, k[0] % 2)))
        else:
            shapes.append((N_CHIP, a, b))
            out_specs.append(pl.BlockSpec((None, a // steps, b), lambda i, k: (k[0], i, 0)))

    def body(k_ref, *refs):
        outs = refs[len(ws) + 1:]
        for t in range(len(ws)):
            outs[t][...] = refs[t][...].astype(BF16)

    return pl.pallas_call(
        body, name="place_weights", out_shape=[jax.ShapeDtypeStruct(s, BF16) for s in shapes],
        grid_spec=pltpu.PrefetchScalarGridSpec(num_scalar_prefetch=1, grid=(steps,), in_specs=in_specs + [ANY],
                                               out_specs=out_specs),
        compiler_params=_cp(dimension_semantics=("parallel",)),
    )(kidx, *ws, after)


HBM = pl.BlockSpec(memory_space=pltpu.HBM)
SEM = pl.BlockSpec(memory_space=pltpu.SEMAPHORE)
EFFECT = pltpu.SideEffectType.DATAFLOW_SIDE_EFFECTING


def _weight_block(ref, kind, k, h):
    if kind == "col":
        ncol = ref.shape[3] // 2
        return ref.at[k // 2, h, :, pl.ds(pl.multiple_of((k % 2) * ncol, LANES), ncol)]
    return ref.at[k, h]


def _in_hbm(a):
    return pltpu.with_memory_space_constraint(a, pltpu.HBM)


def _weight_send_start(placed, kinds, name):
    nt = len(placed)

    def body(*refs):
        send_sems, recv_sems = refs[nt], refs[nt + 1]
        dst = refs[nt + 2:2 * nt + 2]
        token = refs[2 * nt + 2]
        x, y, c, chips = _place()
        kme = 2 * x + y
        for t in range(nt):
            for j, chip in enumerate(chips):
                own = _weight_block(dst[t], kinds[t], kme, c)
                _remote(own, own, send_sems.at[3 * t + j], recv_sems.at[3 * t + j], (*chip, c)).start()
        token[...] = jnp.zeros_like(token)

    return pl.pallas_call(
        body, name=name,
        out_shape=(pltpu.SemaphoreType.DMA((3 * nt,)), pltpu.SemaphoreType.DMA((3 * nt,)),
                   *[pltpu.HBM(a.shape, a.dtype) for a in placed], jax.ShapeDtypeStruct((8, LANES), F32)),
        in_specs=[HBM] * nt, out_specs=(SEM, SEM, *[HBM] * nt, pl.BlockSpec(memory_space=pltpu.VMEM)),
        input_output_aliases={t: 2 + t for t in range(nt)},
        compiler_params=pltpu.CompilerParams(has_side_effects=EFFECT),
    )(*[_in_hbm(a) for a in placed])


def _weight_send_wait(send_sems, recv_sems, arrays, kinds, after, name):
    nt = len(arrays)

    def body(*refs):
        arr = refs[:nt]
        send_sems, recv_sems = refs[nt], refs[nt + 1]
        x, y, c, chips = _place()
        kme = 2 * x + y
        for t in range(nt):
            for j, chip in enumerate(chips):
                own = _weight_block(arr[t], kinds[t], kme, c)
                got = _weight_block(arr[t], kinds[t], 2 * chip[0] + chip[1], c)
                cp = _remote(own, got, send_sems.at[3 * t + j], recv_sems.at[3 * t + j], (*chip, c))
                cp.wait_send()
                cp.wait_recv()

    return pl.pallas_call(
        body, name=name, out_shape=[pltpu.HBM(a.shape, a.dtype) for a in arrays],
        in_specs=[HBM] * nt + [SEM, SEM, ANY], out_specs=[HBM] * nt,
        input_output_aliases={t: t for t in range(nt)},
        compiler_params=pltpu.CompilerParams(has_side_effects=EFFECT),
    )(*arrays, send_sems, recv_sems, after)


def _forward_copies(kinds):
    def make(refs, send_sems, recv_sems):
        x, y, c, chips = _place()
        cps = []
        for t in range(len(kinds)):
            for j, chip in enumerate(chips):
                blk = _weight_block(refs[t], kinds[t], 2 * chip[0] + chip[1], c)
                cps.append(_remote(blk, blk, send_sems.at[3 * t + j], recv_sems.at[3 * t + j], (x, y, 1 - c)))
        return cps
    return make


def _split_start(name, arrays, n_copies, make_copies):
    na = len(arrays)

    def body(*refs):
        send_sems, recv_sems = refs[na], refs[na + 1]
        for cp in make_copies(refs[na + 2:2 * na + 2], send_sems, recv_sems):
            cp.start()
        token = refs[2 * na + 2]
        token[...] = jnp.zeros_like(token)

    return pl.pallas_call(
        body, name=name,
        out_shape=(pltpu.SemaphoreType.DMA((n_copies,)), pltpu.SemaphoreType.DMA((n_copies,)),
                   *[pltpu.HBM(a.shape, a.dtype) for a in arrays], jax.ShapeDtypeStruct((8, LANES), F32)),
        in_specs=[HBM] * na, out_specs=(SEM, SEM, *[HBM] * na, pl.BlockSpec(memory_space=pltpu.VMEM)),
        input_output_aliases={t: 2 + t for t in range(na)},
        compiler_params=pltpu.CompilerParams(has_side_effects=EFFECT),
    )(*[_in_hbm(a) for a in arrays])


def _split_wait(name, started, make_copies, after):
    send_sems, recv_sems, *arrays, _ = started
    na = len(arrays)

    def body(*refs):
        send_sems, recv_sems = refs[na], refs[na + 1]
        for cp in make_copies(refs[:na], send_sems, recv_sems):
            cp.wait_send()
            cp.wait_recv()

    return pl.pallas_call(
        body, name=name, out_shape=[pltpu.HBM(a.shape, a.dtype) for a in arrays],
        in_specs=[HBM] * na + [SEM, SEM, ANY], out_specs=[HBM] * na,
        input_output_aliases={t: t for t in range(na)},
        compiler_params=pltpu.CompilerParams(has_side_effects=EFFECT),
    )(*arrays, send_sems, recv_sems, after)


def _exchange_copies(nt):
    def make(refs, send_sems, recv_sems):
        x, y, c, _ = _place()
        return [_remote(refs[t].at[:, 1 - c], refs[nt + t], send_sems.at[t], recv_sems.at[t], (x, y, 1 - c))
                for t in range(nt)]
    return make


def _sibling_exchange_start(views, name):
    lands = [lax.empty((v.shape[0],) + v.shape[2:], v.dtype) for v in views]
    return _split_start(name, list(views) + lands, len(views), _exchange_copies(len(views)))


def _sibling_exchange_wait(started, after, name):
    nt = (len(started) - 3) // 2
    outs = _split_wait(name, started, _exchange_copies(nt), after)
    return outs[:nt], outs[nt:]


def _scatter_copies(src, land, kinds, send_sems, recv_sems):
    x, y, c, chips = _place()
    cps = []
    for t in range(len(src)):
        for j, chip in enumerate(chips):
            k = 2 * chip[0] + chip[1]
            if kinds[t] == "col":
                ncol = land[t].shape[2]
                win = src[t].at[k // 2, :, pl.ds(pl.multiple_of((k % 2) * ncol, LANES), ncol)]
            else:
                win = src[t].at[k]
            cps.append(_remote(win, land[t].at[j], send_sems.at[3 * t + j], recv_sems.at[3 * t + j], (*chip, c)))
    return cps


def _chip_scatter_start(parts, kinds, name):
    nt = len(parts)
    shapes = []
    for p, kind in zip(parts, kinds):
        shapes.append((3, p.shape[1], p.shape[2] // 2) if kind == "col" else (3,) + p.shape[1:])

    def body(*refs):
        send_sems, recv_sems = refs[2 * nt], refs[2 * nt + 1]
        src, land = refs[2 * nt + 2:3 * nt + 2], refs[3 * nt + 2:4 * nt + 2]
        token = refs[4 * nt + 2]
        for cp in _scatter_copies(src, land, kinds, send_sems, recv_sems):
            cp.start()
        token[...] = jnp.zeros_like(token)

    lands = [lax.empty(s, BF16) for s in shapes]
    return pl.pallas_call(
        body, name=name,
        out_shape=(pltpu.SemaphoreType.DMA((3 * nt,)), pltpu.SemaphoreType.DMA((3 * nt,)),
                   *[pltpu.HBM(a.shape, a.dtype) for a in parts], *[pltpu.HBM(s, BF16) for s in shapes],
                   jax.ShapeDtypeStruct((8, LANES), F32)),
        in_specs=[HBM] * (2 * nt), out_specs=(SEM, SEM, *[HBM] * (2 * nt), pl.BlockSpec(memory_space=pltpu.VMEM)),
        input_output_aliases={t: 2 + t for t in range(2 * nt)},
        compiler_params=pltpu.CompilerParams(has_side_effects=EFFECT),
    )(*[_in_hbm(a) for a in parts], *[_in_hbm(a) for a in lands])


def _chip_scatter_wait(send_sems, recv_sems, parts, lands, kinds, after, name):
    nt = len(parts)

    def body(*refs):
        src, land = refs[:nt], refs[nt:2 * nt]
        send_sems, recv_sems = refs[2 * nt], refs[2 * nt + 1]
        for cp in _scatter_copies(src, land, kinds, send_sems, recv_sems):
            cp.wait_send()
            cp.wait_recv()

    outs = pl.pallas_call(
        body, name=name, out_shape=[pltpu.HBM(a.shape, a.dtype) for a in list(parts) + list(lands)],
        in_specs=[HBM] * (2 * nt) + [SEM, SEM, ANY], out_specs=[HBM] * (2 * nt),
        input_output_aliases={t: t for t in range(2 * nt)},
        compiler_params=pltpu.CompilerParams(has_side_effects=EFFECT),
    )(*parts, *lands, send_sems, recv_sems, after)
    return outs[:nt], outs[nt:]


def _share_copies(nt):
    def make(refs, send_sems, recv_sems):
        x, y, c, _ = _place()
        return [_remote(refs[t].at[c], refs[t].at[c], send_sems.at[t], recv_sems.at[t], (x, y, 1 - c))
                for t in range(nt)]
    return make


def _sibling_share_start(fulls, name):
    return _split_start(name, list(fulls), len(fulls), _share_copies(len(fulls)))


def _sibling_share_wait(started, after, name):
    return _split_wait(name, started, _share_copies(len(started) - 3), after)


SUM_STEPS = 4


def _pair_sum(views, lands, ck):
    nt = len(views)
    in_specs, out_specs, shapes = [], [], []
    for v in views:
        b, _, r, cc = v.shape
        per = SUM_STEPS // b
        tr = r // per
        in_specs.append(pl.BlockSpec((None, None, tr, cc), lambda i, s, per=per: (i // per, s[0], i % per, 0)))
        out_specs.append(pl.BlockSpec((None, tr, cc), lambda i, s, per=per: (i // per, i % per, 0)))
        shapes.append((b, r, cc))
    in_specs = in_specs + out_specs

    def body(s_ref, *refs):
        for t in range(nt):
            refs[2 * nt + t][...] = (refs[t][...].astype(F32) + refs[nt + t][...].astype(F32)).astype(BF16)

    return pl.pallas_call(
        body, name="grad_pair_sum", out_shape=[jax.ShapeDtypeStruct(s, BF16) for s in shapes],
        grid_spec=pltpu.PrefetchScalarGridSpec(num_scalar_prefetch=1, grid=(SUM_STEPS,), in_specs=in_specs,
                                               out_specs=out_specs),
        compiler_params=_cp(dimension_semantics=("parallel",)),
    )(ck, *views, *lands)


def _chip_sum(parts, lands, kinds, ck):
    nt = len(parts)
    steps = 2
    in_own, in_land, out_specs, shapes = [], [], [], []
    for ld, kind in zip(lands, kinds):
        _, r, cc = ld.shape
        tr = r // steps
        if kind == "col":
            in_own.append(pl.BlockSpec((None, tr, cc), lambda i, s: (s[1] // 2, i, s[1] % 2)))
        else:
            in_own.append(pl.BlockSpec((None, tr, cc), lambda i, s: (s[1], i, 0)))
        in_land.append(pl.BlockSpec((3, tr, cc), lambda i, s: (0, i, 0)))
        out_specs.append(pl.BlockSpec((None, tr, cc), lambda i, s: (s[0], i, 0)))
        shapes.append((2, r, cc))

    def body(s_ref, *refs):
        for t in range(nt):
            acc = refs[t][...].astype(F32)
            for j in range(3):
                acc = acc + refs[nt + t][j].astype(F32)
            refs[2 * nt + t][...] = acc

    return pl.pallas_call(
        body, name="grad_chip_sum", out_shape=[jax.ShapeDtypeStruct(s, F32) for s in shapes],
        grid_spec=pltpu.PrefetchScalarGridSpec(num_scalar_prefetch=1, grid=(steps,), in_specs=in_own + in_land,
                                               out_specs=out_specs),
        compiler_params=_cp(dimension_semantics=("parallel",)),
    )(ck, *parts, *lands)


def _sum8(g):
    _, r, cc = g.shape
    tr = _row_tile(r, N_DEV * cc)

    def body(g_ref, o_ref):
        acc = g_ref[0].astype(F32)
        for d in range(1, N_DEV):
            acc = acc + g_ref[d].astype(F32)
        o_ref[...] = acc

    return pl.pallas_call(
        body, name="small_grad_sum", grid=(r // tr,),
        in_specs=[pl.BlockSpec((N_DEV, tr, cc), lambda i: (0, i, 0))],
        out_specs=pl.BlockSpec((tr, cc), lambda i: (i, 0)),
        out_shape=jax.ShapeDtypeStruct((r, cc), F32),
        compiler_params=_cp(dimension_semantics=("parallel",)),
    )(g)


def _silu_rows(c):
    def body(c_ref, o_ref):
        v = c_ref[...]
        o_ref[...] = v * jax.nn.sigmoid(v)

    return pl.pallas_call(body, name="cond_silu", out_shape=jax.ShapeDtypeStruct(c.shape, F32))(c)


def _pack(arrays):
    rows = []
    for a in arrays:
        flat = a.reshape(-1)
        rows.append(jnp.pad(flat, (0, (-flat.shape[0]) % (8 * LANES))).reshape(-1, LANES))
    n = sum(r.shape[0] for r in rows)
    if n % 256:
        rows.append(jnp.zeros((256 - n % 256, LANES), rows[0].dtype))
    return jnp.concatenate(rows, axis=0)


def _unpack(packed, shapes):
    out, off = [], 0
    for s in shapes:
        n = math.prod(s)
        nr = 8 * -(-n // (8 * LANES))
        out.append(packed[off:off + nr].reshape(-1)[:n].reshape(s))
        off += nr
    return out


def _as_rows(a):
    return a.reshape(1, -1) if a.ndim == 1 else a.reshape(-1, a.shape[-1])


def _adamw_many(ws, gs, ms, vs, name, steps=1):
    nt = len(ws)

    def body(*refs):
        for t in range(nt):
            w_ref, g_ref, m_ref, v_ref = (refs[k * nt + t] for k in range(4))
            d, m, v = _adamw_math(w_ref[...], g_ref[...], m_ref[...], v_ref[...])
            refs[4 * nt + t][...] = d
            refs[5 * nt + t][...] = m
            refs[6 * nt + t][...] = v

    shapes = [jax.ShapeDtypeStruct(a.shape, F32) for a in ws]
    if steps == 1:
        outs = pl.pallas_call(body, name=name, out_shape=shapes * 3, compiler_params=_cp())(*ws, *gs, *ms, *vs)
    else:
        specs = [pl.BlockSpec((a.shape[0] // steps, a.shape[1]), lambda i: (i, 0)) for a in ws]
        outs = pl.pallas_call(
            body, name=name, grid=(steps,), in_specs=specs * 4, out_specs=specs * 3, out_shape=shapes * 3,
            compiler_params=_cp(dimension_semantics=("parallel",)),
        )(*ws, *gs, *ms, *vs)
    return outs[:nt], outs[nt:2 * nt], outs[2 * nt:]


def _exchange_big_grads(grads, kinds, layer):
    views = []
    for g, kind in zip(grads, kinds):
        if kind == "col":
            views.append(g.reshape(2, 2, g.shape[1] // 2, g.shape[2]))
        else:
            views.append(g.reshape(N_CHIP, 2, g.shape[0] // (2 * N_CHIP), g.shape[1]))
    return _sibling_exchange_start(views, "grad_exchange_start_%d" % layer)


def _scatter_big_grads(exchanged, kinds, ck, after, layer):
    views, lands = _sibling_exchange_wait(exchanged, after, "grad_exchange_wait_%d" % layer)
    parts = _pair_sum(views, lands, ck)
    return _chip_scatter_start(parts, kinds, "grad_scatter_start_%d" % layer)


def _finish_big_grads(started, kinds, ck, after, layer):
    nt = len(kinds)
    send_sems, recv_sems = started[0], started[1]
    parts, lands = started[2:2 + nt], started[2 + nt:2 + 2 * nt]
    parts, lands = _chip_scatter_wait(send_sems, recv_sems, parts, lands, kinds, after, "grad_scatter_wait_%d" % layer)
    return _sibling_share_start(_chip_sum(parts, lands, kinds, ck), "grad_share_start_%d" % layer)


def _adamw_layer(ws, gs, ms, vs, stacks, layer, name, steps):
    nt = len(ws)
    stacks = [s if s is not None else tuple(lax.empty(w.shape, F32) for _ in range(4)) for s, w in zip(stacks, ws)]

    def body(*refs):
        for t in range(nt):
            w_ref, g_ref, m_ref, v_ref = (refs[k * nt + t] for k in range(4))
            outs = refs[8 * nt + 4 * t:8 * nt + 4 * t + 4]
            g = g_ref[...]
            outs[0][...] = g
            outs[1][...], outs[2][...], outs[3][...] = _adamw_math(w_ref[...], g, m_ref[...], v_ref[...])

    in_specs, g_specs, out_specs = [], [], []
    for w in ws:
        _, r, c = w.shape
        in_specs.append(pl.BlockSpec((None, r // steps, c), lambda i: (layer, i, 0)))
        g_specs.append(pl.BlockSpec((r // steps, c), lambda i: (i, 0)))
        out_specs += [pl.BlockSpec((None, r // steps, c), lambda i: (layer, i, 0))] * 4
    in_specs = in_specs + g_specs + in_specs * 2 + [ANY] * (4 * nt)
    flat = [a for s in stacks for a in s]
    outs = pl.pallas_call(
        body, name=name, grid=(steps,), in_specs=in_specs, out_specs=out_specs,
        out_shape=[jax.ShapeDtypeStruct(a.shape, F32) for a in flat],
        input_output_aliases={4 * nt + k: k for k in range(4 * nt)},
        compiler_params=_cp(dimension_semantics=("parallel",)),
    )(*ws, *gs, *ms, *vs, *flat)
    return [tuple(outs[4 * t:4 * t + 4]) for t in range(nt)]


SMALL_NAMES = ["ada_b", "norm1_g", "norm2_g", "sgu_w", "sgu_b", "pool_w", "pool_scale", "conv_w", "s5_lambda_re",
               "s5_lambda_im", "s5_b_re", "s5_b_im", "s5_c_re", "s5_c_im", "s5_d", "s5_log_dt", "s5_glu_w", "s5_glu_b",
               "mix_norm_g", "norm3_g", "final_norm_g"]
BIG_NAMES = ["ffn1_w_in", "ffn1_w_out", "w_mix_in", "w_mix_out", "ffn2_w_in", "ffn2_w_out"]
BIG_KINDS = ["col", "row", "row", "row", "col", "row"]
WEIGHT_ORDER = ["ada_w", "ada_b", "norm1_g", "ffn1_w_in", "ffn1_w_out", "norm2_g", "w_mix_in", "sgu_w", "sgu_b", "pool_w",
                "pool_scale", "conv_w", "s5_lambda_re", "s5_lambda_im", "s5_b_re", "s5_b_im", "s5_c_re", "s5_c_im", "s5_d",
                "s5_log_dt", "s5_glu_w", "s5_glu_b", "mix_norm_g", "w_mix_out", "norm3_g", "ffn2_w_in", "ffn2_w_out",
                "final_norm_g"]


def _local_step(x, target, cond, fetch_weights, prefetch_weights, p, emit_grads):
    nl, d = DEPTH, x.shape[1]
    row = lambda a: a.reshape(1, -1)
    saved = []
    for l in range(nl):
        (wi1, wo1, wmit, wmo, wi2, wo2), tok = fetch_weights(l, x)
        cl = cond[l] + tok
        mod1, mod2, mod3 = cl[0:3], cl[3:6], cl[6:9]
        lre, lim = p["s5_lambda_re"][l].reshape(-1), p["s5_lambda_im"][l].reshape(-1)
        ldt = jnp.repeat(p["s5_log_dt"][l], 64)
        rowp = jnp.stack([lre, lim, ldt])
        sp = (rowp, rowp.T, p["s5_b_re"][l].reshape(N_STATE, 16), p["s5_b_im"][l].reshape(N_STATE, 16),
              p["s5_c_re"][l].reshape(W_GRP, 64), p["s5_c_im"][l].reshape(W_GRP, 64), row(p["s5_d"][l]))
        glu = (p["s5_glu_w"][l], row(p["s5_glu_b"][l]))
        bias_full = jnp.repeat(p["sgu_b"][l].T, 64, axis=1)
        pw2 = p["pool_w"][l].reshape(W_GRP, 64)
        x1, h1, a1, b1, o1 = _ffn_fwd(x, mod1, row(p["norm1_g"][l]), wi1, wo1)
        z, h2 = _mix_in_fwd(x1, mod2, row(p["norm2_g"][l]), wmit)
        ya = _sgu_fwd(z, p["sgu_w"][l], bias_full)
        yb, yc = _poolconv_fwd(z, pw2, row(p["pool_scale"][l]), p["conv_w"][l])
        ys = (ya, yb, yc, _s5_core_fwd(z, sp))
        x2, m = _mix_out_fwd(ys, glu, row(p["mix_norm_g"][l]), wmo, x1, mod2[2:3])
        mod3 = mod3 + prefetch_weights(l + 1, x2)
        x3, h3, a3, b3, o3 = _ffn_fwd(x2, mod3, row(p["norm3_g"][l]), wi2, wo2)
        saved.append((x, x1, x2, h1, a1, b1, o1, z, h2, ys, m, h3, a3, b3, o3, sp, bias_full, pw2, glu,
                      (wi1, wo1, wmit, wmo, wi2, wo2), cl))
        x = x3

    loss, dx, dfg = _loss_head(x, row(p["final_norm_g"]), target)

    sg = {n: [None] * nl for n in SMALL_NAMES if n not in ("ada_b", "final_norm_g")}
    dcond = [None] * nl
    s5_da, s5_dk = [None] * nl, [None] * nl
    tok = 0.0
    for l in reversed(range(nl)):
        (x0, x1, x2, h1, a1, b1, o1, z, h2, ys, m, h3, a3, b3, o3, sp, bias_full, pw2, glu,
         (wi1, wo1, wmit, wmo, wi2, wo2), cl) = saved[l]
        cl = cl + tok
        mod1, mod2, mod3 = cl[0:3], cl[3:6], cl[6:9]
        dza, dzb, dwi2, dwo2, dgate3 = _ffn_bwd_main(dx, o3, mod3[2:3], h3, a3, b3, wo2)
        dx, rows3 = _ffn_bwd_in(dza, dzb, wi2, x2, dx, mod3, row(p["norm3_g"][l]))
        outs = _mix_out_bwd(dx, m, mod2[2:3], ys, glu, row(p["mix_norm_g"][l]), wmo)
        dys, dgate2, dmng, dwmo, dgw, dgb = outs[0:4], outs[4], outs[5], outs[6], outs[7], outs[8]
        dza_, dsw, dsb = _sgu_bwd(z, dys[0], p["sgu_w"][l], bias_full)
        dzb_, dzc_, dwbd, dps, dcw = _poolconv_bwd(z, dys[1], dys[2], pw2, row(p["pool_scale"][l]), p["conv_w"][l])
        dzd_, dbr, dbi, dcr, dci, dd, da, dk = _s5_core_bwd(z, dys[3], sp)
        dx, rows2, dwmit = _mix_in_bwd((dza_, dzb_, dzc_, dzd_), h2, wmit, x1, dx, mod2, row(p["norm2_g"][l]))
        dza, dzb, dwi1, dwo1, dgate1 = _ffn_bwd_main(dx, o1, mod1[2:3], h1, a1, b1, wo1)
        tok, layer_done = emit_grads(l, [dwi1, dwo1, dwmit, dwmo, dwi2, dwo2])
        dx, rows1 = _ffn_bwd_in(dza, dzb, wi1, x0, dx, mod1 + tok, row(p["norm1_g"][l]))
        if l > 0:
            tok = layer_done(dx)[0, 0]
        dcond[l] = jnp.concatenate([rows1[0:2], dgate1, rows2[0:2], dgate2, rows3[0:2], dgate3], axis=0)
        sg["norm1_g"][l], sg["norm2_g"][l], sg["norm3_g"][l] = rows1[2], rows2[2], rows3[2]
        sg["mix_norm_g"][l] = dmng[0]
        sg["sgu_w"][l] = dsw
        sg["sgu_b"][l] = dsb[:, 0:4].T
        g4 = dwbd.reshape(4, 64, 4, 64)
        sg["pool_w"][l] = jnp.stack([g4[k, :, k, :] for k in range(4)])
        sg["pool_scale"][l] = dps[0]
        sg["conv_w"][l] = dcw[0:3]
        sg["s5_b_re"][l], sg["s5_b_im"][l] = dbr.reshape(16, 64, 16), dbi.reshape(16, 64, 16)
        sg["s5_c_re"][l], sg["s5_c_im"][l] = dcr.reshape(16, 16, 64), dci.reshape(16, 16, 64)
        sg["s5_d"][l] = dd[0]
        sg["s5_glu_w"][l], sg["s5_glu_b"][l] = dgw, dgb[0]
        s5_da[l], s5_dk[l] = da, dk

    n16 = nl * 16
    dlre, dlim, dldt = _s5_param_bwd(
        p["s5_lambda_re"].reshape(n16, 64), p["s5_lambda_im"].reshape(n16, 64),
        jnp.repeat(p["s5_log_dt"].reshape(n16, 1), 64, axis=1),
        jnp.stack([a[0] for a in s5_da]).reshape(n16, 64), jnp.stack([a[1] for a in s5_da]).reshape(n16, 64),
        jnp.stack([k[:, 0] for k in s5_dk]).reshape(n16, 64), jnp.stack([k[:, 1] for k in s5_dk]).reshape(n16, 64))
    small = {n: jnp.stack(v) for n, v in sg.items() if v[0] is not None}
    small["s5_lambda_re"] = dlre.reshape(nl, 16, 64)
    small["s5_lambda_im"] = dlim.reshape(nl, 16, 64)
    small["s5_log_dt"] = dldt.reshape(nl, 16)
    small["final_norm_g"] = dfg[0]
    return loss, dx, small, jnp.stack(dcond), layer_done


def kernel(x, c, ada_w, ada_b, norm1_g, ffn1_w_in, ffn1_w_out, norm2_g, w_mix_in, sgu_w, sgu_b, pool_w, pool_scale, conv_w, s5_lambda_re, s5_lambda_im, s5_b_re, s5_b_im, s5_c_re, s5_c_im, s5_d, s5_log_dt, s5_glu_w, s5_glu_b, mix_norm_g, w_mix_out, norm3_g, ffn2_w_in, ffn2_w_out, final_norm_g, loss_target, m_ada_w, m_ada_b, m_norm1_g, m_ffn1_w_in, m_ffn1_w_out, m_norm2_g, m_w_mix_in, m_sgu_w, m_sgu_b, m_pool_w, m_pool_scale, m_conv_w, m_s5_lambda_re, m_s5_lambda_im, m_s5_b_re, m_s5_b_im, m_s5_c_re, m_s5_c_im, m_s5_d, m_s5_log_dt, m_s5_glu_w, m_s5_glu_b, m_mix_norm_g, m_w_mix_out, m_norm3_g, m_ffn2_w_in, m_ffn2_w_out, m_final_norm_g, v_ada_w, v_ada_b, v_norm1_g, v_ffn1_w_in, v_ffn1_w_out, v_norm2_g, v_w_mix_in, v_sgu_w, v_sgu_b, v_pool_w, v_pool_scale, v_conv_w, v_s5_lambda_re, v_s5_lambda_im, v_s5_b_re, v_s5_b_im, v_s5_c_re, v_s5_c_im, v_s5_d, v_s5_log_dt, v_s5_glu_w, v_s5_glu_b, v_mix_norm_g, v_w_mix_out, v_norm3_g, v_ffn2_w_in, v_ffn2_w_out, v_final_norm_g):
    args = dict(locals())
    w = {n: args[n] for n in WEIGHT_ORDER}
    mom = {n: args["m_" + n] for n in WEIGHT_ORDER}
    vel = {n: args["v_" + n] for n in WEIGHT_ORDER}
    nl, d = DEPTH, x.shape[-1]
    s = x.shape[1]
    px, py, pc = lax.axis_index("x"), lax.axis_index("y"), lax.axis_index("c")
    kme = 2 * px + py
    me = 2 * kme + pc
    kidx = jnp.reshape(kme, (1,)).astype(jnp.int32)

    shards = [ffn1_w_in, ffn1_w_out, jnp.swapaxes(w_mix_in, 1, 2), w_mix_out, ffn2_w_in, ffn2_w_out]
    started_weights = {}

    def start_weights(l, after):
        placed = _place_weights(shards, l, kidx, after)
        views = [a.reshape(a.shape[0], 2, a.shape[1] // 2, a.shape[2]) for a in placed]
        *handles, token = _weight_send_start(views, BIG_KINDS, "weight_send_start_%d" % l)
        started_weights[l] = handles
        return token

    cact = _silu_rows(c)
    pre = _pack([cact, conv_w, s5_glu_w])
    pre_all = _all_gather8(pre, "gather_prelude")
    token = start_weights(0, pre_all)
    parts = [_unpack(pre_all[dev], [cact.shape, conv_w.shape, s5_glu_w.shape]) for dev in range(N_DEV)]
    cact_all = pre_all[:, :d // LANES, :].reshape(N_DEV, d)
    conv_full = jnp.concatenate([parts[2 * k][1] for k in range(N_CHIP)], axis=2)
    glu_full = jnp.concatenate([parts[2 * k][2] for k in range(N_CHIP)], axis=1)

    n_ada = ada_w.shape[2]
    ada_b_loc = lax.dynamic_slice_in_dim(ada_b, kme * n_ada, n_ada, axis=1).reshape(nl, 1, n_ada) + token[0, 0]
    cond_part = _cond_fwd(cact_all, ada_w, ada_b_loc)
    cond_mine = lax.dynamic_update_slice(lax.empty((N_DEV, nl * N_DEV, n_ada), F32),
                                         cond_part.reshape(1, nl * N_DEV, n_ada), (me, 0, 0))
    cond_gathering = _split_start("cond_send_start", [cond_mine], 4, _gather_first_copies())
    token = cond_gathering[-1]
    for l in range(1, nl):
        token = start_weights(l, token)
    cond_arrived, = _split_wait("cond_send_wait", cond_gathering, _gather_first_copies(), token)
    cond_all = _gather_pass_on(cond_arrived, "cond_pass_on").reshape(N_DEV, nl, N_DEV, n_ada)
    cond_me = jnp.concatenate(
        [lax.dynamic_index_in_dim(cond_all[2 * k], me, axis=1, keepdims=False) for k in range(N_CHIP)], axis=1)
    cond = cond_me.reshape(nl, 9, d)

    forwarding = {}

    def prefetch_weights(l, after):
        if l >= nl:
            return 0.0
        send_sems, recv_sems, *views = started_weights.pop(l)
        views = _weight_send_wait(send_sems, recv_sems, views, BIG_KINDS, after, "weight_send_wait_%d" % l)
        forwarding[l] = _split_start("weight_forward_start_%d" % l, views, 3 * len(views), _forward_copies(BIG_KINDS))
        return forwarding[l][-1][0, 0]

    def fetch_weights(l, after):
        if l not in forwarding:
            prefetch_weights(l, after)
        views = _split_wait("weight_forward_wait_%d" % l, forwarding.pop(l), _forward_copies(BIG_KINDS), after)
        full = [v.reshape(2, 2 * v.shape[2], v.shape[3]) if kind == "col" else v.reshape(-1, v.shape[3])
                for v, kind in zip(views, BIG_KINDS)]
        return full, 0.0

    ck = jnp.stack([pc, kme]).astype(jnp.int32)
    scattering, sharing = [], []
    stacks = {n: None for n in BIG_NAMES}
    groups = ((["ffn1_w_in", "ffn2_w_in"], 16, "adamw_w_in"),
              (["ffn1_w_out", "w_mix_in", "w_mix_out", "ffn2_w_out"], 8, "adamw_w_out"))

    def as_reduced(t):
        return {n: jnp.swapaxes(t[n], 1, 2) if n == "w_mix_in" else t[n] for n in BIG_NAMES}

    w_r, m_r, v_r = as_reduced(w), as_reduced(mom), as_reduced(vel)

    def apply_adamw(l, fulls):
        g = {n: f.reshape(2 * f.shape[1], f.shape[2]) for n, f in zip(BIG_NAMES, fulls)}
        for names, steps, call in groups:
            outs = _adamw_layer([w_r[n] for n in names], [g[n] for n in names], [m_r[n] for n in names],
                                [v_r[n] for n in names], [stacks[n] for n in names], l, call, steps)
            stacks.update(zip(names, outs))

    def retire_share(after):
        l2, shared = sharing.pop(0)
        apply_adamw(l2, _sibling_share_wait(shared, after, "grad_share_wait_%d" % l2))

    def retire_scatter(after):
        l1, scattered = scattering.pop(0)
        sharing.append((l1, _finish_big_grads(scattered, BIG_KINDS, ck, after, l1)))

    def retire(after):
        if sharing:
            retire_share(after)
        if scattering:
            retire_scatter(after)

    def emit_grads(l, grads_l):
        exchanged = _exchange_big_grads(grads_l, BIG_KINDS, l)

        def layer_done(after):
            started = _scatter_big_grads(exchanged, BIG_KINDS, ck, after, l)
            retire(after)
            scattering.append((l, started))
            return started[-1]

        return exchanged[-1][0, 0], layer_done

    p = {n: w[n] for n in SMALL_NAMES}
    p["conv_w"], p["s5_glu_w"] = conv_full, glu_full
    loss, dx, small, dcond, first_layer_done = _local_step(x[0], loss_target[0], cond, fetch_weights, prefetch_weights,
                                                           p, emit_grads)

    small_order = [n for n in SMALL_NAMES if n != "ada_b"]
    packed = _pack([dcond] + [small[n] for n in small_order]).astype(BF16)
    mine = lax.dynamic_update_slice(lax.empty((N_DEV,) + packed.shape, BF16), packed[None], (me, 0, 0))
    gathering = _split_start("small_grads_send_start", [mine], 4, _gather_first_copies())
    scatter_token = first_layer_done(gathering[-1])
    while sharing:
        retire_share(scatter_token)
    arrived, = _split_wait("small_grads_send_wait", gathering, _gather_first_copies(), stacks[BIG_NAMES[0]][0])
    gathered_small = _gather_pass_on(arrived, "small_grads_pass_on")
    total = _sum8(gathered_small)
    shapes = [dcond.shape] + [small[n].shape for n in small_order]
    tot = dict(zip(["ada_b"] + small_order, _unpack(total, shapes)))
    grads = {n: tot[n] for n in SMALL_NAMES}
    grads["ada_b"] = tot["ada_b"].reshape(nl, 9 * d)
    grads["conv_w"] = lax.dynamic_slice_in_dim(tot["conv_w"], kme * conv_w.shape[2], conv_w.shape[2], axis=2)
    grads["s5_glu_w"] = lax.dynamic_slice_in_dim(tot["s5_glu_w"], kme * s5_glu_w.shape[1], s5_glu_w.shape[1], axis=1)

    dcond_all = gathered_small.reshape(N_DEV, -1)[:, :dcond.size].reshape(N_DEV, nl, 9 * d)
    dcond_loc = jnp.swapaxes(lax.dynamic_slice_in_dim(dcond_all, kme * n_ada, n_ada, axis=2), 0, 1)
    g_ada, d_ada, m_ada, v_ada = _ada_w_update(cact_all, dcond_loc, ada_w, m_ada_w, v_ada_w)

    while scattering or sharing:
        retire(g_ada)
    delta, new_m, new_v = {}, {}, {}
    for n in BIG_NAMES:
        grads[n], delta[n], new_m[n], new_v[n] = (jnp.swapaxes(a, 1, 2) if n == "w_mix_in" else a for a in stacks[n])

    grads["ada_w"], delta["ada_w"], new_m["ada_w"], new_v["ada_w"] = g_ada, d_ada, m_ada, v_ada
    wide = ("s5_b_re", "s5_b_im")
    for names, call, steps in (([n for n in SMALL_NAMES if n not in wide], "adamw_small", 1),
                               (list(wide), "adamw_s5_b", DEPTH)):
        outs = _adamw_many(*[[_as_rows(t[n]) for n in names] for t in (w, grads, mom, vel)], call, steps)
        for res, o in zip((delta, new_m, new_v), outs):
            res.update({n: a.reshape(w[n].shape) for n, a in zip(names, o)})

    loss_total = lax.psum(loss[0, 0], ("x", "y", "c"))
    return (loss_total, dx[None], *[grads[n] for n in WEIGHT_ORDER], *[delta[n] for n in WEIGHT_ORDER],
            *[new_m[n] for n in WEIGHT_ORDER], *[new_v[n] for n in WEIGHT_ORDER])
```

```python
import math

import jax
import jax.numpy as jnp
from jax import lax
from jax.experimental import pallas as pl
from jax.experimental.pallas import tpu as pltpu

F32, BF16 = jnp.float32, jnp.bfloat16
EPS = 1e-6
DEPTH = 4
N_DEV = 8
N_CHIP = 4
W_GRP = 256
CHUNK = 128
N_SEG = 8
LANES = 128
FFN_TF = 256
FFN_TF_WIDE = 1408
FFN_TM_WIDE = 512
VMEM_LIMIT = 56 * 1024 * 1024
ADAM_LR, ADAM_B1, ADAM_B2, ADAM_EPS, ADAM_WD, ADAM_STEP = 0.001, 0.9, 0.999, 1e-08, 0.01, 10
MESH_ID = pl.DeviceIdType.MESH
HI = lax.Precision.HIGHEST
ANY = pl.BlockSpec(memory_space=pl.ANY)


def _cp(**kw):
    return pltpu.CompilerParams(vmem_limit_bytes=VMEM_LIMIT, **kw)


def _dot(a, b):
    return jnp.dot(a.astype(BF16), b.astype(BF16), preferred_element_type=F32)


def _dot_nt(a, b):
    return lax.dot_general(a.astype(BF16), b.astype(BF16), (((1,), (1,)), ((), ())), preferred_element_type=F32)


def _dot_tn(a, b):
    return lax.dot_general(a.astype(BF16), b.astype(BF16), (((0,), (0,)), ((), ())), preferred_element_type=F32)


def _dot_hi(a, b):
    return jnp.dot(a, b, preferred_element_type=F32, precision=HI)


def _gelu(x):
    k = 0.7978845608028654
    t = jnp.tanh(k * (x + 0.044715 * x * x * x))
    return 0.5 * x * (1.0 + t), t


def _gelu_grad(x, t):
    k = 0.7978845608028654
    return 0.5 * (1.0 + t) + 0.5 * x * (1.0 - t * t) * k * (1.0 + 3.0 * 0.044715 * x * x)


def _iota(shape, axis):
    return lax.broadcasted_iota(jnp.int32, shape, axis)


def _full(shape):
    nd = len(shape)
    return pl.BlockSpec(shape, lambda *_: (0,) * nd)


def _norm_mod(xv, g, shift, scale):
    r = lax.rsqrt(jnp.mean(xv * xv, axis=-1, keepdims=True) + EPS)
    return (xv * r * g) * (1.0 + scale) + shift


def _norm_mod_bwd(xv, g, scale, dh):
    r = lax.rsqrt(jnp.mean(xv * xv, axis=-1, keepdims=True) + EPS)
    xh = xv * r
    n = xh * g
    dsh = jnp.sum(dh, axis=0, keepdims=True)
    dsc = jnp.sum(dh * n, axis=0, keepdims=True)
    dn = dh * (1.0 + scale)
    dg = jnp.sum(dn * xh, axis=0, keepdims=True)
    dxh = dn * g
    dx = r * (dxh - xh * jnp.mean(dxh * xh, axis=-1, keepdims=True))
    return dx, dsh, dsc, dg


def _tm(s):
    return min(s, 1024)


def _ffn_fwd(x, mod, g, wi, wo):
    s, d = x.shape
    f = wo.shape[0]
    tf, tm = FFN_TF_WIDE, min(s, FFN_TM_WIDE)
    nf, nt = f // tf, s // tm

    def body(x_ref, mod_ref, g_ref, wa_ref, wb_ref, wo_ref, xn_ref, h_ref, a_ref, b_ref, o_ref, acc):
        j = pl.program_id(1)

        @pl.when(j == 0)
        def _():
            hh = _norm_mod(x_ref[...], g_ref[...], mod_ref[0:1, :], mod_ref[1:2, :])
            h_ref[...] = hh.astype(BF16)
            acc[...] = jnp.zeros_like(acc)

        h = h_ref[...]
        a = jnp.dot(h, wa_ref[...], preferred_element_type=F32)
        b = jnp.dot(h, wb_ref[...], preferred_element_type=F32)
        a_ref[...] = a.astype(BF16)
        b_ref[...] = b.astype(BF16)
        u = (a * jax.nn.sigmoid(a)) * b
        acc[...] += jnp.dot(u.astype(BF16), wo_ref[...], preferred_element_type=F32)

        @pl.when(j == nf - 1)
        def _():
            o = acc[...]
            o_ref[...] = o.astype(BF16)
            xn_ref[...] = x_ref[...] + 0.5 * mod_ref[2:3, :] * o

    row = pl.BlockSpec((tm, d), lambda i, j: (i, 0))
    chunk = pl.BlockSpec((tm, tf), lambda i, j: (i, j))
    return pl.pallas_call(
        body, name="ffn_fwd", grid=(nt, nf),
        in_specs=[row, _full((3, d)), _full((1, d)),
                  pl.BlockSpec((None, d, tf), lambda i, j: (0, 0, j)),
                  pl.BlockSpec((None, d, tf), lambda i, j: (1, 0, j)),
                  pl.BlockSpec((tf, d), lambda i, j: (j, 0))],
        out_specs=[row, row, chunk, chunk, row],
        out_shape=[jax.ShapeDtypeStruct((s, d), F32), jax.ShapeDtypeStruct((s, d), BF16),
                   jax.ShapeDtypeStruct((s, f), BF16), jax.ShapeDtypeStruct((s, f), BF16),
                   jax.ShapeDtypeStruct((s, d), BF16)],
        scratch_shapes=[pltpu.VMEM((tm, d), F32)],
        compiler_params=_cp(dimension_semantics=("parallel", "arbitrary")),
    )(x, mod, g, wi, wi, wo)


def _ffn_bwd_main(dxo, o, gate, h, a, b, wo):
    s, d = dxo.shape
    f = wo.shape[0]
    tf = FFN_TF
    nf = f // tf

    def body(dxo_ref, o_hbm, gate_ref, h_ref, a_ref, b_ref, wo_ref, dza_ref, dzb_ref, dwi_ref, dwo_ref, dg_ref,
             do_s, o_s, sem):
        j = pl.program_id(0)
        o_copy = pltpu.make_async_copy(o_hbm, o_s, sem)

        @pl.when(j == 0)
        def _():
            o_copy.start()
            do_s[...] = (0.5 * gate_ref[...] * dxo_ref[...]).astype(BF16)

        dov = do_s[...]
        hv = h_ref[...]
        du = lax.dot_general(dov, wo_ref[...], (((1,), (1,)), ((), ())), preferred_element_type=F32)
        av = a_ref[...].astype(F32)
        bv = b_ref[...].astype(F32)
        sa = jax.nn.sigmoid(av)
        si = av * sa
        u = (si * bv).astype(BF16)
        da = (du * bv * (sa * (1.0 + av * (1.0 - sa)))).astype(BF16)
        db = (du * si).astype(BF16)
        dza_ref[...] = da
        dzb_ref[...] = db
        dwo_ref[...] = _dot_tn(u, dov).astype(BF16)
        dwi_ref[0] = _dot_tn(hv, da).astype(BF16)
        dwi_ref[1] = _dot_tn(hv, db).astype(BF16)

        @pl.when(j == nf - 1)
        def _():
            o_copy.wait()
            dg_ref[...] = 0.5 * jnp.sum(o_s[...].astype(F32) * dxo_ref[...], axis=0, keepdims=True)

    chunk = pl.BlockSpec((s, tf), lambda j: (0, j))
    once = lambda: pl.BlockSpec((s, d), lambda j: (0, 0), pipeline_mode=pl.Buffered(1))
    return pl.pallas_call(
        body, name="ffn_bwd_main", grid=(nf,),
        in_specs=[once(), ANY, _full((1, d)), once(), chunk, chunk, pl.BlockSpec((tf, d), lambda j: (j, 0))],
        out_specs=[chunk, chunk, pl.BlockSpec((2, d, tf), lambda j: (0, 0, j)),
                   pl.BlockSpec((tf, d), lambda j: (j, 0)), _full((1, d))],
        out_shape=[jax.ShapeDtypeStruct((s, f), BF16), jax.ShapeDtypeStruct((s, f), BF16),
                   jax.ShapeDtypeStruct((2, d, f), BF16), jax.ShapeDtypeStruct((f, d), BF16),
                   jax.ShapeDtypeStruct((1, d), F32)],
        scratch_shapes=[pltpu.VMEM((s, d), BF16), pltpu.VMEM((s, d), BF16), pltpu.SemaphoreType.DMA(())],
        compiler_params=_cp(dimension_semantics=("arbitrary",)),
    )(dxo, o, gate, h, a, b, wo)


def _ffn_bwd_in(dza, dzb, wi, x, dxo, mod, g):
    s, d = x.shape
    f = dza.shape[1]
    tf, tm = FFN_TF_WIDE, min(s, FFN_TM_WIDE)
    nf, nt = f // tf, s // tm

    def body(dza_ref, dzb_ref, wa_ref, wb_ref, x_ref, dxo_ref, mod_ref, g_ref, dx_ref, rows_ref, acc):
        j, i = pl.program_id(0), pl.program_id(1)
        rows = pl.ds(pl.multiple_of(i * tm, tm), tm)

        @pl.when(jnp.logical_and(i == 0, j == 0))
        def _():
            rows_ref[...] = jnp.zeros_like(rows_ref)

        part = (lax.dot_general(dza_ref[...], wa_ref[...], (((1,), (1,)), ((), ())), preferred_element_type=F32)
                + lax.dot_general(dzb_ref[...], wb_ref[...], (((1,), (1,)), ((), ())), preferred_element_type=F32))

        @pl.when(j == 0)
        def _():
            acc[rows, :] = part

        @pl.when(jnp.logical_and(j > 0, j < nf - 1))
        def _():
            acc[rows, :] += part

        @pl.when(j == nf - 1)
        def _():
            dh = part + acc[rows, :] if nf > 1 else part
            dx, dsh, dsc, dg = _norm_mod_bwd(x_ref[...], g_ref[...], mod_ref[1:2, :], dh)
            dx_ref[...] = dx + dxo_ref[...]
            rows_ref[0:1, :] += dsh
            rows_ref[1:2, :] += dsc
            rows_ref[2:3, :] += dg

    late = pl.BlockSpec((tm, d), lambda j, i: (jnp.where(j == nf - 1, i, 0), 0))
    chunk = pl.BlockSpec((tm, tf), lambda j, i: (i, j))
    return pl.pallas_call(
        body, name="ffn_bwd_in", grid=(nf, nt),
        in_specs=[chunk, chunk,
                  pl.BlockSpec((None, d, tf), lambda j, i: (0, 0, j)),
                  pl.BlockSpec((None, d, tf), lambda j, i: (1, 0, j)),
                  late, late, _full((3, d)), _full((1, d))],
        out_specs=[late, _full((8, d))],
        out_shape=[jax.ShapeDtypeStruct((s, d), F32), jax.ShapeDtypeStruct((8, d), F32)],
        scratch_shapes=[pltpu.VMEM((s, d), F32)],
        compiler_params=_cp(dimension_semantics=("arbitrary", "arbitrary")),
    )(dza, dzb, wi, wi, x, dxo, mod, g)


def _mix_in_fwd(x, mod, g, wmit):
    s, d = x.shape
    p = wmit.shape[0]
    tm = _tm(s)

    def body(x_ref, mod_ref, g_ref, w_ref, z_ref, h_ref):
        hh = _norm_mod(x_ref[...], g_ref[...], mod_ref[0:1, :], mod_ref[1:2, :]).astype(BF16)
        h_ref[...] = hh
        z_ref[...] = lax.dot_general(hh, w_ref[...], (((1,), (1,)), ((), ())), preferred_element_type=F32)

    row = pl.BlockSpec((tm, d), lambda i: (i, 0))
    return pl.pallas_call(
        body, name="mix_in_fwd", grid=(s // tm,),
        in_specs=[row, _full((3, d)), _full((1, d)), _full((p, d))],
        out_specs=[pl.BlockSpec((tm, p), lambda i: (i, 0)), row],
        out_shape=[jax.ShapeDtypeStruct((s, p), F32), jax.ShapeDtypeStruct((s, d), BF16)],
        compiler_params=_cp(dimension_semantics=("parallel",)),
    )(x, mod, g, wmit)


def _mix_in_bwd(dzs, h, wmit, x, dxo, mod, g):
    s, d = x.shape
    p = wmit.shape[0]
    tm = min(s, 512)
    nt = s // tm

    def body(za_ref, zb_ref, zc_ref, zd_ref, h_ref, w_ref, x_ref, dxo_ref, mod_ref, g_ref,
             dx_ref, rows_ref, dw_ref, acc):
        i = pl.program_id(0)

        @pl.when(i == 0)
        def _():
            rows_ref[...] = jnp.zeros_like(rows_ref)
            acc[...] = jnp.zeros_like(acc)

        dz = jnp.concatenate([za_ref[...], zb_ref[...], zc_ref[...], zd_ref[...]], axis=1).astype(BF16)
        acc[...] += _dot_tn(dz, h_ref[...])
        dh = jnp.dot(dz, w_ref[...], preferred_element_type=F32)
        dx, dsh, dsc, dg = _norm_mod_bwd(x_ref[...], g_ref[...], mod_ref[1:2, :], dh)
        dx_ref[...] = dx + dxo_ref[...]
        rows_ref[0:1, :] += dsh
        rows_ref[1:2, :] += dsc
        rows_ref[2:3, :] += dg

        @pl.when(i == nt - 1)
        def _():
            dw_ref[...] = acc[...].astype(BF16)

    row = pl.BlockSpec((tm, d), lambda i: (i, 0))
    zspecs = [pl.BlockSpec((tm, z.shape[1]), lambda i: (i, 0)) for z in dzs]
    return pl.pallas_call(
        body, name="mix_in_bwd", grid=(nt,),
        in_specs=zspecs + [row, _full((p, d)), row, row, _full((3, d)), _full((1, d))],
        out_specs=[row, _full((8, d)), _full((p, d))],
        out_shape=[jax.ShapeDtypeStruct((s, d), F32), jax.ShapeDtypeStruct((8, d), F32),
                   jax.ShapeDtypeStruct((p, d), BF16)],
        scratch_shapes=[pltpu.VMEM((p, d), F32)],
        compiler_params=_cp(dimension_semantics=("arbitrary",)),
    )(*dzs, h, wmit, x, dxo, mod, g)


def _group_norm(ys, mng):
    outs, hats, rs = [], [], []
    for k, y in enumerate(ys):
        r = lax.rsqrt(jnp.mean(y * y, axis=-1, keepdims=True) + EPS)
        yh = y * r
        hats.append(yh)
        rs.append(r)
        outs.append(yh * mng[:, k * W_GRP:(k + 1) * W_GRP])
    return jnp.concatenate(outs, axis=1), hats, rs


def _s5_glu(y, gw, gb):
    yg, t = _gelu(y)
    gate = jax.nn.sigmoid(_dot(yg, gw) + gb)
    return yg * gate, yg, t, gate


def _mix_out_fwd(ys, glu, mng, wmo, x, gate):
    s, d = x.shape
    tm = _tm(s)

    def body(ya, yb, yc, ypre, gw_ref, gb_ref, mng_ref, w_ref, x_ref, gate_ref, xn_ref, m_ref):
        yd = _s5_glu(ypre[...], gw_ref[...], gb_ref[...])[0]
        yn, _, _ = _group_norm([ya[...], yb[...], yc[...], yd], mng_ref[...])
        m = jnp.dot(yn.astype(BF16), w_ref[...], preferred_element_type=F32)
        m_ref[...] = m
        xn_ref[...] = x_ref[...] + gate_ref[...] * m

    row = pl.BlockSpec((tm, d), lambda i: (i, 0))
    grp = pl.BlockSpec((tm, W_GRP), lambda i: (i, 0))
    return pl.pallas_call(
        body, name="mix_out_fwd", grid=(s // tm,),
        in_specs=[grp, grp, grp, grp, _full((W_GRP, W_GRP)), _full((1, W_GRP)), _full((1, d)), _full((d, d)), row,
                  _full((1, d))],
        out_specs=[row, row],
        out_shape=[jax.ShapeDtypeStruct((s, d), F32), jax.ShapeDtypeStruct((s, d), F32)],
        compiler_params=_cp(dimension_semantics=("parallel",)),
    )(*ys, *glu, mng, wmo, x, gate)


def _mix_out_bwd(dxo, m, gate, ys, glu, mng, wmo):
    s, d = dxo.shape
    tm = min(s, 512)
    nt = s // tm

    def body(dxo_ref, m_ref, gate_ref, ya, yb, yc, ypre, gw_ref, gb_ref, mng_ref, w_ref,
             dya, dyb, dyc, dypre, dgate_ref, dmng_ref, dw_ref, dgw_ref, dgb_ref, acc):
        i = pl.program_id(0)

        @pl.when(i == 0)
        def _():
            dgate_ref[...] = jnp.zeros_like(dgate_ref)
            dmng_ref[...] = jnp.zeros_like(dmng_ref)
            dgw_ref[...] = jnp.zeros_like(dgw_ref)
            dgb_ref[...] = jnp.zeros_like(dgb_ref)
            acc[...] = jnp.zeros_like(acc)

        dxv = dxo_ref[...]
        dgate_ref[...] += jnp.sum(m_ref[...] * dxv, axis=0, keepdims=True)
        dm = (gate_ref[...] * dxv).astype(BF16)
        mng = mng_ref[...]
        gw = gw_ref[...]
        yp = ypre[...]
        yd, yg, t, glu_gate = _s5_glu(yp, gw, gb_ref[...])
        yn, hats, rs = _group_norm([ya[...], yb[...], yc[...], yd], mng)
        acc[...] += _dot_tn(yn, dm)
        dyn = lax.dot_general(dm, w_ref[...], (((1,), (1,)), ((), ())), preferred_element_type=F32)
        dmng_parts, dys = [], []
        for k, (yh, r) in enumerate(zip(hats, rs)):
            dk = dyn[:, k * W_GRP:(k + 1) * W_GRP]
            dmng_parts.append(jnp.sum(dk * yh, axis=0, keepdims=True))
            dyh = dk * mng[:, k * W_GRP:(k + 1) * W_GRP]
            dys.append(r * (dyh - yh * jnp.mean(dyh * yh, axis=-1, keepdims=True)))
        dmng_ref[...] += jnp.concatenate(dmng_parts, axis=1)
        dya[...], dyb[...], dyc[...] = dys[0], dys[1], dys[2]
        dyd = dys[3]
        dlin = dyd * yg * glu_gate * (1.0 - glu_gate)
        dgw_ref[...] += _dot_tn(yg, dlin)
        dgb_ref[...] += jnp.sum(dlin, axis=0, keepdims=True)
        dypre[...] = (dyd * glu_gate + _dot_nt(dlin, gw)) * _gelu_grad(yp, t)

        @pl.when(i == nt - 1)
        def _():
            dw_ref[...] = acc[...].astype(BF16)

    row = pl.BlockSpec((tm, d), lambda i: (i, 0))
    grp = pl.BlockSpec((tm, W_GRP), lambda i: (i, 0))
    return pl.pallas_call(
        body, name="mix_out_bwd", grid=(nt,),
        in_specs=[row, row, _full((1, d)), grp, grp, grp, grp, _full((W_GRP, W_GRP)), _full((1, W_GRP)), _full((1, d)),
                  _full((d, d))],
        out_specs=[grp, grp, grp, grp, _full((1, d)), _full((1, d)), _full((d, d)), _full((W_GRP, W_GRP)),
                   _full((1, W_GRP))],
        out_shape=[jax.ShapeDtypeStruct((s, W_GRP), F32)] * 4
        + [jax.ShapeDtypeStruct((1, d), F32), jax.ShapeDtypeStruct((1, d), F32), jax.ShapeDtypeStruct((d, d), BF16),
           jax.ShapeDtypeStruct((W_GRP, W_GRP), F32), jax.ShapeDtypeStruct((1, W_GRP), F32)],
        scratch_shapes=[pltpu.VMEM((d, d), F32)],
        compiler_params=_cp(dimension_semantics=("arbitrary",)),
    )(dxo, m, gate, *ys, *glu, mng, wmo)


def _sgu_consts():
    r = _iota((W_GRP, W_GRP), 0) >> 6
    c = _iota((W_GRP, W_GRP), 1) >> 6
    avg = jnp.where(r == c, 1.0 / 64.0, 0.0).astype(F32)
    tril = _iota((CHUNK, CHUNK), 0) >= _iota((CHUNK, CHUNK), 1)
    head = _iota((CHUNK, W_GRP), 1) >> 6
    return avg, tril, head


def _sgu_pre(za, avg):
    zg, t = _gelu(za)
    u, v = zg[:, :W_GRP], zg[:, W_GRP:]
    mu = _dot_hi(v, avg)
    vc = v - mu
    r = lax.rsqrt(_dot_hi(vc * vc, avg) + EPS)
    return t, u, vc * r, r


def _sgu_fwd(z, sgu_w, bias_full):
    s = z.shape[0]
    tm = min(s, 512)

    def body(za_ref, w_ref, bias_ref, ya_ref):
        avg, tril, head = _sgu_consts()
        _, u, vn, _ = _sgu_pre(za_ref[...], avg)
        wm = [jnp.where(tril, w_ref[h], 0.0).astype(BF16) for h in range(4)]
        vb = vn.astype(BF16)
        for n in range(tm // CHUNK):
            rows = slice(n * CHUNK, (n + 1) * CHUNK)
            mixed = bias_ref[...]
            for h in range(4):
                mixed = mixed + jnp.where(head == h, jnp.dot(wm[h], vb[rows], preferred_element_type=F32), 0.0)
            ya_ref[rows, :] = u[rows] * mixed

    return pl.pallas_call(
        body, name="sgu_fwd", grid=(s // tm,),
        in_specs=[pl.BlockSpec((tm, 2 * W_GRP), lambda i: (i, 0)), _full((4, CHUNK, CHUNK)), _full((CHUNK, W_GRP))],
        out_specs=pl.BlockSpec((tm, W_GRP), lambda i: (i, 0)),
        out_shape=jax.ShapeDtypeStruct((s, W_GRP), F32),
        compiler_params=_cp(dimension_semantics=("parallel",)),
    )(z, sgu_w, bias_full)


def _sgu_bwd(z, dya, sgu_w, bias_full):
    s = z.shape[0]
    tm = min(s, 512)
    nt = s // tm

    def body(za_ref, dya_ref, w_ref, bias_ref, dza_ref, dw_ref, db_ref, du_s, dvn_s):
        i = pl.program_id(0)

        @pl.when(i == 0)
        def _():
            dw_ref[...] = jnp.zeros_like(dw_ref)
            db_ref[...] = jnp.zeros_like(db_ref)

        avg, tril, head = _sgu_consts()
        za = za_ref[...]
        t, u, vn, r = _sgu_pre(za, avg)
        wm = [jnp.where(tril, w_ref[h], 0.0).astype(BF16) for h in range(4)]
        vb = vn.astype(BF16)
        dya = dya_ref[...]
        dw = [jnp.zeros((CHUNK, CHUNK), F32) for _ in range(4)]
        db = jnp.zeros((CHUNK, W_GRP), F32)
        for n in range(tm // CHUNK):
            rows = slice(n * CHUNK, (n + 1) * CHUNK)
            mixed = bias_ref[...]
            for h in range(4):
                mixed = mixed + jnp.where(head == h, jnp.dot(wm[h], vb[rows], preferred_element_type=F32), 0.0)
            dmix = dya[rows] * u[rows]
            du_s[rows, :] = dya[rows] * mixed
            db = db + dmix
            dmb = dmix.astype(BF16)
            dvn = jnp.zeros((CHUNK, W_GRP), F32)
            for h in range(4):
                dmh = jnp.where(head == h, dmix, 0.0)
                dw[h] = dw[h] + _dot_nt(dmh, vb[rows])
                dvn = dvn + jnp.where(head == h, _dot_tn(wm[h], dmb), 0.0)
            dvn_s[rows, :] = dvn
        for h in range(4):
            dw_ref[h] += jnp.where(tril, dw[h], 0.0)
        sel = ((_iota((W_GRP, CHUNK), 0) >> 6) == _iota((W_GRP, CHUNK), 1)).astype(F32)
        db_ref[...] += _dot_hi(db, sel)
        dvn = dvn_s[...]
        dv = r * (dvn - _dot_hi(dvn, avg) - vn * _dot_hi(dvn * vn, avg))
        dzg = jnp.concatenate([du_s[...], dv], axis=1)
        dza_ref[...] = dzg * _gelu_grad(za, t)

    return pl.pallas_call(
        body, name="sgu_bwd", grid=(nt,),
        in_specs=[pl.BlockSpec((tm, 2 * W_GRP), lambda i: (i, 0)), pl.BlockSpec((tm, W_GRP), lambda i: (i, 0)),
                  _full((4, CHUNK, CHUNK)), _full((CHUNK, W_GRP))],
        out_specs=[pl.BlockSpec((tm, 2 * W_GRP), lambda i: (i, 0)), _full((4, CHUNK, CHUNK)), _full((CHUNK, CHUNK))],
        out_shape=[jax.ShapeDtypeStruct((s, 2 * W_GRP), F32), jax.ShapeDtypeStruct((4, CHUNK, CHUNK), F32),
                   jax.ShapeDtypeStruct((CHUNK, CHUNK), F32)],
        scratch_shapes=[pltpu.VMEM((tm, W_GRP), F32), pltpu.VMEM((tm, W_GRP), F32)],
        compiler_params=_cp(dimension_semantics=("arbitrary",)),
    )(z, dya, sgu_w, bias_full)


def _shift_down(x, k):
    return jnp.where(_iota(x.shape, 0) < k, 0.0, pltpu.roll(x, k, 0))


def _shift_up(x, k):
    n = x.shape[0]
    return jnp.where(_iota(x.shape, 0) >= n - k, 0.0, pltpu.roll(x, n - k, 0))


def _by_pool_group(shape, v2, v4, v8, v16):
    col = _iota(shape, 1)
    return jnp.where(col < 64, v2, jnp.where(col < 128, v4, jnp.where(col < 192, v8, v16)))


def _pool_core(zb, pw2):
    s2 = zb + _shift_down(zb, 1)
    s4 = s2 + _shift_down(s2, 2)
    s8 = s4 + _shift_down(s4, 4)
    s16 = s8 + _shift_down(s8, 8)
    win = _by_pool_group(zb.shape, s2, s4, s8, s16)
    wlen = _by_pool_group(zb.shape, 2.0, 4.0, 8.0, 16.0)
    cnt = jnp.minimum((_iota(zb.shape, 0) + 1).astype(F32), wlen)
    p = win / cnt - zb
    wt = jnp.tile(pw2, (1, 4))
    wbd = jnp.where((_iota(wt.shape, 0) >> 6) == (_iota(wt.shape, 1) >> 6), wt, 0.0).astype(BF16)
    return p, cnt, wbd


def _conv_core(zc, cw):
    bg, cg, xh = zc[:, :W_GRP], zc[:, W_GRP:2 * W_GRP], zc[:, 2 * W_GRP:]
    y = cg * xh
    y1, y2 = _shift_down(y, 1), _shift_down(y, 2)
    out = cw[2:3, :] * y + cw[1:2, :] * y1 + cw[0:1, :] * y2
    return bg, cg, xh, y, y1, y2, out


def _poolconv_fwd(z, pw2, pscale, cw):
    s = z.shape[0]

    def body(zb_ref, zc_ref, pw_ref, ps_ref, cw_ref, yb_ref, yc_ref):
        p, _, wbd = _pool_core(zb_ref[...], pw_ref[...])
        yb_ref[...] = jnp.dot(p.astype(BF16), wbd, preferred_element_type=F32) * ps_ref[...]
        bg, _, _, _, _, _, out = _conv_core(zc_ref[...], cw_ref[...])
        yc_ref[...] = bg * out

    return pl.pallas_call(
        body, name="poolconv_fwd", grid=(1,),
        in_specs=[pl.BlockSpec((s, W_GRP), lambda i: (0, 2)), pl.BlockSpec((s, 3 * W_GRP), lambda i: (0, 1)),
                  _full((W_GRP, 64)), _full((1, W_GRP)), _full((3, W_GRP))],
        out_specs=[_full((s, W_GRP)), _full((s, W_GRP))],
        out_shape=[jax.ShapeDtypeStruct((s, W_GRP), F32)] * 2,
        compiler_params=_cp(dimension_semantics=("arbitrary",)),
    )(z, z, pw2, pscale, cw)


def _poolconv_bwd(z, dyb, dyc, pw2, pscale, cw):
    s = z.shape[0]

    def body(zb_ref, zc_ref, dyb_ref, dyc_ref, pw_ref, ps_ref, cw_ref, dzb_ref, dzc_ref, dw_ref, dps_ref, dcw_ref):
        zb = zb_ref[...]
        p, cnt, wbd = _pool_core(zb, pw_ref[...])
        pb = p.astype(BF16)
        out = jnp.dot(pb, wbd, preferred_element_type=F32)
        dyb = dyb_ref[...]
        dps_ref[...] = jnp.sum(dyb * out, axis=0, keepdims=True)
        dout = (dyb * ps_ref[...]).astype(BF16)
        dw = _dot_tn(pb, dout)
        dw_ref[...] = jnp.where((_iota(dw.shape, 0) >> 6) == (_iota(dw.shape, 1) >> 6), dw, 0.0)
        dp = lax.dot_general(dout, wbd, (((1,), (1,)), ((), ())), preferred_element_type=F32)
        dwin = dp / cnt
        t2 = dwin + _shift_up(dwin, 1)
        t4 = t2 + _shift_up(t2, 2)
        t8 = t4 + _shift_up(t4, 4)
        t16 = t8 + _shift_up(t8, 8)
        dzb_ref[...] = _by_pool_group(zb.shape, t2, t4, t8, t16) - dp

        cw = cw_ref[...]
        bg, cg, xh, y, y1, y2, out = _conv_core(zc_ref[...], cw)
        dyc = dyc_ref[...]
        dout = dyc * bg
        dcw_ref[...] = jnp.zeros_like(dcw_ref)
        dcw_ref[0:1, :] = jnp.sum(dout * y2, axis=0, keepdims=True)
        dcw_ref[1:2, :] = jnp.sum(dout * y1, axis=0, keepdims=True)
        dcw_ref[2:3, :] = jnp.sum(dout * y, axis=0, keepdims=True)
        dy = cw[2:3, :] * dout + cw[1:2, :] * _shift_up(dout, 1) + cw[0:1, :] * _shift_up(dout, 2)
        dzc_ref[...] = jnp.concatenate([dyc * out, dy * xh, dy * cg], axis=1)

    return pl.pallas_call(
        body, name="poolconv_bwd", grid=(1,),
        in_specs=[pl.BlockSpec((s, W_GRP), lambda i: (0, 2)), pl.BlockSpec((s, 3 * W_GRP), lambda i: (0, 1)),
                  _full((s, W_GRP)), _full((s, W_GRP)), _full((W_GRP, 64)), _full((1, W_GRP)), _full((3, W_GRP))],
        out_specs=[_full((s, W_GRP)), _full((s, 3 * W_GRP)), _full((W_GRP, W_GRP)), _full((1, W_GRP)), _full((8, W_GRP))],
        out_shape=[jax.ShapeDtypeStruct((s, W_GRP), F32), jax.ShapeDtypeStruct((s, 3 * W_GRP), F32),
                   jax.ShapeDtypeStruct((W_GRP, W_GRP), F32), jax.ShapeDtypeStruct((1, W_GRP), F32),
                   jax.ShapeDtypeStruct((8, W_GRP), F32)],
        compiler_params=_cp(dimension_semantics=("arbitrary",)),
    )(z, z, dyb, dyc, pw2, pscale, cw)


N_STATE = 1024
HALF_STATE = N_STATE // 2
HALF_CH = W_GRP // 2
N_SLAB = HALF_STATE // LANES


def _s5_disc(lre, lim, ldt):
    dt = jnp.exp(ldt)
    mag = jnp.exp(lre * dt)
    ang = lim * dt
    ar, ai = mag * jnp.cos(ang), mag * jnp.sin(ang)
    nr, ni = ar - 1.0, ai
    den = lre * lre + lim * lim
    kr = (nr * lre + ni * lim) / den
    ki = (ni * lre - nr * lim) / den
    return ar, ai, kr, ki


def _s5_mats(colp, br, bi, cr, ci):
    _, _, kr, ki = _s5_disc(colp[:, 0:1], colp[:, 1:2], colp[:, 2:3])
    bbr = kr * br - ki * bi
    bbi = kr * bi + ki * br
    bmask = (_iota((HALF_STATE, HALF_CH), 0) >> 6) == (_iota((HALF_STATE, HALF_CH), 1) >> 4)
    cmask = (_iota((HALF_CH, HALF_STATE), 0) >> 4) == (_iota((HALF_CH, HALF_STATE), 1) >> 6)
    btr = jnp.where(bmask, jnp.tile(bbr, (1, 8)), 0.0).astype(BF16)
    bti = jnp.where(bmask, jnp.tile(bbi, (1, 8)), 0.0).astype(BF16)
    ctr = jnp.where(cmask, jnp.tile(cr, (1, 8)), 0.0).astype(BF16)
    cti = jnp.where(cmask, jnp.tile(ci, (1, 8)), 0.0).astype(BF16)
    return kr, ki, btr, bti, ctr, cti, bmask, cmask


def _slab(q):
    return slice(q * LANES, (q + 1) * LANES)


def _cmul(ar, ai, br, bi):
    return ar * br - ai * bi, ar * bi + ai * br


def _sub_shift(x, k, up):
    row = _iota(x.shape, 0)
    if up:
        return jnp.where(row >= N_SEG - k, 0.0, pltpu.roll(x, N_SEG - k, 0))
    return jnp.where(row < k, 0.0, pltpu.roll(x, k, 0))


def _seg_rows(j):
    return pl.ds(pl.multiple_of(j * N_SEG, N_SEG), N_SEG)


def _interleave(src, dst, seg):
    def step(j, carry):
        dst[_seg_rows(j), :] = src[pl.ds(j, N_SEG, stride=seg), :]
        return carry
    lax.fori_loop(0, seg, step, 0)


def _deinterleave(src, dst, seg):
    def step(j, carry):
        dst[pl.ds(j, N_SEG, stride=seg), :] = src[_seg_rows(j), :]
        return carry
    lax.fori_loop(0, seg, step, 0)


def _scan(xr, xi, ar_row, ai_row, seg, reverse, states=None):
    nlog = int(math.log2(seg))
    assert (1 << nlog) == seg
    grads = []
    for q0 in range(0, N_SLAB, 4):
        qs = list(range(q0, q0 + 4))
        aq = [(jnp.broadcast_to(ar_row[:, _slab(q)], (N_SEG, LANES)),
               jnp.broadcast_to(ai_row[:, _slab(q)], (N_SEG, LANES))) for q in qs]
        zero = jnp.zeros((N_SEG, LANES), F32)

        def local(jj, carry, qs=qs, aq=aq):
            j = seg - 1 - jj if reverse else jj
            out = []
            for n, q in enumerate(qs):
                rows = _seg_rows(j)
                pr, pi = _cmul(aq[n][0], aq[n][1], carry[2 * n], carry[2 * n + 1])
                nr = pr + xr[q, rows, :]
                ni = pi + xi[q, rows, :]
                xr[q, rows, :] = nr
                xi[q, rows, :] = ni
                out += [nr, ni]
            return tuple(out)

        fin = lax.fori_loop(0, seg, local, (zero,) * 8)
        cins = []
        for n in range(4):
            er, ei = fin[2 * n], fin[2 * n + 1]
            pr, pi = aq[n]
            for _ in range(nlog):
                pr, pi = _cmul(pr, pi, pr, pi)
            yr, yi = er, ei
            for k in (1, 2, 4):
                sr, si = _cmul(pr, pi, _sub_shift(yr, k, reverse), _sub_shift(yi, k, reverse))
                yr, yi = yr + sr, yi + si
                pr, pi = _cmul(pr, pi, pr, pi)
            cins.append((_sub_shift(yr, 1, reverse), _sub_shift(yi, 1, reverse)))

        def fix(jj, carry, qs=qs, aq=aq, cins=cins):
            j = seg - 1 - jj if reverse else jj
            out, sums = [], []
            for n, q in enumerate(qs):
                rows = _seg_rows(j)
                pwr, pwi = carry[2 * n], carry[2 * n + 1]
                cr, ci = _cmul(pwr, pwi, cins[n][0], cins[n][1])
                v_r, v_i = xr[q, rows, :] + cr, xi[q, rows, :] + ci
                xr[q, rows, :] = v_r
                xi[q, rows, :] = v_i
                nr, ni = _cmul(pwr, pwi, aq[n][0], aq[n][1])
                out += [nr, ni]
                if states is not None:
                    prev = _seg_rows(j - 1)
                    p_r, p_i = states[0][q, prev, :], states[1][q, prev, :]
                    sums += [carry[8 + 2 * n] + v_r * p_r + v_i * p_i, carry[9 + 2 * n] - v_r * p_i + v_i * p_r]
            return tuple(out + sums)

        powers = tuple(v for pair in aq for v in pair)
        if states is None:
            lax.fori_loop(0, seg, fix, powers)
            continue
        assert reverse
        fix_last = lax.fori_loop(0, seg - 1, fix, powers + (zero,) * 8)
        first = _seg_rows(0)
        for n, q in enumerate(qs):
            cr, ci = _cmul(fix_last[2 * n], fix_last[2 * n + 1], cins[n][0], cins[n][1])
            v_r, v_i = xr[q, first, :] + cr, xi[q, first, :] + ci
            xr[q, first, :] = v_r
            xi[q, first, :] = v_i
            p_r = _sub_shift(states[0][q, _seg_rows(seg - 1), :], 1, False)
            p_i = _sub_shift(states[1][q, _seg_rows(seg - 1), :], 1, False)
            grads.append((jnp.sum(fix_last[8 + 2 * n] + v_r * p_r + v_i * p_i, axis=0, keepdims=True),
                          jnp.sum(fix_last[9 + 2 * n] - v_r * p_i + v_i * p_r, axis=0, keepdims=True)))
    return grads


def _s5_forward_states(u, btr, bti, ar_row, ai_row, xr, xi, seg):
    ub = u.astype(BF16)
    for q in range(N_SLAB):
        xr[q] = _dot_nt(ub, btr[_slab(q), :])
        xi[q] = _dot_nt(ub, bti[_slab(q), :])
    _scan(xr, xi, ar_row, ai_row, seg, False)


def _s5_readout(u, xr, xi, ctr, cti, d):
    y = d * u
    for q in range(N_SLAB):
        y = y + _dot_nt(xr[q], ctr[:, _slab(q)]) - _dot_nt(xi[q], cti[:, _slab(q)])
    return y


def _s5_param_specs():
    return [pl.BlockSpec((3, HALF_STATE), lambda i: (0, i)), pl.BlockSpec((HALF_STATE, 3), lambda i: (i, 0)),
            pl.BlockSpec((HALF_STATE, 16), lambda i: (i, 0)), pl.BlockSpec((HALF_STATE, 16), lambda i: (i, 0)),
            pl.BlockSpec((HALF_CH, 64), lambda i: (i, 0)), pl.BlockSpec((HALF_CH, 64), lambda i: (i, 0)),
            pl.BlockSpec((1, HALF_CH), lambda i: (0, i))]


def _s5_core_fwd(z, sp):
    s = z.shape[0]
    seg = s // N_SEG

    def body(u_ref, rowp, colp, br, bi, cr, ci, d_ref, y_ref, xr, xi, us, ys):
        ar, ai, _, _ = _s5_disc(rowp[0:1, :], rowp[1:2, :], rowp[2:3, :])
        _, _, btr, bti, ctr, cti, _, _ = _s5_mats(colp[...], br[...], bi[...], cr[...], ci[...])
        _interleave(u_ref, us, seg)
        u = us[...]
        _s5_forward_states(u, btr, bti, ar, ai, xr, xi, seg)
        ys[...] = _s5_readout(u, xr, xi, ctr, cti, d_ref[...])
        _deinterleave(ys, y_ref, seg)

    return pl.pallas_call(
        body, name="s5_core_fwd", grid=(2,),
        in_specs=[pl.BlockSpec((s, HALF_CH), lambda i: (0, 12 + i))] + _s5_param_specs(),
        out_specs=pl.BlockSpec((s, HALF_CH), lambda i: (0, i)),
        out_shape=jax.ShapeDtypeStruct((s, W_GRP), F32),
        scratch_shapes=[pltpu.VMEM((N_SLAB, s, LANES), F32)] * 2 + [pltpu.VMEM((s, HALF_CH), F32)] * 2,
        compiler_params=_cp(dimension_semantics=("parallel",)),
    )(z, *sp)


def _s5_core_bwd(z, dy, sp):
    s = z.shape[0]
    seg = s // N_SEG

    def body(u_ref, dy_ref, rowp, colp, br_ref, bi_ref, cr_ref, ci_ref, d_ref,
             du_ref, dbr_ref, dbi_ref, dcr_ref, dci_ref, dd_ref, da_ref, dk_ref,
             xr, xi, gr, gi, us, dys):
        ar, ai, _, _ = _s5_disc(rowp[0:1, :], rowp[1:2, :], rowp[2:3, :])
        br, bi = br_ref[...], bi_ref[...]
        kr, ki, btr, bti, ctr, cti, bmask, cmask = _s5_mats(colp[...], br, bi, cr_ref[...], ci_ref[...])
        _interleave(u_ref, us, seg)
        _interleave(dy_ref, dys, seg)
        u = us[...]
        d = d_ref[...]
        _s5_forward_states(u, btr, bti, ar, ai, xr, xi, seg)

        dy = dys[...]
        dd_ref[...] = jnp.sum(dy * u, axis=0, keepdims=True)
        du = d * dy
        dyb = dy.astype(BF16)
        dctr, dcti = [], []
        for q in range(N_SLAB):
            gr[q] = jnp.dot(dyb, ctr[:, _slab(q)], preferred_element_type=F32)
            gi[q] = -jnp.dot(dyb, cti[:, _slab(q)], preferred_element_type=F32)
            dctr.append(_dot_tn(dyb, xr[q]))
            dcti.append(-_dot_tn(dyb, xi[q]))
        selp = ((_iota((HALF_STATE, 64), 0) & 63) == _iota((HALF_STATE, 64), 1)).astype(F32)
        dcr_ref[...] = _dot_hi(jnp.where(cmask, jnp.concatenate(dctr, axis=1), 0.0), selp)
        dci_ref[...] = _dot_hi(jnp.where(cmask, jnp.concatenate(dcti, axis=1), 0.0), selp)

        da = _scan(gr, gi, ar, -ai, seg, True, states=(xr, xi))
        dar, dai = [p[0] for p in da], [p[1] for p in da]
        da_ref[...] = jnp.zeros_like(da_ref)
        da_ref[0:1, :] = jnp.concatenate(dar, axis=1)
        da_ref[1:2, :] = jnp.concatenate(dai, axis=1)

        ub = u.astype(BF16)
        dbtr, dbti = [], []
        for q in range(N_SLAB):
            g_r, g_i = gr[q].astype(BF16), gi[q].astype(BF16)
            du = du + jnp.dot(g_r, btr[_slab(q), :], preferred_element_type=F32) \
                + jnp.dot(g_i, bti[_slab(q), :], preferred_element_type=F32)
            dbtr.append(_dot_tn(g_r, ub))
            dbti.append(_dot_tn(g_i, ub))
        us[...] = du
        _deinterleave(us, du_ref, seg)
        selc =((_iota((HALF_CH, 16), 0) & 15) == _iota((HALF_CH, 16), 1)).astype(F32)
        dbbr = _dot_hi(jnp.where(bmask, jnp.concatenate(dbtr, axis=0), 0.0), selc)
        dbbi = _dot_hi(jnp.where(bmask, jnp.concatenate(dbti, axis=0), 0.0), selc)
        dbr_ref[...] = kr * dbbr + ki * dbbi
        dbi_ref[...] = kr * dbbi - ki * dbbr
        dk_ref[:, 0:1] = jnp.sum(dbbr * br + dbbi * bi, axis=1, keepdims=True)
        dk_ref[:, 1:2] = jnp.sum(dbbi * br - dbbr * bi, axis=1, keepdims=True)

    half = pl.BlockSpec((s, HALF_CH), lambda i: (0, i))
    return pl.pallas_call(
        body, name="s5_core_bwd", grid=(2,),
        in_specs=[pl.BlockSpec((s, HALF_CH), lambda i: (0, 12 + i)), half] + _s5_param_specs(),
        out_specs=[half, pl.BlockSpec((HALF_STATE, 16), lambda i: (i, 0)), pl.BlockSpec((HALF_STATE, 16), lambda i: (i, 0)),
                   pl.BlockSpec((HALF_CH, 64), lambda i: (i, 0)), pl.BlockSpec((HALF_CH, 64), lambda i: (i, 0)),
                   pl.BlockSpec((1, HALF_CH), lambda i: (0, i)), pl.BlockSpec((8, HALF_STATE), lambda i: (0, i)),
                   pl.BlockSpec((HALF_STATE, 2), lambda i: (i, 0))],
        out_shape=[jax.ShapeDtypeStruct((s, W_GRP), F32), jax.ShapeDtypeStruct((N_STATE, 16), F32),
                   jax.ShapeDtypeStruct((N_STATE, 16), F32), jax.ShapeDtypeStruct((W_GRP, 64), F32),
                   jax.ShapeDtypeStruct((W_GRP, 64), F32), jax.ShapeDtypeStruct((1, W_GRP), F32),
                   jax.ShapeDtypeStruct((8, N_STATE), F32), jax.ShapeDtypeStruct((N_STATE, 2), F32)],
        scratch_shapes=[pltpu.VMEM((N_SLAB, s, LANES), F32)] * 4 + [pltpu.VMEM((s, HALF_CH), F32)] * 2,
        compiler_params=_cp(dimension_semantics=("parallel",)),
    )(z, dy, *sp)


def _s5_param_bwd(lre, lim, ldt, da_r, da_i, dk_r, dk_i):
    n = lre.shape[0]

    def body(lre_ref, lim_ref, ldt_ref, dar_ref, dai_ref, dkr_ref, dki_ref, o_re, o_im, o_dt):
        lre, lim, ldt = lre_ref[...], lim_ref[...], ldt_ref[...]
        dt = jnp.exp(ldt)
        ar, ai, kr, ki = _s5_disc(lre, lim, ldt)
        mag = jnp.exp(lre * dt)
        den = lre * lre + lim * lim
        dkr, dki = dkr_ref[...], dki_ref[...]
        nr, ni = ar - 1.0, ai
        d_ar = dar_ref[...] + (dkr * lre - dki * lim) / den
        d_ai = dai_ref[...] + (dkr * lim + dki * lre) / den
        kk = (kr * dkr + ki * dki) * 2.0 / den
        d_lre = (dkr * nr + dki * ni) / den - kk * lre
        d_lim = (dkr * ni - dki * nr) / den - kk * lim
        d_mag = (d_ar * ar + d_ai * ai) / mag
        d_ang = d_ai * ar - d_ar * ai
        o_re[...] = d_lre + d_mag * mag * dt
        o_im[...] = d_lim + d_ang * dt
        o_dt[...] = jnp.sum((d_mag * mag * lre + d_ang * lim) * dt, axis=1, keepdims=True)

    return pl.pallas_call(
        body, name="s5_param_bwd",
        out_shape=[jax.ShapeDtypeStruct((n, 64), F32), jax.ShapeDtypeStruct((n, 64), F32),
                   jax.ShapeDtypeStruct((n, 1), F32)],
    )(lre, lim, ldt, da_r, da_i, dk_r, dk_i)


def _loss_head(x, fg, target):
    s, d = x.shape
    tm = _tm(s)

    def body(x_ref, fg_ref, t_ref, loss_ref, dx_ref, dfg_ref):
        i = pl.program_id(0)

        @pl.when(i == 0)
        def _():
            loss_ref[...] = jnp.zeros_like(loss_ref)
            dfg_ref[...] = jnp.zeros_like(dfg_ref)

        xv, g = x_ref[...], fg_ref[...]
        r = lax.rsqrt(jnp.mean(xv * xv, axis=-1, keepdims=True) + EPS)
        xh = xv * r
        err = xh * g - t_ref[...]
        loss_ref[...] += 0.5 * jnp.sum(jnp.mean(err * err, axis=-1, keepdims=True), axis=0, keepdims=True)
        dy = err * (1.0 / d)
        dfg_ref[...] += jnp.sum(dy * xh, axis=0, keepdims=True)
        dxh = dy * g
        dx_ref[...] = r * (dxh - xh * jnp.mean(dxh * xh, axis=-1, keepdims=True))

    row = pl.BlockSpec((tm, d), lambda i: (i, 0))
    return pl.pallas_call(
        body, name="loss_head", grid=(s // tm,),
        in_specs=[row, _full((1, d)), row], out_specs=[_full((1, 1)), row, _full((1, d))],
        out_shape=[jax.ShapeDtypeStruct((1, 1), F32), jax.ShapeDtypeStruct((s, d), F32),
                   jax.ShapeDtypeStruct((1, d), F32)],
        compiler_params=_cp(dimension_semantics=("arbitrary",)),
    )(x, fg, target)


ADA_TN = 384


def _cond_fwd(cact, ada_w, ada_b_loc):
    nl, d, n = ada_w.shape

    def body(c_ref, w_ref, b_ref, o_ref):
        o_ref[...] = _dot(c_ref[...], w_ref[...]) + b_ref[...]

    return pl.pallas_call(
        body, name="cond_fwd", grid=(nl, n // ADA_TN),
        in_specs=[_full((N_DEV, d)), pl.BlockSpec((None, d, ADA_TN), lambda l, j: (l, 0, j)),
                  pl.BlockSpec((None, 1, ADA_TN), lambda l, j: (l, 0, j))],
        out_specs=pl.BlockSpec((None, N_DEV, ADA_TN), lambda l, j: (l, 0, j)),
        out_shape=jax.ShapeDtypeStruct((nl, N_DEV, n), F32),
        compiler_params=_cp(dimension_semantics=("parallel", "parallel")),
    )(cact, ada_w, ada_b_loc)


ELEMENTWISE_BLOCK_BYTES = 1 << 20


def _row_tile(r, c, itemsize=4):
    best = None
    for t in range(8, r + 1, 8):
        if r % t == 0 and t * c * itemsize <= ELEMENTWISE_BLOCK_BYTES:
            best = t
    return best if best is not None else r


def _adamw_math(w, g, m, v):
    m = ADAM_B1 * m + (1.0 - ADAM_B1) * g
    v = ADAM_B2 * v + (1.0 - ADAM_B2) * (g * g)
    m_hat = m / (1.0 - ADAM_B1 ** ADAM_STEP)
    v_hat = v / (1.0 - ADAM_B2 ** ADAM_STEP)
    delta = -ADAM_LR * (m_hat / (jnp.sqrt(v_hat) + ADAM_EPS) + ADAM_WD * w)
    return delta, m, v


def _ada_w_update(cact, dcond_loc, w, m, v):
    nl, d, n = w.shape

    def body(c_ref, dc_ref, w_ref, m_ref, v_ref, g_out, d_out, m_out, v_out):
        g = _dot_tn(c_ref[...], dc_ref[...])
        g_out[...] = g
        d_out[...], m_out[...], v_out[...] = _adamw_math(w_ref[...], g, m_ref[...], v_ref[...])

    blk = pl.BlockSpec((None, d, ADA_TN), lambda l, j: (l, 0, j))
    return pl.pallas_call(
        body, name="ada_w_update", grid=(nl, n // ADA_TN),
        in_specs=[_full((N_DEV, d)), pl.BlockSpec((None, N_DEV, ADA_TN), lambda l, j: (l, 0, j)), blk, blk, blk],
        out_specs=[blk] * 4, out_shape=[jax.ShapeDtypeStruct((nl, d, n), F32)] * 4,
        compiler_params=_cp(dimension_semantics=("parallel", "parallel")),
    )(cact, dcond_loc, w, m, v)


def _place():
    x, y, c = lax.axis_index("x"), lax.axis_index("y"), lax.axis_index("c")
    chips = [(1 - x, y), (x, 1 - y), (1 - x, 1 - y)]
    return x, y, c, chips


def _remote(src, dst, send_sem, recv_sem, to):
    return pltpu.make_async_remote_copy(src_ref=src, dst_ref=dst, send_sem=send_sem, recv_sem=recv_sem,
                                        device_id=to, device_id_type=MESH_ID)


def _sems(n):
    return [pltpu.SemaphoreType.DMA((n,)), pltpu.SemaphoreType.DMA((n,))]


def _all_gather8(v, name):
    r, cdim = v.shape

    def body(x_ref, out_ref, stage, send_sems, recv_sems):
        x, y, c, chips = _place()
        sibling = (x, y, 1 - c)

        def slot(px, py, pc):
            return out_ref.at[4 * px + 2 * py + pc]

        first = [_remote(x_ref, slot(x, y, c), send_sems.at[0], recv_sems.at[0], sibling)]
        first += [_remote(x_ref, slot(x, y, c), send_sems.at[1 + j], recv_sems.at[1 + j], (*chip, c))
                  for j, chip in enumerate(chips)]
        for cp in first:
            cp.start()
        pltpu.sync_copy(x_ref, stage)
        pltpu.sync_copy(stage, slot(x, y, c))
        passed = []
        for j, chip in enumerate(chips):
            blk = slot(*chip, c)
            _remote(blk, blk, send_sems.at[1 + j], recv_sems.at[1 + j], (x, y, c)).wait_recv()
            fw = _remote(blk, blk, send_sems.at[4 + j], recv_sems.at[4 + j], sibling)
            fw.start()
            passed.append(fw)
        blk = slot(x, y, 1 - c)
        _remote(blk, blk, send_sems.at[0], recv_sems.at[0], (x, y, c)).wait_recv()
        for j, chip in enumerate(chips):
            blk = slot(*chip, 1 - c)
            _remote(blk, blk, send_sems.at[4 + j], recv_sems.at[4 + j], (x, y, c)).wait_recv()
        for cp in first + passed:
            cp.wait_send()

    return pl.pallas_call(
        body, name=name, out_shape=jax.ShapeDtypeStruct((N_DEV, r, cdim), v.dtype),
        in_specs=[ANY], out_specs=ANY,
        scratch_shapes=[pltpu.VMEM((r, cdim), v.dtype)] + _sems(7),
        compiler_params=_cp(),
    )(v)


def _gather_first_copies():
    def make(refs, send_sems, recv_sems):
        x, y, c, chips = _place()
        mine = refs[0].at[4 * x + 2 * y + c]
        to = [(x, y, 1 - c)] + [(*chip, c) for chip in chips]
        return [_remote(mine, mine, send_sems.at[k], recv_sems.at[k], dev) for k, dev in enumerate(to)]
    return make


def _gather_pass_on(buf, name):
    def body(in_ref, out_ref, send_sems, recv_sems):
        x, y, c, chips = _place()
        passed = []
        for j, chip in enumerate(chips):
            blk = out_ref.at[4 * chip[0] + 2 * chip[1] + c]
            fw = _remote(blk, blk, send_sems.at[j], recv_sems.at[j], (x, y, 1 - c))
            fw.start()
            passed.append(fw)
        for j, chip in enumerate(chips):
            blk = out_ref.at[4 * chip[0] + 2 * chip[1] + 1 - c]
            _remote(blk, blk, send_sems.at[j], recv_sems.at[j], (x, y, c)).wait_recv()
        for fw in passed:
            fw.wait_send()

    return pl.pallas_call(
        body, name=name, out_shape=jax.ShapeDtypeStruct(buf.shape, buf.dtype),
        in_specs=[ANY], out_specs=ANY, input_output_aliases={0: 0}, scratch_shapes=_sems(3),
    )(buf)


def _place_weights(ws, layer, kidx, after):
    steps = 4
    shapes, in_specs, out_specs = [], [], []
    for w, kind in zip(ws, BIG_KINDS):
        _, a, b = w.shape
        in_specs.append(pl.BlockSpec((None, a // steps, b), lambda i, k: (layer, i, 0)))
        if kind == "col":
            shapes.append((2, a, 2 * b))
            out_specs.append(pl.BlockSpec((None, a // steps, b), lambda i, k: (k[0] // 2, i, k[0] % 2)))
        else:
            shapes.append((N_CHIP, a, b))
            out_specs.append(pl.BlockSpec((None, a // steps, b), lambda i, k: (k[0], i, 0)))

    def body(k_ref, *refs):
        outs = refs[len(ws) + 1:]
        for t in range(len(ws)):
            outs[t][...] = refs[t][...].astype(BF16)

    return pl.pallas_call(
        body, name="place_weights", out_shape=[jax.ShapeDtypeStruct(s, BF16) for s in shapes],
        grid_spec=pltpu.PrefetchScalarGridSpec(num_scalar_prefetch=1, grid=(steps,), in_specs=in_specs + [ANY],
                                               out_specs=out_specs),
        compiler_params=_cp(dimension_semantics=("parallel",)),
    )(kidx, *ws, after)


HBM = pl.BlockSpec(memory_space=pltpu.HBM)
SEM = pl.BlockSpec(memory_space=pltpu.SEMAPHORE)
EFFECT = pltpu.SideEffectType.DATAFLOW_SIDE_EFFECTING


def _weight_block(ref, kind, k, h):
    if kind == "col":
        ncol = ref.shape[3] // 2
        return ref.at[k // 2, h, :, pl.ds(pl.multiple_of((k % 2) * ncol, LANES), ncol)]
    return ref.at[k, h]


def _in_hbm(a):
    return pltpu.with_memory_space_constraint(a, pltpu.HBM)


def _weight_send_start(placed, kinds, name):
    nt = len(placed)

    def body(*refs):
        send_sems, recv_sems = refs[nt], refs[nt + 1]
        dst = refs[nt + 2:2 * nt + 2]
        token = refs[2 * nt + 2]
        x, y, c, chips = _place()
        kme = 2 * x + y
        for t in range(nt):
            for j, chip in enumerate(chips):
                own = _weight_block(dst[t], kinds[t], kme, c)
                _remote(own, own, send_sems.at[3 * t + j], recv_sems.at[3 * t + j], (*chip, c)).start()
        token[...] = jnp.zeros_like(token)

    return pl.pallas_call(
        body, name=name,
        out_shape=(pltpu.SemaphoreType.DMA((3 * nt,)), pltpu.SemaphoreType.DMA((3 * nt,)),
                   *[pltpu.HBM(a.shape, a.dtype) for a in placed], jax.ShapeDtypeStruct((8, LANES), F32)),
        in_specs=[HBM] * nt, out_specs=(SEM, SEM, *[HBM] * nt, pl.BlockSpec(memory_space=pltpu.VMEM)),
        input_output_aliases={t: 2 + t for t in range(nt)},
        compiler_params=pltpu.CompilerParams(has_side_effects=EFFECT),
    )(*[_in_hbm(a) for a in placed])


def _weight_send_wait(send_sems, recv_sems, arrays, kinds, after, name):
    nt = len(arrays)

    def body(*refs):
        arr = refs[:nt]
        send_sems, recv_sems = refs[nt], refs[nt + 1]
        x, y, c, chips = _place()
        kme = 2 * x + y
        for t in range(nt):
            for j, chip in enumerate(chips):
                own = _weight_block(arr[t], kinds[t], kme, c)
                got = _weight_block(arr[t], kinds[t], 2 * chip[0] + chip[1], c)
                cp = _remote(own, got, send_sems.at[3 * t + j], recv_sems.at[3 * t + j], (*chip, c))
                cp.wait_send()
                cp.wait_recv()

    return pl.pallas_call(
        body, name=name, out_shape=[pltpu.HBM(a.shape, a.dtype) for a in arrays],
        in_specs=[HBM] * nt + [SEM, SEM, ANY], out_specs=[HBM] * nt,
        input_output_aliases={t: t for t in range(nt)},
        compiler_params=pltpu.CompilerParams(has_side_effects=EFFECT),
    )(*arrays, send_sems, recv_sems, after)


def _forward_copies(kinds):
    def make(refs, send_sems, recv_sems):
        x, y, c, chips = _place()
        cps = []
        for t in range(len(kinds)):
            for j, chip in enumerate(chips):
                blk = _weight_block(refs[t], kinds[t], 2 * chip[0] + chip[1], c)
                cps.append(_remote(blk, blk, send_sems.at[3 * t + j], recv_sems.at[3 * t + j], (x, y, 1 - c)))
        return cps
    return make


def _split_start(name, arrays, n_copies, make_copies):
    na = len(arrays)

    def body(*refs):
        send_sems, recv_sems = refs[na], refs[na + 1]
        for cp in make_copies(refs[na + 2:2 * na + 2], send_sems, recv_sems):
            cp.start()
        token = refs[2 * na + 2]
        token[...] = jnp.zeros_like(token)

    return pl.pallas_call(
        body, name=name,
        out_shape=(pltpu.SemaphoreType.DMA((n_copies,)), pltpu.SemaphoreType.DMA((n_copies,)),
                   *[pltpu.HBM(a.shape, a.dtype) for a in arrays], jax.ShapeDtypeStruct((8, LANES), F32)),
        in_specs=[HBM] * na, out_specs=(SEM, SEM, *[HBM] * na, pl.BlockSpec(memory_space=pltpu.VMEM)),
        input_output_aliases={t: 2 + t for t in range(na)},
        compiler_params=pltpu.CompilerParams(has_side_effects=EFFECT),
    )(*[_in_hbm(a) for a in arrays])


def _split_wait(name, started, make_copies, after):
    send_sems, recv_sems, *arrays, _ = started
    na = len(arrays)

    def body(*refs):
        send_sems, recv_sems = refs[na], refs[na + 1]
        for cp in make_copies(refs[:na], send_sems, recv_sems):
            cp.wait_send()
            cp.wait_recv()

    return pl.pallas_call(
        body, name=name, out_shape=[pltpu.HBM(a.shape, a.dtype) for a in arrays],
        in_specs=[HBM] * na + [SEM, SEM, ANY], out_specs=[HBM] * na,
        input_output_aliases={t: t for t in range(na)},
        compiler_params=pltpu.CompilerParams(has_side_effects=EFFECT),
    )(*arrays, send_sems, recv_sems, after)


def _exchange_copies(nt):
    def make(refs, send_sems, recv_sems):
        x, y, c, _ = _place()
        return [_remote(refs[t].at[:, 1 - c], refs[nt + t], send_sems.at[t], recv_sems.at[t], (x, y, 1 - c))
                for t in range(nt)]
    return make


def _sibling_exchange_start(views, name):
    lands = [lax.empty((v.shape[0],) + v.shape[2:], v.dtype) for v in views]
    return _split_start(name, list(views) + lands, len(views), _exchange_copies(len(views)))


def _sibling_exchange_wait(started, after, name):
    nt = (len(started) - 3) // 2
    outs = _split_wait(name, started, _exchange_copies(nt), after)
    return outs[:nt], outs[nt:]


def _scatter_copies(src, land, kinds, send_sems, recv_sems):
    x, y, c, chips = _place()
    cps = []
    for t in range(len(src)):
        for j, chip in enumerate(chips):
            k = 2 * chip[0] + chip[1]
            if kinds[t] == "col":
                ncol = land[t].shape[2]
                win = src[t].at[k // 2, :, pl.ds(pl.multiple_of((k % 2) * ncol, LANES), ncol)]
            else:
                win = src[t].at[k]
            cps.append(_remote(win, land[t].at[j], send_sems.at[3 * t + j], recv_sems.at[3 * t + j], (*chip, c)))
    return cps


def _chip_scatter_start(parts, kinds, name):
    nt = len(parts)
    shapes = []
    for p, kind in zip(parts, kinds):
        shapes.append((3, p.shape[1], p.shape[2] // 2) if kind == "col" else (3,) + p.shape[1:])

    def body(*refs):
        send_sems, recv_sems = refs[2 * nt], refs[2 * nt + 1]
        src, land = refs[2 * nt + 2:3 * nt + 2], refs[3 * nt + 2:4 * nt + 2]
        token = refs[4 * nt + 2]
        for cp in _scatter_copies(src, land, kinds, send_sems, recv_sems):
            cp.start()
        token[...] = jnp.zeros_like(token)

    lands = [lax.empty(s, BF16) for s in shapes]
    return pl.pallas_call(
        body, name=name,
        out_shape=(pltpu.SemaphoreType.DMA((3 * nt,)), pltpu.SemaphoreType.DMA((3 * nt,)),
                   *[pltpu.HBM(a.shape, a.dtype) for a in parts], *[pltpu.HBM(s, BF16) for s in shapes],
                   jax.ShapeDtypeStruct((8, LANES), F32)),
        in_specs=[HBM] * (2 * nt), out_specs=(SEM, SEM, *[HBM] * (2 * nt), pl.BlockSpec(memory_space=pltpu.VMEM)),
        input_output_aliases={t: 2 + t for t in range(2 * nt)},
        compiler_params=pltpu.CompilerParams(has_side_effects=EFFECT),
    )(*[_in_hbm(a) for a in parts], *[_in_hbm(a) for a in lands])


def _chip_scatter_wait(send_sems, recv_sems, parts, lands, kinds, after, name):
    nt = len(parts)

    def body(*refs):
        src, land = refs[:nt], refs[nt:2 * nt]
        send_sems, recv_sems = refs[2 * nt], refs[2 * nt + 1]
        for cp in _scatter_copies(src, land, kinds, send_sems, recv_sems):
            cp.wait_send()
            cp.wait_recv()

    outs = pl.pallas_call(
        body, name=name, out_shape=[pltpu.HBM(a.shape, a.dtype) for a in list(parts) + list(lands)],
        in_specs=[HBM] * (2 * nt) + [SEM, SEM, ANY], out_specs=[HBM] * (2 * nt),
        input_output_aliases={t: t for t in range(2 * nt)},
        compiler_params=pltpu.CompilerParams(has_side_effects=EFFECT),
    )(*parts, *lands, send_sems, recv_sems, after)
    return outs[:nt], outs[nt:]


def _share_copies(nt):
    def make(refs, send_sems, recv_sems):
        x, y, c, _ = _place()
        return [_remote(refs[t].at[c], refs[t].at[c], send_sems.at[t], recv_sems.at[t], (x, y, 1 - c))
                for t in range(nt)]
    return make


def _sibling_share_start(fulls, name):
    return _split_start(name, list(fulls), len(fulls), _share_copies(len(fulls)))


def _sibling_share_wait(started, after, name):
    return _split_wait(name, started, _share_copies(len(started) - 3), after)


SUM_STEPS = 4


def _pair_sum(views, lands, ck):
    nt = len(views)
    in_specs, out_specs, shapes = [], [], []
    for v in views:
        b, _, r, cc = v.shape
        per = SUM_STEPS // b
        tr = r // per
        in_specs.append(pl.BlockSpec((None, None, tr, cc), lambda i, s, per=per: (i // per, s[0], i % per, 0)))
        out_specs.append(pl.BlockSpec((None, tr, cc), lambda i, s, per=per: (i // per, i % per, 0)))
        shapes.append((b, r, cc))
    in_specs = in_specs + out_specs

    def body(s_ref, *refs):
        for t in range(nt):
            refs[2 * nt + t][...] = (refs[t][...].astype(F32) + refs[nt + t][...].astype(F32)).astype(BF16)

    return pl.pallas_call(
        body, name="grad_pair_sum", out_shape=[jax.ShapeDtypeStruct(s, BF16) for s in shapes],
        grid_spec=pltpu.PrefetchScalarGridSpec(num_scalar_prefetch=1, grid=(SUM_STEPS,), in_specs=in_specs,
                                               out_specs=out_specs),
        compiler_params=_cp(dimension_semantics=("parallel",)),
    )(ck, *views, *lands)


def _chip_sum(parts, lands, kinds, ck):
    nt = len(parts)
    steps = 2
    in_own, in_land, out_specs, shapes = [], [], [], []
    for ld, kind in zip(lands, kinds):
        _, r, cc = ld.shape
        tr = r // steps
        if kind == "col":
            in_own.append(pl.BlockSpec((None, tr, cc), lambda i, s: (s[1] // 2, i, s[1] % 2)))
        else:
            in_own.append(pl.BlockSpec((None, tr, cc), lambda i, s: (s[1], i, 0)))
        in_land.append(pl.BlockSpec((3, tr, cc), lambda i, s: (0, i, 0)))
        out_specs.append(pl.BlockSpec((None, tr, cc), lambda i, s: (s[0], i, 0)))
        shapes.append((2, r, cc))

    def body(s_ref, *refs):
        for t in range(nt):
            acc = refs[t][...].astype(F32)
            for j in range(3):
                acc = acc + refs[nt + t][j].astype(F32)
            refs[2 * nt + t][...] = acc

    return pl.pallas_call(
        body, name="grad_chip_sum", out_shape=[jax.ShapeDtypeStruct(s, F32) for s in shapes],
        grid_spec=pltpu.PrefetchScalarGridSpec(num_scalar_prefetch=1, grid=(steps,), in_specs=in_own + in_land,
                                               out_specs=out_specs),
        compiler_params=_cp(dimension_semantics=("parallel",)),
    )(ck, *parts, *lands)


def _sum8(g):
    _, r, cc = g.shape
    tr = _row_tile(r, N_DEV * cc)

    def body(g_ref, o_ref):
        acc = g_ref[0].astype(F32)
        for d in range(1, N_DEV):
            acc = acc + g_ref[d].astype(F32)
        o_ref[...] = acc

    return pl.pallas_call(
        body, name="small_grad_sum", grid=(r // tr,),
        in_specs=[pl.BlockSpec((N_DEV, tr, cc), lambda i: (0, i, 0))],
        out_specs=pl.BlockSpec((tr, cc), lambda i: (i, 0)),
        out_shape=jax.ShapeDtypeStruct((r, cc), F32),
        compiler_params=_cp(dimension_semantics=("parallel",)),
    )(g)


def _silu_rows(c):
    def body(c_ref, o_ref):
        v = c_ref[...]
        o_ref[...] = v * jax.nn.sigmoid(v)

    return pl.pallas_call(body, name="cond_silu", out_shape=jax.ShapeDtypeStruct(c.shape, F32))(c)


def _pack(arrays):
    rows = []
    for a in arrays:
        flat = a.reshape(-1)
        rows.append(jnp.pad(flat, (0, (-flat.shape[0]) % (8 * LANES))).reshape(-1, LANES))
    n = sum(r.shape[0] for r in rows)
    if n % 256:
        rows.append(jnp.zeros((256 - n % 256, LANES), rows[0].dtype))
    return jnp.concatenate(rows, axis=0)


def _unpack(packed, shapes):
    out, off = [], 0
    for s in shapes:
        n = math.prod(s)
        nr = 8 * -(-n // (8 * LANES))
        out.append(packed[off:off + nr].reshape(-1)[:n].reshape(s))
        off += nr
    return out


def _as_rows(a):
    return a.reshape(1, -1) if a.ndim == 1 else a.reshape(-1, a.shape[-1])


def _adamw_many(ws, gs, ms, vs, name, steps=1):
    nt = len(ws)

    def body(*refs):
        for t in range(nt):
            w_ref, g_ref, m_ref, v_ref = (refs[k * nt + t] for k in range(4))
            d, m, v = _adamw_math(w_ref[...], g_ref[...], m_ref[...], v_ref[...])
            refs[4 * nt + t][...] = d
            refs[5 * nt + t][...] = m
            refs[6 * nt + t][...] = v

    shapes = [jax.ShapeDtypeStruct(a.shape, F32) for a in ws]
    if steps == 1:
        outs = pl.pallas_call(body, name=name, out_shape=shapes * 3, compiler_params=_cp())(*ws, *gs, *ms, *vs)
    else:
        specs = [pl.BlockSpec((a.shape[0] // steps, a.shape[1]), lambda i: (i, 0)) for a in ws]
        outs = pl.pallas_call(
            body, name=name, grid=(steps,), in_specs=specs * 4, out_specs=specs * 3, out_shape=shapes * 3,
            compiler_params=_cp(dimension_semantics=("parallel",)),
        )(*ws, *gs, *ms, *vs)
    return outs[:nt], outs[nt:2 * nt], outs[2 * nt:]


def _exchange_big_grads(grads, kinds, layer):
    views = []
    for g, kind in zip(grads, kinds):
        if kind == "col":
            views.append(g.reshape(2, 2, g.shape[1] // 2, g.shape[2]))
        else:
            views.append(g.reshape(N_CHIP, 2, g.shape[0] // (2 * N_CHIP), g.shape[1]))
    return _sibling_exchange_start(views, "grad_exchange_start_%d" % layer)


def _scatter_big_grads(exchanged, kinds, ck, after, layer):
    views, lands = _sibling_exchange_wait(exchanged, after, "grad_exchange_wait_%d" % layer)
    parts = _pair_sum(views, lands, ck)
    return _chip_scatter_start(parts, kinds, "grad_scatter_start_%d" % layer)


def _finish_big_grads(started, kinds, ck, after, layer):
    nt = len(kinds)
    send_sems, recv_sems = started[0], started[1]
    parts, lands = started[2:2 + nt], started[2 + nt:2 + 2 * nt]
    parts, lands = _chip_scatter_wait(send_sems, recv_sems, parts, lands, kinds, after, "grad_scatter_wait_%d" % layer)
    return _sibling_share_start(_chip_sum(parts, lands, kinds, ck), "grad_share_start_%d" % layer)


def _adamw_layer(ws, gs, ms, vs, stacks, layer, name, steps):
    nt = len(ws)
    stacks = [s if s is not None else tuple(lax.empty(w.shape, F32) for _ in range(4)) for s, w in zip(stacks, ws)]

    def body(*refs):
        for t in range(nt):
            w_ref, g_ref, m_ref, v_ref = (refs[k * nt + t] for k in range(4))
            outs = refs[8 * nt + 4 * t:8 * nt + 4 * t + 4]
            g = g_ref[...]
            outs[0][...] = g
            outs[1][...], outs[2][...], outs[3][...] = _adamw_math(w_ref[...], g, m_ref[...], v_ref[...])

    in_specs, g_specs, out_specs = [], [], []
    for w in ws:
        _, r, c = w.shape
        in_specs.append(pl.BlockSpec((None, r // steps, c), lambda i: (layer, i, 0)))
        g_specs.append(pl.BlockSpec((r // steps, c), lambda i: (i, 0)))
        out_specs += [pl.BlockSpec((None, r // steps, c), lambda i: (layer, i, 0))] * 4
    in_specs = in_specs + g_specs + in_specs * 2 + [ANY] * (4 * nt)
    flat = [a for s in stacks for a in s]
    outs = pl.pallas_call(
        body, name=name, grid=(steps,), in_specs=in_specs, out_specs=out_specs,
        out_shape=[jax.ShapeDtypeStruct(a.shape, F32) for a in flat],
        input_output_aliases={4 * nt + k: k for k in range(4 * nt)},
        compiler_params=_cp(dimension_semantics=("parallel",)),
    )(*ws, *gs, *ms, *vs, *flat)
    return [tuple(outs[4 * t:4 * t + 4]) for t in range(nt)]


SMALL_NAMES = ["ada_b", "norm1_g", "norm2_g", "sgu_w", "sgu_b", "pool_w", "pool_scale", "conv_w", "s5_lambda_re",
               "s5_lambda_im", "s5_b_re", "s5_b_im", "s5_c_re", "s5_c_im", "s5_d", "s5_log_dt", "s5_glu_w", "s5_glu_b",
               "mix_norm_g", "norm3_g", "final_norm_g"]
BIG_NAMES = ["ffn1_w_in", "ffn1_w_out", "w_mix_in", "w_mix_out", "ffn2_w_in", "ffn2_w_out"]
BIG_KINDS = ["col", "row", "row", "row", "col", "row"]
WEIGHT_ORDER = ["ada_w", "ada_b", "norm1_g", "ffn1_w_in", "ffn1_w_out", "norm2_g", "w_mix_in", "sgu_w", "sgu_b", "pool_w",
                "pool_scale", "conv_w", "s5_lambda_re", "s5_lambda_im", "s5_b_re", "s5_b_im", "s5_c_re", "s5_c_im", "s5_d",
                "s5_log_dt", "s5_glu_w", "s5_glu_b", "mix_norm_g", "w_mix_out", "norm3_g", "ffn2_w_in", "ffn2_w_out",
                "final_norm_g"]


def _local_step(x, target, cond, fetch_weights, prefetch_weights, p, emit_grads):
    nl, d = DEPTH, x.shape[1]
    row = lambda a: a.reshape(1, -1)
    saved = []
    for l in range(nl):
        (wi1, wo1, wmit, wmo, wi2, wo2), tok = fetch_weights(l, x)
        cl = cond[l] + tok
        mod1, mod2, mod3 = cl[0:3], cl[3:6], cl[6:9]
        lre, lim = p["s5_lambda_re"][l].reshape(-1), p["s5_lambda_im"][l].reshape(-1)
        ldt = jnp.repeat(p["s5_log_dt"][l], 64)
        rowp = jnp.stack([lre, lim, ldt])
        sp = (rowp, rowp.T, p["s5_b_re"][l].reshape(N_STATE, 16), p["s5_b_im"][l].reshape(N_STATE, 16),
              p["s5_c_re"][l].reshape(W_GRP, 64), p["s5_c_im"][l].reshape(W_GRP, 64), row(p["s5_d"][l]))
        glu = (p["s5_glu_w"][l], row(p["s5_glu_b"][l]))
        bias_full = jnp.repeat(p["sgu_b"][l].T, 64, axis=1)
        pw2 = p["pool_w"][l].reshape(W_GRP, 64)
        x1, h1, a1, b1, o1 = _ffn_fwd(x, mod1, row(p["norm1_g"][l]), wi1, wo1)
        z, h2 = _mix_in_fwd(x1, mod2, row(p["norm2_g"][l]), wmit)
        ya = _sgu_fwd(z, p["sgu_w"][l], bias_full)
        yb, yc = _poolconv_fwd(z, pw2, row(p["pool_scale"][l]), p["conv_w"][l])
        ys = (ya, yb, yc, _s5_core_fwd(z, sp))
        x2, m = _mix_out_fwd(ys, glu, row(p["mix_norm_g"][l]), wmo, x1, mod2[2:3])
        mod3 = mod3 + prefetch_weights(l + 1, x2)
        x3, h3, a3, b3, o3 = _ffn_fwd(x2, mod3, row(p["norm3_g"][l]), wi2, wo2)
        saved.append((x, x1, x2, h1, a1, b1, o1, z, h2, ys, m, h3, a3, b3, o3, sp, bias_full, pw2, glu,
                      (wi1, wo1, wmit, wmo, wi2, wo2), cl))
        x = x3

    loss, dx, dfg = _loss_head(x, row(p["final_norm_g"]), target)

    sg = {n: [None] * nl for n in SMALL_NAMES if n not in ("ada_b", "final_norm_g")}
    dcond = [None] * nl
    s5_da, s5_dk = [None] * nl, [None] * nl
    tok = 0.0
    for l in reversed(range(nl)):
        (x0, x1, x2, h1, a1, b1, o1, z, h2, ys, m, h3, a3, b3, o3, sp, bias_full, pw2, glu,
         (wi1, wo1, wmit, wmo, wi2, wo2), cl) = saved[l]
        cl = cl + tok
        mod1, mod2, mod3 = cl[0:3], cl[3:6], cl[6:9]
        dza, dzb, dwi2, dwo2, dgate3 = _ffn_bwd_main(dx, o3, mod3[2:3], h3, a3, b3, wo2)
        dx, rows3 = _ffn_bwd_in(dza, dzb, wi2, x2, dx, mod3, row(p["norm3_g"][l]))
        outs = _mix_out_bwd(dx, m, mod2[2:3], ys, glu, row(p["mix_norm_g"][l]), wmo)
        dys, dgate2, dmng, dwmo, dgw, dgb = outs[0:4], outs[4], outs[5], outs[6], outs[7], outs[8]
        dza_, dsw, dsb = _sgu_bwd(z, dys[0], p["sgu_w"][l], bias_full)
        dzb_, dzc_, dwbd, dps, dcw = _poolconv_bwd(z, dys[1], dys[2], pw2, row(p["pool_scale"][l]), p["conv_w"][l])
        dzd_, dbr, dbi, dcr, dci, dd, da, dk = _s5_core_bwd(z, dys[3], sp)
        dx, rows2, dwmit = _mix_in_bwd((dza_, dzb_, dzc_, dzd_), h2, wmit, x1, dx, mod2, row(p["norm2_g"][l]))
        dza, dzb, dwi1, dwo1, dgate1 = _ffn_bwd_main(dx, o1, mod1[2:3], h1, a1, b1, wo1)
        tok, layer_done = emit_grads(l, [dwi1, dwo1, dwmit, dwmo, dwi2, dwo2])
        dx, rows1 = _ffn_bwd_in(dza, dzb, wi1, x0, dx, mod1 + tok, row(p["norm1_g"][l]))
        if l > 0:
            tok = layer_done(dx)[0, 0]
        dcond[l] = jnp.concatenate([rows1[0:2], dgate1, rows2[0:2], dgate2, rows3[0:2], dgate3], axis=0)
        sg["norm1_g"][l], sg["norm2_g"][l], sg["norm3_g"][l] = rows1[2], rows2[2], rows3[2]
        sg["mix_norm_g"][l] = dmng[0]
        sg["sgu_w"][l] = dsw
        sg["sgu_b"][l] = dsb[:, 0:4].T
        g4 = dwbd.reshape(4, 64, 4, 64)
        sg["pool_w"][l] = jnp.stack([g4[k, :, k, :] for k in range(4)])
        sg["pool_scale"][l] = dps[0]
        sg["conv_w"][l] = dcw[0:3]
        sg["s5_b_re"][l], sg["s5_b_im"][l] = dbr.reshape(16, 64, 16), dbi.reshape(16, 64, 16)
        sg["s5_c_re"][l], sg["s5_c_im"][l] = dcr.reshape(16, 16, 64), dci.reshape(16, 16, 64)
        sg["s5_d"][l] = dd[0]
        sg["s5_glu_w"][l], sg["s5_glu_b"][l] = dgw, dgb[0]
        s5_da[l], s5_dk[l] = da, dk

    n16 = nl * 16
    dlre, dlim, dldt = _s5_param_bwd(
        p["s5_lambda_re"].reshape(n16, 64), p["s5_lambda_im"].reshape(n16, 64),
        jnp.repeat(p["s5_log_dt"].reshape(n16, 1), 64, axis=1),
        jnp.stack([a[0] for a in s5_da]).reshape(n16, 64), jnp.stack([a[1] for a in s5_da]).reshape(n16, 64),
        jnp.stack([k[:, 0] for k in s5_dk]).reshape(n16, 64), jnp.stack([k[:, 1] for k in s5_dk]).reshape(n16, 64))
    small = {n: jnp.stack(v) for n, v in sg.items() if v[0] is not None}
    small["s5_lambda_re"] = dlre.reshape(nl, 16, 64)
    small["s5_lambda_im"] = dlim.reshape(nl, 16, 64)
    small["s5_log_dt"] = dldt.reshape(nl, 16)
    small["final_norm_g"] = dfg[0]
    return loss, dx, small, jnp.stack(dcond), layer_done


def kernel(x, c, ada_w, ada_b, norm1_g, ffn1_w_in, ffn1_w_out, norm2_g, w_mix_in, sgu_w, sgu_b, pool_w, pool_scale, conv_w, s5_lambda_re, s5_lambda_im, s5_b_re, s5_b_im, s5_c_re, s5_c_im, s5_d, s5_log_dt, s5_glu_w, s5_glu_b, mix_norm_g, w_mix_out, norm3_g, ffn2_w_in, ffn2_w_out, final_norm_g, loss_target, m_ada_w, m_ada_b, m_norm1_g, m_ffn1_w_in, m_ffn1_w_out, m_norm2_g, m_w_mix_in, m_sgu_w, m_sgu_b, m_pool_w, m_pool_scale, m_conv_w, m_s5_lambda_re, m_s5_lambda_im, m_s5_b_re, m_s5_b_im, m_s5_c_re, m_s5_c_im, m_s5_d, m_s5_log_dt, m_s5_glu_w, m_s5_glu_b, m_mix_norm_g, m_w_mix_out, m_norm3_g, m_ffn2_w_in, m_ffn2_w_out, m_final_norm_g, v_ada_w, v_ada_b, v_norm1_g, v_ffn1_w_in, v_ffn1_w_out, v_norm2_g, v_w_mix_in, v_sgu_w, v_sgu_b, v_pool_w, v_pool_scale, v_conv_w, v_s5_lambda_re, v_s5_lambda_im, v_s5_b_re, v_s5_b_im, v_s5_c_re, v_s5_c_im, v_s5_d, v_s5_log_dt, v_s5_glu_w, v_s5_glu_b, v_mix_norm_g, v_w_mix_out, v_norm3_g, v_ffn2_w_in, v_ffn2_w_out, v_final_norm_g):
    args = dict(locals())
    w = {n: args[n] for n in WEIGHT_ORDER}
    mom = {n: args["m_" + n] for n in WEIGHT_ORDER}
    vel = {n: args["v_" + n] for n in WEIGHT_ORDER}
    nl, d = DEPTH, x.shape[-1]
    s = x.shape[1]
    px, py, pc = lax.axis_index("x"), lax.axis_index("y"), lax.axis_index("c")
    kme = 2 * px + py
    me = 2 * kme + pc
    kidx = jnp.reshape(kme, (1,)).astype(jnp.int32)

    shards = [ffn1_w_in, ffn1_w_out, jnp.swapaxes(w_mix_in, 1, 2), w_mix_out, ffn2_w_in, ffn2_w_out]
    started_weights = {}

    def start_weights(l, after):
        placed = _place_weights(shards, l, kidx, after)
        views = [a.reshape(a.shape[0], 2, a.shape[1] // 2, a.shape[2]) for a in placed]
        *handles, token = _weight_send_start(views, BIG_KINDS, "weight_send_start_%d" % l)
        started_weights[l] = handles
        return token

    cact = _silu_rows(c)
    pre = _pack([cact, conv_w, s5_glu_w])
    pre_all = _all_gather8(pre, "gather_prelude")
    token = start_weights(0, pre_all)
    parts = [_unpack(pre_all[dev], [cact.shape, conv_w.shape, s5_glu_w.shape]) for dev in range(N_DEV)]
    cact_all = pre_all[:, :d // LANES, :].reshape(N_DEV, d)
    conv_full = jnp.concatenate([parts[2 * k][1] for k in range(N_CHIP)], axis=2)
    glu_full = jnp.concatenate([parts[2 * k][2] for k in range(N_CHIP)], axis=1)

    n_ada = ada_w.shape[2]
    ada_b_loc = lax.dynamic_slice_in_dim(ada_b, kme * n_ada, n_ada, axis=1).reshape(nl, 1, n_ada) + token[0, 0]
    cond_part = _cond_fwd(cact_all, ada_w, ada_b_loc)
    cond_mine = lax.dynamic_update_slice(lax.empty((N_DEV, nl * N_DEV, n_ada), F32),
                                         cond_part.reshape(1, nl * N_DEV, n_ada), (me, 0, 0))
    cond_gathering = _split_start("cond_send_start", [cond_mine], 4, _gather_first_copies())
    token = cond_gathering[-1]
    for l in range(1, nl):
        token = start_weights(l, token)
    cond_arrived, = _split_wait("cond_send_wait", cond_gathering, _gather_first_copies(), token)
    cond_all = _gather_pass_on(cond_arrived, "cond_pass_on").reshape(N_DEV, nl, N_DEV, n_ada)
    cond_me = jnp.concatenate(
        [lax.dynamic_index_in_dim(cond_all[2 * k], me, axis=1, keepdims=False) for k in range(N_CHIP)], axis=1)
    cond = cond_me.reshape(nl, 9, d)

    forwarding = {}

    def prefetch_weights(l, after):
        if l >= nl:
            return 0.0
        send_sems, recv_sems, *views = started_weights.pop(l)
        views = _weight_send_wait(send_sems, recv_sems, views, BIG_KINDS, after, "weight_send_wait_%d" % l)
        forwarding[l] = _split_start("weight_forward_start_%d" % l, views, 3 * len(views), _forward_copies(BIG_KINDS))
        return forwarding[l][-1][0, 0]

    def fetch_weights(l, after):
        if l not in forwarding:
            prefetch_weights(l, after)
        views = _split_wait("weight_forward_wait_%d" % l, forwarding.pop(l), _forward_copies(BIG_KINDS), after)
        full = [v.reshape(2, 2 * v.shape[2], v.shape[3]) if kind == "col" else v.reshape(-1, v.shape[3])
                for v, kind in zip(views, BIG_KINDS)]
        return full, 0.0

    ck = jnp.stack([pc, kme]).astype(jnp.int32)
    scattering, sharing = [], []
    stacks = {n: None for n in BIG_NAMES}
    groups = ((["ffn1_w_in", "ffn2_w_in"], 16, "adamw_w_in"),
              (["ffn1_w_out", "w_mix_in", "w_mix_out", "ffn2_w_out"], 8, "adamw_w_out"))

    def as_reduced(t):
        return {n: jnp.swapaxes(t[n], 1, 2) if n == "w_mix_in" else t[n] for n in BIG_NAMES}

    w_r, m_r, v_r = as_reduced(w), as_reduced(mom), as_reduced(vel)

    def apply_adamw(l, fulls):
        g = {n: f.reshape(2 * f.shape[1], f.shape[2]) for n, f in zip(BIG_NAMES, fulls)}
        for names, steps, call in groups:
            outs = _adamw_layer([w_r[n] for n in names], [g[n] for n in names], [m_r[n] for n in names],
                                [v_r[n] for n in names], [stacks[n] for n in names], l, call, steps)
            stacks.update(zip(names, outs))

    def retire_share(after):
        l2, shared = sharing.pop(0)
        apply_adamw(l2, _sibling_share_wait(shared, after, "grad_share_wait_%d" % l2))

    def retire_scatter(after):
        l1, scattered = scattering.pop(0)
        sharing.append((l1, _finish_big_grads(scattered, BIG_KINDS, ck, after, l1)))

    def retire(after):
        if sharing:
            retire_share(after)
        if scattering:
            retire_scatter(after)

    def emit_grads(l, grads_l):
        exchanged = _exchange_big_grads(grads_l, BIG_KINDS, l)

        def layer_done(after):
            started = _scatter_big_grads(exchanged, BIG_KINDS, ck, after, l)
            retire(after)
            scattering.append((l, started))
            return started[-1]

        return exchanged[-1][0, 0], layer_done

    p = {n: w[n] for n in SMALL_NAMES}
    p["conv_w"], p["s5_glu_w"] = conv_full, glu_full
    loss, dx, small, dcond, first_layer_done = _local_step(x[0], loss_target[0], cond, fetch_weights, prefetch_weights,
                                                           p, emit_grads)

    small_order = [n for n in SMALL_NAMES if n != "ada_b"]
    packed = _pack([dcond] + [small[n] for n in small_order]).astype(BF16)
    mine = lax.dynamic_update_slice(lax.empty((N_DEV,) + packed.shape, BF16), packed[None], (me, 0, 0))
    gathering = _split_start("small_grads_send_start", [mine], 4, _gather_first_copies())
    scatter_token = first_layer_done(gathering[-1])
    while sharing:
        retire_share(scatter_token)
    arrived, = _split_wait("small_grads_send_wait", gathering, _gather_first_copies(), stacks[BIG_NAMES[0]][0])
    gathered_small = _gather_pass_on(arrived, "small_grads_pass_on")
    total = _sum8(gathered_small)
    shapes = [dcond.shape] + [small[n].shape for n in small_order]
    tot = dict(zip(["ada_b"] + small_order, _unpack(total, shapes)))
    grads = {n: tot[n] for n in SMALL_NAMES}
    grads["ada_b"] = tot["ada_b"].reshape(nl, 9 * d)
    grads["conv_w"] = lax.dynamic_slice_in_dim(tot["conv_w"], kme * conv_w.shape[2], conv_w.shape[2], axis=2)
    grads["s5_glu_w"] = lax.dynamic_slice_in_dim(tot["s5_glu_w"], kme * s5_glu_w.shape[1], s5_glu_w.shape[1], axis=1)

    dcond_all = gathered_small.reshape(N_DEV, -1)[:, :dcond.size].reshape(N_DEV, nl, 9 * d)
    dcond_loc = jnp.swapaxes(lax.dynamic_slice_in_dim(dcond_all, kme * n_ada, n_ada, axis=2), 0, 1)
    g_ada, d_ada, m_ada, v_ada = _ada_w_update(cact_all, dcond_loc, ada_w, m_ada_w, v_ada_w)

    while scattering or sharing:
        retire(g_ada)
    delta, new_m, new_v = {}, {}, {}
    for n in BIG_NAMES:
        grads[n], delta[n], new_m[n], new_v[n] = (jnp.swapaxes(a, 1, 2) if n == "w_mix_in" else a for a in stacks[n])

    grads["ada_w"], delta["ada_w"], new_m["ada_w"], new_v["ada_w"] = g_ada, d_ada, m_ada, v_ada
    wide = ("s5_b_re", "s5_b_im")
    for names, call, steps in (([n for n in SMALL_NAMES if n not in wide], "adamw_small", 1),
                               (list(wide), "adamw_s5_b", DEPTH)):
        outs = _adamw_many(*[[_as_rows(t[n]) for n in names] for t in (w, grads, mom, vel)], call, steps)
        for res, o in zip((delta, new_m, new_v), outs):
            res.update({n: a.reshape(w[n].shape) for n, a in zip(names, o)})

    loss_total = lax.psum(loss[0, 0], ("x", "y", "c"))
    return (loss_total, dx[None], *[grads[n] for n in WEIGHT_ORDER], *[delta[n] for n in WEIGHT_ORDER],
            *[new_m[n] for n in WEIGHT_ORDER], *[new_v[n] for n in WEIGHT_ORDER])
```

```python
import math

import jax
import jax.numpy as jnp
from jax import lax
from jax.experimental import pallas as pl
from jax.experimental.pallas import tpu as pltpu

F32, BF16 = jnp.float32, jnp.bfloat16
EPS = 1e-6
DEPTH = 4
N_DEV = 8
N_CHIP = 4
W_GRP = 256
CHUNK = 128
N_SEG = 8
LANES = 128
FFN_TF = 256
FFN_TF_WIDE = 1408
FFN_TM_WIDE = 512
VMEM_LIMIT = 56 * 1024 * 1024
ADAM_LR, ADAM_B1, ADAM_B2, ADAM_EPS, ADAM_WD, ADAM_STEP = 0.001, 0.9, 0.999, 1e-08, 0.01, 10
MESH_ID = pl.DeviceIdType.MESH
HI = lax.Precision.HIGHEST
ANY = pl.BlockSpec(memory_space=pl.ANY)


def _cp(**kw):
    return pltpu.CompilerParams(vmem_limit_bytes=VMEM_LIMIT, **kw)


def _dot(a, b):
    return jnp.dot(a.astype(BF16), b.astype(BF16), preferred_element_type=F32)


def _dot_nt(a, b):
    return lax.dot_general(a.astype(BF16), b.astype(BF16), (((1,), (1,)), ((), ())), preferred_element_type=F32)


def _dot_tn(a, b):
    return lax.dot_general(a.astype(BF16), b.astype(BF16), (((0,), (0,)), ((), ())), preferred_element_type=F32)


def _dot_hi(a, b):
    return jnp.dot(a, b, preferred_element_type=F32, precision=HI)


def _gelu(x):
    k = 0.7978845608028654
    t = jnp.tanh(k * (x + 0.044715 * x * x * x))
    return 0.5 * x * (1.0 + t), t


def _gelu_grad(x, t):
    k = 0.7978845608028654
    return 0.5 * (1.0 + t) + 0.5 * x * (1.0 - t * t) * k * (1.0 + 3.0 * 0.044715 * x * x)


def _iota(shape, axis):
    return lax.broadcasted_iota(jnp.int32, shape, axis)


def _full(shape):
    nd = len(shape)
    return pl.BlockSpec(shape, lambda *_: (0,) * nd)


def _norm_mod(xv, g, shift, scale):
    r = lax.rsqrt(jnp.mean(xv * xv, axis=-1, keepdims=True) + EPS)
    return (xv * r * g) * (1.0 + scale) + shift


def _norm_mod_bwd(xv, g, scale, dh):
    r = lax.rsqrt(jnp.mean(xv * xv, axis=-1, keepdims=True) + EPS)
    xh = xv * r
    n = xh * g
    dsh = jnp.sum(dh, axis=0, keepdims=True)
    dsc = jnp.sum(dh * n, axis=0, keepdims=True)
    dn = dh * (1.0 + scale)
    dg = jnp.sum(dn * xh, axis=0, keepdims=True)
    dxh = dn * g
    dx = r * (dxh - xh * jnp.mean(dxh * xh, axis=-1, keepdims=True))
    return dx, dsh, dsc, dg


def _tm(s):
    return min(s, 1024)


def _ffn_fwd(x, mod, g, wi, wo):
    s, d = x.shape
    f = wo.shape[0]
    tf, tm = FFN_TF_WIDE, min(s, FFN_TM_WIDE)
    nf, nt = f // tf, s // tm

    def body(x_ref, mod_ref, g_ref, wa_ref, wb_ref, wo_ref, xn_ref, h_ref, a_ref, b_ref, o_ref, acc):
        j = pl.program_id(1)

        @pl.when(j == 0)
        def _():
            hh = _norm_mod(x_ref[...], g_ref[...], mod_ref[0:1, :], mod_ref[1:2, :])
            h_ref[...] = hh.astype(BF16)
            acc[...] = jnp.zeros_like(acc)

        h = h_ref[...]
        a = jnp.dot(h, wa_ref[...], preferred_element_type=F32)
        b = jnp.dot(h, wb_ref[...], preferred_element_type=F32)
        a_ref[...] = a.astype(BF16)
        b_ref[...] = b.astype(BF16)
        u = (a * jax.nn.sigmoid(a)) * b
        acc[...] += jnp.dot(u.astype(BF16), wo_ref[...], preferred_element_type=F32)

        @pl.when(j == nf - 1)
        def _():
            o = acc[...]
            o_ref[...] = o.astype(BF16)
            xn_ref[...] = x_ref[...] + 0.5 * mod_ref[2:3, :] * o

    row = pl.BlockSpec((tm, d), lambda i, j: (i, 0))
    chunk = pl.BlockSpec((tm, tf), lambda i, j: (i, j))
    return pl.pallas_call(
        body, name="ffn_fwd", grid=(nt, nf),
        in_specs=[row, _full((3, d)), _full((1, d)),
                  pl.BlockSpec((None, d, tf), lambda i, j: (0, 0, j)),
                  pl.BlockSpec((None, d, tf), lambda i, j: (1, 0, j)),
                  pl.BlockSpec((tf, d), lambda i, j: (j, 0))],
        out_specs=[row, row, chunk, chunk, row],
        out_shape=[jax.ShapeDtypeStruct((s, d), F32), jax.ShapeDtypeStruct((s, d), BF16),
                   jax.ShapeDtypeStruct((s, f), BF16), jax.ShapeDtypeStruct((s, f), BF16),
                   jax.ShapeDtypeStruct((s, d), BF16)],
        scratch_shapes=[pltpu.VMEM((tm, d), F32)],
        compiler_params=_cp(dimension_semantics=("parallel", "arbitrary")),
    )(x, mod, g, wi, wi, wo)


def _ffn_bwd_main(dxo, o, gate, h, a, b, wo):
    s, d = dxo.shape
    f = wo.shape[0]
    tf = FFN_TF
    nf = f // tf

    def body(dxo_ref, o_ref, gate_ref, h_ref, a_ref, b_ref, wo_ref, dza_ref, dzb_ref, dwi_ref, dwo_ref, dg_ref, do_s):
        @pl.when(pl.program_id(0) == 0)
        def _():
            dxv = dxo_ref[...]
            do_s[...] = (0.5 * gate_ref[...] * dxv).astype(BF16)
            dg_ref[...] = 0.5 * jnp.sum(o_ref[...].astype(F32) * dxv, axis=0, keepdims=True)

        dov = do_s[...]
        hv = h_ref[...]
        du = lax.dot_general(dov, wo_ref[...], (((1,), (1,)), ((), ())), preferred_element_type=F32)
        av = a_ref[...].astype(F32)
        bv = b_ref[...].astype(F32)
        sa = jax.nn.sigmoid(av)
        si = av * sa
        u = (si * bv).astype(BF16)
        da = (du * bv * (sa * (1.0 + av * (1.0 - sa)))).astype(BF16)
        db = (du * si).astype(BF16)
        dza_ref[...] = da
        dzb_ref[...] = db
        dwo_ref[...] = _dot_tn(u, dov).astype(BF16)
        dwi_ref[0] = _dot_tn(hv, da).astype(BF16)
        dwi_ref[1] = _dot_tn(hv, db).astype(BF16)

    chunk = pl.BlockSpec((s, tf), lambda j: (0, j))
    once = lambda: pl.BlockSpec((s, d), lambda j: (0, 0), pipeline_mode=pl.Buffered(1))
    return pl.pallas_call(
        body, name="ffn_bwd_main", grid=(nf,),
        in_specs=[once(), once(), _full((1, d)), once(), chunk, chunk, pl.BlockSpec((tf, d), lambda j: (j, 0))],
        out_specs=[chunk, chunk, pl.BlockSpec((2, d, tf), lambda j: (0, 0, j)),
                   pl.BlockSpec((tf, d), lambda j: (j, 0)), _full((1, d))],
        out_shape=[jax.ShapeDtypeStruct((s, f), BF16), jax.ShapeDtypeStruct((s, f), BF16),
                   jax.ShapeDtypeStruct((2, d, f), BF16), jax.ShapeDtypeStruct((f, d), BF16),
                   jax.ShapeDtypeStruct((1, d), F32)],
        scratch_shapes=[pltpu.VMEM((s, d), BF16)],
        compiler_params=_cp(dimension_semantics=("arbitrary",)),
    )(dxo, o, gate, h, a, b, wo)


def _ffn_bwd_in(dza, dzb, wi, x, dxo, mod, g):
    s, d = x.shape
    f = dza.shape[1]
    tf, tm = FFN_TF_WIDE, min(s, FFN_TM_WIDE)
    nf, nt = f // tf, s // tm

    def body(dza_ref, dzb_ref, wa_ref, wb_ref, x_ref, dxo_ref, mod_ref, g_ref, dx_ref, rows_ref, acc):
        j, i = pl.program_id(0), pl.program_id(1)
        rows = pl.ds(pl.multiple_of(i * tm, tm), tm)

        @pl.when(jnp.logical_and(i == 0, j == 0))
        def _():
            rows_ref[...] = jnp.zeros_like(rows_ref)

        part = (lax.dot_general(dza_ref[...], wa_ref[...], (((1,), (1,)), ((), ())), preferred_element_type=F32)
                + lax.dot_general(dzb_ref[...], wb_ref[...], (((1,), (1,)), ((), ())), preferred_element_type=F32))

        @pl.when(j == 0)
        def _():
            acc[rows, :] = part

        @pl.when(jnp.logical_and(j > 0, j < nf - 1))
        def _():
            acc[rows, :] += part

        @pl.when(j == nf - 1)
        def _():
            dh = part + acc[rows, :] if nf > 1 else part
            dx, dsh, dsc, dg = _norm_mod_bwd(x_ref[...], g_ref[...], mod_ref[1:2, :], dh)
            dx_ref[...] = dx + dxo_ref[...]
            rows_ref[0:1, :] += dsh
            rows_ref[1:2, :] += dsc
            rows_ref[2:3, :] += dg

    late = pl.BlockSpec((tm, d), lambda j, i: (jnp.where(j == nf - 1, i, 0), 0))
    chunk = pl.BlockSpec((tm, tf), lambda j, i: (i, j))
    return pl.pallas_call(
        body, name="ffn_bwd_in", grid=(nf, nt),
        in_specs=[chunk, chunk,
                  pl.BlockSpec((None, d, tf), lambda j, i: (0, 0, j)),
                  pl.BlockSpec((None, d, tf), lambda j, i: (1, 0, j)),
                  late, late, _full((3, d)), _full((1, d))],
        out_specs=[late, _full((8, d))],
        out_shape=[jax.ShapeDtypeStruct((s, d), F32), jax.ShapeDtypeStruct((8, d), F32)],
        scratch_shapes=[pltpu.VMEM((s, d), F32)],
        compiler_params=_cp(dimension_semantics=("arbitrary", "arbitrary")),
    )(dza, dzb, wi, wi, x, dxo, mod, g)


def _mix_in_fwd(x, mod, g, wmit):
    s, d = x.shape
    p = wmit.shape[0]
    tm = _tm(s)

    def body(x_ref, mod_ref, g_ref, w_ref, z_ref, h_ref):
        hh = _norm_mod(x_ref[...], g_ref[...], mod_ref[0:1, :], mod_ref[1:2, :]).astype(BF16)
        h_ref[...] = hh
        z_ref[...] = lax.dot_general(hh, w_ref[...], (((1,), (1,)), ((), ())), preferred_element_type=F32)

    row = pl.BlockSpec((tm, d), lambda i: (i, 0))
    return pl.pallas_call(
        body, name="mix_in_fwd", grid=(s // tm,),
        in_specs=[row, _full((3, d)), _full((1, d)), _full((p, d))],
        out_specs=[pl.BlockSpec((tm, p), lambda i: (i, 0)), row],
        out_shape=[jax.ShapeDtypeStruct((s, p), F32), jax.ShapeDtypeStruct((s, d), BF16)],
        compiler_params=_cp(dimension_semantics=("parallel",)),
    )(x, mod, g, wmit)


def _mix_in_bwd(dzs, h, wmit, x, dxo, mod, g):
    s, d = x.shape
    p = wmit.shape[0]
    tm = min(s, 512)
    nt = s // tm

    def body(za_ref, zb_ref, zc_ref, zd_ref, h_ref, w_ref, x_ref, dxo_ref, mod_ref, g_ref,
             dx_ref, rows_ref, dw_ref, acc):
        i = pl.program_id(0)

        @pl.when(i == 0)
        def _():
            rows_ref[...] = jnp.zeros_like(rows_ref)
            acc[...] = jnp.zeros_like(acc)

        dz = jnp.concatenate([za_ref[...], zb_ref[...], zc_ref[...], zd_ref[...]], axis=1).astype(BF16)
        acc[...] += _dot_tn(dz, h_ref[...])
        dh = jnp.dot(dz, w_ref[...], preferred_element_type=F32)
        dx, dsh, dsc, dg = _norm_mod_bwd(x_ref[...], g_ref[...], mod_ref[1:2, :], dh)
        dx_ref[...] = dx + dxo_ref[...]
        rows_ref[0:1, :] += dsh
        rows_ref[1:2, :] += dsc
        rows_ref[2:3, :] += dg

        @pl.when(i == nt - 1)
        def _():
            dw_ref[...] = acc[...].astype(BF16)

    row = pl.BlockSpec((tm, d), lambda i: (i, 0))
    zspecs = [pl.BlockSpec((tm, z.shape[1]), lambda i: (i, 0)) for z in dzs]
    return pl.pallas_call(
        body, name="mix_in_bwd", grid=(nt,),
        in_specs=zspecs + [row, _full((p, d)), row, row, _full((3, d)), _full((1, d))],
        out_specs=[row, _full((8, d)), _full((p, d))],
        out_shape=[jax.ShapeDtypeStruct((s, d), F32), jax.ShapeDtypeStruct((8, d), F32),
                   jax.ShapeDtypeStruct((p, d), BF16)],
        scratch_shapes=[pltpu.VMEM((p, d), F32)],
        compiler_params=_cp(dimension_semantics=("arbitrary",)),
    )(*dzs, h, wmit, x, dxo, mod, g)


def _group_norm(ys, mng):
    outs, hats, rs = [], [], []
    for k, y in enumerate(ys):
        r = lax.rsqrt(jnp.mean(y * y, axis=-1, keepdims=True) + EPS)
        yh = y * r
        hats.append(yh)
        rs.append(r)
        outs.append(yh * mng[:, k * W_GRP:(k + 1) * W_GRP])
    return jnp.concatenate(outs, axis=1), hats, rs


def _s5_glu(y, gw, gb):
    yg, t = _gelu(y)
    gate = jax.nn.sigmoid(_dot(yg, gw) + gb)
    return yg * gate, yg, t, gate


def _mix_out_fwd(ys, glu, mng, wmo, x, gate):
    s, d = x.shape
    tm = _tm(s)

    def body(ya, yb, yc, ypre, gw_ref, gb_ref, mng_ref, w_ref, x_ref, gate_ref, xn_ref, m_ref):
        yd = _s5_glu(ypre[...], gw_ref[...], gb_ref[...])[0]
        yn, _, _ = _group_norm([ya[...], yb[...], yc[...], yd], mng_ref[...])
        m = jnp.dot(yn.astype(BF16), w_ref[...], preferred_element_type=F32)
        m_ref[...] = m
        xn_ref[...] = x_ref[...] + gate_ref[...] * m

    row = pl.BlockSpec((tm, d), lambda i: (i, 0))
    grp = pl.BlockSpec((tm, W_GRP), lambda i: (i, 0))
    return pl.pallas_call(
        body, name="mix_out_fwd", grid=(s // tm,),
        in_specs=[grp, grp, grp, grp, _full((W_GRP, W_GRP)), _full((1, W_GRP)), _full((1, d)), _full((d, d)), row,
                  _full((1, d))],
        out_specs=[row, row],
        out_shape=[jax.ShapeDtypeStruct((s, d), F32), jax.ShapeDtypeStruct((s, d), F32)],
        compiler_params=_cp(dimension_semantics=("parallel",)),
    )(*ys, *glu, mng, wmo, x, gate)


def _mix_out_bwd(dxo, m, gate, ys, glu, mng, wmo):
    s, d = dxo.shape
    tm = min(s, 512)
    nt = s // tm

    def body(dxo_ref, m_ref, gate_ref, ya, yb, yc, ypre, gw_ref, gb_ref, mng_ref, w_ref,
             dya, dyb, dyc, dypre, dgate_ref, dmng_ref, dw_ref, dgw_ref, dgb_ref, acc):
        i = pl.program_id(0)

        @pl.when(i == 0)
        def _():
            dgate_ref[...] = jnp.zeros_like(dgate_ref)
            dmng_ref[...] = jnp.zeros_like(dmng_ref)
            dgw_ref[...] = jnp.zeros_like(dgw_ref)
            dgb_ref[...] = jnp.zeros_like(dgb_ref)
            acc[...] = jnp.zeros_like(acc)

        dxv = dxo_ref[...]
        dgate_ref[...] += jnp.sum(m_ref[...] * dxv, axis=0, keepdims=True)
        dm = (gate_ref[...] * dxv).astype(BF16)
        mng = mng_ref[...]
        gw = gw_ref[...]
        yp = ypre[...]
        yd, yg, t, glu_gate = _s5_glu(yp, gw, gb_ref[...])
        yn, hats, rs = _group_norm([ya[...], yb[...], yc[...], yd], mng)
        acc[...] += _dot_tn(yn, dm)
        dyn = lax.dot_general(dm, w_ref[...], (((1,), (1,)), ((), ())), preferred_element_type=F32)
        dmng_parts, dys = [], []
        for k, (yh, r) in enumerate(zip(hats, rs)):
            dk = dyn[:, k * W_GRP:(k + 1) * W_GRP]
            dmng_parts.append(jnp.sum(dk * yh, axis=0, keepdims=True))
            dyh = dk * mng[:, k * W_GRP:(k + 1) * W_GRP]
            dys.append(r * (dyh - yh * jnp.mean(dyh * yh, axis=-1, keepdims=True)))
        dmng_ref[...] += jnp.concatenate(dmng_parts, axis=1)
        dya[...], dyb[...], dyc[...] = dys[0], dys[1], dys[2]
        dyd = dys[3]
        dlin = dyd * yg * glu_gate * (1.0 - glu_gate)
        dgw_ref[...] += _dot_tn(yg, dlin)
        dgb_ref[...] += jnp.sum(dlin, axis=0, keepdims=True)
        dypre[...] = (dyd * glu_gate + _dot_nt(dlin, gw)) * _gelu_grad(yp, t)

        @pl.when(i == nt - 1)
        def _():
            dw_ref[...] = acc[...].astype(BF16)

    row = pl.BlockSpec((tm, d), lambda i: (i, 0))
    grp = pl.BlockSpec((tm, W_GRP), lambda i: (i, 0))
    return pl.pallas_call(
        body, name="mix_out_bwd", grid=(nt,),
        in_specs=[row, row, _full((1, d)), grp, grp, grp, grp, _full((W_GRP, W_GRP)), _full((1, W_GRP)), _full((1, d)),
                  _full((d, d))],
        out_specs=[grp, grp, grp, grp, _full((1, d)), _full((1, d)), _full((d, d)), _full((W_GRP, W_GRP)),
                   _full((1, W_GRP))],
        out_shape=[jax.ShapeDtypeStruct((s, W_GRP), F32)] * 4
        + [jax.ShapeDtypeStruct((1, d), F32), jax.ShapeDtypeStruct((1, d), F32), jax.ShapeDtypeStruct((d, d), BF16),
           jax.ShapeDtypeStruct((W_GRP, W_GRP), F32), jax.ShapeDtypeStruct((1, W_GRP), F32)],
        scratch_shapes=[pltpu.VMEM((d, d), F32)],
        compiler_params=_cp(dimension_semantics=("arbitrary",)),
    )(dxo, m, gate, *ys, *glu, mng, wmo)


def _sgu_consts():
    r = _iota((W_GRP, W_GRP), 0) >> 6
    c = _iota((W_GRP, W_GRP), 1) >> 6
    avg = jnp.where(r == c, 1.0 / 64.0, 0.0).astype(F32)
    tril = _iota((CHUNK, CHUNK), 0) >= _iota((CHUNK, CHUNK), 1)
    head = _iota((CHUNK, W_GRP), 1) >> 6
    return avg, tril, head


def _sgu_pre(za, avg):
    zg, t = _gelu(za)
    u, v = zg[:, :W_GRP], zg[:, W_GRP:]
    mu = _dot_hi(v, avg)
    vc = v - mu
    r = lax.rsqrt(_dot_hi(vc * vc, avg) + EPS)
    return t, u, vc * r, r


def _sgu_fwd(z, sgu_w, bias_full):
    s = z.shape[0]
    tm = min(s, 512)

    def body(za_ref, w_ref, bias_ref, ya_ref):
        avg, tril, head = _sgu_consts()
        _, u, vn, _ = _sgu_pre(za_ref[...], avg)
        wm = [jnp.where(tril, w_ref[h], 0.0).astype(BF16) for h in range(4)]
        vb = vn.astype(BF16)
        for n in range(tm // CHUNK):
            rows = slice(n * CHUNK, (n + 1) * CHUNK)
            mixed = bias_ref[...]
            for h in range(4):
                mixed = mixed + jnp.where(head == h, jnp.dot(wm[h], vb[rows], preferred_element_type=F32), 0.0)
            ya_ref[rows, :] = u[rows] * mixed

    return pl.pallas_call(
        body, name="sgu_fwd", grid=(s // tm,),
        in_specs=[pl.BlockSpec((tm, 2 * W_GRP), lambda i: (i, 0)), _full((4, CHUNK, CHUNK)), _full((CHUNK, W_GRP))],
        out_specs=pl.BlockSpec((tm, W_GRP), lambda i: (i, 0)),
        out_shape=jax.ShapeDtypeStruct((s, W_GRP), F32),
        compiler_params=_cp(dimension_semantics=("parallel",)),
    )(z, sgu_w, bias_full)


def _sgu_bwd(z, dya, sgu_w, bias_full):
    s = z.shape[0]
    tm = min(s, 512)
    nt = s // tm

    def body(za_ref, dya_ref, w_ref, bias_ref, dza_ref, dw_ref, db_ref, du_s, dvn_s):
        i = pl.program_id(0)

        @pl.when(i == 0)
        def _():
            dw_ref[...] = jnp.zeros_like(dw_ref)
            db_ref[...] = jnp.zeros_like(db_ref)

        avg, tril, head = _sgu_consts()
        za = za_ref[...]
        t, u, vn, r = _sgu_pre(za, avg)
        wm = [jnp.where(tril, w_ref[h], 0.0).astype(BF16) for h in range(4)]
        vb = vn.astype(BF16)
        dya = dya_ref[...]
        dw = [jnp.zeros((CHUNK, CHUNK), F32) for _ in range(4)]
        db = jnp.zeros((CHUNK, W_GRP), F32)
        for n in range(tm // CHUNK):
            rows = slice(n * CHUNK, (n + 1) * CHUNK)
            mixed = bias_ref[...]
            for h in range(4):
                mixed = mixed + jnp.where(head == h, jnp.dot(wm[h], vb[rows], preferred_element_type=F32), 0.0)
            dmix = dya[rows] * u[rows]
            du_s[rows, :] = dya[rows] * mixed
            db = db + dmix
            dmb = dmix.astype(BF16)
            dvn = jnp.zeros((CHUNK, W_GRP), F32)
            for h in range(4):
                dmh = jnp.where(head == h, dmix, 0.0)
                dw[h] = dw[h] + _dot_nt(dmh, vb[rows])
                dvn = dvn + jnp.where(head == h, _dot_tn(wm[h], dmb), 0.0)
            dvn_s[rows, :] = dvn
        for h in range(4):
            dw_ref[h] += jnp.where(tril, dw[h], 0.0)
        sel = ((_iota((W_GRP, CHUNK), 0) >> 6) == _iota((W_GRP, CHUNK), 1)).astype(F32)
        db_ref[...] += _dot_hi(db, sel)
        dvn = dvn_s[...]
        dv = r * (dvn - _dot_hi(dvn, avg) - vn * _dot_hi(dvn * vn, avg))
        dzg = jnp.concatenate([du_s[...], dv], axis=1)
        dza_ref[...] = dzg * _gelu_grad(za, t)

    return pl.pallas_call(
        body, name="sgu_bwd", grid=(nt,),
        in_specs=[pl.BlockSpec((tm, 2 * W_GRP), lambda i: (i, 0)), pl.BlockSpec((tm, W_GRP), lambda i: (i, 0)),
                  _full((4, CHUNK, CHUNK)), _full((CHUNK, W_GRP))],
        out_specs=[pl.BlockSpec((tm, 2 * W_GRP), lambda i: (i, 0)), _full((4, CHUNK, CHUNK)), _full((CHUNK, CHUNK))],
        out_shape=[jax.ShapeDtypeStruct((s, 2 * W_GRP), F32), jax.ShapeDtypeStruct((4, CHUNK, CHUNK), F32),
                   jax.ShapeDtypeStruct((CHUNK, CHUNK), F32)],
        scratch_shapes=[pltpu.VMEM((tm, W_GRP), F32), pltpu.VMEM((tm, W_GRP), F32)],
        compiler_params=_cp(dimension_semantics=("arbitrary",)),
    )(z, dya, sgu_w, bias_full)


def _shift_down(x, k):
    return jnp.where(_iota(x.shape, 0) < k, 0.0, pltpu.roll(x, k, 0))


def _shift_up(x, k):
    n = x.shape[0]
    return jnp.where(_iota(x.shape, 0) >= n - k, 0.0, pltpu.roll(x, n - k, 0))


def _by_pool_group(shape, v2, v4, v8, v16):
    col = _iota(shape, 1)
    return jnp.where(col < 64, v2, jnp.where(col < 128, v4, jnp.where(col < 192, v8, v16)))


def _pool_core(zb, pw2):
    s2 = zb + _shift_down(zb, 1)
    s4 = s2 + _shift_down(s2, 2)
    s8 = s4 + _shift_down(s4, 4)
    s16 = s8 + _shift_down(s8, 8)
    win = _by_pool_group(zb.shape, s2, s4, s8, s16)
    wlen = _by_pool_group(zb.shape, 2.0, 4.0, 8.0, 16.0)
    cnt = jnp.minimum((_iota(zb.shape, 0) + 1).astype(F32), wlen)
    p = win / cnt - zb
    wt = jnp.tile(pw2, (1, 4))
    wbd = jnp.where((_iota(wt.shape, 0) >> 6) == (_iota(wt.shape, 1) >> 6), wt, 0.0).astype(BF16)
    return p, cnt, wbd


def _conv_core(zc, cw):
    bg, cg, xh = zc[:, :W_GRP], zc[:, W_GRP:2 * W_GRP], zc[:, 2 * W_GRP:]
    y = cg * xh
    y1, y2 = _shift_down(y, 1), _shift_down(y, 2)
    out = cw[2:3, :] * y + cw[1:2, :] * y1 + cw[0:1, :] * y2
    return bg, cg, xh, y, y1, y2, out


def _poolconv_fwd(z, pw2, pscale, cw):
    s = z.shape[0]

    def body(zb_ref, zc_ref, pw_ref, ps_ref, cw_ref, yb_ref, yc_ref):
        p, _, wbd = _pool_core(zb_ref[...], pw_ref[...])
        yb_ref[...] = jnp.dot(p.astype(BF16), wbd, preferred_element_type=F32) * ps_ref[...]
        bg, _, _, _, _, _, out = _conv_core(zc_ref[...], cw_ref[...])
        yc_ref[...] = bg * out

    return pl.pallas_call(
        body, name="poolconv_fwd", grid=(1,),
        in_specs=[pl.BlockSpec((s, W_GRP), lambda i: (0, 2)), pl.BlockSpec((s, 3 * W_GRP), lambda i: (0, 1)),
                  _full((W_GRP, 64)), _full((1, W_GRP)), _full((3, W_GRP))],
        out_specs=[_full((s, W_GRP)), _full((s, W_GRP))],
        out_shape=[jax.ShapeDtypeStruct((s, W_GRP), F32)] * 2,
        compiler_params=_cp(dimension_semantics=("arbitrary",)),
    )(z, z, pw2, pscale, cw)


def _poolconv_bwd(z, dyb, dyc, pw2, pscale, cw):
    s = z.shape[0]

    def body(zb_ref, zc_ref, dyb_ref, dyc_ref, pw_ref, ps_ref, cw_ref, dzb_ref, dzc_ref, dw_ref, dps_ref, dcw_ref):
        zb = zb_ref[...]
        p, cnt, wbd = _pool_core(zb, pw_ref[...])
        pb = p.astype(BF16)
        out = jnp.dot(pb, wbd, preferred_element_type=F32)
        dyb = dyb_ref[...]
        dps_ref[...] = jnp.sum(dyb * out, axis=0, keepdims=True)
        dout = (dyb * ps_ref[...]).astype(BF16)
        dw = _dot_tn(pb, dout)
        dw_ref[...] = jnp.where((_iota(dw.shape, 0) >> 6) == (_iota(dw.shape, 1) >> 6), dw, 0.0)
        dp = lax.dot_general(dout, wbd, (((1,), (1,)), ((), ())), preferred_element_type=F32)
        dwin = dp / cnt
        t2 = dwin + _shift_up(dwin, 1)
        t4 = t2 + _shift_up(t2, 2)
        t8 = t4 + _shift_up(t4, 4)
        t16 = t8 + _shift_up(t8, 8)
        dzb_ref[...] = _by_pool_group(zb.shape, t2, t4, t8, t16) - dp

        cw = cw_ref[...]
        bg, cg, xh, y, y1, y2, out = _conv_core(zc_ref[...], cw)
        dyc = dyc_ref[...]
        dout = dyc * bg
        dcw_ref[...] = jnp.zeros_like(dcw_ref)
        dcw_ref[0:1, :] = jnp.sum(dout * y2, axis=0, keepdims=True)
        dcw_ref[1:2, :] = jnp.sum(dout * y1, axis=0, keepdims=True)
        dcw_ref[2:3, :] = jnp.sum(dout * y, axis=0, keepdims=True)
        dy = cw[2:3, :] * dout + cw[1:2, :] * _shift_up(dout, 1) + cw[0:1, :] * _shift_up(dout, 2)
        dzc_ref[...] = jnp.concatenate([dyc * out, dy * xh, dy * cg], axis=1)

    return pl.pallas_call(
        body, name="poolconv_bwd", grid=(1,),
        in_specs=[pl.BlockSpec((s, W_GRP), lambda i: (0, 2)), pl.BlockSpec((s, 3 * W_GRP), lambda i: (0, 1)),
                  _full((s, W_GRP)), _full((s, W_GRP)), _full((W_GRP, 64)), _full((1, W_GRP)), _full((3, W_GRP))],
        out_specs=[_full((s, W_GRP)), _full((s, 3 * W_GRP)), _full((W_GRP, W_GRP)), _full((1, W_GRP)), _full((8, W_GRP))],
        out_shape=[jax.ShapeDtypeStruct((s, W_GRP), F32), jax.ShapeDtypeStruct((s, 3 * W_GRP), F32),
                   jax.ShapeDtypeStruct((W_GRP, W_GRP), F32), jax.ShapeDtypeStruct((1, W_GRP), F32),
                   jax.ShapeDtypeStruct((8, W_GRP), F32)],
        compiler_params=_cp(dimension_semantics=("arbitrary",)),
    )(z, z, dyb, dyc, pw2, pscale, cw)


N_STATE = 1024
HALF_STATE = N_STATE // 2
HALF_CH = W_GRP // 2
N_SLAB = HALF_STATE // LANES


def _s5_disc(lre, lim, ldt):
    dt = jnp.exp(ldt)
    mag = jnp.exp(lre * dt)
    ang = lim * dt
    ar, ai = mag * jnp.cos(ang), mag * jnp.sin(ang)
    nr, ni = ar - 1.0, ai
    den = lre * lre + lim * lim
    kr = (nr * lre + ni * lim) / den
    ki = (ni * lre - nr * lim) / den
    return ar, ai, kr, ki


def _s5_mats(colp, br, bi, cr, ci):
    _, _, kr, ki = _s5_disc(colp[:, 0:1], colp[:, 1:2], colp[:, 2:3])
    bbr = kr * br - ki * bi
    bbi = kr * bi + ki * br
    bmask = (_iota((HALF_STATE, HALF_CH), 0) >> 6) == (_iota((HALF_STATE, HALF_CH), 1) >> 4)
    cmask = (_iota((HALF_CH, HALF_STATE), 0) >> 4) == (_iota((HALF_CH, HALF_STATE), 1) >> 6)
    btr = jnp.where(bmask, jnp.tile(bbr, (1, 8)), 0.0).astype(BF16)
    bti = jnp.where(bmask, jnp.tile(bbi, (1, 8)), 0.0).astype(BF16)
    ctr = jnp.where(cmask, jnp.tile(cr, (1, 8)), 0.0).astype(BF16)
    cti = jnp.where(cmask, jnp.tile(ci, (1, 8)), 0.0).astype(BF16)
    return kr, ki, btr, bti, ctr, cti, bmask, cmask


def _slab(q):
    return slice(q * LANES, (q + 1) * LANES)


def _cmul(ar, ai, br, bi):
    return ar * br - ai * bi, ar * bi + ai * br


def _sub_shift(x, k, up):
    row = _iota(x.shape, 0)
    if up:
        return jnp.where(row >= N_SEG - k, 0.0, pltpu.roll(x, N_SEG - k, 0))
    return jnp.where(row < k, 0.0, pltpu.roll(x, k, 0))


def _seg_rows(j):
    return pl.ds(pl.multiple_of(j * N_SEG, N_SEG), N_SEG)


def _interleave(src, dst, seg):
    def step(j, carry):
        dst[_seg_rows(j), :] = src[pl.ds(j, N_SEG, stride=seg), :]
        return carry
    lax.fori_loop(0, seg, step, 0)


def _deinterleave(src, dst, seg):
    def step(j, carry):
        dst[pl.ds(j, N_SEG, stride=seg), :] = src[_seg_rows(j), :]
        return carry
    lax.fori_loop(0, seg, step, 0)


def _scan(xr, xi, ar_row, ai_row, seg, reverse, states=None):
    nlog = int(math.log2(seg))
    assert (1 << nlog) == seg
    grads = []
    for q0 in range(0, N_SLAB, 4):
        qs = list(range(q0, q0 + 4))
        aq = [(jnp.broadcast_to(ar_row[:, _slab(q)], (N_SEG, LANES)),
               jnp.broadcast_to(ai_row[:, _slab(q)], (N_SEG, LANES))) for q in qs]
        zero = jnp.zeros((N_SEG, LANES), F32)

        def local(jj, carry, qs=qs, aq=aq):
            j = seg - 1 - jj if reverse else jj
            out = []
            for n, q in enumerate(qs):
                rows = _seg_rows(j)
                pr, pi = _cmul(aq[n][0], aq[n][1], carry[2 * n], carry[2 * n + 1])
                nr = pr + xr[q, rows, :]
                ni = pi + xi[q, rows, :]
                xr[q, rows, :] = nr
                xi[q, rows, :] = ni
                out += [nr, ni]
            return tuple(out)

        fin = lax.fori_loop(0, seg, local, (zero,) * 8)
        cins = []
        for n in range(4):
            er, ei = fin[2 * n], fin[2 * n + 1]
            pr, pi = aq[n]
            for _ in range(nlog):
                pr, pi = _cmul(pr, pi, pr, pi)
            yr, yi = er, ei
            for k in (1, 2, 4):
                sr, si = _cmul(pr, pi, _sub_shift(yr, k, reverse), _sub_shift(yi, k, reverse))
                yr, yi = yr + sr, yi + si
                pr, pi = _cmul(pr, pi, pr, pi)
            cins.append((_sub_shift(yr, 1, reverse), _sub_shift(yi, 1, reverse)))

        def fix(jj, carry, qs=qs, aq=aq, cins=cins):
            j = seg - 1 - jj if reverse else jj
            out, sums = [], []
            for n, q in enumerate(qs):
                rows = _seg_rows(j)
                pwr, pwi = carry[2 * n], carry[2 * n + 1]
                cr, ci = _cmul(pwr, pwi, cins[n][0], cins[n][1])
                v_r, v_i = xr[q, rows, :] + cr, xi[q, rows, :] + ci
                xr[q, rows, :] = v_r
                xi[q, rows, :] = v_i
                nr, ni = _cmul(pwr, pwi, aq[n][0], aq[n][1])
                out += [nr, ni]
                if states is not None:
                    prev = _seg_rows(j - 1)
                    p_r, p_i = states[0][q, prev, :], states[1][q, prev, :]
                    sums += [carry[8 + 2 * n] + v_r * p_r + v_i * p_i, carry[9 + 2 * n] - v_r * p_i + v_i * p_r]
            return tuple(out + sums)

        powers = tuple(v for pair in aq for v in pair)
        if states is None:
            lax.fori_loop(0, seg, fix, powers)
            continue
        assert reverse
        fix_last = lax.fori_loop(0, seg - 1, fix, powers + (zero,) * 8)
        first = _seg_rows(0)
        for n, q in enumerate(qs):
            cr, ci = _cmul(fix_last[2 * n], fix_last[2 * n + 1], cins[n][0], cins[n][1])
            v_r, v_i = xr[q, first, :] + cr, xi[q, first, :] + ci
            xr[q, first, :] = v_r
            xi[q, first, :] = v_i
            p_r = _sub_shift(states[0][q, _seg_rows(seg - 1), :], 1, False)
            p_i = _sub_shift(states[1][q, _seg_rows(seg - 1), :], 1, False)
            grads.append((jnp.sum(fix_last[8 + 2 * n] + v_r * p_r + v_i * p_i, axis=0, keepdims=True),
                          jnp.sum(fix_last[9 + 2 * n] - v_r * p_i + v_i * p_r, axis=0, keepdims=True)))
    return grads


def _s5_forward_states(u, btr, bti, ar_row, ai_row, xr, xi, seg):
    ub = u.astype(BF16)
    for q in range(N_SLAB):
        xr[q] = _dot_nt(ub, btr[_slab(q), :])
        xi[q] = _dot_nt(ub, bti[_slab(q), :])
    _scan(xr, xi, ar_row, ai_row, seg, False)


def _s5_readout(u, xr, xi, ctr, cti, d):
    y = d * u
    for q in range(N_SLAB):
        y = y + _dot_nt(xr[q], ctr[:, _slab(q)]) - _dot_nt(xi[q], cti[:, _slab(q)])
    return y


def _s5_param_specs():
    return [pl.BlockSpec((3, HALF_STATE), lambda i: (0, i)), pl.BlockSpec((HALF_STATE, 3), lambda i: (i, 0)),
            pl.BlockSpec((HALF_STATE, 16), lambda i: (i, 0)), pl.BlockSpec((HALF_STATE, 16), lambda i: (i, 0)),
            pl.BlockSpec((HALF_CH, 64), lambda i: (i, 0)), pl.BlockSpec((HALF_CH, 64), lambda i: (i, 0)),
            pl.BlockSpec((1, HALF_CH), lambda i: (0, i))]


def _s5_core_fwd(z, sp):
    s = z.shape[0]
    seg = s // N_SEG

    def body(u_ref, rowp, colp, br, bi, cr, ci, d_ref, y_ref, xr, xi, us, ys):
        ar, ai, _, _ = _s5_disc(rowp[0:1, :], rowp[1:2, :], rowp[2:3, :])
        _, _, btr, bti, ctr, cti, _, _ = _s5_mats(colp[...], br[...], bi[...], cr[...], ci[...])
        _interleave(u_ref, us, seg)
        u = us[...]
        _s5_forward_states(u, btr, bti, ar, ai, xr, xi, seg)
        ys[...] = _s5_readout(u, xr, xi, ctr, cti, d_ref[...])
        _deinterleave(ys, y_ref, seg)

    return pl.pallas_call(
        body, name="s5_core_fwd", grid=(2,),
        in_specs=[pl.BlockSpec((s, HALF_CH), lambda i: (0, 12 + i))] + _s5_param_specs(),
        out_specs=pl.BlockSpec((s, HALF_CH), lambda i: (0, i)),
        out_shape=jax.ShapeDtypeStruct((s, W_GRP), F32),
        scratch_shapes=[pltpu.VMEM((N_SLAB, s, LANES), F32)] * 2 + [pltpu.VMEM((s, HALF_CH), F32)] * 2,
        compiler_params=_cp(dimension_semantics=("parallel",)),
    )(z, *sp)


def _s5_core_bwd(z, dy, sp):
    s = z.shape[0]
    seg = s // N_SEG

    def body(u_ref, dy_ref, rowp, colp, br_ref, bi_ref, cr_ref, ci_ref, d_ref,
             du_ref, dbr_ref, dbi_ref, dcr_ref, dci_ref, dd_ref, da_ref, dk_ref,
             xr, xi, gr, gi, us, dys):
        ar, ai, _, _ = _s5_disc(rowp[0:1, :], rowp[1:2, :], rowp[2:3, :])
        br, bi = br_ref[...], bi_ref[...]
        kr, ki, btr, bti, ctr, cti, bmask, cmask = _s5_mats(colp[...], br, bi, cr_ref[...], ci_ref[...])
        _interleave(u_ref, us, seg)
        _interleave(dy_ref, dys, seg)
        u = us[...]
        d = d_ref[...]
        _s5_forward_states(u, btr, bti, ar, ai, xr, xi, seg)

        dy = dys[...]
        dd_ref[...] = jnp.sum(dy * u, axis=0, keepdims=True)
        du = d * dy
        dyb = dy.astype(BF16)
        dctr, dcti = [], []
        for q in range(N_SLAB):
            gr[q] = jnp.dot(dyb, ctr[:, _slab(q)], preferred_element_type=F32)
            gi[q] = -jnp.dot(dyb, cti[:, _slab(q)], preferred_element_type=F32)
            dctr.append(_dot_tn(dyb, xr[q]))
            dcti.append(-_dot_tn(dyb, xi[q]))
        selp = ((_iota((HALF_STATE, 64), 0) & 63) == _iota((HALF_STATE, 64), 1)).astype(F32)
        dcr_ref[...] = _dot_hi(jnp.where(cmask, jnp.concatenate(dctr, axis=1), 0.0), selp)
        dci_ref[...] = _dot_hi(jnp.where(cmask, jnp.concatenate(dcti, axis=1), 0.0), selp)

        da = _scan(gr, gi, ar, -ai, seg, True, states=(xr, xi))
        dar, dai = [p[0] for p in da], [p[1] for p in da]
        da_ref[...] = jnp.zeros_like(da_ref)
        da_ref[0:1, :] = jnp.concatenate(dar, axis=1)
        da_ref[1:2, :] = jnp.concatenate(dai, axis=1)

        ub = u.astype(BF16)
        dbtr, dbti = [], []
        for q in range(N_SLAB):
            g_r, g_i = gr[q].astype(BF16), gi[q].astype(BF16)
            du = du + jnp.dot(g_r, btr[_slab(q), :], preferred_element_type=F32) \
                + jnp.dot(g_i, bti[_slab(q), :], preferred_element_type=F32)
            dbtr.append(_dot_tn(g_r, ub))
            dbti.append(_dot_tn(g_i, ub))
        us[...] = du
        _deinterleave(us, du_ref, seg)
        selc =((_iota((HALF_CH, 16), 0) & 15) == _iota((HALF_CH, 16), 1)).astype(F32)
        dbbr = _dot_hi(jnp.where(bmask, jnp.concatenate(dbtr, axis=0), 0.0), selc)
        dbbi = _dot_hi(jnp.where(bmask, jnp.concatenate(dbti, axis=0), 0.0), selc)
        dbr_ref[...] = kr * dbbr + ki * dbbi
        dbi_ref[...] = kr * dbbi - ki * dbbr
        dk_ref[:, 0:1] = jnp.sum(dbbr * br + dbbi * bi, axis=1, keepdims=True)
        dk_ref[:, 1:2] = jnp.sum(dbbi * br - dbbr * bi, axis=1, keepdims=True)

    half = pl.BlockSpec((s, HALF_CH), lambda i: (0, i))
    return pl.pallas_call(
        body, name="s5_core_bwd", grid=(2,),
        in_specs=[pl.BlockSpec((s, HALF_CH), lambda i: (0, 12 + i)), half] + _s5_param_specs(),
        out_specs=[half, pl.BlockSpec((HALF_STATE, 16), lambda i: (i, 0)), pl.BlockSpec((HALF_STATE, 16), lambda i: (i, 0)),
                   pl.BlockSpec((HALF_CH, 64), lambda i: (i, 0)), pl.BlockSpec((HALF_CH, 64), lambda i: (i, 0)),
                   pl.BlockSpec((1, HALF_CH), lambda i: (0, i)), pl.BlockSpec((8, HALF_STATE), lambda i: (0, i)),
                   pl.BlockSpec((HALF_STATE, 2), lambda i: (i, 0))],
        out_shape=[jax.ShapeDtypeStruct((s, W_GRP), F32), jax.ShapeDtypeStruct((N_STATE, 16), F32),
                   jax.ShapeDtypeStruct((N_STATE, 16), F32), jax.ShapeDtypeStruct((W_GRP, 64), F32),
                   jax.ShapeDtypeStruct((W_GRP, 64), F32), jax.ShapeDtypeStruct((1, W_GRP), F32),
                   jax.ShapeDtypeStruct((8, N_STATE), F32), jax.ShapeDtypeStruct((N_STATE, 2), F32)],
        scratch_shapes=[pltpu.VMEM((N_SLAB, s, LANES), F32)] * 4 + [pltpu.VMEM((s, HALF_CH), F32)] * 2,
        compiler_params=_cp(dimension_semantics=("parallel",)),
    )(z, dy, *sp)


def _s5_param_bwd(lre, lim, ldt, da_r, da_i, dk_r, dk_i):
    n = lre.shape[0]

    def body(lre_ref, lim_ref, ldt_ref, dar_ref, dai_ref, dkr_ref, dki_ref, o_re, o_im, o_dt):
        lre, lim, ldt = lre_ref[...], lim_ref[...], ldt_ref[...]
        dt = jnp.exp(ldt)
        ar, ai, kr, ki = _s5_disc(lre, lim, ldt)
        mag = jnp.exp(lre * dt)
        den = lre * lre + lim * lim
        dkr, dki = dkr_ref[...], dki_ref[...]
        nr, ni = ar - 1.0, ai
        d_ar = dar_ref[...] + (dkr * lre - dki * lim) / den
        d_ai = dai_ref[...] + (dkr * lim + dki * lre) / den
        kk = (kr * dkr + ki * dki) * 2.0 / den
        d_lre = (dkr * nr + dki * ni) / den - kk * lre
        d_lim = (dkr * ni - dki * nr) / den - kk * lim
        d_mag = (d_ar * ar + d_ai * ai) / mag
        d_ang = d_ai * ar - d_ar * ai
        o_re[...] = d_lre + d_mag * mag * dt
        o_im[...] = d_lim + d_ang * dt
        o_dt[...] = jnp.sum((d_mag * mag * lre + d_ang * lim) * dt, axis=1, keepdims=True)

    return pl.pallas_call(
        body, name="s5_param_bwd",
        out_shape=[jax.ShapeDtypeStruct((n, 64), F32), jax.ShapeDtypeStruct((n, 64), F32),
                   jax.ShapeDtypeStruct((n, 1), F32)],
    )(lre, lim, ldt, da_r, da_i, dk_r, dk_i)


def _loss_head(x, fg, target):
    s, d = x.shape
    tm = _tm(s)

    def body(x_ref, fg_ref, t_ref, loss_ref, dx_ref, dfg_ref):
        i = pl.program_id(0)

        @pl.when(i == 0)
        def _():
            loss_ref[...] = jnp.zeros_like(loss_ref)
            dfg_ref[...] = jnp.zeros_like(dfg_ref)

        xv, g = x_ref[...], fg_ref[...]
        r = lax.rsqrt(jnp.mean(xv * xv, axis=-1, keepdims=True) + EPS)
        xh = xv * r
        err = xh * g - t_ref[...]
        loss_ref[...] += 0.5 * jnp.sum(jnp.mean(err * err, axis=-1, keepdims=True), axis=0, keepdims=True)
        dy = err * (1.0 / d)
        dfg_ref[...] += jnp.sum(dy * xh, axis=0, keepdims=True)
        dxh = dy * g
        dx_ref[...] = r * (dxh - xh * jnp.mean(dxh * xh, axis=-1, keepdims=True))

    row = pl.BlockSpec((tm, d), lambda i: (i, 0))
    return pl.pallas_call(
        body, name="loss_head", grid=(s // tm,),
        in_specs=[row, _full((1, d)), row], out_specs=[_full((1, 1)), row, _full((1, d))],
        out_shape=[jax.ShapeDtypeStruct((1, 1), F32), jax.ShapeDtypeStruct((s, d), F32),
                   jax.ShapeDtypeStruct((1, d), F32)],
        compiler_params=_cp(dimension_semantics=("arbitrary",)),
    )(x, fg, target)


ADA_TN = 384


def _cond_fwd(cact, ada_w, ada_b_loc):
    nl, d, n = ada_w.shape

    def body(c_ref, w_ref, b_ref, o_ref):
        o_ref[...] = _dot(c_ref[...], w_ref[...]) + b_ref[...]

    return pl.pallas_call(
        body, name="cond_fwd", grid=(nl, n // ADA_TN),
        in_specs=[_full((N_DEV, d)), pl.BlockSpec((None, d, ADA_TN), lambda l, j: (l, 0, j)),
                  pl.BlockSpec((None, 1, ADA_TN), lambda l, j: (l, 0, j))],
        out_specs=pl.BlockSpec((None, N_DEV, ADA_TN), lambda l, j: (l, 0, j)),
        out_shape=jax.ShapeDtypeStruct((nl, N_DEV, n), F32),
        compiler_params=_cp(dimension_semantics=("parallel", "parallel")),
    )(cact, ada_w, ada_b_loc)


ELEMENTWISE_BLOCK_BYTES = 1 << 20


def _row_tile(r, c, itemsize=4):
    best = None
    for t in range(8, r + 1, 8):
        if r % t == 0 and t * c * itemsize <= ELEMENTWISE_BLOCK_BYTES:
            best = t
    return best if best is not None else r


def _adamw_math(w, g, m, v):
    m = ADAM_B1 * m + (1.0 - ADAM_B1) * g
    v = ADAM_B2 * v + (1.0 - ADAM_B2) * (g * g)
    m_hat = m / (1.0 - ADAM_B1 ** ADAM_STEP)
    v_hat = v / (1.0 - ADAM_B2 ** ADAM_STEP)
    delta = -ADAM_LR * (m_hat / (jnp.sqrt(v_hat) + ADAM_EPS) + ADAM_WD * w)
    return delta, m, v


def _ada_w_update(cact, dcond_loc, w, m, v):
    nl, d, n = w.shape

    def body(c_ref, dc_ref, w_ref, m_ref, v_ref, g_out, d_out, m_out, v_out):
        g = _dot_tn(c_ref[...], dc_ref[...])
        g_out[...] = g
        d_out[...], m_out[...], v_out[...] = _adamw_math(w_ref[...], g, m_ref[...], v_ref[...])

    blk = pl.BlockSpec((None, d, ADA_TN), lambda l, j: (l, 0, j))
    return pl.pallas_call(
        body, name="ada_w_update", grid=(nl, n // ADA_TN),
        in_specs=[_full((N_DEV, d)), pl.BlockSpec((None, N_DEV, ADA_TN), lambda l, j: (l, 0, j)), blk, blk, blk],
        out_specs=[blk] * 4, out_shape=[jax.ShapeDtypeStruct((nl, d, n), F32)] * 4,
        compiler_params=_cp(dimension_semantics=("parallel", "parallel")),
    )(cact, dcond_loc, w, m, v)


def _place():
    x, y, c = lax.axis_index("x"), lax.axis_index("y"), lax.axis_index("c")
    chips = [(1 - x, y), (x, 1 - y), (1 - x, 1 - y)]
    return x, y, c, chips


def _remote(src, dst, send_sem, recv_sem, to):
    return pltpu.make_async_remote_copy(src_ref=src, dst_ref=dst, send_sem=send_sem, recv_sem=recv_sem,
                                        device_id=to, device_id_type=MESH_ID)


def _sems(n):
    return [pltpu.SemaphoreType.DMA((n,)), pltpu.SemaphoreType.DMA((n,))]


def _all_gather8(v, name):
    r, cdim = v.shape

    def body(x_ref, out_ref, stage, send_sems, recv_sems):
        x, y, c, chips = _place()
        sibling = (x, y, 1 - c)

        def slot(px, py, pc):
            return out_ref.at[4 * px + 2 * py + pc]

        first = [_remote(x_ref, slot(x, y, c), send_sems.at[0], recv_sems.at[0], sibling)]
        first += [_remote(x_ref, slot(x, y, c), send_sems.at[1 + j], recv_sems.at[1 + j], (*chip, c))
                  for j, chip in enumerate(chips)]
        for cp in first:
            cp.start()
        pltpu.sync_copy(x_ref, stage)
        pltpu.sync_copy(stage, slot(x, y, c))
        passed = []
        for j, chip in enumerate(chips):
            blk = slot(*chip, c)
            _remote(blk, blk, send_sems.at[1 + j], recv_sems.at[1 + j], (x, y, c)).wait_recv()
            fw = _remote(blk, blk, send_sems.at[4 + j], recv_sems.at[4 + j], sibling)
            fw.start()
            passed.append(fw)
        blk = slot(x, y, 1 - c)
        _remote(blk, blk, send_sems.at[0], recv_sems.at[0], (x, y, c)).wait_recv()
        for j, chip in enumerate(chips):
            blk = slot(*chip, 1 - c)
            _remote(blk, blk, send_sems.at[4 + j], recv_sems.at[4 + j], (x, y, c)).wait_recv()
        for cp in first + passed:
            cp.wait_send()

    return pl.pallas_call(
        body, name=name, out_shape=jax.ShapeDtypeStruct((N_DEV, r, cdim), v.dtype),
        in_specs=[ANY], out_specs=ANY,
        scratch_shapes=[pltpu.VMEM((r, cdim), v.dtype)] + _sems(7),
        compiler_params=_cp(),
    )(v)


def _gather_first_copies():
    def make(refs, send_sems, recv_sems):
        x, y, c, chips = _place()
        mine = refs[0].at[4 * x + 2 * y + c]
        to = [(x, y, 1 - c)] + [(*chip, c) for chip in chips]
        return [_remote(mine, mine, send_sems.at[k], recv_sems.at[k], dev) for k, dev in enumerate(to)]
    return make


def _gather_pass_on(buf, name):
    def body(in_ref, out_ref, send_sems, recv_sems):
        x, y, c, chips = _place()
        passed = []
        for j, chip in enumerate(chips):
            blk = out_ref.at[4 * chip[0] + 2 * chip[1] + c]
            fw = _remote(blk, blk, send_sems.at[j], recv_sems.at[j], (x, y, 1 - c))
            fw.start()
            passed.append(fw)
        for j, chip in enumerate(chips):
            blk = out_ref.at[4 * chip[0] + 2 * chip[1] + 1 - c]
            _remote(blk, blk, send_sems.at[j], recv_sems.at[j], (x, y, c)).wait_recv()
        for fw in passed:
            fw.wait_send()

    return pl.pallas_call(
        body, name=name, out_shape=jax.ShapeDtypeStruct(buf.shape, buf.dtype),
        in_specs=[ANY], out_specs=ANY, input_output_aliases={0: 0}, scratch_shapes=_sems(3),
    )(buf)


def _place_weights(ws, layer, kidx, after):
    steps = 4
    shapes, in_specs, out_specs = [], [], []
    for w, kind in zip(ws, BIG_KINDS):
        _, a, b = w.shape
        in_specs.append(pl.BlockSpec((None, a // steps, b), lambda i, k: (layer, i, 0)))
        if kind == "col":
            shapes.append((2, a, 2 * b))
            out_specs.append(pl.BlockSpec((None, a // steps, b), lambda i, k: (k[0] // 2, i, k[0] % 2)))
        else:
            shapes.append((N_CHIP, a, b))
            out_specs.append(pl.BlockSpec((None, a // steps, b), lambda i, k: (k[0], i, 0)))

    def body(k_ref, *refs):
        outs = refs[len(ws) + 1:]
        for t in range(len(ws)):
            outs[t][...] = refs[t][...].astype(BF16)

    return pl.pallas_call(
        body, name="place_weights", out_shape=[jax.ShapeDtypeStruct(s, BF16) for s in shapes],
        grid_spec=pltpu.PrefetchScalarGridSpec(num_scalar_prefetch=1, grid=(steps,), in_specs=in_specs + [ANY],
                                               out_specs=out_specs),
        compiler_params=_cp(dimension_semantics=("parallel",)),
    )(kidx, *ws, after)


HBM = pl.BlockSpec(memory_space=pltpu.HBM)
SEM = pl.BlockSpec(memory_space=pltpu.SEMAPHORE)
EFFECT = pltpu.SideEffectType.DATAFLOW_SIDE_EFFECTING


def _weight_block(ref, kind, k, h):
    if kind == "col":
        ncol = ref.shape[3] // 2
        return ref.at[k // 2, h, :, pl.ds(pl.multiple_of((k % 2) * ncol, LANES), ncol)]
    return ref.at[k, h]


def _in_hbm(a):
    return pltpu.with_memory_space_constraint(a, pltpu.HBM)


def _weight_send_start(placed, kinds, name):
    nt = len(placed)

    def body(*refs):
        send_sems, recv_sems = refs[nt], refs[nt + 1]
        dst = refs[nt + 2:2 * nt + 2]
        token = refs[2 * nt + 2]
        x, y, c, chips = _place()
        kme = 2 * x + y
        for t in range(nt):
            for j, chip in enumerate(chips):
                own = _weight_block(dst[t], kinds[t], kme, c)
                _remote(own, own, send_sems.at[3 * t + j], recv_sems.at[3 * t + j], (*chip, c)).start()
        token[...] = jnp.zeros_like(token)

    return pl.pallas_call(
        body, name=name,
        out_shape=(pltpu.SemaphoreType.DMA((3 * nt,)), pltpu.SemaphoreType.DMA((3 * nt,)),
                   *[pltpu.HBM(a.shape, a.dtype) for a in placed], jax.ShapeDtypeStruct((8, LANES), F32)),
        in_specs=[HBM] * nt, out_specs=(SEM, SEM, *[HBM] * nt, pl.BlockSpec(memory_space=pltpu.VMEM)),
        input_output_aliases={t: 2 + t for t in range(nt)},
        compiler_params=pltpu.CompilerParams(has_side_effects=EFFECT),
    )(*[_in_hbm(a) for a in placed])


def _weight_send_wait(send_sems, recv_sems, arrays, kinds, after, name):
    nt = len(arrays)

    def body(*refs):
        arr = refs[:nt]
        send_sems, recv_sems = refs[nt], refs[nt + 1]
        x, y, c, chips = _place()
        kme = 2 * x + y
        for t in range(nt):
            for j, chip in enumerate(chips):
                own = _weight_block(arr[t], kinds[t], kme, c)
                got = _weight_block(arr[t], kinds[t], 2 * chip[0] + chip[1], c)
                cp = _remote(own, got, send_sems.at[3 * t + j], recv_sems.at[3 * t + j], (*chip, c))
                cp.wait_send()
                cp.wait_recv()

    return pl.pallas_call(
        body, name=name, out_shape=[pltpu.HBM(a.shape, a.dtype) for a in arrays],
        in_specs=[HBM] * nt + [SEM, SEM, ANY], out_specs=[HBM] * nt,
        input_output_aliases={t: t for t in range(nt)},
        compiler_params=pltpu.CompilerParams(has_side_effects=EFFECT),
    )(*arrays, send_sems, recv_sems, after)


def _forward_copies(kinds):
    def make(refs, send_sems, recv_sems):
        x, y, c, chips = _place()
        cps = []
        for t in range(len(kinds)):
            for j, chip in enumerate(chips):
                blk = _weight_block(refs[t], kinds[t], 2 * chip[0] + chip[1], c)
                cps.append(_remote(blk, blk, send_sems.at[3 * t + j], recv_sems.at[3 * t + j], (x, y, 1 - c)))
        return cps
    return make


def _split_start(name, arrays, n_copies, make_copies):
    na = len(arrays)

    def body(*refs):
        send_sems, recv_sems = refs[na], refs[na + 1]
        for cp in make_copies(refs[na + 2:2 * na + 2], send_sems, recv_sems):
            cp.start()
        token = refs[2 * na + 2]
        token[...] = jnp.zeros_like(token)

    return pl.pallas_call(
        body, name=name,
        out_shape=(pltpu.SemaphoreType.DMA((n_copies,)), pltpu.SemaphoreType.DMA((n_copies,)),
                   *[pltpu.HBM(a.shape, a.dtype) for a in arrays], jax.ShapeDtypeStruct((8, LANES), F32)),
        in_specs=[HBM] * na, out_specs=(SEM, SEM, *[HBM] * na, pl.BlockSpec(memory_space=pltpu.VMEM)),
        input_output_aliases={t: 2 + t for t in range(na)},
        compiler_params=pltpu.CompilerParams(has_side_effects=EFFECT),
    )(*[_in_hbm(a) for a in arrays])


def _split_wait(name, started, make_copies, after):
    send_sems, recv_sems, *arrays, _ = started
    na = len(arrays)

    def body(*refs):
        send_sems, recv_sems = refs[na], refs[na + 1]
        for cp in make_copies(refs[:na], send_sems, recv_sems):
            cp.wait_send()
            cp.wait_recv()

    return pl.pallas_call(
        body, name=name, out_shape=[pltpu.HBM(a.shape, a.dtype) for a in arrays],
        in_specs=[HBM] * na + [SEM, SEM, ANY], out_specs=[HBM] * na,
        input_output_aliases={t: t for t in range(na)},
        compiler_params=pltpu.CompilerParams(has_side_effects=EFFECT),
    )(*arrays, send_sems, recv_sems, after)


def _exchange_copies(nt):
    def make(refs, send_sems, recv_sems):
        x, y, c, _ = _place()
        return [_remote(refs[t].at[:, 1 - c], refs[nt + t], send_sems.at[t], recv_sems.at[t], (x, y, 1 - c))
                for t in range(nt)]
    return make


def _sibling_exchange_start(views, name):
    lands = [lax.empty((v.shape[0],) + v.shape[2:], v.dtype) for v in views]
    return _split_start(name, list(views) + lands, len(views), _exchange_copies(len(views)))


def _sibling_exchange_wait(started, after, name):
    nt = (len(started) - 3) // 2
    outs = _split_wait(name, started, _exchange_copies(nt), after)
    return outs[:nt], outs[nt:]


def _scatter_copies(src, land, kinds, send_sems, recv_sems):
    x, y, c, chips = _place()
    cps = []
    for t in range(len(src)):
        for j, chip in enumerate(chips):
            k = 2 * chip[0] + chip[1]
            if kinds[t] == "col":
                ncol = land[t].shape[2]
                win = src[t].at[k // 2, :, pl.ds(pl.multiple_of((k % 2) * ncol, LANES), ncol)]
            else:
                win = src[t].at[k]
            cps.append(_remote(win, land[t].at[j], send_sems.at[3 * t + j], recv_sems.at[3 * t + j], (*chip, c)))
    return cps


def _chip_scatter_start(parts, kinds, name):
    nt = len(parts)
    shapes = []
    for p, kind in zip(parts, kinds):
        shapes.append((3, p.shape[1], p.shape[2] // 2) if kind == "col" else (3,) + p.shape[1:])

    def body(*refs):
        send_sems, recv_sems = refs[2 * nt], refs[2 * nt + 1]
        src, land = refs[2 * nt + 2:3 * nt + 2], refs[3 * nt + 2:4 * nt + 2]
        token = refs[4 * nt + 2]
        for cp in _scatter_copies(src, land, kinds, send_sems, recv_sems):
            cp.start()
        token[...] = jnp.zeros_like(token)

    lands = [lax.empty(s, BF16) for s in shapes]
    return pl.pallas_call(
        body, name=name,
        out_shape=(pltpu.SemaphoreType.DMA((3 * nt,)), pltpu.SemaphoreType.DMA((3 * nt,)),
                   *[pltpu.HBM(a.shape, a.dtype) for a in parts], *[pltpu.HBM(s, BF16) for s in shapes],
                   jax.ShapeDtypeStruct((8, LANES), F32)),
        in_specs=[HBM] * (2 * nt), out_specs=(SEM, SEM, *[HBM] * (2 * nt), pl.BlockSpec(memory_space=pltpu.VMEM)),
        input_output_aliases={t: 2 + t for t in range(2 * nt)},
        compiler_params=pltpu.CompilerParams(has_side_effects=EFFECT),
    )(*[_in_hbm(a) for a in parts], *[_in_hbm(a) for a in lands])


def _chip_scatter_wait(send_sems, recv_sems, parts, lands, kinds, after, name):
    nt = len(parts)

    def body(*refs):
        src, land = refs[:nt], refs[nt:2 * nt]
        send_sems, recv_sems = refs[2 * nt], refs[2 * nt + 1]
        for cp in _scatter_copies(src, land, kinds, send_sems, recv_sems):
            cp.wait_send()
            cp.wait_recv()

    outs = pl.pallas_call(
        body, name=name, out_shape=[pltpu.HBM(a.shape, a.dtype) for a in list(parts) + list(lands)],
        in_specs=[HBM] * (2 * nt) + [SEM, SEM, ANY], out_specs=[HBM] * (2 * nt),
        input_output_aliases={t: t for t in range(2 * nt)},
        compiler_params=pltpu.CompilerParams(has_side_effects=EFFECT),
    )(*parts, *lands, send_sems, recv_sems, after)
    return outs[:nt], outs[nt:]


def _share_copies(nt):
    def make(refs, send_sems, recv_sems):
        x, y, c, _ = _place()
        return [_remote(refs[t].at[c], refs[t].at[c], send_sems.at[t], recv_sems.at[t], (x, y, 1 - c))
                for t in range(nt)]
    return make


def _sibling_share_start(fulls, name):
    return _split_start(name, list(fulls), len(fulls), _share_copies(len(fulls)))


def _sibling_share_wait(started, after, name):
    return _split_wait(name, started, _share_copies(len(started) - 3), after)


SUM_STEPS = 4


def _pair_sum(views, lands, ck):
    nt = len(views)
    in_specs, out_specs, shapes = [], [], []
    for v in views:
        b, _, r, cc = v.shape
        per = SUM_STEPS // b
        tr = r // per
        in_specs.append(pl.BlockSpec((None, None, tr, cc), lambda i, s, per=per: (i // per, s[0], i % per, 0)))
        out_specs.append(pl.BlockSpec((None, tr, cc), lambda i, s, per=per: (i // per, i % per, 0)))
        shapes.append((b, r, cc))
    in_specs = in_specs + out_specs

    def body(s_ref, *refs):
        for t in range(nt):
            refs[2 * nt + t][...] = (refs[t][...].astype(F32) + refs[nt + t][...].astype(F32)).astype(BF16)

    return pl.pallas_call(
        body, name="grad_pair_sum", out_shape=[jax.ShapeDtypeStruct(s, BF16) for s in shapes],
        grid_spec=pltpu.PrefetchScalarGridSpec(num_scalar_prefetch=1, grid=(SUM_STEPS,), in_specs=in_specs,
                                               out_specs=out_specs),
        compiler_params=_cp(dimension_semantics=("parallel",)),
    )(ck, *views, *lands)


def _chip_sum(parts, lands, kinds, ck):
    nt = len(parts)
    steps = 2
    in_own, in_land, out_specs, shapes = [], [], [], []
    for ld, kind in zip(lands, kinds):
        _, r, cc = ld.shape
        tr = r // steps
        if kind == "col":
            in_own.append(pl.BlockSpec((None, tr, cc), lambda i, s: (s[1] // 2, i, s[1] % 2)))
        else:
            in_own.append(pl.BlockSpec((None, tr, cc), lambda i, s: (s[1], i, 0)))
        in_land.append(pl.BlockSpec((3, tr, cc), lambda i, s: (0, i, 0)))
        out_specs.append(pl.BlockSpec((None, tr, cc), lambda i, s: (s[0], i, 0)))
        shapes.append((2, r, cc))

    def body(s_ref, *refs):
        for t in range(nt):
            acc = refs[t][...].astype(F32)
            for j in range(3):
                acc = acc + refs[nt + t][j].astype(F32)
            refs[2 * nt + t][...] = acc

    return pl.pallas_call(
        body, name="grad_chip_sum", out_shape=[jax.ShapeDtypeStruct(s, F32) for s in shapes],
        grid_spec=pltpu.PrefetchScalarGridSpec(num_scalar_prefetch=1, grid=(steps,), in_specs=in_own + in_land,
                                               out_specs=out_specs),
        compiler_params=_cp(dimension_semantics=("parallel",)),
    )(ck, *parts, *lands)


def _sum8(g):
    _, r, cc = g.shape
    tr = _row_tile(r, N_DEV * cc)

    def body(g_ref, o_ref):
        acc = g_ref[0].astype(F32)
        for d in range(1, N_DEV):
            acc = acc + g_ref[d].astype(F32)
        o_ref[...] = acc

    return pl.pallas_call(
        body, name="small_grad_sum", grid=(r // tr,),
        in_specs=[pl.BlockSpec((N_DEV, tr, cc), lambda i: (0, i, 0))],
        out_specs=pl.BlockSpec((tr, cc), lambda i: (i, 0)),
        out_shape=jax.ShapeDtypeStruct((r, cc), F32),
        compiler_params=_cp(dimension_semantics=("parallel",)),
    )(g)


def _silu_rows(c):
    def body(c_ref, o_ref):
        v = c_ref[...]
        o_ref[...] = v * jax.nn.sigmoid(v)

    return pl.pallas_call(body, name="cond_silu", out_shape=jax.ShapeDtypeStruct(c.shape, F32))(c)


def _pack(arrays):
    rows = []
    for a in arrays:
        flat = a.reshape(-1)
        rows.append(jnp.pad(flat, (0, (-flat.shape[0]) % (8 * LANES))).reshape(-1, LANES))
    n = sum(r.shape[0] for r in rows)
    if n % 256:
        rows.append(jnp.zeros((256 - n % 256, LANES), rows[0].dtype))
    return jnp.concatenate(rows, axis=0)


def _unpack(packed, shapes):
    out, off = [], 0
    for s in shapes:
        n = math.prod(s)
        nr = 8 * -(-n // (8 * LANES))
        out.append(packed[off:off + nr].reshape(-1)[:n].reshape(s))
        off += nr
    return out


def _as_rows(a):
    return a.reshape(1, -1) if a.ndim == 1 else a.reshape(-1, a.shape[-1])


def _adamw_many(ws, gs, ms, vs, name, steps=1):
    nt = len(ws)

    def body(*refs):
        for t in range(nt):
            w_ref, g_ref, m_ref, v_ref = (refs[k * nt + t] for k in range(4))
            d, m, v = _adamw_math(w_ref[...], g_ref[...], m_ref[...], v_ref[...])
            refs[4 * nt + t][...] = d
            refs[5 * nt + t][...] = m
            refs[6 * nt + t][...] = v

    shapes = [jax.ShapeDtypeStruct(a.shape, F32) for a in ws]
    if steps == 1:
        outs = pl.pallas_call(body, name=name, out_shape=shapes * 3, compiler_params=_cp())(*ws, *gs, *ms, *vs)
    else:
        specs = [pl.BlockSpec((a.shape[0] // steps, a.shape[1]), lambda i: (i, 0)) for a in ws]
        outs = pl.pallas_call(
            body, name=name, grid=(steps,), in_specs=specs * 4, out_specs=specs * 3, out_shape=shapes * 3,
            compiler_params=_cp(dimension_semantics=("parallel",)),
        )(*ws, *gs, *ms, *vs)
    return outs[:nt], outs[nt:2 * nt], outs[2 * nt:]


def _exchange_big_grads(grads, kinds, layer):
    views = []
    for g, kind in zip(grads, kinds):
        if kind == "col":
            views.append(g.reshape(2, 2, g.shape[1] // 2, g.shape[2]))
        else:
            views.append(g.reshape(N_CHIP, 2, g.shape[0] // (2 * N_CHIP), g.shape[1]))
    return _sibling_exchange_start(views, "grad_exchange_start_%d" % layer)


def _scatter_big_grads(exchanged, kinds, ck, after, layer):
    views, lands = _sibling_exchange_wait(exchanged, after, "grad_exchange_wait_%d" % layer)
    parts = _pair_sum(views, lands, ck)
    return _chip_scatter_start(parts, kinds, "grad_scatter_start_%d" % layer)


def _finish_big_grads(started, kinds, ck, after, layer):
    nt = len(kinds)
    send_sems, recv_sems = started[0], started[1]
    parts, lands = started[2:2 + nt], started[2 + nt:2 + 2 * nt]
    parts, lands = _chip_scatter_wait(send_sems, recv_sems, parts, lands, kinds, after, "grad_scatter_wait_%d" % layer)
    return _sibling_share_start(_chip_sum(parts, lands, kinds, ck), "grad_share_start_%d" % layer)


def _adamw_layer(ws, gs, ms, vs, stacks, layer, name, steps):
    nt = len(ws)
    stacks = [s if s is not None else tuple(lax.empty(w.shape, F32) for _ in range(4)) for s, w in zip(stacks, ws)]

    def body(*refs):
        for t in range(nt):
            w_ref, g_ref, m_ref, v_ref = (refs[k * nt + t] for k in range(4))
            outs = refs[8 * nt + 4 * t:8 * nt + 4 * t + 4]
            g = g_ref[...]
            outs[0][...] = g
            outs[1][...], outs[2][...], outs[3][...] = _adamw_math(w_ref[...], g, m_ref[...], v_ref[...])

    in_specs, g_specs, out_specs = [], [], []
    for w in ws:
        _, r, c = w.shape
        in_specs.append(pl.BlockSpec((None, r // steps, c), lambda i: (layer, i, 0)))
        g_specs.append(pl.BlockSpec((r // steps, c), lambda i: (i, 0)))
        out_specs += [pl.BlockSpec((None, r // steps, c), lambda i: (layer, i, 0))] * 4
    in_specs = in_specs + g_specs + in_specs * 2 + [ANY] * (4 * nt)
    flat = [a for s in stacks for a in s]
    outs = pl.pallas_call(
        body, name=name, grid=(steps,), in_specs=in_specs, out_specs=out_specs,
        out_shape=[jax.ShapeDtypeStruct(a.shape, F32) for a in flat],
        input_output_aliases={4 * nt + k: k for k in range(4 * nt)},
        compiler_params=_cp(dimension_semantics=("parallel",)),
    )(*ws, *gs, *ms, *vs, *flat)
    return [tuple(outs[4 * t:4 * t + 4]) for t in range(nt)]


SMALL_NAMES = ["ada_b", "norm1_g", "norm2_g", "sgu_w", "sgu_b", "pool_w", "pool_scale", "conv_w", "s5_lambda_re",
               "s5_lambda_im", "s5_b_re", "s5_b_im", "s5_c_re", "s5_c_im", "s5_d", "s5_log_dt", "s5_glu_w", "s5_glu_b",
               "mix_norm_g", "norm3_g", "final_norm_g"]
BIG_NAMES = ["ffn1_w_in", "ffn1_w_out", "w_mix_in", "w_mix_out", "ffn2_w_in", "ffn2_w_out"]
BIG_KINDS = ["col", "row", "row", "row", "col", "row"]
WEIGHT_ORDER = ["ada_w", "ada_b", "norm1_g", "ffn1_w_in", "ffn1_w_out", "norm2_g", "w_mix_in", "sgu_w", "sgu_b", "pool_w",
                "pool_scale", "conv_w", "s5_lambda_re", "s5_lambda_im", "s5_b_re", "s5_b_im", "s5_c_re", "s5_c_im", "s5_d",
                "s5_log_dt", "s5_glu_w", "s5_glu_b", "mix_norm_g", "w_mix_out", "norm3_g", "ffn2_w_in", "ffn2_w_out",
                "final_norm_g"]


def _local_step(x, target, cond, fetch_weights, prefetch_weights, p, emit_grads):
    nl, d = DEPTH, x.shape[1]
    row = lambda a: a.reshape(1, -1)
    saved = []
    for l in range(nl):
        (wi1, wo1, wmit, wmo, wi2, wo2), tok = fetch_weights(l, x)
        cl = cond[l] + tok
        mod1, mod2, mod3 = cl[0:3], cl[3:6], cl[6:9]
        lre, lim = p["s5_lambda_re"][l].reshape(-1), p["s5_lambda_im"][l].reshape(-1)
        ldt = jnp.repeat(p["s5_log_dt"][l], 64)
        rowp = jnp.stack([lre, lim, ldt])
        sp = (rowp, rowp.T, p["s5_b_re"][l].reshape(N_STATE, 16), p["s5_b_im"][l].reshape(N_STATE, 16),
              p["s5_c_re"][l].reshape(W_GRP, 64), p["s5_c_im"][l].reshape(W_GRP, 64), row(p["s5_d"][l]))
        glu = (p["s5_glu_w"][l], row(p["s5_glu_b"][l]))
        bias_full = jnp.repeat(p["sgu_b"][l].T, 64, axis=1)
        pw2 = p["pool_w"][l].reshape(W_GRP, 64)
        x1, h1, a1, b1, o1 = _ffn_fwd(x, mod1, row(p["norm1_g"][l]), wi1, wo1)
        z, h2 = _mix_in_fwd(x1, mod2, row(p["norm2_g"][l]), wmit)
        ya = _sgu_fwd(z, p["sgu_w"][l], bias_full)
        yb, yc = _poolconv_fwd(z, pw2, row(p["pool_scale"][l]), p["conv_w"][l])
        ys = (ya, yb, yc, _s5_core_fwd(z, sp))
        x2, m = _mix_out_fwd(ys, glu, row(p["mix_norm_g"][l]), wmo, x1, mod2[2:3])
        mod3 = mod3 + prefetch_weights(l + 1, x2)
        x3, h3, a3, b3, o3 = _ffn_fwd(x2, mod3, row(p["norm3_g"][l]), wi2, wo2)
        saved.append((x, x1, x2, h1, a1, b1, o1, z, h2, ys, m, h3, a3, b3, o3, sp, bias_full, pw2, glu,
                      (wi1, wo1, wmit, wmo, wi2, wo2), cl))
        x = x3

    loss, dx, dfg = _loss_head(x, row(p["final_norm_g"]), target)

    sg = {n: [None] * nl for n in SMALL_NAMES if n not in ("ada_b", "final_norm_g")}
    dcond = [None] * nl
    s5_da, s5_dk = [None] * nl, [None] * nl
    tok = 0.0
    for l in reversed(range(nl)):
        (x0, x1, x2, h1, a1, b1, o1, z, h2, ys, m, h3, a3, b3, o3, sp, bias_full, pw2, glu,
         (wi1, wo1, wmit, wmo, wi2, wo2), cl) = saved[l]
        cl = cl + tok
        mod1, mod2, mod3 = cl[0:3], cl[3:6], cl[6:9]
        dza, dzb, dwi2, dwo2, dgate3 = _ffn_bwd_main(dx, o3, mod3[2:3], h3, a3, b3, wo2)
        dx, rows3 = _ffn_bwd_in(dza, dzb, wi2, x2, dx, mod3, row(p["norm3_g"][l]))
        outs = _mix_out_bwd(dx, m, mod2[2:3], ys, glu, row(p["mix_norm_g"][l]), wmo)
        dys, dgate2, dmng, dwmo, dgw, dgb = outs[0:4], outs[4], outs[5], outs[6], outs[7], outs[8]
        dza_, dsw, dsb = _sgu_bwd(z, dys[0], p["sgu_w"][l], bias_full)
        dzb_, dzc_, dwbd, dps, dcw = _poolconv_bwd(z, dys[1], dys[2], pw2, row(p["pool_scale"][l]), p["conv_w"][l])
        dzd_, dbr, dbi, dcr, dci, dd, da, dk = _s5_core_bwd(z, dys[3], sp)
        dx, rows2, dwmit = _mix_in_bwd((dza_, dzb_, dzc_, dzd_), h2, wmit, x1, dx, mod2, row(p["norm2_g"][l]))
        dza, dzb, dwi1, dwo1, dgate1 = _ffn_bwd_main(dx, o1, mod1[2:3], h1, a1, b1, wo1)
        tok, layer_done = emit_grads(l, [dwi1, dwo1, dwmit, dwmo, dwi2, dwo2])
        dx, rows1 = _ffn_bwd_in(dza, dzb, wi1, x0, dx, mod1 + tok, row(p["norm1_g"][l]))
        if l > 0:
            tok = layer_done(dx)[0, 0]
        dcond[l] = jnp.concatenate([rows1[0:2], dgate1, rows2[0:2], dgate2, rows3[0:2], dgate3], axis=0)
        sg["norm1_g"][l], sg["norm2_g"][l], sg["norm3_g"][l] = rows1[2], rows2[2], rows3[2]
        sg["mix_norm_g"][l] = dmng[0]
        sg["sgu_w"][l] = dsw
        sg["sgu_b"][l] = dsb[:, 0:4].T
        g4 = dwbd.reshape(4, 64, 4, 64)
        sg["pool_w"][l] = jnp.stack([g4[k, :, k, :] for k in range(4)])
        sg["pool_scale"][l] = dps[0]
        sg["conv_w"][l] = dcw[0:3]
        sg["s5_b_re"][l], sg["s5_b_im"][l] = dbr.reshape(16, 64, 16), dbi.reshape(16, 64, 16)
        sg["s5_c_re"][l], sg["s5_c_im"][l] = dcr.reshape(16, 16, 64), dci.reshape(16, 16, 64)
        sg["s5_d"][l] = dd[0]
        sg["s5_glu_w"][l], sg["s5_glu_b"][l] = dgw, dgb[0]
        s5_da[l], s5_dk[l] = da, dk

    n16 = nl * 16
    dlre, dlim, dldt = _s5_param_bwd(
        p["s5_lambda_re"].reshape(n16, 64), p["s5_lambda_im"].reshape(n16, 64),
        jnp.repeat(p["s5_log_dt"].reshape(n16, 1), 64, axis=1),
        jnp.stack([a[0] for a in s5_da]).reshape(n16, 64), jnp.stack([a[1] for a in s5_da]).reshape(n16, 64),
        jnp.stack([k[:, 0] for k in s5_dk]).reshape(n16, 64), jnp.stack([k[:, 1] for k in s5_dk]).reshape(n16, 64))
    small = {n: jnp.stack(v) for n, v in sg.items() if v[0] is not None}
    small["s5_lambda_re"] = dlre.reshape(nl, 16, 64)
    small["s5_lambda_im"] = dlim.reshape(nl, 16, 64)
    small["s5_log_dt"] = dldt.reshape(nl, 16)
    small["final_norm_g"] = dfg[0]
    return loss, dx, small, jnp.stack(dcond), layer_done


def kernel(x, c, ada_w, ada_b, norm1_g, ffn1_w_in, ffn1_w_out, norm2_g, w_mix_in, sgu_w, sgu_b, pool_w, pool_scale, conv_w, s5_lambda_re, s5_lambda_im, s5_b_re, s5_b_im, s5_c_re, s5_c_im, s5_d, s5_log_dt, s5_glu_w, s5_glu_b, mix_norm_g, w_mix_out, norm3_g, ffn2_w_in, ffn2_w_out, final_norm_g, loss_target, m_ada_w, m_ada_b, m_norm1_g, m_ffn1_w_in, m_ffn1_w_out, m_norm2_g, m_w_mix_in, m_sgu_w, m_sgu_b, m_pool_w, m_pool_scale, m_conv_w, m_s5_lambda_re, m_s5_lambda_im, m_s5_b_re, m_s5_b_im, m_s5_c_re, m_s5_c_im, m_s5_d, m_s5_log_dt, m_s5_glu_w, m_s5_glu_b, m_mix_norm_g, m_w_mix_out, m_norm3_g, m_ffn2_w_in, m_ffn2_w_out, m_final_norm_g, v_ada_w, v_ada_b, v_norm1_g, v_ffn1_w_in, v_ffn1_w_out, v_norm2_g, v_w_mix_in, v_sgu_w, v_sgu_b, v_pool_w, v_pool_scale, v_conv_w, v_s5_lambda_re, v_s5_lambda_im, v_s5_b_re, v_s5_b_im, v_s5_c_re, v_s5_c_im, v_s5_d, v_s5_log_dt, v_s5_glu_w, v_s5_glu_b, v_mix_norm_g, v_w_mix_out, v_norm3_g, v_ffn2_w_in, v_ffn2_w_out, v_final_norm_g):
    args = dict(locals())
    w = {n: args[n] for n in WEIGHT_ORDER}
    mom = {n: args["m_" + n] for n in WEIGHT_ORDER}
    vel = {n: args["v_" + n] for n in WEIGHT_ORDER}
    nl, d = DEPTH, x.shape[-1]
    s = x.shape[1]
    px, py, pc = lax.axis_index("x"), lax.axis_index("y"), lax.axis_index("c")
    kme = 2 * px + py
    me = 2 * kme + pc
    kidx = jnp.reshape(kme, (1,)).astype(jnp.int32)

    shards = [ffn1_w_in, ffn1_w_out, jnp.swapaxes(w_mix_in, 1, 2), w_mix_out, ffn2_w_in, ffn2_w_out]
    started_weights = {}

    def start_weights(l, after):
        placed = _place_weights(shards, l, kidx, after)
        views = [a.reshape(a.shape[0], 2, a.shape[1] // 2, a.shape[2]) for a in placed]
        *handles, token = _weight_send_start(views, BIG_KINDS, "weight_send_start_%d" % l)
        started_weights[l] = handles
        return token

    cact = _silu_rows(c)
    pre = _pack([cact, conv_w, s5_glu_w])
    pre_all = _all_gather8(pre, "gather_prelude")
    token = start_weights(0, pre_all)
    parts = [_unpack(pre_all[dev], [cact.shape, conv_w.shape, s5_glu_w.shape]) for dev in range(N_DEV)]
    cact_all = pre_all[:, :d // LANES, :].reshape(N_DEV, d)
    conv_full = jnp.concatenate([parts[2 * k][1] for k in range(N_CHIP)], axis=2)
    glu_full = jnp.concatenate([parts[2 * k][2] for k in range(N_CHIP)], axis=1)

    n_ada = ada_w.shape[2]
    ada_b_loc = lax.dynamic_slice_in_dim(ada_b, kme * n_ada, n_ada, axis=1).reshape(nl, 1, n_ada) + token[0, 0]
    cond_part = _cond_fwd(cact_all, ada_w, ada_b_loc)
    cond_mine = lax.dynamic_update_slice(lax.empty((N_DEV, nl * N_DEV, n_ada), F32),
                                         cond_part.reshape(1, nl * N_DEV, n_ada), (me, 0, 0))
    cond_gathering = _split_start("cond_send_start", [cond_mine], 4, _gather_first_copies())
    token = cond_gathering[-1]
    for l in range(1, nl):
        token = start_weights(l, token)
    cond_arrived, = _split_wait("cond_send_wait", cond_gathering, _gather_first_copies(), token)
    cond_all = _gather_pass_on(cond_arrived, "cond_pass_on").reshape(N_DEV, nl, N_DEV, n_ada)
    cond_me = jnp.concatenate(
        [lax.dynamic_index_in_dim(cond_all[2 * k], me, axis=1, keepdims=False) for k in range(N_CHIP)], axis=1)
    cond = cond_me.reshape(nl, 9, d)

    forwarding = {}

    def prefetch_weights(l, after):
        if l >= nl:
            return 0.0
        send_sems, recv_sems, *views = started_weights.pop(l)
        views = _weight_send_wait(send_sems, recv_sems, views, BIG_KINDS, after, "weight_send_wait_%d" % l)
        forwarding[l] = _split_start("weight_forward_start_%d" % l, views, 3 * len(views), _forward_copies(BIG_KINDS))
        return forwarding[l][-1][0, 0]

    def fetch_weights(l, after):
        if l not in forwarding:
            prefetch_weights(l, after)
        views = _split_wait("weight_forward_wait_%d" % l, forwarding.pop(l), _forward_copies(BIG_KINDS), after)
        full = [v.reshape(2, 2 * v.shape[2], v.shape[3]) if kind == "col" else v.reshape(-1, v.shape[3])
                for v, kind in zip(views, BIG_KINDS)]
        return full, 0.0

    ck = jnp.stack([pc, kme]).astype(jnp.int32)
    scattering, sharing = [], []
    stacks = {n: None for n in BIG_NAMES}
    groups = ((["ffn1_w_in", "ffn2_w_in"], 16, "adamw_w_in"),
              (["ffn1_w_out", "w_mix_in", "w_mix_out", "ffn2_w_out"], 8, "adamw_w_out"))

    def as_reduced(t):
        return {n: jnp.swapaxes(t[n], 1, 2) if n == "w_mix_in" else t[n] for n in BIG_NAMES}

    w_r, m_r, v_r = as_reduced(w), as_reduced(mom), as_reduced(vel)

    def apply_adamw(l, fulls):
        g = {n: f.reshape(2 * f.shape[1], f.shape[2]) for n, f in zip(BIG_NAMES, fulls)}
        for names, steps, call in groups:
            outs = _adamw_layer([w_r[n] for n in names], [g[n] for n in names], [m_r[n] for n in names],
                                [v_r[n] for n in names], [stacks[n] for n in names], l, call, steps)
            stacks.update(zip(names, outs))

    def retire_share(after):
        l2, shared = sharing.pop(0)
        apply_adamw(l2, _sibling_share_wait(shared, after, "grad_share_wait_%d" % l2))

    def retire_scatter(after):
        l1, scattered = scattering.pop(0)
        sharing.append((l1, _finish_big_grads(scattered, BIG_KINDS, ck, after, l1)))

    def retire(after):
        if sharing:
            retire_share(after)
        if scattering:
            retire_scatter(after)

    def emit_grads(l, grads_l):
        exchanged = _exchange_big_grads(grads_l, BIG_KINDS, l)

        def layer_done(after):
            started = _scatter_big_grads(exchanged, BIG_KINDS, ck, after, l)
            retire(after)
            scattering.append((l, started))
            return started[-1]

        return exchanged[-1][0, 0], layer_done

    p = {n: w[n] for n in SMALL_NAMES}
    p["conv_w"], p["s5_glu_w"] = conv_full, glu_full
    loss, dx, small, dcond, first_layer_done = _local_step(x[0], loss_target[0], cond, fetch_weights, prefetch_weights,
                                                           p, emit_grads)

    small_order = [n for n in SMALL_NAMES if n != "ada_b"]
    packed = _pack([dcond] + [small[n] for n in small_order]).astype(BF16)
    mine = lax.dynamic_update_slice(lax.empty((N_DEV,) + packed.shape, BF16), packed[None], (me, 0, 0))
    gathering = _split_start("small_grads_send_start", [mine], 4, _gather_first_copies())
    scatter_token = first_layer_done(gathering[-1])
    while sharing:
        retire_share(scatter_token)
    arrived, = _split_wait("small_grads_send_wait", gathering, _gather_first_copies(), stacks[BIG_NAMES[0]][0])
    gathered_small = _gather_pass_on(arrived, "small_grads_pass_on")
    total = _sum8(gathered_small)
    shapes = [dcond.shape] + [small[n].shape for n in small_order]
    tot = dict(zip(["ada_b"] + small_order, _unpack(total, shapes)))
    grads = {n: tot[n] for n in SMALL_NAMES}
    grads["ada_b"] = tot["ada_b"].reshape(nl, 9 * d)
    grads["conv_w"] = lax.dynamic_slice_in_dim(tot["conv_w"], kme * conv_w.shape[2], conv_w.shape[2], axis=2)
    grads["s5_glu_w"] = lax.dynamic_slice_in_dim(tot["s5_glu_w"], kme * s5_glu_w.shape[1], s5_glu_w.shape[1], axis=1)

    dcond_all = gathered_small.reshape(N_DEV, -1)[:, :dcond.size].reshape(N_DEV, nl, 9 * d)
    dcond_loc = jnp.swapaxes(lax.dynamic_slice_in_dim(dcond_all, kme * n_ada, n_ada, axis=2), 0, 1)
    g_ada, d_ada, m_ada, v_ada = _ada_w_update(cact_all, dcond_loc, ada_w, m_ada_w, v_ada_w)

    while scattering or sharing:
        retire(g_ada)
    delta, new_m, new_v = {}, {}, {}
    for n in BIG_NAMES:
        grads[n], delta[n], new_m[n], new_v[n] = (jnp.swapaxes(a, 1, 2) if n == "w_mix_in" else a for a in stacks[n])

    grads["ada_w"], delta["ada_w"], new_m["ada_w"], new_v["ada_w"] = g_ada, d_ada, m_ada, v_ada
    wide = ("s5_b_re", "s5_b_im")
    for names, call, steps in (([n for n in SMALL_NAMES if n not in wide], "adamw_small", 1),
                               (list(wide), "adamw_s5_b", DEPTH)):
        outs = _adamw_many(*[[_as_rows(t[n]) for n in names] for t in (w, grads, mom, vel)], call, steps)
        for res, o in zip((delta, new_m, new_v), outs):
            res.update({n: a.reshape(w[n].shape) for n, a in zip(names, o)})

    loss_local, _ = lax.optimization_barrier((loss[0, 0], new_v["ffn1_w_in"]))
    loss_total = lax.psum(loss_local, ("x", "y", "c"))
    return (loss_total, dx[None], *[grads[n] for n in WEIGHT_ORDER], *[delta[n] for n in WEIGHT_ORDER],
            *[new_m[n] for n in WEIGHT_ORDER], *[new_v[n] for n in WEIGHT_ORDER])
```
